```python
import jax
import jax.numpy as jnp
from jax import lax
import numpy as np


D_MODEL = 2048
BATCH = 8
SEQ = 2048
DEPTH = 2

RMS_EPS = 1e-6
PLE_DIM = 256
SB_HEADS = 8
SB_HEAD_DIM = 128
SB_BLOCK = 128
SB_WIDTH = SB_HEADS * SB_HEAD_DIM
HG_HEADS = 8
HG_KEY_DIM = 128
HG_VAL_DIM = 128
HG_CHUNK = 64
HG_KEY_WIDTH = HG_HEADS * HG_KEY_DIM
HG_VAL_WIDTH = HG_HEADS * HG_VAL_DIM
EVEN_SPLITS = [SB_WIDTH, SB_WIDTH, SB_WIDTH, HG_KEY_WIDTH, HG_KEY_WIDTH, HG_VAL_WIDTH, HG_VAL_WIDTH]
EVEN_IN_WIDTH = sum(EVEN_SPLITS)
EVEN_OUT_WIDTH = SB_WIDTH + HG_VAL_WIDTH
LRU_WIDTH = ((4 * D_MODEL // 3 + 255) // 256) * 256
RG_BLOCK_WIDTH = 256
RG_BLOCKS = LRU_WIDTH // RG_BLOCK_WIDTH
CONV_WIDTH = 4
RG_C = 8.0
D_FF = ((8 * D_MODEL // 3 + 255) // 256) * 256
N_EVEN = (DEPTH + 1) // 2
N_ODD = DEPTH // 2

kernel_name = "hybrid_stickbreak_hgrn2_rglru_block"


def rmsnorm(x, g):
    xf = x.astype(jnp.float32)
    y = xf * lax.rsqrt(jnp.mean(xf * xf, axis=-1, keepdims=True) + RMS_EPS)
    return (y * g.astype(jnp.float32)).astype(x.dtype)


def split_heads(a, n_heads, head_dim):
    b, s, _ = a.shape
    return a.reshape(b, s, n_heads, head_dim).transpose(0, 2, 1, 3)


def merge_heads(a):
    b, h, s, d = a.shape
    return a.transpose(0, 2, 1, 3).reshape(b, s, h * d)


def stick_breaking_attention(q, k, v):
    seq = q.shape[2]
    qf = q.astype(jnp.float32) * (SB_HEAD_DIM ** -0.5)
    kf = k.astype(jnp.float32)
    vf = v.astype(jnp.float32)
    outs = []
    for blk in range(seq // SB_BLOCK):
        q0 = blk * SB_BLOCK
        q1 = q0 + SB_BLOCK
        z = jnp.einsum('bhtd,bhsd->bhts', qf[:, :, q0:q1], kf[:, :, :q1])
        mask = jnp.arange(q1)[None, :] < (q0 + jnp.arange(SB_BLOCK))[:, None]
        log_beta = jax.nn.log_sigmoid(z)
        log_one_minus = jnp.where(mask, log_beta - z, 0.0)
        log_remain = lax.cumsum(log_one_minus, axis=3, reverse=True) - log_one_minus
        w = jnp.where(mask, jnp.exp(log_beta + log_remain), 0.0)
        outs.append(jnp.einsum('bhts,bhsd->bhtd', w, vf[:, :, :q1]))
    return jnp.concatenate(outs, axis=2).astype(q.dtype)


def hgrn2_chunkwise(q, k, v, log_f):
    b, h, s, dk = q.shape
    dv = v.shape[-1]
    n_chunks = s // HG_CHUNK

    def to_chunks(a):
        return a.reshape(b, h, n_chunks, HG_CHUNK, a.shape[-1]).transpose(2, 0, 1, 3, 4)

    causal = (jnp.arange(HG_CHUNK)[:, None] >= jnp.arange(HG_CHUNK)[None, :])[:, :, None]

    def step(state, inp):
        qi, ki, vi, gi = inp
        cum = jnp.cumsum(gi, axis=2)
        o_inter = jnp.einsum('bhtk,bhkv->bhtv', qi * jnp.exp(cum), state)
        diff = cum[:, :, :, None, :] - cum[:, :, None, :, :]
        decay = jnp.exp(jnp.where(causal, diff, -jnp.inf))
        scores = jnp.einsum('bhtk,bhsk,bhtsk->bhts', qi, ki, decay)
        o_intra = jnp.einsum('bhts,bhsv->bhtv', scores, vi)
        last = cum[:, :, -1:, :]
        k_dec = ki * jnp.exp(last - cum)
        new_state = state * jnp.exp(last[:, :, 0, :, None]) + jnp.einsum('bhsk,bhsv->bhkv', k_dec, vi)
        return new_state, o_inter + o_intra

    state0 = jnp.zeros((b, h, dk, dv), jnp.float32)
    _, o = lax.scan(step, state0, (to_chunks(q), to_chunks(k), to_chunks(v), to_chunks(log_f)))
    return o.transpose(1, 2, 0, 3, 4).reshape(b, h, s, dv)


def hgrn2(hq, hf, hi, hg, lb, norm_g):
    lbf = lb.astype(jnp.float32)
    f = lbf + (1.0 - lbf) * jax.nn.sigmoid(hf.astype(jnp.float32))
    q = jax.nn.silu(hq.astype(jnp.float32))
    o = hgrn2_chunkwise(split_heads(q, HG_HEADS, HG_KEY_DIM),
                        split_heads(1.0 - f, HG_HEADS, HG_KEY_DIM),
                        split_heads(hi.astype(jnp.float32), HG_HEADS, HG_VAL_DIM),
                        split_heads(jnp.log(f), HG_HEADS, HG_KEY_DIM))
    o = rmsnorm(o, norm_g)
    return (merge_heads(o) * jax.nn.silu(hg.astype(jnp.float32))).astype(hq.dtype)


def even_mixer(h, w_in, w_out, lb, hg_norm_g):
    sq, sk, sv, hq, hf, hi, hg = jnp.split(h @ w_in, np.cumsum(EVEN_SPLITS)[:-1].tolist(), axis=-1)
    a_out = merge_heads(stick_breaking_attention(split_heads(sq, SB_HEADS, SB_HEAD_DIM),
                                                 split_heads(sk, SB_HEADS, SB_HEAD_DIM),
                                                 split_heads(sv, SB_HEADS, SB_HEAD_DIM)))
    b_out = hgrn2(hq, hf, hi, hg, lb, hg_norm_g)
    return jnp.concatenate([a_out.astype(h.dtype), b_out.astype(h.dtype)], axis=-1) @ w_out


def causal_depthwise_conv(x, w, b):
    s = x.shape[1]
    xp = jnp.pad(x, ((0, 0), (CONV_WIDTH - 1, 0), (0, 0)))
    y = b
    for tap in range(CONV_WIDTH):
        y = y + xp[:, CONV_WIDTH - 1 - tap:CONV_WIDTH - 1 - tap + s] * w[tap]
    return y


def block_diag_linear(x, w, b):
    bsz, s, _ = x.shape
    xb = x.reshape(bsz, s, RG_BLOCKS, RG_BLOCK_WIDTH)
    return (jnp.einsum('bsni,nio->bsno', xb, w) + b).reshape(bsz, s, LRU_WIDTH)


def rg_lru(x, wa, ba, wx, bx, lam):
    s = x.shape[1]
    r = jax.nn.sigmoid(block_diag_linear(x, wa, ba).astype(jnp.float32))
    i = jax.nn.sigmoid(block_diag_linear(x, wx, bx).astype(jnp.float32))
    log_a = -RG_C * r * jax.nn.softplus(-lam.astype(jnp.float32))
    a = jnp.exp(log_a)
    mult = jnp.sqrt(-jnp.expm1(2.0 * log_a))
    mult = jnp.where((jnp.arange(s) == 0)[None, :, None], 1.0, mult)
    u = x.astype(jnp.float32) * i * mult

    def combine(left, right):
        a1, b1 = left
        a2, b2 = right
        return a1 * a2, a2 * b1 + b2

    _, hs = lax.associative_scan(combine, (a, u), axis=1)
    return hs.astype(x.dtype)


def odd_mixer(h, w_in, conv_w, conv_b, wa, ba, wx, bx, lam, w_out):
    gate_branch, x_branch = jnp.split(h @ w_in, 2, axis=-1)
    y = rg_lru(causal_depthwise_conv(x_branch, conv_w, conv_b), wa, ba, wx, bx, lam)
    return (jax.nn.gelu(gate_branch) * y) @ w_out


def swiglu(h, w_gate_up, w_down):
    g, u = jnp.split(h @ w_gate_up, 2, axis=-1)
    return (jax.nn.silu(g) * u) @ w_down


def per_layer_embedding(h, p_i, w_up, w_gate, g):
    e = p_i @ w_up
    gate = jax.nn.sigmoid(h @ w_gate)
    return rmsnorm(gate * e, g)


def _fwd_setup_inputs(seed: int = 0) -> dict:
    key = jax.random.key(seed)
    ks = jax.random.split(key, 24)

    def nrm(i, shape, scale):
        return scale * jax.random.normal(ks[i], shape, jnp.float32)

    u = jax.random.uniform(ks[18], (N_ODD, LRU_WIDTH), jnp.float32, 0.9, 0.999)
    a0 = u ** (1.0 / RG_C)
    rg_lambda = jnp.log(a0) - jnp.log1p(-a0)
    return {
        'x': nrm(0, (BATCH, SEQ, D_MODEL), 1.0),
        'p': nrm(1, (DEPTH, BATCH, SEQ, PLE_DIM), 1.0),
        'mix_pre_g': 1.0 + nrm(2, (DEPTH, D_MODEL), 0.05),
        'mix_post_g': 1.0 + nrm(3, (DEPTH, D_MODEL), 0.05),
        'ffn_pre_g': 1.0 + nrm(4, (DEPTH, D_MODEL), 0.05),
        'ffn_post_g': 1.0 + nrm(5, (DEPTH, D_MODEL), 0.05),
        'ple_norm_g': 1.0 + nrm(6, (DEPTH, D_MODEL), 0.05),
        'w_in_even': nrm(7, (N_EVEN, D_MODEL, EVEN_IN_WIDTH), D_MODEL ** -0.5),
        'w_out_even': nrm(8, (N_EVEN, EVEN_OUT_WIDTH, D_MODEL), EVEN_OUT_WIDTH ** -0.5),
        'hg_lb_logits': nrm(9, (N_EVEN + 1, HG_KEY_WIDTH), 0.5),
        'hg_norm_g': 1.0 + nrm(10, (N_EVEN, HG_VAL_DIM), 0.05),
        'w_in_odd': nrm(11, (N_ODD, D_MODEL, 2 * LRU_WIDTH), D_MODEL ** -0.5),
        'conv_w': nrm(12, (N_ODD, CONV_WIDTH, LRU_WIDTH), CONV_WIDTH ** -0.5),
        'conv_b': nrm(13, (N_ODD, LRU_WIDTH), 0.02),
        'rg_wa': nrm(14, (N_ODD, RG_BLOCKS, RG_BLOCK_WIDTH, RG_BLOCK_WIDTH), RG_BLOCK_WIDTH ** -0.5),
        'rg_ba': nrm(15, (N_ODD, RG_BLOCKS, RG_BLOCK_WIDTH), 0.02),
        'rg_wx': nrm(16, (N_ODD, RG_BLOCKS, RG_BLOCK_WIDTH, RG_BLOCK_WIDTH), RG_BLOCK_WIDTH ** -0.5),
        'rg_bx': nrm(17, (N_ODD, RG_BLOCKS, RG_BLOCK_WIDTH), 0.02),
        'rg_lambda': rg_lambda,
        'w_out_odd': nrm(19, (N_ODD, LRU_WIDTH, D_MODEL), LRU_WIDTH ** -0.5),
        'w_gate_up': nrm(20, (DEPTH, D_MODEL, 2 * D_FF), D_MODEL ** -0.5),
        'w_down': nrm(21, (DEPTH, D_FF, D_MODEL), D_FF ** -0.5),
        'w_ple_up': nrm(22, (DEPTH, PLE_DIM, D_MODEL), PLE_DIM ** -0.5),
        'w_ple_gate': nrm(23, (DEPTH, D_MODEL, D_MODEL), D_MODEL ** -0.5),
    }


def _fwd_reference(x, p, mix_pre_g, mix_post_g, ffn_pre_g, ffn_post_g, ple_norm_g,
              w_in_even, w_out_even, hg_lb_logits, hg_norm_g,
              w_in_odd, conv_w, conv_b, rg_wa, rg_ba, rg_wx, rg_bx, rg_lambda, w_out_odd,
              w_gate_up, w_down, w_ple_up, w_ple_gate):
    lb_all = jnp.cumsum(jax.nn.softmax(hg_lb_logits.astype(jnp.float32), axis=0), axis=0)
    h = x
    for i in range(DEPTH):
        j = i // 2
        n = rmsnorm(h, mix_pre_g[i])
        if i % 2 == 0:
            m = even_mixer(n, w_in_even[j], w_out_even[j], lb_all[j], hg_norm_g[j])
        else:
            m = odd_mixer(n, w_in_odd[j], conv_w[j], conv_b[j], rg_wa[j], rg_ba[j],
                          rg_wx[j], rg_bx[j], rg_lambda[j], w_out_odd[j])
        h = h + rmsnorm(m, mix_post_g[i])
        h = h + rmsnorm(swiglu(rmsnorm(h, ffn_pre_g[i]), w_gate_up[i], w_down[i]), ffn_post_g[i])
        h = h + per_layer_embedding(h, p[i], w_ple_up[i], w_ple_gate[i], ple_norm_g[i])
    return h


import jax as _jax
import jax.numpy as _jnp

TWIN_FORMAT = 'train_step'
FWD_PARAMS = ['x', 'p', 'mix_pre_g', 'mix_post_g', 'ffn_pre_g', 'ffn_post_g', 'ple_norm_g', 'w_in_even', 'w_out_even', 'hg_lb_logits', 'hg_norm_g', 'w_in_odd', 'conv_w', 'conv_b', 'rg_wa', 'rg_ba', 'rg_wx', 'rg_bx', 'rg_lambda', 'w_out_odd', 'w_gate_up', 'w_down', 'w_ple_up', 'w_ple_gate']
TWIN_WEIGHTS = ['mix_pre_g', 'mix_post_g', 'ffn_pre_g', 'ffn_post_g', 'ple_norm_g', 'w_in_even', 'w_out_even', 'hg_lb_logits', 'hg_norm_g', 'w_in_odd', 'conv_w', 'conv_b', 'rg_wa', 'rg_ba', 'rg_wx', 'rg_bx', 'rg_lambda', 'w_out_odd', 'w_gate_up', 'w_down', 'w_ple_up', 'w_ple_gate']
TWIN_DIFF_INPUT = 'x'
TWIN_INPUTS = ['x', 'p', 'mix_pre_g', 'mix_post_g', 'ffn_pre_g', 'ffn_post_g', 'ple_norm_g', 'w_in_even', 'w_out_even', 'hg_lb_logits', 'hg_norm_g', 'w_in_odd', 'conv_w', 'conv_b', 'rg_wa', 'rg_ba', 'rg_wx', 'rg_bx', 'rg_lambda', 'w_out_odd', 'w_gate_up', 'w_down', 'w_ple_up', 'w_ple_gate', 'loss_target', 'm_mix_pre_g', 'm_mix_post_g', 'm_ffn_pre_g', 'm_ffn_post_g', 'm_ple_norm_g', 'm_w_in_even', 'm_w_out_even', 'm_hg_lb_logits', 'm_hg_norm_g', 'm_w_in_odd', 'm_conv_w', 'm_conv_b', 'm_rg_wa', 'm_rg_ba', 'm_rg_wx', 'm_rg_bx', 'm_rg_lambda', 'm_w_out_odd', 'm_w_gate_up', 'm_w_down', 'm_w_ple_up', 'm_w_ple_gate', 'v_mix_pre_g', 'v_mix_post_g', 'v_ffn_pre_g', 'v_ffn_post_g', 'v_ple_norm_g', 'v_w_in_even', 'v_w_out_even', 'v_hg_lb_logits', 'v_hg_norm_g', 'v_w_in_odd', 'v_conv_w', 'v_conv_b', 'v_rg_wa', 'v_rg_ba', 'v_rg_wx', 'v_rg_bx', 'v_rg_lambda', 'v_w_out_odd', 'v_w_gate_up', 'v_w_down', 'v_w_ple_up', 'v_w_ple_gate']
TWIN_OUTPUTS = ['loss', 'grad_x', 'grad_mix_pre_g', 'grad_mix_post_g', 'grad_ffn_pre_g', 'grad_ffn_post_g', 'grad_ple_norm_g', 'grad_w_in_even', 'grad_w_out_even', 'grad_hg_lb_logits', 'grad_hg_norm_g', 'grad_w_in_odd', 'grad_conv_w', 'grad_conv_b', 'grad_rg_wa', 'grad_rg_ba', 'grad_rg_wx', 'grad_rg_bx', 'grad_rg_lambda', 'grad_w_out_odd', 'grad_w_gate_up', 'grad_w_down', 'grad_w_ple_up', 'grad_w_ple_gate', 'delta_mix_pre_g', 'delta_mix_post_g', 'delta_ffn_pre_g', 'delta_ffn_post_g', 'delta_ple_norm_g', 'delta_w_in_even', 'delta_w_out_even', 'delta_hg_lb_logits', 'delta_hg_norm_g', 'delta_w_in_odd', 'delta_conv_w', 'delta_conv_b', 'delta_rg_wa', 'delta_rg_ba', 'delta_rg_wx', 'delta_rg_bx', 'delta_rg_lambda', 'delta_w_out_odd', 'delta_w_gate_up', 'delta_w_down', 'delta_w_ple_up', 'delta_w_ple_gate', 'new_m_mix_pre_g', 'new_m_mix_post_g', 'new_m_ffn_pre_g', 'new_m_ffn_post_g', 'new_m_ple_norm_g', 'new_m_w_in_even', 'new_m_w_out_even', 'new_m_hg_lb_logits', 'new_m_hg_norm_g', 'new_m_w_in_odd', 'new_m_conv_w', 'new_m_conv_b', 'new_m_rg_wa', 'new_m_rg_ba', 'new_m_rg_wx', 'new_m_rg_bx', 'new_m_rg_lambda', 'new_m_w_out_odd', 'new_m_w_gate_up', 'new_m_w_down', 'new_m_w_ple_up', 'new_m_w_ple_gate', 'new_v_mix_pre_g', 'new_v_mix_post_g', 'new_v_ffn_pre_g', 'new_v_ffn_post_g', 'new_v_ple_norm_g', 'new_v_w_in_even', 'new_v_w_out_even', 'new_v_hg_lb_logits', 'new_v_hg_norm_g', 'new_v_w_in_odd', 'new_v_conv_w', 'new_v_conv_b', 'new_v_rg_wa', 'new_v_rg_ba', 'new_v_rg_wx', 'new_v_rg_bx', 'new_v_rg_lambda', 'new_v_w_out_odd', 'new_v_w_gate_up', 'new_v_w_down', 'new_v_w_ple_up', 'new_v_w_ple_gate']
TWIN_LEAF_KINDS = {'loss': 'loss', 'grad_x': 'grad_x', 'grad_mix_pre_g': 'grad_w', 'grad_mix_post_g': 'grad_w', 'grad_ffn_pre_g': 'grad_w', 'grad_ffn_post_g': 'grad_w', 'grad_ple_norm_g': 'grad_w', 'grad_w_in_even': 'grad_w', 'grad_w_out_even': 'grad_w', 'grad_hg_lb_logits': 'grad_w', 'grad_hg_norm_g': 'grad_w', 'grad_w_in_odd': 'grad_w', 'grad_conv_w': 'grad_w', 'grad_conv_b': 'grad_w', 'grad_rg_wa': 'grad_w', 'grad_rg_ba': 'grad_w', 'grad_rg_wx': 'grad_w', 'grad_rg_bx': 'grad_w', 'grad_rg_lambda': 'grad_w', 'grad_w_out_odd': 'grad_w', 'grad_w_gate_up': 'grad_w', 'grad_w_down': 'grad_w', 'grad_w_ple_up': 'grad_w', 'grad_w_ple_gate': 'grad_w', 'delta_mix_pre_g': 'delta_w', 'delta_mix_post_g': 'delta_w', 'delta_ffn_pre_g': 'delta_w', 'delta_ffn_post_g': 'delta_w', 'delta_ple_norm_g': 'delta_w', 'delta_w_in_even': 'delta_w', 'delta_w_out_even': 'delta_w', 'delta_hg_lb_logits': 'delta_w', 'delta_hg_norm_g': 'delta_w', 'delta_w_in_odd': 'delta_w', 'delta_conv_w': 'delta_w', 'delta_conv_b': 'delta_w', 'delta_rg_wa': 'delta_w', 'delta_rg_ba': 'delta_w', 'delta_rg_wx': 'delta_w', 'delta_rg_bx': 'delta_w', 'delta_rg_lambda': 'delta_w', 'delta_w_out_odd': 'delta_w', 'delta_w_gate_up': 'delta_w', 'delta_w_down': 'delta_w', 'delta_w_ple_up': 'delta_w', 'delta_w_ple_gate': 'delta_w', 'new_m_mix_pre_g': 'new_m', 'new_m_mix_post_g': 'new_m', 'new_m_ffn_pre_g': 'new_m', 'new_m_ffn_post_g': 'new_m', 'new_m_ple_norm_g': 'new_m', 'new_m_w_in_even': 'new_m', 'new_m_w_out_even': 'new_m', 'new_m_hg_lb_logits': 'new_m', 'new_m_hg_norm_g': 'new_m', 'new_m_w_in_odd': 'new_m', 'new_m_conv_w': 'new_m', 'new_m_conv_b': 'new_m', 'new_m_rg_wa': 'new_m', 'new_m_rg_ba': 'new_m', 'new_m_rg_wx': 'new_m', 'new_m_rg_bx': 'new_m', 'new_m_rg_lambda': 'new_m', 'new_m_w_out_odd': 'new_m', 'new_m_w_gate_up': 'new_m', 'new_m_w_down': 'new_m', 'new_m_w_ple_up': 'new_m', 'new_m_w_ple_gate': 'new_m', 'new_v_mix_pre_g': 'new_v', 'new_v_mix_post_g': 'new_v', 'new_v_ffn_pre_g': 'new_v', 'new_v_ffn_post_g': 'new_v', 'new_v_ple_norm_g': 'new_v', 'new_v_w_in_even': 'new_v', 'new_v_w_out_even': 'new_v', 'new_v_hg_lb_logits': 'new_v', 'new_v_hg_norm_g': 'new_v', 'new_v_w_in_odd': 'new_v', 'new_v_conv_w': 'new_v', 'new_v_conv_b': 'new_v', 'new_v_rg_wa': 'new_v', 'new_v_rg_ba': 'new_v', 'new_v_rg_wx': 'new_v', 'new_v_rg_bx': 'new_v', 'new_v_rg_lambda': 'new_v', 'new_v_w_out_odd': 'new_v', 'new_v_w_gate_up': 'new_v', 'new_v_w_down': 'new_v', 'new_v_w_ple_up': 'new_v', 'new_v_w_ple_gate': 'new_v'}


def _forward(args):
    return _fwd_reference(*[args[k] for k in FWD_PARAMS])


def _output_shape():
    out = _jax.eval_shape(lambda: _forward(_fwd_setup_inputs(0)))
    return out.shape, out.dtype

N_MICROBATCH = 1
ADAM_LR = 0.001
ADAM_B1 = 0.9
ADAM_B2 = 0.999
ADAM_EPS = 1e-08
ADAM_WD = 0.01
ADAM_STEP = 10
PER_EXAMPLE_BATCH_AXIS = {'x': 0, 'p': 1, 'loss_target': 0}
SHARED_INPUTS = []
_WEIGHT_DTYPES = {'mix_pre_g': _jnp.float32, 'mix_post_g': _jnp.float32, 'ffn_pre_g': _jnp.float32, 'ffn_post_g': _jnp.float32, 'ple_norm_g': _jnp.float32, 'w_in_even': _jnp.float32, 'w_out_even': _jnp.float32, 'hg_lb_logits': _jnp.float32, 'hg_norm_g': _jnp.float32, 'w_in_odd': _jnp.float32, 'conv_w': _jnp.float32, 'conv_b': _jnp.float32, 'rg_wa': _jnp.float32, 'rg_ba': _jnp.float32, 'rg_wx': _jnp.float32, 'rg_bx': _jnp.float32, 'rg_lambda': _jnp.float32, 'w_out_odd': _jnp.float32, 'w_gate_up': _jnp.float32, 'w_down': _jnp.float32, 'w_ple_up': _jnp.float32, 'w_ple_gate': _jnp.float32}
MOMENT_SCALE = {'mix_pre_g': 4.175227e-01, 'mix_post_g': 7.963542e+00, 'ffn_pre_g': 3.547129e-01, 'ffn_post_g': 8.014914e+00, 'ple_norm_g': 8.019339e+00, 'w_in_even': 2.705541e-01, 'w_out_even': 3.900434e-01, 'hg_lb_logits': 3.207491e-02, 'hg_norm_g': 1.220991e+00, 'w_in_odd': 1.874703e-01, 'conv_w': 1.898431e-01, 'conv_b': 3.172656e+00, 'rg_wa': 5.651914e-02, 'rg_ba': 5.183318e-02, 'rg_wx': 1.012678e-01, 'rg_bx': 7.090421e-02, 'rg_lambda': 1.027338e-01, 'w_out_odd': 2.339016e-01, 'w_gate_up': 1.427366e-01, 'w_down': 2.430211e-01, 'w_ple_up': 2.208053e-01, 'w_ple_gate': 1.211421e-01}


def _to_microbatches(a, axis):
    t = _jnp.moveaxis(a, axis, 0)
    t = t.reshape((N_MICROBATCH, t.shape[0] // N_MICROBATCH) + t.shape[1:])
    return _jnp.moveaxis(t, 1, axis + 1)


def setup_inputs(seed: int = 0) -> dict:
    inp = _fwd_setup_inputs(seed)
    key = _jax.random.fold_in(_jax.random.key(seed), 7919)
    shape, _ = _output_shape()
    out = dict(inp)
    out["loss_target"] = _jax.random.normal(_jax.random.fold_in(key, 0), shape, _jnp.float32)
    for i, name in enumerate(TWIN_WEIGHTS):
        w = inp[name].astype(_jnp.float32)
        if MOMENT_SCALE is None:
            s = _jnp.sqrt(_jnp.mean(_jnp.square(w)) + 1e-30)
        else:
            s = MOMENT_SCALE[name]
        km, kv = _jax.random.split(_jax.random.fold_in(key, i + 1))
        out[name] = w
        out["m_" + name] = s * _jax.random.normal(km, w.shape, _jnp.float32)
        out["v_" + name] = (s * s) * _jax.random.uniform(kv, w.shape, _jnp.float32, 0.5, 1.5)
    if N_MICROBATCH > 1:
        for name, axis in PER_EXAMPLE_BATCH_AXIS.items():
            out[name] = _to_microbatches(out[name], axis)
    return {'x': out['x'], 'p': out['p'], 'mix_pre_g': out['mix_pre_g'], 'mix_post_g': out['mix_post_g'], 'ffn_pre_g': out['ffn_pre_g'], 'ffn_post_g': out['ffn_post_g'], 'ple_norm_g': out['ple_norm_g'], 'w_in_even': out['w_in_even'], 'w_out_even': out['w_out_even'], 'hg_lb_logits': out['hg_lb_logits'], 'hg_norm_g': out['hg_norm_g'], 'w_in_odd': out['w_in_odd'], 'conv_w': out['conv_w'], 'conv_b': out['conv_b'], 'rg_wa': out['rg_wa'], 'rg_ba': out['rg_ba'], 'rg_wx': out['rg_wx'], 'rg_bx': out['rg_bx'], 'rg_lambda': out['rg_lambda'], 'w_out_odd': out['w_out_odd'], 'w_gate_up': out['w_gate_up'], 'w_down': out['w_down'], 'w_ple_up': out['w_ple_up'], 'w_ple_gate': out['w_ple_gate'], 'loss_target': out['loss_target'], 'm_mix_pre_g': out['m_mix_pre_g'], 'm_mix_post_g': out['m_mix_post_g'], 'm_ffn_pre_g': out['m_ffn_pre_g'], 'm_ffn_post_g': out['m_ffn_post_g'], 'm_ple_norm_g': out['m_ple_norm_g'], 'm_w_in_even': out['m_w_in_even'], 'm_w_out_even': out['m_w_out_even'], 'm_hg_lb_logits': out['m_hg_lb_logits'], 'm_hg_norm_g': out['m_hg_norm_g'], 'm_w_in_odd': out['m_w_in_odd'], 'm_conv_w': out['m_conv_w'], 'm_conv_b': out['m_conv_b'], 'm_rg_wa': out['m_rg_wa'], 'm_rg_ba': out['m_rg_ba'], 'm_rg_wx': out['m_rg_wx'], 'm_rg_bx': out['m_rg_bx'], 'm_rg_lambda': out['m_rg_lambda'], 'm_w_out_odd': out['m_w_out_odd'], 'm_w_gate_up': out['m_w_gate_up'], 'm_w_down': out['m_w_down'], 'm_w_ple_up': out['m_w_ple_up'], 'm_w_ple_gate': out['m_w_ple_gate'], 'v_mix_pre_g': out['v_mix_pre_g'], 'v_mix_post_g': out['v_mix_post_g'], 'v_ffn_pre_g': out['v_ffn_pre_g'], 'v_ffn_post_g': out['v_ffn_post_g'], 'v_ple_norm_g': out['v_ple_norm_g'], 'v_w_in_even': out['v_w_in_even'], 'v_w_out_even': out['v_w_out_even'], 'v_hg_lb_logits': out['v_hg_lb_logits'], 'v_hg_norm_g': out['v_hg_norm_g'], 'v_w_in_odd': out['v_w_in_odd'], 'v_conv_w': out['v_conv_w'], 'v_conv_b': out['v_conv_b'], 'v_rg_wa': out['v_rg_wa'], 'v_rg_ba': out['v_rg_ba'], 'v_rg_wx': out['v_rg_wx'], 'v_rg_bx': out['v_rg_bx'], 'v_rg_lambda': out['v_rg_lambda'], 'v_w_out_odd': out['v_w_out_odd'], 'v_w_gate_up': out['v_w_gate_up'], 'v_w_down': out['v_w_down'], 'v_w_ple_up': out['v_w_ple_up'], 'v_w_ple_gate': out['v_w_ple_gate']}


def _loss(weights, diff, rest, loss_target):
    with _jax.named_scope("forward"):
        args = {**rest, TWIN_DIFF_INPUT: diff, **{k: w.astype(_WEIGHT_DTYPES[k]) for k, w in weights.items()}}
        y = _forward(args)
    with _jax.named_scope("loss_head"):
        err = _jnp.square(y.astype(_jnp.float32) - loss_target)
        return 0.5 * _jnp.sum(_jnp.mean(err, axis=-1)) if err.ndim else 0.5 * err


def _adamw(w, g, m, v):
    m = ADAM_B1 * m + (1.0 - ADAM_B1) * g
    v = ADAM_B2 * v + (1.0 - ADAM_B2) * _jnp.square(g)
    m_hat = m / (1.0 - ADAM_B1 ** ADAM_STEP)
    v_hat = v / (1.0 - ADAM_B2 ** ADAM_STEP)
    delta = -ADAM_LR * (m_hat / (_jnp.sqrt(v_hat) + ADAM_EPS) + ADAM_WD * w)
    return delta, m, v


def reference(x, p, mix_pre_g, mix_post_g, ffn_pre_g, ffn_post_g, ple_norm_g, w_in_even, w_out_even, hg_lb_logits, hg_norm_g, w_in_odd, conv_w, conv_b, rg_wa, rg_ba, rg_wx, rg_bx, rg_lambda, w_out_odd, w_gate_up, w_down, w_ple_up, w_ple_gate, loss_target, m_mix_pre_g, m_mix_post_g, m_ffn_pre_g, m_ffn_post_g, m_ple_norm_g, m_w_in_even, m_w_out_even, m_hg_lb_logits, m_hg_norm_g, m_w_in_odd, m_conv_w, m_conv_b, m_rg_wa, m_rg_ba, m_rg_wx, m_rg_bx, m_rg_lambda, m_w_out_odd, m_w_gate_up, m_w_down, m_w_ple_up, m_w_ple_gate, v_mix_pre_g, v_mix_post_g, v_ffn_pre_g, v_ffn_post_g, v_ple_norm_g, v_w_in_even, v_w_out_even, v_hg_lb_logits, v_hg_norm_g, v_w_in_odd, v_conv_w, v_conv_b, v_rg_wa, v_rg_ba, v_rg_wx, v_rg_bx, v_rg_lambda, v_w_out_odd, v_w_gate_up, v_w_down, v_w_ple_up, v_w_ple_gate):
    given = dict(x=x, p=p, mix_pre_g=mix_pre_g, mix_post_g=mix_post_g, ffn_pre_g=ffn_pre_g, ffn_post_g=ffn_post_g, ple_norm_g=ple_norm_g, w_in_even=w_in_even, w_out_even=w_out_even, hg_lb_logits=hg_lb_logits, hg_norm_g=hg_norm_g, w_in_odd=w_in_odd, conv_w=conv_w, conv_b=conv_b, rg_wa=rg_wa, rg_ba=rg_ba, rg_wx=rg_wx, rg_bx=rg_bx, rg_lambda=rg_lambda, w_out_odd=w_out_odd, w_gate_up=w_gate_up, w_down=w_down, w_ple_up=w_ple_up, w_ple_gate=w_ple_gate, loss_target=loss_target, m_mix_pre_g=m_mix_pre_g, m_mix_post_g=m_mix_post_g, m_ffn_pre_g=m_ffn_pre_g, m_ffn_post_g=m_ffn_post_g, m_ple_norm_g=m_ple_norm_g, m_w_in_even=m_w_in_even, m_w_out_even=m_w_out_even, m_hg_lb_logits=m_hg_lb_logits, m_hg_norm_g=m_hg_norm_g, m_w_in_odd=m_w_in_odd, m_conv_w=m_conv_w, m_conv_b=m_conv_b, m_rg_wa=m_rg_wa, m_rg_ba=m_rg_ba, m_rg_wx=m_rg_wx, m_rg_bx=m_rg_bx, m_rg_lambda=m_rg_lambda, m_w_out_odd=m_w_out_odd, m_w_gate_up=m_w_gate_up, m_w_down=m_w_down, m_w_ple_up=m_w_ple_up, m_w_ple_gate=m_w_ple_gate, v_mix_pre_g=v_mix_pre_g, v_mix_post_g=v_mix_post_g, v_ffn_pre_g=v_ffn_pre_g, v_ffn_post_g=v_ffn_post_g, v_ple_norm_g=v_ple_norm_g, v_w_in_even=v_w_in_even, v_w_out_even=v_w_out_even, v_hg_lb_logits=v_hg_lb_logits, v_hg_norm_g=v_hg_norm_g, v_w_in_odd=v_w_in_odd, v_conv_w=v_conv_w, v_conv_b=v_conv_b, v_rg_wa=v_rg_wa, v_rg_ba=v_rg_ba, v_rg_wx=v_rg_wx, v_rg_bx=v_rg_bx, v_rg_lambda=v_rg_lambda, v_w_out_odd=v_w_out_odd, v_w_gate_up=v_w_gate_up, v_w_down=v_w_down, v_w_ple_up=v_w_ple_up, v_w_ple_gate=v_w_ple_gate)
    weights = {n: given[n] for n in TWIN_WEIGHTS}
    shared = {n: given[n] for n in SHARED_INPUTS}
    per_example = {n: given[n] for n in ['x', 'p']}
    grad_fn = _jax.value_and_grad(_loss, argnums=(0, 1))

    def one_microbatch(ex, loss_target):
        ex = dict(ex)
        diff = ex.pop(TWIN_DIFF_INPUT)
        return grad_fn(weights, diff, {**shared, **ex}, loss_target)

    if N_MICROBATCH == 1:
        loss, (grad_w, grad_x) = one_microbatch(per_example, given["loss_target"])
    else:
        def body(carry, xs):
            loss_sum, grad_sum = carry
            l_k, (gw_k, gx_k) = one_microbatch(xs[0], xs[1])
            with _jax.named_scope("update"):
                return (loss_sum + l_k, _jax.tree.map(_jnp.add, grad_sum, gw_k)), gx_k

        init = (_jnp.zeros((), _jnp.float32), _jax.tree.map(_jnp.zeros_like, weights))
        (loss, grad_w), grad_x = _jax.lax.scan(body, init, (per_example, given["loss_target"]))
    with _jax.named_scope("update"):
        delta_w, new_m, new_v = {}, {}, {}
        for n in TWIN_WEIGHTS:
            delta_w[n], new_m[n], new_v[n] = _adamw(weights[n], grad_w[n], given["m_" + n], given["v_" + n])
    return (loss, grad_x, *[grad_w[n] for n in TWIN_WEIGHTS], *[delta_w[n] for n in TWIN_WEIGHTS],
            *[new_m[n] for n in TWIN_WEIGHTS], *[new_v[n] for n in TWIN_WEIGHTS])
```

```python
import functools
import math

import jax
import jax.numpy as jnp
from jax import lax
from jax.experimental import pallas as pl
from jax.experimental.pallas import tpu as pltpu

F32 = jnp.float32
BF16 = jnp.bfloat16

VMEM_LIMIT_BYTES = 56 * 1024 * 1024
LANES = 128
SUBLANES = 8

N_DEV = 8
HEAD_DIM = 128
SB_TILE = 128
HG_CHUNK = 32
RG_BLOCK = 256
CONV_TAPS = 4
RG_C = 8.0
RMS_EPS = 1e-6

ADAM_LR = 0.001
ADAM_B1 = 0.9
ADAM_B2 = 0.999
ADAM_EPS = 1e-08
ADAM_WD = 0.01
ADAM_STEP = 10

MESH_AXES = ("x", "y", "c")


def _params(*sem):
    return pltpu.CompilerParams(dimension_semantics=sem, vmem_limit_bytes=VMEM_LIMIT_BYTES)


def _pick(n, cands):
    for c in cands:
        if c <= n and n % c == 0:
            return c
    return n


def _mm(a, b, mode, *, name, out_dtype=F32, tm=512, tn=512, tk=None):
    if mode == "nn":
        (m, k), (k2, n) = a.shape, b.shape
    elif mode == "nt":
        (m, k), (n, k2) = a.shape, b.shape
    else:
        (k, m), (k2, n) = a.shape, b.shape
    assert k == k2, (a.shape, b.shape, mode)
    tm, tn = min(tm, m), min(tn, n)
    tk = k if tk is None else min(tk, k)
    assert m % tm == 0 and n % tn == 0 and k % tk == 0, (m, n, k, tm, tn, tk)
    nk = k // tk

    if mode == "tn":
        assert nk == 1
        return _mm_tn(a, b, name=name, out_dtype=out_dtype, tm=tm, tn=tn)

    a_spec = pl.BlockSpec((tm, tk), lambda i, j, kk: (i, kk))
    if mode == "nn":
        b_spec = pl.BlockSpec((tk, tn), lambda i, j, kk: (kk, j))
        dims = (((1,), (0,)), ((), ()))
    else:
        b_spec = pl.BlockSpec((tn, tk), lambda i, j, kk: (j, kk))
        dims = (((1,), (1,)), ((), ()))

    def body(a_ref, b_ref, o_ref, *acc):
        part = lax.dot_general(a_ref[...].astype(BF16), b_ref[...].astype(BF16), dims, preferred_element_type=F32)
        if nk == 1:
            o_ref[...] = part.astype(out_dtype)
        else:
            acc_ref, = acc
            kk = pl.program_id(2)

            @pl.when(kk == 0)
            def _():
                acc_ref[...] = part

            @pl.when(kk > 0)
            def _():
                acc_ref[...] += part

            @pl.when(kk == nk - 1)
            def _():
                o_ref[...] = acc_ref[...].astype(out_dtype)

    return pl.pallas_call(
        body, name=name,
        grid=(m // tm, n // tn, nk),
        in_specs=[a_spec, b_spec],
        out_specs=pl.BlockSpec((tm, tn), lambda i, j, kk: (i, j)),
        out_shape=jax.ShapeDtypeStruct((m, n), out_dtype),
        scratch_shapes=[] if nk == 1 else [pltpu.VMEM((tm, tn), F32)],
        compiler_params=_params("parallel", "parallel", "arbitrary"),
    )(a, b)


def _mm_tn(a, b, *, name, out_dtype, tm, tn):
    k, m = a.shape
    n = b.shape[1]

    def body(a_ref, b_ref, o_ref, at_ref):
        @pl.when(pl.program_id(1) == 0)
        def _():
            at_ref[...] = a_ref[...].astype(F32).T.astype(BF16)

        o_ref[...] = jnp.dot(at_ref[...], b_ref[...].astype(BF16), preferred_element_type=F32).astype(out_dtype)

    return pl.pallas_call(
        body, name=name,
        grid=(m // tm, n // tn),
        in_specs=[pl.BlockSpec((k, tm), lambda i, j: (0, i)), pl.BlockSpec((k, tn), lambda i, j: (0, j))],
        out_specs=pl.BlockSpec((tm, tn), lambda i, j: (i, j)),
        out_shape=jax.ShapeDtypeStruct((m, n), out_dtype),
        scratch_shapes=[pltpu.VMEM((tm, k), BF16)],
        compiler_params=_params("parallel", "arbitrary"),
    )(a, b)


def _rowcall(name, fn, rows, pars, row_outs, red_rows=(), *, cols, ts=256, tc=None):
    rows = [r if isinstance(r, tuple) else (r, 0) for r in rows]
    pars = [p if isinstance(p, tuple) else (p, 0) for p in pars]
    s = rows[0][0].shape[0]
    tc = cols if tc is None else tc
    ts = min(ts, s)
    assert s % ts == 0 and cols % tc == 0, (name, s, ts, cols, tc)
    n_in, n_row_out = len(rows) + len(pars), len(row_outs)

    def body(*refs):
        outs = fn(*[r[...] for r in refs[:n_in]])
        outs = outs if isinstance(outs, (tuple, list)) else (outs,)
        o_refs = refs[n_in:]
        for o_ref, val in zip(o_refs[:n_row_out], outs[:n_row_out]):
            o_ref[...] = val.astype(o_ref.dtype)
        first = pl.program_id(1) == 0
        for o_ref, val in zip(o_refs[n_row_out:], outs[n_row_out:]):
            @pl.when(first)
            def _(o_ref=o_ref, val=val):
                o_ref[...] = val

            @pl.when(jnp.logical_not(first))
            def _(o_ref=o_ref, val=val):
                o_ref[...] += val

    def row_map(off):
        return lambda j, i: (i, j + off)

    def par_map(off):
        return lambda j, i: (0, j + off)

    return pl.pallas_call(
        body, name=name,
        grid=(cols // tc, s // ts),
        in_specs=[pl.BlockSpec((ts, tc), row_map(off)) for _, off in rows]
        + [pl.BlockSpec((p.shape[0], tc), par_map(off)) for p, off in pars],
        out_specs=[pl.BlockSpec((ts, tc), lambda j, i: (i, j)) for _ in row_outs]
        + [pl.BlockSpec((r, tc), lambda j, i: (0, j)) for r in red_rows],
        out_shape=[jax.ShapeDtypeStruct((s, cols), dt) for dt in row_outs]
        + [jax.ShapeDtypeStruct((r, cols), F32) for r in red_rows],
        compiler_params=_params("parallel", "arbitrary"),
    )(*[r for r, _ in rows], *[p for p, _ in pars])


def _rms(x, g):
    return x * lax.rsqrt(jnp.mean(x * x, axis=-1, keepdims=True) + RMS_EPS) * g


def _sigmoid(x):
    return jax.nn.sigmoid(x)


def _silu(x):
    return x * jax.nn.sigmoid(x)


def _gelu(x):
    return 0.5 * x * (1.0 + jnp.tanh(math.sqrt(2.0 / math.pi) * (x + 0.044715 * (x * x * x))))


def _softplus(x):
    return jnp.maximum(x, 0.0) + jnp.log1p(jnp.exp(-jnp.abs(x)))


def _split3(x):
    hi = x.astype(BF16)
    r1 = x - hi.astype(F32)
    mid = r1.astype(BF16)
    lo = (r1 - mid.astype(F32)).astype(BF16)
    return hi, mid, lo


def _xdot(x, t):
    return sum(jnp.dot(p, t, preferred_element_type=F32) for p in _split3(x))


def _xdot_l(t, x):
    return sum(jnp.dot(t, p, preferred_element_type=F32) for p in _split3(x))


_NT = (((1,), (1,)), ((), ()))
_TN = (((0,), (0,)), ((), ()))


def _dot(a, b, dims=None):
    if dims is None:
        return jnp.dot(a.astype(BF16), b.astype(BF16), preferred_element_type=F32)
    return lax.dot_general(a.astype(BF16), b.astype(BF16), dims, preferred_element_type=F32)


def _iota(shape, axis):
    return lax.broadcasted_iota(jnp.int32, shape, axis)


def _sb_tile(qb, kblk, kb, qi, row, col, upper, c_rem):
    z = lax.dot_general(qb, kblk, _NT, preferred_element_type=F32)
    mask = (col + kb * SB_TILE) < (row + qi * SB_TILE)
    soft = jnp.log1p(jnp.exp(-jnp.abs(z)))
    lbeta = jnp.minimum(z, 0.0) - soft
    l1m = jnp.where(mask, -jnp.maximum(z, 0.0) - soft, 0.0)
    rem = _xdot(l1m, upper) + c_rem
    w = jnp.where(mask, jnp.exp(lbeta + rem), 0.0)
    return z, mask, lbeta, l1m, w


def _sb_fwd(proj, n_heads, *, name):
    s = proj.shape[0]
    t = SB_TILE
    scale = HEAD_DIM ** -0.5

    def body(q_ref, k_ref, v_ref, o_ref):
        qi = pl.program_id(1)
        qb = (q_ref[...] * scale).astype(BF16)
        row, col = _iota((t, t), 0), _iota((t, t), 1)
        upper = (row > col).astype(BF16)

        def step(j, carry):
            acc, c_rem = carry
            kb = qi - j
            rows = pl.ds(pl.multiple_of(kb * t, t), t)
            kblk = k_ref[rows, :].astype(BF16)
            vblk = v_ref[rows, :].astype(BF16)
            _, _, _, l1m, w = _sb_tile(qb, kblk, kb, qi, row, col, upper, c_rem)
            acc = acc + jnp.dot(w.astype(BF16), vblk, preferred_element_type=F32)
            return acc, c_rem + jnp.sum(l1m, axis=1, keepdims=True)

        acc, _ = lax.fori_loop(0, qi + 1, step, (jnp.zeros((t, HEAD_DIM), F32), jnp.zeros((t, 1), F32)))
        o_ref[...] = acc

    return pl.pallas_call(
        body, name=name,
        grid=(n_heads, s // t),
        in_specs=[pl.BlockSpec((t, HEAD_DIM), lambda h, i: (i, h)),
                  pl.BlockSpec((s, HEAD_DIM), lambda h, i: (0, n_heads + h)),
                  pl.BlockSpec((s, HEAD_DIM), lambda h, i: (0, 2 * n_heads + h))],
        out_specs=pl.BlockSpec((t, HEAD_DIM), lambda h, i: (i, h)),
        out_shape=jax.ShapeDtypeStruct((s, n_heads * HEAD_DIM), F32),
        compiler_params=_params("parallel", "arbitrary"),
    )(proj, proj, proj)


def _sb_bwd(proj, dcat, n_heads, *, name):
    s = proj.shape[0]
    t = SB_TILE
    scale = HEAD_DIM ** -0.5

    def body(q_ref, k_ref, v_ref, do_ref, dq_ref, dk_ref, dv_ref, g_s, sig_s):
        qi = pl.program_id(1)

        @pl.when(qi == 0)
        def _():
            dk_ref[...] = jnp.zeros_like(dk_ref)
            dv_ref[...] = jnp.zeros_like(dv_ref)

        qb = (q_ref[...] * scale).astype(BF16)
        dob = do_ref[...].astype(BF16)
        row, col = _iota((t, t), 0), _iota((t, t), 1)
        upper = (row > col).astype(BF16)
        lower_incl = (row >= col).astype(BF16)

        def weights(j, carry):
            c_rem, g_all = carry
            kb = qi - j
            rows = pl.ds(pl.multiple_of(kb * t, t), t)
            kblk = k_ref[rows, :].astype(BF16)
            vblk = v_ref[rows, :].astype(BF16)
            _, _, lbeta, l1m, w = _sb_tile(qb, kblk, kb, qi, row, col, upper, c_rem)
            g = w * lax.dot_general(dob, vblk, _NT, preferred_element_type=F32)
            dv_ref[rows, :] += lax.dot_general(w.astype(BF16), dob, _TN, preferred_element_type=F32)
            g_s[kb] = g
            sig_s[kb] = jnp.exp(lbeta)
            return c_rem + jnp.sum(l1m, axis=1, keepdims=True), g_all + jnp.sum(g, axis=1, keepdims=True)

        zero_col = jnp.zeros((t, 1), F32)
        _, g_all = lax.fori_loop(0, qi + 1, weights, (zero_col, zero_col))

        def scores(j, carry):
            dq, c_g = carry
            kb = qi - j
            rows = pl.ds(pl.multiple_of(kb * t, t), t)
            g, sig = g_s[kb], sig_s[kb]
            mask = (col + kb * t) < (row + qi * t)
            g_before = g_all - (_xdot(g, lower_incl) + c_g)
            dz = jnp.where(mask, g * (1.0 - sig) - g_before * sig, 0.0).astype(BF16)
            dq = dq + jnp.dot(dz, k_ref[rows, :].astype(BF16), preferred_element_type=F32)
            dk_ref[rows, :] += lax.dot_general(dz, qb, _TN, preferred_element_type=F32)
            return dq, c_g + jnp.sum(g, axis=1, keepdims=True)

        dq, _ = lax.fori_loop(0, qi + 1, scores, (jnp.zeros((t, HEAD_DIM), F32), zero_col))
        dq_ref[...] = dq * scale

    width = n_heads * HEAD_DIM
    return pl.pallas_call(
        body, name=name,
        grid=(n_heads, s // t),
        in_specs=[pl.BlockSpec((t, HEAD_DIM), lambda h, i: (i, h)),
                  pl.BlockSpec((s, HEAD_DIM), lambda h, i: (0, n_heads + h)),
                  pl.BlockSpec((s, HEAD_DIM), lambda h, i: (0, 2 * n_heads + h)),
                  pl.BlockSpec((t, HEAD_DIM), lambda h, i: (i, h))],
        out_specs=[pl.BlockSpec((t, HEAD_DIM), lambda h, i: (i, h)),
                   pl.BlockSpec((s, HEAD_DIM), lambda h, i: (0, h)),
                   pl.BlockSpec((s, HEAD_DIM), lambda h, i: (0, h))],
        out_shape=[jax.ShapeDtypeStruct((s, width), F32)] * 3,
        scratch_shapes=[pltpu.VMEM((s // t, t, t), F32)] * 2,
        compiler_params=_params("parallel", "arbitrary"),
    )(proj, proj, proj, dcat)


def _hg_pre(hq, hf, logits):
    mx = jnp.max(logits, axis=0, keepdims=True)
    ex = jnp.exp(logits - mx)
    lb = ex[0:1, :] / jnp.sum(ex, axis=0, keepdims=True)
    f = lb + (1.0 - lb) * _sigmoid(hf)
    return _silu(hq), 1.0 - f, jnp.log(f)


def _hg_post(o, norm_g, hgate):
    return _rms(o, norm_g) * _silu(hgate)


def _hg_specs(s, n_heads, first_block):
    def at(group):
        return pl.BlockSpec((s, HEAD_DIM), lambda h: (0, first_block + group * n_heads + h))
    return [at(0), at(1), at(2), at(3)]


def _hg_fwd(proj, logits, norm_g, n_heads, *, name):
    s = proj.shape[0]
    hc = HG_CHUNK
    n_chunks = s // hc
    d = HEAD_DIM

    def body(lg_ref, ng_ref, hq_ref, hf_ref, hi_ref, hgt_ref, out_ref, oraw_ref, st_ref,
             q_s, k_s, lf_s, cum_s, qc_s, oc_s):
        q, k, lf = _hg_pre(hq_ref[...], hf_ref[...], lg_ref[...])
        q_s[...] = q
        k_s[...] = k
        lf_s[...] = lf
        tril = (_iota((hc, hc), 0) >= _iota((hc, hc), 1)).astype(BF16)
        srow = _iota((hc, d), 0)

        def chunk(ci, st):
            rows = pl.ds(pl.multiple_of(ci * hc, hc), hc)
            q, k, v = q_s[rows, :], k_s[rows, :], hi_ref[rows, :]
            cum = _xdot_l(tril, lf_s[rows, :])
            st_ref[0, ci] = st
            o_inter = _dot(q * jnp.exp(cum), st, _NT)
            cum_s[...] = cum
            qc_s[...] = q
            for t in range(hc):
                ng = (t // SUBLANES + 1) * SUBLANES
                e = jnp.where(srow[:ng] <= t, jnp.exp(cum_s[t:t + 1, :] - cum[:ng]), 0.0)
                sc = jnp.sum(qc_s[t:t + 1, :] * k[:ng] * e, axis=1, keepdims=True)
                oc_s[t:t + 1, :] = jnp.sum(sc * v[:ng], axis=0, keepdims=True)
            oraw_ref[rows, :] = o_inter + oc_s[...]
            last = cum_s[hc - 1:hc, :]
            return st * jnp.exp(last) + _dot(v, k * jnp.exp(last - cum), _TN)

        lax.fori_loop(0, n_chunks, chunk, jnp.zeros((d, d), F32))
        out_ref[...] = _hg_post(oraw_ref[...], ng_ref[...], hgt_ref[...]).astype(BF16)

    width = n_heads * d
    head_block = pl.BlockSpec((s, d), lambda h: (0, h))
    return pl.pallas_call(
        body, name=name,
        grid=(n_heads,),
        in_specs=[pl.BlockSpec((2, d), lambda h: (0, h)), pl.BlockSpec((1, d), lambda h: (0, 0))]
        + _hg_specs(s, n_heads, 3 * n_heads),
        out_specs=[head_block, head_block, pl.BlockSpec((1, n_chunks, d, d), lambda h: (h, 0, 0, 0))],
        out_shape=[jax.ShapeDtypeStruct((s, width), BF16), jax.ShapeDtypeStruct((s, width), F32),
                   jax.ShapeDtypeStruct((n_heads, n_chunks, d, d), F32)],
        scratch_shapes=[pltpu.VMEM((s, d), F32)] * 3 + [pltpu.VMEM((hc, d), F32)] * 3,
        compiler_params=_params("arbitrary"),
    )(logits, norm_g, proj, proj, proj, proj)


def _hg_bwd(proj, logits, norm_g, oraw, states, dcat, n_heads, *, name):
    s = proj.shape[0]
    hc = HG_CHUNK
    n_chunks = s // hc
    d = HEAD_DIM

    def body(lg_ref, ng_ref, hq_ref, hf_ref, hi_ref, hgt_ref, oraw_ref, st_ref, dout_ref,
             dhq_ref, dhf_ref, dhi_ref, dhgt_ref, dlg_ref, dng_ref,
             q_s, k_s, lf_s, do_s, dq_s, dk_s, dlf_s, cum_s, qc_s, doc_s, dqc_s, dkc_s, dvc_s):
        head = pl.program_id(0)
        (q, k, lf), pre_vjp = jax.vjp(_hg_pre, hq_ref[...], hf_ref[...], lg_ref[...])
        q_s[...] = q
        k_s[...] = k
        lf_s[...] = lf
        _, post_vjp = jax.vjp(_hg_post, oraw_ref[...], ng_ref[...], hgt_ref[...])
        do, dng, dhgt = post_vjp(dout_ref[...])
        do_s[...] = do
        dhgt_ref[...] = dhgt.astype(BF16)

        @pl.when(head == 0)
        def _():
            dng_ref[...] = dng

        @pl.when(head > 0)
        def _():
            dng_ref[...] += dng

        triu = (_iota((hc, hc), 0) <= _iota((hc, hc), 1)).astype(BF16)
        tril = (_iota((hc, hc), 0) >= _iota((hc, hc), 1)).astype(BF16)
        srow = _iota((hc, d), 0)

        def chunk(j, dst):
            ci = n_chunks - 1 - j
            rows = pl.ds(pl.multiple_of(ci * hc, hc), hc)
            q, k, v, do_c = q_s[rows, :], k_s[rows, :], hi_ref[rows, :], do_s[rows, :]
            cum = _xdot_l(tril, lf_s[rows, :])
            st = st_ref[0, ci]
            cum_s[...] = cum
            qc_s[...] = q
            doc_s[...] = do_c
            last = cum_s[hc - 1:hc, :]
            e_cum, e_last = jnp.exp(cum), jnp.exp(last - cum)
            dqc_s[...] = _dot(do_c, st) * e_cum
            dk_state = _dot(v, dst) * e_last
            dkc_s[...] = dk_state
            dvc_s[...] = _dot(k * e_last, dst, _NT)
            d_last = (jnp.sum(dst * st, axis=0, keepdims=True) * jnp.exp(last)
                      + jnp.sum(k * dk_state, axis=0, keepdims=True))
            for t in range(hc):
                ng = (t // SUBLANES + 1) * SUBLANES
                qt, dot_ = qc_s[t:t + 1, :], doc_s[t:t + 1, :]
                e = jnp.where(srow[:ng] <= t, jnp.exp(cum_s[t:t + 1, :] - cum[:ng]), 0.0)
                ke = k[:ng] * e
                d_a = jnp.sum(dot_ * v[:ng], axis=1, keepdims=True)
                dqc_s[t:t + 1, :] += jnp.sum(d_a * ke, axis=0, keepdims=True)
                dkc_s[0:ng, :] += d_a * (qt * e)
                dvc_s[0:ng, :] += jnp.sum(qt * ke, axis=1, keepdims=True) * dot_
            dq, dk = dqc_s[...], dkc_s[...]
            d_b = q * dq - k * dk
            dq_s[rows, :] = dq
            dk_s[rows, :] = dk
            dhi_ref[rows, :] = dvc_s[...].astype(BF16)
            dlf_s[rows, :] = _xdot_l(triu, d_b) + d_last
            return dst * jnp.exp(last) + _dot(do_c, q * e_cum, _TN)

        lax.fori_loop(0, n_chunks, chunk, jnp.zeros((d, d), F32))
        dhq, dhf, dlg = pre_vjp((dq_s[...], dk_s[...], dlf_s[...]))
        dhq_ref[...] = dhq.astype(BF16)
        dhf_ref[...] = dhf.astype(BF16)
        dlg_ref[...] = dlg

    width = n_heads * d
    head_block = pl.BlockSpec((s, d), lambda h: (0, h))
    return pl.pallas_call(
        body, name=name,
        grid=(n_heads,),
        in_specs=[pl.BlockSpec((2, d), lambda h: (0, h)), pl.BlockSpec((1, d), lambda h: (0, 0))]
        + _hg_specs(s, n_heads, 3 * n_heads)
        + [head_block, pl.BlockSpec((1, n_chunks, d, d), lambda h: (h, 0, 0, 0)),
           pl.BlockSpec((s, d), lambda h: (0, n_heads + h))],
        out_specs=[head_block] * 4 + [pl.BlockSpec((2, d), lambda h: (0, h)), pl.BlockSpec((1, d), lambda h: (0, 0))],
        out_shape=[jax.ShapeDtypeStruct((s, width), BF16)] * 4
        + [jax.ShapeDtypeStruct((2, width), F32), jax.ShapeDtypeStruct((1, d), F32)],
        scratch_shapes=[pltpu.VMEM((s, d), F32)] * 7 + [pltpu.VMEM((hc, d), F32)] * 6,
        compiler_params=_params("arbitrary"),
    )(logits, norm_g, proj, proj, proj, proj, oraw, states, dcat)


def _shift_down(x, n, srow):
    if n == 0:
        return x
    return jnp.where(srow >= n, pltpu.roll(x, n, 0), 0.0)


def _shift_up(x, n, srow):
    if n == 0:
        return x
    s = x.shape[0]
    return jnp.where(srow < s - n, pltpu.roll(x, s - n, 0), 0.0)


def _rg_gates_fwd(proj, conv_w, conv_b, wa, ba, wx, bx, *, name):
    s = proj.shape[0]
    nb = wa.shape[0]
    bw = RG_BLOCK

    def body(xb_ref, cw_ref, cb_ref, wa_ref, ba_ref, wx_ref, bx_ref, xc_ref, ra_ref, ix_ref):
        x = xb_ref[...]
        srow = _iota((s, bw), 0)
        cw = cw_ref[...]
        xc = cb_ref[...] + cw[0:1, :] * x
        for tap in range(1, CONV_TAPS):
            xc = xc + cw[tap:tap + 1, :] * _shift_down(x, tap, srow)
        xc_ref[...] = xc
        ra_ref[...] = _dot(xc, wa_ref[0]) + ba_ref[0]
        ix_ref[...] = _dot(xc, wx_ref[0]) + bx_ref[0]

    col = pl.BlockSpec((s, bw), lambda n: (0, n))
    vec = lambda r: pl.BlockSpec((r, bw), lambda n: (0, n))
    mat = pl.BlockSpec((1, bw, bw), lambda n: (n, 0, 0))
    bias = pl.BlockSpec((1, 1, bw), lambda n: (n, 0, 0))
    return pl.pallas_call(
        body, name=name,
        grid=(nb,),
        in_specs=[pl.BlockSpec((s, bw), lambda n: (0, nb + n)), vec(CONV_TAPS), vec(1), mat, bias, mat, bias],
        out_specs=[col] * 3,
        out_shape=[jax.ShapeDtypeStruct((s, nb * bw), F32)] * 3,
        compiler_params=_params("parallel"),
    )(proj, conv_w, conv_b, wa, ba, wx, bx)


def _rg_au(ra, ix, xc, lam, first_row):
    log_a = -RG_C * _sigmoid(ra) * _softplus(-lam)
    th = jnp.tanh(log_a)
    one_minus_a2 = -2.0 * th / (1.0 - th)
    mult = jnp.where(first_row, 1.0, jnp.sqrt(one_minus_a2))
    return jnp.exp(log_a), xc * _sigmoid(ix) * mult


def _rg_out(gate, hs):
    return _gelu(gate) * hs


def _scan_rows(n_groups, reverse, group_fn, init):
    def group(gi, carry):
        g = (n_groups - 1 - gi) if reverse else gi
        return group_fn(pl.multiple_of(g * SUBLANES, SUBLANES), carry)
    return lax.fori_loop(0, n_groups, group, init)


def _rg_scan_fwd(proj, xc, ra, ix, lam, *, name):
    s, width = xc.shape
    tc = LANES

    def body(gate_ref, xc_ref, ra_ref, ix_ref, lam_ref, hs_ref, gact_ref, a_s, u_s):
        first_row = _iota((s, tc), 0) == 0
        a, u = _rg_au(ra_ref[...], ix_ref[...], xc_ref[...], lam_ref[...], first_row)
        a_s[...] = a
        u_s[...] = u
        r8 = _iota((SUBLANES, tc), 0)

        def rows(r0, h):
            ag, ug = a_s[pl.ds(r0, SUBLANES), :], u_s[pl.ds(r0, SUBLANES), :]
            tile = jnp.zeros((SUBLANES, tc), F32)
            for r in range(SUBLANES):
                h = ag[r:r + 1, :] * h + ug[r:r + 1, :]
                tile = jnp.where(r8 == r, h, tile)
            hs_ref[pl.ds(r0, SUBLANES), :] = tile
            return h

        _scan_rows(s // SUBLANES, False, rows, jnp.zeros((1, tc), F32))
        gact_ref[...] = _rg_out(gate_ref[...], hs_ref[...]).astype(BF16)

    col = pl.BlockSpec((s, tc), lambda n: (0, n))
    return pl.pallas_call(
        body, name=name,
        grid=(width // tc,),
        in_specs=[col, col, col, col, pl.BlockSpec((1, tc), lambda n: (0, n))],
        out_specs=[col, col],
        out_shape=[jax.ShapeDtypeStruct((s, width), F32), jax.ShapeDtypeStruct((s, width), BF16)],
        scratch_shapes=[pltpu.VMEM((s, tc), F32)] * 2,
        compiler_params=_params("parallel"),
    )(proj, xc, ra, ix, lam)


def _rg_scan_bwd(dgo, proj, hs, xc, ra, ix, lam, *, name):
    s, width = xc.shape
    tc = LANES

    def body(dgo_ref, gate_ref, hs_ref, xc_ref, ra_ref, ix_ref, lam_ref,
             dgate_ref, dra_ref, dix_ref, dxc_ref, dlam_ref, a_s, dh_s, g_s):
        srow = _iota((s, tc), 0)
        hs = hs_ref[...]
        _, out_vjp = jax.vjp(_rg_out, gate_ref[...], hs)
        dgate, dh = out_vjp(dgo_ref[...])
        dgate_ref[...] = dgate.astype(BF16)
        au = functools.partial(_rg_au, first_row=srow == 0)
        (a, _), au_vjp = jax.vjp(au, ra_ref[...], ix_ref[...], xc_ref[...], lam_ref[...])
        a_s[...] = a
        dh_s[...] = dh
        r8 = _iota((SUBLANES, tc), 0)

        def rows(r0, carry):
            g, a_next = carry
            ag, dg = a_s[pl.ds(r0, SUBLANES), :], dh_s[pl.ds(r0, SUBLANES), :]
            tile = jnp.zeros((SUBLANES, tc), F32)
            for r in reversed(range(SUBLANES)):
                g = dg[r:r + 1, :] + a_next * g
                a_next = ag[r:r + 1, :]
                tile = jnp.where(r8 == r, g, tile)
            g_s[pl.ds(r0, SUBLANES), :] = tile
            return g, a_next

        zero = jnp.zeros((1, tc), F32)
        _scan_rows(s // SUBLANES, True, rows, (zero, zero))
        g = g_s[...]
        dra, dix, dxc, dlam = au_vjp((g * _shift_down(hs, 1, srow), g))
        dra_ref[...] = dra.astype(BF16)
        dix_ref[...] = dix.astype(BF16)
        dxc_ref[...] = dxc
        dlam_ref[...] = dlam

    col = pl.BlockSpec((s, tc), lambda n: (0, n))
    vec = pl.BlockSpec((1, tc), lambda n: (0, n))
    return pl.pallas_call(
        body, name=name,
        grid=(width // tc,),
        in_specs=[col] * 6 + [vec],
        out_specs=[col] * 4 + [vec],
        out_shape=[jax.ShapeDtypeStruct((s, width), BF16)] * 3
        + [jax.ShapeDtypeStruct((s, width), F32), jax.ShapeDtypeStruct((1, width), F32)],
        scratch_shapes=[pltpu.VMEM((s, tc), F32)] * 3,
        compiler_params=_params("parallel"),
    )(dgo, proj, hs, xc, ra, ix, lam)


def _rg_gates_bwd(dra, dix, dxc1, xc, proj, conv_w, wa, wx, *, name):
    s = proj.shape[0]
    nb = wa.shape[0]
    bw = RG_BLOCK

    def body(dra_ref, dix_ref, dxc_ref, xc_ref, xb_ref, cw_ref, wa_ref, wx_ref,
             dxb_ref, dcw_ref, dcb_ref, dwa_ref, dba_ref, dwx_ref, dbx_ref):
        dra, dix = dra_ref[...], dix_ref[...]
        xc_t = xc_ref[...].T.astype(BF16)
        dwa_ref[0] = jnp.dot(xc_t, dra, preferred_element_type=F32)
        dwx_ref[0] = jnp.dot(xc_t, dix, preferred_element_type=F32)
        dba_ref[0] = jnp.sum(dra.astype(F32), axis=0, keepdims=True)
        dbx_ref[0] = jnp.sum(dix.astype(F32), axis=0, keepdims=True)
        dxc = dxc_ref[...] + _dot(dra, wa_ref[0], _NT) + _dot(dix, wx_ref[0], _NT)
        srow = _iota((s, bw), 0)
        x = xb_ref[...]
        cw = cw_ref[...]
        dx = cw[0:1, :] * dxc
        dcw = [jnp.sum(dxc * x, axis=0, keepdims=True)]
        for tap in range(1, CONV_TAPS):
            dx = dx + cw[tap:tap + 1, :] * _shift_up(dxc, tap, srow)
            dcw.append(jnp.sum(dxc * _shift_down(x, tap, srow), axis=0, keepdims=True))
        dxb_ref[...] = dx.astype(BF16)
        r4 = _iota((CONV_TAPS, bw), 0)
        acc = jnp.zeros((CONV_TAPS, bw), F32)
        for tap in range(CONV_TAPS):
            acc = jnp.where(r4 == tap, dcw[tap], acc)
        dcw_ref[...] = acc
        dcb_ref[...] = jnp.sum(dxc, axis=0, keepdims=True)

    col = pl.BlockSpec((s, bw), lambda n: (0, n))
    vec = lambda r: pl.BlockSpec((r, bw), lambda n: (0, n))
    mat = pl.BlockSpec((1, bw, bw), lambda n: (n, 0, 0))
    bias = pl.BlockSpec((1, 1, bw), lambda n: (n, 0, 0))
    width = nb * bw
    return pl.pallas_call(
        body, name=name,
        grid=(nb,),
        in_specs=[col, col, col, col, pl.BlockSpec((s, bw), lambda n: (0, nb + n)), vec(CONV_TAPS), mat, mat],
        out_specs=[col, vec(CONV_TAPS), vec(1), mat, bias, mat, bias],
        out_shape=[jax.ShapeDtypeStruct((s, width), BF16), jax.ShapeDtypeStruct((CONV_TAPS, width), F32),
                   jax.ShapeDtypeStruct((1, width), F32), jax.ShapeDtypeStruct((nb, bw, bw), F32),
                   jax.ShapeDtypeStruct((nb, 1, bw), F32), jax.ShapeDtypeStruct((nb, bw, bw), F32),
                   jax.ShapeDtypeStruct((nb, 1, bw), F32)],
        compiler_params=_params("parallel"),
    )(dra, dix, dxc1, xc, proj, conv_w, wa, wx)


_HBM = pl.BlockSpec(memory_space=pltpu.HBM)
_FLIPS = ((0, 0, 1), (1, 0, 0), (0, 1, 0), (1, 1, 0))
_ALL_FLIPS = tuple((a, b, c) for a in (0, 1) for b in (0, 1) for c in (0, 1))[1:]


def _flip(pos, f):
    return tuple(1 - p if b else p for p, b in zip(pos, f))


def _dev_index(pos):
    return 4 * pos[0] + 2 * pos[1] + pos[2]


def _block(ref, idx, cols):
    if not cols:
        return ref.at[idx]
    n = ref.shape[-1] // N_DEV
    start = pl.multiple_of(idx * n, LANES)
    return ref.at[(slice(None),) * (len(ref.shape) - 1) + (pl.ds(start, n),)]


def _all_gather(xs, *, name, cols=False):
    n_arr = len(xs)

    def body(*refs):
        x_refs, out_refs = refs[:n_arr], refs[n_arr:2 * n_arr]
        send_sems, recv_sems, local_sems = refs[2 * n_arr:]
        me = (lax.axis_index("x"), lax.axis_index("y"), lax.axis_index("c"))
        sibling = _flip(me, _FLIPS[0])
        chips = [_flip(me, f) for f in _FLIPS[1:]]

        def copy(a, k, block, to, src=None):
            dst = _block(out_refs[a], _dev_index(block), cols)
            return pltpu.make_async_remote_copy(
                src_ref=dst if src is None else src, dst_ref=dst,
                send_sem=send_sems.at[7 * a + k], recv_sem=recv_sems.at[7 * a + k],
                device_id=to, device_id_type=pl.DeviceIdType.MESH)

        mine = [pltpu.make_async_copy(x_refs[a], _block(out_refs[a], _dev_index(me), cols), local_sems.at[a])
                for a in range(n_arr)]
        for cp in mine:
            cp.start()
        first = []
        for a in range(n_arr):
            first.append(copy(a, 0, me, sibling, src=x_refs[a]))
            first += [copy(a, 1 + j, me, chip, src=x_refs[a]) for j, chip in enumerate(chips)]
        for cp in first:
            cp.start()
        passed = []
        for j, chip in enumerate(chips):
            for a in range(n_arr):
                copy(a, 1 + j, chip, me).wait_recv()
                fwd = copy(a, 4 + j, chip, sibling)
                fwd.start()
                passed.append(fwd)
        for a in range(n_arr):
            copy(a, 0, sibling, me).wait_recv()
            for j, chip in enumerate(chips):
                copy(a, 4 + j, _flip(chip, _FLIPS[0]), me).wait_recv()
        for cp in first + passed:
            cp.wait_send()
        for cp in mine:
            cp.wait()

    def out_shape(x):
        shape = x.shape[:-1] + (N_DEV * x.shape[-1],) if cols else (N_DEV,) + x.shape
        return jax.ShapeDtypeStruct(shape, x.dtype)

    return pl.pallas_call(
        body, name=name,
        in_specs=[_HBM] * n_arr, out_specs=[_HBM] * n_arr,
        out_shape=[out_shape(x) for x in xs],
        scratch_shapes=[pltpu.SemaphoreType.DMA((7 * n_arr,)), pltpu.SemaphoreType.DMA((7 * n_arr,)),
                        pltpu.SemaphoreType.DMA((n_arr,))],
    )(*xs)


def _exchange(ps, *, name, cols=False):
    n_arr = len(ps)
    blk = ps[0].shape[:-1] + (ps[0].shape[-1] // N_DEV,) if cols else ps[0].shape[1:]

    def body(*refs):
        p_refs, out_ref = refs[:n_arr], refs[n_arr]
        send_sems, recv_sems, local_sems = refs[n_arr + 1:]
        me = (lax.axis_index("x"), lax.axis_index("y"), lax.axis_index("c"))
        me_idx = _dev_index(me)
        own = [pltpu.make_async_copy(_block(p_refs[a], me_idx, cols), out_ref.at[me_idx, a], local_sems.at[a])
               for a in range(n_arr)]
        for cp in own:
            cp.start()
        sends = []
        for k, f in enumerate(_ALL_FLIPS):
            peer = _flip(me, f)
            for a in range(n_arr):
                cp = pltpu.make_async_remote_copy(
                    src_ref=_block(p_refs[a], _dev_index(peer), cols), dst_ref=out_ref.at[me_idx, a],
                    send_sem=send_sems.at[7 * a + k], recv_sem=recv_sems.at[7 * a + k],
                    device_id=peer, device_id_type=pl.DeviceIdType.MESH)
                cp.start()
                sends.append(cp)
        for cp in sends:
            cp.wait()
        for cp in own:
            cp.wait()

    return pl.pallas_call(
        body, name=name,
        in_specs=[_HBM] * n_arr, out_specs=_HBM,
        out_shape=jax.ShapeDtypeStruct((N_DEV, n_arr) + blk, ps[0].dtype),
        scratch_shapes=[pltpu.SemaphoreType.DMA((7 * n_arr,)), pltpu.SemaphoreType.DMA((7 * n_arr,)),
                        pltpu.SemaphoreType.DMA((n_arr,))],
    )(*ps)


def _adamw(parts, w, m, v, *, name):
    r, c = w.shape
    row_bytes = c * (N_DEV * parts.dtype.itemsize + 7 * 4) * 2
    tr = r
    for cand in (512, 256, 128, 64, 32, 16):
        if r % cand == 0 and cand * row_bytes <= 40 * 1024 * 1024:
            tr = cand
            break
    c1 = 1.0 - ADAM_B1 ** ADAM_STEP
    c2 = 1.0 - ADAM_B2 ** ADAM_STEP

    def body(p_ref, w_ref, m_ref, v_ref, g_ref, d_ref, nm_ref, nv_ref):
        g = p_ref[0].astype(F32)
        for j in range(1, N_DEV):
            g = g + p_ref[j].astype(F32)
        nm = ADAM_B1 * m_ref[...] + (1.0 - ADAM_B1) * g
        nv = ADAM_B2 * v_ref[...] + (1.0 - ADAM_B2) * (g * g)
        g_ref[...] = g
        nm_ref[...] = nm
        nv_ref[...] = nv
        d_ref[...] = -ADAM_LR * ((nm / c1) / (jnp.sqrt(nv / c2) + ADAM_EPS) + ADAM_WD * w_ref[...])

    blk = pl.BlockSpec((tr, c), lambda i: (i, 0))
    return pl.pallas_call(
        body, name=name,
        grid=(r // tr,),
        in_specs=[pl.BlockSpec((N_DEV, tr, c), lambda i: (0, i, 0)), blk, blk, blk],
        out_specs=[blk] * 4,
        out_shape=[jax.ShapeDtypeStruct((r, c), F32)] * 4,
        compiler_params=_params("parallel"),
    )(parts, w, m, v)


_TN_CANDS = (512, 256, 128)
_TK_CANDS = (2048, 1536, 1408, 1024, 768, 512, 256, 128)


def _nn(a, b, name, out_dtype=F32):
    return _mm(a, b, "nn", name=name, out_dtype=out_dtype, tm=a.shape[0], tn=_pick(b.shape[1], _TN_CANDS),
               tk=_pick(a.shape[1], _TK_CANDS))


def _nt(a, b, name, out_dtype=F32):
    return _mm(a, b, "nt", name=name, out_dtype=out_dtype, tm=a.shape[0], tn=_pick(b.shape[0], _TN_CANDS),
               tk=_pick(a.shape[1], _TK_CANDS))


def _tn(a, b, name, out_dtype=BF16):
    return _mm(a, b, "tn", name=name, out_dtype=out_dtype, tm=_pick(a.shape[1], _TN_CANDS),
               tn=_pick(b.shape[1], (1024,) + _TN_CANDS))


def _local_step(x, p, target, rep, wts):
    s, d = x.shape
    depth = p.shape[0]
    n_heads = wts["w_in_even"].shape[1] // (7 * HEAD_DIM)
    ff = wts["w_down"][0].shape[0]
    ff_tc = _pick(ff, (512, 256, 128))
    grads = {}
    rep_grads = {k: [None] * depth for k in ("mix_pre_g", "mix_post_g", "ffn_pre_g", "ffn_post_g", "ple_norm_g")}

    def gain(name, i):
        return rep[name][i:i + 1]

    saved = []
    h = x
    for i in range(depth):
        sv = {"h": h}
        n1, = _rowcall(f"pre_norm{i}", lambda hh, g: _rms(hh, g), [h], [gain("mix_pre_g", i)], [BF16], cols=d)
        sv["n1"] = n1
        if i % 2 == 0:
            proj = _nn(n1, wts["w_in_even"], f"in_even{i}")
            a_out = _sb_fwd(proj, n_heads, name=f"sb_fwd{i}")
            b_out, oraw, states = _hg_fwd(proj, rep["hg_lb_logits"], rep["hg_norm_g"], n_heads, name=f"hg_fwd{i}")
            cat = jnp.concatenate([a_out.astype(BF16), b_out], axis=1)
            m = _nn(cat, wts["w_out_even"], f"out_even{i}")
            sv.update(proj=proj, a_out=a_out, oraw=oraw, states=states, cat=cat)
        else:
            proj = _nn(n1, wts["w_in_odd"], f"in_odd{i}")
            xc, ra, ix = _rg_gates_fwd(proj, wts["conv_w"], wts["conv_b"], wts["rg_wa"], wts["rg_ba"],
                                       wts["rg_wx"], wts["rg_bx"], name=f"rg_gates_fwd{i}")
            hs, gact = _rg_scan_fwd(proj, xc, ra, ix, wts["rg_lambda"], name=f"rg_scan_fwd{i}")
            m = _nn(gact, wts["w_out_odd"], f"out_odd{i}")
            sv.update(proj=proj, xc=xc, ra=ra, ix=ix, hs=hs, gact=gact)

        def post_mix(hh, mm, g_post, g_pre):
            h1 = hh + _rms(mm, g_post)
            return h1, _rms(h1, g_pre)

        h1, n2 = _rowcall(f"post_mix{i}", post_mix, [h, m], [gain("mix_post_g", i), gain("ffn_pre_g", i)],
                          [F32, BF16], cols=d)
        gu = _nn(n2, wts["w_gate_up"][i], f"gate_up{i}")
        act, = _rowcall(f"swiglu{i}", lambda g, u: _silu(g) * u, [(gu, 0), (gu, ff // ff_tc)], [], [BF16],
                        cols=ff, tc=ff_tc)
        f = _nn(act, wts["w_down"][i], f"down{i}")

        def post_ffn(hh, ff_out, g_post):
            h2 = hh + _rms(ff_out, g_post)
            return h2, h2

        h2, h2b = _rowcall(f"post_ffn{i}", post_ffn, [h1, f], [gain("ffn_post_g", i)], [F32, BF16], cols=d)
        e = _nn(p[i], wts["w_ple_up"][i], f"ple_up{i}")
        gl = _nn(h2b, wts["w_ple_gate"][i], f"ple_gate{i}")
        h3, = _rowcall(f"ple{i}", lambda hh, a, b, g: hh + _rms(_sigmoid(a) * b, g), [h2, gl, e],
                       [gain("ple_norm_g", i)], [F32], cols=d)
        sv.update(m=m, h1=h1, n2=n2, gu=gu, act=act, f=f, h2b=h2b, e=e, gl=gl)
        saved.append(sv)
        h = h3

    def loss_fn(y, t):
        err = y - t
        return err * (1.0 / d), jnp.sum(err * err, axis=0, keepdims=True) * (0.5 / d)

    dh, loss_cols = _rowcall("loss", loss_fn, [h, target], [], [F32], red_rows=(1,), cols=d)

    for i in reversed(range(depth)):
        sv = saved[i]

        def ple_bwd(dy, a, b, g):
            _, vjp = jax.vjp(lambda a_, b_, g_: _rms(_sigmoid(a_) * b_, g_), a, b, g)
            return vjp(dy)

        dgl, de, rep_grads["ple_norm_g"][i] = _rowcall(
            f"ple_bwd{i}", ple_bwd, [dh, sv["gl"], sv["e"]], [gain("ple_norm_g", i)], [BF16, BF16],
            red_rows=(1,), cols=d)
        grads.setdefault("w_ple_up", [None] * depth)[i] = _tn(p[i], de, f"d_ple_up{i}")
        grads.setdefault("w_ple_gate", [None] * depth)[i] = _tn(sv["h2b"], dgl, f"d_ple_gate{i}")
        dh2_ple = _nt(dgl, wts["w_ple_gate"][i], f"dx_ple_gate{i}")

        def post_ffn_bwd(dy, dx, ff_out, g):
            dh2 = dy + dx
            _, vjp = jax.vjp(_rms, ff_out, g)
            df, dg = vjp(dh2)
            return dh2, df, dg

        dh2, df, rep_grads["ffn_post_g"][i] = _rowcall(
            f"post_ffn_bwd{i}", post_ffn_bwd, [dh, dh2_ple, sv["f"]], [gain("ffn_post_g", i)], [F32, BF16],
            red_rows=(1,), cols=d)
        grads.setdefault("w_down", [None] * depth)[i] = _tn(sv["act"], df, f"d_down{i}")
        dact = _nt(df, wts["w_down"][i], f"dx_down{i}")

        def swiglu_bwd(g, u, dy):
            _, vjp = jax.vjp(lambda g_, u_: _silu(g_) * u_, g, u)
            return vjp(dy)

        dg_, du_ = _rowcall(f"swiglu_bwd{i}", swiglu_bwd, [(sv["gu"], 0), (sv["gu"], ff // ff_tc), dact], [],
                            [BF16, BF16], cols=ff, tc=ff_tc)
        dgu = jnp.concatenate([dg_, du_], axis=1)
        grads.setdefault("w_gate_up", [None] * depth)[i] = _tn(sv["n2"], dgu, f"d_gate_up{i}")
        dn2 = _nt(dgu, wts["w_gate_up"][i], f"dx_gate_up{i}")

        def post_mix_bwd(dy, dn, h1, mm, g_post, g_pre):
            _, vjp_pre = jax.vjp(_rms, h1, g_pre)
            dh1_n, dg_pre = vjp_pre(dn)
            dh1 = dy + dh1_n
            _, vjp_post = jax.vjp(_rms, mm, g_post)
            dm, dg_post = vjp_post(dh1)
            return dh1, dm, dg_pre, dg_post

        dh1, dm, rep_grads["ffn_pre_g"][i], rep_grads["mix_post_g"][i] = _rowcall(
            f"post_mix_bwd{i}", post_mix_bwd, [dh2, dn2, sv["h1"], sv["m"]],
            [gain("mix_post_g", i), gain("ffn_pre_g", i)], [F32, BF16], red_rows=(1, 1), cols=d)

        if i % 2 == 0:
            grads["w_out_even"] = _tn(sv["cat"], dm, f"d_out_even{i}")
            dcat = _nt(dm, wts["w_out_even"], f"dx_out_even{i}")
            dq, dk, dv = _sb_bwd(sv["proj"], dcat, n_heads, name=f"sb_bwd{i}")
            dhq, dhf, dhi, dhg, grads["hg_lb_logits"], grads["hg_norm_g"] = _hg_bwd(
                sv["proj"], rep["hg_lb_logits"], rep["hg_norm_g"], sv["oraw"], sv["states"], dcat, n_heads,
                name=f"hg_bwd{i}")
            dproj = jnp.concatenate([dq.astype(BF16), dk.astype(BF16), dv.astype(BF16), dhq, dhf, dhi, dhg], axis=1)
            grads["w_in_even"] = _tn(sv["n1"], dproj, f"d_in_even{i}")
            dn1 = _nt(dproj, wts["w_in_even"], f"dx_in_even{i}")
        else:
            grads["w_out_odd"] = _tn(sv["gact"], dm, f"d_out_odd{i}")
            dgo = _nt(dm, wts["w_out_odd"], f"dx_out_odd{i}")
            dgate, dra, dix, dxc1, grads["rg_lambda"] = _rg_scan_bwd(
                dgo, sv["proj"], sv["hs"], sv["xc"], sv["ra"], sv["ix"], wts["rg_lambda"], name=f"rg_scan_bwd{i}")
            (dxb, grads["conv_w"], grads["conv_b"], grads["rg_wa"], grads["rg_ba"], grads["rg_wx"],
             grads["rg_bx"]) = _rg_gates_bwd(dra, dix, dxc1, sv["xc"], sv["proj"], wts["conv_w"], wts["rg_wa"],
                                            wts["rg_wx"], name=f"rg_gates_bwd{i}")
            dproj = jnp.concatenate([dgate, dxb], axis=1)
            grads["w_in_odd"] = _tn(sv["n1"], dproj, f"d_in_odd{i}")
            dn1 = _nt(dproj, wts["w_in_odd"], f"dx_in_odd{i}")

        def pre_norm_bwd(dy, dn, hh, g):
            _, vjp = jax.vjp(_rms, hh, g)
            dx, dg = vjp(dn)
            return dy + dx, dg

        dh, rep_grads["mix_pre_g"][i] = _rowcall(
            f"pre_norm_bwd{i}", pre_norm_bwd, [dh1, dn1, sv["h"]], [gain("mix_pre_g", i)], [F32],
            red_rows=(1,), cols=d)

    for k, rows in rep_grads.items():
        grads[k] = jnp.concatenate(rows, axis=0)
    return loss_cols, dh, grads


_WEIGHTS = ("mix_pre_g", "mix_post_g", "ffn_pre_g", "ffn_post_g", "ple_norm_g", "w_in_even", "w_out_even",
            "hg_lb_logits", "hg_norm_g", "w_in_odd", "conv_w", "conv_b", "rg_wa", "rg_ba", "rg_wx", "rg_bx",
            "rg_lambda", "w_out_odd", "w_gate_up", "w_down", "w_ple_up", "w_ple_gate")
_REPLICATED = ("mix_pre_g", "mix_post_g", "ffn_pre_g", "ffn_post_g", "ple_norm_g", "hg_lb_logits", "hg_norm_g")
_SMALL = ("conv_w", "conv_b", "rg_wa", "rg_ba", "rg_wx", "rg_bx", "rg_lambda")
_BIG = {"w_in_even": True, "w_out_even": False, "w_in_odd": True, "w_out_odd": False,
        "w_gate_up": True, "w_down": False, "w_ple_up": True, "w_ple_gate": False}
_PACK_ROW = SUBLANES * LANES


def _pack(arrays):
    flat = jnp.concatenate([a.reshape(-1) for a in arrays])
    pad = -flat.shape[0] % _PACK_ROW
    return jnp.pad(flat, (0, pad)).reshape(-1, LANES)


def _pack_blocks(arrays):
    flat = jnp.concatenate([a.reshape(N_DEV, -1) for a in arrays], axis=1)
    pad = -flat.shape[1] % _PACK_ROW
    return jnp.pad(flat, ((0, 0), (0, pad))).reshape(N_DEV, -1, LANES)


def _unpack(packed, shapes, lead=()):
    flat = packed.reshape(lead + (-1,))
    out, pos = [], 0
    for shape in shapes:
        n = math.prod(shape)
        out.append(flat[..., pos:pos + n].reshape(lead + tuple(shape)))
        pos += n
    return out


def _to_full_small(name, blocks):
    if name == "conv_w":
        return jnp.transpose(blocks, (1, 0, 2)).reshape(blocks.shape[1], -1)
    if name in ("conv_b", "rg_lambda"):
        return blocks.reshape(1, -1)
    nb = blocks.shape[1]
    if name in ("rg_wa", "rg_wx"):
        return jnp.transpose(blocks, (1, 0, 2, 3)).reshape(nb, RG_BLOCK, RG_BLOCK)
    return jnp.transpose(blocks, (1, 0, 2)).reshape(nb, 1, RG_BLOCK)


def _to_blocks_small(name, full):
    if name == "conv_w":
        return jnp.transpose(full.reshape(full.shape[0], N_DEV, -1), (1, 0, 2))
    if name in ("conv_b", "rg_lambda"):
        return full.reshape(N_DEV, -1)
    nb = full.shape[0]
    if name in ("rg_wa", "rg_wx"):
        return jnp.transpose(full.reshape(nb, N_DEV, RG_BLOCK // N_DEV, RG_BLOCK), (1, 0, 2, 3))
    return jnp.transpose(full.reshape(nb, N_DEV, RG_BLOCK // N_DEV), (1, 0, 2))


def _step(inp):
    w = {k: inp[k] for k in _WEIGHTS}
    x, p, target = inp["x"][0], inp["p"][:, 0], inp["loss_target"][0]
    assert w["hg_lb_logits"].shape[0] == 2 and w["w_in_even"].shape[0] == 1 and w["w_in_odd"].shape[0] == 1

    full = {}
    for name, by_cols in _BIG.items():
        shards = [w[name][l].astype(BF16) for l in range(w[name].shape[0])]
        r, c = shards[0].shape
        if by_cols and c % LANES == 0:
            got = _all_gather(shards, name=f"gather_{name}", cols=True)
        else:
            got = _all_gather(shards, name=f"gather_{name}")
            if by_cols:
                got = [jnp.transpose(g, (1, 0, 2)).reshape(r, N_DEV * c) for g in got]
            else:
                got = [g.reshape(N_DEV * r, c) for g in got]
        full[name] = got
    small_shapes = [w[k].shape[1:] for k in _SMALL]
    small_pack, = _all_gather([_pack([w[k][0] for k in _SMALL])], name="gather_small")
    for k, blocks in zip(_SMALL, _unpack(small_pack, small_shapes, lead=(N_DEV,))):
        full[k] = _to_full_small(k, blocks)
    for k in ("w_in_even", "w_out_even", "w_in_odd", "w_out_odd"):
        full[k] = full[k][0]

    rep = {k: w[k] for k in _REPLICATED}
    loss_cols, dx, grads = _local_step(x, p, target, rep, full)
    loss = lax.psum(jnp.sum(loss_cols), MESH_AXES)

    out = {}
    for name, by_cols in _BIG.items():
        g = grads[name] if isinstance(grads[name], list) else [grads[name]]
        n_l, r, c = w[name].shape
        if by_cols and c % LANES == 0:
            parts = _exchange(g, name=f"exchange_{name}", cols=True)
        elif by_cols:
            parts = _exchange([jnp.transpose(a.reshape(r, N_DEV, c), (1, 0, 2)) for a in g], name=f"exchange_{name}")
        else:
            parts = _exchange([a.reshape(N_DEV, r, c) for a in g], name=f"exchange_{name}")
        res = _adamw(parts.reshape(N_DEV, n_l * r, c),
                     *[inp[pre + name].reshape(n_l * r, c) for pre in ("", "m_", "v_")], name=f"adamw_{name}")
        out[name] = [a.reshape(n_l, r, c) for a in res]

    small_parts = _exchange([_pack_blocks([_to_blocks_small(k, grads[k]) for k in _SMALL])], name="exchange_small")
    res = _adamw(small_parts[:, 0], *[_pack([inp[pre + k][0] for k in _SMALL]) for pre in ("", "m_", "v_")],
                 name="adamw_small")
    for k, *vals in zip(_SMALL, *[_unpack(a, small_shapes) for a in res]):
        out[k] = [v[None] for v in vals]

    rep_shapes = [w[k].shape for k in _REPLICATED]
    rep_parts, = _all_gather([_pack([grads[k] for k in _REPLICATED])], name="gather_rep_grads")
    res = _adamw(rep_parts, *[_pack([inp[pre + k] for k in _REPLICATED]) for pre in ("", "m_", "v_")],
                 name="adamw_rep")
    for k, *vals in zip(_REPLICATED, *[_unpack(a, rep_shapes) for a in res]):
        out[k] = vals

    return (loss, dx[None]) + tuple(out[k][j] for j in range(4) for k in _WEIGHTS)


def kernel(x, p, mix_pre_g, mix_post_g, ffn_pre_g, ffn_post_g, ple_norm_g, w_in_even, w_out_even, hg_lb_logits, hg_norm_g, w_in_odd, conv_w, conv_b, rg_wa, rg_ba, rg_wx, rg_bx, rg_lambda, w_out_odd, w_gate_up, w_down, w_ple_up, w_ple_gate, loss_target, m_mix_pre_g, m_mix_post_g, m_ffn_pre_g, m_ffn_post_g, m_ple_norm_g, m_w_in_even, m_w_out_even, m_hg_lb_logits, m_hg_norm_g, m_w_in_odd, m_conv_w, m_conv_b, m_rg_wa, m_rg_ba, m_rg_wx, m_rg_bx, m_rg_lambda, m_w_out_odd, m_w_gate_up, m_w_down, m_w_ple_up, m_w_ple_gate, v_mix_pre_g, v_mix_post_g, v_ffn_pre_g, v_ffn_post_g, v_ple_norm_g, v_w_in_even, v_w_out_even, v_hg_lb_logits, v_hg_norm_g, v_w_in_odd, v_conv_w, v_conv_b, v_rg_wa, v_rg_ba, v_rg_wx, v_rg_bx, v_rg_lambda, v_w_out_odd, v_w_gate_up, v_w_down, v_w_ple_up, v_w_ple_gate):
    return _step(dict(locals()))
```

```python
import functools
import math

import jax
import jax.numpy as jnp
from jax import lax
from jax.experimental import pallas as pl
from jax.experimental.pallas import tpu as pltpu

F32 = jnp.float32
BF16 = jnp.bfloat16

VMEM_LIMIT_BYTES = 56 * 1024 * 1024
LANES = 128
SUBLANES = 8

N_DEV = 8
HEAD_DIM = 128
SB_TILE = 128
HG_CHUNK = 32
RG_BLOCK = 256
CONV_TAPS = 4
RG_C = 8.0
RMS_EPS = 1e-6

ADAM_LR = 0.001
ADAM_B1 = 0.9
ADAM_B2 = 0.999
ADAM_EPS = 1e-08
ADAM_WD = 0.01
ADAM_STEP = 10

MESH_AXES = ("x", "y", "c")


def _params(*sem):
    return pltpu.CompilerParams(dimension_semantics=sem, vmem_limit_bytes=VMEM_LIMIT_BYTES)


def _pick(n, cands):
    for c in cands:
        if c <= n and n % c == 0:
            return c
    return n


def _mm(a, b, mode, *, name, out_dtype=F32, tm=512, tn=512, tk=None):
    if mode == "nn":
        (m, k), (k2, n) = a.shape, b.shape
    elif mode == "nt":
        (m, k), (n, k2) = a.shape, b.shape
    else:
        (k, m), (k2, n) = a.shape, b.shape
    assert k == k2, (a.shape, b.shape, mode)
    tm, tn = min(tm, m), min(tn, n)
    tk = k if tk is None else min(tk, k)
    assert m % tm == 0 and n % tn == 0 and k % tk == 0, (m, n, k, tm, tn, tk)
    nk = k // tk

    if mode == "tn":
        assert nk == 1
        return _mm_tn(a, b, name=name, out_dtype=out_dtype, tm=tm, tn=tn)

    a_spec = pl.BlockSpec((tm, tk), lambda i, j, kk: (i, kk))
    if mode == "nn":
        b_spec = pl.BlockSpec((tk, tn), lambda i, j, kk: (kk, j))
        dims = (((1,), (0,)), ((), ()))
    else:
        b_spec = pl.BlockSpec((tn, tk), lambda i, j, kk: (j, kk))
        dims = (((1,), (1,)), ((), ()))

    def body(a_ref, b_ref, o_ref, *acc):
        part = lax.dot_general(a_ref[...].astype(BF16), b_ref[...].astype(BF16), dims, preferred_element_type=F32)
        if nk == 1:
            o_ref[...] = part.astype(out_dtype)
        else:
            acc_ref, = acc
            kk = pl.program_id(2)

            @pl.when(kk == 0)
            def _():
                acc_ref[...] = part

            @pl.when(kk > 0)
            def _():
                acc_ref[...] += part

            @pl.when(kk == nk - 1)
            def _():
                o_ref[...] = acc_ref[...].astype(out_dtype)

    return pl.pallas_call(
        body, name=name,
        grid=(m // tm, n // tn, nk),
        in_specs=[a_spec, b_spec],
        out_specs=pl.BlockSpec((tm, tn), lambda i, j, kk: (i, j)),
        out_shape=jax.ShapeDtypeStruct((m, n), out_dtype),
        scratch_shapes=[] if nk == 1 else [pltpu.VMEM((tm, tn), F32)],
        compiler_params=_params("parallel", "parallel", "arbitrary"),
    )(a, b)


def _mm_tn(a, b, *, name, out_dtype, tm, tn):
    k, m = a.shape
    n = b.shape[1]

    def body(a_ref, b_ref, o_ref, at_ref):
        @pl.when(pl.program_id(1) == 0)
        def _():
            at_ref[...] = a_ref[...].astype(F32).T.astype(BF16)

        o_ref[...] = jnp.dot(at_ref[...], b_ref[...].astype(BF16), preferred_element_type=F32).astype(out_dtype)

    return pl.pallas_call(
        body, name=name,
        grid=(m // tm, n // tn),
        in_specs=[pl.BlockSpec((k, tm), lambda i, j: (0, i)), pl.BlockSpec((k, tn), lambda i, j: (0, j))],
        out_specs=pl.BlockSpec((tm, tn), lambda i, j: (i, j)),
        out_shape=jax.ShapeDtypeStruct((m, n), out_dtype),
        scratch_shapes=[pltpu.VMEM((tm, k), BF16)],
        compiler_params=_params("parallel", "arbitrary"),
    )(a, b)


def _rowcall(name, fn, rows, pars, row_outs, red_rows=(), *, cols, ts=256, tc=None):
    rows = [r if isinstance(r, tuple) else (r, 0) for r in rows]
    pars = [p if isinstance(p, tuple) else (p, 0) for p in pars]
    s = rows[0][0].shape[0]
    tc = cols if tc is None else tc
    ts = min(ts, s)
    assert s % ts == 0 and cols % tc == 0, (name, s, ts, cols, tc)
    n_in, n_row_out = len(rows) + len(pars), len(row_outs)

    def body(*refs):
        outs = fn(*[r[...] for r in refs[:n_in]])
        outs = outs if isinstance(outs, (tuple, list)) else (outs,)
        o_refs = refs[n_in:]
        for o_ref, val in zip(o_refs[:n_row_out], outs[:n_row_out]):
            o_ref[...] = val.astype(o_ref.dtype)
        first = pl.program_id(1) == 0
        for o_ref, val in zip(o_refs[n_row_out:], outs[n_row_out:]):
            @pl.when(first)
            def _(o_ref=o_ref, val=val):
                o_ref[...] = val

            @pl.when(jnp.logical_not(first))
            def _(o_ref=o_ref, val=val):
                o_ref[...] += val

    def row_map(off):
        return lambda j, i: (i, j + off)

    def par_map(off):
        return lambda j, i: (0, j + off)

    return pl.pallas_call(
        body, name=name,
        grid=(cols // tc, s // ts),
        in_specs=[pl.BlockSpec((ts, tc), row_map(off)) for _, off in rows]
        + [pl.BlockSpec((p.shape[0], tc), par_map(off)) for p, off in pars],
        out_specs=[pl.BlockSpec((ts, tc), lambda j, i: (i, j)) for _ in row_outs]
        + [pl.BlockSpec((r, tc), lambda j, i: (0, j)) for r in red_rows],
        out_shape=[jax.ShapeDtypeStruct((s, cols), dt) for dt in row_outs]
        + [jax.ShapeDtypeStruct((r, cols), F32) for r in red_rows],
        compiler_params=_params("parallel", "arbitrary"),
    )(*[r for r, _ in rows], *[p for p, _ in pars])


def _rms(x, g):
    return x * lax.rsqrt(jnp.mean(x * x, axis=-1, keepdims=True) + RMS_EPS) * g


def _sigmoid(x):
    return jax.nn.sigmoid(x)


def _silu(x):
    return x * jax.nn.sigmoid(x)


def _gelu(x):
    return 0.5 * x * (1.0 + jnp.tanh(math.sqrt(2.0 / math.pi) * (x + 0.044715 * (x * x * x))))


def _softplus(x):
    return jnp.maximum(x, 0.0) + jnp.log1p(jnp.exp(-jnp.abs(x)))


def _split3(x):
    hi = x.astype(BF16)
    r1 = x - hi.astype(F32)
    mid = r1.astype(BF16)
    lo = (r1 - mid.astype(F32)).astype(BF16)
    return hi, mid, lo


def _xdot(x, t):
    return sum(jnp.dot(p, t, preferred_element_type=F32) for p in _split3(x))


def _xdot_l(t, x):
    return sum(jnp.dot(t, p, preferred_element_type=F32) for p in _split3(x))


_NT = (((1,), (1,)), ((), ()))
_TN = (((0,), (0,)), ((), ()))


def _dot(a, b, dims=None):
    if dims is None:
        return jnp.dot(a.astype(BF16), b.astype(BF16), preferred_element_type=F32)
    return lax.dot_general(a.astype(BF16), b.astype(BF16), dims, preferred_element_type=F32)


def _iota(shape, axis):
    return lax.broadcasted_iota(jnp.int32, shape, axis)


def _sb_tile(qb, kblk, kb, qi, row, col, upper, c_rem):
    z = lax.dot_general(qb, kblk, _NT, preferred_element_type=F32)
    mask = (col + kb * SB_TILE) < (row + qi * SB_TILE)
    soft = jnp.log1p(jnp.exp(-jnp.abs(z)))
    lbeta = jnp.minimum(z, 0.0) - soft
    l1m = jnp.where(mask, -jnp.maximum(z, 0.0) - soft, 0.0)
    rem = _xdot(l1m, upper) + c_rem
    w = jnp.where(mask, jnp.exp(lbeta + rem), 0.0)
    return z, mask, lbeta, l1m, w


def _sb_fwd(proj, n_heads, *, name):
    s = proj.shape[0]
    t = SB_TILE
    scale = HEAD_DIM ** -0.5

    def body(q_ref, k_ref, v_ref, o_ref):
        qi = pl.program_id(1)
        qb = (q_ref[...] * scale).astype(BF16)
        row, col = _iota((t, t), 0), _iota((t, t), 1)
        upper = (row > col).astype(BF16)

        def step(j, carry):
            acc, c_rem = carry
            kb = qi - j
            rows = pl.ds(pl.multiple_of(kb * t, t), t)
            kblk = k_ref[rows, :].astype(BF16)
            vblk = v_ref[rows, :].astype(BF16)
            _, _, _, l1m, w = _sb_tile(qb, kblk, kb, qi, row, col, upper, c_rem)
            acc = acc + jnp.dot(w.astype(BF16), vblk, preferred_element_type=F32)
            return acc, c_rem + jnp.sum(l1m, axis=1, keepdims=True)

        acc, _ = lax.fori_loop(0, qi + 1, step, (jnp.zeros((t, HEAD_DIM), F32), jnp.zeros((t, 1), F32)))
        o_ref[...] = acc

    return pl.pallas_call(
        body, name=name,
        grid=(n_heads, s // t),
        in_specs=[pl.BlockSpec((t, HEAD_DIM), lambda h, i: (i, h)),
                  pl.BlockSpec((s, HEAD_DIM), lambda h, i: (0, n_heads + h)),
                  pl.BlockSpec((s, HEAD_DIM), lambda h, i: (0, 2 * n_heads + h))],
        out_specs=pl.BlockSpec((t, HEAD_DIM), lambda h, i: (i, h)),
        out_shape=jax.ShapeDtypeStruct((s, n_heads * HEAD_DIM), F32),
        compiler_params=_params("parallel", "arbitrary"),
    )(proj, proj, proj)


def _sb_bwd(proj, dcat, n_heads, *, name):
    s = proj.shape[0]
    t = SB_TILE
    scale = HEAD_DIM ** -0.5

    def body(q_ref, k_ref, v_ref, do_ref, dq_ref, dk_ref, dv_ref, g_s, sig_s):
        qi = pl.program_id(1)

        @pl.when(qi == 0)
        def _():
            dk_ref[...] = jnp.zeros_like(dk_ref)
            dv_ref[...] = jnp.zeros_like(dv_ref)

        qb = (q_ref[...] * scale).astype(BF16)
        dob = do_ref[...].astype(BF16)
        row, col = _iota((t, t), 0), _iota((t, t), 1)
        upper = (row > col).astype(BF16)
        lower_incl = (row >= col).astype(BF16)

        def weights(j, carry):
            c_rem, g_all = carry
            kb = qi - j
            rows = pl.ds(pl.multiple_of(kb * t, t), t)
            kblk = k_ref[rows, :].astype(BF16)
            vblk = v_ref[rows, :].astype(BF16)
            _, _, lbeta, l1m, w = _sb_tile(qb, kblk, kb, qi, row, col, upper, c_rem)
            g = w * lax.dot_general(dob, vblk, _NT, preferred_element_type=F32)
            dv_ref[rows, :] += lax.dot_general(w.astype(BF16), dob, _TN, preferred_element_type=F32)
            g_s[kb] = g
            sig_s[kb] = jnp.exp(lbeta)
            return c_rem + jnp.sum(l1m, axis=1, keepdims=True), g_all + jnp.sum(g, axis=1, keepdims=True)

        zero_col = jnp.zeros((t, 1), F32)
        _, g_all = lax.fori_loop(0, qi + 1, weights, (zero_col, zero_col))

        def scores(j, carry):
            dq, c_g = carry
            kb = qi - j
            rows = pl.ds(pl.multiple_of(kb * t, t), t)
            g, sig = g_s[kb], sig_s[kb]
            mask = (col + kb * t) < (row + qi * t)
            g_before = g_all - (_xdot(g, lower_incl) + c_g)
            dz = jnp.where(mask, g * (1.0 - sig) - g_before * sig, 0.0).astype(BF16)
            dq = dq + jnp.dot(dz, k_ref[rows, :].astype(BF16), preferred_element_type=F32)
            dk_ref[rows, :] += lax.dot_general(dz, qb, _TN, preferred_element_type=F32)
            return dq, c_g + jnp.sum(g, axis=1, keepdims=True)

        dq, _ = lax.fori_loop(0, qi + 1, scores, (jnp.zeros((t, HEAD_DIM), F32), zero_col))
        dq_ref[...] = dq * scale

    width = n_heads * HEAD_DIM
    return pl.pallas_call(
        body, name=name,
        grid=(n_heads, s // t),
        in_specs=[pl.BlockSpec((t, HEAD_DIM), lambda h, i: (i, h)),
                  pl.BlockSpec((s, HEAD_DIM), lambda h, i: (0, n_heads + h)),
                  pl.BlockSpec((s, HEAD_DIM), lambda h, i: (0, 2 * n_heads + h)),
                  pl.BlockSpec((t, HEAD_DIM), lambda h, i: (i, h))],
        out_specs=[pl.BlockSpec((t, HEAD_DIM), lambda h, i: (i, h)),
                   pl.BlockSpec((s, HEAD_DIM), lambda h, i: (0, h)),
                   pl.BlockSpec((s, HEAD_DIM), lambda h, i: (0, h))],
        out_shape=[jax.ShapeDtypeStruct((s, width), F32)] * 3,
        scratch_shapes=[pltpu.VMEM((s // t, t, t), F32)] * 2,
        compiler_params=_params("parallel", "arbitrary"),
    )(proj, proj, proj, dcat)


def _hg_pre(hq, hf, logits):
    mx = jnp.max(logits, axis=0, keepdims=True)
    ex = jnp.exp(logits - mx)
    lb = ex[0:1, :] / jnp.sum(ex, axis=0, keepdims=True)
    f = lb + (1.0 - lb) * _sigmoid(hf)
    return _silu(hq), 1.0 - f, jnp.log(f)


def _hg_post(o, norm_g, hgate):
    return _rms(o, norm_g) * _silu(hgate)


def _hg_specs(s, n_heads, first_block):
    def at(group):
        return pl.BlockSpec((s, HEAD_DIM), lambda h: (0, first_block + group * n_heads + h))
    return [at(0), at(1), at(2), at(3)]


def _hg_fwd(proj, logits, norm_g, n_heads, *, name):
    s = proj.shape[0]
    hc = HG_CHUNK
    n_chunks = s // hc
    d = HEAD_DIM

    def body(lg_ref, ng_ref, hq_ref, hf_ref, hi_ref, hgt_ref, out_ref, oraw_ref, st_ref,
             q_s, k_s, lf_s, cum_s, qc_s, oc_s):
        q, k, lf = _hg_pre(hq_ref[...], hf_ref[...], lg_ref[...])
        q_s[...] = q
        k_s[...] = k
        lf_s[...] = lf
        tril = (_iota((hc, hc), 0) >= _iota((hc, hc), 1)).astype(BF16)
        srow = _iota((hc, d), 0)

        def chunk(ci, st):
            rows = pl.ds(pl.multiple_of(ci * hc, hc), hc)
            q, k, v = q_s[rows, :], k_s[rows, :], hi_ref[rows, :]
            cum = _xdot_l(tril, lf_s[rows, :])
            st_ref[0, ci] = st
            o_inter = _dot(q * jnp.exp(cum), st, _NT)
            cum_s[...] = cum
            qc_s[...] = q
            for t in range(hc):
                ng = (t // SUBLANES + 1) * SUBLANES
                e = jnp.where(srow[:ng] <= t, jnp.exp(cum_s[t:t + 1, :] - cum[:ng]), 0.0)
                sc = jnp.sum(qc_s[t:t + 1, :] * k[:ng] * e, axis=1, keepdims=True)
                oc_s[t:t + 1, :] = jnp.sum(sc * v[:ng], axis=0, keepdims=True)
            oraw_ref[rows, :] = o_inter + oc_s[...]
            last = cum_s[hc - 1:hc, :]
            return st * jnp.exp(last) + _dot(v, k * jnp.exp(last - cum), _TN)

        lax.fori_loop(0, n_chunks, chunk, jnp.zeros((d, d), F32))
        out_ref[...] = _hg_post(oraw_ref[...], ng_ref[...], hgt_ref[...]).astype(BF16)

    width = n_heads * d
    head_block = pl.BlockSpec((s, d), lambda h: (0, h))
    return pl.pallas_call(
        body, name=name,
        grid=(n_heads,),
        in_specs=[pl.BlockSpec((2, d), lambda h: (0, h)), pl.BlockSpec((1, d), lambda h: (0, 0))]
        + _hg_specs(s, n_heads, 3 * n_heads),
        out_specs=[head_block, head_block, pl.BlockSpec((1, n_chunks, d, d), lambda h: (h, 0, 0, 0))],
        out_shape=[jax.ShapeDtypeStruct((s, width), BF16), jax.ShapeDtypeStruct((s, width), F32),
                   jax.ShapeDtypeStruct((n_heads, n_chunks, d, d), F32)],
        scratch_shapes=[pltpu.VMEM((s, d), F32)] * 3 + [pltpu.VMEM((hc, d), F32)] * 3,
        compiler_params=_params("arbitrary"),
    )(logits, norm_g, proj, proj, proj, proj)


def _hg_bwd(proj, logits, norm_g, oraw, states, dcat, n_heads, *, name):
    s = proj.shape[0]
    hc = HG_CHUNK
    n_chunks = s // hc
    d = HEAD_DIM

    def body(lg_ref, ng_ref, hq_ref, hf_ref, hi_ref, hgt_ref, oraw_ref, st_ref, dout_ref,
             dhq_ref, dhf_ref, dhi_ref, dhgt_ref, dlg_ref, dng_ref,
             q_s, k_s, lf_s, do_s, dq_s, dk_s, dlf_s, cum_s, qc_s, doc_s, dqc_s, dkc_s, dvc_s):
        head = pl.program_id(0)
        (q, k, lf), pre_vjp = jax.vjp(_hg_pre, hq_ref[...], hf_ref[...], lg_ref[...])
        q_s[...] = q
        k_s[...] = k
        lf_s[...] = lf
        _, post_vjp = jax.vjp(_hg_post, oraw_ref[...], ng_ref[...], hgt_ref[...])
        do, dng, dhgt = post_vjp(dout_ref[...])
        do_s[...] = do
        dhgt_ref[...] = dhgt.astype(BF16)

        @pl.when(head == 0)
        def _():
            dng_ref[...] = dng

        @pl.when(head > 0)
        def _():
            dng_ref[...] += dng

        triu = (_iota((hc, hc), 0) <= _iota((hc, hc), 1)).astype(BF16)
        tril = (_iota((hc, hc), 0) >= _iota((hc, hc), 1)).astype(BF16)
        srow = _iota((hc, d), 0)

        def chunk(j, dst):
            ci = n_chunks - 1 - j
            rows = pl.ds(pl.multiple_of(ci * hc, hc), hc)
            q, k, v, do_c = q_s[rows, :], k_s[rows, :], hi_ref[rows, :], do_s[rows, :]
            cum = _xdot_l(tril, lf_s[rows, :])
            st = st_ref[0, ci]
            cum_s[...] = cum
            qc_s[...] = q
            doc_s[...] = do_c
            last = cum_s[hc - 1:hc, :]
            e_cum, e_last = jnp.exp(cum), jnp.exp(last - cum)
            dqc_s[...] = _dot(do_c, st) * e_cum
            dk_state = _dot(v, dst) * e_last
            dkc_s[...] = dk_state
            dvc_s[...] = _dot(k * e_last, dst, _NT)
            d_last = (jnp.sum(dst * st, axis=0, keepdims=True) * jnp.exp(last)
                      + jnp.sum(k * dk_state, axis=0, keepdims=True))
            for t in range(hc):
                ng = (t // SUBLANES + 1) * SUBLANES
                qt, dot_ = qc_s[t:t + 1, :], doc_s[t:t + 1, :]
                e = jnp.where(srow[:ng] <= t, jnp.exp(cum_s[t:t + 1, :] - cum[:ng]), 0.0)
                ke = k[:ng] * e
                d_a = jnp.sum(dot_ * v[:ng], axis=1, keepdims=True)
                dqc_s[t:t + 1, :] += jnp.sum(d_a * ke, axis=0, keepdims=True)
                dkc_s[0:ng, :] += d_a * (qt * e)
                dvc_s[0:ng, :] += jnp.sum(qt * ke, axis=1, keepdims=True) * dot_
            dq, dk = dqc_s[...], dkc_s[...]
            d_b = q * dq - k * dk
            dq_s[rows, :] = dq
            dk_s[rows, :] = dk
            dhi_ref[rows, :] = dvc_s[...].astype(BF16)
            dlf_s[rows, :] = _xdot_l(triu, d_b) + d_last
            return dst * jnp.exp(last) + _dot(do_c, q * e_cum, _TN)

        lax.fori_loop(0, n_chunks, chunk, jnp.zeros((d, d), F32))
        dhq, dhf, dlg = pre_vjp((dq_s[...], dk_s[...], dlf_s[...]))
        dhq_ref[...] = dhq.astype(BF16)
        dhf_ref[...] = dhf.astype(BF16)
        dlg_ref[...] = dlg

    width = n_heads * d
    head_block = pl.BlockSpec((s, d), lambda h: (0, h))
    return pl.pallas_call(
        body, name=name,
        grid=(n_heads,),
        in_specs=[pl.BlockSpec((2, d), lambda h: (0, h)), pl.BlockSpec((1, d), lambda h: (0, 0))]
        + _hg_specs(s, n_heads, 3 * n_heads)
        + [head_block, pl.BlockSpec((1, n_chunks, d, d), lambda h: (h, 0, 0, 0)),
           pl.BlockSpec((s, d), lambda h: (0, n_heads + h))],
        out_specs=[head_block] * 4 + [pl.BlockSpec((2, d), lambda h: (0, h)), pl.BlockSpec((1, d), lambda h: (0, 0))],
        out_shape=[jax.ShapeDtypeStruct((s, width), BF16)] * 4
        + [jax.ShapeDtypeStruct((2, width), F32), jax.ShapeDtypeStruct((1, d), F32)],
        scratch_shapes=[pltpu.VMEM((s, d), F32)] * 7 + [pltpu.VMEM((hc, d), F32)] * 6,
        compiler_params=_params("arbitrary"),
    )(logits, norm_g, proj, proj, proj, proj, oraw, states, dcat)


def _shift_down(x, n, srow):
    if n == 0:
        return x
    return jnp.where(srow >= n, pltpu.roll(x, n, 0), 0.0)


def _shift_up(x, n, srow):
    if n == 0:
        return x
    s = x.shape[0]
    return jnp.where(srow < s - n, pltpu.roll(x, s - n, 0), 0.0)


def _rg_gates_fwd(proj, conv_w, conv_b, wa, ba, wx, bx, *, name):
    s = proj.shape[0]
    nb = wa.shape[0]
    bw = RG_BLOCK

    def body(xb_ref, cw_ref, cb_ref, wa_ref, ba_ref, wx_ref, bx_ref, xc_ref, ra_ref, ix_ref):
        x = xb_ref[...]
        srow = _iota((s, bw), 0)
        cw = cw_ref[...]
        xc = cb_ref[...] + cw[0:1, :] * x
        for tap in range(1, CONV_TAPS):
            xc = xc + cw[tap:tap + 1, :] * _shift_down(x, tap, srow)
        xc_ref[...] = xc
        ra_ref[...] = _dot(xc, wa_ref[0]) + ba_ref[0]
        ix_ref[...] = _dot(xc, wx_ref[0]) + bx_ref[0]

    col = pl.BlockSpec((s, bw), lambda n: (0, n))
    vec = lambda r: pl.BlockSpec((r, bw), lambda n: (0, n))
    mat = pl.BlockSpec((1, bw, bw), lambda n: (n, 0, 0))
    bias = pl.BlockSpec((1, 1, bw), lambda n: (n, 0, 0))
    return pl.pallas_call(
        body, name=name,
        grid=(nb,),
        in_specs=[pl.BlockSpec((s, bw), lambda n: (0, nb + n)), vec(CONV_TAPS), vec(1), mat, bias, mat, bias],
        out_specs=[col] * 3,
        out_shape=[jax.ShapeDtypeStruct((s, nb * bw), F32)] * 3,
        compiler_params=_params("parallel"),
    )(proj, conv_w, conv_b, wa, ba, wx, bx)


def _rg_au(ra, ix, xc, lam, first_row):
    log_a = -RG_C * _sigmoid(ra) * _softplus(-lam)
    th = jnp.tanh(log_a)
    one_minus_a2 = -2.0 * th / (1.0 - th)
    mult = jnp.where(first_row, 1.0, jnp.sqrt(one_minus_a2))
    return jnp.exp(log_a), xc * _sigmoid(ix) * mult


def _rg_out(gate, hs):
    return _gelu(gate) * hs


def _scan_rows(n_groups, reverse, group_fn, init):
    def group(gi, carry):
        g = (n_groups - 1 - gi) if reverse else gi
        return group_fn(pl.multiple_of(g * SUBLANES, SUBLANES), carry)
    return lax.fori_loop(0, n_groups, group, init)


def _rg_scan_fwd(proj, xc, ra, ix, lam, *, name):
    s, width = xc.shape
    tc = LANES

    def body(gate_ref, xc_ref, ra_ref, ix_ref, lam_ref, hs_ref, gact_ref, a_s, u_s):
        first_row = _iota((s, tc), 0) == 0
        a, u = _rg_au(ra_ref[...], ix_ref[...], xc_ref[...], lam_ref[...], first_row)
        a_s[...] = a
        u_s[...] = u
        r8 = _iota((SUBLANES, tc), 0)

        def rows(r0, h):
            ag, ug = a_s[pl.ds(r0, SUBLANES), :], u_s[pl.ds(r0, SUBLANES), :]
            tile = jnp.zeros((SUBLANES, tc), F32)
            for r in range(SUBLANES):
                h = ag[r:r + 1, :] * h + ug[r:r + 1, :]
                tile = jnp.where(r8 == r, h, tile)
            hs_ref[pl.ds(r0, SUBLANES), :] = tile
            return h

        _scan_rows(s // SUBLANES, False, rows, jnp.zeros((1, tc), F32))
        gact_ref[...] = _rg_out(gate_ref[...], hs_ref[...]).astype(BF16)

    col = pl.BlockSpec((s, tc), lambda n: (0, n))
    return pl.pallas_call(
        body, name=name,
        grid=(width // tc,),
        in_specs=[col, col, col, col, pl.BlockSpec((1, tc), lambda n: (0, n))],
        out_specs=[col, col],
        out_shape=[jax.ShapeDtypeStruct((s, width), F32), jax.ShapeDtypeStruct((s, width), BF16)],
        scratch_shapes=[pltpu.VMEM((s, tc), F32)] * 2,
        compiler_params=_params("parallel"),
    )(proj, xc, ra, ix, lam)


def _rg_scan_bwd(dgo, proj, hs, xc, ra, ix, lam, *, name):
    s, width = xc.shape
    tc = LANES

    def body(dgo_ref, gate_ref, hs_ref, xc_ref, ra_ref, ix_ref, lam_ref,
             dgate_ref, dra_ref, dix_ref, dxc_ref, dlam_ref, a_s, dh_s, g_s):
        srow = _iota((s, tc), 0)
        hs = hs_ref[...]
        _, out_vjp = jax.vjp(_rg_out, gate_ref[...], hs)
        dgate, dh = out_vjp(dgo_ref[...])
        dgate_ref[...] = dgate.astype(BF16)
        au = functools.partial(_rg_au, first_row=srow == 0)
        (a, _), au_vjp = jax.vjp(au, ra_ref[...], ix_ref[...], xc_ref[...], lam_ref[...])
        a_s[...] = a
        dh_s[...] = dh
        r8 = _iota((SUBLANES, tc), 0)

        def rows(r0, carry):
            g, a_next = carry
            ag, dg = a_s[pl.ds(r0, SUBLANES), :], dh_s[pl.ds(r0, SUBLANES), :]
            tile = jnp.zeros((SUBLANES, tc), F32)
            for r in reversed(range(SUBLANES)):
                g = dg[r:r + 1, :] + a_next * g
                a_next = ag[r:r + 1, :]
                tile = jnp.where(r8 == r, g, tile)
            g_s[pl.ds(r0, SUBLANES), :] = tile
            return g, a_next

        zero = jnp.zeros((1, tc), F32)
        _scan_rows(s // SUBLANES, True, rows, (zero, zero))
        g = g_s[...]
        dra, dix, dxc, dlam = au_vjp((g * _shift_down(hs, 1, srow), g))
        dra_ref[...] = dra.astype(BF16)
        dix_ref[...] = dix.astype(BF16)
        dxc_ref[...] = dxc
        dlam_ref[...] = dlam

    col = pl.BlockSpec((s, tc), lambda n: (0, n))
    vec = pl.BlockSpec((1, tc), lambda n: (0, n))
    return pl.pallas_call(
        body, name=name,
        grid=(width // tc,),
        in_specs=[col] * 6 + [vec],
        out_specs=[col] * 4 + [vec],
        out_shape=[jax.ShapeDtypeStruct((s, width), BF16)] * 3
        + [jax.ShapeDtypeStruct((s, width), F32), jax.ShapeDtypeStruct((1, width), F32)],
        scratch_shapes=[pltpu.VMEM((s, tc), F32)] * 3,
        compiler_params=_params("parallel"),
    )(dgo, proj, hs, xc, ra, ix, lam)


def _rg_gates_bwd(dra, dix, dxc1, xc, proj, conv_w, wa, wx, *, name):
    s = proj.shape[0]
    nb = wa.shape[0]
    bw = RG_BLOCK

    def body(dra_ref, dix_ref, dxc_ref, xc_ref, xb_ref, cw_ref, wa_ref, wx_ref,
             dxb_ref, dcw_ref, dcb_ref, dwa_ref, dba_ref, dwx_ref, dbx_ref):
        dra, dix = dra_ref[...], dix_ref[...]
        xc_t = xc_ref[...].T.astype(BF16)
        dwa_ref[0] = jnp.dot(xc_t, dra, preferred_element_type=F32)
        dwx_ref[0] = jnp.dot(xc_t, dix, preferred_element_type=F32)
        dba_ref[0] = jnp.sum(dra.astype(F32), axis=0, keepdims=True)
        dbx_ref[0] = jnp.sum(dix.astype(F32), axis=0, keepdims=True)
        dxc = dxc_ref[...] + _dot(dra, wa_ref[0], _NT) + _dot(dix, wx_ref[0], _NT)
        srow = _iota((s, bw), 0)
        x = xb_ref[...]
        cw = cw_ref[...]
        dx = cw[0:1, :] * dxc
        dcw = [jnp.sum(dxc * x, axis=0, keepdims=True)]
        for tap in range(1, CONV_TAPS):
            dx = dx + cw[tap:tap + 1, :] * _shift_up(dxc, tap, srow)
            dcw.append(jnp.sum(dxc * _shift_down(x, tap, srow), axis=0, keepdims=True))
        dxb_ref[...] = dx.astype(BF16)
        r4 = _iota((CONV_TAPS, bw), 0)
        acc = jnp.zeros((CONV_TAPS, bw), F32)
        for tap in range(CONV_TAPS):
            acc = jnp.where(r4 == tap, dcw[tap], acc)
        dcw_ref[...] = acc
        dcb_ref[...] = jnp.sum(dxc, axis=0, keepdims=True)

    col = pl.BlockSpec((s, bw), lambda n: (0, n))
    vec = lambda r: pl.BlockSpec((r, bw), lambda n: (0, n))
    mat = pl.BlockSpec((1, bw, bw), lambda n: (n, 0, 0))
    bias = pl.BlockSpec((1, 1, bw), lambda n: (n, 0, 0))
    width = nb * bw
    return pl.pallas_call(
        body, name=name,
        grid=(nb,),
        in_specs=[col, col, col, col, pl.BlockSpec((s, bw), lambda n: (0, nb + n)), vec(CONV_TAPS), mat, mat],
        out_specs=[col, vec(CONV_TAPS), vec(1), mat, bias, mat, bias],
        out_shape=[jax.ShapeDtypeStruct((s, width), BF16), jax.ShapeDtypeStruct((CONV_TAPS, width), F32),
                   jax.ShapeDtypeStruct((1, width), F32), jax.ShapeDtypeStruct((nb, bw, bw), F32),
                   jax.ShapeDtypeStruct((nb, 1, bw), F32), jax.ShapeDtypeStruct((nb, bw, bw), F32),
                   jax.ShapeDtypeStruct((nb, 1, bw), F32)],
        compiler_params=_params("parallel"),
    )(dra, dix, dxc1, xc, proj, conv_w, wa, wx)


_HBM = pl.BlockSpec(memory_space=pltpu.HBM)
_FLIPS = ((0, 0, 1), (1, 0, 0), (0, 1, 0), (1, 1, 0))
_ALL_FLIPS = tuple((a, b, c) for a in (0, 1) for b in (0, 1) for c in (0, 1))[1:]


def _flip(pos, f):
    return tuple(1 - p if b else p for p, b in zip(pos, f))


def _dev_index(pos):
    return 4 * pos[0] + 2 * pos[1] + pos[2]


def _block(ref, idx, cols):
    if not cols:
        return ref.at[idx]
    n = ref.shape[-1] // N_DEV
    start = pl.multiple_of(idx * n, LANES)
    return ref.at[(slice(None),) * (len(ref.shape) - 1) + (pl.ds(start, n),)]


def _all_gather(xs, *, name, cols=False):
    n_arr = len(xs)

    def body(*refs):
        x_refs, out_refs = refs[:n_arr], refs[n_arr:2 * n_arr]
        send_sems, recv_sems, local_sems = refs[2 * n_arr:]
        me = (lax.axis_index("x"), lax.axis_index("y"), lax.axis_index("c"))
        sibling = _flip(me, _FLIPS[0])
        chips = [_flip(me, f) for f in _FLIPS[1:]]

        def copy(a, k, block, to, src=None):
            dst = _block(out_refs[a], _dev_index(block), cols)
            return pltpu.make_async_remote_copy(
                src_ref=dst if src is None else src, dst_ref=dst,
                send_sem=send_sems.at[7 * a + k], recv_sem=recv_sems.at[7 * a + k],
                device_id=to, device_id_type=pl.DeviceIdType.MESH)

        mine = [pltpu.make_async_copy(x_refs[a], _block(out_refs[a], _dev_index(me), cols), local_sems.at[a])
                for a in range(n_arr)]
        for cp in mine:
            cp.start()
        first = []
        for a in range(n_arr):
            first.append(copy(a, 0, me, sibling, src=x_refs[a]))
            first += [copy(a, 1 + j, me, chip, src=x_refs[a]) for j, chip in enumerate(chips)]
        for cp in first:
            cp.start()
        passed = []
        for j, chip in enumerate(chips):
            for a in range(n_arr):
                copy(a, 1 + j, chip, me).wait_recv()
                fwd = copy(a, 4 + j, chip, sibling)
                fwd.start()
                passed.append(fwd)
        for a in range(n_arr):
            copy(a, 0, sibling, me).wait_recv()
            for j, chip in enumerate(chips):
                copy(a, 4 + j, _flip(chip, _FLIPS[0]), me).wait_recv()
        for cp in first + passed:
            cp.wait_send()
        for cp in mine:
            cp.wait()

    def out_shape(x):
        shape = x.shape[:-1] + (N_DEV * x.shape[-1],) if cols else (N_DEV,) + x.shape
        return jax.ShapeDtypeStruct(shape, x.dtype)

    return pl.pallas_call(
        body, name=name,
        in_specs=[_HBM] * n_arr, out_specs=[_HBM] * n_arr,
        out_shape=[out_shape(x) for x in xs],
        scratch_shapes=[pltpu.SemaphoreType.DMA((7 * n_arr,)), pltpu.SemaphoreType.DMA((7 * n_arr,)),
                        pltpu.SemaphoreType.DMA((n_arr,))],
    )(*xs)


def _exchange(ps, *, name, cols=False):
    n_arr = len(ps)
    blk = ps[0].shape[:-1] + (ps[0].shape[-1] // N_DEV,) if cols else ps[0].shape[1:]

    def body(*refs):
        p_refs, out_ref = refs[:n_arr], refs[n_arr]
        send_sems, recv_sems, local_sems = refs[n_arr + 1:]
        me = (lax.axis_index("x"), lax.axis_index("y"), lax.axis_index("c"))
        me_idx = _dev_index(me)
        own = [pltpu.make_async_copy(_block(p_refs[a], me_idx, cols), out_ref.at[me_idx, a], local_sems.at[a])
               for a in range(n_arr)]
        for cp in own:
            cp.start()
        sends = []
        for k, f in enumerate(_ALL_FLIPS):
            peer = _flip(me, f)
            for a in range(n_arr):
                cp = pltpu.make_async_remote_copy(
                    src_ref=_block(p_refs[a], _dev_index(peer), cols), dst_ref=out_ref.at[me_idx, a],
                    send_sem=send_sems.at[7 * a + k], recv_sem=recv_sems.at[7 * a + k],
                    device_id=peer, device_id_type=pl.DeviceIdType.MESH)
                cp.start()
                sends.append(cp)
        for cp in sends:
            cp.wait()
        for cp in own:
            cp.wait()

    return pl.pallas_call(
        body, name=name,
        in_specs=[_HBM] * n_arr, out_specs=_HBM,
        out_shape=jax.ShapeDtypeStruct((N_DEV, n_arr) + blk, ps[0].dtype),
        scratch_shapes=[pltpu.SemaphoreType.DMA((7 * n_arr,)), pltpu.SemaphoreType.DMA((7 * n_arr,)),
                        pltpu.SemaphoreType.DMA((n_arr,))],
    )(*ps)


_SEM = pl.BlockSpec(memory_space=pltpu.SEMAPHORE)
_ANY = pl.BlockSpec(memory_space=pl.ANY)
_N_PEERS = N_DEV - 1


def _hbm(x):
    return pltpu.with_memory_space_constraint(x, pltpu.HBM)


def _spread_copies(src_refs, land_refs, send_sems, recv_sems, local_sems, src_block, dst_block):
    me = (lax.axis_index("x"), lax.axis_index("y"), lax.axis_index("c"))
    me_idx = _dev_index(me)
    local, remote = [], []
    for a, src in enumerate(src_refs):
        local.append(pltpu.make_async_copy(src_block(a, src, me_idx), dst_block(a, land_refs, me_idx), local_sems.at[a]))
    for k, f in enumerate(_ALL_FLIPS):
        peer = _flip(me, f)
        for a, src in enumerate(src_refs):
            remote.append(pltpu.make_async_remote_copy(
                src_ref=src_block(a, src, _dev_index(peer)), dst_ref=dst_block(a, land_refs, me_idx),
                send_sem=send_sems.at[_N_PEERS * a + k], recv_sem=recv_sems.at[_N_PEERS * a + k],
                device_id=peer, device_id_type=pl.DeviceIdType.MESH))
    return local, remote


def _spread_start(srcs, land_shapes, src_block, dst_block, *, name):
    ns, nl = len(srcs), len(land_shapes)

    def body(*refs):
        src_refs, land_refs = refs[:ns], refs[ns:ns + nl]
        send_sems, recv_sems, local_sems = refs[ns + nl:ns + nl + 3]
        local, remote = _spread_copies(src_refs, land_refs, send_sems, recv_sems, local_sems, src_block, dst_block)
        for cp in local + remote:
            cp.start()

    lands = [_hbm(lax.empty(shape, dtype)) for shape, dtype in land_shapes]
    out = pl.pallas_call(
        body, name=name,
        in_specs=[_HBM] * (ns + nl),
        out_specs=[_SEM] * 3 + [_HBM] * (ns + nl),
        out_shape=[pltpu.SemaphoreType.DMA((_N_PEERS * ns,)), pltpu.SemaphoreType.DMA((_N_PEERS * ns,)),
                   pltpu.SemaphoreType.DMA((ns,))]
        + [pltpu.HBM(x.shape, x.dtype) for x in srcs] + [pltpu.HBM(shape, dtype) for shape, dtype in land_shapes],
        input_output_aliases={i: 3 + i for i in range(ns + nl)},
        compiler_params=pltpu.CompilerParams(has_side_effects=pltpu.SideEffectType.DATAFLOW_SIDE_EFFECTING),
    )(*[_hbm(x) for x in srcs], *lands)
    return dict(sems=out[:3], srcs=out[3:3 + ns], lands=out[3 + ns:], src_block=src_block, dst_block=dst_block)


def _spread_wait(handle, after, *, name):
    ns, nl = len(handle["srcs"]), len(handle["lands"])

    def body(*refs):
        src_refs, land_refs = refs[:ns], refs[ns:ns + nl]
        send_sems, recv_sems, local_sems = refs[ns + nl:ns + nl + 3]
        local, remote = _spread_copies(src_refs, land_refs, send_sems, recv_sems, local_sems,
                                       handle["src_block"], handle["dst_block"])
        for cp in local:
            cp.wait()
        for cp in remote:
            cp.wait_send()
            cp.wait_recv()

    out = pl.pallas_call(
        body, name=name,
        in_specs=[_HBM] * (ns + nl) + [_SEM] * 3 + [_ANY],
        out_specs=[_HBM] * (ns + nl),
        out_shape=[pltpu.HBM(x.shape, x.dtype) for x in handle["srcs"] + handle["lands"]],
        input_output_aliases={i: i for i in range(ns + nl)},
        compiler_params=pltpu.CompilerParams(has_side_effects=pltpu.SideEffectType.DATAFLOW_SIDE_EFFECTING),
    )(*handle["srcs"], *handle["lands"], *handle["sems"], after)
    return list(out[ns:])


def _gather_start(x, *, name, cols=False):
    shape = x.shape[:-1] + (N_DEV * x.shape[-1],) if cols else (N_DEV,) + x.shape
    return _spread_start([x], [(shape, x.dtype)], lambda a, ref, d: ref,
                         lambda a, lands, d: _block(lands[0], d, cols), name=name)


def _exchange_start(ps, *, name, cols=False):
    blk = ps[0].shape[:-1] + (ps[0].shape[-1] // N_DEV,) if cols else ps[0].shape[1:]
    return _spread_start(ps, [((N_DEV, len(ps)) + blk, ps[0].dtype)], lambda a, ref, d: _block(ref, d, cols),
                         lambda a, lands, d: lands[0].at[d, a], name=name)


def _adamw(parts, w, m, v, *, name, layer=0, prev=None):
    n_rows, c = w.shape
    r = parts.shape[1]
    row_bytes = c * (N_DEV * parts.dtype.itemsize + 7 * 4) * 2
    tr = r
    for cand in (512, 256, 128, 64, 32, 16):
        if r % cand == 0 and cand * row_bytes <= 40 * 1024 * 1024:
            tr = cand
            break
    c1 = 1.0 - ADAM_B1 ** ADAM_STEP
    c2 = 1.0 - ADAM_B2 ** ADAM_STEP

    def body(p_ref, w_ref, m_ref, v_ref, *rest):
        g_ref, d_ref, nm_ref, nv_ref = rest[-4:]
        g = p_ref[0].astype(F32)
        for j in range(1, N_DEV):
            g = g + p_ref[j].astype(F32)
        nm = ADAM_B1 * m_ref[...] + (1.0 - ADAM_B1) * g
        nv = ADAM_B2 * v_ref[...] + (1.0 - ADAM_B2) * (g * g)
        g_ref[...] = g
        nm_ref[...] = nm
        nv_ref[...] = nv
        d_ref[...] = -ADAM_LR * ((nm / c1) / (jnp.sqrt(nv / c2) + ADAM_EPS) + ADAM_WD * w_ref[...])

    off = layer * (r // tr)
    blk = pl.BlockSpec((tr, c), lambda i: (i + off, 0))
    prev = list(prev) if prev is not None else []
    return pl.pallas_call(
        body, name=name,
        grid=(r // tr,),
        in_specs=[pl.BlockSpec((N_DEV, tr, c), lambda i: (0, i, 0)), blk, blk, blk] + [_ANY] * len(prev),
        out_specs=[blk] * 4,
        out_shape=[jax.ShapeDtypeStruct((n_rows, c), F32)] * 4,
        input_output_aliases={4 + j: j for j in range(len(prev))},
        compiler_params=_params("parallel"),
    )(parts, w, m, v, *prev)


_TN_CANDS = (512, 256, 128)
_TK_CANDS = (2048, 1536, 1408, 1024, 768, 512, 256, 128)


def _nn(a, b, name, out_dtype=F32):
    return _mm(a, b, "nn", name=name, out_dtype=out_dtype, tm=a.shape[0], tn=_pick(b.shape[1], _TN_CANDS),
               tk=_pick(a.shape[1], _TK_CANDS))


def _nt(a, b, name, out_dtype=F32):
    return _mm(a, b, "nt", name=name, out_dtype=out_dtype, tm=a.shape[0], tn=_pick(b.shape[0], _TN_CANDS),
               tk=_pick(a.shape[1], _TK_CANDS))


def _tn(a, b, name, out_dtype=BF16):
    return _mm(a, b, "tn", name=name, out_dtype=out_dtype, tm=_pick(a.shape[1], _TN_CANDS),
               tn=_pick(b.shape[1], (1024,) + _TN_CANDS))


def _local_step(x, p, target, rep, weight, emit, n_heads, ff):
    s, d = x.shape
    depth = p.shape[0]
    ff_tc = _pick(ff, (512, 256, 128))
    grads = {}
    rep_grads = {k: [None] * depth for k in ("mix_pre_g", "mix_post_g", "ffn_pre_g", "ffn_post_g", "ple_norm_g")}

    def gain(name, i):
        return rep[name][i:i + 1]

    saved = []
    h = x
    for i in range(depth):
        sv = {"h": h}
        n1, = _rowcall(f"pre_norm{i}", lambda hh, g: _rms(hh, g), [h], [gain("mix_pre_g", i)], [BF16], cols=d)
        sv["n1"] = n1
        if i % 2 == 0:
            proj = _nn(n1, weight("w_in_even", 0, n1), f"in_even{i}")
            a_out = _sb_fwd(proj, n_heads, name=f"sb_fwd{i}")
            b_out, oraw, states = _hg_fwd(proj, rep["hg_lb_logits"], rep["hg_norm_g"], n_heads, name=f"hg_fwd{i}")
            cat = jnp.concatenate([a_out.astype(BF16), b_out], axis=1)
            m = _nn(cat, weight("w_out_even", 0, cat), f"out_even{i}")
            sv.update(proj=proj, oraw=oraw, states=states, cat=cat)
        else:
            proj = _nn(n1, weight("w_in_odd", 0, n1), f"in_odd{i}")
            sm = {k: weight(k, 0, proj) for k in _SMALL}
            xc, ra, ix = _rg_gates_fwd(proj, sm["conv_w"], sm["conv_b"], sm["rg_wa"], sm["rg_ba"],
                                       sm["rg_wx"], sm["rg_bx"], name=f"rg_gates_fwd{i}")
            hs, gact = _rg_scan_fwd(proj, xc, ra, ix, sm["rg_lambda"], name=f"rg_scan_fwd{i}")
            m = _nn(gact, weight("w_out_odd", 0, gact), f"out_odd{i}")
            sv.update(proj=proj, xc=xc, ra=ra, ix=ix, hs=hs, gact=gact, sm=sm)

        def post_mix(hh, mm, g_post, g_pre):
            h1 = hh + _rms(mm, g_post)
            return h1, _rms(h1, g_pre)

        h1, n2 = _rowcall(f"post_mix{i}", post_mix, [h, m], [gain("mix_post_g", i), gain("ffn_pre_g", i)],
                          [F32, BF16], cols=d)
        gu = _nn(n2, weight("w_gate_up", i, n2), f"gate_up{i}")
        act, = _rowcall(f"swiglu{i}", lambda g, u: _silu(g) * u, [(gu, 0), (gu, ff // ff_tc)], [], [BF16],
                        cols=ff, tc=ff_tc)
        f = _nn(act, weight("w_down", i, act), f"down{i}")

        def post_ffn(hh, ff_out, g_post):
            h2 = hh + _rms(ff_out, g_post)
            return h2, h2

        h2, h2b = _rowcall(f"post_ffn{i}", post_ffn, [h1, f], [gain("ffn_post_g", i)], [F32, BF16], cols=d)
        e = _nn(p[i], weight("w_ple_up", i, h2b), f"ple_up{i}")
        gl = _nn(h2b, weight("w_ple_gate", i, h2b), f"ple_gate{i}")
        h3, = _rowcall(f"ple{i}", lambda hh, a, b, g: hh + _rms(_sigmoid(a) * b, g), [h2, gl, e],
                       [gain("ple_norm_g", i)], [F32], cols=d)
        sv.update(m=m, h1=h1, n2=n2, gu=gu, act=act, f=f, h2b=h2b, e=e, gl=gl)
        saved.append(sv)
        h = h3

    def loss_fn(y, t):
        err = y - t
        return err * (1.0 / d), jnp.sum(err * err, axis=0, keepdims=True) * (0.5 / d)

    dh, loss_cols = _rowcall("loss", loss_fn, [h, target], [], [F32], red_rows=(1,), cols=d)

    for i in reversed(range(depth)):
        sv = saved[i]

        def ple_bwd(dy, a, b, g):
            _, vjp = jax.vjp(lambda a_, b_, g_: _rms(_sigmoid(a_) * b_, g_), a, b, g)
            return vjp(dy)

        dgl, de, rep_grads["ple_norm_g"][i] = _rowcall(
            f"ple_bwd{i}", ple_bwd, [dh, sv["gl"], sv["e"]], [gain("ple_norm_g", i)], [BF16, BF16],
            red_rows=(1,), cols=d)
        emit("w_ple_up", i, _tn(p[i], de, f"d_ple_up{i}"))
        emit("w_ple_gate", i, _tn(sv["h2b"], dgl, f"d_ple_gate{i}"))
        dh2_ple = _nt(dgl, weight("w_ple_gate", i, dgl), f"dx_ple_gate{i}")

        def post_ffn_bwd(dy, dx, ff_out, g):
            dh2 = dy + dx
            _, vjp = jax.vjp(_rms, ff_out, g)
            df, dg = vjp(dh2)
            return dh2, df, dg

        dh2, df, rep_grads["ffn_post_g"][i] = _rowcall(
            f"post_ffn_bwd{i}", post_ffn_bwd, [dh, dh2_ple, sv["f"]], [gain("ffn_post_g", i)], [F32, BF16],
            red_rows=(1,), cols=d)
        emit("w_down", i, _tn(sv["act"], df, f"d_down{i}"))
        dact = _nt(df, weight("w_down", i, df), f"dx_down{i}")

        def swiglu_bwd(g, u, dy):
            _, vjp = jax.vjp(lambda g_, u_: _silu(g_) * u_, g, u)
            return vjp(dy)

        dg_, du_ = _rowcall(f"swiglu_bwd{i}", swiglu_bwd, [(sv["gu"], 0), (sv["gu"], ff // ff_tc), dact], [],
                            [BF16, BF16], cols=ff, tc=ff_tc)
        dgu = jnp.concatenate([dg_, du_], axis=1)
        emit("w_gate_up", i, _tn(sv["n2"], dgu, f"d_gate_up{i}"))
        dn2 = _nt(dgu, weight("w_gate_up", i, dgu), f"dx_gate_up{i}")

        def post_mix_bwd(dy, dn, h1, mm, g_post, g_pre):
            _, vjp_pre = jax.vjp(_rms, h1, g_pre)
            dh1_n, dg_pre = vjp_pre(dn)
            dh1 = dy + dh1_n
            _, vjp_post = jax.vjp(_rms, mm, g_post)
            dm, dg_post = vjp_post(dh1)
            return dh1, dm, dg_pre, dg_post

        dh1, dm, rep_grads["ffn_pre_g"][i], rep_grads["mix_post_g"][i] = _rowcall(
            f"post_mix_bwd{i}", post_mix_bwd, [dh2, dn2, sv["h1"], sv["m"]],
            [gain("mix_post_g", i), gain("ffn_pre_g", i)], [F32, BF16], red_rows=(1, 1), cols=d)

        if i % 2 == 0:
            emit("w_out_even", 0, _tn(sv["cat"], dm, f"d_out_even{i}"))
            dcat = _nt(dm, weight("w_out_even", 0, dm), f"dx_out_even{i}")
            dq, dk, dv = _sb_bwd(sv["proj"], dcat, n_heads, name=f"sb_bwd{i}")
            dhq, dhf, dhi, dhg, grads["hg_lb_logits"], grads["hg_norm_g"] = _hg_bwd(
                sv["proj"], rep["hg_lb_logits"], rep["hg_norm_g"], sv["oraw"], sv["states"], dcat, n_heads,
                name=f"hg_bwd{i}")
            dproj = jnp.concatenate([dq.astype(BF16), dk.astype(BF16), dv.astype(BF16), dhq, dhf, dhi, dhg], axis=1)
            emit("w_in_even", 0, _tn(sv["n1"], dproj, f"d_in_even{i}"))
            dn1 = _nt(dproj, weight("w_in_even", 0, dproj), f"dx_in_even{i}")
        else:
            sm = sv["sm"]
            emit("w_out_odd", 0, _tn(sv["gact"], dm, f"d_out_odd{i}"))
            dgo = _nt(dm, weight("w_out_odd", 0, dm), f"dx_out_odd{i}")
            dgate, dra, dix, dxc1, grads["rg_lambda"] = _rg_scan_bwd(
                dgo, sv["proj"], sv["hs"], sv["xc"], sv["ra"], sv["ix"], sm["rg_lambda"], name=f"rg_scan_bwd{i}")
            (dxb, grads["conv_w"], grads["conv_b"], grads["rg_wa"], grads["rg_ba"], grads["rg_wx"],
             grads["rg_bx"]) = _rg_gates_bwd(dra, dix, dxc1, sv["xc"], sv["proj"], sm["conv_w"], sm["rg_wa"],
                                            sm["rg_wx"], name=f"rg_gates_bwd{i}")
            dproj = jnp.concatenate([dgate, dxb], axis=1)
            emit("w_in_odd", 0, _tn(sv["n1"], dproj, f"d_in_odd{i}"))
            dn1 = _nt(dproj, weight("w_in_odd", 0, dproj), f"dx_in_odd{i}")

        def pre_norm_bwd(dy, dn, hh, g):
            _, vjp = jax.vjp(_rms, hh, g)
            dx, dg = vjp(dn)
            return dy + dx, dg

        dh, rep_grads["mix_pre_g"][i] = _rowcall(
            f"pre_norm_bwd{i}", pre_norm_bwd, [dh1, dn1, sv["h"]], [gain("mix_pre_g", i)], [F32],
            red_rows=(1,), cols=d)

    for k, rows in rep_grads.items():
        grads[k] = jnp.concatenate(rows, axis=0)
    return loss_cols, dh, grads


_WEIGHTS = ("mix_pre_g", "mix_post_g", "ffn_pre_g", "ffn_post_g", "ple_norm_g", "w_in_even", "w_out_even",
            "hg_lb_logits", "hg_norm_g", "w_in_odd", "conv_w", "conv_b", "rg_wa", "rg_ba", "rg_wx", "rg_bx",
            "rg_lambda", "w_out_odd", "w_gate_up", "w_down", "w_ple_up", "w_ple_gate")
_REPLICATED = ("mix_pre_g", "mix_post_g", "ffn_pre_g", "ffn_post_g", "ple_norm_g", "hg_lb_logits", "hg_norm_g")
_SMALL = ("conv_w", "conv_b", "rg_wa", "rg_ba", "rg_wx", "rg_bx", "rg_lambda")
_BIG = {"w_in_even": True, "w_out_even": False, "w_in_odd": True, "w_out_odd": False,
        "w_gate_up": True, "w_down": False, "w_ple_up": True, "w_ple_gate": False}
_PACK_ROW = SUBLANES * LANES


def _pack(arrays):
    flat = jnp.concatenate([a.reshape(-1) for a in arrays])
    pad = -flat.shape[0] % _PACK_ROW
    return jnp.pad(flat, (0, pad)).reshape(-1, LANES)


def _pack_blocks(arrays):
    flat = jnp.concatenate([a.reshape(N_DEV, -1) for a in arrays], axis=1)
    pad = -flat.shape[1] % _PACK_ROW
    return jnp.pad(flat, ((0, 0), (0, pad))).reshape(N_DEV, -1, LANES)


def _unpack(packed, shapes, lead=()):
    flat = packed.reshape(lead + (-1,))
    out, pos = [], 0
    for shape in shapes:
        n = math.prod(shape)
        out.append(flat[..., pos:pos + n].reshape(lead + tuple(shape)))
        pos += n
    return out


def _to_full_small(name, blocks):
    if name == "conv_w":
        return jnp.transpose(blocks, (1, 0, 2)).reshape(blocks.shape[1], -1)
    if name in ("conv_b", "rg_lambda"):
        return blocks.reshape(1, -1)
    nb = blocks.shape[1]
    if name in ("rg_wa", "rg_wx"):
        return jnp.transpose(blocks, (1, 0, 2, 3)).reshape(nb, RG_BLOCK, RG_BLOCK)
    return jnp.transpose(blocks, (1, 0, 2)).reshape(nb, 1, RG_BLOCK)


def _to_blocks_small(name, full):
    if name == "conv_w":
        return jnp.transpose(full.reshape(full.shape[0], N_DEV, -1), (1, 0, 2))
    if name in ("conv_b", "rg_lambda"):
        return full.reshape(N_DEV, -1)
    nb = full.shape[0]
    if name in ("rg_wa", "rg_wx"):
        return jnp.transpose(full.reshape(nb, N_DEV, RG_BLOCK // N_DEV, RG_BLOCK), (1, 0, 2, 3))
    return jnp.transpose(full.reshape(nb, N_DEV, RG_BLOCK // N_DEV), (1, 0, 2))


def _step(inp):
    w = {k: inp[k] for k in _WEIGHTS}
    x, p, target = inp["x"][0], inp["p"][:, 0], inp["loss_target"][0]
    assert w["hg_lb_logits"].shape[0] == 2 and w["w_in_even"].shape[0] == 1 and w["w_in_odd"].shape[0] == 1

    n_heads = w["w_in_even"].shape[2] * N_DEV // (7 * HEAD_DIM)
    ff = w["w_down"].shape[1] * N_DEV
    small_shapes = [w[k].shape[1:] for k in _SMALL]

    def lands_in_place(name):
        return _BIG[name] and w[name].shape[2] % LANES == 0

    depth = p.shape[0]
    order = [("w_in_even", 0), ("w_out_even", 0)] if depth else []
    for i in range(depth):
        if i == 1:
            order += [("w_in_odd", 0), ("small", 0), ("w_out_odd", 0)]
        order += [("w_gate_up", i), ("w_down", i), ("w_ple_up", i), ("w_ple_gate", i)]
    gathers = {}
    for name, l in order:
        if name == "small":
            gathers[name, l] = _gather_start(_pack([w[k][0] for k in _SMALL]), name="gather_small")
        else:
            gathers[name, l] = _gather_start(w[name][l].astype(BF16), name=f"gather_{name}{l}",
                                             cols=lands_in_place(name))
    ready = {}

    def weight(name, layer, after):
        key = ("small", 0) if name in _SMALL else (name, layer)
        if key not in ready:
            land, = _spread_wait(gathers[key], after, name=f"gathered_{key[0]}{key[1]}")
            if name in _SMALL:
                ready[key] = {k: _to_full_small(k, b)
                              for k, b in zip(_SMALL, _unpack(land, small_shapes, lead=(N_DEV,)))}
            elif lands_in_place(name):
                ready[key] = land
            elif _BIG[name]:
                ready[key] = jnp.transpose(land, (1, 0, 2)).reshape(land.shape[1], -1)
            else:
                ready[key] = land.reshape(-1, land.shape[2])
        return ready[key][name] if name in _SMALL else ready[key]

    exchanges = []

    def emit(name, layer, g):
        _, r, c = w[name].shape
        if lands_in_place(name):
            handle = _exchange_start([g], name=f"exchange_{name}{layer}", cols=True)
        elif _BIG[name]:
            handle = _exchange_start([jnp.transpose(g.reshape(-1, N_DEV, c), (1, 0, 2))],
                                     name=f"exchange_{name}{layer}")
        else:
            handle = _exchange_start([g.reshape(N_DEV, r, c)], name=f"exchange_{name}{layer}")
        exchanges.append((name, layer, handle))

    rep = {k: w[k] for k in _REPLICATED}
    loss_cols, dx, grads = _local_step(x, p, target, rep, weight, emit, n_heads, ff)
    loss = lax.psum(jnp.sum(loss_cols), MESH_AXES)

    out = {}
    for name, layer, handle in exchanges:
        n_l, r, c = w[name].shape
        land, = _spread_wait(handle, dx, name=f"exchanged_{name}{layer}")
        out[name] = _adamw(land.reshape(N_DEV, r, c),
                           *[inp[pre + name].reshape(n_l * r, c) for pre in ("", "m_", "v_")],
                           name=f"adamw_{name}{layer}", layer=layer, prev=out.get(name))
    for name in _BIG:
        out[name] = [a.reshape(w[name].shape) for a in out[name]]

    small_parts = _exchange([_pack_blocks([_to_blocks_small(k, grads[k]) for k in _SMALL])], name="exchange_small")
    res = _adamw(small_parts[:, 0], *[_pack([inp[pre + k][0] for k in _SMALL]) for pre in ("", "m_", "v_")],
                 name="adamw_small")
    for k, *vals in zip(_SMALL, *[_unpack(a, small_shapes) for a in res]):
        out[k] = [v[None] for v in vals]

    rep_shapes = [w[k].shape for k in _REPLICATED]
    rep_parts, = _all_gather([_pack([grads[k] for k in _REPLICATED])], name="gather_rep_grads")
    res = _adamw(rep_parts, *[_pack([inp[pre + k] for k in _REPLICATED]) for pre in ("", "m_", "v_")],
                 name="adamw_rep")
    for k, *vals in zip(_REPLICATED, *[_unpack(a, rep_shapes) for a in res]):
        out[k] = vals

    return (loss, dx[None]) + tuple(out[k][j] for j in range(4) for k in _WEIGHTS)


def kernel(x, p, mix_pre_g, mix_post_g, ffn_pre_g, ffn_post_g, ple_norm_g, w_in_even, w_out_even, hg_lb_logits, hg_norm_g, w_in_odd, conv_w, conv_b, rg_wa, rg_ba, rg_wx, rg_bx, rg_lambda, w_out_odd, w_gate_up, w_down, w_ple_up, w_ple_gate, loss_target, m_mix_pre_g, m_mix_post_g, m_ffn_pre_g, m_ffn_post_g, m_ple_norm_g, m_w_in_even, m_w_out_even, m_hg_lb_logits, m_hg_norm_g, m_w_in_odd, m_conv_w, m_conv_b, m_rg_wa, m_rg_ba, m_rg_wx, m_rg_bx, m_rg_lambda, m_w_out_odd, m_w_gate_up, m_w_down, m_w_ple_up, m_w_ple_gate, v_mix_pre_g, v_mix_post_g, v_ffn_pre_g, v_ffn_post_g, v_ple_norm_g, v_w_in_even, v_w_out_even, v_hg_lb_logits, v_hg_norm_g, v_w_in_odd, v_conv_w, v_conv_b, v_rg_wa, v_rg_ba, v_rg_wx, v_rg_bx, v_rg_lambda, v_w_out_odd, v_w_gate_up, v_w_down, v_w_ple_up, v_w_ple_gate):
    return _step(dict(locals()))
```

```python
import functools
import math

import jax
import jax.numpy as jnp
from jax import lax
from jax.experimental import pallas as pl
from jax.experimental.pallas import tpu as pltpu

F32 = jnp.float32
BF16 = jnp.bfloat16

VMEM_LIMIT_BYTES = 56 * 1024 * 1024
LANES = 128
SUBLANES = 8

N_DEV = 8
HEAD_DIM = 128
SB_TILE = 128
HG_CHUNK = 32
RG_BLOCK = 256
CONV_TAPS = 4
RG_C = 8.0
RMS_EPS = 1e-6

ADAM_LR = 0.001
ADAM_B1 = 0.9
ADAM_B2 = 0.999
ADAM_EPS = 1e-08
ADAM_WD = 0.01
ADAM_STEP = 10

MESH_AXES = ("x", "y", "c")


def _params(*sem):
    return pltpu.CompilerParams(dimension_semantics=sem, vmem_limit_bytes=VMEM_LIMIT_BYTES)


def _pick(n, cands):
    for c in cands:
        if c <= n and n % c == 0:
            return c
    return n


def _mm(a, b, mode, *, name, out_dtype=F32, tm=512, tn=512, tk=None):
    if mode == "nn":
        (m, k), (k2, n) = a.shape, b.shape
    elif mode == "nt":
        (m, k), (n, k2) = a.shape, b.shape
    else:
        (k, m), (k2, n) = a.shape, b.shape
    assert k == k2, (a.shape, b.shape, mode)
    tm, tn = min(tm, m), min(tn, n)
    tk = k if tk is None else min(tk, k)
    assert m % tm == 0 and n % tn == 0 and k % tk == 0, (m, n, k, tm, tn, tk)
    nk = k // tk

    if mode == "tn":
        assert nk == 1
        return _mm_tn(a, b, name=name, out_dtype=out_dtype, tm=tm, tn=tn)

    a_spec = pl.BlockSpec((tm, tk), lambda i, j, kk: (i, kk))
    if mode == "nn":
        b_spec = pl.BlockSpec((tk, tn), lambda i, j, kk: (kk, j))
        dims = (((1,), (0,)), ((), ()))
    else:
        b_spec = pl.BlockSpec((tn, tk), lambda i, j, kk: (j, kk))
        dims = (((1,), (1,)), ((), ()))

    def body(a_ref, b_ref, o_ref, *acc):
        part = lax.dot_general(a_ref[...].astype(BF16), b_ref[...].astype(BF16), dims, preferred_element_type=F32)
        if nk == 1:
            o_ref[...] = part.astype(out_dtype)
        else:
            acc_ref, = acc
            kk = pl.program_id(2)

            @pl.when(kk == 0)
            def _():
                acc_ref[...] = part

            @pl.when(kk > 0)
            def _():
                acc_ref[...] += part

            @pl.when(kk == nk - 1)
            def _():
                o_ref[...] = acc_ref[...].astype(out_dtype)

    return pl.pallas_call(
        body, name=name,
        grid=(m // tm, n // tn, nk),
        in_specs=[a_spec, b_spec],
        out_specs=pl.BlockSpec((tm, tn), lambda i, j, kk: (i, j)),
        out_shape=jax.ShapeDtypeStruct((m, n), out_dtype),
        scratch_shapes=[] if nk == 1 else [pltpu.VMEM((tm, tn), F32)],
        compiler_params=_params("parallel", "parallel", "arbitrary"),
    )(a, b)


def _mm_tn(a, b, *, name, out_dtype, tm, tn):
    k, m = a.shape
    n = b.shape[1]

    def body(a_ref, b_ref, o_ref, at_ref):
        @pl.when(pl.program_id(1) == 0)
        def _():
            at_ref[...] = a_ref[...].astype(F32).T.astype(BF16)

        o_ref[...] = jnp.dot(at_ref[...], b_ref[...].astype(BF16), preferred_element_type=F32).astype(out_dtype)

    return pl.pallas_call(
        body, name=name,
        grid=(m // tm, n // tn),
        in_specs=[pl.BlockSpec((k, tm), lambda i, j: (0, i)), pl.BlockSpec((k, tn), lambda i, j: (0, j))],
        out_specs=pl.BlockSpec((tm, tn), lambda i, j: (i, j)),
        out_shape=jax.ShapeDtypeStruct((m, n), out_dtype),
        scratch_shapes=[pltpu.VMEM((tm, k), BF16)],
        compiler_params=_params("parallel", "arbitrary"),
    )(a, b)


def _rowcall(name, fn, rows, pars, row_outs, red_rows=(), *, cols, ts=256, tc=None):
    rows = [r if isinstance(r, tuple) else (r, 0) for r in rows]
    pars = [p if isinstance(p, tuple) else (p, 0) for p in pars]
    s = rows[0][0].shape[0]
    tc = cols if tc is None else tc
    ts = min(ts, s)
    assert s % ts == 0 and cols % tc == 0, (name, s, ts, cols, tc)
    n_in, n_row_out = len(rows) + len(pars), len(row_outs)

    def body(*refs):
        outs = fn(*[r[...] for r in refs[:n_in]])
        outs = outs if isinstance(outs, (tuple, list)) else (outs,)
        o_refs = refs[n_in:]
        for o_ref, val in zip(o_refs[:n_row_out], outs[:n_row_out]):
            o_ref[...] = val.astype(o_ref.dtype)
        first = pl.program_id(1) == 0
        for o_ref, val in zip(o_refs[n_row_out:], outs[n_row_out:]):
            @pl.when(first)
            def _(o_ref=o_ref, val=val):
                o_ref[...] = val

            @pl.when(jnp.logical_not(first))
            def _(o_ref=o_ref, val=val):
                o_ref[...] += val

    def row_map(off):
        return lambda j, i: (i, j + off)

    def par_map(off):
        return lambda j, i: (0, j + off)

    return pl.pallas_call(
        body, name=name,
        grid=(cols // tc, s // ts),
        in_specs=[pl.BlockSpec((ts, tc), row_map(off)) for _, off in rows]
        + [pl.BlockSpec((p.shape[0], tc), par_map(off)) for p, off in pars],
        out_specs=[pl.BlockSpec((ts, tc), lambda j, i: (i, j)) for _ in row_outs]
        + [pl.BlockSpec((r, tc), lambda j, i: (0, j)) for r in red_rows],
        out_shape=[jax.ShapeDtypeStruct((s, cols), dt) for dt in row_outs]
        + [jax.ShapeDtypeStruct((r, cols), F32) for r in red_rows],
        compiler_params=_params("parallel", "arbitrary"),
    )(*[r for r, _ in rows], *[p for p, _ in pars])


def _rms(x, g):
    return x * lax.rsqrt(jnp.mean(x * x, axis=-1, keepdims=True) + RMS_EPS) * g


def _sigmoid(x):
    return jax.nn.sigmoid(x)


def _silu(x):
    return x * jax.nn.sigmoid(x)


def _gelu(x):
    return 0.5 * x * (1.0 + jnp.tanh(math.sqrt(2.0 / math.pi) * (x + 0.044715 * (x * x * x))))


def _softplus(x):
    return jnp.maximum(x, 0.0) + jnp.log1p(jnp.exp(-jnp.abs(x)))


def _split3(x):
    hi = x.astype(BF16)
    r1 = x - hi.astype(F32)
    mid = r1.astype(BF16)
    lo = (r1 - mid.astype(F32)).astype(BF16)
    return hi, mid, lo


def _xdot(x, t):
    return sum(jnp.dot(p, t, preferred_element_type=F32) for p in _split3(x))


def _xdot_l(t, x):
    return sum(jnp.dot(t, p, preferred_element_type=F32) for p in _split3(x))


_NT = (((1,), (1,)), ((), ()))
_TN = (((0,), (0,)), ((), ()))


def _dot(a, b, dims=None):
    if dims is None:
        return jnp.dot(a.astype(BF16), b.astype(BF16), preferred_element_type=F32)
    return lax.dot_general(a.astype(BF16), b.astype(BF16), dims, preferred_element_type=F32)


def _iota(shape, axis):
    return lax.broadcasted_iota(jnp.int32, shape, axis)


def _sb_tile(qb, kblk, kb, qi, row, col, upper, c_rem):
    z = lax.dot_general(qb, kblk, _NT, preferred_element_type=F32)
    mask = (col + kb * SB_TILE) < (row + qi * SB_TILE)
    soft = jnp.log1p(jnp.exp(-jnp.abs(z)))
    lbeta = jnp.minimum(z, 0.0) - soft
    l1m = jnp.where(mask, -jnp.maximum(z, 0.0) - soft, 0.0)
    rem = _xdot(l1m, upper) + c_rem
    w = jnp.where(mask, jnp.exp(lbeta + rem), 0.0)
    return z, mask, lbeta, l1m, w


def _sb_fwd(proj, n_heads, *, name):
    s = proj.shape[0]
    t = SB_TILE
    scale = HEAD_DIM ** -0.5

    def body(q_ref, k_ref, v_ref, o_ref):
        qi = pl.program_id(1)
        qb = (q_ref[...] * scale).astype(BF16)
        row, col = _iota((t, t), 0), _iota((t, t), 1)
        upper = (row > col).astype(BF16)

        def step(j, carry):
            acc, c_rem = carry
            kb = qi - j
            rows = pl.ds(pl.multiple_of(kb * t, t), t)
            kblk = k_ref[rows, :].astype(BF16)
            vblk = v_ref[rows, :].astype(BF16)
            _, _, _, l1m, w = _sb_tile(qb, kblk, kb, qi, row, col, upper, c_rem)
            acc = acc + jnp.dot(w.astype(BF16), vblk, preferred_element_type=F32)
            return acc, c_rem + jnp.sum(l1m, axis=1, keepdims=True)

        acc, _ = lax.fori_loop(0, qi + 1, step, (jnp.zeros((t, HEAD_DIM), F32), jnp.zeros((t, 1), F32)))
        o_ref[...] = acc

    return pl.pallas_call(
        body, name=name,
        grid=(n_heads, s // t),
        in_specs=[pl.BlockSpec((t, HEAD_DIM), lambda h, i: (i, h)),
                  pl.BlockSpec((s, HEAD_DIM), lambda h, i: (0, n_heads + h)),
                  pl.BlockSpec((s, HEAD_DIM), lambda h, i: (0, 2 * n_heads + h))],
        out_specs=pl.BlockSpec((t, HEAD_DIM), lambda h, i: (i, h)),
        out_shape=jax.ShapeDtypeStruct((s, n_heads * HEAD_DIM), F32),
        compiler_params=_params("parallel", "arbitrary"),
    )(proj, proj, proj)


def _sb_bwd(proj, dcat, n_heads, *, name):
    s = proj.shape[0]
    t = SB_TILE
    scale = HEAD_DIM ** -0.5

    def body(q_ref, k_ref, v_ref, do_ref, dq_ref, dk_ref, dv_ref, g_s, sig_s):
        qi = pl.program_id(1)

        @pl.when(qi == 0)
        def _():
            dk_ref[...] = jnp.zeros_like(dk_ref)
            dv_ref[...] = jnp.zeros_like(dv_ref)

        qb = (q_ref[...] * scale).astype(BF16)
        dob = do_ref[...].astype(BF16)
        row, col = _iota((t, t), 0), _iota((t, t), 1)
        upper = (row > col).astype(BF16)
        lower_incl = (row >= col).astype(BF16)

        def weights(j, carry):
            c_rem, g_all = carry
            kb = qi - j
            rows = pl.ds(pl.multiple_of(kb * t, t), t)
            kblk = k_ref[rows, :].astype(BF16)
            vblk = v_ref[rows, :].astype(BF16)
            _, _, lbeta, l1m, w = _sb_tile(qb, kblk, kb, qi, row, col, upper, c_rem)
            g = w * lax.dot_general(dob, vblk, _NT, preferred_element_type=F32)
            dv_ref[rows, :] += lax.dot_general(w.astype(BF16), dob, _TN, preferred_element_type=F32)
            g_s[kb] = g
            sig_s[kb] = jnp.exp(lbeta)
            return c_rem + jnp.sum(l1m, axis=1, keepdims=True), g_all + jnp.sum(g, axis=1, keepdims=True)

        zero_col = jnp.zeros((t, 1), F32)
        _, g_all = lax.fori_loop(0, qi + 1, weights, (zero_col, zero_col))

        def scores(j, carry):
            dq, c_g = carry
            kb = qi - j
            rows = pl.ds(pl.multiple_of(kb * t, t), t)
            g, sig = g_s[kb], sig_s[kb]
            mask = (col + kb * t) < (row + qi * t)
            g_before = g_all - (_xdot(g, lower_incl) + c_g)
            dz = jnp.where(mask, g * (1.0 - sig) - g_before * sig, 0.0).astype(BF16)
            dq = dq + jnp.dot(dz, k_ref[rows, :].astype(BF16), preferred_element_type=F32)
            dk_ref[rows, :] += lax.dot_general(dz, qb, _TN, preferred_element_type=F32)
            return dq, c_g + jnp.sum(g, axis=1, keepdims=True)

        dq, _ = lax.fori_loop(0, qi + 1, scores, (jnp.zeros((t, HEAD_DIM), F32), zero_col))
        dq_ref[...] = dq * scale

    width = n_heads * HEAD_DIM
    return pl.pallas_call(
        body, name=name,
        grid=(n_heads, s // t),
        in_specs=[pl.BlockSpec((t, HEAD_DIM), lambda h, i: (i, h)),
                  pl.BlockSpec((s, HEAD_DIM), lambda h, i: (0, n_heads + h)),
                  pl.BlockSpec((s, HEAD_DIM), lambda h, i: (0, 2 * n_heads + h)),
                  pl.BlockSpec((t, HEAD_DIM), lambda h, i: (i, h))],
        out_specs=[pl.BlockSpec((t, HEAD_DIM), lambda h, i: (i, h)),
                   pl.BlockSpec((s, HEAD_DIM), lambda h, i: (0, h)),
                   pl.BlockSpec((s, HEAD_DIM), lambda h, i: (0, h))],
        out_shape=[jax.ShapeDtypeStruct((s, width), F32)] * 3,
        scratch_shapes=[pltpu.VMEM((s // t, t, t), F32)] * 2,
        compiler_params=_params("parallel", "arbitrary"),
    )(proj, proj, proj, dcat)


def _hg_pre(hq, hf, logits):
    mx = jnp.max(logits, axis=0, keepdims=True)
    ex = jnp.exp(logits - mx)
    lb = ex[0:1, :] / jnp.sum(ex, axis=0, keepdims=True)
    f = lb + (1.0 - lb) * _sigmoid(hf)
    return _silu(hq), 1.0 - f, jnp.log(f)


def _hg_post(o, norm_g, hgate):
    return _rms(o, norm_g) * _silu(hgate)


def _hg_specs(s, n_heads, first_block):
    def at(group):
        return pl.BlockSpec((s, HEAD_DIM), lambda h: (0, first_block + group * n_heads + h))
    return [at(0), at(1), at(2), at(3)]


def _hg_fwd(proj, logits, norm_g, n_heads, *, name):
    s = proj.shape[0]
    hc = HG_CHUNK
    n_chunks = s // hc
    d = HEAD_DIM

    def body(lg_ref, ng_ref, hq_ref, hf_ref, hi_ref, hgt_ref, out_ref, oraw_ref, st_ref,
             q_s, k_s, lf_s, cum_s, qc_s, oc_s):
        q, k, lf = _hg_pre(hq_ref[...], hf_ref[...], lg_ref[...])
        q_s[...] = q
        k_s[...] = k
        lf_s[...] = lf
        tril = (_iota((hc, hc), 0) >= _iota((hc, hc), 1)).astype(BF16)
        srow = _iota((hc, d), 0)

        def chunk(ci, st):
            rows = pl.ds(pl.multiple_of(ci * hc, hc), hc)
            q, k, v = q_s[rows, :], k_s[rows, :], hi_ref[rows, :]
            cum = _xdot_l(tril, lf_s[rows, :])
            st_ref[0, ci] = st
            o_inter = _dot(q * jnp.exp(cum), st, _NT)
            cum_s[...] = cum
            qc_s[...] = q
            for t in range(hc):
                ng = (t // SUBLANES + 1) * SUBLANES
                e = jnp.where(srow[:ng] <= t, jnp.exp(cum_s[t:t + 1, :] - cum[:ng]), 0.0)
                sc = jnp.sum(qc_s[t:t + 1, :] * k[:ng] * e, axis=1, keepdims=True)
                oc_s[t:t + 1, :] = jnp.sum(sc * v[:ng], axis=0, keepdims=True)
            oraw_ref[rows, :] = o_inter + oc_s[...]
            last = cum_s[hc - 1:hc, :]
            return st * jnp.exp(last) + _dot(v, k * jnp.exp(last - cum), _TN)

        lax.fori_loop(0, n_chunks, chunk, jnp.zeros((d, d), F32))
        out_ref[...] = _hg_post(oraw_ref[...], ng_ref[...], hgt_ref[...]).astype(BF16)

    width = n_heads * d
    head_block = pl.BlockSpec((s, d), lambda h: (0, h))
    return pl.pallas_call(
        body, name=name,
        grid=(n_heads,),
        in_specs=[pl.BlockSpec((2, d), lambda h: (0, h)), pl.BlockSpec((1, d), lambda h: (0, 0))]
        + _hg_specs(s, n_heads, 3 * n_heads),
        out_specs=[head_block, head_block, pl.BlockSpec((1, n_chunks, d, d), lambda h: (h, 0, 0, 0))],
        out_shape=[jax.ShapeDtypeStruct((s, width), BF16), jax.ShapeDtypeStruct((s, width), F32),
                   jax.ShapeDtypeStruct((n_heads, n_chunks, d, d), F32)],
        scratch_shapes=[pltpu.VMEM((s, d), F32)] * 3 + [pltpu.VMEM((hc, d), F32)] * 3,
        compiler_params=_params("arbitrary"),
    )(logits, norm_g, proj, proj, proj, proj)


def _hg_bwd(proj, logits, norm_g, oraw, states, dcat, n_heads, *, name):
    s = proj.shape[0]
    hc = HG_CHUNK
    n_chunks = s // hc
    d = HEAD_DIM

    def body(lg_ref, ng_ref, hq_ref, hf_ref, hi_ref, hgt_ref, oraw_ref, st_ref, dout_ref,
             dhq_ref, dhf_ref, dhi_ref, dhgt_ref, dlg_ref, dng_ref,
             q_s, k_s, lf_s, do_s, dq_s, dk_s, dlf_s, cum_s, qc_s, doc_s, dqc_s, dkc_s, dvc_s):
        head = pl.program_id(0)
        (q, k, lf), pre_vjp = jax.vjp(_hg_pre, hq_ref[...], hf_ref[...], lg_ref[...])
        q_s[...] = q
        k_s[...] = k
        lf_s[...] = lf
        _, post_vjp = jax.vjp(_hg_post, oraw_ref[...], ng_ref[...], hgt_ref[...])
        do, dng, dhgt = post_vjp(dout_ref[...])
        do_s[...] = do
        dhgt_ref[...] = dhgt.astype(BF16)

        @pl.when(head == 0)
        def _():
            dng_ref[...] = dng

        @pl.when(head > 0)
        def _():
            dng_ref[...] += dng

        triu = (_iota((hc, hc), 0) <= _iota((hc, hc), 1)).astype(BF16)
        tril = (_iota((hc, hc), 0) >= _iota((hc, hc), 1)).astype(BF16)
        srow = _iota((hc, d), 0)

        def chunk(j, dst):
            ci = n_chunks - 1 - j
            rows = pl.ds(pl.multiple_of(ci * hc, hc), hc)
            q, k, v, do_c = q_s[rows, :], k_s[rows, :], hi_ref[rows, :], do_s[rows, :]
            cum = _xdot_l(tril, lf_s[rows, :])
            st = st_ref[0, ci]
            cum_s[...] = cum
            qc_s[...] = q
            doc_s[...] = do_c
            last = cum_s[hc - 1:hc, :]
            e_cum, e_last = jnp.exp(cum), jnp.exp(last - cum)
            dqc_s[...] = _dot(do_c, st) * e_cum
            dk_state = _dot(v, dst) * e_last
            dkc_s[...] = dk_state
            dvc_s[...] = _dot(k * e_last, dst, _NT)
            d_last = (jnp.sum(dst * st, axis=0, keepdims=True) * jnp.exp(last)
                      + jnp.sum(k * dk_state, axis=0, keepdims=True))
            for t in range(hc):
                ng = (t // SUBLANES + 1) * SUBLANES
                qt, dot_ = qc_s[t:t + 1, :], doc_s[t:t + 1, :]
                e = jnp.where(srow[:ng] <= t, jnp.exp(cum_s[t:t + 1, :] - cum[:ng]), 0.0)
                ke = k[:ng] * e
                d_a = jnp.sum(dot_ * v[:ng], axis=1, keepdims=True)
                dqc_s[t:t + 1, :] += jnp.sum(d_a * ke, axis=0, keepdims=True)
                dkc_s[0:ng, :] += d_a * (qt * e)
                dvc_s[0:ng, :] += jnp.sum(qt * ke, axis=1, keepdims=True) * dot_
            dq, dk = dqc_s[...], dkc_s[...]
            d_b = q * dq - k * dk
            dq_s[rows, :] = dq
            dk_s[rows, :] = dk
            dhi_ref[rows, :] = dvc_s[...].astype(BF16)
            dlf_s[rows, :] = _xdot_l(triu, d_b) + d_last
            return dst * jnp.exp(last) + _dot(do_c, q * e_cum, _TN)

        lax.fori_loop(0, n_chunks, chunk, jnp.zeros((d, d), F32))
        dhq, dhf, dlg = pre_vjp((dq_s[...], dk_s[...], dlf_s[...]))
        dhq_ref[...] = dhq.astype(BF16)
        dhf_ref[...] = dhf.astype(BF16)
        dlg_ref[...] = dlg

    width = n_heads * d
    head_block = pl.BlockSpec((s, d), lambda h: (0, h))
    return pl.pallas_call(
        body, name=name,
        grid=(n_heads,),
        in_specs=[pl.BlockSpec((2, d), lambda h: (0, h)), pl.BlockSpec((1, d), lambda h: (0, 0))]
        + _hg_specs(s, n_heads, 3 * n_heads)
        + [head_block, pl.BlockSpec((1, n_chunks, d, d), lambda h: (h, 0, 0, 0)),
           pl.BlockSpec((s, d), lambda h: (0, n_heads + h))],
        out_specs=[head_block] * 4 + [pl.BlockSpec((2, d), lambda h: (0, h)), pl.BlockSpec((1, d), lambda h: (0, 0))],
        out_shape=[jax.ShapeDtypeStruct((s, width), BF16)] * 4
        + [jax.ShapeDtypeStruct((2, width), F32), jax.ShapeDtypeStruct((1, d), F32)],
        scratch_shapes=[pltpu.VMEM((s, d), F32)] * 7 + [pltpu.VMEM((hc, d), F32)] * 6,
        compiler_params=_params("arbitrary"),
    )(logits, norm_g, proj, proj, proj, proj, oraw, states, dcat)


def _shift_down(x, n, srow):
    if n == 0:
        return x
    return jnp.where(srow >= n, pltpu.roll(x, n, 0), 0.0)


def _shift_up(x, n, srow):
    if n == 0:
        return x
    s = x.shape[0]
    return jnp.where(srow < s - n, pltpu.roll(x, s - n, 0), 0.0)


def _rg_gates_fwd(proj, conv_w, conv_b, wa, ba, wx, bx, *, name):
    s = proj.shape[0]
    nb = wa.shape[0]
    bw = RG_BLOCK

    def body(xb_ref, cw_ref, cb_ref, wa_ref, ba_ref, wx_ref, bx_ref, xc_ref, ra_ref, ix_ref):
        x = xb_ref[...]
        srow = _iota((s, bw), 0)
        cw = cw_ref[...]
        xc = cb_ref[...] + cw[0:1, :] * x
        for tap in range(1, CONV_TAPS):
            xc = xc + cw[tap:tap + 1, :] * _shift_down(x, tap, srow)
        xc_ref[...] = xc
        ra_ref[...] = _dot(xc, wa_ref[0]) + ba_ref[0]
        ix_ref[...] = _dot(xc, wx_ref[0]) + bx_ref[0]

    col = pl.BlockSpec((s, bw), lambda n: (0, n))
    vec = lambda r: pl.BlockSpec((r, bw), lambda n: (0, n))
    mat = pl.BlockSpec((1, bw, bw), lambda n: (n, 0, 0))
    bias = pl.BlockSpec((1, 1, bw), lambda n: (n, 0, 0))
    return pl.pallas_call(
        body, name=name,
        grid=(nb,),
        in_specs=[pl.BlockSpec((s, bw), lambda n: (0, nb + n)), vec(CONV_TAPS), vec(1), mat, bias, mat, bias],
        out_specs=[col] * 3,
        out_shape=[jax.ShapeDtypeStruct((s, nb * bw), F32)] * 3,
        compiler_params=_params("parallel"),
    )(proj, conv_w, conv_b, wa, ba, wx, bx)


def _rg_au(ra, ix, xc, lam, first_row):
    log_a = -RG_C * _sigmoid(ra) * _softplus(-lam)
    th = jnp.tanh(log_a)
    one_minus_a2 = -2.0 * th / (1.0 - th)
    mult = jnp.where(first_row, 1.0, jnp.sqrt(one_minus_a2))
    return jnp.exp(log_a), xc * _sigmoid(ix) * mult


def _rg_out(gate, hs):
    return _gelu(gate) * hs


def _scan_rows(n_groups, reverse, group_fn, init):
    def group(gi, carry):
        g = (n_groups - 1 - gi) if reverse else gi
        return group_fn(pl.multiple_of(g * SUBLANES, SUBLANES), carry)
    return lax.fori_loop(0, n_groups, group, init)


def _rg_scan_fwd(proj, xc, ra, ix, lam, *, name):
    s, width = xc.shape
    tc = LANES

    def body(gate_ref, xc_ref, ra_ref, ix_ref, lam_ref, hs_ref, gact_ref, a_s, u_s):
        first_row = _iota((s, tc), 0) == 0
        a, u = _rg_au(ra_ref[...], ix_ref[...], xc_ref[...], lam_ref[...], first_row)
        a_s[...] = a
        u_s[...] = u
        r8 = _iota((SUBLANES, tc), 0)

        def rows(r0, h):
            ag, ug = a_s[pl.ds(r0, SUBLANES), :], u_s[pl.ds(r0, SUBLANES), :]
            tile = jnp.zeros((SUBLANES, tc), F32)
            for r in range(SUBLANES):
                h = ag[r:r + 1, :] * h + ug[r:r + 1, :]
                tile = jnp.where(r8 == r, h, tile)
            hs_ref[pl.ds(r0, SUBLANES), :] = tile
            return h

        _scan_rows(s // SUBLANES, False, rows, jnp.zeros((1, tc), F32))
        gact_ref[...] = _rg_out(gate_ref[...], hs_ref[...]).astype(BF16)

    col = pl.BlockSpec((s, tc), lambda n: (0, n))
    return pl.pallas_call(
        body, name=name,
        grid=(width // tc,),
        in_specs=[col, col, col, col, pl.BlockSpec((1, tc), lambda n: (0, n))],
        out_specs=[col, col],
        out_shape=[jax.ShapeDtypeStruct((s, width), F32), jax.ShapeDtypeStruct((s, width), BF16)],
        scratch_shapes=[pltpu.VMEM((s, tc), F32)] * 2,
        compiler_params=_params("parallel"),
    )(proj, xc, ra, ix, lam)


def _rg_scan_bwd(dgo, proj, hs, xc, ra, ix, lam, *, name):
    s, width = xc.shape
    tc = LANES

    def body(dgo_ref, gate_ref, hs_ref, xc_ref, ra_ref, ix_ref, lam_ref,
             dgate_ref, dra_ref, dix_ref, dxc_ref, dlam_ref, a_s, dh_s, g_s):
        srow = _iota((s, tc), 0)
        hs = hs_ref[...]
        _, out_vjp = jax.vjp(_rg_out, gate_ref[...], hs)
        dgate, dh = out_vjp(dgo_ref[...])
        dgate_ref[...] = dgate.astype(BF16)
        au = functools.partial(_rg_au, first_row=srow == 0)
        (a, _), au_vjp = jax.vjp(au, ra_ref[...], ix_ref[...], xc_ref[...], lam_ref[...])
        a_s[...] = a
        dh_s[...] = dh
        r8 = _iota((SUBLANES, tc), 0)

        def rows(r0, carry):
            g, a_next = carry
            ag, dg = a_s[pl.ds(r0, SUBLANES), :], dh_s[pl.ds(r0, SUBLANES), :]
            tile = jnp.zeros((SUBLANES, tc), F32)
            for r in reversed(range(SUBLANES)):
                g = dg[r:r + 1, :] + a_next * g
                a_next = ag[r:r + 1, :]
                tile = jnp.where(r8 == r, g, tile)
            g_s[pl.ds(r0, SUBLANES), :] = tile
            return g, a_next

        zero = jnp.zeros((1, tc), F32)
        _scan_rows(s // SUBLANES, True, rows, (zero, zero))
        g = g_s[...]
        dra, dix, dxc, dlam = au_vjp((g * _shift_down(hs, 1, srow), g))
        dra_ref[...] = dra.astype(BF16)
        dix_ref[...] = dix.astype(BF16)
        dxc_ref[...] = dxc
        dlam_ref[...] = dlam

    col = pl.BlockSpec((s, tc), lambda n: (0, n))
    vec = pl.BlockSpec((1, tc), lambda n: (0, n))
    return pl.pallas_call(
        body, name=name,
        grid=(width // tc,),
        in_specs=[col] * 6 + [vec],
        out_specs=[col] * 4 + [vec],
        out_shape=[jax.ShapeDtypeStruct((s, width), BF16)] * 3
        + [jax.ShapeDtypeStruct((s, width), F32), jax.ShapeDtypeStruct((1, width), F32)],
        scratch_shapes=[pltpu.VMEM((s, tc), F32)] * 3,
        compiler_params=_params("parallel"),
    )(dgo, proj, hs, xc, ra, ix, lam)


def _rg_gates_bwd(dra, dix, dxc1, xc, proj, conv_w, wa, wx, *, name):
    s = proj.shape[0]
    nb = wa.shape[0]
    bw = RG_BLOCK

    def body(dra_ref, dix_ref, dxc_ref, xc_ref, xb_ref, cw_ref, wa_ref, wx_ref,
             dxb_ref, dcw_ref, dcb_ref, dwa_ref, dba_ref, dwx_ref, dbx_ref):
        dra, dix = dra_ref[...], dix_ref[...]
        xc_t = xc_ref[...].T.astype(BF16)
        dwa_ref[0] = jnp.dot(xc_t, dra, preferred_element_type=F32)
        dwx_ref[0] = jnp.dot(xc_t, dix, preferred_element_type=F32)
        dba_ref[0] = jnp.sum(dra.astype(F32), axis=0, keepdims=True)
        dbx_ref[0] = jnp.sum(dix.astype(F32), axis=0, keepdims=True)
        dxc = dxc_ref[...] + _dot(dra, wa_ref[0], _NT) + _dot(dix, wx_ref[0], _NT)
        srow = _iota((s, bw), 0)
        x = xb_ref[...]
        cw = cw_ref[...]
        dx = cw[0:1, :] * dxc
        dcw = [jnp.sum(dxc * x, axis=0, keepdims=True)]
        for tap in range(1, CONV_TAPS):
            dx = dx + cw[tap:tap + 1, :] * _shift_up(dxc, tap, srow)
            dcw.append(jnp.sum(dxc * _shift_down(x, tap, srow), axis=0, keepdims=True))
        dxb_ref[...] = dx.astype(BF16)
        r4 = _iota((CONV_TAPS, bw), 0)
        acc = jnp.zeros((CONV_TAPS, bw), F32)
        for tap in range(CONV_TAPS):
            acc = jnp.where(r4 == tap, dcw[tap], acc)
        dcw_ref[...] = acc
        dcb_ref[...] = jnp.sum(dxc, axis=0, keepdims=True)

    col = pl.BlockSpec((s, bw), lambda n: (0, n))
    vec = lambda r: pl.BlockSpec((r, bw), lambda n: (0, n))
    mat = pl.BlockSpec((1, bw, bw), lambda n: (n, 0, 0))
    bias = pl.BlockSpec((1, 1, bw), lambda n: (n, 0, 0))
    width = nb * bw
    return pl.pallas_call(
        body, name=name,
        grid=(nb,),
        in_specs=[col, col, col, col, pl.BlockSpec((s, bw), lambda n: (0, nb + n)), vec(CONV_TAPS), mat, mat],
        out_specs=[col, vec(CONV_TAPS), vec(1), mat, bias, mat, bias],
        out_shape=[jax.ShapeDtypeStruct((s, width), BF16), jax.ShapeDtypeStruct((CONV_TAPS, width), F32),
                   jax.ShapeDtypeStruct((1, width), F32), jax.ShapeDtypeStruct((nb, bw, bw), F32),
                   jax.ShapeDtypeStruct((nb, 1, bw), F32), jax.ShapeDtypeStruct((nb, bw, bw), F32),
                   jax.ShapeDtypeStruct((nb, 1, bw), F32)],
        compiler_params=_params("parallel"),
    )(dra, dix, dxc1, xc, proj, conv_w, wa, wx)


_HBM = pl.BlockSpec(memory_space=pltpu.HBM)
_FLIPS = ((0, 0, 1), (1, 0, 0), (0, 1, 0), (1, 1, 0))
_ALL_FLIPS = tuple((a, b, c) for a in (0, 1) for b in (0, 1) for c in (0, 1))[1:]


def _flip(pos, f):
    return tuple(1 - p if b else p for p, b in zip(pos, f))


def _dev_index(pos):
    return 4 * pos[0] + 2 * pos[1] + pos[2]


def _block(ref, idx, cols):
    if not cols:
        return ref.at[idx]
    n = ref.shape[-1] // N_DEV
    start = pl.multiple_of(idx * n, LANES)
    return ref.at[(slice(None),) * (len(ref.shape) - 1) + (pl.ds(start, n),)]


def _all_gather(xs, *, name, cols=False):
    n_arr = len(xs)

    def body(*refs):
        x_refs, out_refs = refs[:n_arr], refs[n_arr:2 * n_arr]
        send_sems, recv_sems, local_sems = refs[2 * n_arr:]
        me = (lax.axis_index("x"), lax.axis_index("y"), lax.axis_index("c"))
        sibling = _flip(me, _FLIPS[0])
        chips = [_flip(me, f) for f in _FLIPS[1:]]

        def copy(a, k, block, to, src=None):
            dst = _block(out_refs[a], _dev_index(block), cols)
            return pltpu.make_async_remote_copy(
                src_ref=dst if src is None else src, dst_ref=dst,
                send_sem=send_sems.at[7 * a + k], recv_sem=recv_sems.at[7 * a + k],
                device_id=to, device_id_type=pl.DeviceIdType.MESH)

        mine = [pltpu.make_async_copy(x_refs[a], _block(out_refs[a], _dev_index(me), cols), local_sems.at[a])
                for a in range(n_arr)]
        for cp in mine:
            cp.start()
        first = []
        for a in range(n_arr):
            first.append(copy(a, 0, me, sibling, src=x_refs[a]))
            first += [copy(a, 1 + j, me, chip, src=x_refs[a]) for j, chip in enumerate(chips)]
        for cp in first:
            cp.start()
        passed = []
        for j, chip in enumerate(chips):
            for a in range(n_arr):
                copy(a, 1 + j, chip, me).wait_recv()
                fwd = copy(a, 4 + j, chip, sibling)
                fwd.start()
                passed.append(fwd)
        for a in range(n_arr):
            copy(a, 0, sibling, me).wait_recv()
            for j, chip in enumerate(chips):
                copy(a, 4 + j, _flip(chip, _FLIPS[0]), me).wait_recv()
        for cp in first + passed:
            cp.wait_send()
        for cp in mine:
            cp.wait()

    def out_shape(x):
        shape = x.shape[:-1] + (N_DEV * x.shape[-1],) if cols else (N_DEV,) + x.shape
        return jax.ShapeDtypeStruct(shape, x.dtype)

    return pl.pallas_call(
        body, name=name,
        in_specs=[_HBM] * n_arr, out_specs=[_HBM] * n_arr,
        out_shape=[out_shape(x) for x in xs],
        scratch_shapes=[pltpu.SemaphoreType.DMA((7 * n_arr,)), pltpu.SemaphoreType.DMA((7 * n_arr,)),
                        pltpu.SemaphoreType.DMA((n_arr,))],
    )(*xs)


def _exchange(ps, *, name, cols=False):
    n_arr = len(ps)
    blk = ps[0].shape[:-1] + (ps[0].shape[-1] // N_DEV,) if cols else ps[0].shape[1:]

    def body(*refs):
        p_refs, out_ref = refs[:n_arr], refs[n_arr]
        send_sems, recv_sems, local_sems = refs[n_arr + 1:]
        me = (lax.axis_index("x"), lax.axis_index("y"), lax.axis_index("c"))
        me_idx = _dev_index(me)
        own = [pltpu.make_async_copy(_block(p_refs[a], me_idx, cols), out_ref.at[me_idx, a], local_sems.at[a])
               for a in range(n_arr)]
        for cp in own:
            cp.start()
        sends = []
        for k, f in enumerate(_ALL_FLIPS):
            peer = _flip(me, f)
            for a in range(n_arr):
                cp = pltpu.make_async_remote_copy(
                    src_ref=_block(p_refs[a], _dev_index(peer), cols), dst_ref=out_ref.at[me_idx, a],
                    send_sem=send_sems.at[7 * a + k], recv_sem=recv_sems.at[7 * a + k],
                    device_id=peer, device_id_type=pl.DeviceIdType.MESH)
                cp.start()
                sends.append(cp)
        for cp in sends:
            cp.wait()
        for cp in own:
            cp.wait()

    return pl.pallas_call(
        body, name=name,
        in_specs=[_HBM] * n_arr, out_specs=_HBM,
        out_shape=jax.ShapeDtypeStruct((N_DEV, n_arr) + blk, ps[0].dtype),
        scratch_shapes=[pltpu.SemaphoreType.DMA((7 * n_arr,)), pltpu.SemaphoreType.DMA((7 * n_arr,)),
                        pltpu.SemaphoreType.DMA((n_arr,))],
    )(*ps)


_SEM = pl.BlockSpec(memory_space=pltpu.SEMAPHORE)
_ANY = pl.BlockSpec(memory_space=pl.ANY)
_N_PEERS = N_DEV - 1


def _hbm(x):
    return pltpu.with_memory_space_constraint(x, pltpu.HBM)


def _spread_copies(src_refs, land_refs, send_sems, recv_sems, local_sems, src_block, dst_block):
    me = (lax.axis_index("x"), lax.axis_index("y"), lax.axis_index("c"))
    me_idx = _dev_index(me)
    local, remote = [], []
    for a, src in enumerate(src_refs):
        local.append(pltpu.make_async_copy(src_block(a, src, me_idx), dst_block(a, land_refs, me_idx), local_sems.at[a]))
    for k, f in enumerate(_ALL_FLIPS):
        peer = _flip(me, f)
        for a, src in enumerate(src_refs):
            remote.append(pltpu.make_async_remote_copy(
                src_ref=src_block(a, src, _dev_index(peer)), dst_ref=dst_block(a, land_refs, me_idx),
                send_sem=send_sems.at[_N_PEERS * a + k], recv_sem=recv_sems.at[_N_PEERS * a + k],
                device_id=peer, device_id_type=pl.DeviceIdType.MESH))
    return local, remote


def _spread_start(srcs, land_shapes, src_block, dst_block, *, name):
    ns, nl = len(srcs), len(land_shapes)

    def body(*refs):
        src_refs, land_refs = refs[:ns], refs[ns:ns + nl]
        send_sems, recv_sems, local_sems = refs[ns + nl:ns + nl + 3]
        local, remote = _spread_copies(src_refs, land_refs, send_sems, recv_sems, local_sems, src_block, dst_block)
        for cp in local + remote:
            cp.start()
        token = refs[-1]
        token[...] = jnp.zeros_like(token)

    lands = [_hbm(lax.empty(shape, dtype)) for shape, dtype in land_shapes]
    out = pl.pallas_call(
        body, name=name,
        in_specs=[_HBM] * (ns + nl),
        out_specs=[_SEM] * 3 + [_HBM] * (ns + nl) + [pl.BlockSpec(memory_space=pltpu.VMEM)],
        out_shape=[pltpu.SemaphoreType.DMA((_N_PEERS * ns,)), pltpu.SemaphoreType.DMA((_N_PEERS * ns,)),
                   pltpu.SemaphoreType.DMA((ns,))]
        + [pltpu.HBM(x.shape, x.dtype) for x in srcs] + [pltpu.HBM(shape, dtype) for shape, dtype in land_shapes]
        + [jax.ShapeDtypeStruct((SUBLANES, LANES), F32)],
        input_output_aliases={i: 3 + i for i in range(ns + nl)},
        compiler_params=pltpu.CompilerParams(has_side_effects=pltpu.SideEffectType.DATAFLOW_SIDE_EFFECTING),
    )(*[_hbm(x) for x in srcs], *lands)
    return dict(sems=out[:3], srcs=out[3:3 + ns], lands=out[3 + ns:3 + ns + nl], token=out[-1][0:1, 0:1],
                src_block=src_block, dst_block=dst_block)


def _spread_wait(handle, after, *, name):
    ns, nl = len(handle["srcs"]), len(handle["lands"])

    def body(*refs):
        src_refs, land_refs = refs[:ns], refs[ns:ns + nl]
        send_sems, recv_sems, local_sems = refs[ns + nl:ns + nl + 3]
        local, remote = _spread_copies(src_refs, land_refs, send_sems, recv_sems, local_sems,
                                       handle["src_block"], handle["dst_block"])
        for cp in local:
            cp.wait()
        for cp in remote:
            cp.wait_send()
            cp.wait_recv()

    out = pl.pallas_call(
        body, name=name,
        in_specs=[_HBM] * (ns + nl) + [_SEM] * 3 + [_ANY],
        out_specs=[_HBM] * (ns + nl),
        out_shape=[pltpu.HBM(x.shape, x.dtype) for x in handle["srcs"] + handle["lands"]],
        input_output_aliases={i: i for i in range(ns + nl)},
        compiler_params=pltpu.CompilerParams(has_side_effects=pltpu.SideEffectType.DATAFLOW_SIDE_EFFECTING),
    )(*handle["srcs"], *handle["lands"], *handle["sems"], after)
    return list(out[ns:])


def _gather_start(x, *, name, cols=False):
    shape = x.shape[:-1] + (N_DEV * x.shape[-1],) if cols else (N_DEV,) + x.shape
    return _spread_start([x], [(shape, x.dtype)], lambda a, ref, d: ref,
                         lambda a, lands, d: _block(lands[0], d, cols), name=name)


def _exchange_start(ps, *, name, cols=False):
    blk = ps[0].shape[:-1] + (ps[0].shape[-1] // N_DEV,) if cols else ps[0].shape[1:]
    return _spread_start(ps, [((N_DEV, len(ps)) + blk, ps[0].dtype)], lambda a, ref, d: _block(ref, d, cols),
                         lambda a, lands, d: lands[0].at[d, a], name=name)


def _adamw(parts, w, m, v, *, name, layer=0, prev=None):
    n_rows, c = w.shape
    r = parts.shape[1]
    row_bytes = c * (N_DEV * parts.dtype.itemsize + 7 * 4) * 2
    tr = r
    for cand in (512, 256, 128, 64, 32, 16):
        if r % cand == 0 and cand * row_bytes <= 40 * 1024 * 1024:
            tr = cand
            break
    c1 = 1.0 - ADAM_B1 ** ADAM_STEP
    c2 = 1.0 - ADAM_B2 ** ADAM_STEP

    def body(p_ref, w_ref, m_ref, v_ref, *rest):
        g_ref, d_ref, nm_ref, nv_ref = rest[-4:]
        g = p_ref[0].astype(F32)
        for j in range(1, N_DEV):
            g = g + p_ref[j].astype(F32)
        nm = ADAM_B1 * m_ref[...] + (1.0 - ADAM_B1) * g
        nv = ADAM_B2 * v_ref[...] + (1.0 - ADAM_B2) * (g * g)
        g_ref[...] = g
        nm_ref[...] = nm
        nv_ref[...] = nv
        d_ref[...] = -ADAM_LR * ((nm / c1) / (jnp.sqrt(nv / c2) + ADAM_EPS) + ADAM_WD * w_ref[...])

    off = layer * (r // tr)
    blk = pl.BlockSpec((tr, c), lambda i: (i + off, 0))
    prev = list(prev) if prev is not None else []
    return pl.pallas_call(
        body, name=name,
        grid=(r // tr,),
        in_specs=[pl.BlockSpec((N_DEV, tr, c), lambda i: (0, i, 0)), blk, blk, blk] + [_ANY] * len(prev),
        out_specs=[blk] * 4,
        out_shape=[jax.ShapeDtypeStruct((n_rows, c), F32)] * 4,
        input_output_aliases={4 + j: j for j in range(len(prev))},
        compiler_params=_params("parallel"),
    )(parts, w, m, v, *prev)


_TN_CANDS = (512, 256, 128)
_TK_CANDS = (2048, 1536, 1408, 1024, 768, 512, 256, 128)


def _nn(a, b, name, out_dtype=F32):
    return _mm(a, b, "nn", name=name, out_dtype=out_dtype, tm=a.shape[0], tn=_pick(b.shape[1], _TN_CANDS),
               tk=_pick(a.shape[1], _TK_CANDS))


def _nt(a, b, name, out_dtype=F32):
    return _mm(a, b, "nt", name=name, out_dtype=out_dtype, tm=a.shape[0], tn=_pick(b.shape[0], _TN_CANDS),
               tk=_pick(a.shape[1], _TK_CANDS))


def _tn(a, b, name, out_dtype=BF16):
    return _mm(a, b, "tn", name=name, out_dtype=out_dtype, tm=_pick(a.shape[1], _TN_CANDS),
               tn=_pick(b.shape[1], (1024,) + _TN_CANDS))


def _local_step(x, p, target, rep, weight, emit, n_heads, ff, start_token=None):
    s, d = x.shape
    depth = p.shape[0]
    ff_tc = _pick(ff, (512, 256, 128))
    grads = {}
    rep_grads = {k: [None] * depth for k in ("mix_pre_g", "mix_post_g", "ffn_pre_g", "ffn_post_g", "ple_norm_g")}

    tokens = [start_token]

    def gain(name, i):
        g = rep[name][i:i + 1]
        return g if tokens[0] is None else g + tokens[0]

    def send(name, layer, g):
        token = emit(name, layer, g)
        if token is not None:
            tokens[0] = token if tokens[0] is None else tokens[0] + token

    saved = []
    h = x
    for i in range(depth):
        sv = {"h": h}
        n1, = _rowcall(f"pre_norm{i}", lambda hh, g: _rms(hh, g), [h], [gain("mix_pre_g", i)], [BF16], cols=d)
        sv["n1"] = n1
        if i % 2 == 0:
            proj = _nn(n1, weight("w_in_even", 0, n1), f"in_even{i}")
            a_out = _sb_fwd(proj, n_heads, name=f"sb_fwd{i}")
            b_out, oraw, states = _hg_fwd(proj, rep["hg_lb_logits"], rep["hg_norm_g"], n_heads, name=f"hg_fwd{i}")
            cat = jnp.concatenate([a_out.astype(BF16), b_out], axis=1)
            m = _nn(cat, weight("w_out_even", 0, cat), f"out_even{i}")
            sv.update(proj=proj, oraw=oraw, states=states, cat=cat)
        else:
            proj = _nn(n1, weight("w_in_odd", 0, n1), f"in_odd{i}")
            sm = {k: weight(k, 0, proj) for k in _SMALL}
            xc, ra, ix = _rg_gates_fwd(proj, sm["conv_w"], sm["conv_b"], sm["rg_wa"], sm["rg_ba"],
                                       sm["rg_wx"], sm["rg_bx"], name=f"rg_gates_fwd{i}")
            hs, gact = _rg_scan_fwd(proj, xc, ra, ix, sm["rg_lambda"], name=f"rg_scan_fwd{i}")
            m = _nn(gact, weight("w_out_odd", 0, gact), f"out_odd{i}")
            sv.update(proj=proj, xc=xc, ra=ra, ix=ix, hs=hs, gact=gact, sm=sm)

        def post_mix(hh, mm, g_post, g_pre):
            h1 = hh + _rms(mm, g_post)
            return h1, _rms(h1, g_pre)

        h1, n2 = _rowcall(f"post_mix{i}", post_mix, [h, m], [gain("mix_post_g", i), gain("ffn_pre_g", i)],
                          [F32, BF16], cols=d)
        gu = _nn(n2, weight("w_gate_up", i, n2), f"gate_up{i}")
        act, = _rowcall(f"swiglu{i}", lambda g, u: _silu(g) * u, [(gu, 0), (gu, ff // ff_tc)], [], [BF16],
                        cols=ff, tc=ff_tc)
        f = _nn(act, weight("w_down", i, act), f"down{i}")

        def post_ffn(hh, ff_out, g_post):
            h2 = hh + _rms(ff_out, g_post)
            return h2, h2

        h2, h2b = _rowcall(f"post_ffn{i}", post_ffn, [h1, f], [gain("ffn_post_g", i)], [F32, BF16], cols=d)
        e = _nn(p[i], weight("w_ple_up", i, h2b), f"ple_up{i}")
        gl = _nn(h2b, weight("w_ple_gate", i, h2b), f"ple_gate{i}")
        h3, = _rowcall(f"ple{i}", lambda hh, a, b, g: hh + _rms(_sigmoid(a) * b, g), [h2, gl, e],
                       [gain("ple_norm_g", i)], [F32], cols=d)
        sv.update(m=m, h1=h1, n2=n2, gu=gu, act=act, f=f, h2b=h2b, e=e, gl=gl)
        saved.append(sv)
        h = h3

    def loss_fn(y, t):
        err = y - t
        return err * (1.0 / d), jnp.sum(err * err, axis=0, keepdims=True) * (0.5 / d)

    dh, loss_cols = _rowcall("loss", loss_fn, [h, target], [], [F32], red_rows=(1,), cols=d)

    for i in reversed(range(depth)):
        sv = saved[i]

        def ple_bwd(dy, a, b, g):
            _, vjp = jax.vjp(lambda a_, b_, g_: _rms(_sigmoid(a_) * b_, g_), a, b, g)
            return vjp(dy)

        dgl, de, rep_grads["ple_norm_g"][i] = _rowcall(
            f"ple_bwd{i}", ple_bwd, [dh, sv["gl"], sv["e"]], [gain("ple_norm_g", i)], [BF16, BF16],
            red_rows=(1,), cols=d)
        send("w_ple_up", i, _tn(p[i], de, f"d_ple_up{i}"))
        send("w_ple_gate", i, _tn(sv["h2b"], dgl, f"d_ple_gate{i}"))
        dh2_ple = _nt(dgl, weight("w_ple_gate", i, dgl), f"dx_ple_gate{i}")

        def post_ffn_bwd(dy, dx, ff_out, g):
            dh2 = dy + dx
            _, vjp = jax.vjp(_rms, ff_out, g)
            df, dg = vjp(dh2)
            return dh2, df, dg

        dh2, df, rep_grads["ffn_post_g"][i] = _rowcall(
            f"post_ffn_bwd{i}", post_ffn_bwd, [dh, dh2_ple, sv["f"]], [gain("ffn_post_g", i)], [F32, BF16],
            red_rows=(1,), cols=d)
        send("w_down", i, _tn(sv["act"], df, f"d_down{i}"))
        dact = _nt(df, weight("w_down", i, df), f"dx_down{i}")

        def swiglu_bwd(g, u, dy):
            _, vjp = jax.vjp(lambda g_, u_: _silu(g_) * u_, g, u)
            return vjp(dy)

        dg_, du_ = _rowcall(f"swiglu_bwd{i}", swiglu_bwd, [(sv["gu"], 0), (sv["gu"], ff // ff_tc), dact], [],
                            [BF16, BF16], cols=ff, tc=ff_tc)
        dgu = jnp.concatenate([dg_, du_], axis=1)
        send("w_gate_up", i, _tn(sv["n2"], dgu, f"d_gate_up{i}"))
        dn2 = _nt(dgu, weight("w_gate_up", i, dgu), f"dx_gate_up{i}")

        def post_mix_bwd(dy, dn, h1, mm, g_post, g_pre):
            _, vjp_pre = jax.vjp(_rms, h1, g_pre)
            dh1_n, dg_pre = vjp_pre(dn)
            dh1 = dy + dh1_n
            _, vjp_post = jax.vjp(_rms, mm, g_post)
            dm, dg_post = vjp_post(dh1)
            return dh1, dm, dg_pre, dg_post

        dh1, dm, rep_grads["ffn_pre_g"][i], rep_grads["mix_post_g"][i] = _rowcall(
            f"post_mix_bwd{i}", post_mix_bwd, [dh2, dn2, sv["h1"], sv["m"]],
            [gain("mix_post_g", i), gain("ffn_pre_g", i)], [F32, BF16], red_rows=(1, 1), cols=d)

        if i % 2 == 0:
            send("w_out_even", 0, _tn(sv["cat"], dm, f"d_out_even{i}"))
            dcat = _nt(dm, weight("w_out_even", 0, dm), f"dx_out_even{i}")
            dq, dk, dv = _sb_bwd(sv["proj"], dcat, n_heads, name=f"sb_bwd{i}")
            dhq, dhf, dhi, dhg, grads["hg_lb_logits"], grads["hg_norm_g"] = _hg_bwd(
                sv["proj"], rep["hg_lb_logits"], rep["hg_norm_g"], sv["oraw"], sv["states"], dcat, n_heads,
                name=f"hg_bwd{i}")
            dproj = jnp.concatenate([dq.astype(BF16), dk.astype(BF16), dv.astype(BF16), dhq, dhf, dhi, dhg], axis=1)
            send("w_in_even", 0, _tn(sv["n1"], dproj, f"d_in_even{i}"))
            dn1 = _nt(dproj, weight("w_in_even", 0, dproj), f"dx_in_even{i}")
        else:
            sm = sv["sm"]
            send("w_out_odd", 0, _tn(sv["gact"], dm, f"d_out_odd{i}"))
            dgo = _nt(dm, weight("w_out_odd", 0, dm), f"dx_out_odd{i}")
            dgate, dra, dix, dxc1, grads["rg_lambda"] = _rg_scan_bwd(
                dgo, sv["proj"], sv["hs"], sv["xc"], sv["ra"], sv["ix"], sm["rg_lambda"], name=f"rg_scan_bwd{i}")
            (dxb, grads["conv_w"], grads["conv_b"], grads["rg_wa"], grads["rg_ba"], grads["rg_wx"],
             grads["rg_bx"]) = _rg_gates_bwd(dra, dix, dxc1, sv["xc"], sv["proj"], sm["conv_w"], sm["rg_wa"],
                                            sm["rg_wx"], name=f"rg_gates_bwd{i}")
            send("small", 0, {k: grads.pop(k) for k in _SMALL})
            dproj = jnp.concatenate([dgate, dxb], axis=1)
            send("w_in_odd", 0, _tn(sv["n1"], dproj, f"d_in_odd{i}"))
            dn1 = _nt(dproj, weight("w_in_odd", 0, dproj), f"dx_in_odd{i}")

        def pre_norm_bwd(dy, dn, hh, g):
            _, vjp = jax.vjp(_rms, hh, g)
            dx, dg = vjp(dn)
            return dy + dx, dg

        dh, rep_grads["mix_pre_g"][i] = _rowcall(
            f"pre_norm_bwd{i}", pre_norm_bwd, [dh1, dn1, sv["h"]], [gain("mix_pre_g", i)], [F32],
            red_rows=(1,), cols=d)

    for k, rows in rep_grads.items():
        grads[k] = jnp.concatenate(rows, axis=0)
    return loss_cols, dh, grads


_WEIGHTS = ("mix_pre_g", "mix_post_g", "ffn_pre_g", "ffn_post_g", "ple_norm_g", "w_in_even", "w_out_even",
            "hg_lb_logits", "hg_norm_g", "w_in_odd", "conv_w", "conv_b", "rg_wa", "rg_ba", "rg_wx", "rg_bx",
            "rg_lambda", "w_out_odd", "w_gate_up", "w_down", "w_ple_up", "w_ple_gate")
_REPLICATED = ("mix_pre_g", "mix_post_g", "ffn_pre_g", "ffn_post_g", "ple_norm_g", "hg_lb_logits", "hg_norm_g")
_SMALL = ("conv_w", "conv_b", "rg_wa", "rg_ba", "rg_wx", "rg_bx", "rg_lambda")
_BIG = {"w_in_even": True, "w_out_even": False, "w_in_odd": True, "w_out_odd": False,
        "w_gate_up": True, "w_down": False, "w_ple_up": True, "w_ple_gate": False}
_PACK_ROW = SUBLANES * LANES


def _pack(arrays):
    flat = jnp.concatenate([a.reshape(-1) for a in arrays])
    pad = -flat.shape[0] % _PACK_ROW
    return jnp.pad(flat, (0, pad)).reshape(-1, LANES)


def _pack_blocks(arrays):
    flat = jnp.concatenate([a.reshape(N_DEV, -1) for a in arrays], axis=1)
    pad = -flat.shape[1] % _PACK_ROW
    return jnp.pad(flat, ((0, 0), (0, pad))).reshape(N_DEV, -1, LANES)


def _unpack(packed, shapes, lead=()):
    flat = packed.reshape(lead + (-1,))
    out, pos = [], 0
    for shape in shapes:
        n = math.prod(shape)
        out.append(flat[..., pos:pos + n].reshape(lead + tuple(shape)))
        pos += n
    return out


def _to_full_small(name, blocks):
    if name == "conv_w":
        return jnp.transpose(blocks, (1, 0, 2)).reshape(blocks.shape[1], -1)
    if name in ("conv_b", "rg_lambda"):
        return blocks.reshape(1, -1)
    nb = blocks.shape[1]
    if name in ("rg_wa", "rg_wx"):
        return jnp.transpose(blocks, (1, 0, 2, 3)).reshape(nb, RG_BLOCK, RG_BLOCK)
    return jnp.transpose(blocks, (1, 0, 2)).reshape(nb, 1, RG_BLOCK)


def _to_blocks_small(name, full):
    if name == "conv_w":
        return jnp.transpose(full.reshape(full.shape[0], N_DEV, -1), (1, 0, 2))
    if name in ("conv_b", "rg_lambda"):
        return full.reshape(N_DEV, -1)
    nb = full.shape[0]
    if name in ("rg_wa", "rg_wx"):
        return jnp.transpose(full.reshape(nb, N_DEV, RG_BLOCK // N_DEV, RG_BLOCK), (1, 0, 2, 3))
    return jnp.transpose(full.reshape(nb, N_DEV, RG_BLOCK // N_DEV), (1, 0, 2))


def _step(inp):
    w = {k: inp[k] for k in _WEIGHTS}
    x, p, target = inp["x"][0], inp["p"][:, 0], inp["loss_target"][0]
    assert w["hg_lb_logits"].shape[0] == 2 and w["w_in_even"].shape[0] == 1 and w["w_in_odd"].shape[0] == 1

    n_heads = w["w_in_even"].shape[2] * N_DEV // (7 * HEAD_DIM)
    ff = w["w_down"].shape[1] * N_DEV
    small_shapes = [w[k].shape[1:] for k in _SMALL]

    def lands_in_place(name):
        return _BIG[name] and w[name].shape[2] % LANES == 0

    depth = p.shape[0]
    order = [("w_in_even", 0), ("w_out_even", 0)] if depth else []
    for i in range(depth):
        if i == 1:
            order += [("w_in_odd", 0), ("small", 0), ("w_out_odd", 0)]
        order += [("w_gate_up", i), ("w_down", i), ("w_ple_up", i), ("w_ple_gate", i)]
    gathers = {}
    for name, l in order:
        if name == "small":
            gathers[name, l] = _gather_start(_pack([w[k][0] for k in _SMALL]), name="gather_small")
        else:
            gathers[name, l] = _gather_start(w[name][l].astype(BF16), name=f"gather_{name}{l}",
                                             cols=lands_in_place(name))
    ready = {}

    def weight(name, layer, after):
        key = ("small", 0) if name in _SMALL else (name, layer)
        if key not in ready:
            land, = _spread_wait(gathers[key], after, name=f"gathered_{key[0]}{key[1]}")
            if name in _SMALL:
                ready[key] = {k: _to_full_small(k, b)
                              for k, b in zip(_SMALL, _unpack(land, small_shapes, lead=(N_DEV,)))}
            elif lands_in_place(name):
                ready[key] = land
            elif _BIG[name]:
                ready[key] = jnp.transpose(land, (1, 0, 2)).reshape(land.shape[1], -1)
            else:
                ready[key] = land.reshape(-1, land.shape[2])
        return ready[key][name] if name in _SMALL else ready[key]

    exchanges = []

    def emit(name, layer, g):
        if name == "small":
            handle = _exchange_start([_pack_blocks([_to_blocks_small(k, g[k]) for k in _SMALL])],
                                     name="exchange_small")
            exchanges.append((name, layer, handle))
            return handle["token"]
        _, r, c = w[name].shape
        if lands_in_place(name):
            handle = _exchange_start([g], name=f"exchange_{name}{layer}", cols=True)
        elif _BIG[name]:
            handle = _exchange_start([jnp.transpose(g.reshape(-1, N_DEV, c), (1, 0, 2))],
                                     name=f"exchange_{name}{layer}")
        else:
            handle = _exchange_start([g.reshape(N_DEV, r, c)], name=f"exchange_{name}{layer}")
        exchanges.append((name, layer, handle))
        return handle["token"]

    rep = {k: w[k] for k in _REPLICATED}
    start_token = sum(h["token"] for h in gathers.values())
    loss_cols, dx, grads = _local_step(x, p, target, rep, weight, emit, n_heads, ff, start_token)
    loss = lax.psum(jnp.sum(loss_cols), MESH_AXES)

    rep_gather = _gather_start(_pack([grads[k] for k in _REPLICATED]), name="gather_rep_grads")

    out = {}
    after = dx
    for name, layer, handle in exchanges:
        land, = _spread_wait(handle, after, name=f"exchanged_{name}{layer}")
        if name == "small":
            res = _adamw(land.reshape(N_DEV, -1, LANES), *[_pack([inp[pre + k][0] for k in _SMALL]) for pre in ("", "m_", "v_")],
                         name="adamw_small")
            for k, *vals in zip(_SMALL, *[_unpack(a, small_shapes) for a in res]):
                out[k] = [v[None] for v in vals]
        else:
            n_l, r, c = w[name].shape
            res = out[name] = _adamw(land.reshape(N_DEV, r, c),
                                     *[inp[pre + name].reshape(n_l * r, c) for pre in ("", "m_", "v_")],
                                     name=f"adamw_{name}{layer}", layer=layer, prev=out.get(name))
        after = res[0]
    for name in _BIG:
        out[name] = [a.reshape(w[name].shape) for a in out[name]]

    rep_shapes = [w[k].shape for k in _REPLICATED]
    rep_parts, = _spread_wait(rep_gather, after, name="gathered_rep_grads")
    res = _adamw(rep_parts, *[_pack([inp[pre + k] for k in _REPLICATED]) for pre in ("", "m_", "v_")],
                 name="adamw_rep")
    for k, *vals in zip(_REPLICATED, *[_unpack(a, rep_shapes) for a in res]):
        out[k] = vals

    return (loss, dx[None]) + tuple(out[k][j] for j in range(4) for k in _WEIGHTS)


def kernel(x, p, mix_pre_g, mix_post_g, ffn_pre_g, ffn_post_g, ple_norm_g, w_in_even, w_out_even, hg_lb_logits, hg_norm_g, w_in_odd, conv_w, conv_b, rg_wa, rg_ba, rg_wx, rg_bx, rg_lambda, w_out_odd, w_gate_up, w_down, w_ple_up, w_ple_gate, loss_target, m_mix_pre_g, m_mix_post_g, m_ffn_pre_g, m_ffn_post_g, m_ple_norm_g, m_w_in_even, m_w_out_even, m_hg_lb_logits, m_hg_norm_g, m_w_in_odd, m_conv_w, m_conv_b, m_rg_wa, m_rg_ba, m_rg_wx, m_rg_bx, m_rg_lambda, m_w_out_odd, m_w_gate_up, m_w_down, m_w_ple_up, m_w_ple_gate, v_mix_pre_g, v_mix_post_g, v_ffn_pre_g, v_ffn_post_g, v_ple_norm_g, v_w_in_even, v_w_out_even, v_hg_lb_logits, v_hg_norm_g, v_w_in_odd, v_conv_w, v_conv_b, v_rg_wa, v_rg_ba, v_rg_wx, v_rg_bx, v_rg_lambda, v_w_out_odd, v_w_gate_up, v_w_down, v_w_ple_up, v_w_ple_gate):
    return _step(dict(locals()))
```

```python
import functools
import math

import jax
import jax.numpy as jnp
from jax import lax
from jax.experimental import pallas as pl
from jax.experimental.pallas import tpu as pltpu

F32 = jnp.float32
BF16 = jnp.bfloat16

VMEM_LIMIT_BYTES = 56 * 1024 * 1024
LANES = 128
SUBLANES = 8

N_DEV = 8
HEAD_DIM = 128
SB_Q_TILE = 512
SB_K_TILE = 128
HG_CHUNK = 32
RG_BLOCK = 256
CONV_TAPS = 4
RG_C = 8.0
RMS_EPS = 1e-6

ADAM_LR = 0.001
ADAM_B1 = 0.9
ADAM_B2 = 0.999
ADAM_EPS = 1e-08
ADAM_WD = 0.01
ADAM_STEP = 10

MESH_AXES = ("x", "y", "c")


def _params(*sem):
    return pltpu.CompilerParams(dimension_semantics=sem, vmem_limit_bytes=VMEM_LIMIT_BYTES)


def _pick(n, cands):
    for c in cands:
        if c <= n and n % c == 0:
            return c
    return n


def _mm(a, b, mode, *, name, out_dtype=F32, tm=512, tn=512, tk=None):
    if mode == "nn":
        (m, k), (k2, n) = a.shape, b.shape
    elif mode == "nt":
        (m, k), (n, k2) = a.shape, b.shape
    else:
        (k, m), (k2, n) = a.shape, b.shape
    assert k == k2, (a.shape, b.shape, mode)
    tm, tn = min(tm, m), min(tn, n)
    tk = k if tk is None else min(tk, k)
    assert m % tm == 0 and n % tn == 0 and k % tk == 0, (m, n, k, tm, tn, tk)
    nk = k // tk

    if mode == "tn":
        assert nk == 1
        return _mm_tn(a, b, name=name, out_dtype=out_dtype, tm=tm, tn=tn)

    a_spec = pl.BlockSpec((tm, tk), lambda i, j, kk: (i, kk))
    if mode == "nn":
        b_spec = pl.BlockSpec((tk, tn), lambda i, j, kk: (kk, j))
        dims = (((1,), (0,)), ((), ()))
    else:
        b_spec = pl.BlockSpec((tn, tk), lambda i, j, kk: (j, kk))
        dims = (((1,), (1,)), ((), ()))

    def body(a_ref, b_ref, o_ref, *acc):
        part = lax.dot_general(a_ref[...].astype(BF16), b_ref[...].astype(BF16), dims, preferred_element_type=F32)
        if nk == 1:
            o_ref[...] = part.astype(out_dtype)
        else:
            acc_ref, = acc
            kk = pl.program_id(2)

            @pl.when(kk == 0)
            def _():
                acc_ref[...] = part

            @pl.when(kk > 0)
            def _():
                acc_ref[...] += part

            @pl.when(kk == nk - 1)
            def _():
                o_ref[...] = acc_ref[...].astype(out_dtype)

    return pl.pallas_call(
        body, name=name,
        grid=(m // tm, n // tn, nk),
        in_specs=[a_spec, b_spec],
        out_specs=pl.BlockSpec((tm, tn), lambda i, j, kk: (i, j)),
        out_shape=jax.ShapeDtypeStruct((m, n), out_dtype),
        scratch_shapes=[] if nk == 1 else [pltpu.VMEM((tm, tn), F32)],
        compiler_params=_params("parallel", "parallel", "arbitrary"),
    )(a, b)


def _mm_tn(a, b, *, name, out_dtype, tm, tn):
    k, m = a.shape
    n = b.shape[1]

    def body(a_ref, b_ref, o_ref, at_ref):
        @pl.when(pl.program_id(1) == 0)
        def _():
            at_ref[...] = a_ref[...].astype(F32).T.astype(BF16)

        o_ref[...] = jnp.dot(at_ref[...], b_ref[...].astype(BF16), preferred_element_type=F32).astype(out_dtype)

    return pl.pallas_call(
        body, name=name,
        grid=(m // tm, n // tn),
        in_specs=[pl.BlockSpec((k, tm), lambda i, j: (0, i)), pl.BlockSpec((k, tn), lambda i, j: (0, j))],
        out_specs=pl.BlockSpec((tm, tn), lambda i, j: (i, j)),
        out_shape=jax.ShapeDtypeStruct((m, n), out_dtype),
        scratch_shapes=[pltpu.VMEM((tm, k), BF16)],
        compiler_params=_params("parallel", "arbitrary"),
    )(a, b)


def _rowcall(name, fn, rows, pars, row_outs, red_rows=(), *, cols, ts=256, tc=None):
    rows = [r if isinstance(r, tuple) else (r, 0) for r in rows]
    pars = [p if isinstance(p, tuple) else (p, 0) for p in pars]
    s = rows[0][0].shape[0]
    tc = cols if tc is None else tc
    ts = min(ts, s)
    assert s % ts == 0 and cols % tc == 0, (name, s, ts, cols, tc)
    n_in, n_row_out = len(rows) + len(pars), len(row_outs)

    def body(*refs):
        outs = fn(*[r[...] for r in refs[:n_in]])
        outs = outs if isinstance(outs, (tuple, list)) else (outs,)
        o_refs = refs[n_in:]
        for o_ref, val in zip(o_refs[:n_row_out], outs[:n_row_out]):
            o_ref[...] = val.astype(o_ref.dtype)
        first = pl.program_id(1) == 0
        for o_ref, val in zip(o_refs[n_row_out:], outs[n_row_out:]):
            @pl.when(first)
            def _(o_ref=o_ref, val=val):
                o_ref[...] = val

            @pl.when(jnp.logical_not(first))
            def _(o_ref=o_ref, val=val):
                o_ref[...] += val

    def row_map(off):
        return lambda j, i: (i, j + off)

    def par_map(off):
        return lambda j, i: (0, j + off)

    return pl.pallas_call(
        body, name=name,
        grid=(cols // tc, s // ts),
        in_specs=[pl.BlockSpec((ts, tc), row_map(off)) for _, off in rows]
        + [pl.BlockSpec((p.shape[0], tc), par_map(off)) for p, off in pars],
        out_specs=[pl.BlockSpec((ts, tc), lambda j, i: (i, j)) for _ in row_outs]
        + [pl.BlockSpec((r, tc), lambda j, i: (0, j)) for r in red_rows],
        out_shape=[jax.ShapeDtypeStruct((s, cols), dt) for dt in row_outs]
        + [jax.ShapeDtypeStruct((r, cols), F32) for r in red_rows],
        compiler_params=_params("parallel", "arbitrary"),
    )(*[r for r, _ in rows], *[p for p, _ in pars])


def _rms(x, g):
    return x * lax.rsqrt(jnp.mean(x * x, axis=-1, keepdims=True) + RMS_EPS) * g


def _sigmoid(x):
    return jax.nn.sigmoid(x)


def _silu(x):
    return x * jax.nn.sigmoid(x)


def _gelu(x):
    return 0.5 * x * (1.0 + jnp.tanh(math.sqrt(2.0 / math.pi) * (x + 0.044715 * (x * x * x))))


def _softplus(x):
    return jnp.maximum(x, 0.0) + jnp.log1p(jnp.exp(-jnp.abs(x)))


def _split3(x):
    hi = x.astype(BF16)
    r1 = x - hi.astype(F32)
    mid = r1.astype(BF16)
    lo = (r1 - mid.astype(F32)).astype(BF16)
    return hi, mid, lo


def _xdot(x, t):
    return sum(jnp.dot(p, t, preferred_element_type=F32) for p in _split3(x))


def _xdot_l(t, x):
    return sum(jnp.dot(t, p, preferred_element_type=F32) for p in _split3(x))


_NT = (((1,), (1,)), ((), ()))
_TN = (((0,), (0,)), ((), ()))


def _dot(a, b, dims=None):
    if dims is None:
        return jnp.dot(a.astype(BF16), b.astype(BF16), preferred_element_type=F32)
    return lax.dot_general(a.astype(BF16), b.astype(BF16), dims, preferred_element_type=F32)


def _iota(shape, axis):
    return lax.broadcasted_iota(jnp.int32, shape, axis)


def _sb_tile(qb, kblk, mask, upper, c_rem):
    z = lax.dot_general(qb, kblk, _NT, preferred_element_type=F32)
    soft = jnp.log1p(jnp.exp(-jnp.abs(z)))
    lbeta = jnp.minimum(z, 0.0) - soft
    l1m = jnp.where(mask, -jnp.maximum(z, 0.0) - soft, 0.0)
    rem = _xdot(l1m, upper) + c_rem
    w = jnp.where(mask, jnp.exp(lbeta + rem), 0.0)
    return lbeta, l1m, w


def _sb_tiles(s):
    tq = min(SB_Q_TILE, s)
    return tq, SB_K_TILE, tq // SB_K_TILE


def _sb_fwd(proj, n_heads, *, name):
    s = proj.shape[0]
    t, tk, per_q = _sb_tiles(s)
    scale = HEAD_DIM ** -0.5

    def body(q_ref, k_ref, v_ref, o_ref):
        qi = pl.program_id(1)
        qb = (q_ref[...] * scale).astype(BF16)
        row, col = _iota((t, tk), 0) + qi * t, _iota((t, tk), 1)
        upper = (_iota((tk, tk), 0) > _iota((tk, tk), 1)).astype(BF16)
        n_kb = (qi + 1) * per_q

        def step(j, carry):
            acc, c_rem = carry
            kb = n_kb - 1 - j
            rows = pl.ds(pl.multiple_of(kb * tk, tk), tk)
            kblk = k_ref[rows, :].astype(BF16)
            vblk = v_ref[rows, :].astype(BF16)
            _, l1m, w = _sb_tile(qb, kblk, (col + kb * tk) < row, upper, c_rem)
            acc = acc + jnp.dot(w.astype(BF16), vblk, preferred_element_type=F32)
            return acc, c_rem + jnp.sum(l1m, axis=1, keepdims=True)

        acc, _ = lax.fori_loop(0, n_kb, step, (jnp.zeros((t, HEAD_DIM), F32), jnp.zeros((t, 1), F32)))
        o_ref[...] = acc

    return pl.pallas_call(
        body, name=name,
        grid=(n_heads, s // t),
        in_specs=[pl.BlockSpec((t, HEAD_DIM), lambda h, i: (i, h)),
                  pl.BlockSpec((s, HEAD_DIM), lambda h, i: (0, n_heads + h)),
                  pl.BlockSpec((s, HEAD_DIM), lambda h, i: (0, 2 * n_heads + h))],
        out_specs=pl.BlockSpec((t, HEAD_DIM), lambda h, i: (i, h)),
        out_shape=jax.ShapeDtypeStruct((s, n_heads * HEAD_DIM), F32),
        compiler_params=_params("parallel", "arbitrary"),
    )(proj, proj, proj)


def _sb_bwd(proj, dcat, n_heads, *, name):
    s = proj.shape[0]
    t, tk, per_q = _sb_tiles(s)
    scale = HEAD_DIM ** -0.5

    def body(q_ref, k_ref, v_ref, do_ref, dq_ref, dk_ref, dv_ref, g_s, sig_s):
        qi = pl.program_id(1)

        @pl.when(qi == 0)
        def _():
            dk_ref[...] = jnp.zeros_like(dk_ref)
            dv_ref[...] = jnp.zeros_like(dv_ref)

        qb = (q_ref[...] * scale).astype(BF16)
        dob = do_ref[...].astype(BF16)
        row, col = _iota((t, tk), 0) + qi * t, _iota((t, tk), 1)
        upper = (_iota((tk, tk), 0) > _iota((tk, tk), 1)).astype(BF16)
        lower_incl = (_iota((tk, tk), 0) >= _iota((tk, tk), 1)).astype(BF16)
        n_kb = (qi + 1) * per_q

        def weights(j, carry):
            c_rem, g_all = carry
            kb = n_kb - 1 - j
            rows = pl.ds(pl.multiple_of(kb * tk, tk), tk)
            kblk = k_ref[rows, :].astype(BF16)
            vblk = v_ref[rows, :].astype(BF16)
            lbeta, l1m, w = _sb_tile(qb, kblk, (col + kb * tk) < row, upper, c_rem)
            g = w * lax.dot_general(dob, vblk, _NT, preferred_element_type=F32)
            dv_ref[rows, :] += lax.dot_general(w.astype(BF16), dob, _TN, preferred_element_type=F32)
            g_s[kb] = g
            sig_s[kb] = jnp.exp(lbeta)
            return c_rem + jnp.sum(l1m, axis=1, keepdims=True), g_all + jnp.sum(g, axis=1, keepdims=True)

        zero_col = jnp.zeros((t, 1), F32)
        _, g_all = lax.fori_loop(0, n_kb, weights, (zero_col, zero_col))

        def scores(j, carry):
            dq, c_g = carry
            kb = n_kb - 1 - j
            rows = pl.ds(pl.multiple_of(kb * tk, tk), tk)
            g, sig = g_s[kb], sig_s[kb]
            mask = (col + kb * tk) < row
            g_before = g_all - (_xdot(g, lower_incl) + c_g)
            dz = jnp.where(mask, g * (1.0 - sig) - g_before * sig, 0.0).astype(BF16)
            dq = dq + jnp.dot(dz, k_ref[rows, :].astype(BF16), preferred_element_type=F32)
            dk_ref[rows, :] += lax.dot_general(dz, qb, _TN, preferred_element_type=F32)
            return dq, c_g + jnp.sum(g, axis=1, keepdims=True)

        dq, _ = lax.fori_loop(0, n_kb, scores, (jnp.zeros((t, HEAD_DIM), F32), zero_col))
        dq_ref[...] = dq * scale

    width = n_heads * HEAD_DIM
    return pl.pallas_call(
        body, name=name,
        grid=(n_heads, s // t),
        in_specs=[pl.BlockSpec((t, HEAD_DIM), lambda h, i: (i, h)),
                  pl.BlockSpec((s, HEAD_DIM), lambda h, i: (0, n_heads + h)),
                  pl.BlockSpec((s, HEAD_DIM), lambda h, i: (0, 2 * n_heads + h)),
                  pl.BlockSpec((t, HEAD_DIM), lambda h, i: (i, h))],
        out_specs=[pl.BlockSpec((t, HEAD_DIM), lambda h, i: (i, h)),
                   pl.BlockSpec((s, HEAD_DIM), lambda h, i: (0, h)),
                   pl.BlockSpec((s, HEAD_DIM), lambda h, i: (0, h))],
        out_shape=[jax.ShapeDtypeStruct((s, width), F32)] * 3,
        scratch_shapes=[pltpu.VMEM((s // tk, t, tk), F32)] * 2,
        compiler_params=_params("parallel", "arbitrary"),
    )(proj, proj, proj, dcat)


def _hg_pre(hq, hf, logits):
    mx = jnp.max(logits, axis=0, keepdims=True)
    ex = jnp.exp(logits - mx)
    lb = ex[0:1, :] / jnp.sum(ex, axis=0, keepdims=True)
    f = lb + (1.0 - lb) * _sigmoid(hf)
    return _silu(hq), 1.0 - f, jnp.log(f)


def _hg_post(o, norm_g, hgate):
    return _rms(o, norm_g) * _silu(hgate)


def _hg_specs(s, n_heads, first_block):
    def at(group):
        return pl.BlockSpec((s, HEAD_DIM), lambda h: (0, first_block + group * n_heads + h))
    return [at(0), at(1), at(2), at(3)]


def _hg_fwd(proj, logits, norm_g, n_heads, *, name):
    s = proj.shape[0]
    hc = HG_CHUNK
    n_chunks = s // hc
    d = HEAD_DIM

    def body(lg_ref, ng_ref, hq_ref, hf_ref, hi_ref, hgt_ref, out_ref, oraw_ref, st_ref,
             q_s, k_s, lf_s, cum_s, qc_s, oc_s):
        q, k, lf = _hg_pre(hq_ref[...], hf_ref[...], lg_ref[...])
        q_s[...] = q
        k_s[...] = k
        lf_s[...] = lf
        tril = (_iota((hc, hc), 0) >= _iota((hc, hc), 1)).astype(BF16)
        srow = _iota((hc, d), 0)

        def chunk(ci, st):
            rows = pl.ds(pl.multiple_of(ci * hc, hc), hc)
            q, k, v = q_s[rows, :], k_s[rows, :], hi_ref[rows, :]
            cum = _xdot_l(tril, lf_s[rows, :])
            st_ref[0, ci] = st
            o_inter = _dot(q * jnp.exp(cum), st, _NT)
            cum_s[...] = cum
            qc_s[...] = q
            for t in range(hc):
                ng = (t // SUBLANES + 1) * SUBLANES
                e = jnp.where(srow[:ng] <= t, jnp.exp(cum_s[t:t + 1, :] - cum[:ng]), 0.0)
                sc = jnp.sum(qc_s[t:t + 1, :] * k[:ng] * e, axis=1, keepdims=True)
                oc_s[t:t + 1, :] = jnp.sum(sc * v[:ng], axis=0, keepdims=True)
            oraw_ref[rows, :] = o_inter + oc_s[...]
            last = cum_s[hc - 1:hc, :]
            return st * jnp.exp(last) + _dot(v, k * jnp.exp(last - cum), _TN)

        lax.fori_loop(0, n_chunks, chunk, jnp.zeros((d, d), F32))
        out_ref[...] = _hg_post(oraw_ref[...], ng_ref[...], hgt_ref[...]).astype(BF16)

    width = n_heads * d
    head_block = pl.BlockSpec((s, d), lambda h: (0, h))
    return pl.pallas_call(
        body, name=name,
        grid=(n_heads,),
        in_specs=[pl.BlockSpec((2, d), lambda h: (0, h)), pl.BlockSpec((1, d), lambda h: (0, 0))]
        + _hg_specs(s, n_heads, 3 * n_heads),
        out_specs=[head_block, head_block, pl.BlockSpec((1, n_chunks, d, d), lambda h: (h, 0, 0, 0))],
        out_shape=[jax.ShapeDtypeStruct((s, width), BF16), jax.ShapeDtypeStruct((s, width), F32),
                   jax.ShapeDtypeStruct((n_heads, n_chunks, d, d), F32)],
        scratch_shapes=[pltpu.VMEM((s, d), F32)] * 3 + [pltpu.VMEM((hc, d), F32)] * 3,
        compiler_params=_params("arbitrary"),
    )(logits, norm_g, proj, proj, proj, proj)


def _hg_bwd(proj, logits, norm_g, oraw, states, dcat, n_heads, *, name):
    s = proj.shape[0]
    hc = HG_CHUNK
    n_chunks = s // hc
    d = HEAD_DIM

    def body(lg_ref, ng_ref, hq_ref, hf_ref, hi_ref, hgt_ref, oraw_ref, st_ref, dout_ref,
             dhq_ref, dhf_ref, dhi_ref, dhgt_ref, dlg_ref, dng_ref,
             q_s, k_s, lf_s, do_s, dq_s, dk_s, dlf_s, cum_s, qc_s, doc_s, dqc_s, dkc_s, dvc_s):
        head = pl.program_id(0)
        (q, k, lf), pre_vjp = jax.vjp(_hg_pre, hq_ref[...], hf_ref[...], lg_ref[...])
        q_s[...] = q
        k_s[...] = k
        lf_s[...] = lf
        _, post_vjp = jax.vjp(_hg_post, oraw_ref[...], ng_ref[...], hgt_ref[...])
        do, dng, dhgt = post_vjp(dout_ref[...])
        do_s[...] = do
        dhgt_ref[...] = dhgt.astype(BF16)

        @pl.when(head == 0)
        def _():
            dng_ref[...] = dng

        @pl.when(head > 0)
        def _():
            dng_ref[...] += dng

        triu = (_iota((hc, hc), 0) <= _iota((hc, hc), 1)).astype(BF16)
        tril = (_iota((hc, hc), 0) >= _iota((hc, hc), 1)).astype(BF16)
        srow = _iota((hc, d), 0)

        def chunk(j, dst):
            ci = n_chunks - 1 - j
            rows = pl.ds(pl.multiple_of(ci * hc, hc), hc)
            q, k, v, do_c = q_s[rows, :], k_s[rows, :], hi_ref[rows, :], do_s[rows, :]
            cum = _xdot_l(tril, lf_s[rows, :])
            st = st_ref[0, ci]
            cum_s[...] = cum
            qc_s[...] = q
            doc_s[...] = do_c
            last = cum_s[hc - 1:hc, :]
            e_cum, e_last = jnp.exp(cum), jnp.exp(last - cum)
            dqc_s[...] = _dot(do_c, st) * e_cum
            dk_state = _dot(v, dst) * e_last
            dkc_s[...] = dk_state
            dvc_s[...] = _dot(k * e_last, dst, _NT)
            d_last = (jnp.sum(dst * st, axis=0, keepdims=True) * jnp.exp(last)
                      + jnp.sum(k * dk_state, axis=0, keepdims=True))
            for t in range(hc):
                ng = (t // SUBLANES + 1) * SUBLANES
                qt, dot_ = qc_s[t:t + 1, :], doc_s[t:t + 1, :]
                e = jnp.where(srow[:ng] <= t, jnp.exp(cum_s[t:t + 1, :] - cum[:ng]), 0.0)
                ke = k[:ng] * e
                d_a = jnp.sum(dot_ * v[:ng], axis=1, keepdims=True)
                dqc_s[t:t + 1, :] += jnp.sum(d_a * ke, axis=0, keepdims=True)
                dkc_s[0:ng, :] += d_a * (qt * e)
                dvc_s[0:ng, :] += jnp.sum(qt * ke, axis=1, keepdims=True) * dot_
            dq, dk = dqc_s[...], dkc_s[...]
            d_b = q * dq - k * dk
            dq_s[rows, :] = dq
            dk_s[rows, :] = dk
            dhi_ref[rows, :] = dvc_s[...].astype(BF16)
            dlf_s[rows, :] = _xdot_l(triu, d_b) + d_last
            return dst * jnp.exp(last) + _dot(do_c, q * e_cum, _TN)

        lax.fori_loop(0, n_chunks, chunk, jnp.zeros((d, d), F32))
        dhq, dhf, dlg = pre_vjp((dq_s[...], dk_s[...], dlf_s[...]))
        dhq_ref[...] = dhq.astype(BF16)
        dhf_ref[...] = dhf.astype(BF16)
        dlg_ref[...] = dlg

    width = n_heads * d
    head_block = pl.BlockSpec((s, d), lambda h: (0, h))
    return pl.pallas_call(
        body, name=name,
        grid=(n_heads,),
        in_specs=[pl.BlockSpec((2, d), lambda h: (0, h)), pl.BlockSpec((1, d), lambda h: (0, 0))]
        + _hg_specs(s, n_heads, 3 * n_heads)
        + [head_block, pl.BlockSpec((1, n_chunks, d, d), lambda h: (h, 0, 0, 0)),
           pl.BlockSpec((s, d), lambda h: (0, n_heads + h))],
        out_specs=[head_block] * 4 + [pl.BlockSpec((2, d), lambda h: (0, h)), pl.BlockSpec((1, d), lambda h: (0, 0))],
        out_shape=[jax.ShapeDtypeStruct((s, width), BF16)] * 4
        + [jax.ShapeDtypeStruct((2, width), F32), jax.ShapeDtypeStruct((1, d), F32)],
        scratch_shapes=[pltpu.VMEM((s, d), F32)] * 7 + [pltpu.VMEM((hc, d), F32)] * 6,
        compiler_params=_params("arbitrary"),
    )(logits, norm_g, proj, proj, proj, proj, oraw, states, dcat)


def _shift_down(x, n, srow):
    if n == 0:
        return x
    return jnp.where(srow >= n, pltpu.roll(x, n, 0), 0.0)


def _shift_up(x, n, srow):
    if n == 0:
        return x
    s = x.shape[0]
    return jnp.where(srow < s - n, pltpu.roll(x, s - n, 0), 0.0)


def _rg_gates_fwd(proj, conv_w, conv_b, wa, ba, wx, bx, *, name):
    s = proj.shape[0]
    nb = wa.shape[0]
    bw = RG_BLOCK

    def body(xb_ref, cw_ref, cb_ref, wa_ref, ba_ref, wx_ref, bx_ref, xc_ref, ra_ref, ix_ref):
        x = xb_ref[...]
        srow = _iota((s, bw), 0)
        cw = cw_ref[...]
        xc = cb_ref[...] + cw[0:1, :] * x
        for tap in range(1, CONV_TAPS):
            xc = xc + cw[tap:tap + 1, :] * _shift_down(x, tap, srow)
        xc_ref[...] = xc
        ra_ref[...] = _dot(xc, wa_ref[0]) + ba_ref[0]
        ix_ref[...] = _dot(xc, wx_ref[0]) + bx_ref[0]

    col = pl.BlockSpec((s, bw), lambda n: (0, n))
    vec = lambda r: pl.BlockSpec((r, bw), lambda n: (0, n))
    mat = pl.BlockSpec((1, bw, bw), lambda n: (n, 0, 0))
    bias = pl.BlockSpec((1, 1, bw), lambda n: (n, 0, 0))
    return pl.pallas_call(
        body, name=name,
        grid=(nb,),
        in_specs=[pl.BlockSpec((s, bw), lambda n: (0, nb + n)), vec(CONV_TAPS), vec(1), mat, bias, mat, bias],
        out_specs=[col] * 3,
        out_shape=[jax.ShapeDtypeStruct((s, nb * bw), F32)] * 3,
        compiler_params=_params("parallel"),
    )(proj, conv_w, conv_b, wa, ba, wx, bx)


def _rg_au(ra, ix, xc, lam, first_row):
    log_a = -RG_C * _sigmoid(ra) * _softplus(-lam)
    th = jnp.tanh(log_a)
    one_minus_a2 = -2.0 * th / (1.0 - th)
    mult = jnp.where(first_row, 1.0, jnp.sqrt(one_minus_a2))
    return jnp.exp(log_a), xc * _sigmoid(ix) * mult


def _rg_out(gate, hs):
    return _gelu(gate) * hs


def _scan_rows(n_groups, reverse, group_fn, init):
    def group(gi, carry):
        g = (n_groups - 1 - gi) if reverse else gi
        return group_fn(pl.multiple_of(g * SUBLANES, SUBLANES), carry)
    return lax.fori_loop(0, n_groups, group, init)


def _rg_scan_fwd(proj, xc, ra, ix, lam, *, name):
    s, width = xc.shape
    tc = LANES

    def body(gate_ref, xc_ref, ra_ref, ix_ref, lam_ref, hs_ref, gact_ref, a_s, u_s):
        first_row = _iota((s, tc), 0) == 0
        a, u = _rg_au(ra_ref[...], ix_ref[...], xc_ref[...], lam_ref[...], first_row)
        a_s[...] = a
        u_s[...] = u
        r8 = _iota((SUBLANES, tc), 0)

        def rows(r0, h):
            ag, ug = a_s[pl.ds(r0, SUBLANES), :], u_s[pl.ds(r0, SUBLANES), :]
            tile = jnp.zeros((SUBLANES, tc), F32)
            for r in range(SUBLANES):
                h = ag[r:r + 1, :] * h + ug[r:r + 1, :]
                tile = jnp.where(r8 == r, h, tile)
            hs_ref[pl.ds(r0, SUBLANES), :] = tile
            return h

        _scan_rows(s // SUBLANES, False, rows, jnp.zeros((1, tc), F32))
        gact_ref[...] = _rg_out(gate_ref[...], hs_ref[...]).astype(BF16)

    col = pl.BlockSpec((s, tc), lambda n: (0, n))
    return pl.pallas_call(
        body, name=name,
        grid=(width // tc,),
        in_specs=[col, col, col, col, pl.BlockSpec((1, tc), lambda n: (0, n))],
        out_specs=[col, col],
        out_shape=[jax.ShapeDtypeStruct((s, width), F32), jax.ShapeDtypeStruct((s, width), BF16)],
        scratch_shapes=[pltpu.VMEM((s, tc), F32)] * 2,
        compiler_params=_params("parallel"),
    )(proj, xc, ra, ix, lam)


def _rg_scan_bwd(dgo, proj, hs, xc, ra, ix, lam, *, name):
    s, width = xc.shape
    tc = LANES

    def body(dgo_ref, gate_ref, hs_ref, xc_ref, ra_ref, ix_ref, lam_ref,
             dgate_ref, dra_ref, dix_ref, dxc_ref, dlam_ref, a_s, dh_s, g_s):
        srow = _iota((s, tc), 0)
        hs = hs_ref[...]
        _, out_vjp = jax.vjp(_rg_out, gate_ref[...], hs)
        dgate, dh = out_vjp(dgo_ref[...])
        dgate_ref[...] = dgate.astype(BF16)
        au = functools.partial(_rg_au, first_row=srow == 0)
        (a, _), au_vjp = jax.vjp(au, ra_ref[...], ix_ref[...], xc_ref[...], lam_ref[...])
        a_s[...] = a
        dh_s[...] = dh
        r8 = _iota((SUBLANES, tc), 0)

        def rows(r0, carry):
            g, a_next = carry
            ag, dg = a_s[pl.ds(r0, SUBLANES), :], dh_s[pl.ds(r0, SUBLANES), :]
            tile = jnp.zeros((SUBLANES, tc), F32)
            for r in reversed(range(SUBLANES)):
                g = dg[r:r + 1, :] + a_next * g
                a_next = ag[r:r + 1, :]
                tile = jnp.where(r8 == r, g, tile)
            g_s[pl.ds(r0, SUBLANES), :] = tile
            return g, a_next

        zero = jnp.zeros((1, tc), F32)
        _scan_rows(s // SUBLANES, True, rows, (zero, zero))
        g = g_s[...]
        dra, dix, dxc, dlam = au_vjp((g * _shift_down(hs, 1, srow), g))
        dra_ref[...] = dra.astype(BF16)
        dix_ref[...] = dix.astype(BF16)
        dxc_ref[...] = dxc
        dlam_ref[...] = dlam

    col = pl.BlockSpec((s, tc), lambda n: (0, n))
    vec = pl.BlockSpec((1, tc), lambda n: (0, n))
    return pl.pallas_call(
        body, name=name,
        grid=(width // tc,),
        in_specs=[col] * 6 + [vec],
        out_specs=[col] * 4 + [vec],
        out_shape=[jax.ShapeDtypeStruct((s, width), BF16)] * 3
        + [jax.ShapeDtypeStruct((s, width), F32), jax.ShapeDtypeStruct((1, width), F32)],
        scratch_shapes=[pltpu.VMEM((s, tc), F32)] * 3,
        compiler_params=_params("parallel"),
    )(dgo, proj, hs, xc, ra, ix, lam)


def _rg_gates_bwd(dra, dix, dxc1, xc, proj, conv_w, wa, wx, *, name):
    s = proj.shape[0]
    nb = wa.shape[0]
    bw = RG_BLOCK

    def body(dra_ref, dix_ref, dxc_ref, xc_ref, xb_ref, cw_ref, wa_ref, wx_ref,
             dxb_ref, dcw_ref, dcb_ref, dwa_ref, dba_ref, dwx_ref, dbx_ref):
        dra, dix = dra_ref[...], dix_ref[...]
        xc_t = xc_ref[...].T.astype(BF16)
        dwa_ref[0] = jnp.dot(xc_t, dra, preferred_element_type=F32)
        dwx_ref[0] = jnp.dot(xc_t, dix, preferred_element_type=F32)
        dba_ref[0] = jnp.sum(dra.astype(F32), axis=0, keepdims=True)
        dbx_ref[0] = jnp.sum(dix.astype(F32), axis=0, keepdims=True)
        dxc = dxc_ref[...] + _dot(dra, wa_ref[0], _NT) + _dot(dix, wx_ref[0], _NT)
        srow = _iota((s, bw), 0)
        x = xb_ref[...]
        cw = cw_ref[...]
        dx = cw[0:1, :] * dxc
        dcw = [jnp.sum(dxc * x, axis=0, keepdims=True)]
        for tap in range(1, CONV_TAPS):
            dx = dx + cw[tap:tap + 1, :] * _shift_up(dxc, tap, srow)
            dcw.append(jnp.sum(dxc * _shift_down(x, tap, srow), axis=0, keepdims=True))
        dxb_ref[...] = dx.astype(BF16)
        r4 = _iota((CONV_TAPS, bw), 0)
        acc = jnp.zeros((CONV_TAPS, bw), F32)
        for tap in range(CONV_TAPS):
            acc = jnp.where(r4 == tap, dcw[tap], acc)
        dcw_ref[...] = acc
        dcb_ref[...] = jnp.sum(dxc, axis=0, keepdims=True)

    col = pl.BlockSpec((s, bw), lambda n: (0, n))
    vec = lambda r: pl.BlockSpec((r, bw), lambda n: (0, n))
    mat = pl.BlockSpec((1, bw, bw), lambda n: (n, 0, 0))
    bias = pl.BlockSpec((1, 1, bw), lambda n: (n, 0, 0))
    width = nb * bw
    return pl.pallas_call(
        body, name=name,
        grid=(nb,),
        in_specs=[col, col, col, col, pl.BlockSpec((s, bw), lambda n: (0, nb + n)), vec(CONV_TAPS), mat, mat],
        out_specs=[col, vec(CONV_TAPS), vec(1), mat, bias, mat, bias],
        out_shape=[jax.ShapeDtypeStruct((s, width), BF16), jax.ShapeDtypeStruct((CONV_TAPS, width), F32),
                   jax.ShapeDtypeStruct((1, width), F32), jax.ShapeDtypeStruct((nb, bw, bw), F32),
                   jax.ShapeDtypeStruct((nb, 1, bw), F32), jax.ShapeDtypeStruct((nb, bw, bw), F32),
                   jax.ShapeDtypeStruct((nb, 1, bw), F32)],
        compiler_params=_params("parallel"),
    )(dra, dix, dxc1, xc, proj, conv_w, wa, wx)


_HBM = pl.BlockSpec(memory_space=pltpu.HBM)
_FLIPS = ((0, 0, 1), (1, 0, 0), (0, 1, 0), (1, 1, 0))
_ALL_FLIPS = tuple((a, b, c) for a in (0, 1) for b in (0, 1) for c in (0, 1))[1:]


def _flip(pos, f):
    return tuple(1 - p if b else p for p, b in zip(pos, f))


def _dev_index(pos):
    return 4 * pos[0] + 2 * pos[1] + pos[2]


def _block(ref, idx, cols):
    if not cols:
        return ref.at[idx]
    n = ref.shape[-1] // N_DEV
    start = pl.multiple_of(idx * n, LANES)
    return ref.at[(slice(None),) * (len(ref.shape) - 1) + (pl.ds(start, n),)]


def _all_gather(xs, *, name, cols=False):
    n_arr = len(xs)

    def body(*refs):
        x_refs, out_refs = refs[:n_arr], refs[n_arr:2 * n_arr]
        send_sems, recv_sems, local_sems = refs[2 * n_arr:]
        me = (lax.axis_index("x"), lax.axis_index("y"), lax.axis_index("c"))
        sibling = _flip(me, _FLIPS[0])
        chips = [_flip(me, f) for f in _FLIPS[1:]]

        def copy(a, k, block, to, src=None):
            dst = _block(out_refs[a], _dev_index(block), cols)
            return pltpu.make_async_remote_copy(
                src_ref=dst if src is None else src, dst_ref=dst,
                send_sem=send_sems.at[7 * a + k], recv_sem=recv_sems.at[7 * a + k],
                device_id=to, device_id_type=pl.DeviceIdType.MESH)

        mine = [pltpu.make_async_copy(x_refs[a], _block(out_refs[a], _dev_index(me), cols), local_sems.at[a])
                for a in range(n_arr)]
        for cp in mine:
            cp.start()
        first = []
        for a in range(n_arr):
            first.append(copy(a, 0, me, sibling, src=x_refs[a]))
            first += [copy(a, 1 + j, me, chip, src=x_refs[a]) for j, chip in enumerate(chips)]
        for cp in first:
            cp.start()
        passed = []
        for j, chip in enumerate(chips):
            for a in range(n_arr):
                copy(a, 1 + j, chip, me).wait_recv()
                fwd = copy(a, 4 + j, chip, sibling)
                fwd.start()
                passed.append(fwd)
        for a in range(n_arr):
            copy(a, 0, sibling, me).wait_recv()
            for j, chip in enumerate(chips):
                copy(a, 4 + j, _flip(chip, _FLIPS[0]), me).wait_recv()
        for cp in first + passed:
            cp.wait_send()
        for cp in mine:
            cp.wait()

    def out_shape(x):
        shape = x.shape[:-1] + (N_DEV * x.shape[-1],) if cols else (N_DEV,) + x.shape
        return jax.ShapeDtypeStruct(shape, x.dtype)

    return pl.pallas_call(
        body, name=name,
        in_specs=[_HBM] * n_arr, out_specs=[_HBM] * n_arr,
        out_shape=[out_shape(x) for x in xs],
        scratch_shapes=[pltpu.SemaphoreType.DMA((7 * n_arr,)), pltpu.SemaphoreType.DMA((7 * n_arr,)),
                        pltpu.SemaphoreType.DMA((n_arr,))],
    )(*xs)


def _exchange(ps, *, name, cols=False):
    n_arr = len(ps)
    blk = ps[0].shape[:-1] + (ps[0].shape[-1] // N_DEV,) if cols else ps[0].shape[1:]

    def body(*refs):
        p_refs, out_ref = refs[:n_arr], refs[n_arr]
        send_sems, recv_sems, local_sems = refs[n_arr + 1:]
        me = (lax.axis_index("x"), lax.axis_index("y"), lax.axis_index("c"))
        me_idx = _dev_index(me)
        own = [pltpu.make_async_copy(_block(p_refs[a], me_idx, cols), out_ref.at[me_idx, a], local_sems.at[a])
               for a in range(n_arr)]
        for cp in own:
            cp.start()
        sends = []
        for k, f in enumerate(_ALL_FLIPS):
            peer = _flip(me, f)
            for a in range(n_arr):
                cp = pltpu.make_async_remote_copy(
                    src_ref=_block(p_refs[a], _dev_index(peer), cols), dst_ref=out_ref.at[me_idx, a],
                    send_sem=send_sems.at[7 * a + k], recv_sem=recv_sems.at[7 * a + k],
                    device_id=peer, device_id_type=pl.DeviceIdType.MESH)
                cp.start()
                sends.append(cp)
        for cp in sends:
            cp.wait()
        for cp in own:
            cp.wait()

    return pl.pallas_call(
        body, name=name,
        in_specs=[_HBM] * n_arr, out_specs=_HBM,
        out_shape=jax.ShapeDtypeStruct((N_DEV, n_arr) + blk, ps[0].dtype),
        scratch_shapes=[pltpu.SemaphoreType.DMA((7 * n_arr,)), pltpu.SemaphoreType.DMA((7 * n_arr,)),
                        pltpu.SemaphoreType.DMA((n_arr,))],
    )(*ps)


_SEM = pl.BlockSpec(memory_space=pltpu.SEMAPHORE)
_ANY = pl.BlockSpec(memory_space=pl.ANY)
_N_PEERS = N_DEV - 1


def _hbm(x):
    return pltpu.with_memory_space_constraint(x, pltpu.HBM)


def _spread_copies(src_refs, land_refs, send_sems, recv_sems, local_sems, src_block, dst_block):
    me = (lax.axis_index("x"), lax.axis_index("y"), lax.axis_index("c"))
    me_idx = _dev_index(me)
    local, remote = [], []
    for a, src in enumerate(src_refs):
        local.append(pltpu.make_async_copy(src_block(a, src, me_idx), dst_block(a, land_refs, me_idx), local_sems.at[a]))
    for k, f in enumerate(_ALL_FLIPS):
        peer = _flip(me, f)
        for a, src in enumerate(src_refs):
            remote.append(pltpu.make_async_remote_copy(
                src_ref=src_block(a, src, _dev_index(peer)), dst_ref=dst_block(a, land_refs, me_idx),
                send_sem=send_sems.at[_N_PEERS * a + k], recv_sem=recv_sems.at[_N_PEERS * a + k],
                device_id=peer, device_id_type=pl.DeviceIdType.MESH))
    return local, remote


def _spread_start(srcs, land_shapes, src_block, dst_block, *, name):
    ns, nl = len(srcs), len(land_shapes)

    def body(*refs):
        src_refs, land_refs = refs[:ns], refs[ns:ns + nl]
        send_sems, recv_sems, local_sems = refs[ns + nl:ns + nl + 3]
        local, remote = _spread_copies(src_refs, land_refs, send_sems, recv_sems, local_sems, src_block, dst_block)
        for cp in local + remote:
            cp.start()
        token = refs[-1]
        token[...] = jnp.zeros_like(token)

    lands = [_hbm(lax.empty(shape, dtype)) for shape, dtype in land_shapes]
    out = pl.pallas_call(
        body, name=name,
        in_specs=[_HBM] * (ns + nl),
        out_specs=[_SEM] * 3 + [_HBM] * (ns + nl) + [pl.BlockSpec(memory_space=pltpu.VMEM)],
        out_shape=[pltpu.SemaphoreType.DMA((_N_PEERS * ns,)), pltpu.SemaphoreType.DMA((_N_PEERS * ns,)),
                   pltpu.SemaphoreType.DMA((ns,))]
        + [pltpu.HBM(x.shape, x.dtype) for x in srcs] + [pltpu.HBM(shape, dtype) for shape, dtype in land_shapes]
        + [jax.ShapeDtypeStruct((SUBLANES, LANES), F32)],
        input_output_aliases={i: 3 + i for i in range(ns + nl)},
        compiler_params=pltpu.CompilerParams(has_side_effects=pltpu.SideEffectType.DATAFLOW_SIDE_EFFECTING),
    )(*[_hbm(x) for x in srcs], *lands)
    return dict(sems=out[:3], srcs=out[3:3 + ns], lands=out[3 + ns:3 + ns + nl], token=out[-1][0:1, 0:1],
                src_block=src_block, dst_block=dst_block)


def _spread_wait(handle, after, *, name):
    ns, nl = len(handle["srcs"]), len(handle["lands"])

    def body(*refs):
        src_refs, land_refs = refs[:ns], refs[ns:ns + nl]
        send_sems, recv_sems, local_sems = refs[ns + nl:ns + nl + 3]
        local, remote = _spread_copies(src_refs, land_refs, send_sems, recv_sems, local_sems,
                                       handle["src_block"], handle["dst_block"])
        for cp in local:
            cp.wait()
        for cp in remote:
            cp.wait_send()
            cp.wait_recv()

    out = pl.pallas_call(
        body, name=name,
        in_specs=[_HBM] * (ns + nl) + [_SEM] * 3 + [_ANY],
        out_specs=[_HBM] * (ns + nl),
        out_shape=[pltpu.HBM(x.shape, x.dtype) for x in handle["srcs"] + handle["lands"]],
        input_output_aliases={i: i for i in range(ns + nl)},
        compiler_params=pltpu.CompilerParams(has_side_effects=pltpu.SideEffectType.DATAFLOW_SIDE_EFFECTING),
    )(*handle["srcs"], *handle["lands"], *handle["sems"], after)
    return list(out[ns:])


def _gather_start(x, *, name, cols=False):
    shape = x.shape[:-1] + (N_DEV * x.shape[-1],) if cols else (N_DEV,) + x.shape
    return _spread_start([x], [(shape, x.dtype)], lambda a, ref, d: ref,
                         lambda a, lands, d: _block(lands[0], d, cols), name=name)


def _exchange_start(ps, *, name, cols=False):
    blk = ps[0].shape[:-1] + (ps[0].shape[-1] // N_DEV,) if cols else ps[0].shape[1:]
    return _spread_start(ps, [((N_DEV, len(ps)) + blk, ps[0].dtype)], lambda a, ref, d: _block(ref, d, cols),
                         lambda a, lands, d: lands[0].at[d, a], name=name)


def _adamw(parts, w, m, v, *, name, layer=0, prev=None):
    n_rows, c = w.shape
    r = parts.shape[1]
    row_bytes = c * (N_DEV * parts.dtype.itemsize + 7 * 4) * 2
    tr = r
    for cand in (512, 256, 128, 64, 32, 16):
        if r % cand == 0 and cand * row_bytes <= 40 * 1024 * 1024:
            tr = cand
            break
    c1 = 1.0 - ADAM_B1 ** ADAM_STEP
    c2 = 1.0 - ADAM_B2 ** ADAM_STEP

    def body(p_ref, w_ref, m_ref, v_ref, *rest):
        g_ref, d_ref, nm_ref, nv_ref = rest[-4:]
        g = p_ref[0].astype(F32)
        for j in range(1, N_DEV):
            g = g + p_ref[j].astype(F32)
        nm = ADAM_B1 * m_ref[...] + (1.0 - ADAM_B1) * g
        nv = ADAM_B2 * v_ref[...] + (1.0 - ADAM_B2) * (g * g)
        g_ref[...] = g
        nm_ref[...] = nm
        nv_ref[...] = nv
        d_ref[...] = -ADAM_LR * ((nm / c1) / (jnp.sqrt(nv / c2) + ADAM_EPS) + ADAM_WD * w_ref[...])

    off = layer * (r // tr)
    blk = pl.BlockSpec((tr, c), lambda i: (i + off, 0))
    prev = list(prev) if prev is not None else []
    return pl.pallas_call(
        body, name=name,
        grid=(r // tr,),
        in_specs=[pl.BlockSpec((N_DEV, tr, c), lambda i: (0, i, 0)), blk, blk, blk] + [_ANY] * len(prev),
        out_specs=[blk] * 4,
        out_shape=[jax.ShapeDtypeStruct((n_rows, c), F32)] * 4,
        input_output_aliases={4 + j: j for j in range(len(prev))},
        compiler_params=_params("parallel"),
    )(parts, w, m, v, *prev)


_TN_CANDS = (512, 256, 128)
_TK_CANDS = (2048, 1536, 1408, 1024, 768, 512, 256, 128)


def _nn(a, b, name, out_dtype=F32):
    return _mm(a, b, "nn", name=name, out_dtype=out_dtype, tm=a.shape[0], tn=_pick(b.shape[1], _TN_CANDS),
               tk=_pick(a.shape[1], _TK_CANDS))


def _nt(a, b, name, out_dtype=F32):
    return _mm(a, b, "nt", name=name, out_dtype=out_dtype, tm=a.shape[0], tn=_pick(b.shape[0], _TN_CANDS),
               tk=_pick(a.shape[1], _TK_CANDS))


def _tn(a, b, name, out_dtype=BF16):
    return _mm(a, b, "tn", name=name, out_dtype=out_dtype, tm=_pick(a.shape[1], _TN_CANDS),
               tn=_pick(b.shape[1], (1024,) + _TN_CANDS))


def _local_step(x, p, target, rep, weight, emit, n_heads, ff, start_token=None):
    s, d = x.shape
    depth = p.shape[0]
    ff_tc = _pick(ff, (512, 256, 128))
    grads = {}
    rep_grads = {k: [None] * depth for k in ("mix_pre_g", "mix_post_g", "ffn_pre_g", "ffn_post_g", "ple_norm_g")}

    tokens = [start_token]

    def gain(name, i):
        g = rep[name][i:i + 1]
        return g if tokens[0] is None else g + tokens[0]

    def send(name, layer, g):
        token = emit(name, layer, g)
        if token is not None:
            tokens[0] = token if tokens[0] is None else tokens[0] + token

    saved = []
    h = x
    for i in range(depth):
        sv = {"h": h}
        n1, = _rowcall(f"pre_norm{i}", lambda hh, g: _rms(hh, g), [h], [gain("mix_pre_g", i)], [BF16], cols=d)
        sv["n1"] = n1
        if i % 2 == 0:
            proj = _nn(n1, weight("w_in_even", 0, n1), f"in_even{i}")
            a_out = _sb_fwd(proj, n_heads, name=f"sb_fwd{i}")
            b_out, oraw, states = _hg_fwd(proj, rep["hg_lb_logits"], rep["hg_norm_g"], n_heads, name=f"hg_fwd{i}")
            cat = jnp.concatenate([a_out.astype(BF16), b_out], axis=1)
            m = _nn(cat, weight("w_out_even", 0, cat), f"out_even{i}")
            sv.update(proj=proj, oraw=oraw, states=states, cat=cat)
        else:
            proj = _nn(n1, weight("w_in_odd", 0, n1), f"in_odd{i}")
            sm = {k: weight(k, 0, proj) for k in _SMALL}
            xc, ra, ix = _rg_gates_fwd(proj, sm["conv_w"], sm["conv_b"], sm["rg_wa"], sm["rg_ba"],
                                       sm["rg_wx"], sm["rg_bx"], name=f"rg_gates_fwd{i}")
            hs, gact = _rg_scan_fwd(proj, xc, ra, ix, sm["rg_lambda"], name=f"rg_scan_fwd{i}")
            m = _nn(gact, weight("w_out_odd", 0, gact), f"out_odd{i}")
            sv.update(proj=proj, xc=xc, ra=ra, ix=ix, hs=hs, gact=gact, sm=sm)

        def post_mix(hh, mm, g_post, g_pre):
            h1 = hh + _rms(mm, g_post)
            return h1, _rms(h1, g_pre)

        h1, n2 = _rowcall(f"post_mix{i}", post_mix, [h, m], [gain("mix_post_g", i), gain("ffn_pre_g", i)],
                          [F32, BF16], cols=d)
        gu = _nn(n2, weight("w_gate_up", i, n2), f"gate_up{i}")
        act, = _rowcall(f"swiglu{i}", lambda g, u: _silu(g) * u, [(gu, 0), (gu, ff // ff_tc)], [], [BF16],
                        cols=ff, tc=ff_tc)
        f = _nn(act, weight("w_down", i, act), f"down{i}")

        def post_ffn(hh, ff_out, g_post):
            h2 = hh + _rms(ff_out, g_post)
            return h2, h2

        h2, h2b = _rowcall(f"post_ffn{i}", post_ffn, [h1, f], [gain("ffn_post_g", i)], [F32, BF16], cols=d)
        e = _nn(p[i], weight("w_ple_up", i, h2b), f"ple_up{i}")
        gl = _nn(h2b, weight("w_ple_gate", i, h2b), f"ple_gate{i}")
        h3, = _rowcall(f"ple{i}", lambda hh, a, b, g: hh + _rms(_sigmoid(a) * b, g), [h2, gl, e],
                       [gain("ple_norm_g", i)], [F32], cols=d)
        sv.update(m=m, h1=h1, n2=n2, gu=gu, act=act, f=f, h2b=h2b, e=e, gl=gl)
        saved.append(sv)
        h = h3

    def loss_fn(y, t):
        err = y - t
        return err * (1.0 / d), jnp.sum(err * err, axis=0, keepdims=True) * (0.5 / d)

    dh, loss_cols = _rowcall("loss", loss_fn, [h, target], [], [F32], red_rows=(1,), cols=d)

    for i in reversed(range(depth)):
        sv = saved[i]

        def ple_bwd(dy, a, b, g):
            _, vjp = jax.vjp(lambda a_, b_, g_: _rms(_sigmoid(a_) * b_, g_), a, b, g)
            return vjp(dy)

        dgl, de, rep_grads["ple_norm_g"][i] = _rowcall(
            f"ple_bwd{i}", ple_bwd, [dh, sv["gl"], sv["e"]], [gain("ple_norm_g", i)], [BF16, BF16],
            red_rows=(1,), cols=d)
        send("w_ple_up", i, _tn(p[i], de, f"d_ple_up{i}"))
        send("w_ple_gate", i, _tn(sv["h2b"], dgl, f"d_ple_gate{i}"))
        dh2_ple = _nt(dgl, weight("w_ple_gate", i, dgl), f"dx_ple_gate{i}")

        def post_ffn_bwd(dy, dx, ff_out, g):
            dh2 = dy + dx
            _, vjp = jax.vjp(_rms, ff_out, g)
            df, dg = vjp(dh2)
            return dh2, df, dg

        dh2, df, rep_grads["ffn_post_g"][i] = _rowcall(
            f"post_ffn_bwd{i}", post_ffn_bwd, [dh, dh2_ple, sv["f"]], [gain("ffn_post_g", i)], [F32, BF16],
            red_rows=(1,), cols=d)
        send("w_down", i, _tn(sv["act"], df, f"d_down{i}"))
        dact = _nt(df, weight("w_down", i, df), f"dx_down{i}")

        def swiglu_bwd(g, u, dy):
            _, vjp = jax.vjp(lambda g_, u_: _silu(g_) * u_, g, u)
            return vjp(dy)

        dg_, du_ = _rowcall(f"swiglu_bwd{i}", swiglu_bwd, [(sv["gu"], 0), (sv["gu"], ff // ff_tc), dact], [],
                            [BF16, BF16], cols=ff, tc=ff_tc)
        dgu = jnp.concatenate([dg_, du_], axis=1)
        send("w_gate_up", i, _tn(sv["n2"], dgu, f"d_gate_up{i}"))
        dn2 = _nt(dgu, weight("w_gate_up", i, dgu), f"dx_gate_up{i}")

        def post_mix_bwd(dy, dn, h1, mm, g_post, g_pre):
            _, vjp_pre = jax.vjp(_rms, h1, g_pre)
            dh1_n, dg_pre = vjp_pre(dn)
            dh1 = dy + dh1_n
            _, vjp_post = jax.vjp(_rms, mm, g_post)
            dm, dg_post = vjp_post(dh1)
            return dh1, dm, dg_pre, dg_post

        dh1, dm, rep_grads["ffn_pre_g"][i], rep_grads["mix_post_g"][i] = _rowcall(
            f"post_mix_bwd{i}", post_mix_bwd, [dh2, dn2, sv["h1"], sv["m"]],
            [gain("mix_post_g", i), gain("ffn_pre_g", i)], [F32, BF16], red_rows=(1, 1), cols=d)

        if i % 2 == 0:
            send("w_out_even", 0, _tn(sv["cat"], dm, f"d_out_even{i}"))
            dcat = _nt(dm, weight("w_out_even", 0, dm), f"dx_out_even{i}")
            dq, dk, dv = _sb_bwd(sv["proj"], dcat, n_heads, name=f"sb_bwd{i}")
            dhq, dhf, dhi, dhg, grads["hg_lb_logits"], grads["hg_norm_g"] = _hg_bwd(
                sv["proj"], rep["hg_lb_logits"], rep["hg_norm_g"], sv["oraw"], sv["states"], dcat, n_heads,
                name=f"hg_bwd{i}")
            dproj = jnp.concatenate([dq.astype(BF16), dk.astype(BF16), dv.astype(BF16), dhq, dhf, dhi, dhg], axis=1)
            send("w_in_even", 0, _tn(sv["n1"], dproj, f"d_in_even{i}"))
            dn1 = _nt(dproj, weight("w_in_even", 0, dproj), f"dx_in_even{i}")
        else:
            sm = sv["sm"]
            send("w_out_odd", 0, _tn(sv["gact"], dm, f"d_out_odd{i}"))
            dgo = _nt(dm, weight("w_out_odd", 0, dm), f"dx_out_odd{i}")
            dgate, dra, dix, dxc1, grads["rg_lambda"] = _rg_scan_bwd(
                dgo, sv["proj"], sv["hs"], sv["xc"], sv["ra"], sv["ix"], sm["rg_lambda"], name=f"rg_scan_bwd{i}")
            (dxb, grads["conv_w"], grads["conv_b"], grads["rg_wa"], grads["rg_ba"], grads["rg_wx"],
             grads["rg_bx"]) = _rg_gates_bwd(dra, dix, dxc1, sv["xc"], sv["proj"], sm["conv_w"], sm["rg_wa"],
                                            sm["rg_wx"], name=f"rg_gates_bwd{i}")
            send("small", 0, {k: grads.pop(k) for k in _SMALL})
            dproj = jnp.concatenate([dgate, dxb], axis=1)
            send("w_in_odd", 0, _tn(sv["n1"], dproj, f"d_in_odd{i}"))
            dn1 = _nt(dproj, weight("w_in_odd", 0, dproj), f"dx_in_odd{i}")

        def pre_norm_bwd(dy, dn, hh, g):
            _, vjp = jax.vjp(_rms, hh, g)
            dx, dg = vjp(dn)
            return dy + dx, dg

        dh, rep_grads["mix_pre_g"][i] = _rowcall(
            f"pre_norm_bwd{i}", pre_norm_bwd, [dh1, dn1, sv["h"]], [gain("mix_pre_g", i)], [F32],
            red_rows=(1,), cols=d)

    for k, rows in rep_grads.items():
        grads[k] = jnp.concatenate(rows, axis=0)
    return loss_cols, dh, grads


_WEIGHTS = ("mix_pre_g", "mix_post_g", "ffn_pre_g", "ffn_post_g", "ple_norm_g", "w_in_even", "w_out_even",
            "hg_lb_logits", "hg_norm_g", "w_in_odd", "conv_w", "conv_b", "rg_wa", "rg_ba", "rg_wx", "rg_bx",
            "rg_lambda", "w_out_odd", "w_gate_up", "w_down", "w_ple_up", "w_ple_gate")
_REPLICATED = ("mix_pre_g", "mix_post_g", "ffn_pre_g", "ffn_post_g", "ple_norm_g", "hg_lb_logits", "hg_norm_g")
_SMALL = ("conv_w", "conv_b", "rg_wa", "rg_ba", "rg_wx", "rg_bx", "rg_lambda")
_BIG = {"w_in_even": True, "w_out_even": False, "w_in_odd": True, "w_out_odd": False,
        "w_gate_up": True, "w_down": False, "w_ple_up": True, "w_ple_gate": False}
_PACK_ROW = SUBLANES * LANES


def _pack(arrays):
    flat = jnp.concatenate([a.reshape(-1) for a in arrays])
    pad = -flat.shape[0] % _PACK_ROW
    return jnp.pad(flat, (0, pad)).reshape(-1, LANES)


def _pack_blocks(arrays):
    flat = jnp.concatenate([a.reshape(N_DEV, -1) for a in arrays], axis=1)
    pad = -flat.shape[1] % _PACK_ROW
    return jnp.pad(flat, ((0, 0), (0, pad))).reshape(N_DEV, -1, LANES)


def _unpack(packed, shapes, lead=()):
    flat = packed.reshape(lead + (-1,))
    out, pos = [], 0
    for shape in shapes:
        n = math.prod(shape)
        out.append(flat[..., pos:pos + n].reshape(lead + tuple(shape)))
        pos += n
    return out


def _to_full_small(name, blocks):
    if name == "conv_w":
        return jnp.transpose(blocks, (1, 0, 2)).reshape(blocks.shape[1], -1)
    if name in ("conv_b", "rg_lambda"):
        return blocks.reshape(1, -1)
    nb = blocks.shape[1]
    if name in ("rg_wa", "rg_wx"):
        return jnp.transpose(blocks, (1, 0, 2, 3)).reshape(nb, RG_BLOCK, RG_BLOCK)
    return jnp.transpose(blocks, (1, 0, 2)).reshape(nb, 1, RG_BLOCK)


def _to_blocks_small(name, full):
    if name == "conv_w":
        return jnp.transpose(full.reshape(full.shape[0], N_DEV, -1), (1, 0, 2))
    if name in ("conv_b", "rg_lambda"):
        return full.reshape(N_DEV, -1)
    nb = full.shape[0]
    if name in ("rg_wa", "rg_wx"):
        return jnp.transpose(full.reshape(nb, N_DEV, RG_BLOCK // N_DEV, RG_BLOCK), (1, 0, 2, 3))
    return jnp.transpose(full.reshape(nb, N_DEV, RG_BLOCK // N_DEV), (1, 0, 2))


def _step(inp):
    w = {k: inp[k] for k in _WEIGHTS}
    x, p, target = inp["x"][0], inp["p"][:, 0], inp["loss_target"][0]
    assert w["hg_lb_logits"].shape[0] == 2 and w["w_in_even"].shape[0] == 1 and w["w_in_odd"].shape[0] == 1

    n_heads = w["w_in_even"].shape[2] * N_DEV // (7 * HEAD_DIM)
    ff = w["w_down"].shape[1] * N_DEV
    small_shapes = [w[k].shape[1:] for k in _SMALL]

    def lands_in_place(name):
        return _BIG[name] and w[name].shape[2] % LANES == 0

    depth = p.shape[0]
    order = [("w_in_even", 0), ("w_out_even", 0)] if depth else []
    for i in range(depth):
        if i == 1:
            order += [("w_in_odd", 0), ("small", 0), ("w_out_odd", 0)]
        order += [("w_gate_up", i), ("w_down", i), ("w_ple_up", i), ("w_ple_gate", i)]
    gathers = {}
    for name, l in order:
        if name == "small":
            gathers[name, l] = _gather_start(_pack([w[k][0] for k in _SMALL]), name="gather_small")
        else:
            gathers[name, l] = _gather_start(w[name][l].astype(BF16), name=f"gather_{name}{l}",
                                             cols=lands_in_place(name))
    ready = {}

    def weight(name, layer, after):
        key = ("small", 0) if name in _SMALL else (name, layer)
        if key not in ready:
            land, = _spread_wait(gathers[key], after, name=f"gathered_{key[0]}{key[1]}")
            if name in _SMALL:
                ready[key] = {k: _to_full_small(k, b)
                              for k, b in zip(_SMALL, _unpack(land, small_shapes, lead=(N_DEV,)))}
            elif lands_in_place(name):
                ready[key] = land
            elif _BIG[name]:
                ready[key] = jnp.transpose(land, (1, 0, 2)).reshape(land.shape[1], -1)
            else:
                ready[key] = land.reshape(-1, land.shape[2])
        return ready[key][name] if name in _SMALL else ready[key]

    exchanges = []

    def emit(name, layer, g):
        if name == "small":
            handle = _exchange_start([_pack_blocks([_to_blocks_small(k, g[k]) for k in _SMALL])],
                                     name="exchange_small")
            exchanges.append((name, layer, handle))
            return handle["token"]
        _, r, c = w[name].shape
        if lands_in_place(name):
            handle = _exchange_start([g], name=f"exchange_{name}{layer}", cols=True)
        elif _BIG[name]:
            handle = _exchange_start([jnp.transpose(g.reshape(-1, N_DEV, c), (1, 0, 2))],
                                     name=f"exchange_{name}{layer}")
        else:
            handle = _exchange_start([g.reshape(N_DEV, r, c)], name=f"exchange_{name}{layer}")
        exchanges.append((name, layer, handle))
        return handle["token"]

    rep = {k: w[k] for k in _REPLICATED}
    start_token = sum(h["token"] for h in gathers.values())
    loss_cols, dx, grads = _local_step(x, p, target, rep, weight, emit, n_heads, ff, start_token)

    loss_part = jnp.sum(loss_cols).reshape(1)
    rep_gather = _gather_start(_pack([grads[k] for k in _REPLICATED] + [loss_part]), name="gather_rep_grads")

    out = {}
    after = dx
    for name, layer, handle in exchanges:
        land, = _spread_wait(handle, after, name=f"exchanged_{name}{layer}")
        if name == "small":
            res = _adamw(land.reshape(N_DEV, -1, LANES), *[_pack([inp[pre + k][0] for k in _SMALL]) for pre in ("", "m_", "v_")],
                         name="adamw_small")
            for k, *vals in zip(_SMALL, *[_unpack(a, small_shapes) for a in res]):
                out[k] = [v[None] for v in vals]
        else:
            n_l, r, c = w[name].shape
            res = out[name] = _adamw(land.reshape(N_DEV, r, c),
                                     *[inp[pre + name].reshape(n_l * r, c) for pre in ("", "m_", "v_")],
                                     name=f"adamw_{name}{layer}", layer=layer, prev=out.get(name))
        after = res[0]
    for name in _BIG:
        out[name] = [a.reshape(w[name].shape) for a in out[name]]

    rep_shapes = [w[k].shape for k in _REPLICATED] + [(1,)]
    rep_parts, = _spread_wait(rep_gather, after, name="gathered_rep_grads")
    res = _adamw(rep_parts, *[_pack([inp[pre + k] for k in _REPLICATED] + [jnp.zeros((1,), F32)])
                              for pre in ("", "m_", "v_")], name="adamw_rep")
    for k, *vals in zip(_REPLICATED + ("loss",), *[_unpack(a, rep_shapes) for a in res]):
        out[k] = vals
    loss = out["loss"][0][0]

    return (loss, dx[None]) + tuple(out[k][j] for j in range(4) for k in _WEIGHTS)


def kernel(x, p, mix_pre_g, mix_post_g, ffn_pre_g, ffn_post_g, ple_norm_g, w_in_even, w_out_even, hg_lb_logits, hg_norm_g, w_in_odd, conv_w, conv_b, rg_wa, rg_ba, rg_wx, rg_bx, rg_lambda, w_out_odd, w_gate_up, w_down, w_ple_up, w_ple_gate, loss_target, m_mix_pre_g, m_mix_post_g, m_ffn_pre_g, m_ffn_post_g, m_ple_norm_g, m_w_in_even, m_w_out_even, m_hg_lb_logits, m_hg_norm_g, m_w_in_odd, m_conv_w, m_conv_b, m_rg_wa, m_rg_ba, m_rg_wx, m_rg_bx, m_rg_lambda, m_w_out_odd, m_w_gate_up, m_w_down, m_w_ple_up, m_w_ple_gate, v_mix_pre_g, v_mix_post_g, v_ffn_pre_g, v_ffn_post_g, v_ple_norm_g, v_w_in_even, v_w_out_even, v_hg_lb_logits, v_hg_norm_g, v_w_in_odd, v_conv_w, v_conv_b, v_rg_wa, v_rg_ba, v_rg_wx, v_rg_bx, v_rg_lambda, v_w_out_odd, v_w_gate_up, v_w_down, v_w_ple_up, v_w_ple_gate):
    return _step(dict(locals()))
```

```python
import functools
import math

import jax
import jax.numpy as jnp
from jax import lax
from jax.experimental import pallas as pl
from jax.experimental.pallas import tpu as pltpu

F32 = jnp.float32
BF16 = jnp.bfloat16

VMEM_LIMIT_BYTES = 56 * 1024 * 1024
LANES = 128
SUBLANES = 8

N_DEV = 8
HEAD_DIM = 128
SB_Q_TILE = 512
SB_K_TILE = 128
HG_CHUNK = 32
RG_BLOCK = 256
CONV_TAPS = 4
RG_C = 8.0
RMS_EPS = 1e-6

ADAM_LR = 0.001
ADAM_B1 = 0.9
ADAM_B2 = 0.999
ADAM_EPS = 1e-08
ADAM_WD = 0.01
ADAM_STEP = 10

MESH_AXES = ("x", "y", "c")


def _params(*sem):
    return pltpu.CompilerParams(dimension_semantics=sem, vmem_limit_bytes=VMEM_LIMIT_BYTES)


def _pick(n, cands):
    for c in cands:
        if c <= n and n % c == 0:
            return c
    return n


def _mm(a, b, mode, *, name, out_dtype=F32, tm=512, tn=512, tk=None):
    if mode == "nn":
        (m, k), (k2, n) = a.shape, b.shape
    elif mode == "nt":
        (m, k), (n, k2) = a.shape, b.shape
    else:
        (k, m), (k2, n) = a.shape, b.shape
    assert k == k2, (a.shape, b.shape, mode)
    tm, tn = min(tm, m), min(tn, n)
    tk = k if tk is None else min(tk, k)
    assert m % tm == 0 and n % tn == 0 and k % tk == 0, (m, n, k, tm, tn, tk)
    nk = k // tk

    if mode == "tn":
        assert nk == 1
        return _mm_tn(a, b, name=name, out_dtype=out_dtype, tm=tm, tn=tn)

    a_spec = pl.BlockSpec((tm, tk), lambda i, j, kk: (i, kk))
    if mode == "nn":
        b_spec = pl.BlockSpec((tk, tn), lambda i, j, kk: (kk, j))
        dims = (((1,), (0,)), ((), ()))
    else:
        b_spec = pl.BlockSpec((tn, tk), lambda i, j, kk: (j, kk))
        dims = (((1,), (1,)), ((), ()))

    def body(a_ref, b_ref, o_ref, *acc):
        part = lax.dot_general(a_ref[...].astype(BF16), b_ref[...].astype(BF16), dims, preferred_element_type=F32)
        if nk == 1:
            o_ref[...] = part.astype(out_dtype)
        else:
            acc_ref, = acc
            kk = pl.program_id(2)

            @pl.when(kk == 0)
            def _():
                acc_ref[...] = part

            @pl.when(kk > 0)
            def _():
                acc_ref[...] += part

            @pl.when(kk == nk - 1)
            def _():
                o_ref[...] = acc_ref[...].astype(out_dtype)

    return pl.pallas_call(
        body, name=name,
        grid=(m // tm, n // tn, nk),
        in_specs=[a_spec, b_spec],
        out_specs=pl.BlockSpec((tm, tn), lambda i, j, kk: (i, j)),
        out_shape=jax.ShapeDtypeStruct((m, n), out_dtype),
        scratch_shapes=[] if nk == 1 else [pltpu.VMEM((tm, tn), F32)],
        compiler_params=_params("parallel", "parallel", "arbitrary"),
    )(a, b)


def _mm_tn(a, b, *, name, out_dtype, tm, tn):
    k, m = a.shape
    n = b.shape[1]

    def body(a_ref, b_ref, o_ref, at_ref):
        @pl.when(pl.program_id(1) == 0)
        def _():
            at_ref[...] = a_ref[...].astype(F32).T.astype(BF16)

        o_ref[...] = jnp.dot(at_ref[...], b_ref[...].astype(BF16), preferred_element_type=F32).astype(out_dtype)

    return pl.pallas_call(
        body, name=name,
        grid=(m // tm, n // tn),
        in_specs=[pl.BlockSpec((k, tm), lambda i, j: (0, i)), pl.BlockSpec((k, tn), lambda i, j: (0, j))],
        out_specs=pl.BlockSpec((tm, tn), lambda i, j: (i, j)),
        out_shape=jax.ShapeDtypeStruct((m, n), out_dtype),
        scratch_shapes=[pltpu.VMEM((tm, k), BF16)],
        compiler_params=_params("parallel", "arbitrary"),
    )(a, b)


def _rowcall(name, fn, rows, pars, row_outs, red_rows=(), *, cols, ts=256, tc=None):
    rows = [r if isinstance(r, tuple) else (r, 0) for r in rows]
    pars = [p if isinstance(p, tuple) else (p, 0) for p in pars]
    s = rows[0][0].shape[0]
    tc = cols if tc is None else tc
    ts = min(ts, s)
    assert s % ts == 0 and cols % tc == 0, (name, s, ts, cols, tc)
    n_in, n_row_out = len(rows) + len(pars), len(row_outs)

    def body(*refs):
        outs = fn(*[r[...] for r in refs[:n_in]])
        outs = outs if isinstance(outs, (tuple, list)) else (outs,)
        o_refs = refs[n_in:]
        for o_ref, val in zip(o_refs[:n_row_out], outs[:n_row_out]):
            o_ref[...] = val.astype(o_ref.dtype)
        first = pl.program_id(1) == 0
        for o_ref, val in zip(o_refs[n_row_out:], outs[n_row_out:]):
            @pl.when(first)
            def _(o_ref=o_ref, val=val):
                o_ref[...] = val

            @pl.when(jnp.logical_not(first))
            def _(o_ref=o_ref, val=val):
                o_ref[...] += val

    def row_map(off):
        return lambda j, i: (i, j + off)

    def par_map(off):
        return lambda j, i: (0, j + off)

    return pl.pallas_call(
        body, name=name,
        grid=(cols // tc, s // ts),
        in_specs=[pl.BlockSpec((ts, tc), row_map(off)) for _, off in rows]
        + [pl.BlockSpec((p.shape[0], tc), par_map(off)) for p, off in pars],
        out_specs=[pl.BlockSpec((ts, tc), lambda j, i: (i, j)) for _ in row_outs]
        + [pl.BlockSpec((r, tc), lambda j, i: (0, j)) for r in red_rows],
        out_shape=[jax.ShapeDtypeStruct((s, cols), dt) for dt in row_outs]
        + [jax.ShapeDtypeStruct((r, cols), F32) for r in red_rows],
        compiler_params=_params("parallel", "arbitrary"),
    )(*[r for r, _ in rows], *[p for p, _ in pars])


def _rms(x, g):
    return x * lax.rsqrt(jnp.mean(x * x, axis=-1, keepdims=True) + RMS_EPS) * g


def _sigmoid(x):
    return jax.nn.sigmoid(x)


def _silu(x):
    return x * jax.nn.sigmoid(x)


def _gelu(x):
    return 0.5 * x * (1.0 + jnp.tanh(math.sqrt(2.0 / math.pi) * (x + 0.044715 * (x * x * x))))


def _softplus(x):
    return jnp.maximum(x, 0.0) + jnp.log1p(jnp.exp(-jnp.abs(x)))


def _split3(x):
    hi = x.astype(BF16)
    r1 = x - hi.astype(F32)
    mid = r1.astype(BF16)
    lo = (r1 - mid.astype(F32)).astype(BF16)
    return hi, mid, lo


def _xdot(x, t):
    return sum(jnp.dot(p, t, preferred_element_type=F32) for p in _split3(x))


def _xdot_l(t, x):
    return sum(jnp.dot(t, p, preferred_element_type=F32) for p in _split3(x))


_NT = (((1,), (1,)), ((), ()))
_TN = (((0,), (0,)), ((), ()))


def _dot(a, b, dims=None):
    if dims is None:
        return jnp.dot(a.astype(BF16), b.astype(BF16), preferred_element_type=F32)
    return lax.dot_general(a.astype(BF16), b.astype(BF16), dims, preferred_element_type=F32)


def _iota(shape, axis):
    return lax.broadcasted_iota(jnp.int32, shape, axis)


def _sb_tile(qb, kblk, mask, upper, c_rem):
    z = lax.dot_general(qb, kblk, _NT, preferred_element_type=F32)
    soft = jnp.log1p(jnp.exp(-jnp.abs(z)))
    lbeta = jnp.minimum(z, 0.0) - soft
    l1m = jnp.where(mask, -jnp.maximum(z, 0.0) - soft, 0.0)
    rem = _xdot(l1m, upper) + c_rem
    w = jnp.where(mask, jnp.exp(lbeta + rem), 0.0)
    return lbeta, l1m, w


def _sb_tiles(s):
    tq = min(SB_Q_TILE, s)
    return tq, SB_K_TILE, tq // SB_K_TILE


def _sb_fwd(proj, n_heads, *, name):
    s = proj.shape[0]
    t, tk, per_q = _sb_tiles(s)
    scale = HEAD_DIM ** -0.5

    def body(q_ref, k_ref, v_ref, o_ref):
        qi = pl.program_id(1)
        qb = (q_ref[...] * scale).astype(BF16)
        row, col = _iota((t, tk), 0) + qi * t, _iota((t, tk), 1)
        upper = (_iota((tk, tk), 0) > _iota((tk, tk), 1)).astype(BF16)
        n_kb = (qi + 1) * per_q

        def step(j, carry):
            acc, c_rem = carry
            kb = n_kb - 1 - j
            rows = pl.ds(pl.multiple_of(kb * tk, tk), tk)
            kblk = k_ref[rows, :].astype(BF16)
            vblk = v_ref[rows, :].astype(BF16)
            _, l1m, w = _sb_tile(qb, kblk, (col + kb * tk) < row, upper, c_rem)
            acc = acc + jnp.dot(w.astype(BF16), vblk, preferred_element_type=F32)
            return acc, c_rem + jnp.sum(l1m, axis=1, keepdims=True)

        acc, _ = lax.fori_loop(0, n_kb, step, (jnp.zeros((t, HEAD_DIM), F32), jnp.zeros((t, 1), F32)))
        o_ref[...] = acc

    return pl.pallas_call(
        body, name=name,
        grid=(n_heads, s // t),
        in_specs=[pl.BlockSpec((t, HEAD_DIM), lambda h, i: (i, h)),
                  pl.BlockSpec((s, HEAD_DIM), lambda h, i: (0, n_heads + h)),
                  pl.BlockSpec((s, HEAD_DIM), lambda h, i: (0, 2 * n_heads + h))],
        out_specs=pl.BlockSpec((t, HEAD_DIM), lambda h, i: (i, h)),
        out_shape=jax.ShapeDtypeStruct((s, n_heads * HEAD_DIM), F32),
        compiler_params=_params("parallel", "arbitrary"),
    )(proj, proj, proj)


def _sb_bwd(proj, dcat, n_heads, *, name):
    s = proj.shape[0]
    t, tk, per_q = _sb_tiles(s)
    scale = HEAD_DIM ** -0.5

    def body(q_ref, k_ref, v_ref, do_ref, dq_ref, dk_ref, dv_ref, g_s, sig_s):
        qi = pl.program_id(1)

        @pl.when(qi == 0)
        def _():
            dk_ref[...] = jnp.zeros_like(dk_ref)
            dv_ref[...] = jnp.zeros_like(dv_ref)

        qb = (q_ref[...] * scale).astype(BF16)
        dob = do_ref[...].astype(BF16)
        row, col = _iota((t, tk), 0) + qi * t, _iota((t, tk), 1)
        upper = (_iota((tk, tk), 0) > _iota((tk, tk), 1)).astype(BF16)
        lower_incl = (_iota((tk, tk), 0) >= _iota((tk, tk), 1)).astype(BF16)
        n_kb = (qi + 1) * per_q

        def weights(j, carry):
            c_rem, g_all = carry
            kb = n_kb - 1 - j
            rows = pl.ds(pl.multiple_of(kb * tk, tk), tk)
            kblk = k_ref[rows, :].astype(BF16)
            vblk = v_ref[rows, :].astype(BF16)
            lbeta, l1m, w = _sb_tile(qb, kblk, (col + kb * tk) < row, upper, c_rem)
            g = w * lax.dot_general(dob, vblk, _NT, preferred_element_type=F32)
            dv_ref[rows, :] += lax.dot_general(w.astype(BF16), dob, _TN, preferred_element_type=F32)
            g_s[kb] = g
            sig_s[kb] = jnp.exp(lbeta)
            return c_rem + jnp.sum(l1m, axis=1, keepdims=True), g_all + jnp.sum(g, axis=1, keepdims=True)

        zero_col = jnp.zeros((t, 1), F32)
        _, g_all = lax.fori_loop(0, n_kb, weights, (zero_col, zero_col))

        def scores(j, carry):
            dq, c_g = carry
            kb = n_kb - 1 - j
            rows = pl.ds(pl.multiple_of(kb * tk, tk), tk)
            g, sig = g_s[kb], sig_s[kb]
            mask = (col + kb * tk) < row
            g_before = g_all - (_xdot(g, lower_incl) + c_g)
            dz = jnp.where(mask, g * (1.0 - sig) - g_before * sig, 0.0).astype(BF16)
            dq = dq + jnp.dot(dz, k_ref[rows, :].astype(BF16), preferred_element_type=F32)
            dk_ref[rows, :] += lax.dot_general(dz, qb, _TN, preferred_element_type=F32)
            return dq, c_g + jnp.sum(g, axis=1, keepdims=True)

        dq, _ = lax.fori_loop(0, n_kb, scores, (jnp.zeros((t, HEAD_DIM), F32), zero_col))
        dq_ref[...] = dq * scale

    width = n_heads * HEAD_DIM
    return pl.pallas_call(
        body, name=name,
        grid=(n_heads, s // t),
        in_specs=[pl.BlockSpec((t, HEAD_DIM), lambda h, i: (i, h)),
                  pl.BlockSpec((s, HEAD_DIM), lambda h, i: (0, n_heads + h)),
                  pl.BlockSpec((s, HEAD_DIM), lambda h, i: (0, 2 * n_heads + h)),
                  pl.BlockSpec((t, HEAD_DIM), lambda h, i: (i, h))],
        out_specs=[pl.BlockSpec((t, HEAD_DIM), lambda h, i: (i, h)),
                   pl.BlockSpec((s, HEAD_DIM), lambda h, i: (0, h)),
                   pl.BlockSpec((s, HEAD_DIM), lambda h, i: (0, h))],
        out_shape=[jax.ShapeDtypeStruct((s, width), F32)] * 3,
        scratch_shapes=[pltpu.VMEM((s // tk, t, tk), F32)] * 2,
        compiler_params=_params("parallel", "arbitrary"),
    )(proj, proj, proj, dcat)


def _hg_pre(hq, hf, logits):
    mx = jnp.max(logits, axis=0, keepdims=True)
    ex = jnp.exp(logits - mx)
    lb = ex[0:1, :] / jnp.sum(ex, axis=0, keepdims=True)
    f = lb + (1.0 - lb) * _sigmoid(hf)
    return _silu(hq), 1.0 - f, jnp.log(f)


def _hg_post(o, norm_g, hgate):
    return _rms(o, norm_g) * _silu(hgate)


def _hg_specs(s, n_heads, first_block):
    def at(group):
        return pl.BlockSpec((s, HEAD_DIM), lambda h: (0, first_block + group * n_heads + h))
    return [at(0), at(1), at(2), at(3)]


def _hg_fwd(proj, logits, norm_g, n_heads, *, name):
    s = proj.shape[0]
    hc = HG_CHUNK
    n_chunks = s // hc
    d = HEAD_DIM

    def body(lg_ref, ng_ref, hq_ref, hf_ref, hi_ref, hgt_ref, out_ref, oraw_ref, st_ref,
             q_s, k_s, lf_s, cum_s, qc_s, oc_s):
        q, k, lf = _hg_pre(hq_ref[...], hf_ref[...], lg_ref[...])
        q_s[...] = q
        k_s[...] = k
        lf_s[...] = lf
        tril = (_iota((hc, hc), 0) >= _iota((hc, hc), 1)).astype(BF16)
        srow = _iota((hc, d), 0)

        def chunk(ci, st):
            rows = pl.ds(pl.multiple_of(ci * hc, hc), hc)
            q, k, v = q_s[rows, :], k_s[rows, :], hi_ref[rows, :]
            cum = _xdot_l(tril, lf_s[rows, :])
            st_ref[0, ci] = st
            o_inter = _dot(q * jnp.exp(cum), st, _NT)
            cum_s[...] = cum
            qc_s[...] = q
            for t in range(hc):
                ng = (t // SUBLANES + 1) * SUBLANES
                e = jnp.where(srow[:ng] <= t, jnp.exp(cum_s[t:t + 1, :] - cum[:ng]), 0.0)
                sc = jnp.sum(qc_s[t:t + 1, :] * k[:ng] * e, axis=1, keepdims=True)
                oc_s[t:t + 1, :] = jnp.sum(sc * v[:ng], axis=0, keepdims=True)
            oraw_ref[rows, :] = o_inter + oc_s[...]
            last = cum_s[hc - 1:hc, :]
            return st * jnp.exp(last) + _dot(v, k * jnp.exp(last - cum), _TN)

        lax.fori_loop(0, n_chunks, chunk, jnp.zeros((d, d), F32))
        out_ref[...] = _hg_post(oraw_ref[...], ng_ref[...], hgt_ref[...]).astype(BF16)

    width = n_heads * d
    head_block = pl.BlockSpec((s, d), lambda h: (0, h))
    return pl.pallas_call(
        body, name=name,
        grid=(n_heads,),
        in_specs=[pl.BlockSpec((2, d), lambda h: (0, h)), pl.BlockSpec((1, d), lambda h: (0, 0))]
        + _hg_specs(s, n_heads, 3 * n_heads),
        out_specs=[head_block, head_block, pl.BlockSpec((1, n_chunks, d, d), lambda h: (h, 0, 0, 0))],
        out_shape=[jax.ShapeDtypeStruct((s, width), BF16), jax.ShapeDtypeStruct((s, width), F32),
                   jax.ShapeDtypeStruct((n_heads, n_chunks, d, d), F32)],
        scratch_shapes=[pltpu.VMEM((s, d), F32)] * 3 + [pltpu.VMEM((hc, d), F32)] * 3,
        compiler_params=_params("arbitrary"),
    )(logits, norm_g, proj, proj, proj, proj)


def _hg_bwd(proj, logits, norm_g, oraw, states, dcat, n_heads, *, name):
    s = proj.shape[0]
    hc = HG_CHUNK
    n_chunks = s // hc
    d = HEAD_DIM

    def body(lg_ref, ng_ref, hq_ref, hf_ref, hi_ref, hgt_ref, oraw_ref, st_ref, dout_ref,
             dhq_ref, dhf_ref, dhi_ref, dhgt_ref, dlg_ref, dng_ref,
             q_s, k_s, lf_s, do_s, dq_s, dk_s, dlf_s, cum_s, qc_s, doc_s, dqc_s, dkc_s, dvc_s):
        head = pl.program_id(0)
        (q, k, lf), pre_vjp = jax.vjp(_hg_pre, hq_ref[...], hf_ref[...], lg_ref[...])
        q_s[...] = q
        k_s[...] = k
        lf_s[...] = lf
        _, post_vjp = jax.vjp(_hg_post, oraw_ref[...], ng_ref[...], hgt_ref[...])
        do, dng, dhgt = post_vjp(dout_ref[...])
        do_s[...] = do
        dhgt_ref[...] = dhgt.astype(BF16)

        @pl.when(head == 0)
        def _():
            dng_ref[...] = dng

        @pl.when(head > 0)
        def _():
            dng_ref[...] += dng

        triu = (_iota((hc, hc), 0) <= _iota((hc, hc), 1)).astype(BF16)
        tril = (_iota((hc, hc), 0) >= _iota((hc, hc), 1)).astype(BF16)
        srow = _iota((hc, d), 0)

        def chunk(j, dst):
            ci = n_chunks - 1 - j
            rows = pl.ds(pl.multiple_of(ci * hc, hc), hc)
            q, k, v, do_c = q_s[rows, :], k_s[rows, :], hi_ref[rows, :], do_s[rows, :]
            cum = _xdot_l(tril, lf_s[rows, :])
            st = st_ref[0, ci]
            cum_s[...] = cum
            qc_s[...] = q
            doc_s[...] = do_c
            last = cum_s[hc - 1:hc, :]
            e_cum, e_last = jnp.exp(cum), jnp.exp(last - cum)
            dqc_s[...] = _dot(do_c, st) * e_cum
            dk_state = _dot(v, dst) * e_last
            dkc_s[...] = dk_state
            dvc_s[...] = _dot(k * e_last, dst, _NT)
            d_last = (jnp.sum(dst * st, axis=0, keepdims=True) * jnp.exp(last)
                      + jnp.sum(k * dk_state, axis=0, keepdims=True))
            for t in range(hc):
                ng = (t // SUBLANES + 1) * SUBLANES
                qt, dot_ = qc_s[t:t + 1, :], doc_s[t:t + 1, :]
                e = jnp.where(srow[:ng] <= t, jnp.exp(cum_s[t:t + 1, :] - cum[:ng]), 0.0)
                ke = k[:ng] * e
                d_a = jnp.sum(dot_ * v[:ng], axis=1, keepdims=True)
                dqc_s[t:t + 1, :] += jnp.sum(d_a * ke, axis=0, keepdims=True)
                dkc_s[0:ng, :] += d_a * (qt * e)
                dvc_s[0:ng, :] += jnp.sum(qt * ke, axis=1, keepdims=True) * dot_
            dq, dk = dqc_s[...], dkc_s[...]
            d_b = q * dq - k * dk
            dq_s[rows, :] = dq
            dk_s[rows, :] = dk
            dhi_ref[rows, :] = dvc_s[...].astype(BF16)
            dlf_s[rows, :] = _xdot_l(triu, d_b) + d_last
            return dst * jnp.exp(last) + _dot(do_c, q * e_cum, _TN)

        lax.fori_loop(0, n_chunks, chunk, jnp.zeros((d, d), F32))
        dhq, dhf, dlg = pre_vjp((dq_s[...], dk_s[...], dlf_s[...]))
        dhq_ref[...] = dhq.astype(BF16)
        dhf_ref[...] = dhf.astype(BF16)
        dlg_ref[...] = dlg

    width = n_heads * d
    head_block = pl.BlockSpec((s, d), lambda h: (0, h))
    return pl.pallas_call(
        body, name=name,
        grid=(n_heads,),
        in_specs=[pl.BlockSpec((2, d), lambda h: (0, h)), pl.BlockSpec((1, d), lambda h: (0, 0))]
        + _hg_specs(s, n_heads, 3 * n_heads)
        + [head_block, pl.BlockSpec((1, n_chunks, d, d), lambda h: (h, 0, 0, 0)),
           pl.BlockSpec((s, d), lambda h: (0, n_heads + h))],
        out_specs=[head_block] * 4 + [pl.BlockSpec((2, d), lambda h: (0, h)), pl.BlockSpec((1, d), lambda h: (0, 0))],
        out_shape=[jax.ShapeDtypeStruct((s, width), BF16)] * 4
        + [jax.ShapeDtypeStruct((2, width), F32), jax.ShapeDtypeStruct((1, d), F32)],
        scratch_shapes=[pltpu.VMEM((s, d), F32)] * 7 + [pltpu.VMEM((hc, d), F32)] * 6,
        compiler_params=_params("arbitrary"),
    )(logits, norm_g, proj, proj, proj, proj, oraw, states, dcat)


def _shift_down(x, n, srow):
    if n == 0:
        return x
    return jnp.where(srow >= n, pltpu.roll(x, n, 0), 0.0)


def _shift_up(x, n, srow):
    if n == 0:
        return x
    s = x.shape[0]
    return jnp.where(srow < s - n, pltpu.roll(x, s - n, 0), 0.0)


def _rg_gates_fwd(proj, conv_w, conv_b, wa, ba, wx, bx, *, name):
    s = proj.shape[0]
    nb = wa.shape[0]
    bw = RG_BLOCK

    def body(xb_ref, cw_ref, cb_ref, wa_ref, ba_ref, wx_ref, bx_ref, xc_ref, ra_ref, ix_ref):
        x = xb_ref[...]
        srow = _iota((s, bw), 0)
        cw = cw_ref[...]
        xc = cb_ref[...] + cw[0:1, :] * x
        for tap in range(1, CONV_TAPS):
            xc = xc + cw[tap:tap + 1, :] * _shift_down(x, tap, srow)
        xc_ref[...] = xc
        ra_ref[...] = _dot(xc, wa_ref[0]) + ba_ref[0]
        ix_ref[...] = _dot(xc, wx_ref[0]) + bx_ref[0]

    col = pl.BlockSpec((s, bw), lambda n: (0, n))
    vec = lambda r: pl.BlockSpec((r, bw), lambda n: (0, n))
    mat = pl.BlockSpec((1, bw, bw), lambda n: (n, 0, 0))
    bias = pl.BlockSpec((1, 1, bw), lambda n: (n, 0, 0))
    return pl.pallas_call(
        body, name=name,
        grid=(nb,),
        in_specs=[pl.BlockSpec((s, bw), lambda n: (0, nb + n)), vec(CONV_TAPS), vec(1), mat, bias, mat, bias],
        out_specs=[col] * 3,
        out_shape=[jax.ShapeDtypeStruct((s, nb * bw), F32)] * 3,
        compiler_params=_params("parallel"),
    )(proj, conv_w, conv_b, wa, ba, wx, bx)


def _rg_au(ra, ix, xc, lam, first_row):
    log_a = -RG_C * _sigmoid(ra) * _softplus(-lam)
    th = jnp.tanh(log_a)
    one_minus_a2 = -2.0 * th / (1.0 - th)
    mult = jnp.where(first_row, 1.0, jnp.sqrt(one_minus_a2))
    return jnp.exp(log_a), xc * _sigmoid(ix) * mult


def _rg_out(gate, hs):
    return _gelu(gate) * hs


def _scan_rows(n_groups, reverse, group_fn, init):
    def group(gi, carry):
        g = (n_groups - 1 - gi) if reverse else gi
        return group_fn(pl.multiple_of(g * SUBLANES, SUBLANES), carry)
    return lax.fori_loop(0, n_groups, group, init)


def _rg_scan_fwd(proj, xc, ra, ix, lam, *, name):
    s, width = xc.shape
    tc = LANES

    def body(gate_ref, xc_ref, ra_ref, ix_ref, lam_ref, hs_ref, gact_ref, a_s, u_s):
        first_row = _iota((s, tc), 0) == 0
        a, u = _rg_au(ra_ref[...], ix_ref[...], xc_ref[...], lam_ref[...], first_row)
        a_s[...] = a
        u_s[...] = u
        r8 = _iota((SUBLANES, tc), 0)

        def rows(r0, h):
            ag, ug = a_s[pl.ds(r0, SUBLANES), :], u_s[pl.ds(r0, SUBLANES), :]
            tile = jnp.zeros((SUBLANES, tc), F32)
            for r in range(SUBLANES):
                h = ag[r:r + 1, :] * h + ug[r:r + 1, :]
                tile = jnp.where(r8 == r, h, tile)
            hs_ref[pl.ds(r0, SUBLANES), :] = tile
            return h

        _scan_rows(s // SUBLANES, False, rows, jnp.zeros((1, tc), F32))
        gact_ref[...] = _rg_out(gate_ref[...], hs_ref[...]).astype(BF16)

    col = pl.BlockSpec((s, tc), lambda n: (0, n))
    return pl.pallas_call(
        body, name=name,
        grid=(width // tc,),
        in_specs=[col, col, col, col, pl.BlockSpec((1, tc), lambda n: (0, n))],
        out_specs=[col, col],
        out_shape=[jax.ShapeDtypeStruct((s, width), F32), jax.ShapeDtypeStruct((s, width), BF16)],
        scratch_shapes=[pltpu.VMEM((s, tc), F32)] * 2,
        compiler_params=_params("parallel"),
    )(proj, xc, ra, ix, lam)


def _rg_scan_bwd(dgo, proj, hs, xc, ra, ix, lam, *, name):
    s, width = xc.shape
    tc = LANES

    def body(dgo_ref, gate_ref, hs_ref, xc_ref, ra_ref, ix_ref, lam_ref,
             dgate_ref, dra_ref, dix_ref, dxc_ref, dlam_ref, a_s, dh_s, g_s):
        srow = _iota((s, tc), 0)
        hs = hs_ref[...]
        _, out_vjp = jax.vjp(_rg_out, gate_ref[...], hs)
        dgate, dh = out_vjp(dgo_ref[...])
        dgate_ref[...] = dgate.astype(BF16)
        au = functools.partial(_rg_au, first_row=srow == 0)
        (a, _), au_vjp = jax.vjp(au, ra_ref[...], ix_ref[...], xc_ref[...], lam_ref[...])
        a_s[...] = a
        dh_s[...] = dh
        r8 = _iota((SUBLANES, tc), 0)

        def rows(r0, carry):
            g, a_next = carry
            ag, dg = a_s[pl.ds(r0, SUBLANES), :], dh_s[pl.ds(r0, SUBLANES), :]
            tile = jnp.zeros((SUBLANES, tc), F32)
            for r in reversed(range(SUBLANES)):
                g = dg[r:r + 1, :] + a_next * g
                a_next = ag[r:r + 1, :]
                tile = jnp.where(r8 == r, g, tile)
            g_s[pl.ds(r0, SUBLANES), :] = tile
            return g, a_next

        zero = jnp.zeros((1, tc), F32)
        _scan_rows(s // SUBLANES, True, rows, (zero, zero))
        g = g_s[...]
        dra, dix, dxc, dlam = au_vjp((g * _shift_down(hs, 1, srow), g))
        dra_ref[...] = dra.astype(BF16)
        dix_ref[...] = dix.astype(BF16)
        dxc_ref[...] = dxc
        dlam_ref[...] = dlam

    col = pl.BlockSpec((s, tc), lambda n: (0, n))
    vec = pl.BlockSpec((1, tc), lambda n: (0, n))
    return pl.pallas_call(
        body, name=name,
        grid=(width // tc,),
        in_specs=[col] * 6 + [vec],
        out_specs=[col] * 4 + [vec],
        out_shape=[jax.ShapeDtypeStruct((s, width), BF16)] * 3
        + [jax.ShapeDtypeStruct((s, width), F32), jax.ShapeDtypeStruct((1, width), F32)],
        scratch_shapes=[pltpu.VMEM((s, tc), F32)] * 3,
        compiler_params=_params("parallel"),
    )(dgo, proj, hs, xc, ra, ix, lam)


def _rg_gates_bwd(dra, dix, dxc1, xc, proj, conv_w, wa, wx, *, name):
    s = proj.shape[0]
    nb = wa.shape[0]
    bw = RG_BLOCK

    def body(dra_ref, dix_ref, dxc_ref, xc_ref, xb_ref, cw_ref, wa_ref, wx_ref,
             dxb_ref, dcw_ref, dcb_ref, dwa_ref, dba_ref, dwx_ref, dbx_ref):
        dra, dix = dra_ref[...], dix_ref[...]
        xc_t = xc_ref[...].T.astype(BF16)
        dwa_ref[0] = jnp.dot(xc_t, dra, preferred_element_type=F32)
        dwx_ref[0] = jnp.dot(xc_t, dix, preferred_element_type=F32)
        dba_ref[0] = jnp.sum(dra.astype(F32), axis=0, keepdims=True)
        dbx_ref[0] = jnp.sum(dix.astype(F32), axis=0, keepdims=True)
        dxc = dxc_ref[...] + _dot(dra, wa_ref[0], _NT) + _dot(dix, wx_ref[0], _NT)
        srow = _iota((s, bw), 0)
        x = xb_ref[...]
        cw = cw_ref[...]
        dx = cw[0:1, :] * dxc
        dcw = [jnp.sum(dxc * x, axis=0, keepdims=True)]
        for tap in range(1, CONV_TAPS):
            dx = dx + cw[tap:tap + 1, :] * _shift_up(dxc, tap, srow)
            dcw.append(jnp.sum(dxc * _shift_down(x, tap, srow), axis=0, keepdims=True))
        dxb_ref[...] = dx.astype(BF16)
        r4 = _iota((CONV_TAPS, bw), 0)
        acc = jnp.zeros((CONV_TAPS, bw), F32)
        for tap in range(CONV_TAPS):
            acc = jnp.where(r4 == tap, dcw[tap], acc)
        dcw_ref[...] = acc
        dcb_ref[...] = jnp.sum(dxc, axis=0, keepdims=True)

    col = pl.BlockSpec((s, bw), lambda n: (0, n))
    vec = lambda r: pl.BlockSpec((r, bw), lambda n: (0, n))
    mat = pl.BlockSpec((1, bw, bw), lambda n: (n, 0, 0))
    bias = pl.BlockSpec((1, 1, bw), lambda n: (n, 0, 0))
    width = nb * bw
    return pl.pallas_call(
        body, name=name,
        grid=(nb,),
        in_specs=[col, col, col, col, pl.BlockSpec((s, bw), lambda n: (0, nb + n)), vec(CONV_TAPS), mat, mat],
        out_specs=[col, vec(CONV_TAPS), vec(1), mat, bias, mat, bias],
        out_shape=[jax.ShapeDtypeStruct((s, width), BF16), jax.ShapeDtypeStruct((CONV_TAPS, width), F32),
                   jax.ShapeDtypeStruct((1, width), F32), jax.ShapeDtypeStruct((nb, bw, bw), F32),
                   jax.ShapeDtypeStruct((nb, 1, bw), F32), jax.ShapeDtypeStruct((nb, bw, bw), F32),
                   jax.ShapeDtypeStruct((nb, 1, bw), F32)],
        compiler_params=_params("parallel"),
    )(dra, dix, dxc1, xc, proj, conv_w, wa, wx)


_HBM = pl.BlockSpec(memory_space=pltpu.HBM)
_FLIPS = ((0, 0, 1), (1, 0, 0), (0, 1, 0), (1, 1, 0))
_ALL_FLIPS = tuple((a, b, c) for a in (0, 1) for b in (0, 1) for c in (0, 1))[1:]


def _flip(pos, f):
    return tuple(1 - p if b else p for p, b in zip(pos, f))


def _dev_index(pos):
    return 4 * pos[0] + 2 * pos[1] + pos[2]


def _block(ref, idx, cols):
    if not cols:
        return ref.at[idx]
    n = ref.shape[-1] // N_DEV
    start = pl.multiple_of(idx * n, LANES)
    return ref.at[(slice(None),) * (len(ref.shape) - 1) + (pl.ds(start, n),)]


def _all_gather(xs, *, name, cols=False):
    n_arr = len(xs)

    def body(*refs):
        x_refs, out_refs = refs[:n_arr], refs[n_arr:2 * n_arr]
        send_sems, recv_sems, local_sems = refs[2 * n_arr:]
        me = (lax.axis_index("x"), lax.axis_index("y"), lax.axis_index("c"))
        sibling = _flip(me, _FLIPS[0])
        chips = [_flip(me, f) for f in _FLIPS[1:]]

        def copy(a, k, block, to, src=None):
            dst = _block(out_refs[a], _dev_index(block), cols)
            return pltpu.make_async_remote_copy(
                src_ref=dst if src is None else src, dst_ref=dst,
                send_sem=send_sems.at[7 * a + k], recv_sem=recv_sems.at[7 * a + k],
                device_id=to, device_id_type=pl.DeviceIdType.MESH)

        mine = [pltpu.make_async_copy(x_refs[a], _block(out_refs[a], _dev_index(me), cols), local_sems.at[a])
                for a in range(n_arr)]
        for cp in mine:
            cp.start()
        first = []
        for a in range(n_arr):
            first.append(copy(a, 0, me, sibling, src=x_refs[a]))
            first += [copy(a, 1 + j, me, chip, src=x_refs[a]) for j, chip in enumerate(chips)]
        for cp in first:
            cp.start()
        passed = []
        for j, chip in enumerate(chips):
            for a in range(n_arr):
                copy(a, 1 + j, chip, me).wait_recv()
                fwd = copy(a, 4 + j, chip, sibling)
                fwd.start()
                passed.append(fwd)
        for a in range(n_arr):
            copy(a, 0, sibling, me).wait_recv()
            for j, chip in enumerate(chips):
                copy(a, 4 + j, _flip(chip, _FLIPS[0]), me).wait_recv()
        for cp in first + passed:
            cp.wait_send()
        for cp in mine:
            cp.wait()

    def out_shape(x):
        shape = x.shape[:-1] + (N_DEV * x.shape[-1],) if cols else (N_DEV,) + x.shape
        return jax.ShapeDtypeStruct(shape, x.dtype)

    return pl.pallas_call(
        body, name=name,
        in_specs=[_HBM] * n_arr, out_specs=[_HBM] * n_arr,
        out_shape=[out_shape(x) for x in xs],
        scratch_shapes=[pltpu.SemaphoreType.DMA((7 * n_arr,)), pltpu.SemaphoreType.DMA((7 * n_arr,)),
                        pltpu.SemaphoreType.DMA((n_arr,))],
    )(*xs)


def _exchange(ps, *, name, cols=False):
    n_arr = len(ps)
    blk = ps[0].shape[:-1] + (ps[0].shape[-1] // N_DEV,) if cols else ps[0].shape[1:]

    def body(*refs):
        p_refs, out_ref = refs[:n_arr], refs[n_arr]
        send_sems, recv_sems, local_sems = refs[n_arr + 1:]
        me = (lax.axis_index("x"), lax.axis_index("y"), lax.axis_index("c"))
        me_idx = _dev_index(me)
        own = [pltpu.make_async_copy(_block(p_refs[a], me_idx, cols), out_ref.at[me_idx, a], local_sems.at[a])
               for a in range(n_arr)]
        for cp in own:
            cp.start()
        sends = []
        for k, f in enumerate(_ALL_FLIPS):
            peer = _flip(me, f)
            for a in range(n_arr):
                cp = pltpu.make_async_remote_copy(
                    src_ref=_block(p_refs[a], _dev_index(peer), cols), dst_ref=out_ref.at[me_idx, a],
                    send_sem=send_sems.at[7 * a + k], recv_sem=recv_sems.at[7 * a + k],
                    device_id=peer, device_id_type=pl.DeviceIdType.MESH)
                cp.start()
                sends.append(cp)
        for cp in sends:
            cp.wait()
        for cp in own:
            cp.wait()

    return pl.pallas_call(
        body, name=name,
        in_specs=[_HBM] * n_arr, out_specs=_HBM,
        out_shape=jax.ShapeDtypeStruct((N_DEV, n_arr) + blk, ps[0].dtype),
        scratch_shapes=[pltpu.SemaphoreType.DMA((7 * n_arr,)), pltpu.SemaphoreType.DMA((7 * n_arr,)),
                        pltpu.SemaphoreType.DMA((n_arr,))],
    )(*ps)


_SEM = pl.BlockSpec(memory_space=pltpu.SEMAPHORE)
_ANY = pl.BlockSpec(memory_space=pl.ANY)
_N_PEERS = N_DEV - 1


def _hbm(x):
    return pltpu.with_memory_space_constraint(x, pltpu.HBM)


def _me():
    return lax.axis_index("x"), lax.axis_index("y"), lax.axis_index("c")


def _spread_copies(plan, src_refs, land_refs, send_sems, recv_sems, local_sems):
    local, remote = plan(src_refs, land_refs)
    local = [pltpu.make_async_copy(src, dst, local_sems.at[i]) for i, (src, dst) in enumerate(local)]
    remote = [pltpu.make_async_remote_copy(src_ref=src, dst_ref=dst, send_sem=send_sems.at[k], recv_sem=recv_sems.at[k],
                                           device_id=peer, device_id_type=pl.DeviceIdType.MESH)
              for k, (src, dst, peer) in enumerate(remote)]
    return local, remote


def _spread_start(srcs, lands, plan, n_remote, n_local, *, name):
    ns, nl = len(srcs), len(lands)

    def body(*refs):
        src_refs, land_refs = refs[:ns], refs[ns:ns + nl]
        send_sems, recv_sems, local_sems = refs[ns + nl:ns + nl + 3]
        local, remote = _spread_copies(plan, src_refs, land_refs, send_sems, recv_sems, local_sems)
        for cp in local + remote:
            cp.start()
        token = refs[-1]
        token[...] = jnp.zeros_like(token)

    lands = [_hbm(lax.empty(*x)) if isinstance(x, tuple) else x for x in lands]
    out = pl.pallas_call(
        body, name=name,
        in_specs=[_HBM] * (ns + nl),
        out_specs=[_SEM] * 3 + [_HBM] * (ns + nl) + [pl.BlockSpec(memory_space=pltpu.VMEM)],
        out_shape=[pltpu.SemaphoreType.DMA((n_remote,)), pltpu.SemaphoreType.DMA((n_remote,)),
                   pltpu.SemaphoreType.DMA((max(n_local, 1),))]
        + [pltpu.HBM(x.shape, x.dtype) for x in list(srcs) + lands]
        + [jax.ShapeDtypeStruct((SUBLANES, LANES), F32)],
        input_output_aliases={i: 3 + i for i in range(ns + nl)},
        compiler_params=pltpu.CompilerParams(has_side_effects=pltpu.SideEffectType.DATAFLOW_SIDE_EFFECTING),
    )(*[_hbm(x) for x in srcs], *lands)
    return dict(sems=list(out[:3]), srcs=list(out[3:3 + ns]), lands=list(out[3 + ns:3 + ns + nl]),
                token=out[-1][0:1, 0:1], plan=plan)


def _spread_wait(handle, after, *, name):
    ns, nl = len(handle["srcs"]), len(handle["lands"])

    def body(*refs):
        src_refs, land_refs = refs[:ns], refs[ns:ns + nl]
        send_sems, recv_sems, local_sems = refs[ns + nl:ns + nl + 3]
        local, remote = _spread_copies(handle["plan"], src_refs, land_refs, send_sems, recv_sems, local_sems)
        for cp in local:
            cp.wait()
        for cp in remote:
            cp.wait_send()
            cp.wait_recv()

    out = pl.pallas_call(
        body, name=name,
        in_specs=[_HBM] * (ns + nl) + [_SEM] * 3 + [_ANY],
        out_specs=[_HBM] * (ns + nl),
        out_shape=[pltpu.HBM(x.shape, x.dtype) for x in handle["srcs"] + handle["lands"]],
        input_output_aliases={i: i for i in range(ns + nl)},
        compiler_params=pltpu.CompilerParams(has_side_effects=pltpu.SideEffectType.DATAFLOW_SIDE_EFFECTING),
    )(*handle["srcs"], *handle["lands"], *handle["sems"], after)
    return list(out[ns:])


def _gather_start(x, *, name, cols=False, relayed=True):
    shape = x.shape[:-1] + (N_DEV * x.shape[-1],) if cols else (N_DEV,) + x.shape
    flips = _FLIPS if relayed else _ALL_FLIPS

    def plan(src_refs, land_refs):
        me = _me()
        mine = _block(land_refs[0], _dev_index(me), cols)
        return [(src_refs[0], mine)], [(src_refs[0], mine, _flip(me, f)) for f in flips]

    handle = _spread_start([x], [(shape, x.dtype)], plan, len(flips), 1, name=name)
    handle["cols"] = cols
    return handle


def _gather_relay(handle, after, *, name):
    cols = handle["cols"]
    land, = _spread_wait(handle, after, name=f"{name}_arrived")

    def plan(src_refs, land_refs):
        me = _me()
        blocks = [_block(land_refs[0], _dev_index(_flip(me, f)), cols) for f in _FLIPS[1:]]
        return [], [(blk, blk, _flip(me, _FLIPS[0])) for blk in blocks]

    return _spread_start([], [land], plan, len(_FLIPS) - 1, 0, name=f"{name}_pass")


def _exchange_start(ps, *, name, cols=False):
    blk = ps[0].shape[:-1] + (ps[0].shape[-1] // N_DEV,) if cols else ps[0].shape[1:]

    def plan(src_refs, land_refs):
        me = _me()
        me_idx = _dev_index(me)
        local = [(_block(src, me_idx, cols), land_refs[0].at[me_idx, a]) for a, src in enumerate(src_refs)]
        remote = [(_block(src, _dev_index(_flip(me, f)), cols), land_refs[0].at[me_idx, a], _flip(me, f))
                  for f in _ALL_FLIPS for a, src in enumerate(src_refs)]
        return local, remote

    return _spread_start(ps, [((N_DEV, len(ps)) + blk, ps[0].dtype)], plan, _N_PEERS * len(ps), len(ps), name=name)


def _adamw(parts, w, m, v, *, name, layer=0, prev=None):
    n_rows, c = w.shape
    r = parts.shape[1]
    row_bytes = c * (N_DEV * parts.dtype.itemsize + 7 * 4) * 2
    tr = r
    for cand in (512, 256, 128, 64, 32, 16):
        if r % cand == 0 and cand * row_bytes <= 40 * 1024 * 1024:
            tr = cand
            break
    c1 = 1.0 - ADAM_B1 ** ADAM_STEP
    c2 = 1.0 - ADAM_B2 ** ADAM_STEP

    def body(p_ref, w_ref, m_ref, v_ref, *rest):
        g_ref, d_ref, nm_ref, nv_ref = rest[-4:]
        g = p_ref[0].astype(F32)
        for j in range(1, N_DEV):
            g = g + p_ref[j].astype(F32)
        nm = ADAM_B1 * m_ref[...] + (1.0 - ADAM_B1) * g
        nv = ADAM_B2 * v_ref[...] + (1.0 - ADAM_B2) * (g * g)
        g_ref[...] = g
        nm_ref[...] = nm
        nv_ref[...] = nv
        d_ref[...] = -ADAM_LR * ((nm / c1) / (jnp.sqrt(nv / c2) + ADAM_EPS) + ADAM_WD * w_ref[...])

    off = layer * (r // tr)
    blk = pl.BlockSpec((tr, c), lambda i: (i + off, 0))
    prev = list(prev) if prev is not None else []
    return pl.pallas_call(
        body, name=name,
        grid=(r // tr,),
        in_specs=[pl.BlockSpec((N_DEV, tr, c), lambda i: (0, i, 0)), blk, blk, blk] + [_ANY] * len(prev),
        out_specs=[blk] * 4,
        out_shape=[jax.ShapeDtypeStruct((n_rows, c), F32)] * 4,
        input_output_aliases={4 + j: j for j in range(len(prev))},
        compiler_params=_params("parallel"),
    )(parts, w, m, v, *prev)


_TN_CANDS = (512, 256, 128)
_TK_CANDS = (2048, 1536, 1408, 1024, 768, 512, 256, 128)


def _nn(a, b, name, out_dtype=F32):
    return _mm(a, b, "nn", name=name, out_dtype=out_dtype, tm=a.shape[0], tn=_pick(b.shape[1], _TN_CANDS),
               tk=_pick(a.shape[1], _TK_CANDS))


def _nt(a, b, name, out_dtype=F32):
    return _mm(a, b, "nt", name=name, out_dtype=out_dtype, tm=a.shape[0], tn=_pick(b.shape[0], _TN_CANDS),
               tk=_pick(a.shape[1], _TK_CANDS))


def _tn(a, b, name, out_dtype=BF16):
    return _mm(a, b, "tn", name=name, out_dtype=out_dtype, tm=_pick(a.shape[1], _TN_CANDS),
               tn=_pick(b.shape[1], (1024,) + _TN_CANDS))


def _local_step(x, p, target, rep, weight, emit, n_heads, ff, start_token=None):
    s, d = x.shape
    depth = p.shape[0]
    ff_tc = _pick(ff, (512, 256, 128))
    grads = {}
    rep_grads = {k: [None] * depth for k in ("mix_pre_g", "mix_post_g", "ffn_pre_g", "ffn_post_g", "ple_norm_g")}

    tokens = [start_token]

    def gain(name, i):
        g = rep[name][i:i + 1]
        return g if tokens[0] is None else g + tokens[0]

    def send(name, layer, g):
        token = emit(name, layer, g)
        if token is not None:
            tokens[0] = token if tokens[0] is None else tokens[0] + token

    saved = []
    h = x
    for i in range(depth):
        sv = {"h": h}
        n1, = _rowcall(f"pre_norm{i}", lambda hh, g: _rms(hh, g), [h], [gain("mix_pre_g", i)], [BF16], cols=d)
        sv["n1"] = n1
        if i % 2 == 0:
            proj = _nn(n1, weight("w_in_even", 0, n1), f"in_even{i}")
            a_out = _sb_fwd(proj, n_heads, name=f"sb_fwd{i}")
            b_out, oraw, states = _hg_fwd(proj, rep["hg_lb_logits"], rep["hg_norm_g"], n_heads, name=f"hg_fwd{i}")
            cat = jnp.concatenate([a_out.astype(BF16), b_out], axis=1)
            m = _nn(cat, weight("w_out_even", 0, cat), f"out_even{i}")
            sv.update(proj=proj, oraw=oraw, states=states, cat=cat)
        else:
            proj = _nn(n1, weight("w_in_odd", 0, n1), f"in_odd{i}")
            sm = {k: weight(k, 0, proj) for k in _SMALL}
            xc, ra, ix = _rg_gates_fwd(proj, sm["conv_w"], sm["conv_b"], sm["rg_wa"], sm["rg_ba"],
                                       sm["rg_wx"], sm["rg_bx"], name=f"rg_gates_fwd{i}")
            hs, gact = _rg_scan_fwd(proj, xc, ra, ix, sm["rg_lambda"], name=f"rg_scan_fwd{i}")
            m = _nn(gact, weight("w_out_odd", 0, gact), f"out_odd{i}")
            sv.update(proj=proj, xc=xc, ra=ra, ix=ix, hs=hs, gact=gact, sm=sm)

        def post_mix(hh, mm, g_post, g_pre):
            h1 = hh + _rms(mm, g_post)
            return h1, _rms(h1, g_pre)

        h1, n2 = _rowcall(f"post_mix{i}", post_mix, [h, m], [gain("mix_post_g", i), gain("ffn_pre_g", i)],
                          [F32, BF16], cols=d)
        gu = _nn(n2, weight("w_gate_up", i, n2), f"gate_up{i}")
        act, = _rowcall(f"swiglu{i}", lambda g, u: _silu(g) * u, [(gu, 0), (gu, ff // ff_tc)], [], [BF16],
                        cols=ff, tc=ff_tc)
        f = _nn(act, weight("w_down", i, act), f"down{i}")

        def post_ffn(hh, ff_out, g_post):
            h2 = hh + _rms(ff_out, g_post)
            return h2, h2

        h2, h2b = _rowcall(f"post_ffn{i}", post_ffn, [h1, f], [gain("ffn_post_g", i)], [F32, BF16], cols=d)
        e = _nn(p[i], weight("w_ple_up", i, h2b), f"ple_up{i}")
        gl = _nn(h2b, weight("w_ple_gate", i, h2b), f"ple_gate{i}")
        h3, = _rowcall(f"ple{i}", lambda hh, a, b, g: hh + _rms(_sigmoid(a) * b, g), [h2, gl, e],
                       [gain("ple_norm_g", i)], [F32], cols=d)
        sv.update(m=m, h1=h1, n2=n2, gu=gu, act=act, f=f, h2b=h2b, e=e, gl=gl)
        saved.append(sv)
        h = h3

    def loss_fn(y, t):
        err = y - t
        return err * (1.0 / d), jnp.sum(err * err, axis=0, keepdims=True) * (0.5 / d)

    dh, loss_cols = _rowcall("loss", loss_fn, [h, target], [], [F32], red_rows=(1,), cols=d)

    for i in reversed(range(depth)):
        sv = saved[i]

        def ple_bwd(dy, a, b, g):
            _, vjp = jax.vjp(lambda a_, b_, g_: _rms(_sigmoid(a_) * b_, g_), a, b, g)
            return vjp(dy)

        dgl, de, rep_grads["ple_norm_g"][i] = _rowcall(
            f"ple_bwd{i}", ple_bwd, [dh, sv["gl"], sv["e"]], [gain("ple_norm_g", i)], [BF16, BF16],
            red_rows=(1,), cols=d)
        send("w_ple_up", i, _tn(p[i], de, f"d_ple_up{i}"))
        send("w_ple_gate", i, _tn(sv["h2b"], dgl, f"d_ple_gate{i}"))
        dh2_ple = _nt(dgl, weight("w_ple_gate", i, dgl), f"dx_ple_gate{i}")

        def post_ffn_bwd(dy, dx, ff_out, g):
            dh2 = dy + dx
            _, vjp = jax.vjp(_rms, ff_out, g)
            df, dg = vjp(dh2)
            return dh2, df, dg

        dh2, df, rep_grads["ffn_post_g"][i] = _rowcall(
            f"post_ffn_bwd{i}", post_ffn_bwd, [dh, dh2_ple, sv["f"]], [gain("ffn_post_g", i)], [F32, BF16],
            red_rows=(1,), cols=d)
        send("w_down", i, _tn(sv["act"], df, f"d_down{i}"))
        dact = _nt(df, weight("w_down", i, df), f"dx_down{i}")

        def swiglu_bwd(g, u, dy):
            _, vjp = jax.vjp(lambda g_, u_: _silu(g_) * u_, g, u)
            return vjp(dy)

        dg_, du_ = _rowcall(f"swiglu_bwd{i}", swiglu_bwd, [(sv["gu"], 0), (sv["gu"], ff // ff_tc), dact], [],
                            [BF16, BF16], cols=ff, tc=ff_tc)
        dgu = jnp.concatenate([dg_, du_], axis=1)
        send("w_gate_up", i, _tn(sv["n2"], dgu, f"d_gate_up{i}"))
        dn2 = _nt(dgu, weight("w_gate_up", i, dgu), f"dx_gate_up{i}")

        def post_mix_bwd(dy, dn, h1, mm, g_post, g_pre):
            _, vjp_pre = jax.vjp(_rms, h1, g_pre)
            dh1_n, dg_pre = vjp_pre(dn)
            dh1 = dy + dh1_n
            _, vjp_post = jax.vjp(_rms, mm, g_post)
            dm, dg_post = vjp_post(dh1)
            return dh1, dm, dg_pre, dg_post

        dh1, dm, rep_grads["ffn_pre_g"][i], rep_grads["mix_post_g"][i] = _rowcall(
            f"post_mix_bwd{i}", post_mix_bwd, [dh2, dn2, sv["h1"], sv["m"]],
            [gain("mix_post_g", i), gain("ffn_pre_g", i)], [F32, BF16], red_rows=(1, 1), cols=d)

        if i % 2 == 0:
            send("w_out_even", 0, _tn(sv["cat"], dm, f"d_out_even{i}"))
            dcat = _nt(dm, weight("w_out_even", 0, dm), f"dx_out_even{i}")
            dq, dk, dv = _sb_bwd(sv["proj"], dcat, n_heads, name=f"sb_bwd{i}")
            dhq, dhf, dhi, dhg, grads["hg_lb_logits"], grads["hg_norm_g"] = _hg_bwd(
                sv["proj"], rep["hg_lb_logits"], rep["hg_norm_g"], sv["oraw"], sv["states"], dcat, n_heads,
                name=f"hg_bwd{i}")
            dproj = jnp.concatenate([dq.astype(BF16), dk.astype(BF16), dv.astype(BF16), dhq, dhf, dhi, dhg], axis=1)
            send("w_in_even", 0, _tn(sv["n1"], dproj, f"d_in_even{i}"))
            dn1 = _nt(dproj, weight("w_in_even", 0, dproj), f"dx_in_even{i}")
        else:
            sm = sv["sm"]
            send("w_out_odd", 0, _tn(sv["gact"], dm, f"d_out_odd{i}"))
            dgo = _nt(dm, weight("w_out_odd", 0, dm), f"dx_out_odd{i}")
            dgate, dra, dix, dxc1, grads["rg_lambda"] = _rg_scan_bwd(
                dgo, sv["proj"], sv["hs"], sv["xc"], sv["ra"], sv["ix"], sm["rg_lambda"], name=f"rg_scan_bwd{i}")
            (dxb, grads["conv_w"], grads["conv_b"], grads["rg_wa"], grads["rg_ba"], grads["rg_wx"],
             grads["rg_bx"]) = _rg_gates_bwd(dra, dix, dxc1, sv["xc"], sv["proj"], sm["conv_w"], sm["rg_wa"],
                                            sm["rg_wx"], name=f"rg_gates_bwd{i}")
            send("small", 0, {k: grads.pop(k) for k in _SMALL})
            dproj = jnp.concatenate([dgate, dxb], axis=1)
            send("w_in_odd", 0, _tn(sv["n1"], dproj, f"d_in_odd{i}"))
            dn1 = _nt(dproj, weight("w_in_odd", 0, dproj), f"dx_in_odd{i}")

        def pre_norm_bwd(dy, dn, hh, g):
            _, vjp = jax.vjp(_rms, hh, g)
            dx, dg = vjp(dn)
            return dy + dx, dg

        dh, rep_grads["mix_pre_g"][i] = _rowcall(
            f"pre_norm_bwd{i}", pre_norm_bwd, [dh1, dn1, sv["h"]], [gain("mix_pre_g", i)], [F32],
            red_rows=(1,), cols=d)

    for k, rows in rep_grads.items():
        grads[k] = jnp.concatenate(rows, axis=0)
    return loss_cols, dh, grads


_WEIGHTS = ("mix_pre_g", "mix_post_g", "ffn_pre_g", "ffn_post_g", "ple_norm_g", "w_in_even", "w_out_even",
            "hg_lb_logits", "hg_norm_g", "w_in_odd", "conv_w", "conv_b", "rg_wa", "rg_ba", "rg_wx", "rg_bx",
            "rg_lambda", "w_out_odd", "w_gate_up", "w_down", "w_ple_up", "w_ple_gate")
_REPLICATED = ("mix_pre_g", "mix_post_g", "ffn_pre_g", "ffn_post_g", "ple_norm_g", "hg_lb_logits", "hg_norm_g")
_SMALL = ("conv_w", "conv_b", "rg_wa", "rg_ba", "rg_wx", "rg_bx", "rg_lambda")
_BIG = {"w_in_even": True, "w_out_even": False, "w_in_odd": True, "w_out_odd": False,
        "w_gate_up": True, "w_down": False, "w_ple_up": True, "w_ple_gate": False}
_PACK_ROW = SUBLANES * LANES


def _pack(arrays):
    flat = jnp.concatenate([a.reshape(-1) for a in arrays])
    pad = -flat.shape[0] % _PACK_ROW
    return jnp.pad(flat, (0, pad)).reshape(-1, LANES)


def _pack_blocks(arrays):
    flat = jnp.concatenate([a.reshape(N_DEV, -1) for a in arrays], axis=1)
    pad = -flat.shape[1] % _PACK_ROW
    return jnp.pad(flat, ((0, 0), (0, pad))).reshape(N_DEV, -1, LANES)


def _unpack(packed, shapes, lead=()):
    flat = packed.reshape(lead + (-1,))
    out, pos = [], 0
    for shape in shapes:
        n = math.prod(shape)
        out.append(flat[..., pos:pos + n].reshape(lead + tuple(shape)))
        pos += n
    return out


def _to_full_small(name, blocks):
    if name == "conv_w":
        return jnp.transpose(blocks, (1, 0, 2)).reshape(blocks.shape[1], -1)
    if name in ("conv_b", "rg_lambda"):
        return blocks.reshape(1, -1)
    nb = blocks.shape[1]
    if name in ("rg_wa", "rg_wx"):
        return jnp.transpose(blocks, (1, 0, 2, 3)).reshape(nb, RG_BLOCK, RG_BLOCK)
    return jnp.transpose(blocks, (1, 0, 2)).reshape(nb, 1, RG_BLOCK)


def _to_blocks_small(name, full):
    if name == "conv_w":
        return jnp.transpose(full.reshape(full.shape[0], N_DEV, -1), (1, 0, 2))
    if name in ("conv_b", "rg_lambda"):
        return full.reshape(N_DEV, -1)
    nb = full.shape[0]
    if name in ("rg_wa", "rg_wx"):
        return jnp.transpose(full.reshape(nb, N_DEV, RG_BLOCK // N_DEV, RG_BLOCK), (1, 0, 2, 3))
    return jnp.transpose(full.reshape(nb, N_DEV, RG_BLOCK // N_DEV), (1, 0, 2))


def _step(inp):
    w = {k: inp[k] for k in _WEIGHTS}
    x, p, target = inp["x"][0], inp["p"][:, 0], inp["loss_target"][0]
    assert w["hg_lb_logits"].shape[0] == 2 and w["w_in_even"].shape[0] == 1 and w["w_in_odd"].shape[0] == 1

    n_heads = w["w_in_even"].shape[2] * N_DEV // (7 * HEAD_DIM)
    ff = w["w_down"].shape[1] * N_DEV
    small_shapes = [w[k].shape[1:] for k in _SMALL]

    def lands_in_place(name):
        return _BIG[name] and w[name].shape[2] % LANES == 0

    depth = p.shape[0]
    order = [("w_in_even", 0), ("w_out_even", 0)] if depth else []
    for i in range(depth):
        if i == 1:
            order += [("w_in_odd", 0), ("small", 0), ("w_out_odd", 0)]
        order += [("w_gate_up", i), ("w_down", i), ("w_ple_up", i), ("w_ple_gate", i)]
    gathers = {}
    for name, l in order:
        if name == "small":
            gathers[name, l] = _gather_start(_pack([w[k][0] for k in _SMALL]), name="gather_small", relayed=False)
        else:
            gathers[name, l] = _gather_start(w[name][l].astype(BF16), name=f"gather_{name}{l}",
                                             cols=lands_in_place(name))
    ready = {}

    def relay(key, after):
        if "cols" in gathers[key] and key[0] != "small":
            gathers[key] = _gather_relay(gathers[key], after, name=f"gather_{key[0]}{key[1]}")

    def weight(name, layer, after):
        key = ("small", 0) if name in _SMALL else (name, layer)
        if key not in ready:
            relay(key, after)
            for nxt in order[order.index(key) + 1:order.index(key) + 2]:
                relay(nxt, after)
            land, = _spread_wait(gathers[key], after, name=f"gathered_{key[0]}{key[1]}")
            if name in _SMALL:
                ready[key] = {k: _to_full_small(k, b)
                              for k, b in zip(_SMALL, _unpack(land, small_shapes, lead=(N_DEV,)))}
            elif lands_in_place(name):
                ready[key] = land
            elif _BIG[name]:
                ready[key] = jnp.transpose(land, (1, 0, 2)).reshape(land.shape[1], -1)
            else:
                ready[key] = land.reshape(-1, land.shape[2])
        return ready[key][name] if name in _SMALL else ready[key]

    exchanges = []

    def emit(name, layer, g):
        if name == "small":
            handle = _exchange_start([_pack_blocks([_to_blocks_small(k, g[k]) for k in _SMALL])],
                                     name="exchange_small")
            exchanges.append((name, layer, handle))
            return handle["token"]
        _, r, c = w[name].shape
        if lands_in_place(name):
            handle = _exchange_start([g], name=f"exchange_{name}{layer}", cols=True)
        elif _BIG[name]:
            handle = _exchange_start([jnp.transpose(g.reshape(-1, N_DEV, c), (1, 0, 2))],
                                     name=f"exchange_{name}{layer}")
        else:
            handle = _exchange_start([g.reshape(N_DEV, r, c)], name=f"exchange_{name}{layer}")
        exchanges.append((name, layer, handle))
        return handle["token"]

    rep = {k: w[k] for k in _REPLICATED}
    start_token = sum(h["token"] for h in gathers.values())
    loss_cols, dx, grads = _local_step(x, p, target, rep, weight, emit, n_heads, ff, start_token)

    loss_part = jnp.sum(loss_cols).reshape(1)
    rep_gather = _gather_start(_pack([grads[k] for k in _REPLICATED] + [loss_part]), name="gather_rep_grads",
                               relayed=False)

    out = {}
    after = dx
    for name, layer, handle in exchanges:
        land, = _spread_wait(handle, after, name=f"exchanged_{name}{layer}")
        if name == "small":
            res = _adamw(land.reshape(N_DEV, -1, LANES), *[_pack([inp[pre + k][0] for k in _SMALL]) for pre in ("", "m_", "v_")],
                         name="adamw_small")
            for k, *vals in zip(_SMALL, *[_unpack(a, small_shapes) for a in res]):
                out[k] = [v[None] for v in vals]
        else:
            n_l, r, c = w[name].shape
            res = out[name] = _adamw(land.reshape(N_DEV, r, c),
                                     *[inp[pre + name].reshape(n_l * r, c) for pre in ("", "m_", "v_")],
                                     name=f"adamw_{name}{layer}", layer=layer, prev=out.get(name))
        after = res[0]
    for name in _BIG:
        out[name] = [a.reshape(w[name].shape) for a in out[name]]

    rep_shapes = [w[k].shape for k in _REPLICATED] + [(1,)]
    rep_parts, = _spread_wait(rep_gather, after, name="gathered_rep_grads")
    res = _adamw(rep_parts, *[_pack([inp[pre + k] for k in _REPLICATED] + [jnp.zeros((1,), F32)])
                              for pre in ("", "m_", "v_")], name="adamw_rep")
    for k, *vals in zip(_REPLICATED + ("loss",), *[_unpack(a, rep_shapes) for a in res]):
        out[k] = vals
    loss = out["loss"][0][0]

    return (loss, dx[None]) + tuple(out[k][j] for j in range(4) for k in _WEIGHTS)


def kernel(x, p, mix_pre_g, mix_post_g, ffn_pre_g, ffn_post_g, ple_norm_g, w_in_even, w_out_even, hg_lb_logits, hg_norm_g, w_in_odd, conv_w, conv_b, rg_wa, rg_ba, rg_wx, rg_bx, rg_lambda, w_out_odd, w_gate_up, w_down, w_ple_up, w_ple_gate, loss_target, m_mix_pre_g, m_mix_post_g, m_ffn_pre_g, m_ffn_post_g, m_ple_norm_g, m_w_in_even, m_w_out_even, m_hg_lb_logits, m_hg_norm_g, m_w_in_odd, m_conv_w, m_conv_b, m_rg_wa, m_rg_ba, m_rg_wx, m_rg_bx, m_rg_lambda, m_w_out_odd, m_w_gate_up, m_w_down, m_w_ple_up, m_w_ple_gate, v_mix_pre_g, v_mix_post_g, v_ffn_pre_g, v_ffn_post_g, v_ple_norm_g, v_w_in_even, v_w_out_even, v_hg_lb_logits, v_hg_norm_g, v_w_in_odd, v_conv_w, v_conv_b, v_rg_wa, v_rg_ba, v_rg_wx, v_rg_bx, v_rg_lambda, v_w_out_odd, v_w_gate_up, v_w_down, v_w_ple_up, v_w_ple_gate):
    return _step(dict(locals()))
```

```python
import functools
import math

import jax
import jax.numpy as jnp
from jax import lax
from jax.experimental import pallas as pl
from jax.experimental.pallas import tpu as pltpu

F32 = jnp.float32
BF16 = jnp.bfloat16

VMEM_LIMIT_BYTES = 56 * 1024 * 1024
LANES = 128
SUBLANES = 8

N_DEV = 8
HEAD_DIM = 128
SB_Q_TILE = 512
SB_K_TILE = 128
HG_CHUNK = 32
RG_BLOCK = 256
CONV_TAPS = 4
RG_C = 8.0
RMS_EPS = 1e-6

ADAM_LR = 0.001
ADAM_B1 = 0.9
ADAM_B2 = 0.999
ADAM_EPS = 1e-08
ADAM_WD = 0.01
ADAM_STEP = 10

MESH_AXES = ("x", "y", "c")


def _params(*sem):
    return pltpu.CompilerParams(dimension_semantics=sem, vmem_limit_bytes=VMEM_LIMIT_BYTES)


def _pick(n, cands):
    for c in cands:
        if c <= n and n % c == 0:
            return c
    return n


def _mm(a, b, mode, *, name, out_dtype=F32, tm=512, tn=512, tk=None):
    if mode == "nn":
        (m, k), (k2, n) = a.shape, b.shape
    elif mode == "nt":
        (m, k), (n, k2) = a.shape, b.shape
    else:
        (k, m), (k2, n) = a.shape, b.shape
    assert k == k2, (a.shape, b.shape, mode)
    tm, tn = min(tm, m), min(tn, n)
    tk = k if tk is None else min(tk, k)
    assert m % tm == 0 and n % tn == 0 and k % tk == 0, (m, n, k, tm, tn, tk)
    nk = k // tk

    if mode == "tn":
        assert nk == 1
        return _mm_tn(a, b, name=name, out_dtype=out_dtype, tm=tm, tn=tn)

    a_spec = pl.BlockSpec((tm, tk), lambda i, j, kk: (i, kk))
    if mode == "nn":
        b_spec = pl.BlockSpec((tk, tn), lambda i, j, kk: (kk, j))
        dims = (((1,), (0,)), ((), ()))
    else:
        b_spec = pl.BlockSpec((tn, tk), lambda i, j, kk: (j, kk))
        dims = (((1,), (1,)), ((), ()))

    def body(a_ref, b_ref, o_ref, *acc):
        part = lax.dot_general(a_ref[...].astype(BF16), b_ref[...].astype(BF16), dims, preferred_element_type=F32)
        if nk == 1:
            o_ref[...] = part.astype(out_dtype)
        else:
            acc_ref, = acc
            kk = pl.program_id(2)

            @pl.when(kk == 0)
            def _():
                acc_ref[...] = part

            @pl.when(kk > 0)
            def _():
                acc_ref[...] += part

            @pl.when(kk == nk - 1)
            def _():
                o_ref[...] = acc_ref[...].astype(out_dtype)

    return pl.pallas_call(
        body, name=name,
        grid=(m // tm, n // tn, nk),
        in_specs=[a_spec, b_spec],
        out_specs=pl.BlockSpec((tm, tn), lambda i, j, kk: (i, j)),
        out_shape=jax.ShapeDtypeStruct((m, n), out_dtype),
        scratch_shapes=[] if nk == 1 else [pltpu.VMEM((tm, tn), F32)],
        compiler_params=_params("parallel", "parallel", "arbitrary"),
    )(a, b)


def _mm_tn(a, b, *, name, out_dtype, tm, tn):
    k, m = a.shape
    n = b.shape[1]

    def body(a_ref, b_ref, o_ref, at_ref):
        @pl.when(pl.program_id(1) == 0)
        def _():
            at_ref[...] = a_ref[...].astype(F32).T.astype(BF16)

        o_ref[...] = jnp.dot(at_ref[...], b_ref[...].astype(BF16), preferred_element_type=F32).astype(out_dtype)

    return pl.pallas_call(
        body, name=name,
        grid=(m // tm, n // tn),
        in_specs=[pl.BlockSpec((k, tm), lambda i, j: (0, i)), pl.BlockSpec((k, tn), lambda i, j: (0, j))],
        out_specs=pl.BlockSpec((tm, tn), lambda i, j: (i, j)),
        out_shape=jax.ShapeDtypeStruct((m, n), out_dtype),
        scratch_shapes=[pltpu.VMEM((tm, k), BF16)],
        compiler_params=_params("parallel", "arbitrary"),
    )(a, b)


def _rowcall(name, fn, rows, pars, row_outs, red_rows=(), *, cols, ts=256, tc=None):
    rows = [r if isinstance(r, tuple) else (r, 0) for r in rows]
    pars = [p if isinstance(p, tuple) else (p, 0) for p in pars]
    s = rows[0][0].shape[0]
    tc = cols if tc is None else tc
    ts = min(ts, s)
    assert s % ts == 0 and cols % tc == 0, (name, s, ts, cols, tc)
    n_in, n_row_out = len(rows) + len(pars), len(row_outs)

    def body(*refs):
        outs = fn(*[r[...] for r in refs[:n_in]])
        outs = outs if isinstance(outs, (tuple, list)) else (outs,)
        o_refs = refs[n_in:]
        for o_ref, val in zip(o_refs[:n_row_out], outs[:n_row_out]):
            o_ref[...] = val.astype(o_ref.dtype)
        first = pl.program_id(1) == 0
        for o_ref, val in zip(o_refs[n_row_out:], outs[n_row_out:]):
            @pl.when(first)
            def _(o_ref=o_ref, val=val):
                o_ref[...] = val

            @pl.when(jnp.logical_not(first))
            def _(o_ref=o_ref, val=val):
                o_ref[...] += val

    def row_map(off):
        return lambda j, i: (i, j + off)

    def par_map(off):
        return lambda j, i: (0, j + off)

    return pl.pallas_call(
        body, name=name,
        grid=(cols // tc, s // ts),
        in_specs=[pl.BlockSpec((ts, tc), row_map(off)) for _, off in rows]
        + [pl.BlockSpec((p.shape[0], tc), par_map(off)) for p, off in pars],
        out_specs=[pl.BlockSpec((ts, tc), lambda j, i: (i, j)) for _ in row_outs]
        + [pl.BlockSpec((r, tc), lambda j, i: (0, j)) for r in red_rows],
        out_shape=[jax.ShapeDtypeStruct((s, cols), dt) for dt in row_outs]
        + [jax.ShapeDtypeStruct((r, cols), F32) for r in red_rows],
        compiler_params=_params("parallel", "arbitrary"),
    )(*[r for r, _ in rows], *[p for p, _ in pars])


def _swiglu(gu, dact=None, *, name, ts=128):
    s, two_f = gu.shape
    f = two_f // 2
    ts = min(ts, s)

    def act(g, u):
        return _silu(g) * u

    def body(gu_ref, *refs):
        g, u = gu_ref[:, 0:f].astype(F32), gu_ref[:, f:two_f].astype(F32)
        if dact is None:
            refs[0][...] = act(g, u).astype(BF16)
        else:
            dact_ref, o_ref = refs
            _, vjp = jax.vjp(act, g, u)
            dg, du = vjp(dact_ref[...].astype(F32))
            o_ref[:, 0:f] = dg.astype(BF16)
            o_ref[:, f:two_f] = du.astype(BF16)

    wide, narrow = pl.BlockSpec((ts, two_f), lambda i: (i, 0)), pl.BlockSpec((ts, f), lambda i: (i, 0))
    return pl.pallas_call(
        body, name=name,
        grid=(s // ts,),
        in_specs=[wide] if dact is None else [wide, narrow],
        out_specs=narrow if dact is None else wide,
        out_shape=jax.ShapeDtypeStruct((s, f if dact is None else two_f), BF16),
        compiler_params=_params("parallel"),
    )(*([gu] if dact is None else [gu, dact]))


def _rms(x, g):
    return x * lax.rsqrt(jnp.mean(x * x, axis=-1, keepdims=True) + RMS_EPS) * g


def _sigmoid(x):
    return jax.nn.sigmoid(x)


def _silu(x):
    return x * jax.nn.sigmoid(x)


def _gelu(x):
    return 0.5 * x * (1.0 + jnp.tanh(math.sqrt(2.0 / math.pi) * (x + 0.044715 * (x * x * x))))


def _softplus(x):
    return jnp.maximum(x, 0.0) + jnp.log1p(jnp.exp(-jnp.abs(x)))


def _split3(x):
    hi = x.astype(BF16)
    r1 = x - hi.astype(F32)
    mid = r1.astype(BF16)
    lo = (r1 - mid.astype(F32)).astype(BF16)
    return hi, mid, lo


def _xdot(x, t):
    return sum(jnp.dot(p, t, preferred_element_type=F32) for p in _split3(x))


def _xdot_l(t, x):
    return sum(jnp.dot(t, p, preferred_element_type=F32) for p in _split3(x))


_NT = (((1,), (1,)), ((), ()))
_TN = (((0,), (0,)), ((), ()))


def _dot(a, b, dims=None):
    if dims is None:
        return jnp.dot(a.astype(BF16), b.astype(BF16), preferred_element_type=F32)
    return lax.dot_general(a.astype(BF16), b.astype(BF16), dims, preferred_element_type=F32)


def _iota(shape, axis):
    return lax.broadcasted_iota(jnp.int32, shape, axis)


def _sb_tile(qb, kblk, mask, upper, c_rem):
    z = lax.dot_general(qb, kblk, _NT, preferred_element_type=F32)
    soft = jnp.log1p(jnp.exp(-jnp.abs(z)))
    lbeta = jnp.minimum(z, 0.0) - soft
    l1m = jnp.where(mask, -jnp.maximum(z, 0.0) - soft, 0.0)
    rem = _xdot(l1m, upper) + c_rem
    w = jnp.where(mask, jnp.exp(lbeta + rem), 0.0)
    return lbeta, l1m, w


def _sb_tiles(s):
    tq = min(SB_Q_TILE, s)
    return tq, SB_K_TILE, tq // SB_K_TILE


def _sb_fwd(proj, n_heads, *, name):
    s = proj.shape[0]
    t, tk, per_q = _sb_tiles(s)
    scale = HEAD_DIM ** -0.5

    def body(q_ref, k_ref, v_ref, o_ref):
        qi = pl.program_id(1)
        qb = (q_ref[...] * scale).astype(BF16)
        row, col = _iota((t, tk), 0) + qi * t, _iota((t, tk), 1)
        upper = (_iota((tk, tk), 0) > _iota((tk, tk), 1)).astype(BF16)
        n_kb = (qi + 1) * per_q

        def step(j, carry):
            acc, c_rem = carry
            kb = n_kb - 1 - j
            rows = pl.ds(pl.multiple_of(kb * tk, tk), tk)
            kblk = k_ref[rows, :].astype(BF16)
            vblk = v_ref[rows, :].astype(BF16)
            _, l1m, w = _sb_tile(qb, kblk, (col + kb * tk) < row, upper, c_rem)
            acc = acc + jnp.dot(w.astype(BF16), vblk, preferred_element_type=F32)
            return acc, c_rem + jnp.sum(l1m, axis=1, keepdims=True)

        acc, _ = lax.fori_loop(0, n_kb, step, (jnp.zeros((t, HEAD_DIM), F32), jnp.zeros((t, 1), F32)))
        o_ref[...] = acc

    return pl.pallas_call(
        body, name=name,
        grid=(n_heads, s // t),
        in_specs=[pl.BlockSpec((t, HEAD_DIM), lambda h, i: (i, h)),
                  pl.BlockSpec((s, HEAD_DIM), lambda h, i: (0, n_heads + h)),
                  pl.BlockSpec((s, HEAD_DIM), lambda h, i: (0, 2 * n_heads + h))],
        out_specs=pl.BlockSpec((t, HEAD_DIM), lambda h, i: (i, h)),
        out_shape=jax.ShapeDtypeStruct((s, n_heads * HEAD_DIM), F32),
        compiler_params=_params("parallel", "arbitrary"),
    )(proj, proj, proj)


def _sb_bwd(proj, dcat, n_heads, *, name):
    s = proj.shape[0]
    t, tk, per_q = _sb_tiles(s)
    scale = HEAD_DIM ** -0.5

    def body(q_ref, k_ref, v_ref, do_ref, dq_ref, dk_ref, dv_ref, g_s, sig_s):
        qi = pl.program_id(1)

        @pl.when(qi == 0)
        def _():
            dk_ref[...] = jnp.zeros_like(dk_ref)
            dv_ref[...] = jnp.zeros_like(dv_ref)

        qb = (q_ref[...] * scale).astype(BF16)
        dob = do_ref[...].astype(BF16)
        row, col = _iota((t, tk), 0) + qi * t, _iota((t, tk), 1)
        upper = (_iota((tk, tk), 0) > _iota((tk, tk), 1)).astype(BF16)
        lower_incl = (_iota((tk, tk), 0) >= _iota((tk, tk), 1)).astype(BF16)
        n_kb = (qi + 1) * per_q

        def weights(j, carry):
            c_rem, g_all = carry
            kb = n_kb - 1 - j
            rows = pl.ds(pl.multiple_of(kb * tk, tk), tk)
            kblk = k_ref[rows, :].astype(BF16)
            vblk = v_ref[rows, :].astype(BF16)
            lbeta, l1m, w = _sb_tile(qb, kblk, (col + kb * tk) < row, upper, c_rem)
            g = w * lax.dot_general(dob, vblk, _NT, preferred_element_type=F32)
            dv_ref[rows, :] += lax.dot_general(w.astype(BF16), dob, _TN, preferred_element_type=F32)
            g_s[kb] = g
            sig_s[kb] = jnp.exp(lbeta)
            return c_rem + jnp.sum(l1m, axis=1, keepdims=True), g_all + jnp.sum(g, axis=1, keepdims=True)

        zero_col = jnp.zeros((t, 1), F32)
        _, g_all = lax.fori_loop(0, n_kb, weights, (zero_col, zero_col))

        def scores(j, carry):
            dq, c_g = carry
            kb = n_kb - 1 - j
            rows = pl.ds(pl.multiple_of(kb * tk, tk), tk)
            g, sig = g_s[kb], sig_s[kb]
            mask = (col + kb * tk) < row
            g_before = g_all - (_xdot(g, lower_incl) + c_g)
            dz = jnp.where(mask, g * (1.0 - sig) - g_before * sig, 0.0).astype(BF16)
            dq = dq + jnp.dot(dz, k_ref[rows, :].astype(BF16), preferred_element_type=F32)
            dk_ref[rows, :] += lax.dot_general(dz, qb, _TN, preferred_element_type=F32)
            return dq, c_g + jnp.sum(g, axis=1, keepdims=True)

        dq, _ = lax.fori_loop(0, n_kb, scores, (jnp.zeros((t, HEAD_DIM), F32), zero_col))
        dq_ref[...] = dq * scale

    width = n_heads * HEAD_DIM
    return pl.pallas_call(
        body, name=name,
        grid=(n_heads, s // t),
        in_specs=[pl.BlockSpec((t, HEAD_DIM), lambda h, i: (i, h)),
                  pl.BlockSpec((s, HEAD_DIM), lambda h, i: (0, n_heads + h)),
                  pl.BlockSpec((s, HEAD_DIM), lambda h, i: (0, 2 * n_heads + h)),
                  pl.BlockSpec((t, HEAD_DIM), lambda h, i: (i, h))],
        out_specs=[pl.BlockSpec((t, HEAD_DIM), lambda h, i: (i, h)),
                   pl.BlockSpec((s, HEAD_DIM), lambda h, i: (0, h)),
                   pl.BlockSpec((s, HEAD_DIM), lambda h, i: (0, h))],
        out_shape=[jax.ShapeDtypeStruct((s, width), F32)] * 3,
        scratch_shapes=[pltpu.VMEM((s // tk, t, tk), F32)] * 2,
        compiler_params=_params("parallel", "arbitrary"),
    )(proj, proj, proj, dcat)


def _hg_pre(hq, hf, logits):
    mx = jnp.max(logits, axis=0, keepdims=True)
    ex = jnp.exp(logits - mx)
    lb = ex[0:1, :] / jnp.sum(ex, axis=0, keepdims=True)
    f = lb + (1.0 - lb) * _sigmoid(hf)
    return _silu(hq), 1.0 - f, jnp.log(f)


def _hg_post(o, norm_g, hgate):
    return _rms(o, norm_g) * _silu(hgate)


def _hg_specs(s, n_heads, first_block):
    def at(group):
        return pl.BlockSpec((s, HEAD_DIM), lambda h: (0, first_block + group * n_heads + h))
    return [at(0), at(1), at(2), at(3)]


def _hg_fwd(proj, logits, norm_g, n_heads, *, name):
    s = proj.shape[0]
    hc = HG_CHUNK
    n_chunks = s // hc
    d = HEAD_DIM

    def body(lg_ref, ng_ref, hq_ref, hf_ref, hi_ref, hgt_ref, out_ref, oraw_ref, st_ref,
             q_s, k_s, lf_s, cum_s, qc_s, oc_s):
        q, k, lf = _hg_pre(hq_ref[...], hf_ref[...], lg_ref[...])
        q_s[...] = q
        k_s[...] = k
        lf_s[...] = lf
        tril = (_iota((hc, hc), 0) >= _iota((hc, hc), 1)).astype(BF16)
        srow = _iota((hc, d), 0)

        def chunk(ci, st):
            rows = pl.ds(pl.multiple_of(ci * hc, hc), hc)
            q, k, v = q_s[rows, :], k_s[rows, :], hi_ref[rows, :]
            cum = _xdot_l(tril, lf_s[rows, :])
            st_ref[0, ci] = st
            o_inter = _dot(q * jnp.exp(cum), st, _NT)
            cum_s[...] = cum
            qc_s[...] = q
            for t in range(hc):
                ng = (t // SUBLANES + 1) * SUBLANES
                e = jnp.where(srow[:ng] <= t, jnp.exp(cum_s[t:t + 1, :] - cum[:ng]), 0.0)
                sc = jnp.sum(qc_s[t:t + 1, :] * k[:ng] * e, axis=1, keepdims=True)
                oc_s[t:t + 1, :] = jnp.sum(sc * v[:ng], axis=0, keepdims=True)
            oraw_ref[rows, :] = o_inter + oc_s[...]
            last = cum_s[hc - 1:hc, :]
            return st * jnp.exp(last) + _dot(v, k * jnp.exp(last - cum), _TN)

        lax.fori_loop(0, n_chunks, chunk, jnp.zeros((d, d), F32))
        out_ref[...] = _hg_post(oraw_ref[...], ng_ref[...], hgt_ref[...]).astype(BF16)

    width = n_heads * d
    head_block = pl.BlockSpec((s, d), lambda h: (0, h))
    return pl.pallas_call(
        body, name=name,
        grid=(n_heads,),
        in_specs=[pl.BlockSpec((2, d), lambda h: (0, h)), pl.BlockSpec((1, d), lambda h: (0, 0))]
        + _hg_specs(s, n_heads, 3 * n_heads),
        out_specs=[head_block, head_block, pl.BlockSpec((1, n_chunks, d, d), lambda h: (h, 0, 0, 0))],
        out_shape=[jax.ShapeDtypeStruct((s, width), BF16), jax.ShapeDtypeStruct((s, width), F32),
                   jax.ShapeDtypeStruct((n_heads, n_chunks, d, d), F32)],
        scratch_shapes=[pltpu.VMEM((s, d), F32)] * 3 + [pltpu.VMEM((hc, d), F32)] * 3,
        compiler_params=_params("arbitrary"),
    )(logits, norm_g, proj, proj, proj, proj)


def _hg_bwd(proj, logits, norm_g, oraw, states, dcat, n_heads, *, name):
    s = proj.shape[0]
    hc = HG_CHUNK
    n_chunks = s // hc
    d = HEAD_DIM

    def body(lg_ref, ng_ref, hq_ref, hf_ref, hi_ref, hgt_ref, oraw_ref, st_ref, dout_ref,
             dhq_ref, dhf_ref, dhi_ref, dhgt_ref, dlg_ref, dng_ref,
             q_s, k_s, lf_s, do_s, dq_s, dk_s, dlf_s, cum_s, qc_s, doc_s, dqc_s, dkc_s, dvc_s):
        head = pl.program_id(0)
        (q, k, lf), pre_vjp = jax.vjp(_hg_pre, hq_ref[...], hf_ref[...], lg_ref[...])
        q_s[...] = q
        k_s[...] = k
        lf_s[...] = lf
        _, post_vjp = jax.vjp(_hg_post, oraw_ref[...], ng_ref[...], hgt_ref[...])
        do, dng, dhgt = post_vjp(dout_ref[...])
        do_s[...] = do
        dhgt_ref[...] = dhgt.astype(BF16)

        @pl.when(head == 0)
        def _():
            dng_ref[...] = dng

        @pl.when(head > 0)
        def _():
            dng_ref[...] += dng

        triu = (_iota((hc, hc), 0) <= _iota((hc, hc), 1)).astype(BF16)
        tril = (_iota((hc, hc), 0) >= _iota((hc, hc), 1)).astype(BF16)
        srow = _iota((hc, d), 0)

        def chunk(j, dst):
            ci = n_chunks - 1 - j
            rows = pl.ds(pl.multiple_of(ci * hc, hc), hc)
            q, k, v, do_c = q_s[rows, :], k_s[rows, :], hi_ref[rows, :], do_s[rows, :]
            cum = _xdot_l(tril, lf_s[rows, :])
            st = st_ref[0, ci]
            cum_s[...] = cum
            qc_s[...] = q
            doc_s[...] = do_c
            last = cum_s[hc - 1:hc, :]
            e_cum, e_last = jnp.exp(cum), jnp.exp(last - cum)
            dqc_s[...] = _dot(do_c, st) * e_cum
            dk_state = _dot(v, dst) * e_last
            dkc_s[...] = dk_state
            dvc_s[...] = _dot(k * e_last, dst, _NT)
            d_last = (jnp.sum(dst * st, axis=0, keepdims=True) * jnp.exp(last)
                      + jnp.sum(k * dk_state, axis=0, keepdims=True))
            for t in range(hc):
                ng = (t // SUBLANES + 1) * SUBLANES
                qt, dot_ = qc_s[t:t + 1, :], doc_s[t:t + 1, :]
                e = jnp.where(srow[:ng] <= t, jnp.exp(cum_s[t:t + 1, :] - cum[:ng]), 0.0)
                ke = k[:ng] * e
                d_a = jnp.sum(dot_ * v[:ng], axis=1, keepdims=True)
                dqc_s[t:t + 1, :] += jnp.sum(d_a * ke, axis=0, keepdims=True)
                dkc_s[0:ng, :] += d_a * (qt * e)
                dvc_s[0:ng, :] += jnp.sum(qt * ke, axis=1, keepdims=True) * dot_
            dq, dk = dqc_s[...], dkc_s[...]
            d_b = q * dq - k * dk
            dq_s[rows, :] = dq
            dk_s[rows, :] = dk
            dhi_ref[rows, :] = dvc_s[...].astype(BF16)
            dlf_s[rows, :] = _xdot_l(triu, d_b) + d_last
            return dst * jnp.exp(last) + _dot(do_c, q * e_cum, _TN)

        lax.fori_loop(0, n_chunks, chunk, jnp.zeros((d, d), F32))
        dhq, dhf, dlg = pre_vjp((dq_s[...], dk_s[...], dlf_s[...]))
        dhq_ref[...] = dhq.astype(BF16)
        dhf_ref[...] = dhf.astype(BF16)
        dlg_ref[...] = dlg

    width = n_heads * d
    head_block = pl.BlockSpec((s, d), lambda h: (0, h))
    return pl.pallas_call(
        body, name=name,
        grid=(n_heads,),
        in_specs=[pl.BlockSpec((2, d), lambda h: (0, h)), pl.BlockSpec((1, d), lambda h: (0, 0))]
        + _hg_specs(s, n_heads, 3 * n_heads)
        + [head_block, pl.BlockSpec((1, n_chunks, d, d), lambda h: (h, 0, 0, 0)),
           pl.BlockSpec((s, d), lambda h: (0, n_heads + h))],
        out_specs=[head_block] * 4 + [pl.BlockSpec((2, d), lambda h: (0, h)), pl.BlockSpec((1, d), lambda h: (0, 0))],
        out_shape=[jax.ShapeDtypeStruct((s, width), BF16)] * 4
        + [jax.ShapeDtypeStruct((2, width), F32), jax.ShapeDtypeStruct((1, d), F32)],
        scratch_shapes=[pltpu.VMEM((s, d), F32)] * 7 + [pltpu.VMEM((hc, d), F32)] * 6,
        compiler_params=_params("arbitrary"),
    )(logits, norm_g, proj, proj, proj, proj, oraw, states, dcat)


def _shift_down(x, n, srow):
    if n == 0:
        return x
    return jnp.where(srow >= n, pltpu.roll(x, n, 0), 0.0)


def _shift_up(x, n, srow):
    if n == 0:
        return x
    s = x.shape[0]
    return jnp.where(srow < s - n, pltpu.roll(x, s - n, 0), 0.0)


def _rg_gates_fwd(proj, conv_w, conv_b, wa, ba, wx, bx, *, name):
    s = proj.shape[0]
    nb = wa.shape[0]
    bw = RG_BLOCK

    def body(xb_ref, cw_ref, cb_ref, wa_ref, ba_ref, wx_ref, bx_ref, xc_ref, ra_ref, ix_ref):
        x = xb_ref[...]
        srow = _iota((s, bw), 0)
        cw = cw_ref[...]
        xc = cb_ref[...] + cw[0:1, :] * x
        for tap in range(1, CONV_TAPS):
            xc = xc + cw[tap:tap + 1, :] * _shift_down(x, tap, srow)
        xc_ref[...] = xc
        ra_ref[...] = _dot(xc, wa_ref[0]) + ba_ref[0]
        ix_ref[...] = _dot(xc, wx_ref[0]) + bx_ref[0]

    col = pl.BlockSpec((s, bw), lambda n: (0, n))
    vec = lambda r: pl.BlockSpec((r, bw), lambda n: (0, n))
    mat = pl.BlockSpec((1, bw, bw), lambda n: (n, 0, 0))
    bias = pl.BlockSpec((1, 1, bw), lambda n: (n, 0, 0))
    return pl.pallas_call(
        body, name=name,
        grid=(nb,),
        in_specs=[pl.BlockSpec((s, bw), lambda n: (0, nb + n)), vec(CONV_TAPS), vec(1), mat, bias, mat, bias],
        out_specs=[col] * 3,
        out_shape=[jax.ShapeDtypeStruct((s, nb * bw), F32)] * 3,
        compiler_params=_params("parallel"),
    )(proj, conv_w, conv_b, wa, ba, wx, bx)


def _rg_au(ra, ix, xc, lam, first_row):
    log_a = -RG_C * _sigmoid(ra) * _softplus(-lam)
    th = jnp.tanh(log_a)
    one_minus_a2 = -2.0 * th / (1.0 - th)
    mult = jnp.where(first_row, 1.0, jnp.sqrt(one_minus_a2))
    return jnp.exp(log_a), xc * _sigmoid(ix) * mult


def _rg_out(gate, hs):
    return _gelu(gate) * hs


def _linear_scan(a, b, a_s, b_s, in_s, reverse):
    s, c = a.shape
    within = _iota((s, c), 0) & (SUBLANES - 1)
    shift = 1
    while shift < SUBLANES:
        if reverse:
            take = within < SUBLANES - shift
            a_n, b_n = pltpu.roll(a, s - shift, 0), pltpu.roll(b, s - shift, 0)
        else:
            take = within >= shift
            a_n, b_n = pltpu.roll(a, shift, 0), pltpu.roll(b, shift, 0)
        b = jnp.where(take, a * b_n + b, b)
        a = jnp.where(take, a * a_n, a)
        shift *= 2
    a_s[...] = a
    b_s[...] = b
    n_tiles = s // SUBLANES
    edge = 0 if reverse else SUBLANES - 1

    def tile(i, h):
        rows = pl.ds(pl.multiple_of(((n_tiles - 1 - i) if reverse else i) * SUBLANES, SUBLANES), SUBLANES)
        in_s[rows, :] = jnp.broadcast_to(h, (SUBLANES, c))
        return a_s[rows, :][edge:edge + 1, :] * h + b_s[rows, :][edge:edge + 1, :]

    lax.fori_loop(0, n_tiles, tile, jnp.zeros((1, c), F32))
    return a * in_s[...] + b


def _rg_scan_fwd(proj, xc, ra, ix, lam, *, name):
    s, width = xc.shape
    tc = LANES

    def body(gate_ref, xc_ref, ra_ref, ix_ref, lam_ref, hs_ref, gact_ref, a_s, u_s, in_s):
        first_row = _iota((s, tc), 0) == 0
        a, u = _rg_au(ra_ref[...], ix_ref[...], xc_ref[...], lam_ref[...], first_row)
        hs = _linear_scan(a, u, a_s, u_s, in_s, reverse=False)
        hs_ref[...] = hs
        gact_ref[...] = _rg_out(gate_ref[...], hs).astype(BF16)

    col = pl.BlockSpec((s, tc), lambda n: (0, n))
    return pl.pallas_call(
        body, name=name,
        grid=(width // tc,),
        in_specs=[col, col, col, col, pl.BlockSpec((1, tc), lambda n: (0, n))],
        out_specs=[col, col],
        out_shape=[jax.ShapeDtypeStruct((s, width), F32), jax.ShapeDtypeStruct((s, width), BF16)],
        scratch_shapes=[pltpu.VMEM((s, tc), F32)] * 3,
        compiler_params=_params("parallel"),
    )(proj, xc, ra, ix, lam)


def _rg_scan_bwd(dgo, proj, hs, xc, ra, ix, lam, *, name):
    s, width = xc.shape
    tc = LANES

    def body(dgo_ref, gate_ref, hs_ref, xc_ref, ra_ref, ix_ref, lam_ref,
             dgate_ref, dra_ref, dix_ref, dxc_ref, dlam_ref, a_s, dh_s, g_s):
        srow = _iota((s, tc), 0)
        hs = hs_ref[...]
        _, out_vjp = jax.vjp(_rg_out, gate_ref[...], hs)
        dgate, dh = out_vjp(dgo_ref[...])
        dgate_ref[...] = dgate.astype(BF16)
        au = functools.partial(_rg_au, first_row=srow == 0)
        (a, _), au_vjp = jax.vjp(au, ra_ref[...], ix_ref[...], xc_ref[...], lam_ref[...])
        g = _linear_scan(_shift_up(a, 1, srow), dh, a_s, dh_s, g_s, reverse=True)
        dra, dix, dxc, dlam = au_vjp((g * _shift_down(hs, 1, srow), g))
        dra_ref[...] = dra.astype(BF16)
        dix_ref[...] = dix.astype(BF16)
        dxc_ref[...] = dxc
        dlam_ref[...] = dlam

    col = pl.BlockSpec((s, tc), lambda n: (0, n))
    vec = pl.BlockSpec((1, tc), lambda n: (0, n))
    return pl.pallas_call(
        body, name=name,
        grid=(width // tc,),
        in_specs=[col] * 6 + [vec],
        out_specs=[col] * 4 + [vec],
        out_shape=[jax.ShapeDtypeStruct((s, width), BF16)] * 3
        + [jax.ShapeDtypeStruct((s, width), F32), jax.ShapeDtypeStruct((1, width), F32)],
        scratch_shapes=[pltpu.VMEM((s, tc), F32)] * 3,
        compiler_params=_params("parallel"),
    )(dgo, proj, hs, xc, ra, ix, lam)


def _rg_gates_bwd(dra, dix, dxc1, xc, proj, conv_w, wa, wx, *, name):
    s = proj.shape[0]
    nb = wa.shape[0]
    bw = RG_BLOCK

    def body(dra_ref, dix_ref, dxc_ref, xc_ref, xb_ref, cw_ref, wa_ref, wx_ref,
             dxb_ref, dcw_ref, dcb_ref, dwa_ref, dba_ref, dwx_ref, dbx_ref):
        dra, dix = dra_ref[...], dix_ref[...]
        xc_t = xc_ref[...].T.astype(BF16)
        dwa_ref[0] = jnp.dot(xc_t, dra, preferred_element_type=F32)
        dwx_ref[0] = jnp.dot(xc_t, dix, preferred_element_type=F32)
        dba_ref[0] = jnp.sum(dra.astype(F32), axis=0, keepdims=True)
        dbx_ref[0] = jnp.sum(dix.astype(F32), axis=0, keepdims=True)
        dxc = dxc_ref[...] + _dot(dra, wa_ref[0], _NT) + _dot(dix, wx_ref[0], _NT)
        srow = _iota((s, bw), 0)
        x = xb_ref[...]
        cw = cw_ref[...]
        dx = cw[0:1, :] * dxc
        dcw = [jnp.sum(dxc * x, axis=0, keepdims=True)]
        for tap in range(1, CONV_TAPS):
            dx = dx + cw[tap:tap + 1, :] * _shift_up(dxc, tap, srow)
            dcw.append(jnp.sum(dxc * _shift_down(x, tap, srow), axis=0, keepdims=True))
        dxb_ref[...] = dx.astype(BF16)
        r4 = _iota((CONV_TAPS, bw), 0)
        acc = jnp.zeros((CONV_TAPS, bw), F32)
        for tap in range(CONV_TAPS):
            acc = jnp.where(r4 == tap, dcw[tap], acc)
        dcw_ref[...] = acc
        dcb_ref[...] = jnp.sum(dxc, axis=0, keepdims=True)

    col = pl.BlockSpec((s, bw), lambda n: (0, n))
    vec = lambda r: pl.BlockSpec((r, bw), lambda n: (0, n))
    mat = pl.BlockSpec((1, bw, bw), lambda n: (n, 0, 0))
    bias = pl.BlockSpec((1, 1, bw), lambda n: (n, 0, 0))
    width = nb * bw
    return pl.pallas_call(
        body, name=name,
        grid=(nb,),
        in_specs=[col, col, col, col, pl.BlockSpec((s, bw), lambda n: (0, nb + n)), vec(CONV_TAPS), mat, mat],
        out_specs=[col, vec(CONV_TAPS), vec(1), mat, bias, mat, bias],
        out_shape=[jax.ShapeDtypeStruct((s, width), BF16), jax.ShapeDtypeStruct((CONV_TAPS, width), F32),
                   jax.ShapeDtypeStruct((1, width), F32), jax.ShapeDtypeStruct((nb, bw, bw), F32),
                   jax.ShapeDtypeStruct((nb, 1, bw), F32), jax.ShapeDtypeStruct((nb, bw, bw), F32),
                   jax.ShapeDtypeStruct((nb, 1, bw), F32)],
        compiler_params=_params("parallel"),
    )(dra, dix, dxc1, xc, proj, conv_w, wa, wx)


_HBM = pl.BlockSpec(memory_space=pltpu.HBM)
_FLIPS = ((0, 0, 1), (1, 0, 0), (0, 1, 0), (1, 1, 0))
_ALL_FLIPS = tuple((a, b, c) for a in (0, 1) for b in (0, 1) for c in (0, 1))[1:]


def _flip(pos, f):
    return tuple(1 - p if b else p for p, b in zip(pos, f))


def _dev_index(pos):
    return 4 * pos[0] + 2 * pos[1] + pos[2]


def _block(ref, idx, cols):
    if not cols:
        return ref.at[idx]
    n = ref.shape[-1] // N_DEV
    start = pl.multiple_of(idx * n, LANES)
    return ref.at[(slice(None),) * (len(ref.shape) - 1) + (pl.ds(start, n),)]


def _all_gather(xs, *, name, cols=False):
    n_arr = len(xs)

    def body(*refs):
        x_refs, out_refs = refs[:n_arr], refs[n_arr:2 * n_arr]
        send_sems, recv_sems, local_sems = refs[2 * n_arr:]
        me = (lax.axis_index("x"), lax.axis_index("y"), lax.axis_index("c"))
        sibling = _flip(me, _FLIPS[0])
        chips = [_flip(me, f) for f in _FLIPS[1:]]

        def copy(a, k, block, to, src=None):
            dst = _block(out_refs[a], _dev_index(block), cols)
            return pltpu.make_async_remote_copy(
                src_ref=dst if src is None else src, dst_ref=dst,
                send_sem=send_sems.at[7 * a + k], recv_sem=recv_sems.at[7 * a + k],
                device_id=to, device_id_type=pl.DeviceIdType.MESH)

        mine = [pltpu.make_async_copy(x_refs[a], _block(out_refs[a], _dev_index(me), cols), local_sems.at[a])
                for a in range(n_arr)]
        for cp in mine:
            cp.start()
        first = []
        for a in range(n_arr):
            first.append(copy(a, 0, me, sibling, src=x_refs[a]))
            first += [copy(a, 1 + j, me, chip, src=x_refs[a]) for j, chip in enumerate(chips)]
        for cp in first:
            cp.start()
        passed = []
        for j, chip in enumerate(chips):
            for a in range(n_arr):
                copy(a, 1 + j, chip, me).wait_recv()
                fwd = copy(a, 4 + j, chip, sibling)
                fwd.start()
                passed.append(fwd)
        for a in range(n_arr):
            copy(a, 0, sibling, me).wait_recv()
            for j, chip in enumerate(chips):
                copy(a, 4 + j, _flip(chip, _FLIPS[0]), me).wait_recv()
        for cp in first + passed:
            cp.wait_send()
        for cp in mine:
            cp.wait()

    def out_shape(x):
        shape = x.shape[:-1] + (N_DEV * x.shape[-1],) if cols else (N_DEV,) + x.shape
        return jax.ShapeDtypeStruct(shape, x.dtype)

    return pl.pallas_call(
        body, name=name,
        in_specs=[_HBM] * n_arr, out_specs=[_HBM] * n_arr,
        out_shape=[out_shape(x) for x in xs],
        scratch_shapes=[pltpu.SemaphoreType.DMA((7 * n_arr,)), pltpu.SemaphoreType.DMA((7 * n_arr,)),
                        pltpu.SemaphoreType.DMA((n_arr,))],
    )(*xs)


def _exchange(ps, *, name, cols=False):
    n_arr = len(ps)
    blk = ps[0].shape[:-1] + (ps[0].shape[-1] // N_DEV,) if cols else ps[0].shape[1:]

    def body(*refs):
        p_refs, out_ref = refs[:n_arr], refs[n_arr]
        send_sems, recv_sems, local_sems = refs[n_arr + 1:]
        me = (lax.axis_index("x"), lax.axis_index("y"), lax.axis_index("c"))
        me_idx = _dev_index(me)
        own = [pltpu.make_async_copy(_block(p_refs[a], me_idx, cols), out_ref.at[me_idx, a], local_sems.at[a])
               for a in range(n_arr)]
        for cp in own:
            cp.start()
        sends = []
        for k, f in enumerate(_ALL_FLIPS):
            peer = _flip(me, f)
            for a in range(n_arr):
                cp = pltpu.make_async_remote_copy(
                    src_ref=_block(p_refs[a], _dev_index(peer), cols), dst_ref=out_ref.at[me_idx, a],
                    send_sem=send_sems.at[7 * a + k], recv_sem=recv_sems.at[7 * a + k],
                    device_id=peer, device_id_type=pl.DeviceIdType.MESH)
                cp.start()
                sends.append(cp)
        for cp in sends:
            cp.wait()
        for cp in own:
            cp.wait()

    return pl.pallas_call(
        body, name=name,
        in_specs=[_HBM] * n_arr, out_specs=_HBM,
        out_shape=jax.ShapeDtypeStruct((N_DEV, n_arr) + blk, ps[0].dtype),
        scratch_shapes=[pltpu.SemaphoreType.DMA((7 * n_arr,)), pltpu.SemaphoreType.DMA((7 * n_arr,)),
                        pltpu.SemaphoreType.DMA((n_arr,))],
    )(*ps)


_SEM = pl.BlockSpec(memory_space=pltpu.SEMAPHORE)
_ANY = pl.BlockSpec(memory_space=pl.ANY)
_N_PEERS = N_DEV - 1


def _hbm(x):
    return pltpu.with_memory_space_constraint(x, pltpu.HBM)


def _me():
    return lax.axis_index("x"), lax.axis_index("y"), lax.axis_index("c")


def _spread_copies(plan, src_refs, land_refs, send_sems, recv_sems, local_sems):
    local, remote = plan(src_refs, land_refs)
    local = [pltpu.make_async_copy(src, dst, local_sems.at[i]) for i, (src, dst) in enumerate(local)]
    remote = [pltpu.make_async_remote_copy(src_ref=src, dst_ref=dst, send_sem=send_sems.at[k], recv_sem=recv_sems.at[k],
                                           device_id=peer, device_id_type=pl.DeviceIdType.MESH)
              for k, (src, dst, peer) in enumerate(remote)]
    return local, remote


def _spread_start(srcs, lands, plan, n_remote, n_local, *, name):
    ns, nl = len(srcs), len(lands)

    def body(*refs):
        src_refs, land_refs = refs[:ns], refs[ns:ns + nl]
        send_sems, recv_sems, local_sems = refs[ns + nl:ns + nl + 3]
        local, remote = _spread_copies(plan, src_refs, land_refs, send_sems, recv_sems, local_sems)
        for cp in local + remote:
            cp.start()
        token = refs[-1]
        token[...] = jnp.zeros_like(token)

    lands = [_hbm(lax.empty(*x)) if isinstance(x, tuple) else x for x in lands]
    out = pl.pallas_call(
        body, name=name,
        in_specs=[_HBM] * (ns + nl),
        out_specs=[_SEM] * 3 + [_HBM] * (ns + nl) + [pl.BlockSpec(memory_space=pltpu.VMEM)],
        out_shape=[pltpu.SemaphoreType.DMA((n_remote,)), pltpu.SemaphoreType.DMA((n_remote,)),
                   pltpu.SemaphoreType.DMA((max(n_local, 1),))]
        + [pltpu.HBM(x.shape, x.dtype) for x in list(srcs) + lands]
        + [jax.ShapeDtypeStruct((SUBLANES, LANES), F32)],
        input_output_aliases={i: 3 + i for i in range(ns + nl)},
        compiler_params=pltpu.CompilerParams(has_side_effects=pltpu.SideEffectType.DATAFLOW_SIDE_EFFECTING),
    )(*[_hbm(x) for x in srcs], *lands)
    return dict(sems=list(out[:3]), srcs=list(out[3:3 + ns]), lands=list(out[3 + ns:3 + ns + nl]),
                token=out[-1][0:1, 0:1], plan=plan)


def _spread_wait(handle, after, *, name):
    ns, nl = len(handle["srcs"]), len(handle["lands"])

    def body(*refs):
        src_refs, land_refs = refs[:ns], refs[ns:ns + nl]
        send_sems, recv_sems, local_sems = refs[ns + nl:ns + nl + 3]
        local, remote = _spread_copies(handle["plan"], src_refs, land_refs, send_sems, recv_sems, local_sems)
        for cp in local:
            cp.wait()
        for cp in remote:
            cp.wait_send()
            cp.wait_recv()

    out = pl.pallas_call(
        body, name=name,
        in_specs=[_HBM] * (ns + nl) + [_SEM] * 3 + [_ANY],
        out_specs=[_HBM] * (ns + nl),
        out_shape=[pltpu.HBM(x.shape, x.dtype) for x in handle["srcs"] + handle["lands"]],
        input_output_aliases={i: i for i in range(ns + nl)},
        compiler_params=pltpu.CompilerParams(has_side_effects=pltpu.SideEffectType.DATAFLOW_SIDE_EFFECTING),
    )(*handle["srcs"], *handle["lands"], *handle["sems"], after)
    return list(out[ns:])


def _gather_start(x, *, name, cols=False, relayed=True):
    shape = x.shape[:-1] + (N_DEV * x.shape[-1],) if cols else (N_DEV,) + x.shape
    flips = _FLIPS if relayed else _ALL_FLIPS

    def plan(src_refs, land_refs):
        me = _me()
        mine = _block(land_refs[0], _dev_index(me), cols)
        return [(src_refs[0], mine)], [(src_refs[0], mine, _flip(me, f)) for f in flips]

    handle = _spread_start([x], [(shape, x.dtype)], plan, len(flips), 1, name=name)
    handle["cols"] = cols
    return handle


def _gather_relay(handle, after, *, name):
    cols = handle["cols"]
    land, = _spread_wait(handle, after, name=f"{name}_arrived")

    def plan(src_refs, land_refs):
        me = _me()
        blocks = [_block(land_refs[0], _dev_index(_flip(me, f)), cols) for f in _FLIPS[1:]]
        return [], [(blk, blk, _flip(me, _FLIPS[0])) for blk in blocks]

    return _spread_start([], [land], plan, len(_FLIPS) - 1, 0, name=f"{name}_pass")


def _exchange_start(ps, *, name, cols=False):
    blk = ps[0].shape[:-1] + (ps[0].shape[-1] // N_DEV,) if cols else ps[0].shape[1:]

    def plan(src_refs, land_refs):
        me = _me()
        me_idx = _dev_index(me)
        local = [(_block(src, me_idx, cols), land_refs[0].at[me_idx, a]) for a, src in enumerate(src_refs)]
        remote = [(_block(src, _dev_index(_flip(me, f)), cols), land_refs[0].at[me_idx, a], _flip(me, f))
                  for f in _ALL_FLIPS for a, src in enumerate(src_refs)]
        return local, remote

    return _spread_start(ps, [((N_DEV, len(ps)) + blk, ps[0].dtype)], plan, _N_PEERS * len(ps), len(ps), name=name)


def _adamw(parts, w, m, v, *, name, layer=0, prev=None):
    n_rows, c = w.shape
    r = parts.shape[1]
    row_bytes = c * (N_DEV * parts.dtype.itemsize + 7 * 4) * 2
    tr = r
    for cand in (512, 256, 128, 64, 32, 16):
        if r % cand == 0 and cand * row_bytes <= 40 * 1024 * 1024:
            tr = cand
            break
    c1 = 1.0 - ADAM_B1 ** ADAM_STEP
    c2 = 1.0 - ADAM_B2 ** ADAM_STEP

    def body(p_ref, w_ref, m_ref, v_ref, *rest):
        g_ref, d_ref, nm_ref, nv_ref = rest[-4:]
        g = p_ref[0].astype(F32)
        for j in range(1, N_DEV):
            g = g + p_ref[j].astype(F32)
        nm = ADAM_B1 * m_ref[...] + (1.0 - ADAM_B1) * g
        nv = ADAM_B2 * v_ref[...] + (1.0 - ADAM_B2) * (g * g)
        g_ref[...] = g
        nm_ref[...] = nm
        nv_ref[...] = nv
        d_ref[...] = -ADAM_LR * ((nm / c1) / (jnp.sqrt(nv / c2) + ADAM_EPS) + ADAM_WD * w_ref[...])

    off = layer * (r // tr)
    blk = pl.BlockSpec((tr, c), lambda i: (i + off, 0))
    prev = list(prev) if prev is not None else []
    return pl.pallas_call(
        body, name=name,
        grid=(r // tr,),
        in_specs=[pl.BlockSpec((N_DEV, tr, c), lambda i: (0, i, 0)), blk, blk, blk] + [_ANY] * len(prev),
        out_specs=[blk] * 4,
        out_shape=[jax.ShapeDtypeStruct((n_rows, c), F32)] * 4,
        input_output_aliases={4 + j: j for j in range(len(prev))},
        compiler_params=_params("parallel"),
    )(parts, w, m, v, *prev)


_TN_CANDS = (512, 256, 128)
_TK_CANDS = (2048, 1536, 1408, 1024, 768, 512, 256, 128)


def _nn(a, b, name, out_dtype=F32):
    return _mm(a, b, "nn", name=name, out_dtype=out_dtype, tm=a.shape[0], tn=_pick(b.shape[1], _TN_CANDS),
               tk=_pick(a.shape[1], _TK_CANDS))


def _nt(a, b, name, out_dtype=F32):
    return _mm(a, b, "nt", name=name, out_dtype=out_dtype, tm=a.shape[0], tn=_pick(b.shape[0], _TN_CANDS),
               tk=_pick(a.shape[1], _TK_CANDS))


def _tn(a, b, name, out_dtype=BF16):
    return _mm(a, b, "tn", name=name, out_dtype=out_dtype, tm=_pick(a.shape[1], _TN_CANDS),
               tn=_pick(b.shape[1], (1024,) + _TN_CANDS))


def _local_step(x, p, target, rep, weight, emit, n_heads, ff, start_token=None):
    s, d = x.shape
    depth = p.shape[0]
    grads = {}
    rep_grads = {k: [None] * depth for k in ("mix_pre_g", "mix_post_g", "ffn_pre_g", "ffn_post_g", "ple_norm_g")}

    tokens = [start_token]

    def gain(name, i):
        g = rep[name][i:i + 1]
        return g if tokens[0] is None else g + tokens[0]

    def send(name, layer, g):
        token = emit(name, layer, g)
        if token is not None:
            tokens[0] = token if tokens[0] is None else tokens[0] + token

    saved = []
    h = x
    for i in range(depth):
        sv = {"h": h}
        n1, = _rowcall(f"pre_norm{i}", lambda hh, g: _rms(hh, g), [h], [gain("mix_pre_g", i)], [BF16], cols=d)
        sv["n1"] = n1
        if i % 2 == 0:
            proj = _nn(n1, weight("w_in_even", 0, n1), f"in_even{i}")
            a_out = _sb_fwd(proj, n_heads, name=f"sb_fwd{i}")
            b_out, oraw, states = _hg_fwd(proj, rep["hg_lb_logits"], rep["hg_norm_g"], n_heads, name=f"hg_fwd{i}")
            cat = jnp.concatenate([a_out.astype(BF16), b_out], axis=1)
            m = _nn(cat, weight("w_out_even", 0, cat), f"out_even{i}")
            sv.update(proj=proj, oraw=oraw, states=states, cat=cat)
        else:
            proj = _nn(n1, weight("w_in_odd", 0, n1), f"in_odd{i}")
            sm = {k: weight(k, 0, proj) for k in _SMALL}
            xc, ra, ix = _rg_gates_fwd(proj, sm["conv_w"], sm["conv_b"], sm["rg_wa"], sm["rg_ba"],
                                       sm["rg_wx"], sm["rg_bx"], name=f"rg_gates_fwd{i}")
            hs, gact = _rg_scan_fwd(proj, xc, ra, ix, sm["rg_lambda"], name=f"rg_scan_fwd{i}")
            m = _nn(gact, weight("w_out_odd", 0, gact), f"out_odd{i}")
            sv.update(proj=proj, xc=xc, ra=ra, ix=ix, hs=hs, gact=gact, sm=sm)

        def post_mix(hh, mm, g_post, g_pre):
            h1 = hh + _rms(mm, g_post)
            return h1, _rms(h1, g_pre)

        h1, n2 = _rowcall(f"post_mix{i}", post_mix, [h, m], [gain("mix_post_g", i), gain("ffn_pre_g", i)],
                          [F32, BF16], cols=d)
        gu = _nn(n2, weight("w_gate_up", i, n2), f"gate_up{i}", out_dtype=BF16)
        act = _swiglu(gu, name=f"swiglu{i}")
        f = _nn(act, weight("w_down", i, act), f"down{i}")

        def post_ffn(hh, ff_out, g_post):
            h2 = hh + _rms(ff_out, g_post)
            return h2, h2

        h2, h2b = _rowcall(f"post_ffn{i}", post_ffn, [h1, f], [gain("ffn_post_g", i)], [F32, BF16], cols=d)
        e = _nn(p[i], weight("w_ple_up", i, h2b), f"ple_up{i}")
        gl = _nn(h2b, weight("w_ple_gate", i, h2b), f"ple_gate{i}")
        h3, = _rowcall(f"ple{i}", lambda hh, a, b, g: hh + _rms(_sigmoid(a) * b, g), [h2, gl, e],
                       [gain("ple_norm_g", i)], [F32], cols=d)
        sv.update(m=m, h1=h1, n2=n2, gu=gu, act=act, f=f, h2b=h2b, e=e, gl=gl)
        saved.append(sv)
        h = h3

    def loss_fn(y, t):
        err = y - t
        return err * (1.0 / d), jnp.sum(err * err, axis=0, keepdims=True) * (0.5 / d)

    dh, loss_cols = _rowcall("loss", loss_fn, [h, target], [], [F32], red_rows=(1,), cols=d)

    for i in reversed(range(depth)):
        sv = saved[i]

        def ple_bwd(dy, a, b, g):
            _, vjp = jax.vjp(lambda a_, b_, g_: _rms(_sigmoid(a_) * b_, g_), a, b, g)
            return vjp(dy)

        dgl, de, rep_grads["ple_norm_g"][i] = _rowcall(
            f"ple_bwd{i}", ple_bwd, [dh, sv["gl"], sv["e"]], [gain("ple_norm_g", i)], [BF16, BF16],
            red_rows=(1,), cols=d)
        send("w_ple_up", i, _tn(p[i], de, f"d_ple_up{i}"))
        send("w_ple_gate", i, _tn(sv["h2b"], dgl, f"d_ple_gate{i}"))
        dh2_ple = _nt(dgl, weight("w_ple_gate", i, dgl), f"dx_ple_gate{i}")

        def post_ffn_bwd(dy, dx, ff_out, g):
            dh2 = dy + dx
            _, vjp = jax.vjp(_rms, ff_out, g)
            df, dg = vjp(dh2)
            return dh2, df, dg

        dh2, df, rep_grads["ffn_post_g"][i] = _rowcall(
            f"post_ffn_bwd{i}", post_ffn_bwd, [dh, dh2_ple, sv["f"]], [gain("ffn_post_g", i)], [F32, BF16],
            red_rows=(1,), cols=d)
        send("w_down", i, _tn(sv["act"], df, f"d_down{i}"))
        dact = _nt(df, weight("w_down", i, df), f"dx_down{i}", out_dtype=BF16)
        dgu = _swiglu(sv["gu"], dact, name=f"swiglu_bwd{i}")
        send("w_gate_up", i, _tn(sv["n2"], dgu, f"d_gate_up{i}"))
        dn2 = _nt(dgu, weight("w_gate_up", i, dgu), f"dx_gate_up{i}")

        def post_mix_bwd(dy, dn, h1, mm, g_post, g_pre):
            _, vjp_pre = jax.vjp(_rms, h1, g_pre)
            dh1_n, dg_pre = vjp_pre(dn)
            dh1 = dy + dh1_n
            _, vjp_post = jax.vjp(_rms, mm, g_post)
            dm, dg_post = vjp_post(dh1)
            return dh1, dm, dg_pre, dg_post

        dh1, dm, rep_grads["ffn_pre_g"][i], rep_grads["mix_post_g"][i] = _rowcall(
            f"post_mix_bwd{i}", post_mix_bwd, [dh2, dn2, sv["h1"], sv["m"]],
            [gain("mix_post_g", i), gain("ffn_pre_g", i)], [F32, BF16], red_rows=(1, 1), cols=d)

        if i % 2 == 0:
            send("w_out_even", 0, _tn(sv["cat"], dm, f"d_out_even{i}"))
            dcat = _nt(dm, weight("w_out_even", 0, dm), f"dx_out_even{i}")
            dq, dk, dv = _sb_bwd(sv["proj"], dcat, n_heads, name=f"sb_bwd{i}")
            dhq, dhf, dhi, dhg, grads["hg_lb_logits"], grads["hg_norm_g"] = _hg_bwd(
                sv["proj"], rep["hg_lb_logits"], rep["hg_norm_g"], sv["oraw"], sv["states"], dcat, n_heads,
                name=f"hg_bwd{i}")
            dproj = jnp.concatenate([dq.astype(BF16), dk.astype(BF16), dv.astype(BF16), dhq, dhf, dhi, dhg], axis=1)
            send("w_in_even", 0, _tn(sv["n1"], dproj, f"d_in_even{i}"))
            dn1 = _nt(dproj, weight("w_in_even", 0, dproj), f"dx_in_even{i}")
        else:
            sm = sv["sm"]
            send("w_out_odd", 0, _tn(sv["gact"], dm, f"d_out_odd{i}"))
            dgo = _nt(dm, weight("w_out_odd", 0, dm), f"dx_out_odd{i}")
            dgate, dra, dix, dxc1, grads["rg_lambda"] = _rg_scan_bwd(
                dgo, sv["proj"], sv["hs"], sv["xc"], sv["ra"], sv["ix"], sm["rg_lambda"], name=f"rg_scan_bwd{i}")
            (dxb, grads["conv_w"], grads["conv_b"], grads["rg_wa"], grads["rg_ba"], grads["rg_wx"],
             grads["rg_bx"]) = _rg_gates_bwd(dra, dix, dxc1, sv["xc"], sv["proj"], sm["conv_w"], sm["rg_wa"],
                                            sm["rg_wx"], name=f"rg_gates_bwd{i}")
            send("small", 0, {k: grads.pop(k) for k in _SMALL})
            dproj = jnp.concatenate([dgate, dxb], axis=1)
            send("w_in_odd", 0, _tn(sv["n1"], dproj, f"d_in_odd{i}"))
            dn1 = _nt(dproj, weight("w_in_odd", 0, dproj), f"dx_in_odd{i}")

        def pre_norm_bwd(dy, dn, hh, g):
            _, vjp = jax.vjp(_rms, hh, g)
            dx, dg = vjp(dn)
            return dy + dx, dg

        dh, rep_grads["mix_pre_g"][i] = _rowcall(
            f"pre_norm_bwd{i}", pre_norm_bwd, [dh1, dn1, sv["h"]], [gain("mix_pre_g", i)], [F32],
            red_rows=(1,), cols=d)

    for k, rows in rep_grads.items():
        grads[k] = jnp.concatenate(rows, axis=0)
    return loss_cols, dh, grads


_WEIGHTS = ("mix_pre_g", "mix_post_g", "ffn_pre_g", "ffn_post_g", "ple_norm_g", "w_in_even", "w_out_even",
            "hg_lb_logits", "hg_norm_g", "w_in_odd", "conv_w", "conv_b", "rg_wa", "rg_ba", "rg_wx", "rg_bx",
            "rg_lambda", "w_out_odd", "w_gate_up", "w_down", "w_ple_up", "w_ple_gate")
_REPLICATED = ("mix_pre_g", "mix_post_g", "ffn_pre_g", "ffn_post_g", "ple_norm_g", "hg_lb_logits", "hg_norm_g")
_SMALL = ("conv_w", "conv_b", "rg_wa", "rg_ba", "rg_wx", "rg_bx", "rg_lambda")
_BIG = {"w_in_even": True, "w_out_even": False, "w_in_odd": True, "w_out_odd": False,
        "w_gate_up": True, "w_down": False, "w_ple_up": True, "w_ple_gate": False}
_PACK_ROW = SUBLANES * LANES


def _pack(arrays):
    flat = jnp.concatenate([a.reshape(-1) for a in arrays])
    pad = -flat.shape[0] % _PACK_ROW
    return jnp.pad(flat, (0, pad)).reshape(-1, LANES)


def _pack_blocks(arrays):
    flat = jnp.concatenate([a.reshape(N_DEV, -1) for a in arrays], axis=1)
    pad = -flat.shape[1] % _PACK_ROW
    return jnp.pad(flat, ((0, 0), (0, pad))).reshape(N_DEV, -1, LANES)


def _unpack(packed, shapes, lead=()):
    flat = packed.reshape(lead + (-1,))
    out, pos = [], 0
    for shape in shapes:
        n = math.prod(shape)
        out.append(flat[..., pos:pos + n].reshape(lead + tuple(shape)))
        pos += n
    return out


def _to_full_small(name, blocks):
    if name == "conv_w":
        return jnp.transpose(blocks, (1, 0, 2)).reshape(blocks.shape[1], -1)
    if name in ("conv_b", "rg_lambda"):
        return blocks.reshape(1, -1)
    nb = blocks.shape[1]
    if name in ("rg_wa", "rg_wx"):
        return jnp.transpose(blocks, (1, 0, 2, 3)).reshape(nb, RG_BLOCK, RG_BLOCK)
    return jnp.transpose(blocks, (1, 0, 2)).reshape(nb, 1, RG_BLOCK)


def _to_blocks_small(name, full):
    if name == "conv_w":
        return jnp.transpose(full.reshape(full.shape[0], N_DEV, -1), (1, 0, 2))
    if name in ("conv_b", "rg_lambda"):
        return full.reshape(N_DEV, -1)
    nb = full.shape[0]
    if name in ("rg_wa", "rg_wx"):
        return jnp.transpose(full.reshape(nb, N_DEV, RG_BLOCK // N_DEV, RG_BLOCK), (1, 0, 2, 3))
    return jnp.transpose(full.reshape(nb, N_DEV, RG_BLOCK // N_DEV), (1, 0, 2))


def _step(inp):
    w = {k: inp[k] for k in _WEIGHTS}
    x, p, target = inp["x"][0], inp["p"][:, 0], inp["loss_target"][0]
    assert w["hg_lb_logits"].shape[0] == 2 and w["w_in_even"].shape[0] == 1 and w["w_in_odd"].shape[0] == 1

    n_heads = w["w_in_even"].shape[2] * N_DEV // (7 * HEAD_DIM)
    ff = w["w_down"].shape[1] * N_DEV
    small_shapes = [w[k].shape[1:] for k in _SMALL]

    def lands_in_place(name):
        return _BIG[name] and w[name].shape[2] % LANES == 0

    depth = p.shape[0]
    order = [("w_in_even", 0), ("w_out_even", 0)] if depth else []
    for i in range(depth):
        if i == 1:
            order += [("w_in_odd", 0), ("small", 0), ("w_out_odd", 0)]
        order += [("w_gate_up", i), ("w_down", i), ("w_ple_up", i), ("w_ple_gate", i)]
    gathers = {}
    for name, l in order:
        if name == "small":
            gathers[name, l] = _gather_start(_pack([w[k][0] for k in _SMALL]), name="gather_small", relayed=False)
        else:
            gathers[name, l] = _gather_start(w[name][l].astype(BF16), name=f"gather_{name}{l}",
                                             cols=lands_in_place(name))
    ready = {}

    def relay(key, after):
        if "cols" in gathers[key] and key[0] != "small":
            gathers[key] = _gather_relay(gathers[key], after, name=f"gather_{key[0]}{key[1]}")

    def weight(name, layer, after):
        key = ("small", 0) if name in _SMALL else (name, layer)
        if key not in ready:
            relay(key, after)
            for nxt in order[order.index(key) + 1:order.index(key) + 2]:
                relay(nxt, after)
            land, = _spread_wait(gathers[key], after, name=f"gathered_{key[0]}{key[1]}")
            if name in _SMALL:
                ready[key] = {k: _to_full_small(k, b)
                              for k, b in zip(_SMALL, _unpack(land, small_shapes, lead=(N_DEV,)))}
            elif lands_in_place(name):
                ready[key] = land
            elif _BIG[name]:
                ready[key] = jnp.transpose(land, (1, 0, 2)).reshape(land.shape[1], -1)
            else:
                ready[key] = land.reshape(-1, land.shape[2])
        return ready[key][name] if name in _SMALL else ready[key]

    exchanges = []

    def emit(name, layer, g):
        if name == "small":
            handle = _exchange_start([_pack_blocks([_to_blocks_small(k, g[k]) for k in _SMALL])],
                                     name="exchange_small")
            exchanges.append((name, layer, handle))
            return handle["token"]
        _, r, c = w[name].shape
        if lands_in_place(name):
            handle = _exchange_start([g], name=f"exchange_{name}{layer}", cols=True)
        elif _BIG[name]:
            handle = _exchange_start([jnp.transpose(g.reshape(-1, N_DEV, c), (1, 0, 2))],
                                     name=f"exchange_{name}{layer}")
        else:
            handle = _exchange_start([g.reshape(N_DEV, r, c)], name=f"exchange_{name}{layer}")
        exchanges.append((name, layer, handle))
        return handle["token"]

    rep = {k: w[k] for k in _REPLICATED}
    start_token = sum(h["token"] for h in gathers.values())
    loss_cols, dx, grads = _local_step(x, p, target, rep, weight, emit, n_heads, ff, start_token)

    loss_part = jnp.sum(loss_cols).reshape(1)
    rep_gather = _gather_start(_pack([grads[k] for k in _REPLICATED] + [loss_part]), name="gather_rep_grads",
                               relayed=False)

    out = {}
    after = dx
    for name, layer, handle in exchanges:
        land, = _spread_wait(handle, after, name=f"exchanged_{name}{layer}")
        if name == "small":
            res = _adamw(land.reshape(N_DEV, -1, LANES), *[_pack([inp[pre + k][0] for k in _SMALL]) for pre in ("", "m_", "v_")],
                         name="adamw_small")
            for k, *vals in zip(_SMALL, *[_unpack(a, small_shapes) for a in res]):
                out[k] = [v[None] for v in vals]
        else:
            n_l, r, c = w[name].shape
            res = out[name] = _adamw(land.reshape(N_DEV, r, c),
                                     *[inp[pre + name].reshape(n_l * r, c) for pre in ("", "m_", "v_")],
                                     name=f"adamw_{name}{layer}", layer=layer, prev=out.get(name))
        after = res[0]
    for name in _BIG:
        out[name] = [a.reshape(w[name].shape) for a in out[name]]

    rep_shapes = [w[k].shape for k in _REPLICATED] + [(1,)]
    rep_parts, = _spread_wait(rep_gather, after, name="gathered_rep_grads")
    res = _adamw(rep_parts, *[_pack([inp[pre + k] for k in _REPLICATED] + [jnp.zeros((1,), F32)])
                              for pre in ("", "m_", "v_")], name="adamw_rep")
    for k, *vals in zip(_REPLICATED + ("loss",), *[_unpack(a, rep_shapes) for a in res]):
        out[k] = vals
    loss = out["loss"][0][0]

    return (loss, dx[None]) + tuple(out[k][j] for j in range(4) for k in _WEIGHTS)


def kernel(x, p, mix_pre_g, mix_post_g, ffn_pre_g, ffn_post_g, ple_norm_g, w_in_even, w_out_even, hg_lb_logits, hg_norm_g, w_in_odd, conv_w, conv_b, rg_wa, rg_ba, rg_wx, rg_bx, rg_lambda, w_out_odd, w_gate_up, w_down, w_ple_up, w_ple_gate, loss_target, m_mix_pre_g, m_mix_post_g, m_ffn_pre_g, m_ffn_post_g, m_ple_norm_g, m_w_in_even, m_w_out_even, m_hg_lb_logits, m_hg_norm_g, m_w_in_odd, m_conv_w, m_conv_b, m_rg_wa, m_rg_ba, m_rg_wx, m_rg_bx, m_rg_lambda, m_w_out_odd, m_w_gate_up, m_w_down, m_w_ple_up, m_w_ple_gate, v_mix_pre_g, v_mix_post_g, v_ffn_pre_g, v_ffn_post_g, v_ple_norm_g, v_w_in_even, v_w_out_even, v_hg_lb_logits, v_hg_norm_g, v_w_in_odd, v_conv_w, v_conv_b, v_rg_wa, v_rg_ba, v_rg_wx, v_rg_bx, v_rg_lambda, v_w_out_odd, v_w_gate_up, v_w_down, v_w_ple_up, v_w_ple_gate):
    return _step(dict(locals()))
```

```python
import functools
import math

import jax
import jax.numpy as jnp
from jax import lax
from jax.experimental import pallas as pl
from jax.experimental.pallas import tpu as pltpu

F32 = jnp.float32
BF16 = jnp.bfloat16

VMEM_LIMIT_BYTES = 56 * 1024 * 1024
LANES = 128
SUBLANES = 8

N_DEV = 8
HEAD_DIM = 128
SB_Q_TILE = 512
SB_K_TILE = 128
HG_CHUNK = 32
RG_BLOCK = 256
CONV_TAPS = 4
RG_C = 8.0
RMS_EPS = 1e-6

ADAM_LR = 0.001
ADAM_B1 = 0.9
ADAM_B2 = 0.999
ADAM_EPS = 1e-08
ADAM_WD = 0.01
ADAM_STEP = 10

MESH_AXES = ("x", "y", "c")


def _params(*sem):
    return pltpu.CompilerParams(dimension_semantics=sem, vmem_limit_bytes=VMEM_LIMIT_BYTES)


def _pick(n, cands):
    for c in cands:
        if c <= n and n % c == 0:
            return c
    return n


def _mm(a, b, mode, *, name, out_dtype=F32, tm=512, tn=512, tk=None):
    if mode == "nn":
        (m, k), (k2, n) = a.shape, b.shape
    elif mode == "nt":
        (m, k), (n, k2) = a.shape, b.shape
    else:
        (k, m), (k2, n) = a.shape, b.shape
    assert k == k2, (a.shape, b.shape, mode)
    tm, tn = min(tm, m), min(tn, n)
    tk = k if tk is None else min(tk, k)
    assert m % tm == 0 and n % tn == 0 and k % tk == 0, (m, n, k, tm, tn, tk)
    nk = k // tk

    if mode == "tn":
        assert nk == 1
        return _mm_tn(a, b, name=name, out_dtype=out_dtype, tm=tm, tn=tn)

    a_spec = pl.BlockSpec((tm, tk), lambda i, j, kk: (i, kk))
    if mode == "nn":
        b_spec = pl.BlockSpec((tk, tn), lambda i, j, kk: (kk, j))
        dims = (((1,), (0,)), ((), ()))
    else:
        b_spec = pl.BlockSpec((tn, tk), lambda i, j, kk: (j, kk))
        dims = (((1,), (1,)), ((), ()))

    def body(a_ref, b_ref, o_ref, *acc):
        part = lax.dot_general(a_ref[...].astype(BF16), b_ref[...].astype(BF16), dims, preferred_element_type=F32)
        if nk == 1:
            o_ref[...] = part.astype(out_dtype)
        else:
            acc_ref, = acc
            kk = pl.program_id(2)

            @pl.when(kk == 0)
            def _():
                acc_ref[...] = part

            @pl.when(kk > 0)
            def _():
                acc_ref[...] += part

            @pl.when(kk == nk - 1)
            def _():
                o_ref[...] = acc_ref[...].astype(out_dtype)

    return pl.pallas_call(
        body, name=name,
        grid=(m // tm, n // tn, nk),
        in_specs=[a_spec, b_spec],
        out_specs=pl.BlockSpec((tm, tn), lambda i, j, kk: (i, j)),
        out_shape=jax.ShapeDtypeStruct((m, n), out_dtype),
        scratch_shapes=[] if nk == 1 else [pltpu.VMEM((tm, tn), F32)],
        compiler_params=_params("parallel", "parallel", "arbitrary"),
    )(a, b)


def _mm_tn(a, b, *, name, out_dtype, tm, tn):
    k, m = a.shape
    n = b.shape[1]

    def body(a_ref, b_ref, o_ref, at_ref):
        @pl.when(pl.program_id(1) == 0)
        def _():
            at_ref[...] = a_ref[...].astype(F32).T.astype(BF16)

        o_ref[...] = jnp.dot(at_ref[...], b_ref[...].astype(BF16), preferred_element_type=F32).astype(out_dtype)

    return pl.pallas_call(
        body, name=name,
        grid=(m // tm, n // tn),
        in_specs=[pl.BlockSpec((k, tm), lambda i, j: (0, i)), pl.BlockSpec((k, tn), lambda i, j: (0, j))],
        out_specs=pl.BlockSpec((tm, tn), lambda i, j: (i, j)),
        out_shape=jax.ShapeDtypeStruct((m, n), out_dtype),
        scratch_shapes=[pltpu.VMEM((tm, k), BF16)],
        compiler_params=_params("parallel", "arbitrary"),
    )(a, b)


def _rowcall(name, fn, rows, pars, row_outs, red_rows=(), *, cols, ts=256, tc=None, deps=()):
    rows = [r if isinstance(r, tuple) else (r, 0) for r in rows]
    pars = [p if isinstance(p, tuple) else (p, 0) for p in pars]
    s = rows[0][0].shape[0]
    tc = cols if tc is None else tc
    ts = min(ts, s)
    assert s % ts == 0 and cols % tc == 0, (name, s, ts, cols, tc)
    n_in, n_row_out = len(rows) + len(pars), len(row_outs)

    def body(*refs):
        outs = fn(*[r[...] for r in refs[:n_in]])
        outs = outs if isinstance(outs, (tuple, list)) else (outs,)
        o_refs = refs[n_in + len(deps):]
        for o_ref, val in zip(o_refs[:n_row_out], outs[:n_row_out]):
            o_ref[...] = val.astype(o_ref.dtype)
        first = pl.program_id(1) == 0
        for o_ref, val in zip(o_refs[n_row_out:], outs[n_row_out:]):
            @pl.when(first)
            def _(o_ref=o_ref, val=val):
                o_ref[...] = val

            @pl.when(jnp.logical_not(first))
            def _(o_ref=o_ref, val=val):
                o_ref[...] += val

    def row_map(off):
        return lambda j, i: (i, j + off)

    def par_map(off):
        return lambda j, i: (0, j + off)

    return pl.pallas_call(
        body, name=name,
        grid=(cols // tc, s // ts),
        in_specs=[pl.BlockSpec((ts, tc), row_map(off)) for _, off in rows]
        + [pl.BlockSpec((p.shape[0], tc), par_map(off)) for p, off in pars]
        + [pl.BlockSpec(memory_space=pl.ANY)] * len(deps),
        out_specs=[pl.BlockSpec((ts, tc), lambda j, i: (i, j)) for _ in row_outs]
        + [pl.BlockSpec((r, tc), lambda j, i: (0, j)) for r in red_rows],
        out_shape=[jax.ShapeDtypeStruct((s, cols), dt) for dt in row_outs]
        + [jax.ShapeDtypeStruct((r, cols), F32) for r in red_rows],
        compiler_params=_params("parallel", "arbitrary"),
    )(*[r for r, _ in rows], *[p for p, _ in pars], *deps)


def _swiglu(gu, dact=None, *, name, ts=128):
    s, two_f = gu.shape
    f = two_f // 2
    ts = min(ts, s)

    def act(g, u):
        return _silu(g) * u

    def body(gu_ref, *refs):
        g, u = gu_ref[:, 0:f].astype(F32), gu_ref[:, f:two_f].astype(F32)
        if dact is None:
            refs[0][...] = act(g, u).astype(BF16)
        else:
            dact_ref, o_ref = refs
            _, vjp = jax.vjp(act, g, u)
            dg, du = vjp(dact_ref[...].astype(F32))
            o_ref[:, 0:f] = dg.astype(BF16)
            o_ref[:, f:two_f] = du.astype(BF16)

    wide, narrow = pl.BlockSpec((ts, two_f), lambda i: (i, 0)), pl.BlockSpec((ts, f), lambda i: (i, 0))
    return pl.pallas_call(
        body, name=name,
        grid=(s // ts,),
        in_specs=[wide] if dact is None else [wide, narrow],
        out_specs=narrow if dact is None else wide,
        out_shape=jax.ShapeDtypeStruct((s, f if dact is None else two_f), BF16),
        compiler_params=_params("parallel"),
    )(*([gu] if dact is None else [gu, dact]))


def _rms(x, g):
    return x * lax.rsqrt(jnp.mean(x * x, axis=-1, keepdims=True) + RMS_EPS) * g


def _sigmoid(x):
    return jax.nn.sigmoid(x)


def _silu(x):
    return x * jax.nn.sigmoid(x)


def _gelu(x):
    return 0.5 * x * (1.0 + jnp.tanh(math.sqrt(2.0 / math.pi) * (x + 0.044715 * (x * x * x))))


def _softplus(x):
    return jnp.maximum(x, 0.0) + jnp.log1p(jnp.exp(-jnp.abs(x)))


def _split3(x):
    hi = x.astype(BF16)
    r1 = x - hi.astype(F32)
    mid = r1.astype(BF16)
    lo = (r1 - mid.astype(F32)).astype(BF16)
    return hi, mid, lo


def _xdot(x, t):
    return sum(jnp.dot(p, t, preferred_element_type=F32) for p in _split3(x))


def _xdot_l(t, x):
    return sum(jnp.dot(t, p, preferred_element_type=F32) for p in _split3(x))


_NT = (((1,), (1,)), ((), ()))
_TN = (((0,), (0,)), ((), ()))


def _dot(a, b, dims=None):
    if dims is None:
        return jnp.dot(a.astype(BF16), b.astype(BF16), preferred_element_type=F32)
    return lax.dot_general(a.astype(BF16), b.astype(BF16), dims, preferred_element_type=F32)


def _iota(shape, axis):
    return lax.broadcasted_iota(jnp.int32, shape, axis)


def _sb_tile(qb, kblk, mask, upper, c_rem):
    z = lax.dot_general(qb, kblk, _NT, preferred_element_type=F32)
    soft = jnp.log1p(jnp.exp(-jnp.abs(z)))
    lbeta = jnp.minimum(z, 0.0) - soft
    l1m = jnp.where(mask, -jnp.maximum(z, 0.0) - soft, 0.0)
    rem = _xdot(l1m, upper) + c_rem
    w = jnp.where(mask, jnp.exp(lbeta + rem), 0.0)
    return lbeta, l1m, w


def _sb_tiles(s):
    tq = min(SB_Q_TILE, s)
    return tq, SB_K_TILE, tq // SB_K_TILE


def _sb_fwd(proj, n_heads, *, name):
    s = proj.shape[0]
    t, tk, per_q = _sb_tiles(s)
    scale = HEAD_DIM ** -0.5

    def body(q_ref, k_ref, v_ref, o_ref):
        qi = pl.program_id(1)
        qb = (q_ref[...] * scale).astype(BF16)
        row, col = _iota((t, tk), 0) + qi * t, _iota((t, tk), 1)
        upper = (_iota((tk, tk), 0) > _iota((tk, tk), 1)).astype(BF16)
        n_kb = (qi + 1) * per_q

        def step(j, carry):
            acc, c_rem = carry
            kb = n_kb - 1 - j
            rows = pl.ds(pl.multiple_of(kb * tk, tk), tk)
            kblk = k_ref[rows, :].astype(BF16)
            vblk = v_ref[rows, :].astype(BF16)
            _, l1m, w = _sb_tile(qb, kblk, (col + kb * tk) < row, upper, c_rem)
            acc = acc + jnp.dot(w.astype(BF16), vblk, preferred_element_type=F32)
            return acc, c_rem + jnp.sum(l1m, axis=1, keepdims=True)

        acc, _ = lax.fori_loop(0, n_kb, step, (jnp.zeros((t, HEAD_DIM), F32), jnp.zeros((t, 1), F32)))
        o_ref[...] = acc

    return pl.pallas_call(
        body, name=name,
        grid=(n_heads, s // t),
        in_specs=[pl.BlockSpec((t, HEAD_DIM), lambda h, i: (i, h)),
                  pl.BlockSpec((s, HEAD_DIM), lambda h, i: (0, n_heads + h)),
                  pl.BlockSpec((s, HEAD_DIM), lambda h, i: (0, 2 * n_heads + h))],
        out_specs=pl.BlockSpec((t, HEAD_DIM), lambda h, i: (i, h)),
        out_shape=jax.ShapeDtypeStruct((s, n_heads * HEAD_DIM), F32),
        compiler_params=_params("parallel", "arbitrary"),
    )(proj, proj, proj)


def _sb_bwd(proj, dcat, n_heads, *, name):
    s = proj.shape[0]
    t, tk, per_q = _sb_tiles(s)
    scale = HEAD_DIM ** -0.5

    def body(q_ref, k_ref, v_ref, do_ref, dq_ref, dk_ref, dv_ref, g_s, sig_s):
        qi = pl.program_id(1)

        @pl.when(qi == 0)
        def _():
            dk_ref[...] = jnp.zeros_like(dk_ref)
            dv_ref[...] = jnp.zeros_like(dv_ref)

        qb = (q_ref[...] * scale).astype(BF16)
        dob = do_ref[...].astype(BF16)
        row, col = _iota((t, tk), 0) + qi * t, _iota((t, tk), 1)
        upper = (_iota((tk, tk), 0) > _iota((tk, tk), 1)).astype(BF16)
        lower_incl = (_iota((tk, tk), 0) >= _iota((tk, tk), 1)).astype(BF16)
        n_kb = (qi + 1) * per_q

        def weights(j, carry):
            c_rem, g_all = carry
            kb = n_kb - 1 - j
            rows = pl.ds(pl.multiple_of(kb * tk, tk), tk)
            kblk = k_ref[rows, :].astype(BF16)
            vblk = v_ref[rows, :].astype(BF16)
            lbeta, l1m, w = _sb_tile(qb, kblk, (col + kb * tk) < row, upper, c_rem)
            g = w * lax.dot_general(dob, vblk, _NT, preferred_element_type=F32)
            dv_ref[rows, :] += lax.dot_general(w.astype(BF16), dob, _TN, preferred_element_type=F32)
            g_s[kb] = g
            sig_s[kb] = jnp.exp(lbeta)
            return c_rem + jnp.sum(l1m, axis=1, keepdims=True), g_all + jnp.sum(g, axis=1, keepdims=True)

        zero_col = jnp.zeros((t, 1), F32)
        _, g_all = lax.fori_loop(0, n_kb, weights, (zero_col, zero_col))

        def scores(j, carry):
            dq, c_g = carry
            kb = n_kb - 1 - j
            rows = pl.ds(pl.multiple_of(kb * tk, tk), tk)
            g, sig = g_s[kb], sig_s[kb]
            mask = (col + kb * tk) < row
            g_before = g_all - (_xdot(g, lower_incl) + c_g)
            dz = jnp.where(mask, g * (1.0 - sig) - g_before * sig, 0.0).astype(BF16)
            dq = dq + jnp.dot(dz, k_ref[rows, :].astype(BF16), preferred_element_type=F32)
            dk_ref[rows, :] += lax.dot_general(dz, qb, _TN, preferred_element_type=F32)
            return dq, c_g + jnp.sum(g, axis=1, keepdims=True)

        dq, _ = lax.fori_loop(0, n_kb, scores, (jnp.zeros((t, HEAD_DIM), F32), zero_col))
        dq_ref[...] = dq * scale

    width = n_heads * HEAD_DIM
    return pl.pallas_call(
        body, name=name,
        grid=(n_heads, s // t),
        in_specs=[pl.BlockSpec((t, HEAD_DIM), lambda h, i: (i, h)),
                  pl.BlockSpec((s, HEAD_DIM), lambda h, i: (0, n_heads + h)),
                  pl.BlockSpec((s, HEAD_DIM), lambda h, i: (0, 2 * n_heads + h)),
                  pl.BlockSpec((t, HEAD_DIM), lambda h, i: (i, h))],
        out_specs=[pl.BlockSpec((t, HEAD_DIM), lambda h, i: (i, h)),
                   pl.BlockSpec((s, HEAD_DIM), lambda h, i: (0, h)),
                   pl.BlockSpec((s, HEAD_DIM), lambda h, i: (0, h))],
        out_shape=[jax.ShapeDtypeStruct((s, width), F32)] * 3,
        scratch_shapes=[pltpu.VMEM((s // tk, t, tk), F32)] * 2,
        compiler_params=_params("parallel", "arbitrary"),
    )(proj, proj, proj, dcat)


def _hg_pre(hq, hf, logits):
    mx = jnp.max(logits, axis=0, keepdims=True)
    ex = jnp.exp(logits - mx)
    lb = ex[0:1, :] / jnp.sum(ex, axis=0, keepdims=True)
    f = lb + (1.0 - lb) * _sigmoid(hf)
    return _silu(hq), 1.0 - f, jnp.log(f)


def _hg_post(o, norm_g, hgate):
    return _rms(o, norm_g) * _silu(hgate)


def _hg_specs(s, n_heads, first_block):
    def at(group):
        return pl.BlockSpec((s, HEAD_DIM), lambda h: (0, first_block + group * n_heads + h))
    return [at(0), at(1), at(2), at(3)]


def _hg_fwd(proj, logits, norm_g, n_heads, *, name):
    s = proj.shape[0]
    hc = HG_CHUNK
    n_chunks = s // hc
    d = HEAD_DIM

    def body(lg_ref, ng_ref, hq_ref, hf_ref, hi_ref, hgt_ref, out_ref, oraw_ref, st_ref,
             q_s, k_s, lf_s, cum_s, qc_s, oc_s):
        q, k, lf = _hg_pre(hq_ref[...], hf_ref[...], lg_ref[...])
        q_s[...] = q
        k_s[...] = k
        lf_s[...] = lf
        tril = (_iota((hc, hc), 0) >= _iota((hc, hc), 1)).astype(BF16)
        srow = _iota((hc, d), 0)

        def chunk(ci, st):
            rows = pl.ds(pl.multiple_of(ci * hc, hc), hc)
            q, k, v = q_s[rows, :], k_s[rows, :], hi_ref[rows, :]
            cum = _xdot_l(tril, lf_s[rows, :])
            st_ref[0, ci] = st
            o_inter = _dot(q * jnp.exp(cum), st, _NT)
            cum_s[...] = cum
            qc_s[...] = q
            for t in range(hc):
                ng = (t // SUBLANES + 1) * SUBLANES
                e = jnp.where(srow[:ng] <= t, jnp.exp(cum_s[t:t + 1, :] - cum[:ng]), 0.0)
                sc = jnp.sum(qc_s[t:t + 1, :] * k[:ng] * e, axis=1, keepdims=True)
                oc_s[t:t + 1, :] = jnp.sum(sc * v[:ng], axis=0, keepdims=True)
            oraw_ref[rows, :] = o_inter + oc_s[...]
            last = cum_s[hc - 1:hc, :]
            return st * jnp.exp(last) + _dot(v, k * jnp.exp(last - cum), _TN)

        lax.fori_loop(0, n_chunks, chunk, jnp.zeros((d, d), F32))
        out_ref[...] = _hg_post(oraw_ref[...], ng_ref[...], hgt_ref[...]).astype(BF16)

    width = n_heads * d
    head_block = pl.BlockSpec((s, d), lambda h: (0, h))
    return pl.pallas_call(
        body, name=name,
        grid=(n_heads,),
        in_specs=[pl.BlockSpec((2, d), lambda h: (0, h)), pl.BlockSpec((1, d), lambda h: (0, 0))]
        + _hg_specs(s, n_heads, 3 * n_heads),
        out_specs=[head_block, head_block, pl.BlockSpec((1, n_chunks, d, d), lambda h: (h, 0, 0, 0))],
        out_shape=[jax.ShapeDtypeStruct((s, width), BF16), jax.ShapeDtypeStruct((s, width), F32),
                   jax.ShapeDtypeStruct((n_heads, n_chunks, d, d), F32)],
        scratch_shapes=[pltpu.VMEM((s, d), F32)] * 3 + [pltpu.VMEM((hc, d), F32)] * 3,
        compiler_params=_params("arbitrary"),
    )(logits, norm_g, proj, proj, proj, proj)


def _hg_bwd(proj, logits, norm_g, oraw, states, dcat, n_heads, *, name):
    s = proj.shape[0]
    hc = HG_CHUNK
    n_chunks = s // hc
    d = HEAD_DIM

    def body(lg_ref, ng_ref, hq_ref, hf_ref, hi_ref, hgt_ref, oraw_ref, st_ref, dout_ref,
             dhq_ref, dhf_ref, dhi_ref, dhgt_ref, dlg_ref, dng_ref,
             q_s, k_s, lf_s, do_s, dq_s, dk_s, dlf_s, cum_s, qc_s, doc_s, dqc_s, dkc_s, dvc_s):
        head = pl.program_id(0)
        (q, k, lf), pre_vjp = jax.vjp(_hg_pre, hq_ref[...], hf_ref[...], lg_ref[...])
        q_s[...] = q
        k_s[...] = k
        lf_s[...] = lf
        _, post_vjp = jax.vjp(_hg_post, oraw_ref[...], ng_ref[...], hgt_ref[...])
        do, dng, dhgt = post_vjp(dout_ref[...])
        do_s[...] = do
        dhgt_ref[...] = dhgt.astype(BF16)

        @pl.when(head == 0)
        def _():
            dng_ref[...] = dng

        @pl.when(head > 0)
        def _():
            dng_ref[...] += dng

        triu = (_iota((hc, hc), 0) <= _iota((hc, hc), 1)).astype(BF16)
        tril = (_iota((hc, hc), 0) >= _iota((hc, hc), 1)).astype(BF16)
        srow = _iota((hc, d), 0)

        def chunk(j, dst):
            ci = n_chunks - 1 - j
            rows = pl.ds(pl.multiple_of(ci * hc, hc), hc)
            q, k, v, do_c = q_s[rows, :], k_s[rows, :], hi_ref[rows, :], do_s[rows, :]
            cum = _xdot_l(tril, lf_s[rows, :])
            st = st_ref[0, ci]
            cum_s[...] = cum
            qc_s[...] = q
            doc_s[...] = do_c
            last = cum_s[hc - 1:hc, :]
            e_cum, e_last = jnp.exp(cum), jnp.exp(last - cum)
            dqc_s[...] = _dot(do_c, st) * e_cum
            dk_state = _dot(v, dst) * e_last
            dkc_s[...] = dk_state
            dvc_s[...] = _dot(k * e_last, dst, _NT)
            d_last = (jnp.sum(dst * st, axis=0, keepdims=True) * jnp.exp(last)
                      + jnp.sum(k * dk_state, axis=0, keepdims=True))
            for t in range(hc):
                ng = (t // SUBLANES + 1) * SUBLANES
                qt, dot_ = qc_s[t:t + 1, :], doc_s[t:t + 1, :]
                e = jnp.where(srow[:ng] <= t, jnp.exp(cum_s[t:t + 1, :] - cum[:ng]), 0.0)
                ke = k[:ng] * e
                d_a = jnp.sum(dot_ * v[:ng], axis=1, keepdims=True)
                dqc_s[t:t + 1, :] += jnp.sum(d_a * ke, axis=0, keepdims=True)
                dkc_s[0:ng, :] += d_a * (qt * e)
                dvc_s[0:ng, :] += jnp.sum(qt * ke, axis=1, keepdims=True) * dot_
            dq, dk = dqc_s[...], dkc_s[...]
            d_b = q * dq - k * dk
            dq_s[rows, :] = dq
            dk_s[rows, :] = dk
            dhi_ref[rows, :] = dvc_s[...].astype(BF16)
            dlf_s[rows, :] = _xdot_l(triu, d_b) + d_last
            return dst * jnp.exp(last) + _dot(do_c, q * e_cum, _TN)

        lax.fori_loop(0, n_chunks, chunk, jnp.zeros((d, d), F32))
        dhq, dhf, dlg = pre_vjp((dq_s[...], dk_s[...], dlf_s[...]))
        dhq_ref[...] = dhq.astype(BF16)
        dhf_ref[...] = dhf.astype(BF16)
        dlg_ref[...] = dlg

    width = n_heads * d
    head_block = pl.BlockSpec((s, d), lambda h: (0, h))
    return pl.pallas_call(
        body, name=name,
        grid=(n_heads,),
        in_specs=[pl.BlockSpec((2, d), lambda h: (0, h)), pl.BlockSpec((1, d), lambda h: (0, 0))]
        + _hg_specs(s, n_heads, 3 * n_heads)
        + [head_block, pl.BlockSpec((1, n_chunks, d, d), lambda h: (h, 0, 0, 0)),
           pl.BlockSpec((s, d), lambda h: (0, n_heads + h))],
        out_specs=[head_block] * 4 + [pl.BlockSpec((2, d), lambda h: (0, h)), pl.BlockSpec((1, d), lambda h: (0, 0))],
        out_shape=[jax.ShapeDtypeStruct((s, width), BF16)] * 4
        + [jax.ShapeDtypeStruct((2, width), F32), jax.ShapeDtypeStruct((1, d), F32)],
        scratch_shapes=[pltpu.VMEM((s, d), F32)] * 7 + [pltpu.VMEM((hc, d), F32)] * 6,
        compiler_params=_params("arbitrary"),
    )(logits, norm_g, proj, proj, proj, proj, oraw, states, dcat)


def _shift_down(x, n, srow):
    if n == 0:
        return x
    return jnp.where(srow >= n, pltpu.roll(x, n, 0), 0.0)


def _shift_up(x, n, srow):
    if n == 0:
        return x
    s = x.shape[0]
    return jnp.where(srow < s - n, pltpu.roll(x, s - n, 0), 0.0)


def _rg_gates_fwd(proj, conv_w, conv_b, wa, ba, wx, bx, *, name):
    s = proj.shape[0]
    nb = wa.shape[0]
    bw = RG_BLOCK

    def body(xb_ref, cw_ref, cb_ref, wa_ref, ba_ref, wx_ref, bx_ref, xc_ref, ra_ref, ix_ref):
        x = xb_ref[...]
        srow = _iota((s, bw), 0)
        cw = cw_ref[...]
        xc = cb_ref[...] + cw[0:1, :] * x
        for tap in range(1, CONV_TAPS):
            xc = xc + cw[tap:tap + 1, :] * _shift_down(x, tap, srow)
        xc_ref[...] = xc
        ra_ref[...] = _dot(xc, wa_ref[0]) + ba_ref[0]
        ix_ref[...] = _dot(xc, wx_ref[0]) + bx_ref[0]

    col = pl.BlockSpec((s, bw), lambda n: (0, n))
    vec = lambda r: pl.BlockSpec((r, bw), lambda n: (0, n))
    mat = pl.BlockSpec((1, bw, bw), lambda n: (n, 0, 0))
    bias = pl.BlockSpec((1, 1, bw), lambda n: (n, 0, 0))
    return pl.pallas_call(
        body, name=name,
        grid=(nb,),
        in_specs=[pl.BlockSpec((s, bw), lambda n: (0, nb + n)), vec(CONV_TAPS), vec(1), mat, bias, mat, bias],
        out_specs=[col] * 3,
        out_shape=[jax.ShapeDtypeStruct((s, nb * bw), F32)] * 3,
        compiler_params=_params("parallel"),
    )(proj, conv_w, conv_b, wa, ba, wx, bx)


def _rg_au(ra, ix, xc, lam, first_row):
    log_a = -RG_C * _sigmoid(ra) * _softplus(-lam)
    th = jnp.tanh(log_a)
    one_minus_a2 = -2.0 * th / (1.0 - th)
    mult = jnp.where(first_row, 1.0, jnp.sqrt(one_minus_a2))
    return jnp.exp(log_a), xc * _sigmoid(ix) * mult


def _rg_out(gate, hs):
    return _gelu(gate) * hs


def _linear_scan(a, b, a_s, b_s, in_s, reverse):
    s, c = a.shape
    within = _iota((s, c), 0) & (SUBLANES - 1)
    shift = 1
    while shift < SUBLANES:
        if reverse:
            take = within < SUBLANES - shift
            a_n, b_n = pltpu.roll(a, s - shift, 0), pltpu.roll(b, s - shift, 0)
        else:
            take = within >= shift
            a_n, b_n = pltpu.roll(a, shift, 0), pltpu.roll(b, shift, 0)
        b = jnp.where(take, a * b_n + b, b)
        a = jnp.where(take, a * a_n, a)
        shift *= 2
    a_s[...] = a
    b_s[...] = b
    n_tiles = s // SUBLANES
    edge = 0 if reverse else SUBLANES - 1

    def tile(i, h):
        rows = pl.ds(pl.multiple_of(((n_tiles - 1 - i) if reverse else i) * SUBLANES, SUBLANES), SUBLANES)
        in_s[rows, :] = jnp.broadcast_to(h, (SUBLANES, c))
        return a_s[rows, :][edge:edge + 1, :] * h + b_s[rows, :][edge:edge + 1, :]

    lax.fori_loop(0, n_tiles, tile, jnp.zeros((1, c), F32))
    return a * in_s[...] + b


def _rg_scan_fwd(proj, xc, ra, ix, lam, *, name):
    s, width = xc.shape
    tc = LANES

    def body(gate_ref, xc_ref, ra_ref, ix_ref, lam_ref, hs_ref, gact_ref, a_s, u_s, in_s):
        first_row = _iota((s, tc), 0) == 0
        a, u = _rg_au(ra_ref[...], ix_ref[...], xc_ref[...], lam_ref[...], first_row)
        hs = _linear_scan(a, u, a_s, u_s, in_s, reverse=False)
        hs_ref[...] = hs
        gact_ref[...] = _rg_out(gate_ref[...], hs).astype(BF16)

    col = pl.BlockSpec((s, tc), lambda n: (0, n))
    return pl.pallas_call(
        body, name=name,
        grid=(width // tc,),
        in_specs=[col, col, col, col, pl.BlockSpec((1, tc), lambda n: (0, n))],
        out_specs=[col, col],
        out_shape=[jax.ShapeDtypeStruct((s, width), F32), jax.ShapeDtypeStruct((s, width), BF16)],
        scratch_shapes=[pltpu.VMEM((s, tc), F32)] * 3,
        compiler_params=_params("parallel"),
    )(proj, xc, ra, ix, lam)


def _rg_scan_bwd(dgo, proj, hs, xc, ra, ix, lam, *, name):
    s, width = xc.shape
    tc = LANES

    def body(dgo_ref, gate_ref, hs_ref, xc_ref, ra_ref, ix_ref, lam_ref,
             dgate_ref, dra_ref, dix_ref, dxc_ref, dlam_ref, a_s, dh_s, g_s):
        srow = _iota((s, tc), 0)
        hs = hs_ref[...]
        _, out_vjp = jax.vjp(_rg_out, gate_ref[...], hs)
        dgate, dh = out_vjp(dgo_ref[...])
        dgate_ref[...] = dgate.astype(BF16)
        au = functools.partial(_rg_au, first_row=srow == 0)
        (a, _), au_vjp = jax.vjp(au, ra_ref[...], ix_ref[...], xc_ref[...], lam_ref[...])
        g = _linear_scan(_shift_up(a, 1, srow), dh, a_s, dh_s, g_s, reverse=True)
        dra, dix, dxc, dlam = au_vjp((g * _shift_down(hs, 1, srow), g))
        dra_ref[...] = dra.astype(BF16)
        dix_ref[...] = dix.astype(BF16)
        dxc_ref[...] = dxc
        dlam_ref[...] = dlam

    col = pl.BlockSpec((s, tc), lambda n: (0, n))
    vec = pl.BlockSpec((1, tc), lambda n: (0, n))
    return pl.pallas_call(
        body, name=name,
        grid=(width // tc,),
        in_specs=[col] * 6 + [vec],
        out_specs=[col] * 4 + [vec],
        out_shape=[jax.ShapeDtypeStruct((s, width), BF16)] * 3
        + [jax.ShapeDtypeStruct((s, width), F32), jax.ShapeDtypeStruct((1, width), F32)],
        scratch_shapes=[pltpu.VMEM((s, tc), F32)] * 3,
        compiler_params=_params("parallel"),
    )(dgo, proj, hs, xc, ra, ix, lam)


def _rg_gates_bwd(dra, dix, dxc1, xc, proj, conv_w, wa, wx, *, name):
    s = proj.shape[0]
    nb = wa.shape[0]
    bw = RG_BLOCK

    def body(dra_ref, dix_ref, dxc_ref, xc_ref, xb_ref, cw_ref, wa_ref, wx_ref,
             dxb_ref, dcw_ref, dcb_ref, dwa_ref, dba_ref, dwx_ref, dbx_ref):
        dra, dix = dra_ref[...], dix_ref[...]
        xc_t = xc_ref[...].T.astype(BF16)
        dwa_ref[0] = jnp.dot(xc_t, dra, preferred_element_type=F32)
        dwx_ref[0] = jnp.dot(xc_t, dix, preferred_element_type=F32)
        dba_ref[0] = jnp.sum(dra.astype(F32), axis=0, keepdims=True)
        dbx_ref[0] = jnp.sum(dix.astype(F32), axis=0, keepdims=True)
        dxc = dxc_ref[...] + _dot(dra, wa_ref[0], _NT) + _dot(dix, wx_ref[0], _NT)
        srow = _iota((s, bw), 0)
        x = xb_ref[...]
        cw = cw_ref[...]
        dx = cw[0:1, :] * dxc
        dcw = [jnp.sum(dxc * x, axis=0, keepdims=True)]
        for tap in range(1, CONV_TAPS):
            dx = dx + cw[tap:tap + 1, :] * _shift_up(dxc, tap, srow)
            dcw.append(jnp.sum(dxc * _shift_down(x, tap, srow), axis=0, keepdims=True))
        dxb_ref[...] = dx.astype(BF16)
        r4 = _iota((CONV_TAPS, bw), 0)
        acc = jnp.zeros((CONV_TAPS, bw), F32)
        for tap in range(CONV_TAPS):
            acc = jnp.where(r4 == tap, dcw[tap], acc)
        dcw_ref[...] = acc
        dcb_ref[...] = jnp.sum(dxc, axis=0, keepdims=True)

    col = pl.BlockSpec((s, bw), lambda n: (0, n))
    vec = lambda r: pl.BlockSpec((r, bw), lambda n: (0, n))
    mat = pl.BlockSpec((1, bw, bw), lambda n: (n, 0, 0))
    bias = pl.BlockSpec((1, 1, bw), lambda n: (n, 0, 0))
    width = nb * bw
    return pl.pallas_call(
        body, name=name,
        grid=(nb,),
        in_specs=[col, col, col, col, pl.BlockSpec((s, bw), lambda n: (0, nb + n)), vec(CONV_TAPS), mat, mat],
        out_specs=[col, vec(CONV_TAPS), vec(1), mat, bias, mat, bias],
        out_shape=[jax.ShapeDtypeStruct((s, width), BF16), jax.ShapeDtypeStruct((CONV_TAPS, width), F32),
                   jax.ShapeDtypeStruct((1, width), F32), jax.ShapeDtypeStruct((nb, bw, bw), F32),
                   jax.ShapeDtypeStruct((nb, 1, bw), F32), jax.ShapeDtypeStruct((nb, bw, bw), F32),
                   jax.ShapeDtypeStruct((nb, 1, bw), F32)],
        compiler_params=_params("parallel"),
    )(dra, dix, dxc1, xc, proj, conv_w, wa, wx)


_HBM = pl.BlockSpec(memory_space=pltpu.HBM)
_FLIPS = ((0, 0, 1), (1, 0, 0), (0, 1, 0), (1, 1, 0))
_ALL_FLIPS = tuple((a, b, c) for a in (0, 1) for b in (0, 1) for c in (0, 1))[1:]


def _flip(pos, f):
    return tuple(1 - p if b else p for p, b in zip(pos, f))


def _dev_index(pos):
    return 4 * pos[0] + 2 * pos[1] + pos[2]


def _block(ref, idx, cols):
    if not cols:
        return ref.at[idx]
    n = ref.shape[-1] // N_DEV
    start = pl.multiple_of(idx * n, LANES)
    return ref.at[(slice(None),) * (len(ref.shape) - 1) + (pl.ds(start, n),)]


def _all_gather(xs, *, name, cols=False):
    n_arr = len(xs)

    def body(*refs):
        x_refs, out_refs = refs[:n_arr], refs[n_arr:2 * n_arr]
        send_sems, recv_sems, local_sems = refs[2 * n_arr:]
        me = (lax.axis_index("x"), lax.axis_index("y"), lax.axis_index("c"))
        sibling = _flip(me, _FLIPS[0])
        chips = [_flip(me, f) for f in _FLIPS[1:]]

        def copy(a, k, block, to, src=None):
            dst = _block(out_refs[a], _dev_index(block), cols)
            return pltpu.make_async_remote_copy(
                src_ref=dst if src is None else src, dst_ref=dst,
                send_sem=send_sems.at[7 * a + k], recv_sem=recv_sems.at[7 * a + k],
                device_id=to, device_id_type=pl.DeviceIdType.MESH)

        mine = [pltpu.make_async_copy(x_refs[a], _block(out_refs[a], _dev_index(me), cols), local_sems.at[a])
                for a in range(n_arr)]
        for cp in mine:
            cp.start()
        first = []
        for a in range(n_arr):
            first.append(copy(a, 0, me, sibling, src=x_refs[a]))
            first += [copy(a, 1 + j, me, chip, src=x_refs[a]) for j, chip in enumerate(chips)]
        for cp in first:
            cp.start()
        passed = []
        for j, chip in enumerate(chips):
            for a in range(n_arr):
                copy(a, 1 + j, chip, me).wait_recv()
                fwd = copy(a, 4 + j, chip, sibling)
                fwd.start()
                passed.append(fwd)
        for a in range(n_arr):
            copy(a, 0, sibling, me).wait_recv()
            for j, chip in enumerate(chips):
                copy(a, 4 + j, _flip(chip, _FLIPS[0]), me).wait_recv()
        for cp in first + passed:
            cp.wait_send()
        for cp in mine:
            cp.wait()

    def out_shape(x):
        shape = x.shape[:-1] + (N_DEV * x.shape[-1],) if cols else (N_DEV,) + x.shape
        return jax.ShapeDtypeStruct(shape, x.dtype)

    return pl.pallas_call(
        body, name=name,
        in_specs=[_HBM] * n_arr, out_specs=[_HBM] * n_arr,
        out_shape=[out_shape(x) for x in xs],
        scratch_shapes=[pltpu.SemaphoreType.DMA((7 * n_arr,)), pltpu.SemaphoreType.DMA((7 * n_arr,)),
                        pltpu.SemaphoreType.DMA((n_arr,))],
    )(*xs)


def _exchange(ps, *, name, cols=False):
    n_arr = len(ps)
    blk = ps[0].shape[:-1] + (ps[0].shape[-1] // N_DEV,) if cols else ps[0].shape[1:]

    def body(*refs):
        p_refs, out_ref = refs[:n_arr], refs[n_arr]
        send_sems, recv_sems, local_sems = refs[n_arr + 1:]
        me = (lax.axis_index("x"), lax.axis_index("y"), lax.axis_index("c"))
        me_idx = _dev_index(me)
        own = [pltpu.make_async_copy(_block(p_refs[a], me_idx, cols), out_ref.at[me_idx, a], local_sems.at[a])
               for a in range(n_arr)]
        for cp in own:
            cp.start()
        sends = []
        for k, f in enumerate(_ALL_FLIPS):
            peer = _flip(me, f)
            for a in range(n_arr):
                cp = pltpu.make_async_remote_copy(
                    src_ref=_block(p_refs[a], _dev_index(peer), cols), dst_ref=out_ref.at[me_idx, a],
                    send_sem=send_sems.at[7 * a + k], recv_sem=recv_sems.at[7 * a + k],
                    device_id=peer, device_id_type=pl.DeviceIdType.MESH)
                cp.start()
                sends.append(cp)
        for cp in sends:
            cp.wait()
        for cp in own:
            cp.wait()

    return pl.pallas_call(
        body, name=name,
        in_specs=[_HBM] * n_arr, out_specs=_HBM,
        out_shape=jax.ShapeDtypeStruct((N_DEV, n_arr) + blk, ps[0].dtype),
        scratch_shapes=[pltpu.SemaphoreType.DMA((7 * n_arr,)), pltpu.SemaphoreType.DMA((7 * n_arr,)),
                        pltpu.SemaphoreType.DMA((n_arr,))],
    )(*ps)


_SEM = pl.BlockSpec(memory_space=pltpu.SEMAPHORE)
_ANY = pl.BlockSpec(memory_space=pl.ANY)
_N_PEERS = N_DEV - 1


def _hbm(x):
    return pltpu.with_memory_space_constraint(x, pltpu.HBM)


def _me():
    return lax.axis_index("x"), lax.axis_index("y"), lax.axis_index("c")


def _spread_copies(plan, src_refs, land_refs, send_sems, recv_sems, local_sems):
    local, remote = plan(src_refs, land_refs)
    local = [pltpu.make_async_copy(src, dst, local_sems.at[i]) for i, (src, dst) in enumerate(local)]
    remote = [pltpu.make_async_remote_copy(src_ref=src, dst_ref=dst, send_sem=send_sems.at[k], recv_sem=recv_sems.at[k],
                                           device_id=peer, device_id_type=pl.DeviceIdType.MESH)
              for k, (src, dst, peer) in enumerate(remote)]
    return local, remote


def _spread_start(srcs, lands, plan, n_remote, n_local, *, name):
    ns, nl = len(srcs), len(lands)

    def body(*refs):
        src_refs, land_refs = refs[:ns], refs[ns:ns + nl]
        send_sems, recv_sems, local_sems = refs[ns + nl:ns + nl + 3]
        local, remote = _spread_copies(plan, src_refs, land_refs, send_sems, recv_sems, local_sems)
        for cp in local + remote:
            cp.start()
        token = refs[-1]
        token[...] = jnp.zeros_like(token)

    lands = [_hbm(lax.empty(*x)) if isinstance(x, tuple) else x for x in lands]
    out = pl.pallas_call(
        body, name=name,
        in_specs=[_HBM] * (ns + nl),
        out_specs=[_SEM] * 3 + [_HBM] * (ns + nl) + [pl.BlockSpec(memory_space=pltpu.VMEM)],
        out_shape=[pltpu.SemaphoreType.DMA((n_remote,)), pltpu.SemaphoreType.DMA((n_remote,)),
                   pltpu.SemaphoreType.DMA((max(n_local, 1),))]
        + [pltpu.HBM(x.shape, x.dtype) for x in list(srcs) + lands]
        + [jax.ShapeDtypeStruct((SUBLANES, LANES), F32)],
        input_output_aliases={i: 3 + i for i in range(ns + nl)},
        compiler_params=pltpu.CompilerParams(has_side_effects=pltpu.SideEffectType.DATAFLOW_SIDE_EFFECTING),
    )(*[_hbm(x) for x in srcs], *lands)
    return dict(sems=list(out[:3]), srcs=list(out[3:3 + ns]), lands=list(out[3 + ns:3 + ns + nl]),
                token=out[-1], plan=plan)


def _spread_wait(handle, after, *, name):
    ns, nl = len(handle["srcs"]), len(handle["lands"])

    def body(*refs):
        src_refs, land_refs = refs[:ns], refs[ns:ns + nl]
        send_sems, recv_sems, local_sems = refs[ns + nl:ns + nl + 3]
        local, remote = _spread_copies(handle["plan"], src_refs, land_refs, send_sems, recv_sems, local_sems)
        for cp in local:
            cp.wait()
        for cp in remote:
            cp.wait_send()
            cp.wait_recv()

    out = pl.pallas_call(
        body, name=name,
        in_specs=[_HBM] * (ns + nl) + [_SEM] * 3 + [_ANY],
        out_specs=[_HBM] * (ns + nl),
        out_shape=[pltpu.HBM(x.shape, x.dtype) for x in handle["srcs"] + handle["lands"]],
        input_output_aliases={i: i for i in range(ns + nl)},
        compiler_params=pltpu.CompilerParams(has_side_effects=pltpu.SideEffectType.DATAFLOW_SIDE_EFFECTING),
    )(*handle["srcs"], *handle["lands"], *handle["sems"], after)
    return list(out[ns:])


def _gather_start(x, *, name, cols=False, relayed=True):
    shape = x.shape[:-1] + (N_DEV * x.shape[-1],) if cols else (N_DEV,) + x.shape
    flips = _FLIPS if relayed else _ALL_FLIPS

    def plan(src_refs, land_refs):
        me = _me()
        mine = _block(land_refs[0], _dev_index(me), cols)
        return [(src_refs[0], mine)], [(src_refs[0], mine, _flip(me, f)) for f in flips]

    handle = _spread_start([x], [(shape, x.dtype)], plan, len(flips), 1, name=name)
    handle["cols"] = cols
    return handle


def _gather_relay(handle, after, *, name):
    cols = handle["cols"]
    land, = _spread_wait(handle, after, name=f"{name}_arrived")

    def plan(src_refs, land_refs):
        me = _me()
        blocks = [_block(land_refs[0], _dev_index(_flip(me, f)), cols) for f in _FLIPS[1:]]
        return [], [(blk, blk, _flip(me, _FLIPS[0])) for blk in blocks]

    return _spread_start([], [land], plan, len(_FLIPS) - 1, 0, name=f"{name}_pass")


def _exchange_start(ps, *, name, cols=False):
    blk = ps[0].shape[:-1] + (ps[0].shape[-1] // N_DEV,) if cols else ps[0].shape[1:]

    def plan(src_refs, land_refs):
        me = _me()
        me_idx = _dev_index(me)
        local = [(_block(src, me_idx, cols), land_refs[0].at[me_idx, a]) for a, src in enumerate(src_refs)]
        remote = [(_block(src, _dev_index(_flip(me, f)), cols), land_refs[0].at[me_idx, a], _flip(me, f))
                  for f in _ALL_FLIPS for a, src in enumerate(src_refs)]
        return local, remote

    return _spread_start(ps, [((N_DEV, len(ps)) + blk, ps[0].dtype)], plan, _N_PEERS * len(ps), len(ps), name=name)


def _adamw(parts, w, m, v, *, name, layer=0, prev=None):
    n_rows, c = w.shape
    r = parts.shape[1]
    row_bytes = c * (N_DEV * parts.dtype.itemsize + 7 * 4) * 2
    tr = r
    for cand in (512, 256, 128, 64, 32, 16):
        if r % cand == 0 and cand * row_bytes <= 40 * 1024 * 1024:
            tr = cand
            break
    c1 = 1.0 - ADAM_B1 ** ADAM_STEP
    c2 = 1.0 - ADAM_B2 ** ADAM_STEP

    def body(p_ref, w_ref, m_ref, v_ref, *rest):
        g_ref, d_ref, nm_ref, nv_ref = rest[-4:]
        g = p_ref[0].astype(F32)
        for j in range(1, N_DEV):
            g = g + p_ref[j].astype(F32)
        nm = ADAM_B1 * m_ref[...] + (1.0 - ADAM_B1) * g
        nv = ADAM_B2 * v_ref[...] + (1.0 - ADAM_B2) * (g * g)
        g_ref[...] = g
        nm_ref[...] = nm
        nv_ref[...] = nv
        d_ref[...] = -ADAM_LR * ((nm * (1.0 / c1)) / (jnp.sqrt(nv * (1.0 / c2)) + ADAM_EPS) + ADAM_WD * w_ref[...])

    off = layer * (r // tr)
    blk = pl.BlockSpec((tr, c), lambda i: (i + off, 0))
    prev = list(prev) if prev is not None else []
    return pl.pallas_call(
        body, name=name,
        grid=(r // tr,),
        in_specs=[pl.BlockSpec((N_DEV, tr, c), lambda i: (0, i, 0)), blk, blk, blk] + [_ANY] * len(prev),
        out_specs=[blk] * 4,
        out_shape=[jax.ShapeDtypeStruct((n_rows, c), F32)] * 4,
        input_output_aliases={4 + j: j for j in range(len(prev))},
        compiler_params=_params("parallel"),
    )(parts, w, m, v, *prev)


_TN_CANDS = (512, 256, 128)
_TK_MAX = 5632
_TK_WHOLE_ROWS = 2816


def _contraction_tiles(m, k):
    tk = k
    while tk > _TK_MAX and tk % 2 == 0 and (tk // 2) % LANES == 0:
        tk //= 2
    tm = m if tk <= _TK_WHOLE_ROWS or m % 2 else m // 2
    return tm, tk


def _nn(a, b, name, out_dtype=F32):
    tm, tk = _contraction_tiles(*a.shape)
    return _mm(a, b, "nn", name=name, out_dtype=out_dtype, tm=tm, tn=_pick(b.shape[1], _TN_CANDS), tk=tk)


def _nt(a, b, name, out_dtype=F32):
    tm, tk = _contraction_tiles(*a.shape)
    return _mm(a, b, "nt", name=name, out_dtype=out_dtype, tm=tm, tn=_pick(b.shape[0], _TN_CANDS), tk=tk)


def _tn(a, b, name, out_dtype=BF16):
    return _mm(a, b, "tn", name=name, out_dtype=out_dtype, tm=_pick(a.shape[1], _TN_CANDS),
               tn=_pick(b.shape[1], (1024,) + _TN_CANDS))


def _local_step(x, p, target, rep, weight, emit, n_heads, start_tokens=()):
    s, d = x.shape
    depth = p.shape[0]
    grads = {}
    rep_grads = {k: [None] * depth for k in ("mix_pre_g", "mix_post_g", "ffn_pre_g", "ffn_post_g", "ple_norm_g")}

    pending = list(start_tokens)
    gains = {}

    def gain(name, i):
        if (name, i) not in gains:
            gains[name, i] = rep[name][i:i + 1]
        return gains[name, i]

    def send(name, layer, g):
        token = emit(name, layer, g)
        if token is not None:
            pending.append(token)

    def rowcall(*args, **kwargs):
        deps, pending[:] = tuple(pending), []
        return _rowcall(*args, deps=deps, **kwargs)

    saved = []
    h = x
    for i in range(depth):
        sv = {"h": h}
        n1, = rowcall(f"pre_norm{i}", lambda hh, g: _rms(hh, g), [h], [gain("mix_pre_g", i)], [BF16], cols=d)
        sv["n1"] = n1
        if i % 2 == 0:
            proj = _nn(n1, weight("w_in_even", 0, n1), f"in_even{i}")
            a_out = _sb_fwd(proj, n_heads, name=f"sb_fwd{i}")
            b_out, oraw, states = _hg_fwd(proj, rep["hg_lb_logits"], rep["hg_norm_g"], n_heads, name=f"hg_fwd{i}")
            cat = jnp.concatenate([a_out.astype(BF16), b_out], axis=1)
            m = _nn(cat, weight("w_out_even", 0, cat), f"out_even{i}")
            sv.update(proj=proj, oraw=oraw, states=states, cat=cat)
        else:
            proj = _nn(n1, weight("w_in_odd", 0, n1), f"in_odd{i}")
            sm = {k: weight(k, 0, proj) for k in _SMALL}
            xc, ra, ix = _rg_gates_fwd(proj, sm["conv_w"], sm["conv_b"], sm["rg_wa"], sm["rg_ba"],
                                       sm["rg_wx"], sm["rg_bx"], name=f"rg_gates_fwd{i}")
            hs, gact = _rg_scan_fwd(proj, xc, ra, ix, sm["rg_lambda"], name=f"rg_scan_fwd{i}")
            m = _nn(gact, weight("w_out_odd", 0, gact), f"out_odd{i}")
            sv.update(proj=proj, xc=xc, ra=ra, ix=ix, hs=hs, gact=gact, sm=sm)

        def post_mix(hh, mm, g_post, g_pre):
            h1 = hh + _rms(mm, g_post)
            return h1, _rms(h1, g_pre)

        h1, n2 = rowcall(f"post_mix{i}", post_mix, [h, m], [gain("mix_post_g", i), gain("ffn_pre_g", i)],
                          [F32, BF16], cols=d)
        gu = _nn(n2, weight("w_gate_up", i, n2), f"gate_up{i}", out_dtype=BF16)
        act = _swiglu(gu, name=f"swiglu{i}")
        f = _nn(act, weight("w_down", i, act), f"down{i}")

        def post_ffn(hh, ff_out, g_post):
            h2 = hh + _rms(ff_out, g_post)
            return h2, h2

        h2, h2b = rowcall(f"post_ffn{i}", post_ffn, [h1, f], [gain("ffn_post_g", i)], [F32, BF16], cols=d)
        e = _nn(p[i], weight("w_ple_up", i, h2b), f"ple_up{i}")
        gl = _nn(h2b, weight("w_ple_gate", i, h2b), f"ple_gate{i}")
        h3, = rowcall(f"ple{i}", lambda hh, a, b, g: hh + _rms(_sigmoid(a) * b, g), [h2, gl, e],
                       [gain("ple_norm_g", i)], [F32], cols=d)
        sv.update(m=m, h1=h1, n2=n2, gu=gu, act=act, f=f, h2b=h2b, e=e, gl=gl)
        saved.append(sv)
        h = h3

    def loss_fn(y, t):
        err = y - t
        return err * (1.0 / d), jnp.sum(err * err, axis=0, keepdims=True) * (0.5 / d)

    dh, loss_cols = rowcall("loss", loss_fn, [h, target], [], [F32], red_rows=(1,), cols=d)

    for i in reversed(range(depth)):
        sv = saved[i]

        def ple_bwd(dy, a, b, g):
            _, vjp = jax.vjp(lambda a_, b_, g_: _rms(_sigmoid(a_) * b_, g_), a, b, g)
            return vjp(dy)

        dgl, de, rep_grads["ple_norm_g"][i] = rowcall(
            f"ple_bwd{i}", ple_bwd, [dh, sv["gl"], sv["e"]], [gain("ple_norm_g", i)], [BF16, BF16],
            red_rows=(1,), cols=d)
        send("w_ple_up", i, _tn(p[i], de, f"d_ple_up{i}"))
        send("w_ple_gate", i, _tn(sv["h2b"], dgl, f"d_ple_gate{i}"))
        dh2_ple = _nt(dgl, weight("w_ple_gate", i, dgl), f"dx_ple_gate{i}")

        def post_ffn_bwd(dy, dx, ff_out, g):
            dh2 = dy + dx
            _, vjp = jax.vjp(_rms, ff_out, g)
            df, dg = vjp(dh2)
            return dh2, df, dg

        dh2, df, rep_grads["ffn_post_g"][i] = rowcall(
            f"post_ffn_bwd{i}", post_ffn_bwd, [dh, dh2_ple, sv["f"]], [gain("ffn_post_g", i)], [F32, BF16],
            red_rows=(1,), cols=d)
        send("w_down", i, _tn(sv["act"], df, f"d_down{i}"))
        dact = _nt(df, weight("w_down", i, df), f"dx_down{i}", out_dtype=BF16)
        dgu = _swiglu(sv["gu"], dact, name=f"swiglu_bwd{i}")
        send("w_gate_up", i, _tn(sv["n2"], dgu, f"d_gate_up{i}"))
        dn2 = _nt(dgu, weight("w_gate_up", i, dgu), f"dx_gate_up{i}")

        def post_mix_bwd(dy, dn, h1, mm, g_post, g_pre):
            _, vjp_pre = jax.vjp(_rms, h1, g_pre)
            dh1_n, dg_pre = vjp_pre(dn)
            dh1 = dy + dh1_n
            _, vjp_post = jax.vjp(_rms, mm, g_post)
            dm, dg_post = vjp_post(dh1)
            return dh1, dm, dg_pre, dg_post

        dh1, dm, rep_grads["ffn_pre_g"][i], rep_grads["mix_post_g"][i] = rowcall(
            f"post_mix_bwd{i}", post_mix_bwd, [dh2, dn2, sv["h1"], sv["m"]],
            [gain("mix_post_g", i), gain("ffn_pre_g", i)], [F32, BF16], red_rows=(1, 1), cols=d)

        if i % 2 == 0:
            send("w_out_even", 0, _tn(sv["cat"], dm, f"d_out_even{i}"))
            dcat = _nt(dm, weight("w_out_even", 0, dm), f"dx_out_even{i}")
            dq, dk, dv = _sb_bwd(sv["proj"], dcat, n_heads, name=f"sb_bwd{i}")
            dhq, dhf, dhi, dhg, grads["hg_lb_logits"], grads["hg_norm_g"] = _hg_bwd(
                sv["proj"], rep["hg_lb_logits"], rep["hg_norm_g"], sv["oraw"], sv["states"], dcat, n_heads,
                name=f"hg_bwd{i}")
            dproj = jnp.concatenate([dq.astype(BF16), dk.astype(BF16), dv.astype(BF16), dhq, dhf, dhi, dhg], axis=1)
            send("w_in_even", 0, _tn(sv["n1"], dproj, f"d_in_even{i}"))
            dn1 = _nt(dproj, weight("w_in_even", 0, dproj), f"dx_in_even{i}")
        else:
            sm = sv["sm"]
            send("w_out_odd", 0, _tn(sv["gact"], dm, f"d_out_odd{i}"))
            dgo = _nt(dm, weight("w_out_odd", 0, dm), f"dx_out_odd{i}")
            dgate, dra, dix, dxc1, grads["rg_lambda"] = _rg_scan_bwd(
                dgo, sv["proj"], sv["hs"], sv["xc"], sv["ra"], sv["ix"], sm["rg_lambda"], name=f"rg_scan_bwd{i}")
            (dxb, grads["conv_w"], grads["conv_b"], grads["rg_wa"], grads["rg_ba"], grads["rg_wx"],
             grads["rg_bx"]) = _rg_gates_bwd(dra, dix, dxc1, sv["xc"], sv["proj"], sm["conv_w"], sm["rg_wa"],
                                            sm["rg_wx"], name=f"rg_gates_bwd{i}")
            send("small", 0, {k: grads.pop(k) for k in _SMALL})
            dproj = jnp.concatenate([dgate, dxb], axis=1)
            send("w_in_odd", 0, _tn(sv["n1"], dproj, f"d_in_odd{i}"))
            dn1 = _nt(dproj, weight("w_in_odd", 0, dproj), f"dx_in_odd{i}")

        def pre_norm_bwd(dy, dn, hh, g):
            _, vjp = jax.vjp(_rms, hh, g)
            dx, dg = vjp(dn)
            return dy + dx, dg

        dh, rep_grads["mix_pre_g"][i] = rowcall(
            f"pre_norm_bwd{i}", pre_norm_bwd, [dh1, dn1, sv["h"]], [gain("mix_pre_g", i)], [F32],
            red_rows=(1,), cols=d)

    for k, rows in rep_grads.items():
        grads[k] = jnp.concatenate(rows, axis=0)
    return loss_cols, dh, grads


_WEIGHTS = ("mix_pre_g", "mix_post_g", "ffn_pre_g", "ffn_post_g", "ple_norm_g", "w_in_even", "w_out_even",
            "hg_lb_logits", "hg_norm_g", "w_in_odd", "conv_w", "conv_b", "rg_wa", "rg_ba", "rg_wx", "rg_bx",
            "rg_lambda", "w_out_odd", "w_gate_up", "w_down", "w_ple_up", "w_ple_gate")
_REPLICATED = ("mix_pre_g", "mix_post_g", "ffn_pre_g", "ffn_post_g", "ple_norm_g", "hg_lb_logits", "hg_norm_g")
_SMALL = ("conv_w", "conv_b", "rg_wa", "rg_ba", "rg_wx", "rg_bx", "rg_lambda")
_BIG = {"w_in_even": True, "w_out_even": False, "w_in_odd": True, "w_out_odd": False,
        "w_gate_up": True, "w_down": False, "w_ple_up": True, "w_ple_gate": False}
_PACK_ROW = SUBLANES * LANES


def _pack(arrays):
    flat = jnp.concatenate([a.reshape(-1) for a in arrays])
    pad = -flat.shape[0] % _PACK_ROW
    return jnp.pad(flat, (0, pad)).reshape(-1, LANES)


def _pack_blocks(arrays):
    flat = jnp.concatenate([a.reshape(N_DEV, -1) for a in arrays], axis=1)
    pad = -flat.shape[1] % _PACK_ROW
    return jnp.pad(flat, ((0, 0), (0, pad))).reshape(N_DEV, -1, LANES)


def _unpack(packed, shapes, lead=()):
    flat = packed.reshape(lead + (-1,))
    out, pos = [], 0
    for shape in shapes:
        n = math.prod(shape)
        out.append(flat[..., pos:pos + n].reshape(lead + tuple(shape)))
        pos += n
    return out


def _to_full_small(name, blocks):
    if name == "conv_w":
        return jnp.transpose(blocks, (1, 0, 2)).reshape(blocks.shape[1], -1)
    if name in ("conv_b", "rg_lambda"):
        return blocks.reshape(1, -1)
    nb = blocks.shape[1]
    if name in ("rg_wa", "rg_wx"):
        return jnp.transpose(blocks, (1, 0, 2, 3)).reshape(nb, RG_BLOCK, RG_BLOCK)
    return jnp.transpose(blocks, (1, 0, 2)).reshape(nb, 1, RG_BLOCK)


def _to_blocks_small(name, full):
    if name == "conv_w":
        return jnp.transpose(full.reshape(full.shape[0], N_DEV, -1), (1, 0, 2))
    if name in ("conv_b", "rg_lambda"):
        return full.reshape(N_DEV, -1)
    nb = full.shape[0]
    if name in ("rg_wa", "rg_wx"):
        return jnp.transpose(full.reshape(nb, N_DEV, RG_BLOCK // N_DEV, RG_BLOCK), (1, 0, 2, 3))
    return jnp.transpose(full.reshape(nb, N_DEV, RG_BLOCK // N_DEV), (1, 0, 2))


def _step(inp):
    w = {k: inp[k] for k in _WEIGHTS}
    x, p, target = inp["x"][0], inp["p"][:, 0], inp["loss_target"][0]
    assert w["hg_lb_logits"].shape[0] == 2 and w["w_in_even"].shape[0] == 1 and w["w_in_odd"].shape[0] == 1

    n_heads = w["w_in_even"].shape[2] * N_DEV // (7 * HEAD_DIM)
    small_shapes = [w[k].shape[1:] for k in _SMALL]

    def lands_in_place(name):
        return _BIG[name] and w[name].shape[2] % LANES == 0

    depth = p.shape[0]
    order = [("w_in_even", 0), ("w_out_even", 0)] if depth else []
    for i in range(depth):
        if i == 1:
            order += [("w_in_odd", 0), ("small", 0), ("w_out_odd", 0)]
        order += [("w_gate_up", i), ("w_down", i), ("w_ple_up", i), ("w_ple_gate", i)]
    gathers = {}
    for name, l in order:
        if name == "small":
            gathers[name, l] = _gather_start(_pack([w[k][0] for k in _SMALL]), name="gather_small", relayed=False)
        else:
            gathers[name, l] = _gather_start(w[name][l].astype(BF16), name=f"gather_{name}{l}",
                                             cols=lands_in_place(name))
    ready = {}

    def relay(key, after):
        if "cols" in gathers[key] and key[0] != "small":
            gathers[key] = _gather_relay(gathers[key], after, name=f"gather_{key[0]}{key[1]}")

    def weight(name, layer, after):
        key = ("small", 0) if name in _SMALL else (name, layer)
        if key not in ready:
            relay(key, after)
            for nxt in order[order.index(key) + 1:order.index(key) + 2]:
                relay(nxt, after)
            land, = _spread_wait(gathers[key], after, name=f"gathered_{key[0]}{key[1]}")
            if name in _SMALL:
                ready[key] = {k: _to_full_small(k, b)
                              for k, b in zip(_SMALL, _unpack(land, small_shapes, lead=(N_DEV,)))}
            elif lands_in_place(name):
                ready[key] = land
            elif _BIG[name]:
                ready[key] = jnp.transpose(land, (1, 0, 2)).reshape(land.shape[1], -1)
            else:
                ready[key] = land.reshape(-1, land.shape[2])
        return ready[key][name] if name in _SMALL else ready[key]

    exchanges = []

    def emit(name, layer, g):
        if name == "small":
            handle = _exchange_start([_pack_blocks([_to_blocks_small(k, g[k]) for k in _SMALL])],
                                     name="exchange_small")
            exchanges.append((name, layer, handle))
            return handle["token"]
        _, r, c = w[name].shape
        if lands_in_place(name):
            handle = _exchange_start([g], name=f"exchange_{name}{layer}", cols=True)
        elif _BIG[name]:
            handle = _exchange_start([jnp.transpose(g.reshape(-1, N_DEV, c), (1, 0, 2))],
                                     name=f"exchange_{name}{layer}")
        else:
            handle = _exchange_start([g.reshape(N_DEV, r, c)], name=f"exchange_{name}{layer}")
        exchanges.append((name, layer, handle))
        return handle["token"]

    rep = {k: w[k] for k in _REPLICATED}
    loss_cols, dx, grads = _local_step(x, p, target, rep, weight, emit, n_heads,
                                       [h["token"] for h in gathers.values()])

    loss_part = jnp.sum(loss_cols).reshape(1)
    rep_gather = _gather_start(_pack([grads[k] for k in _REPLICATED] + [loss_part]), name="gather_rep_grads",
                               relayed=False)

    out = {}
    after = dx
    for name, layer, handle in exchanges:
        land, = _spread_wait(handle, after, name=f"exchanged_{name}{layer}")
        if name == "small":
            res = _adamw(land.reshape(N_DEV, -1, LANES), *[_pack([inp[pre + k][0] for k in _SMALL]) for pre in ("", "m_", "v_")],
                         name="adamw_small")
            for k, *vals in zip(_SMALL, *[_unpack(a, small_shapes) for a in res]):
                out[k] = [v[None] for v in vals]
        else:
            n_l, r, c = w[name].shape
            res = out[name] = _adamw(land.reshape(N_DEV, r, c),
                                     *[inp[pre + name].reshape(n_l * r, c) for pre in ("", "m_", "v_")],
                                     name=f"adamw_{name}{layer}", layer=layer, prev=out.get(name))
        after = res[0]
    for name in _BIG:
        out[name] = [a.reshape(w[name].shape) for a in out[name]]

    rep_shapes = [w[k].shape for k in _REPLICATED] + [(1,)]
    rep_parts, = _spread_wait(rep_gather, after, name="gathered_rep_grads")
    res = _adamw(rep_parts, *[_pack([inp[pre + k] for k in _REPLICATED] + [jnp.zeros((1,), F32)])
                              for pre in ("", "m_", "v_")], name="adamw_rep")
    for k, *vals in zip(_REPLICATED + ("loss",), *[_unpack(a, rep_shapes) for a in res]):
        out[k] = vals
    loss = out["loss"][0][0]

    return (loss, dx[None]) + tuple(out[k][j] for j in range(4) for k in _WEIGHTS)


def kernel(x, p, mix_pre_g, mix_post_g, ffn_pre_g, ffn_post_g, ple_norm_g, w_in_even, w_out_even, hg_lb_logits, hg_norm_g, w_in_odd, conv_w, conv_b, rg_wa, rg_ba, rg_wx, rg_bx, rg_lambda, w_out_odd, w_gate_up, w_down, w_ple_up, w_ple_gate, loss_target, m_mix_pre_g, m_mix_post_g, m_ffn_pre_g, m_ffn_post_g, m_ple_norm_g, m_w_in_even, m_w_out_even, m_hg_lb_logits, m_hg_norm_g, m_w_in_odd, m_conv_w, m_conv_b, m_rg_wa, m_rg_ba, m_rg_wx, m_rg_bx, m_rg_lambda, m_w_out_odd, m_w_gate_up, m_w_down, m_w_ple_up, m_w_ple_gate, v_mix_pre_g, v_mix_post_g, v_ffn_pre_g, v_ffn_post_g, v_ple_norm_g, v_w_in_even, v_w_out_even, v_hg_lb_logits, v_hg_norm_g, v_w_in_odd, v_conv_w, v_conv_b, v_rg_wa, v_rg_ba, v_rg_wx, v_rg_bx, v_rg_lambda, v_w_out_odd, v_w_gate_up, v_w_down, v_w_ple_up, v_w_ple_gate):
    return _step(dict(locals()))
```

```python
import functools
import math

import jax
import jax.numpy as jnp
from jax import lax
from jax.experimental import pallas as pl
from jax.experimental.pallas import tpu as pltpu

F32 = jnp.float32
BF16 = jnp.bfloat16

VMEM_LIMIT_BYTES = 56 * 1024 * 1024
LANES = 128
SUBLANES = 8

N_DEV = 8
HEAD_DIM = 128
SB_Q_TILE = 512
SB_K_TILE = 128
HG_CHUNK = 32
RG_BLOCK = 256
CONV_TAPS = 4
RG_C = 8.0
RMS_EPS = 1e-6

ADAM_LR = 0.001
ADAM_B1 = 0.9
ADAM_B2 = 0.999
ADAM_EPS = 1e-08
ADAM_WD = 0.01
ADAM_STEP = 10

MESH_AXES = ("x", "y", "c")


def _params(*sem):
    return pltpu.CompilerParams(dimension_semantics=sem, vmem_limit_bytes=VMEM_LIMIT_BYTES)


def _pick(n, cands):
    for c in cands:
        if c <= n and n % c == 0:
            return c
    return n


def _mm(a, b, mode, *, name, out_dtype=F32, tm=512, tn=512, tk=None):
    if mode == "nn":
        (m, k), (k2, n) = a.shape, b.shape
    else:
        (m, k), (n, k2) = a.shape, b.shape
    assert k == k2, (a.shape, b.shape, mode)
    tm, tn = min(tm, m), min(tn, n)
    tk = k if tk is None else min(tk, k)
    assert m % tm == 0 and n % tn == 0 and k % tk == 0, (m, n, k, tm, tn, tk)
    nk = k // tk

    a_spec = pl.BlockSpec((tm, tk), lambda i, j, kk: (i, kk))
    if mode == "nn":
        b_spec = pl.BlockSpec((tk, tn), lambda i, j, kk: (kk, j))
        dims = (((1,), (0,)), ((), ()))
    else:
        b_spec = pl.BlockSpec((tn, tk), lambda i, j, kk: (j, kk))
        dims = (((1,), (1,)), ((), ()))

    def body(a_ref, b_ref, o_ref, *acc):
        part = lax.dot_general(a_ref[...].astype(BF16), b_ref[...].astype(BF16), dims, preferred_element_type=F32)
        if nk == 1:
            o_ref[...] = part.astype(out_dtype)
        else:
            acc_ref, = acc
            kk = pl.program_id(2)

            @pl.when(kk == 0)
            def _():
                acc_ref[...] = part

            @pl.when(kk > 0)
            def _():
                acc_ref[...] += part

            @pl.when(kk == nk - 1)
            def _():
                o_ref[...] = acc_ref[...].astype(out_dtype)

    return pl.pallas_call(
        body, name=name,
        grid=(m // tm, n // tn, nk),
        in_specs=[a_spec, b_spec],
        out_specs=pl.BlockSpec((tm, tn), lambda i, j, kk: (i, j)),
        out_shape=jax.ShapeDtypeStruct((m, n), out_dtype),
        scratch_shapes=[] if nk == 1 else [pltpu.VMEM((tm, tn), F32)],
        compiler_params=_params("parallel", "parallel", "arbitrary"),
    )(a, b)


def _mm_tn(a, b, *, name, out_dtype, tm, tn, deps=()):
    k, m = a.shape
    n = b.shape[1]

    def body(a_ref, b_ref, *refs):
        o_ref, at_ref = refs[len(deps):]

        @pl.when(pl.program_id(1) == 0)
        def _():
            at_ref[...] = a_ref[...].astype(F32).T.astype(BF16)

        o_ref[...] = jnp.dot(at_ref[...], b_ref[...].astype(BF16), preferred_element_type=F32).astype(out_dtype)

    return pl.pallas_call(
        body, name=name,
        grid=(m // tm, n // tn),
        in_specs=[pl.BlockSpec((k, tm), lambda i, j: (0, i)), pl.BlockSpec((k, tn), lambda i, j: (0, j))]
        + [pl.BlockSpec(memory_space=pl.ANY)] * len(deps),
        out_specs=pl.BlockSpec((tm, tn), lambda i, j: (i, j)),
        out_shape=jax.ShapeDtypeStruct((m, n), out_dtype),
        scratch_shapes=[pltpu.VMEM((tm, k), BF16)],
        compiler_params=_params("parallel", "arbitrary"),
    )(a, b, *deps)


def _rowcall(name, fn, rows, pars, row_outs, red_rows=(), *, cols, ts=256, tc=None, deps=()):
    rows = [r if isinstance(r, tuple) else (r, 0) for r in rows]
    pars = [p if isinstance(p, tuple) else (p, 0) for p in pars]
    s = rows[0][0].shape[0]
    tc = cols if tc is None else tc
    ts = min(ts, s)
    assert s % ts == 0 and cols % tc == 0, (name, s, ts, cols, tc)
    n_in, n_row_out = len(rows) + len(pars), len(row_outs)

    def body(*refs):
        outs = fn(*[r[...] for r in refs[:n_in]])
        outs = outs if isinstance(outs, (tuple, list)) else (outs,)
        o_refs = refs[n_in + len(deps):]
        for o_ref, val in zip(o_refs[:n_row_out], outs[:n_row_out]):
            o_ref[...] = val.astype(o_ref.dtype)
        first = pl.program_id(1) == 0
        for o_ref, val in zip(o_refs[n_row_out:], outs[n_row_out:]):
            @pl.when(first)
            def _(o_ref=o_ref, val=val):
                o_ref[...] = val

            @pl.when(jnp.logical_not(first))
            def _(o_ref=o_ref, val=val):
                o_ref[...] += val

    def row_map(off):
        return lambda j, i: (i, j + off)

    def par_map(off):
        return lambda j, i: (0, j + off)

    return pl.pallas_call(
        body, name=name,
        grid=(cols // tc, s // ts),
        in_specs=[pl.BlockSpec((ts, tc), row_map(off)) for _, off in rows]
        + [pl.BlockSpec((p.shape[0], tc), par_map(off)) for p, off in pars]
        + [pl.BlockSpec(memory_space=pl.ANY)] * len(deps),
        out_specs=[pl.BlockSpec((ts, tc), lambda j, i: (i, j)) for _ in row_outs]
        + [pl.BlockSpec((r, tc), lambda j, i: (0, j)) for r in red_rows],
        out_shape=[jax.ShapeDtypeStruct((s, cols), dt) for dt in row_outs]
        + [jax.ShapeDtypeStruct((r, cols), F32) for r in red_rows],
        compiler_params=_params("parallel", "arbitrary"),
    )(*[r for r, _ in rows], *[p for p, _ in pars], *deps)


def _swiglu(gu, dact=None, *, name, ts=128):
    s, two_f = gu.shape
    f = two_f // 2
    ts = min(ts, s)

    def act(g, u):
        return _silu(g) * u

    def body(gu_ref, *refs):
        g, u = gu_ref[:, 0:f].astype(F32), gu_ref[:, f:two_f].astype(F32)
        if dact is None:
            refs[0][...] = act(g, u).astype(BF16)
        else:
            dact_ref, o_ref = refs
            _, vjp = jax.vjp(act, g, u)
            dg, du = vjp(dact_ref[...].astype(F32))
            o_ref[:, 0:f] = dg.astype(BF16)
            o_ref[:, f:two_f] = du.astype(BF16)

    wide, narrow = pl.BlockSpec((ts, two_f), lambda i: (i, 0)), pl.BlockSpec((ts, f), lambda i: (i, 0))
    return pl.pallas_call(
        body, name=name,
        grid=(s // ts,),
        in_specs=[wide] if dact is None else [wide, narrow],
        out_specs=narrow if dact is None else wide,
        out_shape=jax.ShapeDtypeStruct((s, f if dact is None else two_f), BF16),
        compiler_params=_params("parallel"),
    )(*([gu] if dact is None else [gu, dact]))


def _rms(x, g):
    return x * lax.rsqrt(jnp.mean(x * x, axis=-1, keepdims=True) + RMS_EPS) * g


def _sigmoid(x):
    return jax.nn.sigmoid(x)


def _silu(x):
    return x * jax.nn.sigmoid(x)


def _gelu(x):
    return 0.5 * x * (1.0 + jnp.tanh(math.sqrt(2.0 / math.pi) * (x + 0.044715 * (x * x * x))))


def _softplus(x):
    return jnp.maximum(x, 0.0) + jnp.log1p(jnp.exp(-jnp.abs(x)))


def _split(x, terms):
    parts = []
    for _ in range(terms - 1):
        parts.append(x.astype(BF16))
        x = x - parts[-1].astype(F32)
    return parts + [x.astype(BF16)]


def _xdot(x, t, terms=3):
    return sum(jnp.dot(p, t, preferred_element_type=F32) for p in _split(x, terms))


def _xdot_l(t, x):
    return sum(jnp.dot(t, p, preferred_element_type=F32) for p in _split(x, 3))


_NT = (((1,), (1,)), ((), ()))
_TN = (((0,), (0,)), ((), ()))


def _dot(a, b, dims=None):
    if dims is None:
        return jnp.dot(a.astype(BF16), b.astype(BF16), preferred_element_type=F32)
    return lax.dot_general(a.astype(BF16), b.astype(BF16), dims, preferred_element_type=F32)


def _iota(shape, axis):
    return lax.broadcasted_iota(jnp.int32, shape, axis)


def _sb_tile(qb, kblk, mask, upper, c_rem):
    z = lax.dot_general(qb, kblk, _NT, preferred_element_type=F32)
    soft = jnp.log1p(jnp.exp(-jnp.abs(z)))
    lbeta = jnp.minimum(z, 0.0) - soft
    l1m = -jnp.maximum(z, 0.0) - soft
    if mask is not None:
        l1m = jnp.where(mask, l1m, 0.0)
    rem = _xdot(l1m, upper, terms=2) + c_rem
    w = jnp.exp(lbeta + rem)
    if mask is not None:
        w = jnp.where(mask, w, 0.0)
    return lbeta, l1m, w


def _sb_tiles(s):
    tq = min(SB_Q_TILE, s)
    return tq, SB_K_TILE, tq // SB_K_TILE


def _sb_key_loops(qi, per_q, step, carry):
    n_full = qi * per_q
    carry = lax.fori_loop(0, per_q, lambda j, c: step(n_full + per_q - 1 - j, True, c), carry)
    return lax.fori_loop(0, n_full, lambda j, c: step(n_full - 1 - j, False, c), carry)


def _sb_fwd(proj, n_heads, *, name):
    s = proj.shape[0]
    t, tk, per_q = _sb_tiles(s)
    scale = HEAD_DIM ** -0.5

    def body(q_ref, k_ref, v_ref, o_ref):
        qi = pl.program_id(1)
        qb = (q_ref[...] * scale).astype(BF16)
        row, col = _iota((t, tk), 0) + qi * t, _iota((t, tk), 1)
        upper = (_iota((tk, tk), 0) > _iota((tk, tk), 1)).astype(BF16)

        def step(kb, masked, carry):
            acc, c_rem = carry
            rows = pl.ds(pl.multiple_of(kb * tk, tk), tk)
            kblk = k_ref[rows, :].astype(BF16)
            vblk = v_ref[rows, :].astype(BF16)
            _, l1m, w = _sb_tile(qb, kblk, (col + kb * tk) < row if masked else None, upper, c_rem)
            acc = acc + jnp.dot(w.astype(BF16), vblk, preferred_element_type=F32)
            return acc, c_rem + jnp.sum(l1m, axis=1, keepdims=True)

        acc, _ = _sb_key_loops(qi, per_q, step, (jnp.zeros((t, HEAD_DIM), F32), jnp.zeros((t, 1), F32)))
        o_ref[...] = acc

    return pl.pallas_call(
        body, name=name,
        grid=(n_heads, s // t),
        in_specs=[pl.BlockSpec((t, HEAD_DIM), lambda h, i: (i, h)),
                  pl.BlockSpec((s, HEAD_DIM), lambda h, i: (0, n_heads + h)),
                  pl.BlockSpec((s, HEAD_DIM), lambda h, i: (0, 2 * n_heads + h))],
        out_specs=pl.BlockSpec((t, HEAD_DIM), lambda h, i: (i, h)),
        out_shape=jax.ShapeDtypeStruct((s, n_heads * HEAD_DIM), F32),
        compiler_params=_params("parallel", "arbitrary"),
    )(proj, proj, proj)


def _sb_bwd(proj, dcat, n_heads, *, name):
    s = proj.shape[0]
    t, tk, per_q = _sb_tiles(s)
    scale = HEAD_DIM ** -0.5

    def body(q_ref, k_ref, v_ref, do_ref, dq_ref, dk_ref, dv_ref, g_s, sig_s):
        qi = pl.program_id(1)

        @pl.when(qi == 0)
        def _():
            dk_ref[...] = jnp.zeros_like(dk_ref)
            dv_ref[...] = jnp.zeros_like(dv_ref)

        qb = (q_ref[...] * scale).astype(BF16)
        dob = do_ref[...].astype(BF16)
        row, col = _iota((t, tk), 0) + qi * t, _iota((t, tk), 1)
        upper = (_iota((tk, tk), 0) > _iota((tk, tk), 1)).astype(BF16)
        lower_incl = (_iota((tk, tk), 0) >= _iota((tk, tk), 1)).astype(BF16)

        def weights(kb, masked, carry):
            c_rem, g_all = carry
            rows = pl.ds(pl.multiple_of(kb * tk, tk), tk)
            kblk = k_ref[rows, :].astype(BF16)
            vblk = v_ref[rows, :].astype(BF16)
            lbeta, l1m, w = _sb_tile(qb, kblk, (col + kb * tk) < row if masked else None, upper, c_rem)
            g = w * lax.dot_general(dob, vblk, _NT, preferred_element_type=F32)
            dv_ref[rows, :] += lax.dot_general(w.astype(BF16), dob, _TN, preferred_element_type=F32)
            g_s[kb] = g
            sig_s[kb] = jnp.exp(lbeta)
            return c_rem + jnp.sum(l1m, axis=1, keepdims=True), g_all + jnp.sum(g, axis=1, keepdims=True)

        zero_col = jnp.zeros((t, 1), F32)
        _, g_all = _sb_key_loops(qi, per_q, weights, (zero_col, zero_col))

        def scores(kb, masked, carry):
            dq, c_g = carry
            rows = pl.ds(pl.multiple_of(kb * tk, tk), tk)
            g, sig = g_s[kb], sig_s[kb]
            g_before = g_all - (_xdot(g, lower_incl) + c_g)
            dz = g * (1.0 - sig) - g_before * sig
            if masked:
                dz = jnp.where((col + kb * tk) < row, dz, 0.0)
            dz = dz.astype(BF16)
            dq = dq + jnp.dot(dz, k_ref[rows, :].astype(BF16), preferred_element_type=F32)
            dk_ref[rows, :] += lax.dot_general(dz, qb, _TN, preferred_element_type=F32)
            return dq, c_g + jnp.sum(g, axis=1, keepdims=True)

        dq, _ = _sb_key_loops(qi, per_q, scores, (jnp.zeros((t, HEAD_DIM), F32), zero_col))
        dq_ref[...] = dq * scale

    width = n_heads * HEAD_DIM
    return pl.pallas_call(
        body, name=name,
        grid=(n_heads, s // t),
        in_specs=[pl.BlockSpec((t, HEAD_DIM), lambda h, i: (i, h)),
                  pl.BlockSpec((s, HEAD_DIM), lambda h, i: (0, n_heads + h)),
                  pl.BlockSpec((s, HEAD_DIM), lambda h, i: (0, 2 * n_heads + h)),
                  pl.BlockSpec((t, HEAD_DIM), lambda h, i: (i, h))],
        out_specs=[pl.BlockSpec((t, HEAD_DIM), lambda h, i: (i, h)),
                   pl.BlockSpec((s, HEAD_DIM), lambda h, i: (0, h)),
                   pl.BlockSpec((s, HEAD_DIM), lambda h, i: (0, h))],
        out_shape=[jax.ShapeDtypeStruct((s, width), F32)] * 3,
        scratch_shapes=[pltpu.VMEM((s // tk, t, tk), F32)] * 2,
        compiler_params=_params("parallel", "arbitrary"),
    )(proj, proj, proj, dcat)


def _hg_pre(hq, hf, logits):
    mx = jnp.max(logits, axis=0, keepdims=True)
    ex = jnp.exp(logits - mx)
    lb = ex[0:1, :] / jnp.sum(ex, axis=0, keepdims=True)
    f = lb + (1.0 - lb) * _sigmoid(hf)
    return _silu(hq), 1.0 - f, jnp.log(f)


def _hg_post(o, norm_g, hgate):
    return _rms(o, norm_g) * _silu(hgate)


def _hg_specs(s, n_heads, first_block):
    def at(group):
        return pl.BlockSpec((s, HEAD_DIM), lambda h: (0, first_block + group * n_heads + h))
    return [at(0), at(1), at(2), at(3)]


def _hg_fwd(proj, logits, norm_g, n_heads, *, name):
    s = proj.shape[0]
    hc = HG_CHUNK
    n_chunks = s // hc
    d = HEAD_DIM

    def body(lg_ref, ng_ref, hq_ref, hf_ref, hi_ref, hgt_ref, out_ref, oraw_ref, st_ref,
             q_s, k_s, lf_s, cum_s, qc_s, oc_s):
        q, k, lf = _hg_pre(hq_ref[...], hf_ref[...], lg_ref[...])
        q_s[...] = q
        k_s[...] = k
        lf_s[...] = lf
        tril = (_iota((hc, hc), 0) >= _iota((hc, hc), 1)).astype(BF16)
        srow = _iota((hc, d), 0)

        def chunk(ci, st):
            rows = pl.ds(pl.multiple_of(ci * hc, hc), hc)
            q, k, v = q_s[rows, :], k_s[rows, :], hi_ref[rows, :]
            cum = _xdot_l(tril, lf_s[rows, :])
            st_ref[0, ci] = st
            o_inter = _dot(q * jnp.exp(cum), st, _NT)
            cum_s[...] = cum
            qc_s[...] = q
            for t in range(hc):
                ng = (t // SUBLANES + 1) * SUBLANES
                e = jnp.where(srow[:ng] <= t, jnp.exp(cum_s[t:t + 1, :] - cum[:ng]), 0.0)
                sc = jnp.sum(qc_s[t:t + 1, :] * k[:ng] * e, axis=1, keepdims=True)
                oc_s[t:t + 1, :] = jnp.sum(sc * v[:ng], axis=0, keepdims=True)
            oraw_ref[rows, :] = o_inter + oc_s[...]
            last = cum_s[hc - 1:hc, :]
            return st * jnp.exp(last) + _dot(v, k * jnp.exp(last - cum), _TN)

        lax.fori_loop(0, n_chunks, chunk, jnp.zeros((d, d), F32))
        out_ref[...] = _hg_post(oraw_ref[...], ng_ref[...], hgt_ref[...]).astype(BF16)

    width = n_heads * d
    head_block = pl.BlockSpec((s, d), lambda h: (0, h))
    return pl.pallas_call(
        body, name=name,
        grid=(n_heads,),
        in_specs=[pl.BlockSpec((2, d), lambda h: (0, h)), pl.BlockSpec((1, d), lambda h: (0, 0))]
        + _hg_specs(s, n_heads, 3 * n_heads),
        out_specs=[head_block, head_block, pl.BlockSpec((1, n_chunks, d, d), lambda h: (h, 0, 0, 0))],
        out_shape=[jax.ShapeDtypeStruct((s, width), BF16), jax.ShapeDtypeStruct((s, width), F32),
                   jax.ShapeDtypeStruct((n_heads, n_chunks, d, d), F32)],
        scratch_shapes=[pltpu.VMEM((s, d), F32)] * 3 + [pltpu.VMEM((hc, d), F32)] * 3,
        compiler_params=_params("arbitrary"),
    )(logits, norm_g, proj, proj, proj, proj)


def _hg_bwd(proj, logits, norm_g, oraw, states, dcat, n_heads, *, name):
    s = proj.shape[0]
    hc = HG_CHUNK
    n_chunks = s // hc
    d = HEAD_DIM

    def body(lg_ref, ng_ref, hq_ref, hf_ref, hi_ref, hgt_ref, oraw_ref, st_ref, dout_ref,
             dhq_ref, dhf_ref, dhi_ref, dhgt_ref, dlg_ref, dng_ref,
             q_s, k_s, lf_s, do_s, dq_s, dk_s, dlf_s, cum_s, qc_s, doc_s, dqc_s, dkc_s, dvc_s):
        head = pl.program_id(0)
        (q, k, lf), pre_vjp = jax.vjp(_hg_pre, hq_ref[...], hf_ref[...], lg_ref[...])
        q_s[...] = q
        k_s[...] = k
        lf_s[...] = lf
        _, post_vjp = jax.vjp(_hg_post, oraw_ref[...], ng_ref[...], hgt_ref[...])
        do, dng, dhgt = post_vjp(dout_ref[...])
        do_s[...] = do
        dhgt_ref[...] = dhgt.astype(BF16)

        @pl.when(head == 0)
        def _():
            dng_ref[...] = dng

        @pl.when(head > 0)
        def _():
            dng_ref[...] += dng

        triu = (_iota((hc, hc), 0) <= _iota((hc, hc), 1)).astype(BF16)
        tril = (_iota((hc, hc), 0) >= _iota((hc, hc), 1)).astype(BF16)
        srow = _iota((hc, d), 0)

        def chunk(j, dst):
            ci = n_chunks - 1 - j
            rows = pl.ds(pl.multiple_of(ci * hc, hc), hc)
            q, k, v, do_c = q_s[rows, :], k_s[rows, :], hi_ref[rows, :], do_s[rows, :]
            cum = _xdot_l(tril, lf_s[rows, :])
            st = st_ref[0, ci]
            cum_s[...] = cum
            qc_s[...] = q
            doc_s[...] = do_c
            last = cum_s[hc - 1:hc, :]
            e_cum, e_last = jnp.exp(cum), jnp.exp(last - cum)
            dqc_s[...] = _dot(do_c, st) * e_cum
            dk_state = _dot(v, dst) * e_last
            dkc_s[...] = dk_state
            dvc_s[...] = _dot(k * e_last, dst, _NT)
            d_last = (jnp.sum(dst * st, axis=0, keepdims=True) * jnp.exp(last)
                      + jnp.sum(k * dk_state, axis=0, keepdims=True))
            for t in range(hc):
                ng = (t // SUBLANES + 1) * SUBLANES
                qt, dot_ = qc_s[t:t + 1, :], doc_s[t:t + 1, :]
                e = jnp.where(srow[:ng] <= t, jnp.exp(cum_s[t:t + 1, :] - cum[:ng]), 0.0)
                ke = k[:ng] * e
                d_a = jnp.sum(dot_ * v[:ng], axis=1, keepdims=True)
                dqc_s[t:t + 1, :] += jnp.sum(d_a * ke, axis=0, keepdims=True)
                dkc_s[0:ng, :] += d_a * (qt * e)
                dvc_s[0:ng, :] += jnp.sum(qt * ke, axis=1, keepdims=True) * dot_
            dq, dk = dqc_s[...], dkc_s[...]
            d_b = q * dq - k * dk
            dq_s[rows, :] = dq
            dk_s[rows, :] = dk
            dhi_ref[rows, :] = dvc_s[...].astype(BF16)
            dlf_s[rows, :] = _xdot_l(triu, d_b) + d_last
            return dst * jnp.exp(last) + _dot(do_c, q * e_cum, _TN)

        lax.fori_loop(0, n_chunks, chunk, jnp.zeros((d, d), F32))
        dhq, dhf, dlg = pre_vjp((dq_s[...], dk_s[...], dlf_s[...]))
        dhq_ref[...] = dhq.astype(BF16)
        dhf_ref[...] = dhf.astype(BF16)
        dlg_ref[...] = dlg

    width = n_heads * d
    head_block = pl.BlockSpec((s, d), lambda h: (0, h))
    return pl.pallas_call(
        body, name=name,
        grid=(n_heads,),
        in_specs=[pl.BlockSpec((2, d), lambda h: (0, h)), pl.BlockSpec((1, d), lambda h: (0, 0))]
        + _hg_specs(s, n_heads, 3 * n_heads)
        + [head_block, pl.BlockSpec((1, n_chunks, d, d), lambda h: (h, 0, 0, 0)),
           pl.BlockSpec((s, d), lambda h: (0, n_heads + h))],
        out_specs=[head_block] * 4 + [pl.BlockSpec((2, d), lambda h: (0, h)), pl.BlockSpec((1, d), lambda h: (0, 0))],
        out_shape=[jax.ShapeDtypeStruct((s, width), BF16)] * 4
        + [jax.ShapeDtypeStruct((2, width), F32), jax.ShapeDtypeStruct((1, d), F32)],
        scratch_shapes=[pltpu.VMEM((s, d), F32)] * 7 + [pltpu.VMEM((hc, d), F32)] * 6,
        compiler_params=_params("arbitrary"),
    )(logits, norm_g, proj, proj, proj, proj, oraw, states, dcat)


def _shift_down(x, n, srow):
    if n == 0:
        return x
    return jnp.where(srow >= n, pltpu.roll(x, n, 0), 0.0)


def _shift_up(x, n, srow):
    if n == 0:
        return x
    s = x.shape[0]
    return jnp.where(srow < s - n, pltpu.roll(x, s - n, 0), 0.0)


def _rg_gates_fwd(proj, conv_w, conv_b, wa, ba, wx, bx, *, name):
    s = proj.shape[0]
    nb = wa.shape[0]
    bw = RG_BLOCK

    def body(xb_ref, cw_ref, cb_ref, wa_ref, ba_ref, wx_ref, bx_ref, xc_ref, ra_ref, ix_ref):
        x = xb_ref[...]
        srow = _iota((s, bw), 0)
        cw = cw_ref[...]
        xc = cb_ref[...] + cw[0:1, :] * x
        for tap in range(1, CONV_TAPS):
            xc = xc + cw[tap:tap + 1, :] * _shift_down(x, tap, srow)
        xc_ref[...] = xc
        ra_ref[...] = _dot(xc, wa_ref[0]) + ba_ref[0]
        ix_ref[...] = _dot(xc, wx_ref[0]) + bx_ref[0]

    col = pl.BlockSpec((s, bw), lambda n: (0, n))
    vec = lambda r: pl.BlockSpec((r, bw), lambda n: (0, n))
    mat = pl.BlockSpec((1, bw, bw), lambda n: (n, 0, 0))
    bias = pl.BlockSpec((1, 1, bw), lambda n: (n, 0, 0))
    return pl.pallas_call(
        body, name=name,
        grid=(nb,),
        in_specs=[pl.BlockSpec((s, bw), lambda n: (0, nb + n)), vec(CONV_TAPS), vec(1), mat, bias, mat, bias],
        out_specs=[col] * 3,
        out_shape=[jax.ShapeDtypeStruct((s, nb * bw), F32)] * 3,
        compiler_params=_params("parallel"),
    )(proj, conv_w, conv_b, wa, ba, wx, bx)


def _rg_au(ra, ix, xc, lam, first_row):
    log_a = -RG_C * _sigmoid(ra) * _softplus(-lam)
    th = jnp.tanh(log_a)
    one_minus_a2 = -2.0 * th / (1.0 - th)
    mult = jnp.where(first_row, 1.0, jnp.sqrt(one_minus_a2))
    return jnp.exp(log_a), xc * _sigmoid(ix) * mult


def _rg_out(gate, hs):
    return _gelu(gate) * hs


def _linear_scan(a, b, a_s, b_s, in_s, reverse):
    s, c = a.shape
    within = _iota((s, c), 0) & (SUBLANES - 1)
    shift = 1
    while shift < SUBLANES:
        if reverse:
            take = within < SUBLANES - shift
            a_n, b_n = pltpu.roll(a, s - shift, 0), pltpu.roll(b, s - shift, 0)
        else:
            take = within >= shift
            a_n, b_n = pltpu.roll(a, shift, 0), pltpu.roll(b, shift, 0)
        b = jnp.where(take, a * b_n + b, b)
        a = jnp.where(take, a * a_n, a)
        shift *= 2
    a_s[...] = a
    b_s[...] = b
    n_tiles = s // SUBLANES
    edge = 0 if reverse else SUBLANES - 1

    def tile(i, h):
        rows = pl.ds(pl.multiple_of(((n_tiles - 1 - i) if reverse else i) * SUBLANES, SUBLANES), SUBLANES)
        in_s[rows, :] = jnp.broadcast_to(h, (SUBLANES, c))
        return a_s[rows, :][edge:edge + 1, :] * h + b_s[rows, :][edge:edge + 1, :]

    lax.fori_loop(0, n_tiles, tile, jnp.zeros((1, c), F32))
    return a * in_s[...] + b


def _rg_scan_fwd(proj, xc, ra, ix, lam, *, name):
    s, width = xc.shape
    tc = LANES

    def body(gate_ref, xc_ref, ra_ref, ix_ref, lam_ref, hs_ref, gact_ref, a_s, u_s, in_s):
        first_row = _iota((s, tc), 0) == 0
        a, u = _rg_au(ra_ref[...], ix_ref[...], xc_ref[...], lam_ref[...], first_row)
        hs = _linear_scan(a, u, a_s, u_s, in_s, reverse=False)
        hs_ref[...] = hs
        gact_ref[...] = _rg_out(gate_ref[...], hs).astype(BF16)

    col = pl.BlockSpec((s, tc), lambda n: (0, n))
    return pl.pallas_call(
        body, name=name,
        grid=(width // tc,),
        in_specs=[col, col, col, col, pl.BlockSpec((1, tc), lambda n: (0, n))],
        out_specs=[col, col],
        out_shape=[jax.ShapeDtypeStruct((s, width), F32), jax.ShapeDtypeStruct((s, width), BF16)],
        scratch_shapes=[pltpu.VMEM((s, tc), F32)] * 3,
        compiler_params=_params("parallel"),
    )(proj, xc, ra, ix, lam)


def _rg_scan_bwd(dgo, proj, hs, xc, ra, ix, lam, *, name):
    s, width = xc.shape
    tc = LANES

    def body(dgo_ref, gate_ref, hs_ref, xc_ref, ra_ref, ix_ref, lam_ref,
             dgate_ref, dra_ref, dix_ref, dxc_ref, dlam_ref, a_s, dh_s, g_s):
        srow = _iota((s, tc), 0)
        hs = hs_ref[...]
        _, out_vjp = jax.vjp(_rg_out, gate_ref[...], hs)
        dgate, dh = out_vjp(dgo_ref[...])
        dgate_ref[...] = dgate.astype(BF16)
        au = functools.partial(_rg_au, first_row=srow == 0)
        (a, _), au_vjp = jax.vjp(au, ra_ref[...], ix_ref[...], xc_ref[...], lam_ref[...])
        g = _linear_scan(_shift_up(a, 1, srow), dh, a_s, dh_s, g_s, reverse=True)
        dra, dix, dxc, dlam = au_vjp((g * _shift_down(hs, 1, srow), g))
        dra_ref[...] = dra.astype(BF16)
        dix_ref[...] = dix.astype(BF16)
        dxc_ref[...] = dxc
        dlam_ref[...] = dlam

    col = pl.BlockSpec((s, tc), lambda n: (0, n))
    vec = pl.BlockSpec((1, tc), lambda n: (0, n))
    return pl.pallas_call(
        body, name=name,
        grid=(width // tc,),
        in_specs=[col] * 6 + [vec],
        out_specs=[col] * 4 + [vec],
        out_shape=[jax.ShapeDtypeStruct((s, width), BF16)] * 3
        + [jax.ShapeDtypeStruct((s, width), F32), jax.ShapeDtypeStruct((1, width), F32)],
        scratch_shapes=[pltpu.VMEM((s, tc), F32)] * 3,
        compiler_params=_params("parallel"),
    )(dgo, proj, hs, xc, ra, ix, lam)


def _rg_gates_bwd(dra, dix, dxc1, xc, proj, conv_w, wa, wx, *, name):
    s = proj.shape[0]
    nb = wa.shape[0]
    bw = RG_BLOCK

    def body(dra_ref, dix_ref, dxc_ref, xc_ref, xb_ref, cw_ref, wa_ref, wx_ref,
             dxb_ref, dcw_ref, dcb_ref, dwa_ref, dba_ref, dwx_ref, dbx_ref):
        dra, dix = dra_ref[...], dix_ref[...]
        xc_t = xc_ref[...].T.astype(BF16)
        dwa_ref[0] = jnp.dot(xc_t, dra, preferred_element_type=F32)
        dwx_ref[0] = jnp.dot(xc_t, dix, preferred_element_type=F32)
        dba_ref[0] = jnp.sum(dra.astype(F32), axis=0, keepdims=True)
        dbx_ref[0] = jnp.sum(dix.astype(F32), axis=0, keepdims=True)
        dxc = dxc_ref[...] + _dot(dra, wa_ref[0], _NT) + _dot(dix, wx_ref[0], _NT)
        srow = _iota((s, bw), 0)
        x = xb_ref[...]
        cw = cw_ref[...]
        dx = cw[0:1, :] * dxc
        dcw = [jnp.sum(dxc * x, axis=0, keepdims=True)]
        for tap in range(1, CONV_TAPS):
            dx = dx + cw[tap:tap + 1, :] * _shift_up(dxc, tap, srow)
            dcw.append(jnp.sum(dxc * _shift_down(x, tap, srow), axis=0, keepdims=True))
        dxb_ref[...] = dx.astype(BF16)
        r4 = _iota((CONV_TAPS, bw), 0)
        acc = jnp.zeros((CONV_TAPS, bw), F32)
        for tap in range(CONV_TAPS):
            acc = jnp.where(r4 == tap, dcw[tap], acc)
        dcw_ref[...] = acc
        dcb_ref[...] = jnp.sum(dxc, axis=0, keepdims=True)

    col = pl.BlockSpec((s, bw), lambda n: (0, n))
    vec = lambda r: pl.BlockSpec((r, bw), lambda n: (0, n))
    mat = pl.BlockSpec((1, bw, bw), lambda n: (n, 0, 0))
    bias = pl.BlockSpec((1, 1, bw), lambda n: (n, 0, 0))
    width = nb * bw
    return pl.pallas_call(
        body, name=name,
        grid=(nb,),
        in_specs=[col, col, col, col, pl.BlockSpec((s, bw), lambda n: (0, nb + n)), vec(CONV_TAPS), mat, mat],
        out_specs=[col, vec(CONV_TAPS), vec(1), mat, bias, mat, bias],
        out_shape=[jax.ShapeDtypeStruct((s, width), BF16), jax.ShapeDtypeStruct((CONV_TAPS, width), F32),
                   jax.ShapeDtypeStruct((1, width), F32), jax.ShapeDtypeStruct((nb, bw, bw), F32),
                   jax.ShapeDtypeStruct((nb, 1, bw), F32), jax.ShapeDtypeStruct((nb, bw, bw), F32),
                   jax.ShapeDtypeStruct((nb, 1, bw), F32)],
        compiler_params=_params("parallel"),
    )(dra, dix, dxc1, xc, proj, conv_w, wa, wx)


_HBM = pl.BlockSpec(memory_space=pltpu.HBM)
_FLIPS = ((0, 0, 1), (1, 0, 0), (0, 1, 0), (1, 1, 0))
_ALL_FLIPS = tuple((a, b, c) for a in (0, 1) for b in (0, 1) for c in (0, 1))[1:]


def _flip(pos, f):
    return tuple(1 - p if b else p for p, b in zip(pos, f))


def _dev_index(pos):
    return 4 * pos[0] + 2 * pos[1] + pos[2]


def _block(ref, idx, cols):
    if not cols:
        return ref.at[idx]
    n = ref.shape[-1] // N_DEV
    start = pl.multiple_of(idx * n, LANES)
    return ref.at[(slice(None),) * (len(ref.shape) - 1) + (pl.ds(start, n),)]


def _all_gather(xs, *, name, cols=False):
    n_arr = len(xs)

    def body(*refs):
        x_refs, out_refs = refs[:n_arr], refs[n_arr:2 * n_arr]
        send_sems, recv_sems, local_sems = refs[2 * n_arr:]
        me = (lax.axis_index("x"), lax.axis_index("y"), lax.axis_index("c"))
        sibling = _flip(me, _FLIPS[0])
        chips = [_flip(me, f) for f in _FLIPS[1:]]

        def copy(a, k, block, to, src=None):
            dst = _block(out_refs[a], _dev_index(block), cols)
            return pltpu.make_async_remote_copy(
                src_ref=dst if src is None else src, dst_ref=dst,
                send_sem=send_sems.at[7 * a + k], recv_sem=recv_sems.at[7 * a + k],
                device_id=to, device_id_type=pl.DeviceIdType.MESH)

        mine = [pltpu.make_async_copy(x_refs[a], _block(out_refs[a], _dev_index(me), cols), local_sems.at[a])
                for a in range(n_arr)]
        for cp in mine:
            cp.start()
        first = []
        for a in range(n_arr):
            first.append(copy(a, 0, me, sibling, src=x_refs[a]))
            first += [copy(a, 1 + j, me, chip, src=x_refs[a]) for j, chip in enumerate(chips)]
        for cp in first:
            cp.start()
        passed = []
        for j, chip in enumerate(chips):
            for a in range(n_arr):
                copy(a, 1 + j, chip, me).wait_recv()
                fwd = copy(a, 4 + j, chip, sibling)
                fwd.start()
                passed.append(fwd)
        for a in range(n_arr):
            copy(a, 0, sibling, me).wait_recv()
            for j, chip in enumerate(chips):
                copy(a, 4 + j, _flip(chip, _FLIPS[0]), me).wait_recv()
        for cp in first + passed:
            cp.wait_send()
        for cp in mine:
            cp.wait()

    def out_shape(x):
        shape = x.shape[:-1] + (N_DEV * x.shape[-1],) if cols else (N_DEV,) + x.shape
        return jax.ShapeDtypeStruct(shape, x.dtype)

    return pl.pallas_call(
        body, name=name,
        in_specs=[_HBM] * n_arr, out_specs=[_HBM] * n_arr,
        out_shape=[out_shape(x) for x in xs],
        scratch_shapes=[pltpu.SemaphoreType.DMA((7 * n_arr,)), pltpu.SemaphoreType.DMA((7 * n_arr,)),
                        pltpu.SemaphoreType.DMA((n_arr,))],
    )(*xs)


def _exchange(ps, *, name, cols=False):
    n_arr = len(ps)
    blk = ps[0].shape[:-1] + (ps[0].shape[-1] // N_DEV,) if cols else ps[0].shape[1:]

    def body(*refs):
        p_refs, out_ref = refs[:n_arr], refs[n_arr]
        send_sems, recv_sems, local_sems = refs[n_arr + 1:]
        me = (lax.axis_index("x"), lax.axis_index("y"), lax.axis_index("c"))
        me_idx = _dev_index(me)
        own = [pltpu.make_async_copy(_block(p_refs[a], me_idx, cols), out_ref.at[me_idx, a], local_sems.at[a])
               for a in range(n_arr)]
        for cp in own:
            cp.start()
        sends = []
        for k, f in enumerate(_ALL_FLIPS):
            peer = _flip(me, f)
            for a in range(n_arr):
                cp = pltpu.make_async_remote_copy(
                    src_ref=_block(p_refs[a], _dev_index(peer), cols), dst_ref=out_ref.at[me_idx, a],
                    send_sem=send_sems.at[7 * a + k], recv_sem=recv_sems.at[7 * a + k],
                    device_id=peer, device_id_type=pl.DeviceIdType.MESH)
                cp.start()
                sends.append(cp)
        for cp in sends:
            cp.wait()
        for cp in own:
            cp.wait()

    return pl.pallas_call(
        body, name=name,
        in_specs=[_HBM] * n_arr, out_specs=_HBM,
        out_shape=jax.ShapeDtypeStruct((N_DEV, n_arr) + blk, ps[0].dtype),
        scratch_shapes=[pltpu.SemaphoreType.DMA((7 * n_arr,)), pltpu.SemaphoreType.DMA((7 * n_arr,)),
                        pltpu.SemaphoreType.DMA((n_arr,))],
    )(*ps)


_SEM = pl.BlockSpec(memory_space=pltpu.SEMAPHORE)
_ANY = pl.BlockSpec(memory_space=pl.ANY)
_N_PEERS = N_DEV - 1


def _hbm(x):
    return pltpu.with_memory_space_constraint(x, pltpu.HBM)


def _me():
    return lax.axis_index("x"), lax.axis_index("y"), lax.axis_index("c")


def _spread_copies(plan, src_refs, land_refs, send_sems, recv_sems, local_sems):
    local, remote = plan(src_refs, land_refs)
    local = [pltpu.make_async_copy(src, dst, local_sems.at[i]) for i, (src, dst) in enumerate(local)]
    remote = [pltpu.make_async_remote_copy(src_ref=src, dst_ref=dst, send_sem=send_sems.at[k], recv_sem=recv_sems.at[k],
                                           device_id=peer, device_id_type=pl.DeviceIdType.MESH)
              for k, (src, dst, peer) in enumerate(remote)]
    return local, remote


def _spread_start(srcs, lands, plan, n_remote, n_local, *, name):
    ns, nl = len(srcs), len(lands)

    def body(*refs):
        src_refs, land_refs = refs[:ns], refs[ns:ns + nl]
        send_sems, recv_sems, local_sems = refs[ns + nl:ns + nl + 3]
        local, remote = _spread_copies(plan, src_refs, land_refs, send_sems, recv_sems, local_sems)
        for cp in local + remote:
            cp.start()
        token = refs[-1]
        token[...] = jnp.zeros_like(token)

    lands = [_hbm(lax.empty(*x)) if isinstance(x, tuple) else x for x in lands]
    out = pl.pallas_call(
        body, name=name,
        in_specs=[_HBM] * (ns + nl),
        out_specs=[_SEM] * 3 + [_HBM] * (ns + nl) + [pl.BlockSpec(memory_space=pltpu.VMEM)],
        out_shape=[pltpu.SemaphoreType.DMA((n_remote,)), pltpu.SemaphoreType.DMA((n_remote,)),
                   pltpu.SemaphoreType.DMA((max(n_local, 1),))]
        + [pltpu.HBM(x.shape, x.dtype) for x in list(srcs) + lands]
        + [jax.ShapeDtypeStruct((SUBLANES, LANES), F32)],
        input_output_aliases={i: 3 + i for i in range(ns + nl)},
        compiler_params=pltpu.CompilerParams(has_side_effects=pltpu.SideEffectType.DATAFLOW_SIDE_EFFECTING),
    )(*[_hbm(x) for x in srcs], *lands)
    return dict(sems=list(out[:3]), srcs=list(out[3:3 + ns]), lands=list(out[3 + ns:3 + ns + nl]),
                token=out[-1], plan=plan)


def _spread_wait(handle, after, *, name):
    ns, nl = len(handle["srcs"]), len(handle["lands"])

    def body(*refs):
        src_refs, land_refs = refs[:ns], refs[ns:ns + nl]
        send_sems, recv_sems, local_sems = refs[ns + nl:ns + nl + 3]
        local, remote = _spread_copies(handle["plan"], src_refs, land_refs, send_sems, recv_sems, local_sems)
        for cp in local:
            cp.wait()
        for cp in remote:
            cp.wait_send()
            cp.wait_recv()

    out = pl.pallas_call(
        body, name=name,
        in_specs=[_HBM] * (ns + nl) + [_SEM] * 3 + [_ANY],
        out_specs=[_HBM] * (ns + nl),
        out_shape=[pltpu.HBM(x.shape, x.dtype) for x in handle["srcs"] + handle["lands"]],
        input_output_aliases={i: i for i in range(ns + nl)},
        compiler_params=pltpu.CompilerParams(has_side_effects=pltpu.SideEffectType.DATAFLOW_SIDE_EFFECTING),
    )(*handle["srcs"], *handle["lands"], *handle["sems"], after)
    return list(out[ns:])


def _gather_start(x, *, name, cols=False, relayed=True):
    shape = x.shape[:-1] + (N_DEV * x.shape[-1],) if cols else (N_DEV,) + x.shape
    flips = _FLIPS if relayed else _ALL_FLIPS

    def plan(src_refs, land_refs):
        me = _me()
        mine = _block(land_refs[0], _dev_index(me), cols)
        return [(src_refs[0], mine)], [(src_refs[0], mine, _flip(me, f)) for f in flips]

    handle = _spread_start([x], [(shape, x.dtype)], plan, len(flips), 1, name=name)
    handle["cols"] = cols
    return handle


def _gather_relay(handle, after, *, name):
    cols = handle["cols"]
    land, = _spread_wait(handle, after, name=f"{name}_arrived")

    def plan(src_refs, land_refs):
        me = _me()
        blocks = [_block(land_refs[0], _dev_index(_flip(me, f)), cols) for f in _FLIPS[1:]]
        return [], [(blk, blk, _flip(me, _FLIPS[0])) for blk in blocks]

    return _spread_start([], [land], plan, len(_FLIPS) - 1, 0, name=f"{name}_pass")


def _exchange_start(ps, *, name, cols=False):
    blk = ps[0].shape[:-1] + (ps[0].shape[-1] // N_DEV,) if cols else ps[0].shape[1:]

    def plan(src_refs, land_refs):
        me = _me()
        me_idx = _dev_index(me)
        local = [(_block(src, me_idx, cols), land_refs[0].at[me_idx, a]) for a, src in enumerate(src_refs)]
        remote = [(_block(src, _dev_index(_flip(me, f)), cols), land_refs[0].at[me_idx, a], _flip(me, f))
                  for f in _ALL_FLIPS for a, src in enumerate(src_refs)]
        return local, remote

    return _spread_start(ps, [((N_DEV, len(ps)) + blk, ps[0].dtype)], plan, _N_PEERS * len(ps), len(ps), name=name)


def _adamw(parts, w, m, v, *, name, layer=0, prev=None):
    n_rows, c = w.shape
    r = parts.shape[1]
    row_bytes = c * (N_DEV * parts.dtype.itemsize + 7 * 4) * 2
    tr = r
    for cand in (512, 256, 128, 64, 32, 16):
        if r % cand == 0 and cand * row_bytes <= 40 * 1024 * 1024:
            tr = cand
            break
    c1 = 1.0 - ADAM_B1 ** ADAM_STEP
    c2 = 1.0 - ADAM_B2 ** ADAM_STEP

    def body(p_ref, w_ref, m_ref, v_ref, *rest):
        g_ref, d_ref, nm_ref, nv_ref = rest[-4:]
        g = p_ref[0].astype(F32)
        for j in range(1, N_DEV):
            g = g + p_ref[j].astype(F32)
        nm = ADAM_B1 * m_ref[...] + (1.0 - ADAM_B1) * g
        nv = ADAM_B2 * v_ref[...] + (1.0 - ADAM_B2) * (g * g)
        g_ref[...] = g
        nm_ref[...] = nm
        nv_ref[...] = nv
        d_ref[...] = -ADAM_LR * ((nm * (1.0 / c1)) / (jnp.sqrt(nv * (1.0 / c2)) + ADAM_EPS) + ADAM_WD * w_ref[...])

    off = layer * (r // tr)
    blk = pl.BlockSpec((tr, c), lambda i: (i + off, 0))
    prev = list(prev) if prev is not None else []
    return pl.pallas_call(
        body, name=name,
        grid=(r // tr,),
        in_specs=[pl.BlockSpec((N_DEV, tr, c), lambda i: (0, i, 0)), blk, blk, blk] + [_ANY] * len(prev),
        out_specs=[blk] * 4,
        out_shape=[jax.ShapeDtypeStruct((n_rows, c), F32)] * 4,
        input_output_aliases={4 + j: j for j in range(len(prev))},
        compiler_params=_params("parallel"),
    )(parts, w, m, v, *prev)


_TN_CANDS = (512, 256, 128)
_TK_MAX = 5632
_TK_WHOLE_ROWS = 2816


def _contraction_tiles(m, k):
    tk = k
    while tk > _TK_MAX and tk % 2 == 0 and (tk // 2) % LANES == 0:
        tk //= 2
    tm = m if tk <= _TK_WHOLE_ROWS or m % 2 else m // 2
    return tm, tk


def _nn(a, b, name, out_dtype=F32):
    tm, tk = _contraction_tiles(*a.shape)
    return _mm(a, b, "nn", name=name, out_dtype=out_dtype, tm=tm, tn=_pick(b.shape[1], _TN_CANDS), tk=tk)


def _nt(a, b, name, out_dtype=F32):
    tm, tk = _contraction_tiles(*a.shape)
    return _mm(a, b, "nt", name=name, out_dtype=out_dtype, tm=tm, tn=_pick(b.shape[0], _TN_CANDS), tk=tk)


def _tn(a, b, name, out_dtype=BF16, deps=()):
    assert a.shape[0] == b.shape[0], (a.shape, b.shape)
    return _mm_tn(a, b, name=name, out_dtype=out_dtype, tm=_pick(a.shape[1], _TN_CANDS),
                  tn=_pick(b.shape[1], (1024,) + _TN_CANDS), deps=deps)


def _local_step(x, p, target, rep, weight, emit, n_heads, start_tokens=()):
    s, d = x.shape
    depth = p.shape[0]
    grads = {}
    rep_grads = {k: [None] * depth for k in ("mix_pre_g", "mix_post_g", "ffn_pre_g", "ffn_post_g", "ple_norm_g")}

    pending = list(start_tokens)
    gains = {}

    def gain(name, i):
        if (name, i) not in gains:
            gains[name, i] = rep[name][i:i + 1]
        return gains[name, i]

    def send(name, layer, g):
        token = emit(name, layer, g)
        if token is not None:
            pending.append(token)

    def rowcall(*args, **kwargs):
        deps, pending[:] = tuple(pending), []
        return _rowcall(*args, deps=deps, **kwargs)

    deferred = []

    def send_small(i, name, layer, a, b, mm_name):
        if i == 0 and depth > 1:
            deferred.append((name, layer, a, b, mm_name))
        else:
            send(name, layer, _tn(a, b, mm_name))

    saved = []
    h = x
    for i in range(depth):
        sv = {"h": h}
        n1, = rowcall(f"pre_norm{i}", lambda hh, g: _rms(hh, g), [h], [gain("mix_pre_g", i)], [BF16], cols=d)
        sv["n1"] = n1
        if i % 2 == 0:
            proj = _nn(n1, weight("w_in_even", 0, n1), f"in_even{i}")
            a_out = _sb_fwd(proj, n_heads, name=f"sb_fwd{i}")
            b_out, oraw, states = _hg_fwd(proj, rep["hg_lb_logits"], rep["hg_norm_g"], n_heads, name=f"hg_fwd{i}")
            cat = jnp.concatenate([a_out.astype(BF16), b_out], axis=1)
            m = _nn(cat, weight("w_out_even", 0, cat), f"out_even{i}")
            sv.update(proj=proj, oraw=oraw, states=states, cat=cat)
        else:
            proj = _nn(n1, weight("w_in_odd", 0, n1), f"in_odd{i}")
            sm = {k: weight(k, 0, proj) for k in _SMALL}
            xc, ra, ix = _rg_gates_fwd(proj, sm["conv_w"], sm["conv_b"], sm["rg_wa"], sm["rg_ba"],
                                       sm["rg_wx"], sm["rg_bx"], name=f"rg_gates_fwd{i}")
            hs, gact = _rg_scan_fwd(proj, xc, ra, ix, sm["rg_lambda"], name=f"rg_scan_fwd{i}")
            m = _nn(gact, weight("w_out_odd", 0, gact), f"out_odd{i}")
            sv.update(proj=proj, xc=xc, ra=ra, ix=ix, hs=hs, gact=gact, sm=sm)

        def post_mix(hh, mm, g_post, g_pre):
            h1 = hh + _rms(mm, g_post)
            return h1, _rms(h1, g_pre)

        h1, n2 = rowcall(f"post_mix{i}", post_mix, [h, m], [gain("mix_post_g", i), gain("ffn_pre_g", i)],
                          [F32, BF16], cols=d)
        gu = _nn(n2, weight("w_gate_up", i, n2), f"gate_up{i}", out_dtype=BF16)
        act = _swiglu(gu, name=f"swiglu{i}")
        f = _nn(act, weight("w_down", i, act), f"down{i}")

        def post_ffn(hh, ff_out, g_post):
            h2 = hh + _rms(ff_out, g_post)
            return h2, h2

        h2, h2b = rowcall(f"post_ffn{i}", post_ffn, [h1, f], [gain("ffn_post_g", i)], [F32, BF16], cols=d)
        e = _nn(p[i], weight("w_ple_up", i, h2b), f"ple_up{i}")
        gl = _nn(h2b, weight("w_ple_gate", i, h2b), f"ple_gate{i}")
        h3, = rowcall(f"ple{i}", lambda hh, a, b, g: hh + _rms(_sigmoid(a) * b, g), [h2, gl, e],
                       [gain("ple_norm_g", i)], [F32], cols=d)
        sv.update(m=m, h1=h1, n2=n2, gu=gu, act=act, f=f, h2b=h2b, e=e, gl=gl)
        saved.append(sv)
        h = h3

    def loss_fn(y, t):
        err = y - t
        return err * (1.0 / d), jnp.sum(err * err, axis=0, keepdims=True) * (0.5 / d)

    dh, loss_cols = rowcall("loss", loss_fn, [h, target], [], [F32], red_rows=(1,), cols=d)

    for i in reversed(range(depth)):
        sv = saved[i]

        def ple_bwd(dy, a, b, g):
            _, vjp = jax.vjp(lambda a_, b_, g_: _rms(_sigmoid(a_) * b_, g_), a, b, g)
            return vjp(dy)

        dgl, de, rep_grads["ple_norm_g"][i] = rowcall(
            f"ple_bwd{i}", ple_bwd, [dh, sv["gl"], sv["e"]], [gain("ple_norm_g", i)], [BF16, BF16],
            red_rows=(1,), cols=d)
        send_small(i, "w_ple_up", i, p[i], de, f"d_ple_up{i}")
        send_small(i, "w_ple_gate", i, sv["h2b"], dgl, f"d_ple_gate{i}")
        dh2_ple = _nt(dgl, weight("w_ple_gate", i, dgl), f"dx_ple_gate{i}")

        def post_ffn_bwd(dy, dx, ff_out, g):
            dh2 = dy + dx
            _, vjp = jax.vjp(_rms, ff_out, g)
            df, dg = vjp(dh2)
            return dh2, df, dg

        dh2, df, rep_grads["ffn_post_g"][i] = rowcall(
            f"post_ffn_bwd{i}", post_ffn_bwd, [dh, dh2_ple, sv["f"]], [gain("ffn_post_g", i)], [F32, BF16],
            red_rows=(1,), cols=d)
        send("w_down", i, _tn(sv["act"], df, f"d_down{i}"))
        dact = _nt(df, weight("w_down", i, df), f"dx_down{i}", out_dtype=BF16)
        dgu = _swiglu(sv["gu"], dact, name=f"swiglu_bwd{i}")
        send("w_gate_up", i, _tn(sv["n2"], dgu, f"d_gate_up{i}"))
        dn2 = _nt(dgu, weight("w_gate_up", i, dgu), f"dx_gate_up{i}")

        def post_mix_bwd(dy, dn, h1, mm, g_post, g_pre):
            _, vjp_pre = jax.vjp(_rms, h1, g_pre)
            dh1_n, dg_pre = vjp_pre(dn)
            dh1 = dy + dh1_n
            _, vjp_post = jax.vjp(_rms, mm, g_post)
            dm, dg_post = vjp_post(dh1)
            return dh1, dm, dg_pre, dg_post

        dh1, dm, rep_grads["ffn_pre_g"][i], rep_grads["mix_post_g"][i] = rowcall(
            f"post_mix_bwd{i}", post_mix_bwd, [dh2, dn2, sv["h1"], sv["m"]],
            [gain("mix_post_g", i), gain("ffn_pre_g", i)], [F32, BF16], red_rows=(1, 1), cols=d)

        if i % 2 == 0:
            send_small(i, "w_out_even", 0, sv["cat"], dm, f"d_out_even{i}")
            dcat = _nt(dm, weight("w_out_even", 0, dm), f"dx_out_even{i}")
            dq, dk, dv = _sb_bwd(sv["proj"], dcat, n_heads, name=f"sb_bwd{i}")
            dhq, dhf, dhi, dhg, grads["hg_lb_logits"], grads["hg_norm_g"] = _hg_bwd(
                sv["proj"], rep["hg_lb_logits"], rep["hg_norm_g"], sv["oraw"], sv["states"], dcat, n_heads,
                name=f"hg_bwd{i}")
            dproj = jnp.concatenate([dq.astype(BF16), dk.astype(BF16), dv.astype(BF16), dhq, dhf, dhi, dhg], axis=1)
            send("w_in_even", 0, _tn(sv["n1"], dproj, f"d_in_even{i}"))
            dn1 = _nt(dproj, weight("w_in_even", 0, dproj), f"dx_in_even{i}")
        else:
            sm = sv["sm"]
            send_small(i, "w_out_odd", 0, sv["gact"], dm, f"d_out_odd{i}")
            dgo = _nt(dm, weight("w_out_odd", 0, dm), f"dx_out_odd{i}")
            dgate, dra, dix, dxc1, grads["rg_lambda"] = _rg_scan_bwd(
                dgo, sv["proj"], sv["hs"], sv["xc"], sv["ra"], sv["ix"], sm["rg_lambda"], name=f"rg_scan_bwd{i}")
            (dxb, grads["conv_w"], grads["conv_b"], grads["rg_wa"], grads["rg_ba"], grads["rg_wx"],
             grads["rg_bx"]) = _rg_gates_bwd(dra, dix, dxc1, sv["xc"], sv["proj"], sm["conv_w"], sm["rg_wa"],
                                            sm["rg_wx"], name=f"rg_gates_bwd{i}")
            send("small", 0, {k: grads.pop(k) for k in _SMALL})
            dproj = jnp.concatenate([dgate, dxb], axis=1)
            send("w_in_odd", 0, _tn(sv["n1"], dproj, f"d_in_odd{i}"))
            dn1 = _nt(dproj, weight("w_in_odd", 0, dproj), f"dx_in_odd{i}")

        def pre_norm_bwd(dy, dn, hh, g):
            _, vjp = jax.vjp(_rms, hh, g)
            dx, dg = vjp(dn)
            return dy + dx, dg

        dh, rep_grads["mix_pre_g"][i] = rowcall(
            f"pre_norm_bwd{i}", pre_norm_bwd, [dh1, dn1, sv["h"]], [gain("mix_pre_g", i)], [F32],
            red_rows=(1,), cols=d)

    for name, layer, a, b, mm_name in deferred:
        send(name, layer, _tn(a, b, mm_name, deps=(dh,)))
    for k, rows in rep_grads.items():
        grads[k] = jnp.concatenate(rows, axis=0)
    return loss_cols, dh, grads


_WEIGHTS = ("mix_pre_g", "mix_post_g", "ffn_pre_g", "ffn_post_g", "ple_norm_g", "w_in_even", "w_out_even",
            "hg_lb_logits", "hg_norm_g", "w_in_odd", "conv_w", "conv_b", "rg_wa", "rg_ba", "rg_wx", "rg_bx",
            "rg_lambda", "w_out_odd", "w_gate_up", "w_down", "w_ple_up", "w_ple_gate")
_REPLICATED = ("mix_pre_g", "mix_post_g", "ffn_pre_g", "ffn_post_g", "ple_norm_g", "hg_lb_logits", "hg_norm_g")
_SMALL = ("conv_w", "conv_b", "rg_wa", "rg_ba", "rg_wx", "rg_bx", "rg_lambda")
_BIG = {"w_in_even": True, "w_out_even": False, "w_in_odd": True, "w_out_odd": False,
        "w_gate_up": True, "w_down": False, "w_ple_up": True, "w_ple_gate": False}
_PACK_ROW = SUBLANES * LANES


def _pack(arrays):
    flat = jnp.concatenate([a.reshape(-1) for a in arrays])
    pad = -flat.shape[0] % _PACK_ROW
    return jnp.pad(flat, (0, pad)).reshape(-1, LANES)


def _pack_blocks(arrays):
    flat = jnp.concatenate([a.reshape(N_DEV, -1) for a in arrays], axis=1)
    pad = -flat.shape[1] % _PACK_ROW
    return jnp.pad(flat, ((0, 0), (0, pad))).reshape(N_DEV, -1, LANES)


def _unpack(packed, shapes, lead=()):
    flat = packed.reshape(lead + (-1,))
    out, pos = [], 0
    for shape in shapes:
        n = math.prod(shape)
        out.append(flat[..., pos:pos + n].reshape(lead + tuple(shape)))
        pos += n
    return out


def _to_full_small(name, blocks):
    if name == "conv_w":
        return jnp.transpose(blocks, (1, 0, 2)).reshape(blocks.shape[1], -1)
    if name in ("conv_b", "rg_lambda"):
        return blocks.reshape(1, -1)
    nb = blocks.shape[1]
    if name in ("rg_wa", "rg_wx"):
        return jnp.transpose(blocks, (1, 0, 2, 3)).reshape(nb, RG_BLOCK, RG_BLOCK)
    return jnp.transpose(blocks, (1, 0, 2)).reshape(nb, 1, RG_BLOCK)


def _to_blocks_small(name, full):
    if name == "conv_w":
        return jnp.transpose(full.reshape(full.shape[0], N_DEV, -1), (1, 0, 2))
    if name in ("conv_b", "rg_lambda"):
        return full.reshape(N_DEV, -1)
    nb = full.shape[0]
    if name in ("rg_wa", "rg_wx"):
        return jnp.transpose(full.reshape(nb, N_DEV, RG_BLOCK // N_DEV, RG_BLOCK), (1, 0, 2, 3))
    return jnp.transpose(full.reshape(nb, N_DEV, RG_BLOCK // N_DEV), (1, 0, 2))


def _step(inp):
    w = {k: inp[k] for k in _WEIGHTS}
    x, p, target = inp["x"][0], inp["p"][:, 0], inp["loss_target"][0]
    assert w["hg_lb_logits"].shape[0] == 2 and w["w_in_even"].shape[0] == 1 and w["w_in_odd"].shape[0] == 1

    n_heads = w["w_in_even"].shape[2] * N_DEV // (7 * HEAD_DIM)
    small_shapes = [w[k].shape[1:] for k in _SMALL]

    def lands_in_place(name):
        return _BIG[name] and w[name].shape[2] % LANES == 0

    depth = p.shape[0]
    order = [("w_in_even", 0), ("w_out_even", 0)] if depth else []
    for i in range(depth):
        if i == 1:
            order += [("w_in_odd", 0), ("small", 0), ("w_out_odd", 0)]
        order += [("w_gate_up", i), ("w_down", i), ("w_ple_up", i), ("w_ple_gate", i)]
    gathers = {}
    for name, l in order:
        if name == "small":
            gathers[name, l] = _gather_start(_pack([w[k][0] for k in _SMALL]), name="gather_small", relayed=False)
        else:
            gathers[name, l] = _gather_start(w[name][l].astype(BF16), name=f"gather_{name}{l}",
                                             cols=lands_in_place(name))
    ready = {}

    def relay(key, after):
        if "cols" in gathers[key] and key[0] != "small":
            gathers[key] = _gather_relay(gathers[key], after, name=f"gather_{key[0]}{key[1]}")

    def weight(name, layer, after):
        key = ("small", 0) if name in _SMALL else (name, layer)
        if key not in ready:
            relay(key, after)
            for nxt in order[order.index(key) + 1:order.index(key) + 2]:
                relay(nxt, after)
            land, = _spread_wait(gathers[key], after, name=f"gathered_{key[0]}{key[1]}")
            if name in _SMALL:
                ready[key] = {k: _to_full_small(k, b)
                              for k, b in zip(_SMALL, _unpack(land, small_shapes, lead=(N_DEV,)))}
            elif lands_in_place(name):
                ready[key] = land
            elif _BIG[name]:
                ready[key] = jnp.transpose(land, (1, 0, 2)).reshape(land.shape[1], -1)
            else:
                ready[key] = land.reshape(-1, land.shape[2])
        return ready[key][name] if name in _SMALL else ready[key]

    exchanges = []

    def emit(name, layer, g):
        if name == "small":
            handle = _exchange_start([_pack_blocks([_to_blocks_small(k, g[k]) for k in _SMALL])],
                                     name="exchange_small")
            exchanges.append((name, layer, handle))
            return handle["token"]
        _, r, c = w[name].shape
        if lands_in_place(name):
            handle = _exchange_start([g], name=f"exchange_{name}{layer}", cols=True)
        elif _BIG[name]:
            handle = _exchange_start([jnp.transpose(g.reshape(-1, N_DEV, c), (1, 0, 2))],
                                     name=f"exchange_{name}{layer}")
        else:
            handle = _exchange_start([g.reshape(N_DEV, r, c)], name=f"exchange_{name}{layer}")
        exchanges.append((name, layer, handle))
        return handle["token"]

    rep = {k: w[k] for k in _REPLICATED}
    loss_cols, dx, grads = _local_step(x, p, target, rep, weight, emit, n_heads,
                                       [h["token"] for h in gathers.values()])

    loss_part = jnp.sum(loss_cols).reshape(1)
    rep_gather = _gather_start(_pack([grads[k] for k in _REPLICATED] + [loss_part]), name="gather_rep_grads",
                               relayed=False)

    out = {}
    after = dx
    for name, layer, handle in exchanges:
        land, = _spread_wait(handle, after, name=f"exchanged_{name}{layer}")
        if name == "small":
            res = _adamw(land.reshape(N_DEV, -1, LANES), *[_pack([inp[pre + k][0] for k in _SMALL]) for pre in ("", "m_", "v_")],
                         name="adamw_small")
            for k, *vals in zip(_SMALL, *[_unpack(a, small_shapes) for a in res]):
                out[k] = [v[None] for v in vals]
        else:
            n_l, r, c = w[name].shape
            res = out[name] = _adamw(land.reshape(N_DEV, r, c),
                                     *[inp[pre + name].reshape(n_l * r, c) for pre in ("", "m_", "v_")],
                                     name=f"adamw_{name}{layer}", layer=layer, prev=out.get(name))
        after = res[0]
    for name in _BIG:
        out[name] = [a.reshape(w[name].shape) for a in out[name]]

    rep_shapes = [w[k].shape for k in _REPLICATED] + [(1,)]
    rep_parts, = _spread_wait(rep_gather, after, name="gathered_rep_grads")
    res = _adamw(rep_parts, *[_pack([inp[pre + k] for k in _REPLICATED] + [jnp.zeros((1,), F32)])
                              for pre in ("", "m_", "v_")], name="adamw_rep")
    for k, *vals in zip(_REPLICATED + ("loss",), *[_unpack(a, rep_shapes) for a in res]):
        out[k] = vals
    loss = out["loss"][0][0]

    return (loss, dx[None]) + tuple(out[k][j] for j in range(4) for k in _WEIGHTS)


def kernel(x, p, mix_pre_g, mix_post_g, ffn_pre_g, ffn_post_g, ple_norm_g, w_in_even, w_out_even, hg_lb_logits, hg_norm_g, w_in_odd, conv_w, conv_b, rg_wa, rg_ba, rg_wx, rg_bx, rg_lambda, w_out_odd, w_gate_up, w_down, w_ple_up, w_ple_gate, loss_target, m_mix_pre_g, m_mix_post_g, m_ffn_pre_g, m_ffn_post_g, m_ple_norm_g, m_w_in_even, m_w_out_even, m_hg_lb_logits, m_hg_norm_g, m_w_in_odd, m_conv_w, m_conv_b, m_rg_wa, m_rg_ba, m_rg_wx, m_rg_bx, m_rg_lambda, m_w_out_odd, m_w_gate_up, m_w_down, m_w_ple_up, m_w_ple_gate, v_mix_pre_g, v_mix_post_g, v_ffn_pre_g, v_ffn_post_g, v_ple_norm_g, v_w_in_even, v_w_out_even, v_hg_lb_logits, v_hg_norm_g, v_w_in_odd, v_conv_w, v_conv_b, v_rg_wa, v_rg_ba, v_rg_wx, v_rg_bx, v_rg_lambda, v_w_out_odd, v_w_gate_up, v_w_down, v_w_ple_up, v_w_ple_gate):
    return _step(dict(locals()))
```

```python
import functools
import math

import jax
import jax.numpy as jnp
from jax import lax
from jax.experimental import pallas as pl
from jax.experimental.pallas import tpu as pltpu

F32 = jnp.float32
BF16 = jnp.bfloat16

VMEM_LIMIT_BYTES = 56 * 1024 * 1024
LANES = 128
SUBLANES = 8

N_DEV = 8
HEAD_DIM = 128
SB_Q_TILE = 512
SB_K_TILE = 128
HG_CHUNK = 32
HG_HEADS_PER_STEP = 2
RG_BLOCK = 256
CONV_TAPS = 4
RG_C = 8.0
RMS_EPS = 1e-6

ADAM_LR = 0.001
ADAM_B1 = 0.9
ADAM_B2 = 0.999
ADAM_EPS = 1e-08
ADAM_WD = 0.01
ADAM_STEP = 10

MESH_AXES = ("x", "y", "c")


def _params(*sem):
    return pltpu.CompilerParams(dimension_semantics=sem, vmem_limit_bytes=VMEM_LIMIT_BYTES)


def _pick(n, cands):
    for c in cands:
        if c <= n and n % c == 0:
            return c
    return n


def _mm(a, b, mode, *, name, out_dtype=F32, tm=512, tn=512, tk=None):
    if mode == "nn":
        (m, k), (k2, n) = a.shape, b.shape
    else:
        (m, k), (n, k2) = a.shape, b.shape
    assert k == k2, (a.shape, b.shape, mode)
    tm, tn = min(tm, m), min(tn, n)
    tk = k if tk is None else min(tk, k)
    assert m % tm == 0 and n % tn == 0 and k % tk == 0, (m, n, k, tm, tn, tk)
    nk = k // tk

    a_spec = pl.BlockSpec((tm, tk), lambda i, j, kk: (i, kk))
    if mode == "nn":
        b_spec = pl.BlockSpec((tk, tn), lambda i, j, kk: (kk, j))
        dims = (((1,), (0,)), ((), ()))
    else:
        b_spec = pl.BlockSpec((tn, tk), lambda i, j, kk: (j, kk))
        dims = (((1,), (1,)), ((), ()))

    def body(a_ref, b_ref, o_ref, *acc):
        part = lax.dot_general(a_ref[...].astype(BF16), b_ref[...].astype(BF16), dims, preferred_element_type=F32)
        if nk == 1:
            o_ref[...] = part.astype(out_dtype)
        else:
            acc_ref, = acc
            kk = pl.program_id(2)

            @pl.when(kk == 0)
            def _():
                acc_ref[...] = part

            @pl.when(kk > 0)
            def _():
                acc_ref[...] += part

            @pl.when(kk == nk - 1)
            def _():
                o_ref[...] = acc_ref[...].astype(out_dtype)

    return pl.pallas_call(
        body, name=name,
        grid=(m // tm, n // tn, nk),
        in_specs=[a_spec, b_spec],
        out_specs=pl.BlockSpec((tm, tn), lambda i, j, kk: (i, j)),
        out_shape=jax.ShapeDtypeStruct((m, n), out_dtype),
        scratch_shapes=[] if nk == 1 else [pltpu.VMEM((tm, tn), F32)],
        compiler_params=_params("parallel", "parallel", "arbitrary"),
    )(a, b)


def _mm_tn(a, b, *, name, out_dtype, tm, tn, deps=()):
    k, m = a.shape
    n = b.shape[1]

    def body(a_ref, b_ref, *refs):
        o_ref, at_ref = refs[len(deps):]

        @pl.when(pl.program_id(1) == 0)
        def _():
            at_ref[...] = a_ref[...].astype(F32).T.astype(BF16)

        o_ref[...] = jnp.dot(at_ref[...], b_ref[...].astype(BF16), preferred_element_type=F32).astype(out_dtype)

    return pl.pallas_call(
        body, name=name,
        grid=(m // tm, n // tn),
        in_specs=[pl.BlockSpec((k, tm), lambda i, j: (0, i)), pl.BlockSpec((k, tn), lambda i, j: (0, j))]
        + [pl.BlockSpec(memory_space=pl.ANY)] * len(deps),
        out_specs=pl.BlockSpec((tm, tn), lambda i, j: (i, j)),
        out_shape=jax.ShapeDtypeStruct((m, n), out_dtype),
        scratch_shapes=[pltpu.VMEM((tm, k), BF16)],
        compiler_params=_params("parallel", "arbitrary"),
    )(a, b, *deps)


def _rowcall(name, fn, rows, pars, row_outs, red_rows=(), *, cols, ts=256, tc=None, deps=()):
    rows = [r if isinstance(r, tuple) else (r, 0) for r in rows]
    pars = [p if isinstance(p, tuple) else (p, 0) for p in pars]
    s = rows[0][0].shape[0]
    tc = cols if tc is None else tc
    ts = min(ts, s)
    assert s % ts == 0 and cols % tc == 0, (name, s, ts, cols, tc)
    n_in, n_row_out = len(rows) + len(pars), len(row_outs)

    def body(*refs):
        outs = fn(*[r[...] for r in refs[:n_in]])
        outs = outs if isinstance(outs, (tuple, list)) else (outs,)
        o_refs = refs[n_in + len(deps):]
        for o_ref, val in zip(o_refs[:n_row_out], outs[:n_row_out]):
            o_ref[...] = val.astype(o_ref.dtype)
        first = pl.program_id(1) == 0
        for o_ref, val in zip(o_refs[n_row_out:], outs[n_row_out:]):
            @pl.when(first)
            def _(o_ref=o_ref, val=val):
                o_ref[...] = val

            @pl.when(jnp.logical_not(first))
            def _(o_ref=o_ref, val=val):
                o_ref[...] += val

    def row_map(off):
        return lambda j, i: (i, j + off)

    def par_map(off):
        return lambda j, i: (0, j + off)

    return pl.pallas_call(
        body, name=name,
        grid=(cols // tc, s // ts),
        in_specs=[pl.BlockSpec((ts, tc), row_map(off)) for _, off in rows]
        + [pl.BlockSpec((p.shape[0], tc), par_map(off)) for p, off in pars]
        + [pl.BlockSpec(memory_space=pl.ANY)] * len(deps),
        out_specs=[pl.BlockSpec((ts, tc), lambda j, i: (i, j)) for _ in row_outs]
        + [pl.BlockSpec((r, tc), lambda j, i: (0, j)) for r in red_rows],
        out_shape=[jax.ShapeDtypeStruct((s, cols), dt) for dt in row_outs]
        + [jax.ShapeDtypeStruct((r, cols), F32) for r in red_rows],
        compiler_params=_params("parallel", "arbitrary"),
    )(*[r for r, _ in rows], *[p for p, _ in pars], *deps)


def _swiglu_act(g, u):
    return _silu(g) * u


def _gate_up(n, w, *, name):
    s, d = n.shape
    f = w.shape[1] // 2
    tn = _pick(f, (256, 128))
    nj = f // tn

    def body(a_ref, wg_ref, wu_ref, g_ref, u_ref, act_ref):
        a = a_ref[...].astype(BF16)
        g = jnp.dot(a, wg_ref[...].astype(BF16), preferred_element_type=F32)
        u = jnp.dot(a, wu_ref[...].astype(BF16), preferred_element_type=F32)
        g_ref[...] = g.astype(BF16)
        u_ref[...] = u.astype(BF16)
        act_ref[...] = _swiglu_act(g, u).astype(BF16)

    out = pl.BlockSpec((s, tn), lambda j: (0, j))
    return pl.pallas_call(
        body, name=name,
        grid=(nj,),
        in_specs=[pl.BlockSpec((s, d), lambda j: (0, 0)), pl.BlockSpec((d, tn), lambda j: (0, j)),
                  pl.BlockSpec((d, tn), lambda j: (0, nj + j))],
        out_specs=[out] * 3,
        out_shape=[jax.ShapeDtypeStruct((s, f), BF16)] * 3,
        compiler_params=_params("parallel"),
    )(n, w, w)


def _swiglu_bwd(g, u, dact, *, name, ts=128):
    s, f = g.shape
    ts = min(ts, s)

    def body(g_ref, u_ref, dact_ref, o_ref):
        _, vjp = jax.vjp(_swiglu_act, g_ref[...].astype(F32), u_ref[...].astype(F32))
        dg, du = vjp(dact_ref[...].astype(F32))
        o_ref[:, 0:f] = dg.astype(BF16)
        o_ref[:, f:2 * f] = du.astype(BF16)

    narrow = pl.BlockSpec((ts, f), lambda i: (i, 0))
    return pl.pallas_call(
        body, name=name,
        grid=(s // ts,),
        in_specs=[narrow] * 3,
        out_specs=pl.BlockSpec((ts, 2 * f), lambda i: (i, 0)),
        out_shape=jax.ShapeDtypeStruct((s, 2 * f), BF16),
        compiler_params=_params("parallel"),
    )(g, u, dact)


def _rms(x, g):
    return x * lax.rsqrt(jnp.mean(x * x, axis=-1, keepdims=True) + RMS_EPS) * g


def _sigmoid(x):
    return jax.nn.sigmoid(x)


def _silu(x):
    return x * jax.nn.sigmoid(x)


def _gelu(x):
    return 0.5 * x * (1.0 + jnp.tanh(math.sqrt(2.0 / math.pi) * (x + 0.044715 * (x * x * x))))


def _softplus(x):
    return jnp.maximum(x, 0.0) + jnp.log1p(jnp.exp(-jnp.abs(x)))


def _split(x, terms):
    parts = []
    for _ in range(terms - 1):
        parts.append(x.astype(BF16))
        x = x - parts[-1].astype(F32)
    return parts + [x.astype(BF16)]


def _xdot(x, t, terms=3):
    return sum(jnp.dot(p, t, preferred_element_type=F32) for p in _split(x, terms))


def _xdot_l(t, x):
    return sum(jnp.dot(t, p, preferred_element_type=F32) for p in _split(x, 3))


_NT = (((1,), (1,)), ((), ()))
_TN = (((0,), (0,)), ((), ()))


def _dot(a, b, dims=None):
    if dims is None:
        return jnp.dot(a.astype(BF16), b.astype(BF16), preferred_element_type=F32)
    return lax.dot_general(a.astype(BF16), b.astype(BF16), dims, preferred_element_type=F32)


def _iota(shape, axis):
    return lax.broadcasted_iota(jnp.int32, shape, axis)


def _sb_tile(qb, kblk, mask, upper, c_rem):
    z = lax.dot_general(qb, kblk, _NT, preferred_element_type=F32)
    soft = jnp.log1p(jnp.exp(-jnp.abs(z)))
    lbeta = jnp.minimum(z, 0.0) - soft
    l1m = -jnp.maximum(z, 0.0) - soft
    if mask is not None:
        l1m = jnp.where(mask, l1m, 0.0)
    rem = _xdot(l1m, upper, terms=2) + c_rem
    w = jnp.exp(lbeta + rem)
    if mask is not None:
        w = jnp.where(mask, w, 0.0)
    return lbeta, l1m, w


def _sb_tiles(s):
    tq = min(SB_Q_TILE, s)
    return tq, SB_K_TILE, tq // SB_K_TILE


def _sb_key_loops(qi, per_q, step, carry):
    n_full = qi * per_q
    carry = lax.fori_loop(0, per_q, lambda j, c: step(n_full + per_q - 1 - j, True, c), carry)
    return lax.fori_loop(0, n_full, lambda j, c: step(n_full - 1 - j, False, c), carry)


def _sb_fwd(proj, n_heads, *, name):
    s = proj.shape[0]
    t, tk, per_q = _sb_tiles(s)
    scale = HEAD_DIM ** -0.5

    def body(q_ref, k_ref, v_ref, o_ref):
        qi = pl.program_id(1)
        qb = (q_ref[...] * scale).astype(BF16)
        row, col = _iota((t, tk), 0) + qi * t, _iota((t, tk), 1)
        upper = (_iota((tk, tk), 0) > _iota((tk, tk), 1)).astype(BF16)

        def step(kb, masked, carry):
            acc, c_rem = carry
            rows = pl.ds(pl.multiple_of(kb * tk, tk), tk)
            kblk = k_ref[rows, :].astype(BF16)
            vblk = v_ref[rows, :].astype(BF16)
            _, l1m, w = _sb_tile(qb, kblk, (col + kb * tk) < row if masked else None, upper, c_rem)
            acc = acc + jnp.dot(w.astype(BF16), vblk, preferred_element_type=F32)
            return acc, c_rem + jnp.sum(l1m, axis=1, keepdims=True)

        acc, _ = _sb_key_loops(qi, per_q, step, (jnp.zeros((t, HEAD_DIM), F32), jnp.zeros((t, 1), F32)))
        o_ref[...] = acc

    return pl.pallas_call(
        body, name=name,
        grid=(n_heads, s // t),
        in_specs=[pl.BlockSpec((t, HEAD_DIM), lambda h, i: (i, h)),
                  pl.BlockSpec((s, HEAD_DIM), lambda h, i: (0, n_heads + h)),
                  pl.BlockSpec((s, HEAD_DIM), lambda h, i: (0, 2 * n_heads + h))],
        out_specs=pl.BlockSpec((t, HEAD_DIM), lambda h, i: (i, h)),
        out_shape=jax.ShapeDtypeStruct((s, n_heads * HEAD_DIM), F32),
        compiler_params=_params("parallel", "arbitrary"),
    )(proj, proj, proj)


def _sb_bwd(proj, dcat, n_heads, *, name):
    s = proj.shape[0]
    t, tk, per_q = _sb_tiles(s)
    scale = HEAD_DIM ** -0.5

    def body(q_ref, k_ref, v_ref, do_ref, dq_ref, dk_ref, dv_ref, g_s, sig_s):
        qi = pl.program_id(1)

        @pl.when(qi == 0)
        def _():
            dk_ref[...] = jnp.zeros_like(dk_ref)
            dv_ref[...] = jnp.zeros_like(dv_ref)

        qb = (q_ref[...] * scale).astype(BF16)
        dob = do_ref[...].astype(BF16)
        row, col = _iota((t, tk), 0) + qi * t, _iota((t, tk), 1)
        upper = (_iota((tk, tk), 0) > _iota((tk, tk), 1)).astype(BF16)
        lower_incl = (_iota((tk, tk), 0) >= _iota((tk, tk), 1)).astype(BF16)

        def weights(kb, masked, carry):
            c_rem, g_all = carry
            rows = pl.ds(pl.multiple_of(kb * tk, tk), tk)
            kblk = k_ref[rows, :].astype(BF16)
            vblk = v_ref[rows, :].astype(BF16)
            lbeta, l1m, w = _sb_tile(qb, kblk, (col + kb * tk) < row if masked else None, upper, c_rem)
            g = w * lax.dot_general(dob, vblk, _NT, preferred_element_type=F32)
            dv_ref[rows, :] += lax.dot_general(w.astype(BF16), dob, _TN, preferred_element_type=F32)
            g_s[kb] = g
            sig_s[kb] = jnp.exp(lbeta)
            return c_rem + jnp.sum(l1m, axis=1, keepdims=True), g_all + jnp.sum(g, axis=1, keepdims=True)

        zero_col = jnp.zeros((t, 1), F32)
        _, g_all = _sb_key_loops(qi, per_q, weights, (zero_col, zero_col))

        def scores(kb, masked, carry):
            dq, c_g = carry
            rows = pl.ds(pl.multiple_of(kb * tk, tk), tk)
            g, sig = g_s[kb], sig_s[kb]
            g_before = g_all - (_xdot(g, lower_incl) + c_g)
            dz = g * (1.0 - sig) - g_before * sig
            if masked:
                dz = jnp.where((col + kb * tk) < row, dz, 0.0)
            dz = dz.astype(BF16)
            dq = dq + jnp.dot(dz, k_ref[rows, :].astype(BF16), preferred_element_type=F32)
            dk_ref[rows, :] += lax.dot_general(dz, qb, _TN, preferred_element_type=F32)
            return dq, c_g + jnp.sum(g, axis=1, keepdims=True)

        dq, _ = _sb_key_loops(qi, per_q, scores, (jnp.zeros((t, HEAD_DIM), F32), zero_col))
        dq_ref[...] = dq * scale

    width = n_heads * HEAD_DIM
    return pl.pallas_call(
        body, name=name,
        grid=(n_heads, s // t),
        in_specs=[pl.BlockSpec((t, HEAD_DIM), lambda h, i: (i, h)),
                  pl.BlockSpec((s, HEAD_DIM), lambda h, i: (0, n_heads + h)),
                  pl.BlockSpec((s, HEAD_DIM), lambda h, i: (0, 2 * n_heads + h)),
                  pl.BlockSpec((t, HEAD_DIM), lambda h, i: (i, h))],
        out_specs=[pl.BlockSpec((t, HEAD_DIM), lambda h, i: (i, h)),
                   pl.BlockSpec((s, HEAD_DIM), lambda h, i: (0, h)),
                   pl.BlockSpec((s, HEAD_DIM), lambda h, i: (0, h))],
        out_shape=[jax.ShapeDtypeStruct((s, width), F32)] * 3,
        scratch_shapes=[pltpu.VMEM((s // tk, t, tk), F32)] * 2,
        compiler_params=_params("parallel", "arbitrary"),
    )(proj, proj, proj, dcat)


def _hg_pre(hq, hf, logits):
    mx = jnp.max(logits, axis=0, keepdims=True)
    ex = jnp.exp(logits - mx)
    lb = ex[0:1, :] / jnp.sum(ex, axis=0, keepdims=True)
    f = lb + (1.0 - lb) * _sigmoid(hf)
    return _silu(hq), 1.0 - f, jnp.log(f)


def _hg_post(o, norm_g, hgate):
    return _rms(o, norm_g) * _silu(hgate)


def _hg_specs(s, n_heads, first_block):
    def at(group):
        return pl.BlockSpec((s, HEAD_DIM), lambda h: (0, first_block + group * n_heads + h))
    return [at(0), at(1), at(2), at(3)]


def _hg_fwd(proj, logits, norm_g, n_heads, *, name):
    s = proj.shape[0]
    hc = HG_CHUNK
    n_chunks = s // hc
    d = HEAD_DIM

    hp = HG_HEADS_PER_STEP
    assert n_heads % hp == 0
    wide = hp * d

    def body(lg_ref, ng_ref, hq_ref, hf_ref, hi_ref, hgt_ref, out_ref, oraw_ref, st_ref,
             q_s, k_s, lf_s, cum_s, qc_s, oc_s):
        q, k, lf = _hg_pre(hq_ref[...], hf_ref[...], lg_ref[...])
        q_s[...] = q
        k_s[...] = k
        lf_s[...] = lf
        tril = (_iota((hc, hc), 0) >= _iota((hc, hc), 1)).astype(BF16)
        srow = _iota((hc, d), 0)

        def head_chunk(j, ci, rows, st):
            ln = slice(j * d, (j + 1) * d)
            q, k, v = q_s[rows, ln], k_s[rows, ln], hi_ref[rows, ln]
            cum = _xdot_l(tril, lf_s[rows, ln])
            st_ref[j, ci] = st
            o_inter = _dot(q * jnp.exp(cum), st, _NT)
            cum_s[:, ln] = cum
            qc_s[:, ln] = q
            for t in range(hc):
                ng = (t // SUBLANES + 1) * SUBLANES
                e = jnp.where(srow[:ng] <= t, jnp.exp(cum_s[t:t + 1, ln] - cum[:ng]), 0.0)
                sc = jnp.sum(qc_s[t:t + 1, ln] * k[:ng] * e, axis=1, keepdims=True)
                oc_s[t:t + 1, ln] = jnp.sum(sc * v[:ng], axis=0, keepdims=True)
            oraw_ref[rows, ln] = o_inter + oc_s[:, ln]
            last = cum_s[hc - 1:hc, ln]
            return st * jnp.exp(last) + _dot(v, k * jnp.exp(last - cum), _TN)

        def chunk(ci, states):
            rows = pl.ds(pl.multiple_of(ci * hc, hc), hc)
            return tuple(head_chunk(j, ci, rows, st) for j, st in enumerate(states))

        lax.fori_loop(0, n_chunks, chunk, tuple(jnp.zeros((d, d), F32) for _ in range(hp)))
        for j in range(hp):
            ln = slice(j * d, (j + 1) * d)
            out_ref[:, ln] = _hg_post(oraw_ref[:, ln], ng_ref[...], hgt_ref[:, ln]).astype(BF16)

    width = n_heads * d
    first = 3 * n_heads // hp
    groups = [pl.BlockSpec((s, wide), functools.partial(lambda h, g: (0, first + g * (n_heads // hp) + h), g=g))
              for g in range(4)]
    head_block = pl.BlockSpec((s, wide), lambda h: (0, h))
    return pl.pallas_call(
        body, name=name,
        grid=(n_heads // hp,),
        in_specs=[pl.BlockSpec((2, wide), lambda h: (0, h)), pl.BlockSpec((1, d), lambda h: (0, 0))] + groups,
        out_specs=[head_block, head_block, pl.BlockSpec((hp, n_chunks, d, d), lambda h: (h, 0, 0, 0))],
        out_shape=[jax.ShapeDtypeStruct((s, width), BF16), jax.ShapeDtypeStruct((s, width), F32),
                   jax.ShapeDtypeStruct((n_heads, n_chunks, d, d), F32)],
        scratch_shapes=[pltpu.VMEM((s, wide), F32)] * 3 + [pltpu.VMEM((hc, wide), F32)] * 3,
        compiler_params=_params("arbitrary"),
    )(logits, norm_g, proj, proj, proj, proj)


def _hg_bwd(proj, logits, norm_g, oraw, states, dcat, n_heads, *, name):
    s = proj.shape[0]
    hc = HG_CHUNK
    n_chunks = s // hc
    d = HEAD_DIM

    def body(lg_ref, ng_ref, hq_ref, hf_ref, hi_ref, hgt_ref, oraw_ref, st_ref, dout_ref,
             dhq_ref, dhf_ref, dhi_ref, dhgt_ref, dlg_ref, dng_ref,
             q_s, k_s, lf_s, do_s, dq_s, dk_s, dlf_s, cum_s, qc_s, doc_s, dqc_s, dkc_s, dvc_s):
        head = pl.program_id(0)
        (q, k, lf), pre_vjp = jax.vjp(_hg_pre, hq_ref[...], hf_ref[...], lg_ref[...])
        q_s[...] = q
        k_s[...] = k
        lf_s[...] = lf
        _, post_vjp = jax.vjp(_hg_post, oraw_ref[...], ng_ref[...], hgt_ref[...])
        do, dng, dhgt = post_vjp(dout_ref[...])
        do_s[...] = do
        dhgt_ref[...] = dhgt.astype(BF16)

        @pl.when(head == 0)
        def _():
            dng_ref[...] = dng

        @pl.when(head > 0)
        def _():
            dng_ref[...] += dng

        triu = (_iota((hc, hc), 0) <= _iota((hc, hc), 1)).astype(BF16)
        tril = (_iota((hc, hc), 0) >= _iota((hc, hc), 1)).astype(BF16)
        srow = _iota((hc, d), 0)

        def chunk(j, dst):
            ci = n_chunks - 1 - j
            rows = pl.ds(pl.multiple_of(ci * hc, hc), hc)
            q, k, v, do_c = q_s[rows, :], k_s[rows, :], hi_ref[rows, :], do_s[rows, :]
            cum = _xdot_l(tril, lf_s[rows, :])
            st = st_ref[0, ci]
            cum_s[...] = cum
            qc_s[...] = q
            doc_s[...] = do_c
            last = cum_s[hc - 1:hc, :]
            e_cum, e_last = jnp.exp(cum), jnp.exp(last - cum)
            dqc_s[...] = _dot(do_c, st) * e_cum
            dk_state = _dot(v, dst) * e_last
            dkc_s[...] = dk_state
            dvc_s[...] = _dot(k * e_last, dst, _NT)
            d_last = (jnp.sum(dst * st, axis=0, keepdims=True) * jnp.exp(last)
                      + jnp.sum(k * dk_state, axis=0, keepdims=True))
            for t in range(hc):
                ng = (t // SUBLANES + 1) * SUBLANES
                qt, dot_ = qc_s[t:t + 1, :], doc_s[t:t + 1, :]
                e = jnp.where(srow[:ng] <= t, jnp.exp(cum_s[t:t + 1, :] - cum[:ng]), 0.0)
                ke = k[:ng] * e
                d_a = jnp.sum(dot_ * v[:ng], axis=1, keepdims=True)
                dqc_s[t:t + 1, :] += jnp.sum(d_a * ke, axis=0, keepdims=True)
                dkc_s[0:ng, :] += d_a * (qt * e)
                dvc_s[0:ng, :] += jnp.sum(qt * ke, axis=1, keepdims=True) * dot_
            dq, dk = dqc_s[...], dkc_s[...]
            d_b = q * dq - k * dk
            dq_s[rows, :] = dq
            dk_s[rows, :] = dk
            dhi_ref[rows, :] = dvc_s[...].astype(BF16)
            dlf_s[rows, :] = _xdot_l(triu, d_b) + d_last
            return dst * jnp.exp(last) + _dot(do_c, q * e_cum, _TN)

        lax.fori_loop(0, n_chunks, chunk, jnp.zeros((d, d), F32))
        dhq, dhf, dlg = pre_vjp((dq_s[...], dk_s[...], dlf_s[...]))
        dhq_ref[...] = dhq.astype(BF16)
        dhf_ref[...] = dhf.astype(BF16)
        dlg_ref[...] = dlg

    width = n_heads * d
    head_block = pl.BlockSpec((s, d), lambda h: (0, h))
    return pl.pallas_call(
        body, name=name,
        grid=(n_heads,),
        in_specs=[pl.BlockSpec((2, d), lambda h: (0, h)), pl.BlockSpec((1, d), lambda h: (0, 0))]
        + _hg_specs(s, n_heads, 3 * n_heads)
        + [head_block, pl.BlockSpec((1, n_chunks, d, d), lambda h: (h, 0, 0, 0)),
           pl.BlockSpec((s, d), lambda h: (0, n_heads + h))],
        out_specs=[head_block] * 4 + [pl.BlockSpec((2, d), lambda h: (0, h)), pl.BlockSpec((1, d), lambda h: (0, 0))],
        out_shape=[jax.ShapeDtypeStruct((s, width), BF16)] * 4
        + [jax.ShapeDtypeStruct((2, width), F32), jax.ShapeDtypeStruct((1, d), F32)],
        scratch_shapes=[pltpu.VMEM((s, d), F32)] * 7 + [pltpu.VMEM((hc, d), F32)] * 6,
        compiler_params=_params("arbitrary"),
    )(logits, norm_g, proj, proj, proj, proj, oraw, states, dcat)


def _shift_down(x, n, srow):
    if n == 0:
        return x
    return jnp.where(srow >= n, pltpu.roll(x, n, 0), 0.0)


def _shift_up(x, n, srow):
    if n == 0:
        return x
    s = x.shape[0]
    return jnp.where(srow < s - n, pltpu.roll(x, s - n, 0), 0.0)


def _rg_gates_fwd(proj, conv_w, conv_b, wa, ba, wx, bx, *, name):
    s = proj.shape[0]
    nb = wa.shape[0]
    bw = RG_BLOCK

    def body(xb_ref, cw_ref, cb_ref, wa_ref, ba_ref, wx_ref, bx_ref, xc_ref, ra_ref, ix_ref):
        x = xb_ref[...]
        srow = _iota((s, bw), 0)
        cw = cw_ref[...]
        xc = cb_ref[...] + cw[0:1, :] * x
        for tap in range(1, CONV_TAPS):
            xc = xc + cw[tap:tap + 1, :] * _shift_down(x, tap, srow)
        xc_ref[...] = xc
        ra_ref[...] = _dot(xc, wa_ref[0]) + ba_ref[0]
        ix_ref[...] = _dot(xc, wx_ref[0]) + bx_ref[0]

    col = pl.BlockSpec((s, bw), lambda n: (0, n))
    vec = lambda r: pl.BlockSpec((r, bw), lambda n: (0, n))
    mat = pl.BlockSpec((1, bw, bw), lambda n: (n, 0, 0))
    bias = pl.BlockSpec((1, 1, bw), lambda n: (n, 0, 0))
    return pl.pallas_call(
        body, name=name,
        grid=(nb,),
        in_specs=[pl.BlockSpec((s, bw), lambda n: (0, nb + n)), vec(CONV_TAPS), vec(1), mat, bias, mat, bias],
        out_specs=[col] * 3,
        out_shape=[jax.ShapeDtypeStruct((s, nb * bw), F32)] * 3,
        compiler_params=_params("parallel"),
    )(proj, conv_w, conv_b, wa, ba, wx, bx)


def _rg_au(ra, ix, xc, lam, first_row):
    log_a = -RG_C * _sigmoid(ra) * _softplus(-lam)
    th = jnp.tanh(log_a)
    one_minus_a2 = -2.0 * th / (1.0 - th)
    mult = jnp.where(first_row, 1.0, jnp.sqrt(one_minus_a2))
    return jnp.exp(log_a), xc * _sigmoid(ix) * mult


def _rg_out(gate, hs):
    return _gelu(gate) * hs


def _linear_scan(a, b, a_s, b_s, in_s, reverse):
    s, c = a.shape
    within = _iota((s, c), 0) & (SUBLANES - 1)
    shift = 1
    while shift < SUBLANES:
        if reverse:
            take = within < SUBLANES - shift
            a_n, b_n = pltpu.roll(a, s - shift, 0), pltpu.roll(b, s - shift, 0)
        else:
            take = within >= shift
            a_n, b_n = pltpu.roll(a, shift, 0), pltpu.roll(b, shift, 0)
        b = jnp.where(take, a * b_n + b, b)
        a = jnp.where(take, a * a_n, a)
        shift *= 2
    a_s[...] = a
    b_s[...] = b
    n_tiles = s // SUBLANES
    edge = 0 if reverse else SUBLANES - 1

    def tile(i, h):
        rows = pl.ds(pl.multiple_of(((n_tiles - 1 - i) if reverse else i) * SUBLANES, SUBLANES), SUBLANES)
        in_s[rows, :] = jnp.broadcast_to(h, (SUBLANES, c))
        return a_s[rows, :][edge:edge + 1, :] * h + b_s[rows, :][edge:edge + 1, :]

    lax.fori_loop(0, n_tiles, tile, jnp.zeros((1, c), F32))
    return a * in_s[...] + b


def _rg_scan_fwd(proj, xc, ra, ix, lam, *, name):
    s, width = xc.shape
    tc = LANES

    def body(gate_ref, xc_ref, ra_ref, ix_ref, lam_ref, hs_ref, gact_ref, a_s, u_s, in_s):
        first_row = _iota((s, tc), 0) == 0
        a, u = _rg_au(ra_ref[...], ix_ref[...], xc_ref[...], lam_ref[...], first_row)
        hs = _linear_scan(a, u, a_s, u_s, in_s, reverse=False)
        hs_ref[...] = hs
        gact_ref[...] = _rg_out(gate_ref[...], hs).astype(BF16)

    col = pl.BlockSpec((s, tc), lambda n: (0, n))
    return pl.pallas_call(
        body, name=name,
        grid=(width // tc,),
        in_specs=[col, col, col, col, pl.BlockSpec((1, tc), lambda n: (0, n))],
        out_specs=[col, col],
        out_shape=[jax.ShapeDtypeStruct((s, width), F32), jax.ShapeDtypeStruct((s, width), BF16)],
        scratch_shapes=[pltpu.VMEM((s, tc), F32)] * 3,
        compiler_params=_params("parallel"),
    )(proj, xc, ra, ix, lam)


def _rg_scan_bwd(dgo, proj, hs, xc, ra, ix, lam, *, name):
    s, width = xc.shape
    tc = LANES

    def body(dgo_ref, gate_ref, hs_ref, xc_ref, ra_ref, ix_ref, lam_ref,
             dgate_ref, dra_ref, dix_ref, dxc_ref, dlam_ref, a_s, dh_s, g_s):
        srow = _iota((s, tc), 0)
        hs = hs_ref[...]
        _, out_vjp = jax.vjp(_rg_out, gate_ref[...], hs)
        dgate, dh = out_vjp(dgo_ref[...])
        dgate_ref[...] = dgate.astype(BF16)
        au = functools.partial(_rg_au, first_row=srow == 0)
        (a, _), au_vjp = jax.vjp(au, ra_ref[...], ix_ref[...], xc_ref[...], lam_ref[...])
        g = _linear_scan(_shift_up(a, 1, srow), dh, a_s, dh_s, g_s, reverse=True)
        dra, dix, dxc, dlam = au_vjp((g * _shift_down(hs, 1, srow), g))
        dra_ref[...] = dra.astype(BF16)
        dix_ref[...] = dix.astype(BF16)
        dxc_ref[...] = dxc
        dlam_ref[...] = dlam

    col = pl.BlockSpec((s, tc), lambda n: (0, n))
    vec = pl.BlockSpec((1, tc), lambda n: (0, n))
    return pl.pallas_call(
        body, name=name,
        grid=(width // tc,),
        in_specs=[col] * 6 + [vec],
        out_specs=[col] * 4 + [vec],
        out_shape=[jax.ShapeDtypeStruct((s, width), BF16)] * 3
        + [jax.ShapeDtypeStruct((s, width), F32), jax.ShapeDtypeStruct((1, width), F32)],
        scratch_shapes=[pltpu.VMEM((s, tc), F32)] * 3,
        compiler_params=_params("parallel"),
    )(dgo, proj, hs, xc, ra, ix, lam)


def _rg_gates_bwd(dra, dix, dxc1, xc, proj, conv_w, wa, wx, *, name):
    s = proj.shape[0]
    nb = wa.shape[0]
    bw = RG_BLOCK

    def body(dra_ref, dix_ref, dxc_ref, xc_ref, xb_ref, cw_ref, wa_ref, wx_ref,
             dxb_ref, dcw_ref, dcb_ref, dwa_ref, dba_ref, dwx_ref, dbx_ref):
        dra, dix = dra_ref[...], dix_ref[...]
        xc_t = xc_ref[...].T.astype(BF16)
        dwa_ref[0] = jnp.dot(xc_t, dra, preferred_element_type=F32)
        dwx_ref[0] = jnp.dot(xc_t, dix, preferred_element_type=F32)
        dba_ref[0] = jnp.sum(dra.astype(F32), axis=0, keepdims=True)
        dbx_ref[0] = jnp.sum(dix.astype(F32), axis=0, keepdims=True)
        dxc = dxc_ref[...] + _dot(dra, wa_ref[0], _NT) + _dot(dix, wx_ref[0], _NT)
        srow = _iota((s, bw), 0)
        x = xb_ref[...]
        cw = cw_ref[...]
        dx = cw[0:1, :] * dxc
        dcw = [jnp.sum(dxc * x, axis=0, keepdims=True)]
        for tap in range(1, CONV_TAPS):
            dx = dx + cw[tap:tap + 1, :] * _shift_up(dxc, tap, srow)
            dcw.append(jnp.sum(dxc * _shift_down(x, tap, srow), axis=0, keepdims=True))
        dxb_ref[...] = dx.astype(BF16)
        r4 = _iota((CONV_TAPS, bw), 0)
        acc = jnp.zeros((CONV_TAPS, bw), F32)
        for tap in range(CONV_TAPS):
            acc = jnp.where(r4 == tap, dcw[tap], acc)
        dcw_ref[...] = acc
        dcb_ref[...] = jnp.sum(dxc, axis=0, keepdims=True)

    col = pl.BlockSpec((s, bw), lambda n: (0, n))
    vec = lambda r: pl.BlockSpec((r, bw), lambda n: (0, n))
    mat = pl.BlockSpec((1, bw, bw), lambda n: (n, 0, 0))
    bias = pl.BlockSpec((1, 1, bw), lambda n: (n, 0, 0))
    width = nb * bw
    return pl.pallas_call(
        body, name=name,
        grid=(nb,),
        in_specs=[col, col, col, col, pl.BlockSpec((s, bw), lambda n: (0, nb + n)), vec(CONV_TAPS), mat, mat],
        out_specs=[col, vec(CONV_TAPS), vec(1), mat, bias, mat, bias],
        out_shape=[jax.ShapeDtypeStruct((s, width), BF16), jax.ShapeDtypeStruct((CONV_TAPS, width), F32),
                   jax.ShapeDtypeStruct((1, width), F32), jax.ShapeDtypeStruct((nb, bw, bw), F32),
                   jax.ShapeDtypeStruct((nb, 1, bw), F32), jax.ShapeDtypeStruct((nb, bw, bw), F32),
                   jax.ShapeDtypeStruct((nb, 1, bw), F32)],
        compiler_params=_params("parallel"),
    )(dra, dix, dxc1, xc, proj, conv_w, wa, wx)


_HBM = pl.BlockSpec(memory_space=pltpu.HBM)
_FLIPS = ((0, 0, 1), (1, 0, 0), (0, 1, 0), (1, 1, 0))
_ALL_FLIPS = tuple((a, b, c) for a in (0, 1) for b in (0, 1) for c in (0, 1))[1:]


def _flip(pos, f):
    return tuple(1 - p if b else p for p, b in zip(pos, f))


def _dev_index(pos):
    return 4 * pos[0] + 2 * pos[1] + pos[2]


def _block(ref, idx, cols):
    if not cols:
        return ref.at[idx]
    n = ref.shape[-1] // N_DEV
    start = pl.multiple_of(idx * n, LANES)
    return ref.at[(slice(None),) * (len(ref.shape) - 1) + (pl.ds(start, n),)]


def _all_gather(xs, *, name, cols=False):
    n_arr = len(xs)

    def body(*refs):
        x_refs, out_refs = refs[:n_arr], refs[n_arr:2 * n_arr]
        send_sems, recv_sems, local_sems = refs[2 * n_arr:]
        me = (lax.axis_index("x"), lax.axis_index("y"), lax.axis_index("c"))
        sibling = _flip(me, _FLIPS[0])
        chips = [_flip(me, f) for f in _FLIPS[1:]]

        def copy(a, k, block, to, src=None):
            dst = _block(out_refs[a], _dev_index(block), cols)
            return pltpu.make_async_remote_copy(
                src_ref=dst if src is None else src, dst_ref=dst,
                send_sem=send_sems.at[7 * a + k], recv_sem=recv_sems.at[7 * a + k],
                device_id=to, device_id_type=pl.DeviceIdType.MESH)

        mine = [pltpu.make_async_copy(x_refs[a], _block(out_refs[a], _dev_index(me), cols), local_sems.at[a])
                for a in range(n_arr)]
        for cp in mine:
            cp.start()
        first = []
        for a in range(n_arr):
            first.append(copy(a, 0, me, sibling, src=x_refs[a]))
            first += [copy(a, 1 + j, me, chip, src=x_refs[a]) for j, chip in enumerate(chips)]
        for cp in first:
            cp.start()
        passed = []
        for j, chip in enumerate(chips):
            for a in range(n_arr):
                copy(a, 1 + j, chip, me).wait_recv()
                fwd = copy(a, 4 + j, chip, sibling)
                fwd.start()
                passed.append(fwd)
        for a in range(n_arr):
            copy(a, 0, sibling, me).wait_recv()
            for j, chip in enumerate(chips):
                copy(a, 4 + j, _flip(chip, _FLIPS[0]), me).wait_recv()
        for cp in first + passed:
            cp.wait_send()
        for cp in mine:
            cp.wait()

    def out_shape(x):
        shape = x.shape[:-1] + (N_DEV * x.shape[-1],) if cols else (N_DEV,) + x.shape
        return jax.ShapeDtypeStruct(shape, x.dtype)

    return pl.pallas_call(
        body, name=name,
        in_specs=[_HBM] * n_arr, out_specs=[_HBM] * n_arr,
        out_shape=[out_shape(x) for x in xs],
        scratch_shapes=[pltpu.SemaphoreType.DMA((7 * n_arr,)), pltpu.SemaphoreType.DMA((7 * n_arr,)),
                        pltpu.SemaphoreType.DMA((n_arr,))],
    )(*xs)


def _exchange(ps, *, name, cols=False):
    n_arr = len(ps)
    blk = ps[0].shape[:-1] + (ps[0].shape[-1] // N_DEV,) if cols else ps[0].shape[1:]

    def body(*refs):
        p_refs, out_ref = refs[:n_arr], refs[n_arr]
        send_sems, recv_sems, local_sems = refs[n_arr + 1:]
        me = (lax.axis_index("x"), lax.axis_index("y"), lax.axis_index("c"))
        me_idx = _dev_index(me)
        own = [pltpu.make_async_copy(_block(p_refs[a], me_idx, cols), out_ref.at[me_idx, a], local_sems.at[a])
               for a in range(n_arr)]
        for cp in own:
            cp.start()
        sends = []
        for k, f in enumerate(_ALL_FLIPS):
            peer = _flip(me, f)
            for a in range(n_arr):
                cp = pltpu.make_async_remote_copy(
                    src_ref=_block(p_refs[a], _dev_index(peer), cols), dst_ref=out_ref.at[me_idx, a],
                    send_sem=send_sems.at[7 * a + k], recv_sem=recv_sems.at[7 * a + k],
                    device_id=peer, device_id_type=pl.DeviceIdType.MESH)
                cp.start()
                sends.append(cp)
        for cp in sends:
            cp.wait()
        for cp in own:
            cp.wait()

    return pl.pallas_call(
        body, name=name,
        in_specs=[_HBM] * n_arr, out_specs=_HBM,
        out_shape=jax.ShapeDtypeStruct((N_DEV, n_arr) + blk, ps[0].dtype),
        scratch_shapes=[pltpu.SemaphoreType.DMA((7 * n_arr,)), pltpu.SemaphoreType.DMA((7 * n_arr,)),
                        pltpu.SemaphoreType.DMA((n_arr,))],
    )(*ps)


_SEM = pl.BlockSpec(memory_space=pltpu.SEMAPHORE)
_ANY = pl.BlockSpec(memory_space=pl.ANY)
_N_PEERS = N_DEV - 1


def _hbm(x):
    return pltpu.with_memory_space_constraint(x, pltpu.HBM)


def _me():
    return lax.axis_index("x"), lax.axis_index("y"), lax.axis_index("c")


def _spread_copies(plan, src_refs, land_refs, send_sems, recv_sems, local_sems):
    local, remote = plan(src_refs, land_refs)
    local = [pltpu.make_async_copy(src, dst, local_sems.at[i]) for i, (src, dst) in enumerate(local)]
    remote = [pltpu.make_async_remote_copy(src_ref=src, dst_ref=dst, send_sem=send_sems.at[k], recv_sem=recv_sems.at[k],
                                           device_id=peer, device_id_type=pl.DeviceIdType.MESH)
              for k, (src, dst, peer) in enumerate(remote)]
    return local, remote


def _spread_start(srcs, lands, plan, n_remote, n_local, *, name):
    ns, nl = len(srcs), len(lands)

    def body(*refs):
        src_refs, land_refs = refs[:ns], refs[ns:ns + nl]
        send_sems, recv_sems, local_sems = refs[ns + nl:ns + nl + 3]
        local, remote = _spread_copies(plan, src_refs, land_refs, send_sems, recv_sems, local_sems)
        for cp in local + remote:
            cp.start()
        token = refs[-1]
        token[...] = jnp.zeros_like(token)

    lands = [_hbm(lax.empty(*x)) if isinstance(x, tuple) else x for x in lands]
    out = pl.pallas_call(
        body, name=name,
        in_specs=[_HBM] * (ns + nl),
        out_specs=[_SEM] * 3 + [_HBM] * (ns + nl) + [pl.BlockSpec(memory_space=pltpu.VMEM)],
        out_shape=[pltpu.SemaphoreType.DMA((n_remote,)), pltpu.SemaphoreType.DMA((n_remote,)),
                   pltpu.SemaphoreType.DMA((max(n_local, 1),))]
        + [pltpu.HBM(x.shape, x.dtype) for x in list(srcs) + lands]
        + [jax.ShapeDtypeStruct((SUBLANES, LANES), F32)],
        input_output_aliases={i: 3 + i for i in range(ns + nl)},
        compiler_params=pltpu.CompilerParams(has_side_effects=pltpu.SideEffectType.DATAFLOW_SIDE_EFFECTING),
    )(*[_hbm(x) for x in srcs], *lands)
    return dict(sems=list(out[:3]), srcs=list(out[3:3 + ns]), lands=list(out[3 + ns:3 + ns + nl]),
                token=out[-1], plan=plan)


def _spread_wait(handle, after, *, name):
    ns, nl = len(handle["srcs"]), len(handle["lands"])

    def body(*refs):
        src_refs, land_refs = refs[:ns], refs[ns:ns + nl]
        send_sems, recv_sems, local_sems = refs[ns + nl:ns + nl + 3]
        local, remote = _spread_copies(handle["plan"], src_refs, land_refs, send_sems, recv_sems, local_sems)
        for cp in local:
            cp.wait()
        for cp in remote:
            cp.wait_send()
            cp.wait_recv()

    out = pl.pallas_call(
        body, name=name,
        in_specs=[_HBM] * (ns + nl) + [_SEM] * 3 + [_ANY],
        out_specs=[_HBM] * (ns + nl),
        out_shape=[pltpu.HBM(x.shape, x.dtype) for x in handle["srcs"] + handle["lands"]],
        input_output_aliases={i: i for i in range(ns + nl)},
        compiler_params=pltpu.CompilerParams(has_side_effects=pltpu.SideEffectType.DATAFLOW_SIDE_EFFECTING),
    )(*handle["srcs"], *handle["lands"], *handle["sems"], after)
    return list(out[ns:])


def _gather_start(x, *, name, cols=False, relayed=True):
    shape = x.shape[:-1] + (N_DEV * x.shape[-1],) if cols else (N_DEV,) + x.shape
    flips = _FLIPS if relayed else _ALL_FLIPS

    def plan(src_refs, land_refs):
        me = _me()
        mine = _block(land_refs[0], _dev_index(me), cols)
        return [(src_refs[0], mine)], [(src_refs[0], mine, _flip(me, f)) for f in flips]

    handle = _spread_start([x], [(shape, x.dtype)], plan, len(flips), 1, name=name)
    handle["cols"] = cols
    return handle


def _gather_relay(handle, after, *, name):
    cols = handle["cols"]
    land, = _spread_wait(handle, after, name=f"{name}_arrived")

    def plan(src_refs, land_refs):
        me = _me()
        blocks = [_block(land_refs[0], _dev_index(_flip(me, f)), cols) for f in _FLIPS[1:]]
        return [], [(blk, blk, _flip(me, _FLIPS[0])) for blk in blocks]

    return _spread_start([], [land], plan, len(_FLIPS) - 1, 0, name=f"{name}_pass")


def _exchange_start(ps, *, name, cols=False):
    blk = ps[0].shape[:-1] + (ps[0].shape[-1] // N_DEV,) if cols else ps[0].shape[1:]

    def plan(src_refs, land_refs):
        me = _me()
        me_idx = _dev_index(me)
        local = [(_block(src, me_idx, cols), land_refs[0].at[me_idx, a]) for a, src in enumerate(src_refs)]
        remote = [(_block(src, _dev_index(_flip(me, f)), cols), land_refs[0].at[me_idx, a], _flip(me, f))
                  for f in _ALL_FLIPS for a, src in enumerate(src_refs)]
        return local, remote

    return _spread_start(ps, [((N_DEV, len(ps)) + blk, ps[0].dtype)], plan, _N_PEERS * len(ps), len(ps), name=name)


def _adamw(parts, w, m, v, *, name, layer=0, prev=None):
    n_rows, c = w.shape
    r = parts.shape[1]
    row_bytes = c * (N_DEV * parts.dtype.itemsize + 7 * 4) * 2
    tr = r
    for cand in (512, 256, 128, 64, 32, 16):
        if r % cand == 0 and cand * row_bytes <= 40 * 1024 * 1024:
            tr = cand
            break
    c1 = 1.0 - ADAM_B1 ** ADAM_STEP
    c2 = 1.0 - ADAM_B2 ** ADAM_STEP

    def body(p_ref, w_ref, m_ref, v_ref, *rest):
        g_ref, d_ref, nm_ref, nv_ref = rest[-4:]
        g = p_ref[0].astype(F32)
        for j in range(1, N_DEV):
            g = g + p_ref[j].astype(F32)
        nm = ADAM_B1 * m_ref[...] + (1.0 - ADAM_B1) * g
        nv = ADAM_B2 * v_ref[...] + (1.0 - ADAM_B2) * (g * g)
        g_ref[...] = g
        nm_ref[...] = nm
        nv_ref[...] = nv
        d_ref[...] = -ADAM_LR * ((nm * (1.0 / c1)) / (jnp.sqrt(nv * (1.0 / c2)) + ADAM_EPS) + ADAM_WD * w_ref[...])

    off = layer * (r // tr)
    blk = pl.BlockSpec((tr, c), lambda i: (i + off, 0))
    prev = list(prev) if prev is not None else []
    return pl.pallas_call(
        body, name=name,
        grid=(r // tr,),
        in_specs=[pl.BlockSpec((N_DEV, tr, c), lambda i: (0, i, 0)), blk, blk, blk] + [_ANY] * len(prev),
        out_specs=[blk] * 4,
        out_shape=[jax.ShapeDtypeStruct((n_rows, c), F32)] * 4,
        input_output_aliases={4 + j: j for j in range(len(prev))},
        compiler_params=_params("parallel"),
    )(parts, w, m, v, *prev)


_TN_CANDS = (512, 256, 128)
_TK_MAX = 5632
_TK_WHOLE_ROWS = 2816


def _contraction_tiles(m, k):
    tk = k
    while tk > _TK_MAX and tk % 2 == 0 and (tk // 2) % LANES == 0:
        tk //= 2
    tm = m if tk <= _TK_WHOLE_ROWS or m % 2 else m // 2
    return tm, tk


def _nn(a, b, name, out_dtype=F32):
    tm, tk = _contraction_tiles(*a.shape)
    return _mm(a, b, "nn", name=name, out_dtype=out_dtype, tm=tm, tn=_pick(b.shape[1], _TN_CANDS), tk=tk)


def _nt(a, b, name, out_dtype=F32):
    tm, tk = _contraction_tiles(*a.shape)
    return _mm(a, b, "nt", name=name, out_dtype=out_dtype, tm=tm, tn=_pick(b.shape[0], _TN_CANDS), tk=tk)


def _tn(a, b, name, out_dtype=BF16, deps=()):
    assert a.shape[0] == b.shape[0], (a.shape, b.shape)
    return _mm_tn(a, b, name=name, out_dtype=out_dtype, tm=_pick(a.shape[1], _TN_CANDS),
                  tn=_pick(b.shape[1], (1024,) + _TN_CANDS), deps=deps)


def _local_step(x, p, target, rep, weight, emit, n_heads, start_tokens=()):
    s, d = x.shape
    depth = p.shape[0]
    grads = {}
    rep_grads = {k: [None] * depth for k in ("mix_pre_g", "mix_post_g", "ffn_pre_g", "ffn_post_g", "ple_norm_g")}

    pending = list(start_tokens)
    gains = {}

    def gain(name, i):
        if (name, i) not in gains:
            gains[name, i] = rep[name][i:i + 1]
        return gains[name, i]

    def send(name, layer, g):
        token = emit(name, layer, g)
        if token is not None:
            pending.append(token)

    def rowcall(*args, **kwargs):
        deps, pending[:] = tuple(pending), []
        return _rowcall(*args, deps=deps, **kwargs)

    deferred = []

    def send_small(i, name, layer, a, b, mm_name):
        if i == 0 and depth > 1:
            deferred.append((name, layer, a, b, mm_name))
        else:
            send(name, layer, _tn(a, b, mm_name))

    saved = []
    h = x
    for i in range(depth):
        sv = {"h": h}
        n1, = rowcall(f"pre_norm{i}", lambda hh, g: _rms(hh, g), [h], [gain("mix_pre_g", i)], [BF16], cols=d)
        sv["n1"] = n1
        if i % 2 == 0:
            proj = _nn(n1, weight("w_in_even", 0, n1), f"in_even{i}")
            a_out = _sb_fwd(proj, n_heads, name=f"sb_fwd{i}")
            b_out, oraw, states = _hg_fwd(proj, rep["hg_lb_logits"], rep["hg_norm_g"], n_heads, name=f"hg_fwd{i}")
            cat = jnp.concatenate([a_out.astype(BF16), b_out], axis=1)
            m = _nn(cat, weight("w_out_even", 0, cat), f"out_even{i}")
            sv.update(proj=proj, oraw=oraw, states=states, cat=cat)
        else:
            proj = _nn(n1, weight("w_in_odd", 0, n1), f"in_odd{i}")
            sm = {k: weight(k, 0, proj) for k in _SMALL}
            xc, ra, ix = _rg_gates_fwd(proj, sm["conv_w"], sm["conv_b"], sm["rg_wa"], sm["rg_ba"],
                                       sm["rg_wx"], sm["rg_bx"], name=f"rg_gates_fwd{i}")
            hs, gact = _rg_scan_fwd(proj, xc, ra, ix, sm["rg_lambda"], name=f"rg_scan_fwd{i}")
            m = _nn(gact, weight("w_out_odd", 0, gact), f"out_odd{i}")
            sv.update(proj=proj, xc=xc, ra=ra, ix=ix, hs=hs, gact=gact, sm=sm)

        def post_mix(hh, mm, g_post, g_pre):
            h1 = hh + _rms(mm, g_post)
            return h1, _rms(h1, g_pre)

        h1, n2 = rowcall(f"post_mix{i}", post_mix, [h, m], [gain("mix_post_g", i), gain("ffn_pre_g", i)],
                          [F32, BF16], cols=d)
        gate, up, act = _gate_up(n2, weight("w_gate_up", i, n2), name=f"gate_up{i}")
        f = _nn(act, weight("w_down", i, act), f"down{i}")

        def post_ffn(hh, ff_out, g_post):
            h2 = hh + _rms(ff_out, g_post)
            return h2, h2

        h2, h2b = rowcall(f"post_ffn{i}", post_ffn, [h1, f], [gain("ffn_post_g", i)], [F32, BF16], cols=d)
        e = _nn(p[i], weight("w_ple_up", i, h2b), f"ple_up{i}")
        gl = _nn(h2b, weight("w_ple_gate", i, h2b), f"ple_gate{i}")
        h3, = rowcall(f"ple{i}", lambda hh, a, b, g: hh + _rms(_sigmoid(a) * b, g), [h2, gl, e],
                       [gain("ple_norm_g", i)], [F32], cols=d)
        sv.update(m=m, h1=h1, n2=n2, gate=gate, up=up, act=act, f=f, h2b=h2b, e=e, gl=gl)
        saved.append(sv)
        h = h3

    def loss_fn(y, t):
        err = y - t
        return err * (1.0 / d), jnp.sum(err * err, axis=0, keepdims=True) * (0.5 / d)

    dh, loss_cols = rowcall("loss", loss_fn, [h, target], [], [F32], red_rows=(1,), cols=d)

    for i in reversed(range(depth)):
        sv = saved[i]

        def ple_bwd(dy, a, b, g):
            _, vjp = jax.vjp(lambda a_, b_, g_: _rms(_sigmoid(a_) * b_, g_), a, b, g)
            return vjp(dy)

        dgl, de, rep_grads["ple_norm_g"][i] = rowcall(
            f"ple_bwd{i}", ple_bwd, [dh, sv["gl"], sv["e"]], [gain("ple_norm_g", i)], [BF16, BF16],
            red_rows=(1,), cols=d)
        send_small(i, "w_ple_up", i, p[i], de, f"d_ple_up{i}")
        send_small(i, "w_ple_gate", i, sv["h2b"], dgl, f"d_ple_gate{i}")
        dh2_ple = _nt(dgl, weight("w_ple_gate", i, dgl), f"dx_ple_gate{i}")

        def post_ffn_bwd(dy, dx, ff_out, g):
            dh2 = dy + dx
            _, vjp = jax.vjp(_rms, ff_out, g)
            df, dg = vjp(dh2)
            return dh2, df, dg

        dh2, df, rep_grads["ffn_post_g"][i] = rowcall(
            f"post_ffn_bwd{i}", post_ffn_bwd, [dh, dh2_ple, sv["f"]], [gain("ffn_post_g", i)], [F32, BF16],
            red_rows=(1,), cols=d)
        send("w_down", i, _tn(sv["act"], df, f"d_down{i}"))
        dact = _nt(df, weight("w_down", i, df), f"dx_down{i}", out_dtype=BF16)
        dgu = _swiglu_bwd(sv["gate"], sv["up"], dact, name=f"swiglu_bwd{i}")
        send("w_gate_up", i, _tn(sv["n2"], dgu, f"d_gate_up{i}"))
        dn2 = _nt(dgu, weight("w_gate_up", i, dgu), f"dx_gate_up{i}")

        def post_mix_bwd(dy, dn, h1, mm, g_post, g_pre):
            _, vjp_pre = jax.vjp(_rms, h1, g_pre)
            dh1_n, dg_pre = vjp_pre(dn)
            dh1 = dy + dh1_n
            _, vjp_post = jax.vjp(_rms, mm, g_post)
            dm, dg_post = vjp_post(dh1)
            return dh1, dm, dg_pre, dg_post

        dh1, dm, rep_grads["ffn_pre_g"][i], rep_grads["mix_post_g"][i] = rowcall(
            f"post_mix_bwd{i}", post_mix_bwd, [dh2, dn2, sv["h1"], sv["m"]],
            [gain("mix_post_g", i), gain("ffn_pre_g", i)], [F32, BF16], red_rows=(1, 1), cols=d)

        if i % 2 == 0:
            send_small(i, "w_out_even", 0, sv["cat"], dm, f"d_out_even{i}")
            dcat = _nt(dm, weight("w_out_even", 0, dm), f"dx_out_even{i}")
            dq, dk, dv = _sb_bwd(sv["proj"], dcat, n_heads, name=f"sb_bwd{i}")
            dhq, dhf, dhi, dhg, grads["hg_lb_logits"], grads["hg_norm_g"] = _hg_bwd(
                sv["proj"], rep["hg_lb_logits"], rep["hg_norm_g"], sv["oraw"], sv["states"], dcat, n_heads,
                name=f"hg_bwd{i}")
            dproj = jnp.concatenate([dq.astype(BF16), dk.astype(BF16), dv.astype(BF16), dhq, dhf, dhi, dhg], axis=1)
            send("w_in_even", 0, _tn(sv["n1"], dproj, f"d_in_even{i}"))
            dn1 = _nt(dproj, weight("w_in_even", 0, dproj), f"dx_in_even{i}")
        else:
            sm = sv["sm"]
            send_small(i, "w_out_odd", 0, sv["gact"], dm, f"d_out_odd{i}")
            dgo = _nt(dm, weight("w_out_odd", 0, dm), f"dx_out_odd{i}")
            dgate, dra, dix, dxc1, grads["rg_lambda"] = _rg_scan_bwd(
                dgo, sv["proj"], sv["hs"], sv["xc"], sv["ra"], sv["ix"], sm["rg_lambda"], name=f"rg_scan_bwd{i}")
            (dxb, grads["conv_w"], grads["conv_b"], grads["rg_wa"], grads["rg_ba"], grads["rg_wx"],
             grads["rg_bx"]) = _rg_gates_bwd(dra, dix, dxc1, sv["xc"], sv["proj"], sm["conv_w"], sm["rg_wa"],
                                            sm["rg_wx"], name=f"rg_gates_bwd{i}")
            send("small", 0, {k: grads.pop(k) for k in _SMALL})
            dproj = jnp.concatenate([dgate, dxb], axis=1)
            send("w_in_odd", 0, _tn(sv["n1"], dproj, f"d_in_odd{i}"))
            dn1 = _nt(dproj, weight("w_in_odd", 0, dproj), f"dx_in_odd{i}")

        def pre_norm_bwd(dy, dn, hh, g):
            _, vjp = jax.vjp(_rms, hh, g)
            dx, dg = vjp(dn)
            return dy + dx, dg

        dh, rep_grads["mix_pre_g"][i] = rowcall(
            f"pre_norm_bwd{i}", pre_norm_bwd, [dh1, dn1, sv["h"]], [gain("mix_pre_g", i)], [F32],
            red_rows=(1,), cols=d)

    for name, layer, a, b, mm_name in deferred:
        send(name, layer, _tn(a, b, mm_name, deps=(dh,)))
    for k, rows in rep_grads.items():
        grads[k] = jnp.concatenate(rows, axis=0)
    return loss_cols, dh, grads


_WEIGHTS = ("mix_pre_g", "mix_post_g", "ffn_pre_g", "ffn_post_g", "ple_norm_g", "w_in_even", "w_out_even",
            "hg_lb_logits", "hg_norm_g", "w_in_odd", "conv_w", "conv_b", "rg_wa", "rg_ba", "rg_wx", "rg_bx",
            "rg_lambda", "w_out_odd", "w_gate_up", "w_down", "w_ple_up", "w_ple_gate")
_REPLICATED = ("mix_pre_g", "mix_post_g", "ffn_pre_g", "ffn_post_g", "ple_norm_g", "hg_lb_logits", "hg_norm_g")
_SMALL = ("conv_w", "conv_b", "rg_wa", "rg_ba", "rg_wx", "rg_bx", "rg_lambda")
_BIG = {"w_in_even": True, "w_out_even": False, "w_in_odd": True, "w_out_odd": False,
        "w_gate_up": True, "w_down": False, "w_ple_up": True, "w_ple_gate": False}
_PACK_ROW = SUBLANES * LANES


def _pack(arrays):
    flat = jnp.concatenate([a.reshape(-1) for a in arrays])
    pad = -flat.shape[0] % _PACK_ROW
    return jnp.pad(flat, (0, pad)).reshape(-1, LANES)


def _pack_blocks(arrays):
    flat = jnp.concatenate([a.reshape(N_DEV, -1) for a in arrays], axis=1)
    pad = -flat.shape[1] % _PACK_ROW
    return jnp.pad(flat, ((0, 0), (0, pad))).reshape(N_DEV, -1, LANES)


def _unpack(packed, shapes, lead=()):
    flat = packed.reshape(lead + (-1,))
    out, pos = [], 0
    for shape in shapes:
        n = math.prod(shape)
        out.append(flat[..., pos:pos + n].reshape(lead + tuple(shape)))
        pos += n
    return out


def _to_full_small(name, blocks):
    if name == "conv_w":
        return jnp.transpose(blocks, (1, 0, 2)).reshape(blocks.shape[1], -1)
    if name in ("conv_b", "rg_lambda"):
        return blocks.reshape(1, -1)
    nb = blocks.shape[1]
    if name in ("rg_wa", "rg_wx"):
        return jnp.transpose(blocks, (1, 0, 2, 3)).reshape(nb, RG_BLOCK, RG_BLOCK)
    return jnp.transpose(blocks, (1, 0, 2)).reshape(nb, 1, RG_BLOCK)


def _to_blocks_small(name, full):
    if name == "conv_w":
        return jnp.transpose(full.reshape(full.shape[0], N_DEV, -1), (1, 0, 2))
    if name in ("conv_b", "rg_lambda"):
        return full.reshape(N_DEV, -1)
    nb = full.shape[0]
    if name in ("rg_wa", "rg_wx"):
        return jnp.transpose(full.reshape(nb, N_DEV, RG_BLOCK // N_DEV, RG_BLOCK), (1, 0, 2, 3))
    return jnp.transpose(full.reshape(nb, N_DEV, RG_BLOCK // N_DEV), (1, 0, 2))


def _step(inp):
    w = {k: inp[k] for k in _WEIGHTS}
    x, p, target = inp["x"][0], inp["p"][:, 0], inp["loss_target"][0]
    assert w["hg_lb_logits"].shape[0] == 2 and w["w_in_even"].shape[0] == 1 and w["w_in_odd"].shape[0] == 1

    n_heads = w["w_in_even"].shape[2] * N_DEV // (7 * HEAD_DIM)
    small_shapes = [w[k].shape[1:] for k in _SMALL]

    def lands_in_place(name):
        return _BIG[name] and w[name].shape[2] % LANES == 0

    depth = p.shape[0]
    order = [("w_in_even", 0), ("w_out_even", 0)] if depth else []
    for i in range(depth):
        if i == 1:
            order += [("w_in_odd", 0), ("small", 0), ("w_out_odd", 0)]
        order += [("w_gate_up", i), ("w_down", i), ("w_ple_up", i), ("w_ple_gate", i)]
    gathers = {}
    for name, l in order:
        if name == "small":
            gathers[name, l] = _gather_start(_pack([w[k][0] for k in _SMALL]), name="gather_small", relayed=False)
        else:
            gathers[name, l] = _gather_start(w[name][l].astype(BF16), name=f"gather_{name}{l}",
                                             cols=lands_in_place(name))
    ready = {}

    def relay(key, after):
        if "cols" in gathers[key] and key[0] != "small":
            gathers[key] = _gather_relay(gathers[key], after, name=f"gather_{key[0]}{key[1]}")

    def weight(name, layer, after):
        key = ("small", 0) if name in _SMALL else (name, layer)
        if key not in ready:
            relay(key, after)
            for nxt in order[order.index(key) + 1:order.index(key) + 2]:
                relay(nxt, after)
            land, = _spread_wait(gathers[key], after, name=f"gathered_{key[0]}{key[1]}")
            if name in _SMALL:
                ready[key] = {k: _to_full_small(k, b)
                              for k, b in zip(_SMALL, _unpack(land, small_shapes, lead=(N_DEV,)))}
            elif lands_in_place(name):
                ready[key] = land
            elif _BIG[name]:
                ready[key] = jnp.transpose(land, (1, 0, 2)).reshape(land.shape[1], -1)
            else:
                ready[key] = land.reshape(-1, land.shape[2])
        return ready[key][name] if name in _SMALL else ready[key]

    exchanges = []

    def emit(name, layer, g):
        if name == "small":
            handle = _exchange_start([_pack_blocks([_to_blocks_small(k, g[k]) for k in _SMALL])],
                                     name="exchange_small")
            exchanges.append((name, layer, handle))
            return handle["token"]
        _, r, c = w[name].shape
        if lands_in_place(name):
            handle = _exchange_start([g], name=f"exchange_{name}{layer}", cols=True)
        elif _BIG[name]:
            handle = _exchange_start([jnp.transpose(g.reshape(-1, N_DEV, c), (1, 0, 2))],
                                     name=f"exchange_{name}{layer}")
        else:
            handle = _exchange_start([g.reshape(N_DEV, r, c)], name=f"exchange_{name}{layer}")
        exchanges.append((name, layer, handle))
        return handle["token"]

    rep = {k: w[k] for k in _REPLICATED}
    loss_cols, dx, grads = _local_step(x, p, target, rep, weight, emit, n_heads,
                                       [h["token"] for h in gathers.values()])

    loss_part = jnp.sum(loss_cols).reshape(1)
    rep_gather = _gather_start(_pack([grads[k] for k in _REPLICATED] + [loss_part]), name="gather_rep_grads",
                               relayed=False)

    out = {}
    after = dx
    for name, layer, handle in exchanges:
        land, = _spread_wait(handle, after, name=f"exchanged_{name}{layer}")
        if name == "small":
            res = _adamw(land.reshape(N_DEV, -1, LANES), *[_pack([inp[pre + k][0] for k in _SMALL]) for pre in ("", "m_", "v_")],
                         name="adamw_small")
            for k, *vals in zip(_SMALL, *[_unpack(a, small_shapes) for a in res]):
                out[k] = [v[None] for v in vals]
        else:
            n_l, r, c = w[name].shape
            res = out[name] = _adamw(land.reshape(N_DEV, r, c),
                                     *[inp[pre + name].reshape(n_l * r, c) for pre in ("", "m_", "v_")],
                                     name=f"adamw_{name}{layer}", layer=layer, prev=out.get(name))
        after = res[0]
    for name in _BIG:
        out[name] = [a.reshape(w[name].shape) for a in out[name]]

    rep_shapes = [w[k].shape for k in _REPLICATED] + [(1,)]
    rep_parts, = _spread_wait(rep_gather, after, name="gathered_rep_grads")
    res = _adamw(rep_parts, *[_pack([inp[pre + k] for k in _REPLICATED] + [jnp.zeros((1,), F32)])
                              for pre in ("", "m_", "v_")], name="adamw_rep")
    for k, *vals in zip(_REPLICATED + ("loss",), *[_unpack(a, rep_shapes) for a in res]):
        out[k] = vals
    loss = out["loss"][0][0]

    return (loss, dx[None]) + tuple(out[k][j] for j in range(4) for k in _WEIGHTS)


def kernel(x, p, mix_pre_g, mix_post_g, ffn_pre_g, ffn_post_g, ple_norm_g, w_in_even, w_out_even, hg_lb_logits, hg_norm_g, w_in_odd, conv_w, conv_b, rg_wa, rg_ba, rg_wx, rg_bx, rg_lambda, w_out_odd, w_gate_up, w_down, w_ple_up, w_ple_gate, loss_target, m_mix_pre_g, m_mix_post_g, m_ffn_pre_g, m_ffn_post_g, m_ple_norm_g, m_w_in_even, m_w_out_even, m_hg_lb_logits, m_hg_norm_g, m_w_in_odd, m_conv_w, m_conv_b, m_rg_wa, m_rg_ba, m_rg_wx, m_rg_bx, m_rg_lambda, m_w_out_odd, m_w_gate_up, m_w_down, m_w_ple_up, m_w_ple_gate, v_mix_pre_g, v_mix_post_g, v_ffn_pre_g, v_ffn_post_g, v_ple_norm_g, v_w_in_even, v_w_out_even, v_hg_lb_logits, v_hg_norm_g, v_w_in_odd, v_conv_w, v_conv_b, v_rg_wa, v_rg_ba, v_rg_wx, v_rg_bx, v_rg_lambda, v_w_out_odd, v_w_gate_up, v_w_down, v_w_ple_up, v_w_ple_gate):
    return _step(dict(locals()))
```

```python
import functools
import math

import jax
import jax.numpy as jnp
from jax import lax
from jax.experimental import pallas as pl
from jax.experimental.pallas import tpu as pltpu

F32 = jnp.float32
BF16 = jnp.bfloat16

VMEM_LIMIT_BYTES = 56 * 1024 * 1024
ADAMW_BLOCK_BYTES = 40 * 1024 * 1024
LANES = 128
SUBLANES = 8

N_DEV = 8
HEAD_DIM = 128
SB_Q_TILE = 512
SB_K_TILE = 128
HG_CHUNK = 32
HG_HEADS_PER_STEP = 2
RG_BLOCK = 256
CONV_TAPS = 4
RG_C = 8.0
RMS_EPS = 1e-6

ADAM_LR = 0.001
ADAM_B1 = 0.9
ADAM_B2 = 0.999
ADAM_EPS = 1e-08
ADAM_WD = 0.01
ADAM_STEP = 10


def _params(*sem):
    return pltpu.CompilerParams(dimension_semantics=sem, vmem_limit_bytes=VMEM_LIMIT_BYTES)


def _pick(n, cands):
    for c in cands:
        if c <= n and n % c == 0:
            return c
    return n


def _mm(a, b, mode, *, name, out_dtype=F32, tm=512, tn=512, tk=None):
    if mode == "nn":
        (m, k), (k2, n) = a.shape, b.shape
    else:
        (m, k), (n, k2) = a.shape, b.shape
    assert k == k2, (a.shape, b.shape, mode)
    tm, tn = min(tm, m), min(tn, n)
    tk = k if tk is None else min(tk, k)
    assert m % tm == 0 and n % tn == 0 and k % tk == 0, (m, n, k, tm, tn, tk)
    nk = k // tk

    a_spec = pl.BlockSpec((tm, tk), lambda i, j, kk: (i, kk))
    if mode == "nn":
        b_spec = pl.BlockSpec((tk, tn), lambda i, j, kk: (kk, j))
        dims = (((1,), (0,)), ((), ()))
    else:
        b_spec = pl.BlockSpec((tn, tk), lambda i, j, kk: (j, kk))
        dims = (((1,), (1,)), ((), ()))

    def body(a_ref, b_ref, o_ref, *acc):
        part = lax.dot_general(a_ref[...].astype(BF16), b_ref[...].astype(BF16), dims, preferred_element_type=F32)
        if nk == 1:
            o_ref[...] = part.astype(out_dtype)
        else:
            acc_ref, = acc
            kk = pl.program_id(2)

            @pl.when(kk == 0)
            def _():
                acc_ref[...] = part

            @pl.when(kk > 0)
            def _():
                acc_ref[...] += part

            @pl.when(kk == nk - 1)
            def _():
                o_ref[...] = acc_ref[...].astype(out_dtype)

    return pl.pallas_call(
        body, name=name,
        grid=(m // tm, n // tn, nk),
        in_specs=[a_spec, b_spec],
        out_specs=pl.BlockSpec((tm, tn), lambda i, j, kk: (i, j)),
        out_shape=jax.ShapeDtypeStruct((m, n), out_dtype),
        scratch_shapes=[] if nk == 1 else [pltpu.VMEM((tm, tn), F32)],
        compiler_params=_params("parallel", "parallel", "arbitrary"),
    )(a, b)


def _mm_tn(a, b, *, name, out_dtype, tm, tn, deps=()):
    k, m = a.shape
    n = b.shape[1]

    def body(a_ref, b_ref, *refs):
        o_ref, at_ref = refs[len(deps):]

        @pl.when(pl.program_id(1) == 0)
        def _():
            at_ref[...] = a_ref[...].astype(F32).T.astype(BF16)

        o_ref[...] = jnp.dot(at_ref[...], b_ref[...].astype(BF16), preferred_element_type=F32).astype(out_dtype)

    return pl.pallas_call(
        body, name=name,
        grid=(m // tm, n // tn),
        in_specs=[pl.BlockSpec((k, tm), lambda i, j: (0, i)), pl.BlockSpec((k, tn), lambda i, j: (0, j))]
        + [pl.BlockSpec(memory_space=pl.ANY)] * len(deps),
        out_specs=pl.BlockSpec((tm, tn), lambda i, j: (i, j)),
        out_shape=jax.ShapeDtypeStruct((m, n), out_dtype),
        scratch_shapes=[pltpu.VMEM((tm, k), BF16)],
        compiler_params=_params("parallel", "arbitrary"),
    )(a, b, *deps)


def _rowcall(name, fn, rows, pars, row_outs, red_rows=(), *, cols, ts=256, tc=None, deps=()):
    rows = [r if isinstance(r, tuple) else (r, 0) for r in rows]
    pars = [p if isinstance(p, tuple) else (p, 0) for p in pars]
    s = rows[0][0].shape[0]
    tc = cols if tc is None else tc
    ts = min(ts, s)
    assert s % ts == 0 and cols % tc == 0, (name, s, ts, cols, tc)
    n_in, n_row_out = len(rows) + len(pars), len(row_outs)

    def body(*refs):
        outs = fn(*[r[...] for r in refs[:n_in]])
        outs = outs if isinstance(outs, (tuple, list)) else (outs,)
        o_refs = refs[n_in + len(deps):]
        for o_ref, val in zip(o_refs[:n_row_out], outs[:n_row_out]):
            o_ref[...] = val.astype(o_ref.dtype)
        first = pl.program_id(1) == 0
        for o_ref, val in zip(o_refs[n_row_out:], outs[n_row_out:]):
            @pl.when(first)
            def _(o_ref=o_ref, val=val):
                o_ref[...] = val

            @pl.when(jnp.logical_not(first))
            def _(o_ref=o_ref, val=val):
                o_ref[...] += val

    def row_map(off):
        return lambda j, i: (i, j + off)

    def par_map(off):
        return lambda j, i: (0, j + off)

    return pl.pallas_call(
        body, name=name,
        grid=(cols // tc, s // ts),
        in_specs=[pl.BlockSpec((ts, tc), row_map(off)) for _, off in rows]
        + [pl.BlockSpec((p.shape[0], tc), par_map(off)) for p, off in pars]
        + [pl.BlockSpec(memory_space=pl.ANY)] * len(deps),
        out_specs=[pl.BlockSpec((ts, tc), lambda j, i: (i, j)) for _ in row_outs]
        + [pl.BlockSpec((r, tc), lambda j, i: (0, j)) for r in red_rows],
        out_shape=[jax.ShapeDtypeStruct((s, cols), dt) for dt in row_outs]
        + [jax.ShapeDtypeStruct((r, cols), F32) for r in red_rows],
        compiler_params=_params("parallel", "arbitrary"),
    )(*[r for r, _ in rows], *[p for p, _ in pars], *deps)


def _swiglu_act(g, u):
    return _silu(g) * u


def _gate_up(n, w, *, name):
    s, d = n.shape
    f = w.shape[1] // 2
    tn = _pick(f, (256, 128))
    nj = f // tn

    def body(a_ref, wg_ref, wu_ref, g_ref, u_ref, act_ref):
        a = a_ref[...].astype(BF16)
        g = jnp.dot(a, wg_ref[...].astype(BF16), preferred_element_type=F32)
        u = jnp.dot(a, wu_ref[...].astype(BF16), preferred_element_type=F32)
        g_ref[...] = g.astype(BF16)
        u_ref[...] = u.astype(BF16)
        act_ref[...] = _swiglu_act(g, u).astype(BF16)

    out = pl.BlockSpec((s, tn), lambda j: (0, j))
    return pl.pallas_call(
        body, name=name,
        grid=(nj,),
        in_specs=[pl.BlockSpec((s, d), lambda j: (0, 0)), pl.BlockSpec((d, tn), lambda j: (0, j)),
                  pl.BlockSpec((d, tn), lambda j: (0, nj + j))],
        out_specs=[out] * 3,
        out_shape=[jax.ShapeDtypeStruct((s, f), BF16)] * 3,
        compiler_params=_params("parallel"),
    )(n, w, w)


def _swiglu_bwd(g, u, dact, *, name, ts=128):
    s, f = g.shape
    ts = min(ts, s)

    def body(g_ref, u_ref, dact_ref, o_ref):
        _, vjp = jax.vjp(_swiglu_act, g_ref[...].astype(F32), u_ref[...].astype(F32))
        dg, du = vjp(dact_ref[...].astype(F32))
        o_ref[:, 0:f] = dg.astype(BF16)
        o_ref[:, f:2 * f] = du.astype(BF16)

    narrow = pl.BlockSpec((ts, f), lambda i: (i, 0))
    return pl.pallas_call(
        body, name=name,
        grid=(s // ts,),
        in_specs=[narrow] * 3,
        out_specs=pl.BlockSpec((ts, 2 * f), lambda i: (i, 0)),
        out_shape=jax.ShapeDtypeStruct((s, 2 * f), BF16),
        compiler_params=_params("parallel"),
    )(g, u, dact)


def _rms(x, g):
    return x * lax.rsqrt(jnp.mean(x * x, axis=-1, keepdims=True) + RMS_EPS) * g


def _sigmoid(x):
    return jax.nn.sigmoid(x)


def _silu(x):
    return x * jax.nn.sigmoid(x)


def _gelu(x):
    return 0.5 * x * (1.0 + jnp.tanh(math.sqrt(2.0 / math.pi) * (x + 0.044715 * (x * x * x))))


def _softplus(x):
    return jnp.maximum(x, 0.0) + jnp.log1p(jnp.exp(-jnp.abs(x)))


def _split(x, terms):
    parts = []
    for _ in range(terms - 1):
        parts.append(x.astype(BF16))
        x = x - parts[-1].astype(F32)
    return parts + [x.astype(BF16)]


def _xdot(x, t, terms=3):
    return sum(jnp.dot(p, t, preferred_element_type=F32) for p in _split(x, terms))


def _xdot_l(t, x):
    return sum(jnp.dot(t, p, preferred_element_type=F32) for p in _split(x, 3))


_NT = (((1,), (1,)), ((), ()))
_TN = (((0,), (0,)), ((), ()))


def _dot(a, b, dims=None):
    if dims is None:
        return jnp.dot(a.astype(BF16), b.astype(BF16), preferred_element_type=F32)
    return lax.dot_general(a.astype(BF16), b.astype(BF16), dims, preferred_element_type=F32)


def _iota(shape, axis):
    return lax.broadcasted_iota(jnp.int32, shape, axis)


def _sb_tile(qb, kblk, mask, upper, c_rem):
    z = lax.dot_general(qb, kblk, _NT, preferred_element_type=F32)
    soft = jnp.log1p(jnp.exp(-jnp.abs(z)))
    lbeta = jnp.minimum(z, 0.0) - soft
    l1m = -jnp.maximum(z, 0.0) - soft
    if mask is not None:
        l1m = jnp.where(mask, l1m, 0.0)
    rem = _xdot(l1m, upper, terms=2) + c_rem
    w = jnp.exp(lbeta + rem)
    if mask is not None:
        w = jnp.where(mask, w, 0.0)
    return lbeta, l1m, w


def _sb_tiles(s):
    tq = min(SB_Q_TILE, s)
    return tq, SB_K_TILE, tq // SB_K_TILE


def _sb_key_loops(qi, per_q, step, carry):
    n_full = qi * per_q
    carry = lax.fori_loop(0, per_q, lambda j, c: step(n_full + per_q - 1 - j, True, c), carry)
    return lax.fori_loop(0, n_full, lambda j, c: step(n_full - 1 - j, False, c), carry)


def _sb_fwd(proj, n_heads, *, name):
    s = proj.shape[0]
    t, tk, per_q = _sb_tiles(s)
    scale = HEAD_DIM ** -0.5

    def body(q_ref, k_ref, v_ref, o_ref):
        qi = pl.program_id(1)
        qb = (q_ref[...] * scale).astype(BF16)
        row, col = _iota((t, tk), 0) + qi * t, _iota((t, tk), 1)
        upper = (_iota((tk, tk), 0) > _iota((tk, tk), 1)).astype(BF16)

        def step(kb, masked, carry):
            acc, c_rem = carry
            rows = pl.ds(pl.multiple_of(kb * tk, tk), tk)
            kblk = k_ref[rows, :].astype(BF16)
            vblk = v_ref[rows, :].astype(BF16)
            _, l1m, w = _sb_tile(qb, kblk, (col + kb * tk) < row if masked else None, upper, c_rem)
            acc = acc + jnp.dot(w.astype(BF16), vblk, preferred_element_type=F32)
            return acc, c_rem + jnp.sum(l1m, axis=1, keepdims=True)

        acc, _ = _sb_key_loops(qi, per_q, step, (jnp.zeros((t, HEAD_DIM), F32), jnp.zeros((t, 1), F32)))
        o_ref[...] = acc

    return pl.pallas_call(
        body, name=name,
        grid=(n_heads, s // t),
        in_specs=[pl.BlockSpec((t, HEAD_DIM), lambda h, i: (i, h)),
                  pl.BlockSpec((s, HEAD_DIM), lambda h, i: (0, n_heads + h)),
                  pl.BlockSpec((s, HEAD_DIM), lambda h, i: (0, 2 * n_heads + h))],
        out_specs=pl.BlockSpec((t, HEAD_DIM), lambda h, i: (i, h)),
        out_shape=jax.ShapeDtypeStruct((s, n_heads * HEAD_DIM), F32),
        compiler_params=_params("parallel", "arbitrary"),
    )(proj, proj, proj)


def _sb_bwd(proj, dcat, n_heads, *, name):
    s = proj.shape[0]
    t, tk, per_q = _sb_tiles(s)
    scale = HEAD_DIM ** -0.5

    def body(q_ref, k_ref, v_ref, do_ref, dq_ref, dk_ref, dv_ref, g_s, sig_s):
        qi = pl.program_id(1)

        @pl.when(qi == 0)
        def _():
            dk_ref[...] = jnp.zeros_like(dk_ref)
            dv_ref[...] = jnp.zeros_like(dv_ref)

        qb = (q_ref[...] * scale).astype(BF16)
        dob = do_ref[...].astype(BF16)
        row, col = _iota((t, tk), 0) + qi * t, _iota((t, tk), 1)
        upper = (_iota((tk, tk), 0) > _iota((tk, tk), 1)).astype(BF16)
        lower_incl = (_iota((tk, tk), 0) >= _iota((tk, tk), 1)).astype(BF16)

        def weights(kb, masked, carry):
            c_rem, g_all = carry
            rows = pl.ds(pl.multiple_of(kb * tk, tk), tk)
            kblk = k_ref[rows, :].astype(BF16)
            vblk = v_ref[rows, :].astype(BF16)
            lbeta, l1m, w = _sb_tile(qb, kblk, (col + kb * tk) < row if masked else None, upper, c_rem)
            g = w * lax.dot_general(dob, vblk, _NT, preferred_element_type=F32)
            dv_ref[rows, :] += lax.dot_general(w.astype(BF16), dob, _TN, preferred_element_type=F32)
            g_s[kb] = g
            sig_s[kb] = jnp.exp(lbeta)
            return c_rem + jnp.sum(l1m, axis=1, keepdims=True), g_all + jnp.sum(g, axis=1, keepdims=True)

        zero_col = jnp.zeros((t, 1), F32)
        _, g_all = _sb_key_loops(qi, per_q, weights, (zero_col, zero_col))

        def scores(kb, masked, carry):
            dq, c_g = carry
            rows = pl.ds(pl.multiple_of(kb * tk, tk), tk)
            g, sig = g_s[kb], sig_s[kb]
            g_before = g_all - (_xdot(g, lower_incl) + c_g)
            dz = g * (1.0 - sig) - g_before * sig
            if masked:
                dz = jnp.where((col + kb * tk) < row, dz, 0.0)
            dz = dz.astype(BF16)
            dq = dq + jnp.dot(dz, k_ref[rows, :].astype(BF16), preferred_element_type=F32)
            dk_ref[rows, :] += lax.dot_general(dz, qb, _TN, preferred_element_type=F32)
            return dq, c_g + jnp.sum(g, axis=1, keepdims=True)

        dq, _ = _sb_key_loops(qi, per_q, scores, (jnp.zeros((t, HEAD_DIM), F32), zero_col))
        dq_ref[...] = dq * scale

    width = n_heads * HEAD_DIM
    return pl.pallas_call(
        body, name=name,
        grid=(n_heads, s // t),
        in_specs=[pl.BlockSpec((t, HEAD_DIM), lambda h, i: (i, h)),
                  pl.BlockSpec((s, HEAD_DIM), lambda h, i: (0, n_heads + h)),
                  pl.BlockSpec((s, HEAD_DIM), lambda h, i: (0, 2 * n_heads + h)),
                  pl.BlockSpec((t, HEAD_DIM), lambda h, i: (i, h))],
        out_specs=[pl.BlockSpec((t, HEAD_DIM), lambda h, i: (i, h)),
                   pl.BlockSpec((s, HEAD_DIM), lambda h, i: (0, h)),
                   pl.BlockSpec((s, HEAD_DIM), lambda h, i: (0, h))],
        out_shape=[jax.ShapeDtypeStruct((s, width), F32)] * 3,
        scratch_shapes=[pltpu.VMEM((s // tk, t, tk), F32)] * 2,
        compiler_params=_params("parallel", "arbitrary"),
    )(proj, proj, proj, dcat)


def _hg_pre(hq, hf, logits):
    mx = jnp.max(logits, axis=0, keepdims=True)
    ex = jnp.exp(logits - mx)
    lb = ex[0:1, :] / jnp.sum(ex, axis=0, keepdims=True)
    f = lb + (1.0 - lb) * _sigmoid(hf)
    return _silu(hq), 1.0 - f, jnp.log(f)


def _hg_post(o, norm_g, hgate):
    return _rms(o, norm_g) * _silu(hgate)


def _hg_specs(s, n_heads, first_block):
    def at(group):
        return pl.BlockSpec((s, HEAD_DIM), lambda h: (0, first_block + group * n_heads + h))
    return [at(0), at(1), at(2), at(3)]


def _hg_fwd(proj, logits, norm_g, n_heads, *, name):
    s = proj.shape[0]
    hc = HG_CHUNK
    n_chunks = s // hc
    d = HEAD_DIM

    hp = HG_HEADS_PER_STEP
    assert n_heads % hp == 0
    wide = hp * d

    def body(lg_ref, ng_ref, hq_ref, hf_ref, hi_ref, hgt_ref, out_ref, oraw_ref, st_ref,
             q_s, k_s, lf_s, cum_s, qc_s, oc_s):
        q, k, lf = _hg_pre(hq_ref[...], hf_ref[...], lg_ref[...])
        q_s[...] = q
        k_s[...] = k
        lf_s[...] = lf
        tril = (_iota((hc, hc), 0) >= _iota((hc, hc), 1)).astype(BF16)
        srow = _iota((hc, d), 0)

        def head_chunk(j, ci, rows, st):
            ln = slice(j * d, (j + 1) * d)
            q, k, v = q_s[rows, ln], k_s[rows, ln], hi_ref[rows, ln]
            cum = _xdot_l(tril, lf_s[rows, ln])
            st_ref[j, ci] = st
            o_inter = _dot(q * jnp.exp(cum), st, _NT)
            cum_s[:, ln] = cum
            qc_s[:, ln] = q
            for t in range(hc):
                ng = (t // SUBLANES + 1) * SUBLANES
                e = jnp.where(srow[:ng] <= t, jnp.exp(cum_s[t:t + 1, ln] - cum[:ng]), 0.0)
                sc = jnp.sum(qc_s[t:t + 1, ln] * k[:ng] * e, axis=1, keepdims=True)
                oc_s[t:t + 1, ln] = jnp.sum(sc * v[:ng], axis=0, keepdims=True)
            oraw_ref[rows, ln] = o_inter + oc_s[:, ln]
            last = cum_s[hc - 1:hc, ln]
            return st * jnp.exp(last) + _dot(v, k * jnp.exp(last - cum), _TN)

        def chunk(ci, states):
            rows = pl.ds(pl.multiple_of(ci * hc, hc), hc)
            return tuple(head_chunk(j, ci, rows, st) for j, st in enumerate(states))

        lax.fori_loop(0, n_chunks, chunk, tuple(jnp.zeros((d, d), F32) for _ in range(hp)))
        for j in range(hp):
            ln = slice(j * d, (j + 1) * d)
            out_ref[:, ln] = _hg_post(oraw_ref[:, ln], ng_ref[...], hgt_ref[:, ln]).astype(BF16)

    width = n_heads * d
    first = 3 * n_heads // hp
    groups = [pl.BlockSpec((s, wide), functools.partial(lambda h, g: (0, first + g * (n_heads // hp) + h), g=g))
              for g in range(4)]
    head_block = pl.BlockSpec((s, wide), lambda h: (0, h))
    return pl.pallas_call(
        body, name=name,
        grid=(n_heads // hp,),
        in_specs=[pl.BlockSpec((2, wide), lambda h: (0, h)), pl.BlockSpec((1, d), lambda h: (0, 0))] + groups,
        out_specs=[head_block, head_block, pl.BlockSpec((hp, n_chunks, d, d), lambda h: (h, 0, 0, 0))],
        out_shape=[jax.ShapeDtypeStruct((s, width), BF16), jax.ShapeDtypeStruct((s, width), F32),
                   jax.ShapeDtypeStruct((n_heads, n_chunks, d, d), F32)],
        scratch_shapes=[pltpu.VMEM((s, wide), F32)] * 3 + [pltpu.VMEM((hc, wide), F32)] * 3,
        compiler_params=_params("arbitrary"),
    )(logits, norm_g, proj, proj, proj, proj)


def _hg_bwd(proj, logits, norm_g, oraw, states, dcat, n_heads, *, name):
    s = proj.shape[0]
    hc = HG_CHUNK
    n_chunks = s // hc
    d = HEAD_DIM

    def body(lg_ref, ng_ref, hq_ref, hf_ref, hi_ref, hgt_ref, oraw_ref, st_ref, dout_ref,
             dhq_ref, dhf_ref, dhi_ref, dhgt_ref, dlg_ref, dng_ref,
             q_s, k_s, lf_s, do_s, dq_s, dk_s, dlf_s, cum_s, qc_s, doc_s, dqc_s, dkc_s, dvc_s):
        head = pl.program_id(0)
        (q, k, lf), pre_vjp = jax.vjp(_hg_pre, hq_ref[...], hf_ref[...], lg_ref[...])
        q_s[...] = q
        k_s[...] = k
        lf_s[...] = lf
        _, post_vjp = jax.vjp(_hg_post, oraw_ref[...], ng_ref[...], hgt_ref[...])
        do, dng, dhgt = post_vjp(dout_ref[...])
        do_s[...] = do
        dhgt_ref[...] = dhgt.astype(BF16)

        @pl.when(head == 0)
        def _():
            dng_ref[...] = dng

        @pl.when(head > 0)
        def _():
            dng_ref[...] += dng

        triu = (_iota((hc, hc), 0) <= _iota((hc, hc), 1)).astype(BF16)
        tril = (_iota((hc, hc), 0) >= _iota((hc, hc), 1)).astype(BF16)
        srow = _iota((hc, d), 0)

        def chunk(j, dst):
            ci = n_chunks - 1 - j
            rows = pl.ds(pl.multiple_of(ci * hc, hc), hc)
            q, k, v, do_c = q_s[rows, :], k_s[rows, :], hi_ref[rows, :], do_s[rows, :]
            cum = _xdot_l(tril, lf_s[rows, :])
            st = st_ref[0, ci]
            cum_s[...] = cum
            qc_s[...] = q
            doc_s[...] = do_c
            last = cum_s[hc - 1:hc, :]
            e_cum, e_last = jnp.exp(cum), jnp.exp(last - cum)
            dqc_s[...] = _dot(do_c, st) * e_cum
            dk_state = _dot(v, dst) * e_last
            dkc_s[...] = dk_state
            dvc_s[...] = _dot(k * e_last, dst, _NT)
            d_last = (jnp.sum(dst * st, axis=0, keepdims=True) * jnp.exp(last)
                      + jnp.sum(k * dk_state, axis=0, keepdims=True))
            for t in range(hc):
                ng = (t // SUBLANES + 1) * SUBLANES
                qt, dot_ = qc_s[t:t + 1, :], doc_s[t:t + 1, :]
                e = jnp.where(srow[:ng] <= t, jnp.exp(cum_s[t:t + 1, :] - cum[:ng]), 0.0)
                ke = k[:ng] * e
                d_a = jnp.sum(dot_ * v[:ng], axis=1, keepdims=True)
                dqc_s[t:t + 1, :] += jnp.sum(d_a * ke, axis=0, keepdims=True)
                dkc_s[0:ng, :] += d_a * (qt * e)
                dvc_s[0:ng, :] += jnp.sum(qt * ke, axis=1, keepdims=True) * dot_
            dq, dk = dqc_s[...], dkc_s[...]
            d_b = q * dq - k * dk
            dq_s[rows, :] = dq
            dk_s[rows, :] = dk
            dhi_ref[rows, :] = dvc_s[...].astype(BF16)
            dlf_s[rows, :] = _xdot_l(triu, d_b) + d_last
            return dst * jnp.exp(last) + _dot(do_c, q * e_cum, _TN)

        lax.fori_loop(0, n_chunks, chunk, jnp.zeros((d, d), F32))
        dhq, dhf, dlg = pre_vjp((dq_s[...], dk_s[...], dlf_s[...]))
        dhq_ref[...] = dhq.astype(BF16)
        dhf_ref[...] = dhf.astype(BF16)
        dlg_ref[...] = dlg

    width = n_heads * d
    head_block = pl.BlockSpec((s, d), lambda h: (0, h))
    return pl.pallas_call(
        body, name=name,
        grid=(n_heads,),
        in_specs=[pl.BlockSpec((2, d), lambda h: (0, h)), pl.BlockSpec((1, d), lambda h: (0, 0))]
        + _hg_specs(s, n_heads, 3 * n_heads)
        + [head_block, pl.BlockSpec((1, n_chunks, d, d), lambda h: (h, 0, 0, 0)),
           pl.BlockSpec((s, d), lambda h: (0, n_heads + h))],
        out_specs=[head_block] * 4 + [pl.BlockSpec((2, d), lambda h: (0, h)), pl.BlockSpec((1, d), lambda h: (0, 0))],
        out_shape=[jax.ShapeDtypeStruct((s, width), BF16)] * 4
        + [jax.ShapeDtypeStruct((2, width), F32), jax.ShapeDtypeStruct((1, d), F32)],
        scratch_shapes=[pltpu.VMEM((s, d), F32)] * 7 + [pltpu.VMEM((hc, d), F32)] * 6,
        compiler_params=_params("arbitrary"),
    )(logits, norm_g, proj, proj, proj, proj, oraw, states, dcat)


def _shift_down(x, n, srow):
    if n == 0:
        return x
    return jnp.where(srow >= n, pltpu.roll(x, n, 0), 0.0)


def _shift_up(x, n, srow):
    if n == 0:
        return x
    s = x.shape[0]
    return jnp.where(srow < s - n, pltpu.roll(x, s - n, 0), 0.0)


def _rg_gates_fwd(proj, conv_w, conv_b, wa, ba, wx, bx, *, name):
    s = proj.shape[0]
    nb = wa.shape[0]
    bw = RG_BLOCK

    def body(xb_ref, cw_ref, cb_ref, wa_ref, ba_ref, wx_ref, bx_ref, xc_ref, ra_ref, ix_ref):
        x = xb_ref[...]
        srow = _iota((s, bw), 0)
        cw = cw_ref[...]
        xc = cb_ref[...] + cw[0:1, :] * x
        for tap in range(1, CONV_TAPS):
            xc = xc + cw[tap:tap + 1, :] * _shift_down(x, tap, srow)
        xc_ref[...] = xc
        ra_ref[...] = _dot(xc, wa_ref[0]) + ba_ref[0]
        ix_ref[...] = _dot(xc, wx_ref[0]) + bx_ref[0]

    col = pl.BlockSpec((s, bw), lambda n: (0, n))
    vec = lambda r: pl.BlockSpec((r, bw), lambda n: (0, n))
    mat = pl.BlockSpec((1, bw, bw), lambda n: (n, 0, 0))
    bias = pl.BlockSpec((1, 1, bw), lambda n: (n, 0, 0))
    return pl.pallas_call(
        body, name=name,
        grid=(nb,),
        in_specs=[pl.BlockSpec((s, bw), lambda n: (0, nb + n)), vec(CONV_TAPS), vec(1), mat, bias, mat, bias],
        out_specs=[col] * 3,
        out_shape=[jax.ShapeDtypeStruct((s, nb * bw), F32)] * 3,
        compiler_params=_params("parallel"),
    )(proj, conv_w, conv_b, wa, ba, wx, bx)


def _rg_au(ra, ix, xc, lam, first_row):
    log_a = -RG_C * _sigmoid(ra) * _softplus(-lam)
    th = jnp.tanh(log_a)
    one_minus_a2 = -2.0 * th / (1.0 - th)
    mult = jnp.where(first_row, 1.0, jnp.sqrt(one_minus_a2))
    return jnp.exp(log_a), xc * _sigmoid(ix) * mult


def _rg_out(gate, hs):
    return _gelu(gate) * hs


def _linear_scan(a, b, a_s, b_s, in_s, reverse):
    s, c = a.shape
    within = _iota((s, c), 0) & (SUBLANES - 1)
    shift = 1
    while shift < SUBLANES:
        if reverse:
            take = within < SUBLANES - shift
            a_n, b_n = pltpu.roll(a, s - shift, 0), pltpu.roll(b, s - shift, 0)
        else:
            take = within >= shift
            a_n, b_n = pltpu.roll(a, shift, 0), pltpu.roll(b, shift, 0)
        b = jnp.where(take, a * b_n + b, b)
        a = jnp.where(take, a * a_n, a)
        shift *= 2
    a_s[...] = a
    b_s[...] = b
    n_tiles = s // SUBLANES
    edge = 0 if reverse else SUBLANES - 1

    def tile(i, h):
        rows = pl.ds(pl.multiple_of(((n_tiles - 1 - i) if reverse else i) * SUBLANES, SUBLANES), SUBLANES)
        in_s[rows, :] = jnp.broadcast_to(h, (SUBLANES, c))
        return a_s[rows, :][edge:edge + 1, :] * h + b_s[rows, :][edge:edge + 1, :]

    lax.fori_loop(0, n_tiles, tile, jnp.zeros((1, c), F32))
    return a * in_s[...] + b


def _rg_scan_fwd(proj, xc, ra, ix, lam, *, name):
    s, width = xc.shape
    tc = LANES

    def body(gate_ref, xc_ref, ra_ref, ix_ref, lam_ref, hs_ref, gact_ref, a_s, u_s, in_s):
        first_row = _iota((s, tc), 0) == 0
        a, u = _rg_au(ra_ref[...], ix_ref[...], xc_ref[...], lam_ref[...], first_row)
        hs = _linear_scan(a, u, a_s, u_s, in_s, reverse=False)
        hs_ref[...] = hs
        gact_ref[...] = _rg_out(gate_ref[...], hs).astype(BF16)

    col = pl.BlockSpec((s, tc), lambda n: (0, n))
    return pl.pallas_call(
        body, name=name,
        grid=(width // tc,),
        in_specs=[col, col, col, col, pl.BlockSpec((1, tc), lambda n: (0, n))],
        out_specs=[col, col],
        out_shape=[jax.ShapeDtypeStruct((s, width), F32), jax.ShapeDtypeStruct((s, width), BF16)],
        scratch_shapes=[pltpu.VMEM((s, tc), F32)] * 3,
        compiler_params=_params("parallel"),
    )(proj, xc, ra, ix, lam)


def _rg_scan_bwd(dgo, proj, hs, xc, ra, ix, lam, *, name):
    s, width = xc.shape
    tc = LANES

    def body(dgo_ref, gate_ref, hs_ref, xc_ref, ra_ref, ix_ref, lam_ref,
             dgate_ref, dra_ref, dix_ref, dxc_ref, dlam_ref, a_s, dh_s, g_s):
        srow = _iota((s, tc), 0)
        hs = hs_ref[...]
        _, out_vjp = jax.vjp(_rg_out, gate_ref[...], hs)
        dgate, dh = out_vjp(dgo_ref[...])
        dgate_ref[...] = dgate.astype(BF16)
        au = functools.partial(_rg_au, first_row=srow == 0)
        (a, _), au_vjp = jax.vjp(au, ra_ref[...], ix_ref[...], xc_ref[...], lam_ref[...])
        g = _linear_scan(_shift_up(a, 1, srow), dh, a_s, dh_s, g_s, reverse=True)
        dra, dix, dxc, dlam = au_vjp((g * _shift_down(hs, 1, srow), g))
        dra_ref[...] = dra.astype(BF16)
        dix_ref[...] = dix.astype(BF16)
        dxc_ref[...] = dxc
        dlam_ref[...] = dlam

    col = pl.BlockSpec((s, tc), lambda n: (0, n))
    vec = pl.BlockSpec((1, tc), lambda n: (0, n))
    return pl.pallas_call(
        body, name=name,
        grid=(width // tc,),
        in_specs=[col] * 6 + [vec],
        out_specs=[col] * 4 + [vec],
        out_shape=[jax.ShapeDtypeStruct((s, width), BF16)] * 3
        + [jax.ShapeDtypeStruct((s, width), F32), jax.ShapeDtypeStruct((1, width), F32)],
        scratch_shapes=[pltpu.VMEM((s, tc), F32)] * 3,
        compiler_params=_params("parallel"),
    )(dgo, proj, hs, xc, ra, ix, lam)


def _rg_gates_bwd(dra, dix, dxc1, xc, proj, conv_w, wa, wx, *, name):
    s = proj.shape[0]
    nb = wa.shape[0]
    bw = RG_BLOCK

    def body(dra_ref, dix_ref, dxc_ref, xc_ref, xb_ref, cw_ref, wa_ref, wx_ref,
             dxb_ref, dcw_ref, dcb_ref, dwa_ref, dba_ref, dwx_ref, dbx_ref):
        dra, dix = dra_ref[...], dix_ref[...]
        xc_t = xc_ref[...].T.astype(BF16)
        dwa_ref[0] = jnp.dot(xc_t, dra, preferred_element_type=F32)
        dwx_ref[0] = jnp.dot(xc_t, dix, preferred_element_type=F32)
        dba_ref[0] = jnp.sum(dra.astype(F32), axis=0, keepdims=True)
        dbx_ref[0] = jnp.sum(dix.astype(F32), axis=0, keepdims=True)
        dxc = dxc_ref[...] + _dot(dra, wa_ref[0], _NT) + _dot(dix, wx_ref[0], _NT)
        srow = _iota((s, bw), 0)
        x = xb_ref[...]
        cw = cw_ref[...]
        dx = cw[0:1, :] * dxc
        dcw = [jnp.sum(dxc * x, axis=0, keepdims=True)]
        for tap in range(1, CONV_TAPS):
            dx = dx + cw[tap:tap + 1, :] * _shift_up(dxc, tap, srow)
            dcw.append(jnp.sum(dxc * _shift_down(x, tap, srow), axis=0, keepdims=True))
        dxb_ref[...] = dx.astype(BF16)
        r4 = _iota((CONV_TAPS, bw), 0)
        acc = jnp.zeros((CONV_TAPS, bw), F32)
        for tap in range(CONV_TAPS):
            acc = jnp.where(r4 == tap, dcw[tap], acc)
        dcw_ref[...] = acc
        dcb_ref[...] = jnp.sum(dxc, axis=0, keepdims=True)

    col = pl.BlockSpec((s, bw), lambda n: (0, n))
    vec = lambda r: pl.BlockSpec((r, bw), lambda n: (0, n))
    mat = pl.BlockSpec((1, bw, bw), lambda n: (n, 0, 0))
    bias = pl.BlockSpec((1, 1, bw), lambda n: (n, 0, 0))
    width = nb * bw
    return pl.pallas_call(
        body, name=name,
        grid=(nb,),
        in_specs=[col, col, col, col, pl.BlockSpec((s, bw), lambda n: (0, nb + n)), vec(CONV_TAPS), mat, mat],
        out_specs=[col, vec(CONV_TAPS), vec(1), mat, bias, mat, bias],
        out_shape=[jax.ShapeDtypeStruct((s, width), BF16), jax.ShapeDtypeStruct((CONV_TAPS, width), F32),
                   jax.ShapeDtypeStruct((1, width), F32), jax.ShapeDtypeStruct((nb, bw, bw), F32),
                   jax.ShapeDtypeStruct((nb, 1, bw), F32), jax.ShapeDtypeStruct((nb, bw, bw), F32),
                   jax.ShapeDtypeStruct((nb, 1, bw), F32)],
        compiler_params=_params("parallel"),
    )(dra, dix, dxc1, xc, proj, conv_w, wa, wx)


_HBM = pl.BlockSpec(memory_space=pltpu.HBM)
_FLIPS = ((0, 0, 1), (1, 0, 0), (0, 1, 0), (1, 1, 0))
_ALL_FLIPS = tuple((a, b, c) for a in (0, 1) for b in (0, 1) for c in (0, 1))[1:]


def _flip(pos, f):
    return tuple(1 - p if b else p for p, b in zip(pos, f))


def _dev_index(pos):
    return 4 * pos[0] + 2 * pos[1] + pos[2]


def _block(ref, idx, cols):
    if not cols:
        return ref.at[idx]
    n = ref.shape[-1] // N_DEV
    start = pl.multiple_of(idx * n, LANES)
    return ref.at[(slice(None),) * (len(ref.shape) - 1) + (pl.ds(start, n),)]


_SEM = pl.BlockSpec(memory_space=pltpu.SEMAPHORE)
_ANY = pl.BlockSpec(memory_space=pl.ANY)
_N_PEERS = N_DEV - 1


def _hbm(x):
    return pltpu.with_memory_space_constraint(x, pltpu.HBM)


def _me():
    return lax.axis_index("x"), lax.axis_index("y"), lax.axis_index("c")


def _spread_copies(plan, src_refs, land_refs, send_sems, recv_sems, local_sems):
    local, remote = plan(src_refs, land_refs)
    local = [pltpu.make_async_copy(src, dst, local_sems.at[i]) for i, (src, dst) in enumerate(local)]
    remote = [pltpu.make_async_remote_copy(src_ref=src, dst_ref=dst, send_sem=send_sems.at[k], recv_sem=recv_sems.at[k],
                                           device_id=peer, device_id_type=pl.DeviceIdType.MESH)
              for k, (src, dst, peer) in enumerate(remote)]
    return local, remote


def _spread_start(srcs, lands, plan, n_remote, n_local, *, name):
    ns, nl = len(srcs), len(lands)

    def body(*refs):
        src_refs, land_refs = refs[:ns], refs[ns:ns + nl]
        send_sems, recv_sems, local_sems = refs[ns + nl:ns + nl + 3]
        local, remote = _spread_copies(plan, src_refs, land_refs, send_sems, recv_sems, local_sems)
        for cp in local + remote:
            cp.start()
        token = refs[-1]
        token[...] = jnp.zeros_like(token)

    lands = [_hbm(lax.empty(*x)) if isinstance(x, tuple) else x for x in lands]
    out = pl.pallas_call(
        body, name=name,
        in_specs=[_HBM] * (ns + nl),
        out_specs=[_SEM] * 3 + [_HBM] * (ns + nl) + [pl.BlockSpec(memory_space=pltpu.VMEM)],
        out_shape=[pltpu.SemaphoreType.DMA((n_remote,)), pltpu.SemaphoreType.DMA((n_remote,)),
                   pltpu.SemaphoreType.DMA((max(n_local, 1),))]
        + [pltpu.HBM(x.shape, x.dtype) for x in list(srcs) + lands]
        + [jax.ShapeDtypeStruct((SUBLANES, LANES), F32)],
        input_output_aliases={i: 3 + i for i in range(ns + nl)},
        compiler_params=pltpu.CompilerParams(has_side_effects=pltpu.SideEffectType.DATAFLOW_SIDE_EFFECTING),
    )(*[_hbm(x) for x in srcs], *lands)
    return dict(sems=list(out[:3]), srcs=list(out[3:3 + ns]), lands=list(out[3 + ns:3 + ns + nl]),
                token=out[-1], plan=plan)


def _spread_wait(handle, after, *, name):
    ns, nl = len(handle["srcs"]), len(handle["lands"])

    def body(*refs):
        src_refs, land_refs = refs[:ns], refs[ns:ns + nl]
        send_sems, recv_sems, local_sems = refs[ns + nl:ns + nl + 3]
        local, remote = _spread_copies(handle["plan"], src_refs, land_refs, send_sems, recv_sems, local_sems)
        for cp in local:
            cp.wait()
        for cp in remote:
            cp.wait_send()
            cp.wait_recv()

    out = pl.pallas_call(
        body, name=name,
        in_specs=[_HBM] * (ns + nl) + [_SEM] * 3 + [_ANY],
        out_specs=[_HBM] * (ns + nl),
        out_shape=[pltpu.HBM(x.shape, x.dtype) for x in handle["srcs"] + handle["lands"]],
        input_output_aliases={i: i for i in range(ns + nl)},
        compiler_params=pltpu.CompilerParams(has_side_effects=pltpu.SideEffectType.DATAFLOW_SIDE_EFFECTING),
    )(*handle["srcs"], *handle["lands"], *handle["sems"], after)
    return list(out[ns:])


def _gather_start(x, *, name, cols=False, relayed=True):
    shape = x.shape[:-1] + (N_DEV * x.shape[-1],) if cols else (N_DEV,) + x.shape
    flips = _FLIPS if relayed else _ALL_FLIPS

    def plan(src_refs, land_refs):
        me = _me()
        mine = _block(land_refs[0], _dev_index(me), cols)
        return [(src_refs[0], mine)], [(src_refs[0], mine, _flip(me, f)) for f in flips]

    handle = _spread_start([x], [(shape, x.dtype)], plan, len(flips), 1, name=name)
    handle["cols"] = cols
    return handle


def _gather_relay(handle, after, *, name):
    cols = handle["cols"]
    land, = _spread_wait(handle, after, name=f"{name}_arrived")

    def plan(src_refs, land_refs):
        me = _me()
        blocks = [_block(land_refs[0], _dev_index(_flip(me, f)), cols) for f in _FLIPS[1:]]
        return [], [(blk, blk, _flip(me, _FLIPS[0])) for blk in blocks]

    return _spread_start([], [land], plan, len(_FLIPS) - 1, 0, name=f"{name}_pass")


def _exchange_start(ps, *, name, cols=False):
    blk = ps[0].shape[:-1] + (ps[0].shape[-1] // N_DEV,) if cols else ps[0].shape[1:]

    def plan(src_refs, land_refs):
        me = _me()
        me_idx = _dev_index(me)
        local = [(_block(src, me_idx, cols), land_refs[0].at[me_idx, a]) for a, src in enumerate(src_refs)]
        remote = [(_block(src, _dev_index(_flip(me, f)), cols), land_refs[0].at[me_idx, a], _flip(me, f))
                  for f in _ALL_FLIPS for a, src in enumerate(src_refs)]
        return local, remote

    return _spread_start(ps, [((N_DEV, len(ps)) + blk, ps[0].dtype)], plan, _N_PEERS * len(ps), len(ps), name=name)


def _adamw(parts, w, m, v, *, name, layer=0, prev=None):
    n_rows, c = w.shape
    r = parts.shape[1]
    row_bytes = c * (N_DEV * parts.dtype.itemsize + 7 * 4) * 2
    tr = r
    for cand in (512, 256, 128, 64, 32, 16):
        if r % cand == 0 and cand * row_bytes <= ADAMW_BLOCK_BYTES:
            tr = cand
            break
    c1 = 1.0 - ADAM_B1 ** ADAM_STEP
    c2 = 1.0 - ADAM_B2 ** ADAM_STEP

    def body(p_ref, w_ref, m_ref, v_ref, *rest):
        g_ref, d_ref, nm_ref, nv_ref = rest[-4:]
        g = p_ref[0].astype(F32)
        for j in range(1, N_DEV):
            g = g + p_ref[j].astype(F32)
        nm = ADAM_B1 * m_ref[...] + (1.0 - ADAM_B1) * g
        nv = ADAM_B2 * v_ref[...] + (1.0 - ADAM_B2) * (g * g)
        g_ref[...] = g
        nm_ref[...] = nm
        nv_ref[...] = nv
        d_ref[...] = -ADAM_LR * ((nm * (1.0 / c1)) / (jnp.sqrt(nv * (1.0 / c2)) + ADAM_EPS) + ADAM_WD * w_ref[...])

    off = layer * (r // tr)
    blk = pl.BlockSpec((tr, c), lambda i: (i + off, 0))
    prev = list(prev) if prev is not None else []
    return pl.pallas_call(
        body, name=name,
        grid=(r // tr,),
        in_specs=[pl.BlockSpec((N_DEV, tr, c), lambda i: (0, i, 0)), blk, blk, blk] + [_ANY] * len(prev),
        out_specs=[blk] * 4,
        out_shape=[jax.ShapeDtypeStruct((n_rows, c), F32)] * 4,
        input_output_aliases={4 + j: j for j in range(len(prev))},
        compiler_params=_params("parallel"),
    )(parts, w, m, v, *prev)


_TN_CANDS = (512, 256, 128)
_TK_MAX = 5632
_TK_WHOLE_ROWS = 2816


def _contraction_tiles(m, k):
    tk = k
    while tk > _TK_MAX and tk % 2 == 0 and (tk // 2) % LANES == 0:
        tk //= 2
    tm = m if tk <= _TK_WHOLE_ROWS or m % 2 else m // 2
    return tm, tk


def _nn(a, b, name, out_dtype=F32):
    tm, tk = _contraction_tiles(*a.shape)
    return _mm(a, b, "nn", name=name, out_dtype=out_dtype, tm=tm, tn=_pick(b.shape[1], _TN_CANDS), tk=tk)


def _nt(a, b, name, out_dtype=F32):
    tm, tk = _contraction_tiles(*a.shape)
    return _mm(a, b, "nt", name=name, out_dtype=out_dtype, tm=tm, tn=_pick(b.shape[0], _TN_CANDS), tk=tk)


def _tn(a, b, name, out_dtype=BF16, deps=()):
    assert a.shape[0] == b.shape[0], (a.shape, b.shape)
    return _mm_tn(a, b, name=name, out_dtype=out_dtype, tm=_pick(a.shape[1], _TN_CANDS),
                  tn=_pick(b.shape[1], (1024,) + _TN_CANDS), deps=deps)


def _local_step(x, p, target, rep, weight, emit, n_heads, start_tokens=()):
    s, d = x.shape
    depth = p.shape[0]
    grads = {}
    rep_grads = {k: [None] * depth for k in ("mix_pre_g", "mix_post_g", "ffn_pre_g", "ffn_post_g", "ple_norm_g")}

    pending = list(start_tokens)
    gains = {}

    def gain(name, i):
        if (name, i) not in gains:
            gains[name, i] = rep[name][i:i + 1]
        return gains[name, i]

    def send(name, layer, g):
        token = emit(name, layer, g)
        if token is not None:
            pending.append(token)

    def rowcall(*args, **kwargs):
        deps, pending[:] = tuple(pending), []
        return _rowcall(*args, deps=deps, **kwargs)

    deferred = []

    def send_small(i, name, layer, a, b, mm_name):
        if i == 0 and depth > 1:
            deferred.append((name, layer, a, b, mm_name))
        else:
            send(name, layer, _tn(a, b, mm_name))

    saved = []
    h = x
    for i in range(depth):
        sv = {"h": h}
        n1, = rowcall(f"pre_norm{i}", lambda hh, g: _rms(hh, g), [h], [gain("mix_pre_g", i)], [BF16], cols=d)
        sv["n1"] = n1
        if i % 2 == 0:
            proj = _nn(n1, weight("w_in_even", 0, n1), f"in_even{i}")
            a_out = _sb_fwd(proj, n_heads, name=f"sb_fwd{i}")
            b_out, oraw, states = _hg_fwd(proj, rep["hg_lb_logits"], rep["hg_norm_g"], n_heads, name=f"hg_fwd{i}")
            cat = jnp.concatenate([a_out.astype(BF16), b_out], axis=1)
            m = _nn(cat, weight("w_out_even", 0, cat), f"out_even{i}")
            sv.update(proj=proj, oraw=oraw, states=states, cat=cat)
        else:
            proj = _nn(n1, weight("w_in_odd", 0, n1), f"in_odd{i}")
            sm = {k: weight(k, 0, proj) for k in _SMALL}
            xc, ra, ix = _rg_gates_fwd(proj, sm["conv_w"], sm["conv_b"], sm["rg_wa"], sm["rg_ba"],
                                       sm["rg_wx"], sm["rg_bx"], name=f"rg_gates_fwd{i}")
            hs, gact = _rg_scan_fwd(proj, xc, ra, ix, sm["rg_lambda"], name=f"rg_scan_fwd{i}")
            m = _nn(gact, weight("w_out_odd", 0, gact), f"out_odd{i}")
            sv.update(proj=proj, xc=xc, ra=ra, ix=ix, hs=hs, gact=gact, sm=sm)

        def post_mix(hh, mm, g_post, g_pre):
            h1 = hh + _rms(mm, g_post)
            return h1, _rms(h1, g_pre)

        h1, n2 = rowcall(f"post_mix{i}", post_mix, [h, m], [gain("mix_post_g", i), gain("ffn_pre_g", i)],
                          [F32, BF16], cols=d)
        gate, up, act = _gate_up(n2, weight("w_gate_up", i, n2), name=f"gate_up{i}")
        f = _nn(act, weight("w_down", i, act), f"down{i}")

        def post_ffn(hh, ff_out, g_post):
            h2 = hh + _rms(ff_out, g_post)
            return h2, h2

        h2, h2b = rowcall(f"post_ffn{i}", post_ffn, [h1, f], [gain("ffn_post_g", i)], [F32, BF16], cols=d)
        e = _nn(p[i], weight("w_ple_up", i, h2b), f"ple_up{i}")
        gl = _nn(h2b, weight("w_ple_gate", i, h2b), f"ple_gate{i}")
        h3, = rowcall(f"ple{i}", lambda hh, a, b, g: hh + _rms(_sigmoid(a) * b, g), [h2, gl, e],
                       [gain("ple_norm_g", i)], [F32], cols=d)
        sv.update(m=m, h1=h1, n2=n2, gate=gate, up=up, act=act, f=f, h2b=h2b, e=e, gl=gl)
        saved.append(sv)
        h = h3

    def loss_fn(y, t):
        err = y - t
        return err * (1.0 / d), jnp.sum(err * err, axis=0, keepdims=True) * (0.5 / d)

    dh, loss_cols = rowcall("loss", loss_fn, [h, target], [], [F32], red_rows=(1,), cols=d)

    for i in reversed(range(depth)):
        sv = saved[i]

        def ple_bwd(dy, a, b, g):
            _, vjp = jax.vjp(lambda a_, b_, g_: _rms(_sigmoid(a_) * b_, g_), a, b, g)
            return vjp(dy)

        dgl, de, rep_grads["ple_norm_g"][i] = rowcall(
            f"ple_bwd{i}", ple_bwd, [dh, sv["gl"], sv["e"]], [gain("ple_norm_g", i)], [BF16, BF16],
            red_rows=(1,), cols=d)
        send_small(i, "w_ple_up", i, p[i], de, f"d_ple_up{i}")
        send_small(i, "w_ple_gate", i, sv["h2b"], dgl, f"d_ple_gate{i}")
        dh2_ple = _nt(dgl, weight("w_ple_gate", i, dgl), f"dx_ple_gate{i}")

        def post_ffn_bwd(dy, dx, ff_out, g):
            dh2 = dy + dx
            _, vjp = jax.vjp(_rms, ff_out, g)
            df, dg = vjp(dh2)
            return dh2, df, dg

        dh2, df, rep_grads["ffn_post_g"][i] = rowcall(
            f"post_ffn_bwd{i}", post_ffn_bwd, [dh, dh2_ple, sv["f"]], [gain("ffn_post_g", i)], [F32, BF16],
            red_rows=(1,), cols=d)
        send("w_down", i, _tn(sv["act"], df, f"d_down{i}"))
        dact = _nt(df, weight("w_down", i, df), f"dx_down{i}", out_dtype=BF16)
        dgu = _swiglu_bwd(sv["gate"], sv["up"], dact, name=f"swiglu_bwd{i}")
        send("w_gate_up", i, _tn(sv["n2"], dgu, f"d_gate_up{i}"))
        dn2 = _nt(dgu, weight("w_gate_up", i, dgu), f"dx_gate_up{i}")

        def post_mix_bwd(dy, dn, h1, mm, g_post, g_pre):
            _, vjp_pre = jax.vjp(_rms, h1, g_pre)
            dh1_n, dg_pre = vjp_pre(dn)
            dh1 = dy + dh1_n
            _, vjp_post = jax.vjp(_rms, mm, g_post)
            dm, dg_post = vjp_post(dh1)
            return dh1, dm, dg_pre, dg_post

        dh1, dm, rep_grads["ffn_pre_g"][i], rep_grads["mix_post_g"][i] = rowcall(
            f"post_mix_bwd{i}", post_mix_bwd, [dh2, dn2, sv["h1"], sv["m"]],
            [gain("mix_post_g", i), gain("ffn_pre_g", i)], [F32, BF16], red_rows=(1, 1), cols=d)

        if i % 2 == 0:
            send_small(i, "w_out_even", 0, sv["cat"], dm, f"d_out_even{i}")
            dcat = _nt(dm, weight("w_out_even", 0, dm), f"dx_out_even{i}")
            dq, dk, dv = _sb_bwd(sv["proj"], dcat, n_heads, name=f"sb_bwd{i}")
            dhq, dhf, dhi, dhg, grads["hg_lb_logits"], grads["hg_norm_g"] = _hg_bwd(
                sv["proj"], rep["hg_lb_logits"], rep["hg_norm_g"], sv["oraw"], sv["states"], dcat, n_heads,
                name=f"hg_bwd{i}")
            dproj = jnp.concatenate([dq.astype(BF16), dk.astype(BF16), dv.astype(BF16), dhq, dhf, dhi, dhg], axis=1)
            send("w_in_even", 0, _tn(sv["n1"], dproj, f"d_in_even{i}"))
            dn1 = _nt(dproj, weight("w_in_even", 0, dproj), f"dx_in_even{i}")
        else:
            sm = sv["sm"]
            send_small(i, "w_out_odd", 0, sv["gact"], dm, f"d_out_odd{i}")
            dgo = _nt(dm, weight("w_out_odd", 0, dm), f"dx_out_odd{i}")
            dgate, dra, dix, dxc1, grads["rg_lambda"] = _rg_scan_bwd(
                dgo, sv["proj"], sv["hs"], sv["xc"], sv["ra"], sv["ix"], sm["rg_lambda"], name=f"rg_scan_bwd{i}")
            (dxb, grads["conv_w"], grads["conv_b"], grads["rg_wa"], grads["rg_ba"], grads["rg_wx"],
             grads["rg_bx"]) = _rg_gates_bwd(dra, dix, dxc1, sv["xc"], sv["proj"], sm["conv_w"], sm["rg_wa"],
                                            sm["rg_wx"], name=f"rg_gates_bwd{i}")
            send("small", 0, {k: grads.pop(k) for k in _SMALL})
            dproj = jnp.concatenate([dgate, dxb], axis=1)
            send("w_in_odd", 0, _tn(sv["n1"], dproj, f"d_in_odd{i}"))
            dn1 = _nt(dproj, weight("w_in_odd", 0, dproj), f"dx_in_odd{i}")

        def pre_norm_bwd(dy, dn, hh, g):
            _, vjp = jax.vjp(_rms, hh, g)
            dx, dg = vjp(dn)
            return dy + dx, dg

        dh, rep_grads["mix_pre_g"][i] = rowcall(
            f"pre_norm_bwd{i}", pre_norm_bwd, [dh1, dn1, sv["h"]], [gain("mix_pre_g", i)], [F32],
            red_rows=(1,), cols=d)

    for name, layer, a, b, mm_name in deferred:
        send(name, layer, _tn(a, b, mm_name, deps=(dh,)))
    for k, rows in rep_grads.items():
        grads[k] = jnp.concatenate(rows, axis=0)
    return loss_cols, dh, grads


_WEIGHTS = ("mix_pre_g", "mix_post_g", "ffn_pre_g", "ffn_post_g", "ple_norm_g", "w_in_even", "w_out_even",
            "hg_lb_logits", "hg_norm_g", "w_in_odd", "conv_w", "conv_b", "rg_wa", "rg_ba", "rg_wx", "rg_bx",
            "rg_lambda", "w_out_odd", "w_gate_up", "w_down", "w_ple_up", "w_ple_gate")
_REPLICATED = ("mix_pre_g", "mix_post_g", "ffn_pre_g", "ffn_post_g", "ple_norm_g", "hg_lb_logits", "hg_norm_g")
_SMALL = ("conv_w", "conv_b", "rg_wa", "rg_ba", "rg_wx", "rg_bx", "rg_lambda")
_BIG = {"w_in_even": True, "w_out_even": False, "w_in_odd": True, "w_out_odd": False,
        "w_gate_up": True, "w_down": False, "w_ple_up": True, "w_ple_gate": False}
_PACK_ROW = SUBLANES * LANES


def _pack(arrays):
    flat = jnp.concatenate([a.reshape(-1) for a in arrays])
    pad = -flat.shape[0] % _PACK_ROW
    return jnp.pad(flat, (0, pad)).reshape(-1, LANES)


def _pack_blocks(arrays):
    flat = jnp.concatenate([a.reshape(N_DEV, -1) for a in arrays], axis=1)
    pad = -flat.shape[1] % _PACK_ROW
    return jnp.pad(flat, ((0, 0), (0, pad))).reshape(N_DEV, -1, LANES)


def _unpack(packed, shapes, lead=()):
    flat = packed.reshape(lead + (-1,))
    out, pos = [], 0
    for shape in shapes:
        n = math.prod(shape)
        out.append(flat[..., pos:pos + n].reshape(lead + tuple(shape)))
        pos += n
    return out


def _to_full_small(name, blocks):
    if name == "conv_w":
        return jnp.transpose(blocks, (1, 0, 2)).reshape(blocks.shape[1], -1)
    if name in ("conv_b", "rg_lambda"):
        return blocks.reshape(1, -1)
    nb = blocks.shape[1]
    if name in ("rg_wa", "rg_wx"):
        return jnp.transpose(blocks, (1, 0, 2, 3)).reshape(nb, RG_BLOCK, RG_BLOCK)
    return jnp.transpose(blocks, (1, 0, 2)).reshape(nb, 1, RG_BLOCK)


def _to_blocks_small(name, full):
    if name == "conv_w":
        return jnp.transpose(full.reshape(full.shape[0], N_DEV, -1), (1, 0, 2))
    if name in ("conv_b", "rg_lambda"):
        return full.reshape(N_DEV, -1)
    nb = full.shape[0]
    if name in ("rg_wa", "rg_wx"):
        return jnp.transpose(full.reshape(nb, N_DEV, RG_BLOCK // N_DEV, RG_BLOCK), (1, 0, 2, 3))
    return jnp.transpose(full.reshape(nb, N_DEV, RG_BLOCK // N_DEV), (1, 0, 2))


def _step(inp):
    w = {k: inp[k] for k in _WEIGHTS}
    x, p, target = inp["x"][0], inp["p"][:, 0], inp["loss_target"][0]
    assert w["hg_lb_logits"].shape[0] == 2 and w["w_in_even"].shape[0] == 1 and w["w_in_odd"].shape[0] == 1

    n_heads = w["w_in_even"].shape[2] * N_DEV // (7 * HEAD_DIM)
    small_shapes = [w[k].shape[1:] for k in _SMALL]

    def lands_in_place(name):
        return _BIG[name] and w[name].shape[2] % LANES == 0

    depth = p.shape[0]
    order = [("w_in_even", 0), ("w_out_even", 0)] if depth else []
    for i in range(depth):
        if i == 1:
            order += [("w_in_odd", 0), ("small", 0), ("w_out_odd", 0)]
        order += [("w_gate_up", i), ("w_down", i), ("w_ple_up", i), ("w_ple_gate", i)]
    heavy = ("w_gate_up", "w_down", "w_in_odd", "w_out_odd")
    gathers = {}
    for name, l in sorted(order, key=lambda key: key[0] in heavy):
        if name == "small":
            gathers[name, l] = _gather_start(_pack([w[k][0] for k in _SMALL]), name="gather_small", relayed=False)
        else:
            gathers[name, l] = _gather_start(w[name][l].astype(BF16), name=f"gather_{name}{l}",
                                             cols=lands_in_place(name))
    ready = {}

    def relay(key, after):
        if "cols" in gathers[key] and key[0] != "small":
            gathers[key] = _gather_relay(gathers[key], after, name=f"gather_{key[0]}{key[1]}")

    def weight(name, layer, after):
        key = ("small", 0) if name in _SMALL else (name, layer)
        if key not in ready:
            relay(key, after)
            for nxt in order[order.index(key) + 1:order.index(key) + 2]:
                relay(nxt, after)
            land, = _spread_wait(gathers[key], after, name=f"gathered_{key[0]}{key[1]}")
            if name in _SMALL:
                ready[key] = {k: _to_full_small(k, b)
                              for k, b in zip(_SMALL, _unpack(land, small_shapes, lead=(N_DEV,)))}
            elif lands_in_place(name):
                ready[key] = land
            elif _BIG[name]:
                ready[key] = jnp.transpose(land, (1, 0, 2)).reshape(land.shape[1], -1)
            else:
                ready[key] = land.reshape(-1, land.shape[2])
        return ready[key][name] if name in _SMALL else ready[key]

    exchanges = []

    def emit(name, layer, g):
        if name == "small":
            handle = _exchange_start([_pack_blocks([_to_blocks_small(k, g[k]) for k in _SMALL])],
                                     name="exchange_small")
            exchanges.append((name, layer, handle))
            return handle["token"]
        _, r, c = w[name].shape
        if lands_in_place(name):
            handle = _exchange_start([g], name=f"exchange_{name}{layer}", cols=True)
        elif _BIG[name]:
            handle = _exchange_start([jnp.transpose(g.reshape(-1, N_DEV, c), (1, 0, 2))],
                                     name=f"exchange_{name}{layer}")
        else:
            handle = _exchange_start([g.reshape(N_DEV, r, c)], name=f"exchange_{name}{layer}")
        exchanges.append((name, layer, handle))
        return handle["token"]

    rep = {k: w[k] for k in _REPLICATED}
    loss_cols, dx, grads = _local_step(x, p, target, rep, weight, emit, n_heads,
                                       [h["token"] for h in gathers.values()])

    loss_part = jnp.sum(loss_cols).reshape(1)
    rep_gather = _gather_start(_pack([grads[k] for k in _REPLICATED] + [loss_part]), name="gather_rep_grads",
                               relayed=False)

    out = {}
    after = dx
    for name, layer, handle in exchanges:
        land, = _spread_wait(handle, after, name=f"exchanged_{name}{layer}")
        if name == "small":
            res = _adamw(land.reshape(N_DEV, -1, LANES), *[_pack([inp[pre + k][0] for k in _SMALL]) for pre in ("", "m_", "v_")],
                         name="adamw_small")
            for k, *vals in zip(_SMALL, *[_unpack(a, small_shapes) for a in res]):
                out[k] = [v[None] for v in vals]
        else:
            n_l, r, c = w[name].shape
            res = out[name] = _adamw(land.reshape(N_DEV, r, c),
                                     *[inp[pre + name].reshape(n_l * r, c) for pre in ("", "m_", "v_")],
                                     name=f"adamw_{name}{layer}", layer=layer, prev=out.get(name))
        after = res[0]
    for name in _BIG:
        out[name] = [a.reshape(w[name].shape) for a in out[name]]

    rep_shapes = [w[k].shape for k in _REPLICATED] + [(1,)]
    rep_parts, = _spread_wait(rep_gather, after, name="gathered_rep_grads")
    res = _adamw(rep_parts, *[_pack([inp[pre + k] for k in _REPLICATED] + [jnp.zeros((1,), F32)])
                              for pre in ("", "m_", "v_")], name="adamw_rep")
    for k, *vals in zip(_REPLICATED + ("loss",), *[_unpack(a, rep_shapes) for a in res]):
        out[k] = vals
    loss = out["loss"][0][0]

    return (loss, dx[None]) + tuple(out[k][j] for j in range(4) for k in _WEIGHTS)


def kernel(x, p, mix_pre_g, mix_post_g, ffn_pre_g, ffn_post_g, ple_norm_g, w_in_even, w_out_even, hg_lb_logits, hg_norm_g, w_in_odd, conv_w, conv_b, rg_wa, rg_ba, rg_wx, rg_bx, rg_lambda, w_out_odd, w_gate_up, w_down, w_ple_up, w_ple_gate, loss_target, m_mix_pre_g, m_mix_post_g, m_ffn_pre_g, m_ffn_post_g, m_ple_norm_g, m_w_in_even, m_w_out_even, m_hg_lb_logits, m_hg_norm_g, m_w_in_odd, m_conv_w, m_conv_b, m_rg_wa, m_rg_ba, m_rg_wx, m_rg_bx, m_rg_lambda, m_w_out_odd, m_w_gate_up, m_w_down, m_w_ple_up, m_w_ple_gate, v_mix_pre_g, v_mix_post_g, v_ffn_pre_g, v_ffn_post_g, v_ple_norm_g, v_w_in_even, v_w_out_even, v_hg_lb_logits, v_hg_norm_g, v_w_in_odd, v_conv_w, v_conv_b, v_rg_wa, v_rg_ba, v_rg_wx, v_rg_bx, v_rg_lambda, v_w_out_odd, v_w_gate_up, v_w_down, v_w_ple_up, v_w_ple_gate):
    return _step(dict(locals()))
```

```python
import functools
import math

import jax
import jax.numpy as jnp
from jax import lax
from jax.experimental import pallas as pl
from jax.experimental.pallas import tpu as pltpu

F32 = jnp.float32
BF16 = jnp.bfloat16

VMEM_LIMIT_BYTES = 56 * 1024 * 1024
ADAMW_BLOCK_BYTES = 40 * 1024 * 1024
LANES = 128
SUBLANES = 8

N_DEV = 8
HEAD_DIM = 128
SB_Q_TILE = 512
SB_K_TILE = 128
HG_CHUNK = 32
HG_HEADS_PER_STEP = 2
RG_BLOCK = 256
CONV_TAPS = 4
RG_C = 8.0
RMS_EPS = 1e-6

ADAM_LR = 0.001
ADAM_B1 = 0.9
ADAM_B2 = 0.999
ADAM_EPS = 1e-08
ADAM_WD = 0.01
ADAM_STEP = 10


def _params(*sem):
    return pltpu.CompilerParams(dimension_semantics=sem, vmem_limit_bytes=VMEM_LIMIT_BYTES)


def _pick(n, cands):
    for c in cands:
        if c <= n and n % c == 0:
            return c
    return n


def _mm(a, b, mode, *, name, out_dtype=F32, tm=512, tn=512, tk=None):
    if mode == "nn":
        (m, k), (k2, n) = a.shape, b.shape
    else:
        (m, k), (n, k2) = a.shape, b.shape
    assert k == k2, (a.shape, b.shape, mode)
    tm, tn = min(tm, m), min(tn, n)
    tk = k if tk is None else min(tk, k)
    assert m % tm == 0 and n % tn == 0 and k % tk == 0, (m, n, k, tm, tn, tk)
    nk = k // tk

    a_spec = pl.BlockSpec((tm, tk), lambda i, j, kk: (i, kk))
    if mode == "nn":
        b_spec = pl.BlockSpec((tk, tn), lambda i, j, kk: (kk, j))
        dims = (((1,), (0,)), ((), ()))
    else:
        b_spec = pl.BlockSpec((tn, tk), lambda i, j, kk: (j, kk))
        dims = (((1,), (1,)), ((), ()))

    def body(a_ref, b_ref, o_ref, *acc):
        part = lax.dot_general(a_ref[...].astype(BF16), b_ref[...].astype(BF16), dims, preferred_element_type=F32)
        if nk == 1:
            o_ref[...] = part.astype(out_dtype)
        else:
            acc_ref, = acc
            kk = pl.program_id(2)

            @pl.when(kk == 0)
            def _():
                acc_ref[...] = part

            @pl.when(kk > 0)
            def _():
                acc_ref[...] += part

            @pl.when(kk == nk - 1)
            def _():
                o_ref[...] = acc_ref[...].astype(out_dtype)

    return pl.pallas_call(
        body, name=name,
        grid=(m // tm, n // tn, nk),
        in_specs=[a_spec, b_spec],
        out_specs=pl.BlockSpec((tm, tn), lambda i, j, kk: (i, j)),
        out_shape=jax.ShapeDtypeStruct((m, n), out_dtype),
        scratch_shapes=[] if nk == 1 else [pltpu.VMEM((tm, tn), F32)],
        compiler_params=_params("parallel", "parallel", "arbitrary"),
    )(a, b)


def _mm_tn(a, b, *, name, out_dtype, tm, tn, deps=()):
    k, m = a.shape
    n = b.shape[1]

    def body(a_ref, b_ref, *refs):
        o_ref, at_ref = refs[len(deps):]

        @pl.when(pl.program_id(1) == 0)
        def _():
            at_ref[...] = a_ref[...].astype(F32).T.astype(BF16)

        o_ref[...] = jnp.dot(at_ref[...], b_ref[...].astype(BF16), preferred_element_type=F32).astype(out_dtype)

    return pl.pallas_call(
        body, name=name,
        grid=(m // tm, n // tn),
        in_specs=[pl.BlockSpec((k, tm), lambda i, j: (0, i)), pl.BlockSpec((k, tn), lambda i, j: (0, j))]
        + [pl.BlockSpec(memory_space=pl.ANY)] * len(deps),
        out_specs=pl.BlockSpec((tm, tn), lambda i, j: (i, j)),
        out_shape=jax.ShapeDtypeStruct((m, n), out_dtype),
        scratch_shapes=[pltpu.VMEM((tm, k), BF16)],
        compiler_params=_params("parallel", "arbitrary"),
    )(a, b, *deps)


def _rowcall(name, fn, rows, pars, row_outs, red_rows=(), *, cols, ts=256, tc=None, deps=()):
    rows = [r if isinstance(r, tuple) else (r, 0) for r in rows]
    pars = [p if isinstance(p, tuple) else (p, 0) for p in pars]
    s = rows[0][0].shape[0]
    tc = cols if tc is None else tc
    ts = min(ts, s)
    assert s % ts == 0 and cols % tc == 0, (name, s, ts, cols, tc)
    n_in, n_row_out = len(rows) + len(pars), len(row_outs)

    def body(*refs):
        outs = fn(*[r[...] for r in refs[:n_in]])
        outs = outs if isinstance(outs, (tuple, list)) else (outs,)
        o_refs = refs[n_in + len(deps):]
        for o_ref, val in zip(o_refs[:n_row_out], outs[:n_row_out]):
            o_ref[...] = val.astype(o_ref.dtype)
        first = pl.program_id(1) == 0
        for o_ref, val in zip(o_refs[n_row_out:], outs[n_row_out:]):
            @pl.when(first)
            def _(o_ref=o_ref, val=val):
                o_ref[...] = val

            @pl.when(jnp.logical_not(first))
            def _(o_ref=o_ref, val=val):
                o_ref[...] += val

    def row_map(off):
        return lambda j, i: (i, j + off)

    def par_map(off):
        return lambda j, i: (0, j + off)

    return pl.pallas_call(
        body, name=name,
        grid=(cols // tc, s // ts),
        in_specs=[pl.BlockSpec((ts, tc), row_map(off)) for _, off in rows]
        + [pl.BlockSpec((p.shape[0], tc), par_map(off)) for p, off in pars]
        + [pl.BlockSpec(memory_space=pl.ANY)] * len(deps),
        out_specs=[pl.BlockSpec((ts, tc), lambda j, i: (i, j)) for _ in row_outs]
        + [pl.BlockSpec((r, tc), lambda j, i: (0, j)) for r in red_rows],
        out_shape=[jax.ShapeDtypeStruct((s, cols), dt) for dt in row_outs]
        + [jax.ShapeDtypeStruct((r, cols), F32) for r in red_rows],
        compiler_params=_params("parallel", "arbitrary"),
    )(*[r for r, _ in rows], *[p for p, _ in pars], *deps)


def _swiglu_act(g, u):
    return _silu(g) * u


def _gate_up(n, w, *, name):
    s, d = n.shape
    f = w.shape[1] // 2
    tn = _pick(f, (256, 128))
    nj = f // tn

    def body(a_ref, wg_ref, wu_ref, g_ref, u_ref, act_ref):
        a = a_ref[...].astype(BF16)
        g = jnp.dot(a, wg_ref[...].astype(BF16), preferred_element_type=F32)
        u = jnp.dot(a, wu_ref[...].astype(BF16), preferred_element_type=F32)
        g_ref[...] = g.astype(BF16)
        u_ref[...] = u.astype(BF16)
        act_ref[...] = _swiglu_act(g, u).astype(BF16)

    out = pl.BlockSpec((s, tn), lambda j: (0, j))
    return pl.pallas_call(
        body, name=name,
        grid=(nj,),
        in_specs=[pl.BlockSpec((s, d), lambda j: (0, 0)), pl.BlockSpec((d, tn), lambda j: (0, j)),
                  pl.BlockSpec((d, tn), lambda j: (0, nj + j))],
        out_specs=[out] * 3,
        out_shape=[jax.ShapeDtypeStruct((s, f), BF16)] * 3,
        compiler_params=_params("parallel"),
    )(n, w, w)


def _swiglu_bwd(g, u, dact, *, name, ts=128):
    s, f = g.shape
    ts = min(ts, s)

    def body(g_ref, u_ref, dact_ref, o_ref):
        _, vjp = jax.vjp(_swiglu_act, g_ref[...].astype(F32), u_ref[...].astype(F32))
        dg, du = vjp(dact_ref[...].astype(F32))
        o_ref[:, 0:f] = dg.astype(BF16)
        o_ref[:, f:2 * f] = du.astype(BF16)

    narrow = pl.BlockSpec((ts, f), lambda i: (i, 0))
    return pl.pallas_call(
        body, name=name,
        grid=(s // ts,),
        in_specs=[narrow] * 3,
        out_specs=pl.BlockSpec((ts, 2 * f), lambda i: (i, 0)),
        out_shape=jax.ShapeDtypeStruct((s, 2 * f), BF16),
        compiler_params=_params("parallel"),
    )(g, u, dact)


def _rms(x, g):
    return x * lax.rsqrt(jnp.mean(x * x, axis=-1, keepdims=True) + RMS_EPS) * g


def _sigmoid(x):
    return jax.nn.sigmoid(x)


def _silu(x):
    return x * jax.nn.sigmoid(x)


def _gelu(x):
    return 0.5 * x * (1.0 + jnp.tanh(math.sqrt(2.0 / math.pi) * (x + 0.044715 * (x * x * x))))


def _softplus(x):
    return jnp.maximum(x, 0.0) + jnp.log1p(jnp.exp(-jnp.abs(x)))


def _split(x, terms):
    parts = []
    for _ in range(terms - 1):
        parts.append(x.astype(BF16))
        x = x - parts[-1].astype(F32)
    return parts + [x.astype(BF16)]


def _xdot(x, t, terms=3):
    return sum(jnp.dot(p, t, preferred_element_type=F32) for p in _split(x, terms))


def _xdot_l(t, x):
    return sum(jnp.dot(t, p, preferred_element_type=F32) for p in _split(x, 3))


_NT = (((1,), (1,)), ((), ()))
_TN = (((0,), (0,)), ((), ()))


def _dot(a, b, dims=None):
    if dims is None:
        return jnp.dot(a.astype(BF16), b.astype(BF16), preferred_element_type=F32)
    return lax.dot_general(a.astype(BF16), b.astype(BF16), dims, preferred_element_type=F32)


def _iota(shape, axis):
    return lax.broadcasted_iota(jnp.int32, shape, axis)


def _sb_tile(qb, kblk, mask, upper, c_rem):
    z = lax.dot_general(qb, kblk, _NT, preferred_element_type=F32)
    soft = jnp.log1p(jnp.exp(-jnp.abs(z)))
    lbeta = jnp.minimum(z, 0.0) - soft
    l1m = -jnp.maximum(z, 0.0) - soft
    if mask is not None:
        l1m = jnp.where(mask, l1m, 0.0)
    rem = _xdot(l1m, upper, terms=2) + c_rem
    w = jnp.exp(lbeta + rem)
    if mask is not None:
        w = jnp.where(mask, w, 0.0)
    return lbeta, l1m, w


def _sb_tiles(s):
    tq = min(SB_Q_TILE, s)
    return tq, SB_K_TILE, tq // SB_K_TILE


def _sb_key_loops(qi, per_q, step, carry):
    n_full = qi * per_q
    carry = lax.fori_loop(0, per_q, lambda j, c: step(n_full + per_q - 1 - j, True, c), carry)
    return lax.fori_loop(0, n_full, lambda j, c: step(n_full - 1 - j, False, c), carry)


def _sb_fwd(proj, n_heads, *, name):
    s = proj.shape[0]
    t, tk, per_q = _sb_tiles(s)
    scale = HEAD_DIM ** -0.5

    def body(q_ref, k_ref, v_ref, o_ref):
        qi = pl.program_id(1)
        qb = (q_ref[...] * scale).astype(BF16)
        row, col = _iota((t, tk), 0) + qi * t, _iota((t, tk), 1)
        upper = (_iota((tk, tk), 0) > _iota((tk, tk), 1)).astype(BF16)

        def step(kb, masked, carry):
            acc, c_rem = carry
            rows = pl.ds(pl.multiple_of(kb * tk, tk), tk)
            kblk = k_ref[rows, :].astype(BF16)
            vblk = v_ref[rows, :].astype(BF16)
            _, l1m, w = _sb_tile(qb, kblk, (col + kb * tk) < row if masked else None, upper, c_rem)
            acc = acc + jnp.dot(w.astype(BF16), vblk, preferred_element_type=F32)
            return acc, c_rem + jnp.sum(l1m, axis=1, keepdims=True)

        acc, _ = _sb_key_loops(qi, per_q, step, (jnp.zeros((t, HEAD_DIM), F32), jnp.zeros((t, 1), F32)))
        o_ref[...] = acc

    return pl.pallas_call(
        body, name=name,
        grid=(n_heads, s // t),
        in_specs=[pl.BlockSpec((t, HEAD_DIM), lambda h, i: (i, h)),
                  pl.BlockSpec((s, HEAD_DIM), lambda h, i: (0, n_heads + h)),
                  pl.BlockSpec((s, HEAD_DIM), lambda h, i: (0, 2 * n_heads + h))],
        out_specs=pl.BlockSpec((t, HEAD_DIM), lambda h, i: (i, h)),
        out_shape=jax.ShapeDtypeStruct((s, n_heads * HEAD_DIM), F32),
        compiler_params=_params("parallel", "arbitrary"),
    )(proj, proj, proj)


def _sb_bwd(proj, dcat, n_heads, *, name):
    s = proj.shape[0]
    t, tk, per_q = _sb_tiles(s)
    scale = HEAD_DIM ** -0.5

    def body(q_ref, k_ref, v_ref, do_ref, dq_ref, dk_ref, dv_ref, g_s, sig_s):
        qi = pl.program_id(1)

        @pl.when(qi == 0)
        def _():
            dk_ref[...] = jnp.zeros_like(dk_ref)
            dv_ref[...] = jnp.zeros_like(dv_ref)

        qb = (q_ref[...] * scale).astype(BF16)
        dob = do_ref[...].astype(BF16)
        row, col = _iota((t, tk), 0) + qi * t, _iota((t, tk), 1)
        upper = (_iota((tk, tk), 0) > _iota((tk, tk), 1)).astype(BF16)
        lower_incl = (_iota((tk, tk), 0) >= _iota((tk, tk), 1)).astype(BF16)

        def weights(kb, masked, carry):
            c_rem, g_all = carry
            rows = pl.ds(pl.multiple_of(kb * tk, tk), tk)
            kblk = k_ref[rows, :].astype(BF16)
            vblk = v_ref[rows, :].astype(BF16)
            lbeta, l1m, w = _sb_tile(qb, kblk, (col + kb * tk) < row if masked else None, upper, c_rem)
            g = w * lax.dot_general(dob, vblk, _NT, preferred_element_type=F32)
            dv_ref[rows, :] += lax.dot_general(w.astype(BF16), dob, _TN, preferred_element_type=F32)
            g_s[kb] = g
            sig_s[kb] = jnp.exp(lbeta)
            return c_rem + jnp.sum(l1m, axis=1, keepdims=True), g_all + jnp.sum(g, axis=1, keepdims=True)

        zero_col = jnp.zeros((t, 1), F32)
        _, g_all = _sb_key_loops(qi, per_q, weights, (zero_col, zero_col))

        def scores(kb, masked, carry):
            dq, c_g = carry
            rows = pl.ds(pl.multiple_of(kb * tk, tk), tk)
            g, sig = g_s[kb], sig_s[kb]
            g_before = g_all - (_xdot(g, lower_incl) + c_g)
            dz = g * (1.0 - sig) - g_before * sig
            if masked:
                dz = jnp.where((col + kb * tk) < row, dz, 0.0)
            dz = dz.astype(BF16)
            dq = dq + jnp.dot(dz, k_ref[rows, :].astype(BF16), preferred_element_type=F32)
            dk_ref[rows, :] += lax.dot_general(dz, qb, _TN, preferred_element_type=F32)
            return dq, c_g + jnp.sum(g, axis=1, keepdims=True)

        dq, _ = _sb_key_loops(qi, per_q, scores, (jnp.zeros((t, HEAD_DIM), F32), zero_col))
        dq_ref[...] = dq * scale

    width = n_heads * HEAD_DIM
    return pl.pallas_call(
        body, name=name,
        grid=(n_heads, s // t),
        in_specs=[pl.BlockSpec((t, HEAD_DIM), lambda h, i: (i, h)),
                  pl.BlockSpec((s, HEAD_DIM), lambda h, i: (0, n_heads + h)),
                  pl.BlockSpec((s, HEAD_DIM), lambda h, i: (0, 2 * n_heads + h)),
                  pl.BlockSpec((t, HEAD_DIM), lambda h, i: (i, h))],
        out_specs=[pl.BlockSpec((t, HEAD_DIM), lambda h, i: (i, h)),
                   pl.BlockSpec((s, HEAD_DIM), lambda h, i: (0, h)),
                   pl.BlockSpec((s, HEAD_DIM), lambda h, i: (0, h))],
        out_shape=[jax.ShapeDtypeStruct((s, width), F32)] * 3,
        scratch_shapes=[pltpu.VMEM((s // tk, t, tk), F32)] * 2,
        compiler_params=_params("parallel", "arbitrary"),
    )(proj, proj, proj, dcat)


def _hg_pre(hq, hf, logits):
    mx = jnp.max(logits, axis=0, keepdims=True)
    ex = jnp.exp(logits - mx)
    lb = ex[0:1, :] / jnp.sum(ex, axis=0, keepdims=True)
    f = lb + (1.0 - lb) * _sigmoid(hf)
    return _silu(hq), 1.0 - f, jnp.log(f)


def _hg_post(o, norm_g, hgate):
    return _rms(o, norm_g) * _silu(hgate)


def _hg_specs(s, n_heads, first_block):
    def at(group):
        return pl.BlockSpec((s, HEAD_DIM), lambda h: (0, first_block + group * n_heads + h))
    return [at(0), at(1), at(2), at(3)]


def _hg_fwd(proj, logits, norm_g, n_heads, *, name):
    s = proj.shape[0]
    hc = HG_CHUNK
    n_chunks = s // hc
    d = HEAD_DIM

    hp = HG_HEADS_PER_STEP
    assert n_heads % hp == 0
    wide = hp * d

    def body(lg_ref, ng_ref, hq_ref, hf_ref, hi_ref, hgt_ref, out_ref, oraw_ref, st_ref,
             q_s, k_s, lf_s, cum_s, qc_s, oc_s):
        q, k, lf = _hg_pre(hq_ref[...], hf_ref[...], lg_ref[...])
        q_s[...] = q
        k_s[...] = k
        lf_s[...] = lf
        tril = (_iota((hc, hc), 0) >= _iota((hc, hc), 1)).astype(BF16)
        srow = _iota((hc, d), 0)

        def head_chunk(j, ci, rows, st):
            ln = slice(j * d, (j + 1) * d)
            q, k, v = q_s[rows, ln], k_s[rows, ln], hi_ref[rows, ln]
            cum = _xdot_l(tril, lf_s[rows, ln])
            st_ref[j, ci] = st
            o_inter = _dot(q * jnp.exp(cum), st, _NT)
            cum_s[:, ln] = cum
            qc_s[:, ln] = q
            for t in range(hc):
                ng = (t // SUBLANES + 1) * SUBLANES
                e = jnp.where(srow[:ng] <= t, jnp.exp(cum_s[t:t + 1, ln] - cum[:ng]), 0.0)
                sc = jnp.sum(qc_s[t:t + 1, ln] * k[:ng] * e, axis=1, keepdims=True)
                oc_s[t:t + 1, ln] = jnp.sum(sc * v[:ng], axis=0, keepdims=True)
            oraw_ref[rows, ln] = o_inter + oc_s[:, ln]
            last = cum_s[hc - 1:hc, ln]
            return st * jnp.exp(last) + _dot(v, k * jnp.exp(last - cum), _TN)

        def chunk(ci, states):
            rows = pl.ds(pl.multiple_of(ci * hc, hc), hc)
            return tuple(head_chunk(j, ci, rows, st) for j, st in enumerate(states))

        lax.fori_loop(0, n_chunks, chunk, tuple(jnp.zeros((d, d), F32) for _ in range(hp)))
        for j in range(hp):
            ln = slice(j * d, (j + 1) * d)
            out_ref[:, ln] = _hg_post(oraw_ref[:, ln], ng_ref[...], hgt_ref[:, ln]).astype(BF16)

    width = n_heads * d
    first = 3 * n_heads // hp
    groups = [pl.BlockSpec((s, wide), functools.partial(lambda h, g: (0, first + g * (n_heads // hp) + h), g=g))
              for g in range(4)]
    head_block = pl.BlockSpec((s, wide), lambda h: (0, h))
    return pl.pallas_call(
        body, name=name,
        grid=(n_heads // hp,),
        in_specs=[pl.BlockSpec((2, wide), lambda h: (0, h)), pl.BlockSpec((1, d), lambda h: (0, 0))] + groups,
        out_specs=[head_block, head_block, pl.BlockSpec((hp, n_chunks, d, d), lambda h: (h, 0, 0, 0))],
        out_shape=[jax.ShapeDtypeStruct((s, width), BF16), jax.ShapeDtypeStruct((s, width), F32),
                   jax.ShapeDtypeStruct((n_heads, n_chunks, d, d), F32)],
        scratch_shapes=[pltpu.VMEM((s, wide), F32)] * 3 + [pltpu.VMEM((hc, wide), F32)] * 3,
        compiler_params=_params("arbitrary"),
    )(logits, norm_g, proj, proj, proj, proj)


def _hg_bwd(proj, logits, norm_g, oraw, states, dcat, n_heads, *, name):
    s = proj.shape[0]
    hc = HG_CHUNK
    n_chunks = s // hc
    d = HEAD_DIM

    def body(lg_ref, ng_ref, hq_ref, hf_ref, hi_ref, hgt_ref, oraw_ref, st_ref, dout_ref,
             dhq_ref, dhf_ref, dhi_ref, dhgt_ref, dlg_ref, dng_ref,
             q_s, k_s, lf_s, do_s, dq_s, dk_s, dlf_s, cum_s, qc_s, doc_s, dqc_s, dkc_s, dvc_s):
        head = pl.program_id(0)
        (q, k, lf), pre_vjp = jax.vjp(_hg_pre, hq_ref[...], hf_ref[...], lg_ref[...])
        q_s[...] = q
        k_s[...] = k
        lf_s[...] = lf
        _, post_vjp = jax.vjp(_hg_post, oraw_ref[...], ng_ref[...], hgt_ref[...])
        do, dng, dhgt = post_vjp(dout_ref[...])
        do_s[...] = do
        dhgt_ref[...] = dhgt.astype(BF16)

        @pl.when(head == 0)
        def _():
            dng_ref[...] = dng

        @pl.when(head > 0)
        def _():
            dng_ref[...] += dng

        triu = (_iota((hc, hc), 0) <= _iota((hc, hc), 1)).astype(BF16)
        tril = (_iota((hc, hc), 0) >= _iota((hc, hc), 1)).astype(BF16)
        srow = _iota((hc, d), 0)

        def chunk(j, dst):
            ci = n_chunks - 1 - j
            rows = pl.ds(pl.multiple_of(ci * hc, hc), hc)
            q, k, v, do_c = q_s[rows, :], k_s[rows, :], hi_ref[rows, :], do_s[rows, :]
            cum = _xdot_l(tril, lf_s[rows, :])
            st = st_ref[0, ci]
            cum_s[...] = cum
            qc_s[...] = q
            doc_s[...] = do_c
            last = cum_s[hc - 1:hc, :]
            e_cum, e_last = jnp.exp(cum), jnp.exp(last - cum)
            dqc_s[...] = _dot(do_c, st) * e_cum
            dk_state = _dot(v, dst) * e_last
            dkc_s[...] = dk_state
            dvc_s[...] = _dot(k * e_last, dst, _NT)
            d_last = (jnp.sum(dst * st, axis=0, keepdims=True) * jnp.exp(last)
                      + jnp.sum(k * dk_state, axis=0, keepdims=True))
            for t in range(hc):
                ng = (t // SUBLANES + 1) * SUBLANES
                qt, dot_ = qc_s[t:t + 1, :], doc_s[t:t + 1, :]
                e = jnp.where(srow[:ng] <= t, jnp.exp(cum_s[t:t + 1, :] - cum[:ng]), 0.0)
                ke = k[:ng] * e
                d_a = jnp.sum(dot_ * v[:ng], axis=1, keepdims=True)
                dqc_s[t:t + 1, :] += jnp.sum(d_a * ke, axis=0, keepdims=True)
                dkc_s[0:ng, :] += d_a * (qt * e)
                dvc_s[0:ng, :] += jnp.sum(qt * ke, axis=1, keepdims=True) * dot_
            dq, dk = dqc_s[...], dkc_s[...]
            d_b = q * dq - k * dk
            dq_s[rows, :] = dq
            dk_s[rows, :] = dk
            dhi_ref[rows, :] = dvc_s[...].astype(BF16)
            dlf_s[rows, :] = _xdot_l(triu, d_b) + d_last
            return dst * jnp.exp(last) + _dot(do_c, q * e_cum, _TN)

        lax.fori_loop(0, n_chunks, chunk, jnp.zeros((d, d), F32))
        dhq, dhf, dlg = pre_vjp((dq_s[...], dk_s[...], dlf_s[...]))
        dhq_ref[...] = dhq.astype(BF16)
        dhf_ref[...] = dhf.astype(BF16)
        dlg_ref[...] = dlg

    width = n_heads * d
    head_block = pl.BlockSpec((s, d), lambda h: (0, h))
    return pl.pallas_call(
        body, name=name,
        grid=(n_heads,),
        in_specs=[pl.BlockSpec((2, d), lambda h: (0, h)), pl.BlockSpec((1, d), lambda h: (0, 0))]
        + _hg_specs(s, n_heads, 3 * n_heads)
        + [head_block, pl.BlockSpec((1, n_chunks, d, d), lambda h: (h, 0, 0, 0)),
           pl.BlockSpec((s, d), lambda h: (0, n_heads + h))],
        out_specs=[head_block] * 4 + [pl.BlockSpec((2, d), lambda h: (0, h)), pl.BlockSpec((1, d), lambda h: (0, 0))],
        out_shape=[jax.ShapeDtypeStruct((s, width), BF16)] * 4
        + [jax.ShapeDtypeStruct((2, width), F32), jax.ShapeDtypeStruct((1, d), F32)],
        scratch_shapes=[pltpu.VMEM((s, d), F32)] * 7 + [pltpu.VMEM((hc, d), F32)] * 6,
        compiler_params=_params("arbitrary"),
    )(logits, norm_g, proj, proj, proj, proj, oraw, states, dcat)


def _shift_down(x, n, srow):
    if n == 0:
        return x
    return jnp.where(srow >= n, pltpu.roll(x, n, 0), 0.0)


def _shift_up(x, n, srow):
    if n == 0:
        return x
    s = x.shape[0]
    return jnp.where(srow < s - n, pltpu.roll(x, s - n, 0), 0.0)


def _rg_gates_fwd(proj, conv_w, conv_b, wa, ba, wx, bx, *, name):
    s = proj.shape[0]
    nb = wa.shape[0]
    bw = RG_BLOCK

    def body(xb_ref, cw_ref, cb_ref, wa_ref, ba_ref, wx_ref, bx_ref, xc_ref, ra_ref, ix_ref):
        x = xb_ref[...]
        srow = _iota((s, bw), 0)
        cw = cw_ref[...]
        xc = cb_ref[...] + cw[0:1, :] * x
        for tap in range(1, CONV_TAPS):
            xc = xc + cw[tap:tap + 1, :] * _shift_down(x, tap, srow)
        xc_ref[...] = xc
        ra_ref[...] = _dot(xc, wa_ref[0]) + ba_ref[0]
        ix_ref[...] = _dot(xc, wx_ref[0]) + bx_ref[0]

    col = pl.BlockSpec((s, bw), lambda n: (0, n))
    vec = lambda r: pl.BlockSpec((r, bw), lambda n: (0, n))
    mat = pl.BlockSpec((1, bw, bw), lambda n: (n, 0, 0))
    bias = pl.BlockSpec((1, 1, bw), lambda n: (n, 0, 0))
    return pl.pallas_call(
        body, name=name,
        grid=(nb,),
        in_specs=[pl.BlockSpec((s, bw), lambda n: (0, nb + n)), vec(CONV_TAPS), vec(1), mat, bias, mat, bias],
        out_specs=[col] * 3,
        out_shape=[jax.ShapeDtypeStruct((s, nb * bw), F32)] * 3,
        compiler_params=_params("parallel"),
    )(proj, conv_w, conv_b, wa, ba, wx, bx)


def _rg_au(ra, ix, xc, lam, first_row):
    log_a = -RG_C * _sigmoid(ra) * _softplus(-lam)
    th = jnp.tanh(log_a)
    one_minus_a2 = -2.0 * th / (1.0 - th)
    mult = jnp.where(first_row, 1.0, jnp.sqrt(one_minus_a2))
    return jnp.exp(log_a), xc * _sigmoid(ix) * mult


def _rg_out(gate, hs):
    return _gelu(gate) * hs


def _linear_scan(a, b, a_s, b_s, in_s, reverse):
    s, c = a.shape
    within = _iota((s, c), 0) & (SUBLANES - 1)
    shift = 1
    while shift < SUBLANES:
        if reverse:
            take = within < SUBLANES - shift
            a_n, b_n = pltpu.roll(a, s - shift, 0), pltpu.roll(b, s - shift, 0)
        else:
            take = within >= shift
            a_n, b_n = pltpu.roll(a, shift, 0), pltpu.roll(b, shift, 0)
        b = jnp.where(take, a * b_n + b, b)
        a = jnp.where(take, a * a_n, a)
        shift *= 2
    a_s[...] = a
    b_s[...] = b
    n_tiles = s // SUBLANES
    edge = 0 if reverse else SUBLANES - 1

    def tile(i, h):
        rows = pl.ds(pl.multiple_of(((n_tiles - 1 - i) if reverse else i) * SUBLANES, SUBLANES), SUBLANES)
        in_s[rows, :] = jnp.broadcast_to(h, (SUBLANES, c))
        return a_s[rows, :][edge:edge + 1, :] * h + b_s[rows, :][edge:edge + 1, :]

    lax.fori_loop(0, n_tiles, tile, jnp.zeros((1, c), F32))
    return a * in_s[...] + b


def _rg_scan_fwd(proj, xc, ra, ix, lam, *, name):
    s, width = xc.shape
    tc = LANES

    def body(gate_ref, xc_ref, ra_ref, ix_ref, lam_ref, hs_ref, gact_ref, a_s, u_s, in_s):
        first_row = _iota((s, tc), 0) == 0
        a, u = _rg_au(ra_ref[...], ix_ref[...], xc_ref[...], lam_ref[...], first_row)
        hs = _linear_scan(a, u, a_s, u_s, in_s, reverse=False)
        hs_ref[...] = hs
        gact_ref[...] = _rg_out(gate_ref[...], hs).astype(BF16)

    col = pl.BlockSpec((s, tc), lambda n: (0, n))
    return pl.pallas_call(
        body, name=name,
        grid=(width // tc,),
        in_specs=[col, col, col, col, pl.BlockSpec((1, tc), lambda n: (0, n))],
        out_specs=[col, col],
        out_shape=[jax.ShapeDtypeStruct((s, width), F32), jax.ShapeDtypeStruct((s, width), BF16)],
        scratch_shapes=[pltpu.VMEM((s, tc), F32)] * 3,
        compiler_params=_params("parallel"),
    )(proj, xc, ra, ix, lam)


def _rg_scan_bwd(dgo, proj, hs, xc, ra, ix, lam, *, name):
    s, width = xc.shape
    tc = LANES

    def body(dgo_ref, gate_ref, hs_ref, xc_ref, ra_ref, ix_ref, lam_ref,
             dgate_ref, dra_ref, dix_ref, dxc_ref, dlam_ref, a_s, dh_s, g_s):
        srow = _iota((s, tc), 0)
        hs = hs_ref[...]
        _, out_vjp = jax.vjp(_rg_out, gate_ref[...], hs)
        dgate, dh = out_vjp(dgo_ref[...])
        dgate_ref[...] = dgate.astype(BF16)
        au = functools.partial(_rg_au, first_row=srow == 0)
        (a, _), au_vjp = jax.vjp(au, ra_ref[...], ix_ref[...], xc_ref[...], lam_ref[...])
        g = _linear_scan(_shift_up(a, 1, srow), dh, a_s, dh_s, g_s, reverse=True)
        dra, dix, dxc, dlam = au_vjp((g * _shift_down(hs, 1, srow), g))
        dra_ref[...] = dra.astype(BF16)
        dix_ref[...] = dix.astype(BF16)
        dxc_ref[...] = dxc
        dlam_ref[...] = dlam

    col = pl.BlockSpec((s, tc), lambda n: (0, n))
    vec = pl.BlockSpec((1, tc), lambda n: (0, n))
    return pl.pallas_call(
        body, name=name,
        grid=(width // tc,),
        in_specs=[col] * 6 + [vec],
        out_specs=[col] * 4 + [vec],
        out_shape=[jax.ShapeDtypeStruct((s, width), BF16)] * 3
        + [jax.ShapeDtypeStruct((s, width), F32), jax.ShapeDtypeStruct((1, width), F32)],
        scratch_shapes=[pltpu.VMEM((s, tc), F32)] * 3,
        compiler_params=_params("parallel"),
    )(dgo, proj, hs, xc, ra, ix, lam)


def _rg_gates_bwd(dra, dix, dxc1, xc, proj, conv_w, wa, wx, *, name):
    s = proj.shape[0]
    nb = wa.shape[0]
    bw = RG_BLOCK

    def body(dra_ref, dix_ref, dxc_ref, xc_ref, xb_ref, cw_ref, wa_ref, wx_ref,
             dxb_ref, dcw_ref, dcb_ref, dwa_ref, dba_ref, dwx_ref, dbx_ref):
        dra, dix = dra_ref[...], dix_ref[...]
        xc_t = xc_ref[...].T.astype(BF16)
        dwa_ref[0] = jnp.dot(xc_t, dra, preferred_element_type=F32)
        dwx_ref[0] = jnp.dot(xc_t, dix, preferred_element_type=F32)
        dba_ref[0] = jnp.sum(dra.astype(F32), axis=0, keepdims=True)
        dbx_ref[0] = jnp.sum(dix.astype(F32), axis=0, keepdims=True)
        dxc = dxc_ref[...] + _dot(dra, wa_ref[0], _NT) + _dot(dix, wx_ref[0], _NT)
        srow = _iota((s, bw), 0)
        x = xb_ref[...]
        cw = cw_ref[...]
        dx = cw[0:1, :] * dxc
        dcw = [jnp.sum(dxc * x, axis=0, keepdims=True)]
        for tap in range(1, CONV_TAPS):
            dx = dx + cw[tap:tap + 1, :] * _shift_up(dxc, tap, srow)
            dcw.append(jnp.sum(dxc * _shift_down(x, tap, srow), axis=0, keepdims=True))
        dxb_ref[...] = dx.astype(BF16)
        r4 = _iota((CONV_TAPS, bw), 0)
        acc = jnp.zeros((CONV_TAPS, bw), F32)
        for tap in range(CONV_TAPS):
            acc = jnp.where(r4 == tap, dcw[tap], acc)
        dcw_ref[...] = acc
        dcb_ref[...] = jnp.sum(dxc, axis=0, keepdims=True)

    col = pl.BlockSpec((s, bw), lambda n: (0, n))
    vec = lambda r: pl.BlockSpec((r, bw), lambda n: (0, n))
    mat = pl.BlockSpec((1, bw, bw), lambda n: (n, 0, 0))
    bias = pl.BlockSpec((1, 1, bw), lambda n: (n, 0, 0))
    width = nb * bw
    return pl.pallas_call(
        body, name=name,
        grid=(nb,),
        in_specs=[col, col, col, col, pl.BlockSpec((s, bw), lambda n: (0, nb + n)), vec(CONV_TAPS), mat, mat],
        out_specs=[col, vec(CONV_TAPS), vec(1), mat, bias, mat, bias],
        out_shape=[jax.ShapeDtypeStruct((s, width), BF16), jax.ShapeDtypeStruct((CONV_TAPS, width), F32),
                   jax.ShapeDtypeStruct((1, width), F32), jax.ShapeDtypeStruct((nb, bw, bw), F32),
                   jax.ShapeDtypeStruct((nb, 1, bw), F32), jax.ShapeDtypeStruct((nb, bw, bw), F32),
                   jax.ShapeDtypeStruct((nb, 1, bw), F32)],
        compiler_params=_params("parallel"),
    )(dra, dix, dxc1, xc, proj, conv_w, wa, wx)


_HBM = pl.BlockSpec(memory_space=pltpu.HBM)
_FLIPS = ((0, 0, 1), (1, 0, 0), (0, 1, 0), (1, 1, 0))
_ALL_FLIPS = tuple((a, b, c) for a in (0, 1) for b in (0, 1) for c in (0, 1))[1:]


def _flip(pos, f):
    return tuple(1 - p if b else p for p, b in zip(pos, f))


def _dev_index(pos):
    return 4 * pos[0] + 2 * pos[1] + pos[2]


def _block(ref, idx, cols):
    if not cols:
        return ref.at[idx]
    n = ref.shape[-1] // N_DEV
    start = pl.multiple_of(idx * n, LANES)
    return ref.at[(slice(None),) * (len(ref.shape) - 1) + (pl.ds(start, n),)]


_SEM = pl.BlockSpec(memory_space=pltpu.SEMAPHORE)
_ANY = pl.BlockSpec(memory_space=pl.ANY)
_N_PEERS = N_DEV - 1


def _hbm(x):
    return pltpu.with_memory_space_constraint(x, pltpu.HBM)


def _me():
    return lax.axis_index("x"), lax.axis_index("y"), lax.axis_index("c")


def _spread_copies(plan, src_refs, land_refs, send_sems, recv_sems, local_sems):
    local, remote = plan(src_refs, land_refs)
    local = [pltpu.make_async_copy(src, dst, local_sems.at[i]) for i, (src, dst) in enumerate(local)]
    remote = [pltpu.make_async_remote_copy(src_ref=src, dst_ref=dst, send_sem=send_sems.at[k], recv_sem=recv_sems.at[k],
                                           device_id=peer, device_id_type=pl.DeviceIdType.MESH)
              for k, (src, dst, peer) in enumerate(remote)]
    return local, remote


def _spread_start(srcs, lands, plan, n_remote, n_local, *, name):
    ns, nl = len(srcs), len(lands)

    def body(*refs):
        src_refs, land_refs = refs[:ns], refs[ns:ns + nl]
        send_sems, recv_sems, local_sems = refs[ns + nl:ns + nl + 3]
        local, remote = _spread_copies(plan, src_refs, land_refs, send_sems, recv_sems, local_sems)
        for cp in local + remote:
            cp.start()
        token = refs[-1]
        token[...] = jnp.zeros_like(token)

    lands = [_hbm(lax.empty(*x)) if isinstance(x, tuple) else x for x in lands]
    out = pl.pallas_call(
        body, name=name,
        in_specs=[_HBM] * (ns + nl),
        out_specs=[_SEM] * 3 + [_HBM] * (ns + nl) + [pl.BlockSpec(memory_space=pltpu.VMEM)],
        out_shape=[pltpu.SemaphoreType.DMA((n_remote,)), pltpu.SemaphoreType.DMA((n_remote,)),
                   pltpu.SemaphoreType.DMA((max(n_local, 1),))]
        + [pltpu.HBM(x.shape, x.dtype) for x in list(srcs) + lands]
        + [jax.ShapeDtypeStruct((SUBLANES, LANES), F32)],
        input_output_aliases={i: 3 + i for i in range(ns + nl)},
        compiler_params=pltpu.CompilerParams(has_side_effects=pltpu.SideEffectType.DATAFLOW_SIDE_EFFECTING),
    )(*[_hbm(x) for x in srcs], *lands)
    return dict(sems=list(out[:3]), srcs=list(out[3:3 + ns]), lands=list(out[3 + ns:3 + ns + nl]),
                token=out[-1], plan=plan)


def _spread_wait(handle, after, *, name):
    ns, nl = len(handle["srcs"]), len(handle["lands"])

    def body(*refs):
        src_refs, land_refs = refs[:ns], refs[ns:ns + nl]
        send_sems, recv_sems, local_sems = refs[ns + nl:ns + nl + 3]
        local, remote = _spread_copies(handle["plan"], src_refs, land_refs, send_sems, recv_sems, local_sems)
        for cp in local:
            cp.wait()
        for cp in remote:
            cp.wait_send()
            cp.wait_recv()

    out = pl.pallas_call(
        body, name=name,
        in_specs=[_HBM] * (ns + nl) + [_SEM] * 3 + [_ANY],
        out_specs=[_HBM] * (ns + nl),
        out_shape=[pltpu.HBM(x.shape, x.dtype) for x in handle["srcs"] + handle["lands"]],
        input_output_aliases={i: i for i in range(ns + nl)},
        compiler_params=pltpu.CompilerParams(has_side_effects=pltpu.SideEffectType.DATAFLOW_SIDE_EFFECTING),
    )(*handle["srcs"], *handle["lands"], *handle["sems"], after)
    return list(out[ns:])


def _gather_start(x, *, name, cols=False, relayed=True):
    shape = x.shape[:-1] + (N_DEV * x.shape[-1],) if cols else (N_DEV,) + x.shape
    flips = _FLIPS if relayed else _ALL_FLIPS

    def plan(src_refs, land_refs):
        me = _me()
        mine = _block(land_refs[0], _dev_index(me), cols)
        return [(src_refs[0], mine)], [(src_refs[0], mine, _flip(me, f)) for f in flips]

    handle = _spread_start([x], [(shape, x.dtype)], plan, len(flips), 1, name=name)
    handle["cols"] = cols
    return handle


def _gather_relay(handle, after, *, name):
    cols = handle["cols"]
    land, = _spread_wait(handle, after, name=f"{name}_arrived")

    def plan(src_refs, land_refs):
        me = _me()
        blocks = [_block(land_refs[0], _dev_index(_flip(me, f)), cols) for f in _FLIPS[1:]]
        return [], [(blk, blk, _flip(me, _FLIPS[0])) for blk in blocks]

    return _spread_start([], [land], plan, len(_FLIPS) - 1, 0, name=f"{name}_pass")


def _exchange_start(ps, *, name, cols=False):
    blk = ps[0].shape[:-1] + (ps[0].shape[-1] // N_DEV,) if cols else ps[0].shape[1:]

    def plan(src_refs, land_refs):
        me = _me()
        me_idx = _dev_index(me)
        local = [(_block(src, me_idx, cols), land_refs[0].at[me_idx, a]) for a, src in enumerate(src_refs)]
        remote = [(_block(src, _dev_index(_flip(me, f)), cols), land_refs[0].at[me_idx, a], _flip(me, f))
                  for f in _ALL_FLIPS for a, src in enumerate(src_refs)]
        return local, remote

    return _spread_start(ps, [((N_DEV, len(ps)) + blk, ps[0].dtype)], plan, _N_PEERS * len(ps), len(ps), name=name)


def _adamw(parts, w, m, v, *, name, layer=0, prev=None):
    n_rows, c = w.shape
    r = parts.shape[1]
    row_bytes = c * (N_DEV * parts.dtype.itemsize + 7 * 4) * 2
    tr = r
    for cand in (512, 256, 128, 64, 32, 16):
        if r % cand == 0 and cand * row_bytes <= ADAMW_BLOCK_BYTES:
            tr = cand
            break
    c1 = 1.0 - ADAM_B1 ** ADAM_STEP
    c2 = 1.0 - ADAM_B2 ** ADAM_STEP

    def body(p_ref, w_ref, m_ref, v_ref, *rest):
        g_ref, d_ref, nm_ref, nv_ref = rest[-4:]
        g = p_ref[0].astype(F32)
        for j in range(1, N_DEV):
            g = g + p_ref[j].astype(F32)
        nm = ADAM_B1 * m_ref[...] + (1.0 - ADAM_B1) * g
        nv = ADAM_B2 * v_ref[...] + (1.0 - ADAM_B2) * (g * g)
        g_ref[...] = g
        nm_ref[...] = nm
        nv_ref[...] = nv
        d_ref[...] = -ADAM_LR * ((nm * (1.0 / c1)) / (jnp.sqrt(nv * (1.0 / c2)) + ADAM_EPS) + ADAM_WD * w_ref[...])

    off = layer * (r // tr)
    blk = pl.BlockSpec((tr, c), lambda i: (i + off, 0))
    prev = list(prev) if prev is not None else []
    return pl.pallas_call(
        body, name=name,
        grid=(r // tr,),
        in_specs=[pl.BlockSpec((N_DEV, tr, c), lambda i: (0, i, 0)), blk, blk, blk] + [_ANY] * len(prev),
        out_specs=[blk] * 4,
        out_shape=[jax.ShapeDtypeStruct((n_rows, c), F32)] * 4,
        input_output_aliases={4 + j: j for j in range(len(prev))},
        compiler_params=_params("parallel"),
    )(parts, w, m, v, *prev)


_TN_CANDS = (512, 256, 128)
_TK_MAX = 5632
_TK_WHOLE_ROWS = 2816


def _contraction_tiles(m, k):
    tk = k
    while tk > _TK_MAX and tk % 2 == 0 and (tk // 2) % LANES == 0:
        tk //= 2
    tm = m if tk <= _TK_WHOLE_ROWS or m % 2 else m // 2
    return tm, tk


def _nn(a, b, name, out_dtype=F32):
    tm, tk = _contraction_tiles(*a.shape)
    return _mm(a, b, "nn", name=name, out_dtype=out_dtype, tm=tm, tn=_pick(b.shape[1], _TN_CANDS), tk=tk)


def _nt(a, b, name, out_dtype=F32):
    tm, tk = _contraction_tiles(*a.shape)
    return _mm(a, b, "nt", name=name, out_dtype=out_dtype, tm=tm, tn=_pick(b.shape[0], _TN_CANDS), tk=tk)


def _tn(a, b, name, out_dtype=BF16, deps=()):
    assert a.shape[0] == b.shape[0], (a.shape, b.shape)
    return _mm_tn(a, b, name=name, out_dtype=out_dtype, tm=_pick(a.shape[1], _TN_CANDS),
                  tn=_pick(b.shape[1], (1024,) + _TN_CANDS), deps=deps)


def _local_step(x, p, target, rep, weight, emit, n_heads, start_tokens=()):
    s, d = x.shape
    depth = p.shape[0]
    grads = {}
    rep_grads = {k: [None] * depth for k in ("mix_pre_g", "mix_post_g", "ffn_pre_g", "ffn_post_g", "ple_norm_g")}

    pending = list(start_tokens)
    gains = {}

    def gain(name, i):
        if (name, i) not in gains:
            gains[name, i] = rep[name][i:i + 1]
        return gains[name, i]

    def send(name, layer, g):
        token = emit(name, layer, g)
        if token is not None:
            pending.append(token)

    def rowcall(*args, **kwargs):
        deps, pending[:] = tuple(pending), []
        return _rowcall(*args, deps=deps, **kwargs)

    deferred = []

    def send_small(i, name, layer, a, b, mm_name):
        if i == 0 and depth > 1:
            deferred.append((name, layer, a, b, mm_name))
        else:
            send(name, layer, _tn(a, b, mm_name))

    saved = []
    h = x
    for i in range(depth):
        sv = {"h": h}
        n1, = rowcall(f"pre_norm{i}", lambda hh, g: _rms(hh, g), [h], [gain("mix_pre_g", i)], [BF16], cols=d)
        sv["n1"] = n1
        if i % 2 == 0:
            proj = _nn(n1, weight("w_in_even", 0, n1), f"in_even{i}")
            a_out = _sb_fwd(proj, n_heads, name=f"sb_fwd{i}")
            b_out, oraw, states = _hg_fwd(proj, rep["hg_lb_logits"], rep["hg_norm_g"], n_heads, name=f"hg_fwd{i}")
            cat = jnp.concatenate([a_out.astype(BF16), b_out], axis=1)
            m = _nn(cat, weight("w_out_even", 0, cat), f"out_even{i}")
            sv.update(proj=proj, oraw=oraw, states=states, cat=cat)
        else:
            proj = _nn(n1, weight("w_in_odd", 0, n1), f"in_odd{i}")
            sm = {k: weight(k, 0, proj) for k in _SMALL}
            xc, ra, ix = _rg_gates_fwd(proj, sm["conv_w"], sm["conv_b"], sm["rg_wa"], sm["rg_ba"],
                                       sm["rg_wx"], sm["rg_bx"], name=f"rg_gates_fwd{i}")
            hs, gact = _rg_scan_fwd(proj, xc, ra, ix, sm["rg_lambda"], name=f"rg_scan_fwd{i}")
            m = _nn(gact, weight("w_out_odd", 0, gact), f"out_odd{i}")
            sv.update(proj=proj, xc=xc, ra=ra, ix=ix, hs=hs, gact=gact, sm=sm)

        def post_mix(hh, mm, g_post, g_pre):
            h1 = hh + _rms(mm, g_post)
            return h1, _rms(h1, g_pre)

        h1, n2 = rowcall(f"post_mix{i}", post_mix, [h, m], [gain("mix_post_g", i), gain("ffn_pre_g", i)],
                          [F32, BF16], cols=d)
        gate, up, act = _gate_up(n2, weight("w_gate_up", i, n2), name=f"gate_up{i}")
        f = _nn(act, weight("w_down", i, act), f"down{i}")

        def post_ffn(hh, ff_out, g_post):
            h2 = hh + _rms(ff_out, g_post)
            return h2, h2

        h2, h2b = rowcall(f"post_ffn{i}", post_ffn, [h1, f], [gain("ffn_post_g", i)], [F32, BF16], cols=d)
        e = _nn(p[i], weight("w_ple_up", i, h2b), f"ple_up{i}")
        gl = _nn(h2b, weight("w_ple_gate", i, h2b), f"ple_gate{i}")
        h3, = rowcall(f"ple{i}", lambda hh, a, b, g: hh + _rms(_sigmoid(a) * b, g), [h2, gl, e],
                       [gain("ple_norm_g", i)], [F32], cols=d)
        sv.update(m=m, h1=h1, n2=n2, gate=gate, up=up, act=act, f=f, h2b=h2b, e=e, gl=gl)
        saved.append(sv)
        h = h3

    def loss_fn(y, t):
        err = y - t
        return err * (1.0 / d), jnp.sum(err * err, axis=0, keepdims=True) * (0.5 / d)

    dh, loss_cols = rowcall("loss", loss_fn, [h, target], [], [F32], red_rows=(1,), cols=d)

    for i in reversed(range(depth)):
        sv = saved[i]

        def ple_bwd(dy, a, b, g):
            _, vjp = jax.vjp(lambda a_, b_, g_: _rms(_sigmoid(a_) * b_, g_), a, b, g)
            return vjp(dy)

        dgl, de, rep_grads["ple_norm_g"][i] = rowcall(
            f"ple_bwd{i}", ple_bwd, [dh, sv["gl"], sv["e"]], [gain("ple_norm_g", i)], [BF16, BF16],
            red_rows=(1,), cols=d)
        send_small(i, "w_ple_up", i, p[i], de, f"d_ple_up{i}")
        send_small(i, "w_ple_gate", i, sv["h2b"], dgl, f"d_ple_gate{i}")
        dh2_ple = _nt(dgl, weight("w_ple_gate", i, dgl), f"dx_ple_gate{i}")

        def post_ffn_bwd(dy, dx, ff_out, g):
            dh2 = dy + dx
            _, vjp = jax.vjp(_rms, ff_out, g)
            df, dg = vjp(dh2)
            return dh2, df, dg

        dh2, df, rep_grads["ffn_post_g"][i] = rowcall(
            f"post_ffn_bwd{i}", post_ffn_bwd, [dh, dh2_ple, sv["f"]], [gain("ffn_post_g", i)], [F32, BF16],
            red_rows=(1,), cols=d)
        send("w_down", i, _tn(sv["act"], df, f"d_down{i}"))
        dact = _nt(df, weight("w_down", i, df), f"dx_down{i}", out_dtype=BF16)
        dgu = _swiglu_bwd(sv["gate"], sv["up"], dact, name=f"swiglu_bwd{i}")
        send("w_gate_up", i, _tn(sv["n2"], dgu, f"d_gate_up{i}"))
        dn2 = _nt(dgu, weight("w_gate_up", i, dgu), f"dx_gate_up{i}")

        def post_mix_bwd(dy, dn, h1, mm, g_post, g_pre):
            _, vjp_pre = jax.vjp(_rms, h1, g_pre)
            dh1_n, dg_pre = vjp_pre(dn)
            dh1 = dy + dh1_n
            _, vjp_post = jax.vjp(_rms, mm, g_post)
            dm, dg_post = vjp_post(dh1)
            return dh1, dm, dg_pre, dg_post

        dh1, dm, rep_grads["ffn_pre_g"][i], rep_grads["mix_post_g"][i] = rowcall(
            f"post_mix_bwd{i}", post_mix_bwd, [dh2, dn2, sv["h1"], sv["m"]],
            [gain("mix_post_g", i), gain("ffn_pre_g", i)], [F32, BF16], red_rows=(1, 1), cols=d)

        if i % 2 == 0:
            send_small(i, "w_out_even", 0, sv["cat"], dm, f"d_out_even{i}")
            dcat = _nt(dm, weight("w_out_even", 0, dm), f"dx_out_even{i}")
            dq, dk, dv = _sb_bwd(sv["proj"], dcat, n_heads, name=f"sb_bwd{i}")
            dhq, dhf, dhi, dhg, grads["hg_lb_logits"], grads["hg_norm_g"] = _hg_bwd(
                sv["proj"], rep["hg_lb_logits"], rep["hg_norm_g"], sv["oraw"], sv["states"], dcat, n_heads,
                name=f"hg_bwd{i}")
            dproj = jnp.concatenate([dq.astype(BF16), dk.astype(BF16), dv.astype(BF16), dhq, dhf, dhi, dhg], axis=1)
            send("w_in_even", 0, _tn(sv["n1"], dproj, f"d_in_even{i}"))
            dn1 = _nt(dproj, weight("w_in_even", 0, dproj), f"dx_in_even{i}")
        else:
            sm = sv["sm"]
            send_small(i, "w_out_odd", 0, sv["gact"], dm, f"d_out_odd{i}")
            dgo = _nt(dm, weight("w_out_odd", 0, dm), f"dx_out_odd{i}")
            dgate, dra, dix, dxc1, grads["rg_lambda"] = _rg_scan_bwd(
                dgo, sv["proj"], sv["hs"], sv["xc"], sv["ra"], sv["ix"], sm["rg_lambda"], name=f"rg_scan_bwd{i}")
            (dxb, grads["conv_w"], grads["conv_b"], grads["rg_wa"], grads["rg_ba"], grads["rg_wx"],
             grads["rg_bx"]) = _rg_gates_bwd(dra, dix, dxc1, sv["xc"], sv["proj"], sm["conv_w"], sm["rg_wa"],
                                            sm["rg_wx"], name=f"rg_gates_bwd{i}")
            send("small", 0, {k: grads.pop(k) for k in _SMALL})
            dproj = jnp.concatenate([dgate, dxb], axis=1)
            send("w_in_odd", 0, _tn(sv["n1"], dproj, f"d_in_odd{i}"))
            dn1 = _nt(dproj, weight("w_in_odd", 0, dproj), f"dx_in_odd{i}")

        def pre_norm_bwd(dy, dn, hh, g):
            _, vjp = jax.vjp(_rms, hh, g)
            dx, dg = vjp(dn)
            return dy + dx, dg

        dh, rep_grads["mix_pre_g"][i] = rowcall(
            f"pre_norm_bwd{i}", pre_norm_bwd, [dh1, dn1, sv["h"]], [gain("mix_pre_g", i)], [F32],
            red_rows=(1,), cols=d)

    for name, layer, a, b, mm_name in deferred:
        send(name, layer, _tn(a, b, mm_name, deps=(dh,)))
    for k, rows in rep_grads.items():
        grads[k] = jnp.concatenate(rows, axis=0)
    return loss_cols, dh, grads


_WEIGHTS = ("mix_pre_g", "mix_post_g", "ffn_pre_g", "ffn_post_g", "ple_norm_g", "w_in_even", "w_out_even",
            "hg_lb_logits", "hg_norm_g", "w_in_odd", "conv_w", "conv_b", "rg_wa", "rg_ba", "rg_wx", "rg_bx",
            "rg_lambda", "w_out_odd", "w_gate_up", "w_down", "w_ple_up", "w_ple_gate")
_REPLICATED = ("mix_pre_g", "mix_post_g", "ffn_pre_g", "ffn_post_g", "ple_norm_g", "hg_lb_logits", "hg_norm_g")
_SMALL = ("conv_w", "conv_b", "rg_wa", "rg_ba", "rg_wx", "rg_bx", "rg_lambda")
_BIG = {"w_in_even": True, "w_out_even": False, "w_in_odd": True, "w_out_odd": False,
        "w_gate_up": True, "w_down": False, "w_ple_up": True, "w_ple_gate": False}
_PACK_ROW = SUBLANES * LANES


def _pack(arrays):
    flat = jnp.concatenate([a.reshape(-1) for a in arrays])
    pad = -flat.shape[0] % _PACK_ROW
    return jnp.pad(flat, (0, pad)).reshape(-1, LANES)


def _pack_blocks(arrays):
    flat = jnp.concatenate([a.reshape(N_DEV, -1) for a in arrays], axis=1)
    pad = -flat.shape[1] % _PACK_ROW
    return jnp.pad(flat, ((0, 0), (0, pad))).reshape(N_DEV, -1, LANES)


def _unpack(packed, shapes, lead=()):
    flat = packed.reshape(lead + (-1,))
    out, pos = [], 0
    for shape in shapes:
        n = math.prod(shape)
        out.append(flat[..., pos:pos + n].reshape(lead + tuple(shape)))
        pos += n
    return out


def _to_full_small(name, blocks):
    if name == "conv_w":
        return jnp.transpose(blocks, (1, 0, 2)).reshape(blocks.shape[1], -1)
    if name in ("conv_b", "rg_lambda"):
        return blocks.reshape(1, -1)
    nb = blocks.shape[1]
    if name in ("rg_wa", "rg_wx"):
        return jnp.transpose(blocks, (1, 0, 2, 3)).reshape(nb, RG_BLOCK, RG_BLOCK)
    return jnp.transpose(blocks, (1, 0, 2)).reshape(nb, 1, RG_BLOCK)


def _to_blocks_small(name, full):
    if name == "conv_w":
        return jnp.transpose(full.reshape(full.shape[0], N_DEV, -1), (1, 0, 2))
    if name in ("conv_b", "rg_lambda"):
        return full.reshape(N_DEV, -1)
    nb = full.shape[0]
    if name in ("rg_wa", "rg_wx"):
        return jnp.transpose(full.reshape(nb, N_DEV, RG_BLOCK // N_DEV, RG_BLOCK), (1, 0, 2, 3))
    return jnp.transpose(full.reshape(nb, N_DEV, RG_BLOCK // N_DEV), (1, 0, 2))


def _step(inp):
    w = {k: inp[k] for k in _WEIGHTS}
    x, p, target = inp["x"][0], inp["p"][:, 0], inp["loss_target"][0]
    assert w["hg_lb_logits"].shape[0] == 2 and w["w_in_even"].shape[0] == 1 and w["w_in_odd"].shape[0] == 1

    n_heads = w["w_in_even"].shape[2] * N_DEV // (7 * HEAD_DIM)
    small_shapes = [w[k].shape[1:] for k in _SMALL]

    def lands_in_place(name):
        return _BIG[name] and w[name].shape[2] % LANES == 0

    depth = p.shape[0]
    order = [("w_in_even", 0), ("w_out_even", 0)] if depth else []
    for i in range(depth):
        if i == 1:
            order += [("w_in_odd", 0), ("small", 0), ("w_out_odd", 0)]
        order += [("w_gate_up", i), ("w_down", i), ("w_ple_up", i), ("w_ple_gate", i)]
    heavy = ("w_gate_up", "w_down", "w_in_odd", "w_out_odd")
    gathers = {}
    first_started = 0.0
    for name, l in sorted(order, key=lambda key: key[0] in heavy):
        if name == "small":
            gathers[name, l] = _gather_start(_pack([w[k][0] for k in _SMALL]) + first_started, name="gather_small",
                                             relayed=False)
        else:
            gathers[name, l] = _gather_start((w[name][l] + first_started).astype(BF16), name=f"gather_{name}{l}",
                                             cols=lands_in_place(name))
        if len(gathers) == 1:
            first_started = gathers[name, l]["token"][0, 0]
    ready = {}

    def relay(key, after):
        if "cols" in gathers[key] and key[0] != "small":
            gathers[key] = _gather_relay(gathers[key], after, name=f"gather_{key[0]}{key[1]}")

    def weight(name, layer, after):
        key = ("small", 0) if name in _SMALL else (name, layer)
        if key not in ready:
            relay(key, after)
            for nxt in order[order.index(key) + 1:order.index(key) + 2]:
                relay(nxt, after)
            land, = _spread_wait(gathers[key], after, name=f"gathered_{key[0]}{key[1]}")
            if name in _SMALL:
                ready[key] = {k: _to_full_small(k, b)
                              for k, b in zip(_SMALL, _unpack(land, small_shapes, lead=(N_DEV,)))}
            elif lands_in_place(name):
                ready[key] = land
            elif _BIG[name]:
                ready[key] = jnp.transpose(land, (1, 0, 2)).reshape(land.shape[1], -1)
            else:
                ready[key] = land.reshape(-1, land.shape[2])
        return ready[key][name] if name in _SMALL else ready[key]

    exchanges = []

    def emit(name, layer, g):
        if name == "small":
            handle = _exchange_start([_pack_blocks([_to_blocks_small(k, g[k]) for k in _SMALL])],
                                     name="exchange_small")
            exchanges.append((name, layer, handle))
            return handle["token"]
        _, r, c = w[name].shape
        if lands_in_place(name):
            handle = _exchange_start([g], name=f"exchange_{name}{layer}", cols=True)
        elif _BIG[name]:
            handle = _exchange_start([jnp.transpose(g.reshape(-1, N_DEV, c), (1, 0, 2))],
                                     name=f"exchange_{name}{layer}")
        else:
            handle = _exchange_start([g.reshape(N_DEV, r, c)], name=f"exchange_{name}{layer}")
        exchanges.append((name, layer, handle))
        return handle["token"]

    rep = {k: w[k] for k in _REPLICATED}
    loss_cols, dx, grads = _local_step(x, p, target, rep, weight, emit, n_heads,
                                       [h["token"] for h in gathers.values()])

    loss_part = jnp.sum(loss_cols).reshape(1)
    rep_gather = _gather_start(_pack([grads[k] for k in _REPLICATED] + [loss_part]), name="gather_rep_grads",
                               relayed=False)

    out = {}
    after = dx
    for name, layer, handle in exchanges:
        land, = _spread_wait(handle, after, name=f"exchanged_{name}{layer}")
        if name == "small":
            res = _adamw(land.reshape(N_DEV, -1, LANES), *[_pack([inp[pre + k][0] for k in _SMALL]) for pre in ("", "m_", "v_")],
                         name="adamw_small")
            for k, *vals in zip(_SMALL, *[_unpack(a, small_shapes) for a in res]):
                out[k] = [v[None] for v in vals]
        else:
            n_l, r, c = w[name].shape
            res = out[name] = _adamw(land.reshape(N_DEV, r, c),
                                     *[inp[pre + name].reshape(n_l * r, c) for pre in ("", "m_", "v_")],
                                     name=f"adamw_{name}{layer}", layer=layer, prev=out.get(name))
        after = res[0]
    for name in _BIG:
        out[name] = [a.reshape(w[name].shape) for a in out[name]]

    rep_shapes = [w[k].shape for k in _REPLICATED] + [(1,)]
    rep_parts, = _spread_wait(rep_gather, after, name="gathered_rep_grads")
    res = _adamw(rep_parts, *[_pack([inp[pre + k] for k in _REPLICATED] + [jnp.zeros((1,), F32)])
                              for pre in ("", "m_", "v_")], name="adamw_rep")
    for k, *vals in zip(_REPLICATED + ("loss",), *[_unpack(a, rep_shapes) for a in res]):
        out[k] = vals
    loss = out["loss"][0][0]

    return (loss, dx[None]) + tuple(out[k][j] for j in range(4) for k in _WEIGHTS)


def kernel(x, p, mix_pre_g, mix_post_g, ffn_pre_g, ffn_post_g, ple_norm_g, w_in_even, w_out_even, hg_lb_logits, hg_norm_g, w_in_odd, conv_w, conv_b, rg_wa, rg_ba, rg_wx, rg_bx, rg_lambda, w_out_odd, w_gate_up, w_down, w_ple_up, w_ple_gate, loss_target, m_mix_pre_g, m_mix_post_g, m_ffn_pre_g, m_ffn_post_g, m_ple_norm_g, m_w_in_even, m_w_out_even, m_hg_lb_logits, m_hg_norm_g, m_w_in_odd, m_conv_w, m_conv_b, m_rg_wa, m_rg_ba, m_rg_wx, m_rg_bx, m_rg_lambda, m_w_out_odd, m_w_gate_up, m_w_down, m_w_ple_up, m_w_ple_gate, v_mix_pre_g, v_mix_post_g, v_ffn_pre_g, v_ffn_post_g, v_ple_norm_g, v_w_in_even, v_w_out_even, v_hg_lb_logits, v_hg_norm_g, v_w_in_odd, v_conv_w, v_conv_b, v_rg_wa, v_rg_ba, v_rg_wx, v_rg_bx, v_rg_lambda, v_w_out_odd, v_w_gate_up, v_w_down, v_w_ple_up, v_w_ple_gate):
    return _step(dict(locals()))
```

```python
import functools
import math

import jax
import jax.numpy as jnp
from jax import lax
from jax.experimental import pallas as pl
from jax.experimental.pallas import tpu as pltpu

F32 = jnp.float32
BF16 = jnp.bfloat16

VMEM_LIMIT_BYTES = 56 * 1024 * 1024
ADAMW_BLOCK_BYTES = 40 * 1024 * 1024
LANES = 128
SUBLANES = 8

N_DEV = 8
HEAD_DIM = 128
SB_Q_TILE = 512
SB_K_TILE = 128
HG_CHUNK = 32
HG_HEADS_PER_STEP = 2
RG_BLOCK = 256
CONV_TAPS = 4
RG_C = 8.0
RMS_EPS = 1e-6

ADAM_LR = 0.001
ADAM_B1 = 0.9
ADAM_B2 = 0.999
ADAM_EPS = 1e-08
ADAM_WD = 0.01
ADAM_STEP = 10


def _params(*sem):
    return pltpu.CompilerParams(dimension_semantics=sem, vmem_limit_bytes=VMEM_LIMIT_BYTES)


def _pick(n, cands):
    for c in cands:
        if c <= n and n % c == 0:
            return c
    return n


def _mm(a, b, mode, *, name, out_dtype=F32, tm=512, tn=512, tk=None, deps=()):
    if mode == "nn":
        (m, k), (k2, n) = a.shape, b.shape
    else:
        (m, k), (n, k2) = a.shape, b.shape
    assert k == k2, (a.shape, b.shape, mode)
    tm, tn = min(tm, m), min(tn, n)
    tk = k if tk is None else min(tk, k)
    assert m % tm == 0 and n % tn == 0 and k % tk == 0, (m, n, k, tm, tn, tk)
    nk = k // tk

    a_spec = pl.BlockSpec((tm, tk), lambda i, j, kk: (i, kk))
    if mode == "nn":
        b_spec = pl.BlockSpec((tk, tn), lambda i, j, kk: (kk, j))
        dims = (((1,), (0,)), ((), ()))
    else:
        b_spec = pl.BlockSpec((tn, tk), lambda i, j, kk: (j, kk))
        dims = (((1,), (1,)), ((), ()))

    def body(a_ref, b_ref, *refs):
        o_ref, *acc = refs[len(deps):]
        part = lax.dot_general(a_ref[...].astype(BF16), b_ref[...].astype(BF16), dims, preferred_element_type=F32)
        if nk == 1:
            o_ref[...] = part.astype(out_dtype)
        else:
            acc_ref, = acc
            kk = pl.program_id(2)

            @pl.when(kk == 0)
            def _():
                acc_ref[...] = part

            @pl.when(kk > 0)
            def _():
                acc_ref[...] += part

            @pl.when(kk == nk - 1)
            def _():
                o_ref[...] = acc_ref[...].astype(out_dtype)

    return pl.pallas_call(
        body, name=name,
        grid=(m // tm, n // tn, nk),
        in_specs=[a_spec, b_spec] + [pl.BlockSpec(memory_space=pl.ANY)] * len(deps),
        out_specs=pl.BlockSpec((tm, tn), lambda i, j, kk: (i, j)),
        out_shape=jax.ShapeDtypeStruct((m, n), out_dtype),
        scratch_shapes=[] if nk == 1 else [pltpu.VMEM((tm, tn), F32)],
        compiler_params=_params("parallel", "parallel", "arbitrary"),
    )(a, b, *deps)


def _mm_tn(a, b, *, name, out_dtype, tm, tn, deps=()):
    k, m = a.shape
    n = b.shape[1]

    def body(a_ref, b_ref, *refs):
        o_ref, at_ref = refs[len(deps):]

        @pl.when(pl.program_id(1) == 0)
        def _():
            at_ref[...] = a_ref[...].astype(F32).T.astype(BF16)

        o_ref[...] = jnp.dot(at_ref[...], b_ref[...].astype(BF16), preferred_element_type=F32).astype(out_dtype)

    return pl.pallas_call(
        body, name=name,
        grid=(m // tm, n // tn),
        in_specs=[pl.BlockSpec((k, tm), lambda i, j: (0, i)), pl.BlockSpec((k, tn), lambda i, j: (0, j))]
        + [pl.BlockSpec(memory_space=pl.ANY)] * len(deps),
        out_specs=pl.BlockSpec((tm, tn), lambda i, j: (i, j)),
        out_shape=jax.ShapeDtypeStruct((m, n), out_dtype),
        scratch_shapes=[pltpu.VMEM((tm, k), BF16)],
        compiler_params=_params("parallel", "arbitrary"),
    )(a, b, *deps)


def _rowcall(name, fn, rows, pars, row_outs, red_rows=(), *, cols, ts=256, tc=None, deps=()):
    rows = [r if isinstance(r, tuple) else (r, 0) for r in rows]
    pars = [p if isinstance(p, tuple) else (p, 0) for p in pars]
    s = rows[0][0].shape[0]
    tc = cols if tc is None else tc
    ts = min(ts, s)
    assert s % ts == 0 and cols % tc == 0, (name, s, ts, cols, tc)
    n_in, n_row_out = len(rows) + len(pars), len(row_outs)

    def body(*refs):
        outs = fn(*[r[...] for r in refs[:n_in]])
        outs = outs if isinstance(outs, (tuple, list)) else (outs,)
        o_refs = refs[n_in + len(deps):]
        for o_ref, val in zip(o_refs[:n_row_out], outs[:n_row_out]):
            o_ref[...] = val.astype(o_ref.dtype)
        first = pl.program_id(1) == 0
        for o_ref, val in zip(o_refs[n_row_out:], outs[n_row_out:]):
            @pl.when(first)
            def _(o_ref=o_ref, val=val):
                o_ref[...] = val

            @pl.when(jnp.logical_not(first))
            def _(o_ref=o_ref, val=val):
                o_ref[...] += val

    def row_map(off):
        return lambda j, i: (i, j + off)

    def par_map(off):
        return lambda j, i: (0, j + off)

    return pl.pallas_call(
        body, name=name,
        grid=(cols // tc, s // ts),
        in_specs=[pl.BlockSpec((ts, tc), row_map(off)) for _, off in rows]
        + [pl.BlockSpec((p.shape[0], tc), par_map(off)) for p, off in pars]
        + [pl.BlockSpec(memory_space=pl.ANY)] * len(deps),
        out_specs=[pl.BlockSpec((ts, tc), lambda j, i: (i, j)) for _ in row_outs]
        + [pl.BlockSpec((r, tc), lambda j, i: (0, j)) for r in red_rows],
        out_shape=[jax.ShapeDtypeStruct((s, cols), dt) for dt in row_outs]
        + [jax.ShapeDtypeStruct((r, cols), F32) for r in red_rows],
        compiler_params=_params("parallel", "arbitrary"),
    )(*[r for r, _ in rows], *[p for p, _ in pars], *deps)


def _swiglu_act(g, u):
    return _silu(g) * u


def _gate_up(n, w, *, name):
    s, d = n.shape
    f = w.shape[1] // 2
    tn = _pick(f, (256, 128))
    nj = f // tn

    def body(a_ref, wg_ref, wu_ref, g_ref, u_ref, act_ref):
        a = a_ref[...].astype(BF16)
        g = jnp.dot(a, wg_ref[...].astype(BF16), preferred_element_type=F32)
        u = jnp.dot(a, wu_ref[...].astype(BF16), preferred_element_type=F32)
        g_ref[...] = g.astype(BF16)
        u_ref[...] = u.astype(BF16)
        act_ref[...] = _swiglu_act(g, u).astype(BF16)

    out = pl.BlockSpec((s, tn), lambda j: (0, j))
    return pl.pallas_call(
        body, name=name,
        grid=(nj,),
        in_specs=[pl.BlockSpec((s, d), lambda j: (0, 0)), pl.BlockSpec((d, tn), lambda j: (0, j)),
                  pl.BlockSpec((d, tn), lambda j: (0, nj + j))],
        out_specs=[out] * 3,
        out_shape=[jax.ShapeDtypeStruct((s, f), BF16)] * 3,
        compiler_params=_params("parallel"),
    )(n, w, w)


def _swiglu_bwd(g, u, dact, *, name, ts=128):
    s, f = g.shape
    ts = min(ts, s)

    def body(g_ref, u_ref, dact_ref, o_ref):
        _, vjp = jax.vjp(_swiglu_act, g_ref[...].astype(F32), u_ref[...].astype(F32))
        dg, du = vjp(dact_ref[...].astype(F32))
        o_ref[:, 0:f] = dg.astype(BF16)
        o_ref[:, f:2 * f] = du.astype(BF16)

    narrow = pl.BlockSpec((ts, f), lambda i: (i, 0))
    return pl.pallas_call(
        body, name=name,
        grid=(s // ts,),
        in_specs=[narrow] * 3,
        out_specs=pl.BlockSpec((ts, 2 * f), lambda i: (i, 0)),
        out_shape=jax.ShapeDtypeStruct((s, 2 * f), BF16),
        compiler_params=_params("parallel"),
    )(g, u, dact)


def _rms(x, g):
    return x * lax.rsqrt(jnp.mean(x * x, axis=-1, keepdims=True) + RMS_EPS) * g


def _sigmoid(x):
    return jax.nn.sigmoid(x)


def _silu(x):
    return x * jax.nn.sigmoid(x)


def _gelu(x):
    return 0.5 * x * (1.0 + jnp.tanh(math.sqrt(2.0 / math.pi) * (x + 0.044715 * (x * x * x))))


def _softplus(x):
    return jnp.maximum(x, 0.0) + jnp.log1p(jnp.exp(-jnp.abs(x)))


def _split(x, terms):
    parts = []
    for _ in range(terms - 1):
        parts.append(x.astype(BF16))
        x = x - parts[-1].astype(F32)
    return parts + [x.astype(BF16)]


def _xdot(x, t, terms=3):
    return sum(jnp.dot(p, t, preferred_element_type=F32) for p in _split(x, terms))


def _xdot_l(t, x):
    return sum(jnp.dot(t, p, preferred_element_type=F32) for p in _split(x, 3))


_NT = (((1,), (1,)), ((), ()))
_TN = (((0,), (0,)), ((), ()))


def _dot(a, b, dims=None):
    if dims is None:
        return jnp.dot(a.astype(BF16), b.astype(BF16), preferred_element_type=F32)
    return lax.dot_general(a.astype(BF16), b.astype(BF16), dims, preferred_element_type=F32)


def _iota(shape, axis):
    return lax.broadcasted_iota(jnp.int32, shape, axis)


def _sb_tile(qb, kblk, mask, upper, c_rem):
    z = lax.dot_general(qb, kblk, _NT, preferred_element_type=F32)
    soft = jnp.log1p(jnp.exp(-jnp.abs(z)))
    lbeta = jnp.minimum(z, 0.0) - soft
    l1m = -jnp.maximum(z, 0.0) - soft
    if mask is not None:
        l1m = jnp.where(mask, l1m, 0.0)
    rem = _xdot(l1m, upper, terms=2) + c_rem
    w = jnp.exp(lbeta + rem)
    if mask is not None:
        w = jnp.where(mask, w, 0.0)
    return lbeta, l1m, w


def _sb_tiles(s):
    tq = min(SB_Q_TILE, s)
    return tq, SB_K_TILE, tq // SB_K_TILE


def _sb_key_loops(qi, per_q, step, carry):
    n_full = qi * per_q
    carry = lax.fori_loop(0, per_q, lambda j, c: step(n_full + per_q - 1 - j, True, c), carry)
    return lax.fori_loop(0, n_full, lambda j, c: step(n_full - 1 - j, False, c), carry)


def _sb_fwd(proj, n_heads, *, name):
    s = proj.shape[0]
    t, tk, per_q = _sb_tiles(s)
    scale = HEAD_DIM ** -0.5

    def body(q_ref, k_ref, v_ref, o_ref):
        qi = pl.program_id(1)
        qb = (q_ref[...] * scale).astype(BF16)
        row, col = _iota((t, tk), 0) + qi * t, _iota((t, tk), 1)
        upper = (_iota((tk, tk), 0) > _iota((tk, tk), 1)).astype(BF16)

        def step(kb, masked, carry):
            acc, c_rem = carry
            rows = pl.ds(pl.multiple_of(kb * tk, tk), tk)
            kblk = k_ref[rows, :].astype(BF16)
            vblk = v_ref[rows, :].astype(BF16)
            _, l1m, w = _sb_tile(qb, kblk, (col + kb * tk) < row if masked else None, upper, c_rem)
            acc = acc + jnp.dot(w.astype(BF16), vblk, preferred_element_type=F32)
            return acc, c_rem + jnp.sum(l1m, axis=1, keepdims=True)

        acc, _ = _sb_key_loops(qi, per_q, step, (jnp.zeros((t, HEAD_DIM), F32), jnp.zeros((t, 1), F32)))
        o_ref[...] = acc

    return pl.pallas_call(
        body, name=name,
        grid=(n_heads, s // t),
        in_specs=[pl.BlockSpec((t, HEAD_DIM), lambda h, i: (i, h)),
                  pl.BlockSpec((s, HEAD_DIM), lambda h, i: (0, n_heads + h)),
                  pl.BlockSpec((s, HEAD_DIM), lambda h, i: (0, 2 * n_heads + h))],
        out_specs=pl.BlockSpec((t, HEAD_DIM), lambda h, i: (i, h)),
        out_shape=jax.ShapeDtypeStruct((s, n_heads * HEAD_DIM), F32),
        compiler_params=_params("parallel", "arbitrary"),
    )(proj, proj, proj)


def _sb_bwd(proj, dcat, n_heads, *, name):
    s = proj.shape[0]
    t, tk, per_q = _sb_tiles(s)
    scale = HEAD_DIM ** -0.5

    def body(q_ref, k_ref, v_ref, do_ref, dq_ref, dk_ref, dv_ref, g_s, sig_s):
        qi = pl.program_id(1)

        @pl.when(qi == 0)
        def _():
            dk_ref[...] = jnp.zeros_like(dk_ref)
            dv_ref[...] = jnp.zeros_like(dv_ref)

        qb = (q_ref[...] * scale).astype(BF16)
        dob = do_ref[...].astype(BF16)
        row, col = _iota((t, tk), 0) + qi * t, _iota((t, tk), 1)
        upper = (_iota((tk, tk), 0) > _iota((tk, tk), 1)).astype(BF16)
        lower_incl = (_iota((tk, tk), 0) >= _iota((tk, tk), 1)).astype(BF16)

        def weights(kb, masked, carry):
            c_rem, g_all = carry
            rows = pl.ds(pl.multiple_of(kb * tk, tk), tk)
            kblk = k_ref[rows, :].astype(BF16)
            vblk = v_ref[rows, :].astype(BF16)
            lbeta, l1m, w = _sb_tile(qb, kblk, (col + kb * tk) < row if masked else None, upper, c_rem)
            g = w * lax.dot_general(dob, vblk, _NT, preferred_element_type=F32)
            dv_ref[rows, :] += lax.dot_general(w.astype(BF16), dob, _TN, preferred_element_type=F32)
            g_s[kb] = g
            sig_s[kb] = jnp.exp(lbeta)
            return c_rem + jnp.sum(l1m, axis=1, keepdims=True), g_all + jnp.sum(g, axis=1, keepdims=True)

        zero_col = jnp.zeros((t, 1), F32)
        _, g_all = _sb_key_loops(qi, per_q, weights, (zero_col, zero_col))

        def scores(kb, masked, carry):
            dq, c_g = carry
            rows = pl.ds(pl.multiple_of(kb * tk, tk), tk)
            g, sig = g_s[kb], sig_s[kb]
            g_before = g_all - (_xdot(g, lower_incl) + c_g)
            dz = g * (1.0 - sig) - g_before * sig
            if masked:
                dz = jnp.where((col + kb * tk) < row, dz, 0.0)
            dz = dz.astype(BF16)
            dq = dq + jnp.dot(dz, k_ref[rows, :].astype(BF16), preferred_element_type=F32)
            dk_ref[rows, :] += lax.dot_general(dz, qb, _TN, preferred_element_type=F32)
            return dq, c_g + jnp.sum(g, axis=1, keepdims=True)

        dq, _ = _sb_key_loops(qi, per_q, scores, (jnp.zeros((t, HEAD_DIM), F32), zero_col))
        dq_ref[...] = dq * scale

    width = n_heads * HEAD_DIM
    return pl.pallas_call(
        body, name=name,
        grid=(n_heads, s // t),
        in_specs=[pl.BlockSpec((t, HEAD_DIM), lambda h, i: (i, h)),
                  pl.BlockSpec((s, HEAD_DIM), lambda h, i: (0, n_heads + h)),
                  pl.BlockSpec((s, HEAD_DIM), lambda h, i: (0, 2 * n_heads + h)),
                  pl.BlockSpec((t, HEAD_DIM), lambda h, i: (i, h))],
        out_specs=[pl.BlockSpec((t, HEAD_DIM), lambda h, i: (i, h)),
                   pl.BlockSpec((s, HEAD_DIM), lambda h, i: (0, h)),
                   pl.BlockSpec((s, HEAD_DIM), lambda h, i: (0, h))],
        out_shape=[jax.ShapeDtypeStruct((s, width), F32)] * 3,
        scratch_shapes=[pltpu.VMEM((s // tk, t, tk), F32)] * 2,
        compiler_params=_params("parallel", "arbitrary"),
    )(proj, proj, proj, dcat)


def _hg_pre(hq, hf, logits):
    mx = jnp.max(logits, axis=0, keepdims=True)
    ex = jnp.exp(logits - mx)
    lb = ex[0:1, :] / jnp.sum(ex, axis=0, keepdims=True)
    f = lb + (1.0 - lb) * _sigmoid(hf)
    return _silu(hq), 1.0 - f, jnp.log(f)


def _hg_post(o, norm_g, hgate):
    return _rms(o, norm_g) * _silu(hgate)


def _hg_specs(s, n_heads, first_block):
    def at(group):
        return pl.BlockSpec((s, HEAD_DIM), lambda h: (0, first_block + group * n_heads + h))
    return [at(0), at(1), at(2), at(3)]


def _hg_fwd(proj, logits, norm_g, n_heads, *, name):
    s = proj.shape[0]
    hc = HG_CHUNK
    n_chunks = s // hc
    d = HEAD_DIM

    hp = HG_HEADS_PER_STEP
    assert n_heads % hp == 0
    wide = hp * d

    def body(lg_ref, ng_ref, hq_ref, hf_ref, hi_ref, hgt_ref, out_ref, oraw_ref, st_ref,
             q_s, k_s, lf_s, cum_s, qc_s, oc_s):
        q, k, lf = _hg_pre(hq_ref[...], hf_ref[...], lg_ref[...])
        q_s[...] = q
        k_s[...] = k
        lf_s[...] = lf
        tril = (_iota((hc, hc), 0) >= _iota((hc, hc), 1)).astype(BF16)
        srow = _iota((hc, d), 0)

        def head_chunk(j, ci, rows, st):
            ln = slice(j * d, (j + 1) * d)
            q, k, v = q_s[rows, ln], k_s[rows, ln], hi_ref[rows, ln]
            cum = _xdot_l(tril, lf_s[rows, ln])
            st_ref[j, ci] = st
            o_inter = _dot(q * jnp.exp(cum), st, _NT)
            cum_s[:, ln] = cum
            qc_s[:, ln] = q
            for t in range(hc):
                ng = (t // SUBLANES + 1) * SUBLANES
                e = jnp.where(srow[:ng] <= t, jnp.exp(cum_s[t:t + 1, ln] - cum[:ng]), 0.0)
                sc = jnp.sum(qc_s[t:t + 1, ln] * k[:ng] * e, axis=1, keepdims=True)
                oc_s[t:t + 1, ln] = jnp.sum(sc * v[:ng], axis=0, keepdims=True)
            oraw_ref[rows, ln] = o_inter + oc_s[:, ln]
            last = cum_s[hc - 1:hc, ln]
            return st * jnp.exp(last) + _dot(v, k * jnp.exp(last - cum), _TN)

        def chunk(ci, states):
            rows = pl.ds(pl.multiple_of(ci * hc, hc), hc)
            return tuple(head_chunk(j, ci, rows, st) for j, st in enumerate(states))

        lax.fori_loop(0, n_chunks, chunk, tuple(jnp.zeros((d, d), F32) for _ in range(hp)))
        for j in range(hp):
            ln = slice(j * d, (j + 1) * d)
            out_ref[:, ln] = _hg_post(oraw_ref[:, ln], ng_ref[...], hgt_ref[:, ln]).astype(BF16)

    width = n_heads * d
    first = 3 * n_heads // hp
    groups = [pl.BlockSpec((s, wide), functools.partial(lambda h, g: (0, first + g * (n_heads // hp) + h), g=g))
              for g in range(4)]
    head_block = pl.BlockSpec((s, wide), lambda h: (0, h))
    return pl.pallas_call(
        body, name=name,
        grid=(n_heads // hp,),
        in_specs=[pl.BlockSpec((2, wide), lambda h: (0, h)), pl.BlockSpec((1, d), lambda h: (0, 0))] + groups,
        out_specs=[head_block, head_block, pl.BlockSpec((hp, n_chunks, d, d), lambda h: (h, 0, 0, 0))],
        out_shape=[jax.ShapeDtypeStruct((s, width), BF16), jax.ShapeDtypeStruct((s, width), F32),
                   jax.ShapeDtypeStruct((n_heads, n_chunks, d, d), F32)],
        scratch_shapes=[pltpu.VMEM((s, wide), F32)] * 3 + [pltpu.VMEM((hc, wide), F32)] * 3,
        compiler_params=_params("arbitrary"),
    )(logits, norm_g, proj, proj, proj, proj)


def _hg_bwd(proj, logits, norm_g, oraw, states, dcat, n_heads, *, name):
    s = proj.shape[0]
    hc = HG_CHUNK
    n_chunks = s // hc
    d = HEAD_DIM

    def body(lg_ref, ng_ref, hq_ref, hf_ref, hi_ref, hgt_ref, oraw_ref, st_ref, dout_ref,
             dhq_ref, dhf_ref, dhi_ref, dhgt_ref, dlg_ref, dng_ref,
             q_s, k_s, lf_s, do_s, dq_s, dk_s, dlf_s, cum_s, qc_s, doc_s, dqc_s, dkc_s, dvc_s):
        head = pl.program_id(0)
        (q, k, lf), pre_vjp = jax.vjp(_hg_pre, hq_ref[...], hf_ref[...], lg_ref[...])
        q_s[...] = q
        k_s[...] = k
        lf_s[...] = lf
        _, post_vjp = jax.vjp(_hg_post, oraw_ref[...], ng_ref[...], hgt_ref[...])
        do, dng, dhgt = post_vjp(dout_ref[...])
        do_s[...] = do
        dhgt_ref[...] = dhgt.astype(BF16)

        @pl.when(head == 0)
        def _():
            dng_ref[...] = dng

        @pl.when(head > 0)
        def _():
            dng_ref[...] += dng

        triu = (_iota((hc, hc), 0) <= _iota((hc, hc), 1)).astype(BF16)
        tril = (_iota((hc, hc), 0) >= _iota((hc, hc), 1)).astype(BF16)
        srow = _iota((hc, d), 0)

        def chunk(j, dst):
            ci = n_chunks - 1 - j
            rows = pl.ds(pl.multiple_of(ci * hc, hc), hc)
            q, k, v, do_c = q_s[rows, :], k_s[rows, :], hi_ref[rows, :], do_s[rows, :]
            cum = _xdot_l(tril, lf_s[rows, :])
            st = st_ref[0, ci]
            cum_s[...] = cum
            qc_s[...] = q
            doc_s[...] = do_c
            last = cum_s[hc - 1:hc, :]
            e_cum, e_last = jnp.exp(cum), jnp.exp(last - cum)
            dqc_s[...] = _dot(do_c, st) * e_cum
            dk_state = _dot(v, dst) * e_last
            dkc_s[...] = dk_state
            dvc_s[...] = _dot(k * e_last, dst, _NT)
            d_last = (jnp.sum(dst * st, axis=0, keepdims=True) * jnp.exp(last)
                      + jnp.sum(k * dk_state, axis=0, keepdims=True))
            for t in range(hc):
                ng = (t // SUBLANES + 1) * SUBLANES
                qt, dot_ = qc_s[t:t + 1, :], doc_s[t:t + 1, :]
                e = jnp.where(srow[:ng] <= t, jnp.exp(cum_s[t:t + 1, :] - cum[:ng]), 0.0)
                ke = k[:ng] * e
                d_a = jnp.sum(dot_ * v[:ng], axis=1, keepdims=True)
                dqc_s[t:t + 1, :] += jnp.sum(d_a * ke, axis=0, keepdims=True)
                dkc_s[0:ng, :] += d_a * (qt * e)
                dvc_s[0:ng, :] += jnp.sum(qt * ke, axis=1, keepdims=True) * dot_
            dq, dk = dqc_s[...], dkc_s[...]
            d_b = q * dq - k * dk
            dq_s[rows, :] = dq
            dk_s[rows, :] = dk
            dhi_ref[rows, :] = dvc_s[...].astype(BF16)
            dlf_s[rows, :] = _xdot_l(triu, d_b) + d_last
            return dst * jnp.exp(last) + _dot(do_c, q * e_cum, _TN)

        lax.fori_loop(0, n_chunks, chunk, jnp.zeros((d, d), F32))
        dhq, dhf, dlg = pre_vjp((dq_s[...], dk_s[...], dlf_s[...]))
        dhq_ref[...] = dhq.astype(BF16)
        dhf_ref[...] = dhf.astype(BF16)
        dlg_ref[...] = dlg

    width = n_heads * d
    head_block = pl.BlockSpec((s, d), lambda h: (0, h))
    return pl.pallas_call(
        body, name=name,
        grid=(n_heads,),
        in_specs=[pl.BlockSpec((2, d), lambda h: (0, h)), pl.BlockSpec((1, d), lambda h: (0, 0))]
        + _hg_specs(s, n_heads, 3 * n_heads)
        + [head_block, pl.BlockSpec((1, n_chunks, d, d), lambda h: (h, 0, 0, 0)),
           pl.BlockSpec((s, d), lambda h: (0, n_heads + h))],
        out_specs=[head_block] * 4 + [pl.BlockSpec((2, d), lambda h: (0, h)), pl.BlockSpec((1, d), lambda h: (0, 0))],
        out_shape=[jax.ShapeDtypeStruct((s, width), BF16)] * 4
        + [jax.ShapeDtypeStruct((2, width), F32), jax.ShapeDtypeStruct((1, d), F32)],
        scratch_shapes=[pltpu.VMEM((s, d), F32)] * 7 + [pltpu.VMEM((hc, d), F32)] * 6,
        compiler_params=_params("arbitrary"),
    )(logits, norm_g, proj, proj, proj, proj, oraw, states, dcat)


def _shift_down(x, n, srow):
    if n == 0:
        return x
    return jnp.where(srow >= n, pltpu.roll(x, n, 0), 0.0)


def _shift_up(x, n, srow):
    if n == 0:
        return x
    s = x.shape[0]
    return jnp.where(srow < s - n, pltpu.roll(x, s - n, 0), 0.0)


def _rg_gates_fwd(proj, conv_w, conv_b, wa, ba, wx, bx, *, name):
    s = proj.shape[0]
    nb = wa.shape[0]
    bw = RG_BLOCK

    def body(xb_ref, cw_ref, cb_ref, wa_ref, ba_ref, wx_ref, bx_ref, xc_ref, ra_ref, ix_ref):
        x = xb_ref[...]
        srow = _iota((s, bw), 0)
        cw = cw_ref[...]
        xc = cb_ref[...] + cw[0:1, :] * x
        for tap in range(1, CONV_TAPS):
            xc = xc + cw[tap:tap + 1, :] * _shift_down(x, tap, srow)
        xc_ref[...] = xc
        ra_ref[...] = _dot(xc, wa_ref[0]) + ba_ref[0]
        ix_ref[...] = _dot(xc, wx_ref[0]) + bx_ref[0]

    col = pl.BlockSpec((s, bw), lambda n: (0, n))
    vec = lambda r: pl.BlockSpec((r, bw), lambda n: (0, n))
    mat = pl.BlockSpec((1, bw, bw), lambda n: (n, 0, 0))
    bias = pl.BlockSpec((1, 1, bw), lambda n: (n, 0, 0))
    return pl.pallas_call(
        body, name=name,
        grid=(nb,),
        in_specs=[pl.BlockSpec((s, bw), lambda n: (0, nb + n)), vec(CONV_TAPS), vec(1), mat, bias, mat, bias],
        out_specs=[col] * 3,
        out_shape=[jax.ShapeDtypeStruct((s, nb * bw), F32)] * 3,
        compiler_params=_params("parallel"),
    )(proj, conv_w, conv_b, wa, ba, wx, bx)


def _rg_au(ra, ix, xc, lam, first_row):
    log_a = -RG_C * _sigmoid(ra) * _softplus(-lam)
    th = jnp.tanh(log_a)
    one_minus_a2 = -2.0 * th / (1.0 - th)
    mult = jnp.where(first_row, 1.0, jnp.sqrt(one_minus_a2))
    return jnp.exp(log_a), xc * _sigmoid(ix) * mult


def _rg_out(gate, hs):
    return _gelu(gate) * hs


def _linear_scan(a, b, a_s, b_s, in_s, reverse):
    s, c = a.shape
    within = _iota((s, c), 0) & (SUBLANES - 1)
    shift = 1
    while shift < SUBLANES:
        if reverse:
            take = within < SUBLANES - shift
            a_n, b_n = pltpu.roll(a, s - shift, 0), pltpu.roll(b, s - shift, 0)
        else:
            take = within >= shift
            a_n, b_n = pltpu.roll(a, shift, 0), pltpu.roll(b, shift, 0)
        b = jnp.where(take, a * b_n + b, b)
        a = jnp.where(take, a * a_n, a)
        shift *= 2
    a_s[...] = a
    b_s[...] = b
    n_tiles = s // SUBLANES
    edge = 0 if reverse else SUBLANES - 1

    def tile(i, h):
        rows = pl.ds(pl.multiple_of(((n_tiles - 1 - i) if reverse else i) * SUBLANES, SUBLANES), SUBLANES)
        in_s[rows, :] = jnp.broadcast_to(h, (SUBLANES, c))
        return a_s[rows, :][edge:edge + 1, :] * h + b_s[rows, :][edge:edge + 1, :]

    lax.fori_loop(0, n_tiles, tile, jnp.zeros((1, c), F32))
    return a * in_s[...] + b


def _rg_scan_fwd(proj, xc, ra, ix, lam, *, name):
    s, width = xc.shape
    tc = LANES

    def body(gate_ref, xc_ref, ra_ref, ix_ref, lam_ref, hs_ref, gact_ref, a_s, u_s, in_s):
        first_row = _iota((s, tc), 0) == 0
        a, u = _rg_au(ra_ref[...], ix_ref[...], xc_ref[...], lam_ref[...], first_row)
        hs = _linear_scan(a, u, a_s, u_s, in_s, reverse=False)
        hs_ref[...] = hs
        gact_ref[...] = _rg_out(gate_ref[...], hs).astype(BF16)

    col = pl.BlockSpec((s, tc), lambda n: (0, n))
    return pl.pallas_call(
        body, name=name,
        grid=(width // tc,),
        in_specs=[col, col, col, col, pl.BlockSpec((1, tc), lambda n: (0, n))],
        out_specs=[col, col],
        out_shape=[jax.ShapeDtypeStruct((s, width), F32), jax.ShapeDtypeStruct((s, width), BF16)],
        scratch_shapes=[pltpu.VMEM((s, tc), F32)] * 3,
        compiler_params=_params("parallel"),
    )(proj, xc, ra, ix, lam)


def _rg_scan_bwd(dgo, proj, hs, xc, ra, ix, lam, *, name):
    s, width = xc.shape
    tc = LANES

    def body(dgo_ref, gate_ref, hs_ref, xc_ref, ra_ref, ix_ref, lam_ref,
             dgate_ref, dra_ref, dix_ref, dxc_ref, dlam_ref, a_s, dh_s, g_s):
        srow = _iota((s, tc), 0)
        hs = hs_ref[...]
        _, out_vjp = jax.vjp(_rg_out, gate_ref[...], hs)
        dgate, dh = out_vjp(dgo_ref[...])
        dgate_ref[...] = dgate.astype(BF16)
        au = functools.partial(_rg_au, first_row=srow == 0)
        (a, _), au_vjp = jax.vjp(au, ra_ref[...], ix_ref[...], xc_ref[...], lam_ref[...])
        g = _linear_scan(_shift_up(a, 1, srow), dh, a_s, dh_s, g_s, reverse=True)
        dra, dix, dxc, dlam = au_vjp((g * _shift_down(hs, 1, srow), g))
        dra_ref[...] = dra.astype(BF16)
        dix_ref[...] = dix.astype(BF16)
        dxc_ref[...] = dxc
        dlam_ref[...] = dlam

    col = pl.BlockSpec((s, tc), lambda n: (0, n))
    vec = pl.BlockSpec((1, tc), lambda n: (0, n))
    return pl.pallas_call(
        body, name=name,
        grid=(width // tc,),
        in_specs=[col] * 6 + [vec],
        out_specs=[col] * 4 + [vec],
        out_shape=[jax.ShapeDtypeStruct((s, width), BF16)] * 3
        + [jax.ShapeDtypeStruct((s, width), F32), jax.ShapeDtypeStruct((1, width), F32)],
        scratch_shapes=[pltpu.VMEM((s, tc), F32)] * 3,
        compiler_params=_params("parallel"),
    )(dgo, proj, hs, xc, ra, ix, lam)


def _rg_gates_bwd(dra, dix, dxc1, xc, proj, conv_w, wa, wx, *, name):
    s = proj.shape[0]
    nb = wa.shape[0]
    bw = RG_BLOCK

    def body(dra_ref, dix_ref, dxc_ref, xc_ref, xb_ref, cw_ref, wa_ref, wx_ref,
             dxb_ref, dcw_ref, dcb_ref, dwa_ref, dba_ref, dwx_ref, dbx_ref):
        dra, dix = dra_ref[...], dix_ref[...]
        xc_t = xc_ref[...].T.astype(BF16)
        dwa_ref[0] = jnp.dot(xc_t, dra, preferred_element_type=F32)
        dwx_ref[0] = jnp.dot(xc_t, dix, preferred_element_type=F32)
        dba_ref[0] = jnp.sum(dra.astype(F32), axis=0, keepdims=True)
        dbx_ref[0] = jnp.sum(dix.astype(F32), axis=0, keepdims=True)
        dxc = dxc_ref[...] + _dot(dra, wa_ref[0], _NT) + _dot(dix, wx_ref[0], _NT)
        srow = _iota((s, bw), 0)
        x = xb_ref[...]
        cw = cw_ref[...]
        dx = cw[0:1, :] * dxc
        dcw = [jnp.sum(dxc * x, axis=0, keepdims=True)]
        for tap in range(1, CONV_TAPS):
            dx = dx + cw[tap:tap + 1, :] * _shift_up(dxc, tap, srow)
            dcw.append(jnp.sum(dxc * _shift_down(x, tap, srow), axis=0, keepdims=True))
        dxb_ref[...] = dx.astype(BF16)
        r4 = _iota((CONV_TAPS, bw), 0)
        acc = jnp.zeros((CONV_TAPS, bw), F32)
        for tap in range(CONV_TAPS):
            acc = jnp.where(r4 == tap, dcw[tap], acc)
        dcw_ref[...] = acc
        dcb_ref[...] = jnp.sum(dxc, axis=0, keepdims=True)

    col = pl.BlockSpec((s, bw), lambda n: (0, n))
    vec = lambda r: pl.BlockSpec((r, bw), lambda n: (0, n))
    mat = pl.BlockSpec((1, bw, bw), lambda n: (n, 0, 0))
    bias = pl.BlockSpec((1, 1, bw), lambda n: (n, 0, 0))
    width = nb * bw
    return pl.pallas_call(
        body, name=name,
        grid=(nb,),
        in_specs=[col, col, col, col, pl.BlockSpec((s, bw), lambda n: (0, nb + n)), vec(CONV_TAPS), mat, mat],
        out_specs=[col, vec(CONV_TAPS), vec(1), mat, bias, mat, bias],
        out_shape=[jax.ShapeDtypeStruct((s, width), BF16), jax.ShapeDtypeStruct((CONV_TAPS, width), F32),
                   jax.ShapeDtypeStruct((1, width), F32), jax.ShapeDtypeStruct((nb, bw, bw), F32),
                   jax.ShapeDtypeStruct((nb, 1, bw), F32), jax.ShapeDtypeStruct((nb, bw, bw), F32),
                   jax.ShapeDtypeStruct((nb, 1, bw), F32)],
        compiler_params=_params("parallel"),
    )(dra, dix, dxc1, xc, proj, conv_w, wa, wx)


_HBM = pl.BlockSpec(memory_space=pltpu.HBM)
_FLIPS = ((0, 0, 1), (1, 0, 0), (0, 1, 0), (1, 1, 0))
_ALL_FLIPS = tuple((a, b, c) for a in (0, 1) for b in (0, 1) for c in (0, 1))[1:]


def _flip(pos, f):
    return tuple(1 - p if b else p for p, b in zip(pos, f))


def _dev_index(pos):
    return 4 * pos[0] + 2 * pos[1] + pos[2]


def _block(ref, idx, cols):
    if not cols:
        return ref.at[idx]
    n = ref.shape[-1] // N_DEV
    start = pl.multiple_of(idx * n, LANES)
    return ref.at[(slice(None),) * (len(ref.shape) - 1) + (pl.ds(start, n),)]


_SEM = pl.BlockSpec(memory_space=pltpu.SEMAPHORE)
_ANY = pl.BlockSpec(memory_space=pl.ANY)
_N_PEERS = N_DEV - 1


def _hbm(x):
    return pltpu.with_memory_space_constraint(x, pltpu.HBM)


def _me():
    return lax.axis_index("x"), lax.axis_index("y"), lax.axis_index("c")


def _spread_copies(plan, src_refs, land_refs, send_sems, recv_sems, local_sems):
    local, remote = plan(src_refs, land_refs)
    local = [pltpu.make_async_copy(src, dst, local_sems.at[i]) for i, (src, dst) in enumerate(local)]
    remote = [pltpu.make_async_remote_copy(src_ref=src, dst_ref=dst, send_sem=send_sems.at[k], recv_sem=recv_sems.at[k],
                                           device_id=peer, device_id_type=pl.DeviceIdType.MESH)
              for k, (src, dst, peer) in enumerate(remote)]
    return local, remote


def _spread_start(srcs, lands, plan, n_remote, n_local, *, name):
    ns, nl = len(srcs), len(lands)

    def body(*refs):
        src_refs, land_refs = refs[:ns], refs[ns:ns + nl]
        send_sems, recv_sems, local_sems = refs[ns + nl:ns + nl + 3]
        local, remote = _spread_copies(plan, src_refs, land_refs, send_sems, recv_sems, local_sems)
        for cp in local + remote:
            cp.start()
        token = refs[-1]
        token[...] = jnp.zeros_like(token)

    lands = [_hbm(lax.empty(*x)) if isinstance(x, tuple) else x for x in lands]
    out = pl.pallas_call(
        body, name=name,
        in_specs=[_HBM] * (ns + nl),
        out_specs=[_SEM] * 3 + [_HBM] * (ns + nl) + [pl.BlockSpec(memory_space=pltpu.VMEM)],
        out_shape=[pltpu.SemaphoreType.DMA((n_remote,)), pltpu.SemaphoreType.DMA((n_remote,)),
                   pltpu.SemaphoreType.DMA((max(n_local, 1),))]
        + [pltpu.HBM(x.shape, x.dtype) for x in list(srcs) + lands]
        + [jax.ShapeDtypeStruct((SUBLANES, LANES), F32)],
        input_output_aliases={i: 3 + i for i in range(ns + nl)},
        compiler_params=pltpu.CompilerParams(has_side_effects=pltpu.SideEffectType.DATAFLOW_SIDE_EFFECTING),
    )(*[_hbm(x) for x in srcs], *lands)
    return dict(sems=list(out[:3]), srcs=list(out[3:3 + ns]), lands=list(out[3 + ns:3 + ns + nl]),
                token=out[-1], plan=plan)


def _spread_wait(handle, after, *, name):
    ns, nl = len(handle["srcs"]), len(handle["lands"])

    def body(*refs):
        src_refs, land_refs = refs[:ns], refs[ns:ns + nl]
        send_sems, recv_sems, local_sems = refs[ns + nl:ns + nl + 3]
        local, remote = _spread_copies(handle["plan"], src_refs, land_refs, send_sems, recv_sems, local_sems)
        for cp in local:
            cp.wait()
        for cp in remote:
            cp.wait_send()
            cp.wait_recv()

    out = pl.pallas_call(
        body, name=name,
        in_specs=[_HBM] * (ns + nl) + [_SEM] * 3 + [_ANY],
        out_specs=[_HBM] * (ns + nl),
        out_shape=[pltpu.HBM(x.shape, x.dtype) for x in handle["srcs"] + handle["lands"]],
        input_output_aliases={i: i for i in range(ns + nl)},
        compiler_params=pltpu.CompilerParams(has_side_effects=pltpu.SideEffectType.DATAFLOW_SIDE_EFFECTING),
    )(*handle["srcs"], *handle["lands"], *handle["sems"], after)
    return list(out[ns:])


def _gather_start(x, *, name, cols=False, relayed=True):
    shape = x.shape[:-1] + (N_DEV * x.shape[-1],) if cols else (N_DEV,) + x.shape
    flips = _FLIPS if relayed else _ALL_FLIPS

    def plan(src_refs, land_refs):
        me = _me()
        mine = _block(land_refs[0], _dev_index(me), cols)
        return [(src_refs[0], mine)], [(src_refs[0], mine, _flip(me, f)) for f in flips]

    handle = _spread_start([x], [(shape, x.dtype)], plan, len(flips), 1, name=name)
    handle["cols"] = cols
    return handle


def _gather_relay(handle, after, *, name):
    cols = handle["cols"]
    land, = _spread_wait(handle, after, name=f"{name}_arrived")

    def plan(src_refs, land_refs):
        me = _me()
        blocks = [_block(land_refs[0], _dev_index(_flip(me, f)), cols) for f in _FLIPS[1:]]
        return [], [(blk, blk, _flip(me, _FLIPS[0])) for blk in blocks]

    return _spread_start([], [land], plan, len(_FLIPS) - 1, 0, name=f"{name}_pass")


def _exchange_start(ps, *, name, cols=False):
    blk = ps[0].shape[:-1] + (ps[0].shape[-1] // N_DEV,) if cols else ps[0].shape[1:]

    def plan(src_refs, land_refs):
        me = _me()
        me_idx = _dev_index(me)
        local = [(_block(src, me_idx, cols), land_refs[0].at[me_idx, a]) for a, src in enumerate(src_refs)]
        remote = [(_block(src, _dev_index(_flip(me, f)), cols), land_refs[0].at[me_idx, a], _flip(me, f))
                  for f in _ALL_FLIPS for a, src in enumerate(src_refs)]
        return local, remote

    return _spread_start(ps, [((N_DEV, len(ps)) + blk, ps[0].dtype)], plan, _N_PEERS * len(ps), len(ps), name=name)


def _adamw(parts, w, m, v, *, name, layer=0, prev=None):
    n_rows, c = w.shape
    r = parts.shape[1]
    row_bytes = c * (N_DEV * parts.dtype.itemsize + 7 * 4) * 2
    tr = r
    for cand in (512, 256, 128, 64, 32, 16):
        if r % cand == 0 and cand * row_bytes <= ADAMW_BLOCK_BYTES:
            tr = cand
            break
    c1 = 1.0 - ADAM_B1 ** ADAM_STEP
    c2 = 1.0 - ADAM_B2 ** ADAM_STEP

    def body(p_ref, w_ref, m_ref, v_ref, *rest):
        g_ref, d_ref, nm_ref, nv_ref = rest[-4:]
        g = p_ref[0].astype(F32)
        for j in range(1, N_DEV):
            g = g + p_ref[j].astype(F32)
        nm = ADAM_B1 * m_ref[...] + (1.0 - ADAM_B1) * g
        nv = ADAM_B2 * v_ref[...] + (1.0 - ADAM_B2) * (g * g)
        g_ref[...] = g
        nm_ref[...] = nm
        nv_ref[...] = nv
        d_ref[...] = -ADAM_LR * ((nm * (1.0 / c1)) / (jnp.sqrt(nv * (1.0 / c2)) + ADAM_EPS) + ADAM_WD * w_ref[...])

    off = layer * (r // tr)
    blk = pl.BlockSpec((tr, c), lambda i: (i + off, 0))
    prev = list(prev) if prev is not None else []
    return pl.pallas_call(
        body, name=name,
        grid=(r // tr,),
        in_specs=[pl.BlockSpec((N_DEV, tr, c), lambda i: (0, i, 0)), blk, blk, blk] + [_ANY] * len(prev),
        out_specs=[blk] * 4,
        out_shape=[jax.ShapeDtypeStruct((n_rows, c), F32)] * 4,
        input_output_aliases={4 + j: j for j in range(len(prev))},
        compiler_params=_params("parallel"),
    )(parts, w, m, v, *prev)


_TN_CANDS = (512, 256, 128)
_TK_MAX = 5632
_TK_WHOLE_ROWS = 2816


def _contraction_tiles(m, k):
    tk = k
    while tk > _TK_MAX and tk % 2 == 0 and (tk // 2) % LANES == 0:
        tk //= 2
    tm = m if tk <= _TK_WHOLE_ROWS or m % 2 else m // 2
    return tm, tk


def _nn(a, b, name, out_dtype=F32):
    tm, tk = _contraction_tiles(*a.shape)
    return _mm(a, b, "nn", name=name, out_dtype=out_dtype, tm=tm, tn=_pick(b.shape[1], _TN_CANDS), tk=tk)


def _nt(a, b, name, out_dtype=F32, deps=()):
    tm, tk = _contraction_tiles(*a.shape)
    return _mm(a, b, "nt", name=name, out_dtype=out_dtype, tm=tm, tn=_pick(b.shape[0], _TN_CANDS), tk=tk,
               deps=deps)


def _tn(a, b, name, out_dtype=BF16, deps=()):
    assert a.shape[0] == b.shape[0], (a.shape, b.shape)
    return _mm_tn(a, b, name=name, out_dtype=out_dtype, tm=_pick(a.shape[1], _TN_CANDS),
                  tn=_pick(b.shape[1], (1024,) + _TN_CANDS), deps=deps)


def _local_step(x, p, target, rep, weight, emit, n_heads, start_tokens=()):
    s, d = x.shape
    depth = p.shape[0]
    grads = {}
    rep_grads = {k: [None] * depth for k in ("mix_pre_g", "mix_post_g", "ffn_pre_g", "ffn_post_g", "ple_norm_g")}

    pending = list(start_tokens)
    gains = {}

    def gain(name, i):
        if (name, i) not in gains:
            gains[name, i] = rep[name][i:i + 1]
        return gains[name, i]

    def send(name, layer, g):
        token = emit(name, layer, g)
        if token is not None:
            pending.append(token)

    def rowcall(*args, **kwargs):
        deps, pending[:] = tuple(pending), []
        return _rowcall(*args, deps=deps, **kwargs)

    def grad_in(*args, **kwargs):
        deps, pending[:] = tuple(pending), []
        return _nt(*args, deps=deps, **kwargs)

    deferred = []

    def send_small(i, name, layer, a, b, mm_name):
        if i == 0 and depth > 1:
            deferred.append((name, layer, a, b, mm_name))
        else:
            send(name, layer, _tn(a, b, mm_name))

    saved = []
    h = x
    for i in range(depth):
        sv = {"h": h}
        n1, = rowcall(f"pre_norm{i}", lambda hh, g: _rms(hh, g), [h], [gain("mix_pre_g", i)], [BF16], cols=d)
        sv["n1"] = n1
        if i % 2 == 0:
            proj = _nn(n1, weight("w_in_even", 0, n1), f"in_even{i}")
            a_out = _sb_fwd(proj, n_heads, name=f"sb_fwd{i}")
            b_out, oraw, states = _hg_fwd(proj, rep["hg_lb_logits"], rep["hg_norm_g"], n_heads, name=f"hg_fwd{i}")
            cat = jnp.concatenate([a_out.astype(BF16), b_out], axis=1)
            m = _nn(cat, weight("w_out_even", 0, cat), f"out_even{i}")
            sv.update(proj=proj, oraw=oraw, states=states, cat=cat)
        else:
            proj = _nn(n1, weight("w_in_odd", 0, n1), f"in_odd{i}")
            sm = {k: weight(k, 0, proj) for k in _SMALL}
            xc, ra, ix = _rg_gates_fwd(proj, sm["conv_w"], sm["conv_b"], sm["rg_wa"], sm["rg_ba"],
                                       sm["rg_wx"], sm["rg_bx"], name=f"rg_gates_fwd{i}")
            hs, gact = _rg_scan_fwd(proj, xc, ra, ix, sm["rg_lambda"], name=f"rg_scan_fwd{i}")
            m = _nn(gact, weight("w_out_odd", 0, gact), f"out_odd{i}")
            sv.update(proj=proj, xc=xc, ra=ra, ix=ix, hs=hs, gact=gact, sm=sm)

        def post_mix(hh, mm, g_post, g_pre):
            h1 = hh + _rms(mm, g_post)
            return h1, _rms(h1, g_pre)

        h1, n2 = rowcall(f"post_mix{i}", post_mix, [h, m], [gain("mix_post_g", i), gain("ffn_pre_g", i)],
                          [F32, BF16], cols=d)
        gate, up, act = _gate_up(n2, weight("w_gate_up", i, n2), name=f"gate_up{i}")
        f = _nn(act, weight("w_down", i, act), f"down{i}")

        def post_ffn(hh, ff_out, g_post):
            h2 = hh + _rms(ff_out, g_post)
            return h2, h2

        h2, h2b = rowcall(f"post_ffn{i}", post_ffn, [h1, f], [gain("ffn_post_g", i)], [F32, BF16], cols=d)
        e = _nn(p[i], weight("w_ple_up", i, h2b), f"ple_up{i}")
        gl = _nn(h2b, weight("w_ple_gate", i, h2b), f"ple_gate{i}")
        h3, = rowcall(f"ple{i}", lambda hh, a, b, g: hh + _rms(_sigmoid(a) * b, g), [h2, gl, e],
                       [gain("ple_norm_g", i)], [F32], cols=d)
        sv.update(m=m, h1=h1, n2=n2, gate=gate, up=up, act=act, f=f, h2b=h2b, e=e, gl=gl)
        saved.append(sv)
        h = h3

    def loss_fn(y, t):
        err = y - t
        return err * (1.0 / d), jnp.sum(err * err, axis=0, keepdims=True) * (0.5 / d)

    dh, loss_cols = rowcall("loss", loss_fn, [h, target], [], [F32], red_rows=(1,), cols=d)

    for i in reversed(range(depth)):
        sv = saved[i]

        def ple_bwd(dy, a, b, g):
            _, vjp = jax.vjp(lambda a_, b_, g_: _rms(_sigmoid(a_) * b_, g_), a, b, g)
            return vjp(dy)

        dgl, de, rep_grads["ple_norm_g"][i] = rowcall(
            f"ple_bwd{i}", ple_bwd, [dh, sv["gl"], sv["e"]], [gain("ple_norm_g", i)], [BF16, BF16],
            red_rows=(1,), cols=d)
        send_small(i, "w_ple_up", i, p[i], de, f"d_ple_up{i}")
        send_small(i, "w_ple_gate", i, sv["h2b"], dgl, f"d_ple_gate{i}")
        dh2_ple = grad_in(dgl, weight("w_ple_gate", i, dgl), f"dx_ple_gate{i}")

        def post_ffn_bwd(dy, dx, ff_out, g):
            dh2 = dy + dx
            _, vjp = jax.vjp(_rms, ff_out, g)
            df, dg = vjp(dh2)
            return dh2, df, dg

        dh2, df, rep_grads["ffn_post_g"][i] = rowcall(
            f"post_ffn_bwd{i}", post_ffn_bwd, [dh, dh2_ple, sv["f"]], [gain("ffn_post_g", i)], [F32, BF16],
            red_rows=(1,), cols=d)
        send("w_down", i, _tn(sv["act"], df, f"d_down{i}"))
        dact = grad_in(df, weight("w_down", i, df), f"dx_down{i}", out_dtype=BF16)
        dgu = _swiglu_bwd(sv["gate"], sv["up"], dact, name=f"swiglu_bwd{i}")
        send("w_gate_up", i, _tn(sv["n2"], dgu, f"d_gate_up{i}"))
        dn2 = grad_in(dgu, weight("w_gate_up", i, dgu), f"dx_gate_up{i}")

        def post_mix_bwd(dy, dn, h1, mm, g_post, g_pre):
            _, vjp_pre = jax.vjp(_rms, h1, g_pre)
            dh1_n, dg_pre = vjp_pre(dn)
            dh1 = dy + dh1_n
            _, vjp_post = jax.vjp(_rms, mm, g_post)
            dm, dg_post = vjp_post(dh1)
            return dh1, dm, dg_pre, dg_post

        dh1, dm, rep_grads["ffn_pre_g"][i], rep_grads["mix_post_g"][i] = rowcall(
            f"post_mix_bwd{i}", post_mix_bwd, [dh2, dn2, sv["h1"], sv["m"]],
            [gain("mix_post_g", i), gain("ffn_pre_g", i)], [F32, BF16], red_rows=(1, 1), cols=d)

        if i % 2 == 0:
            send_small(i, "w_out_even", 0, sv["cat"], dm, f"d_out_even{i}")
            dcat = grad_in(dm, weight("w_out_even", 0, dm), f"dx_out_even{i}")
            dq, dk, dv = _sb_bwd(sv["proj"], dcat, n_heads, name=f"sb_bwd{i}")
            dhq, dhf, dhi, dhg, grads["hg_lb_logits"], grads["hg_norm_g"] = _hg_bwd(
                sv["proj"], rep["hg_lb_logits"], rep["hg_norm_g"], sv["oraw"], sv["states"], dcat, n_heads,
                name=f"hg_bwd{i}")
            dproj = jnp.concatenate([dq.astype(BF16), dk.astype(BF16), dv.astype(BF16), dhq, dhf, dhi, dhg], axis=1)
            send("w_in_even", 0, _tn(sv["n1"], dproj, f"d_in_even{i}"))
            dn1 = grad_in(dproj, weight("w_in_even", 0, dproj), f"dx_in_even{i}")
        else:
            sm = sv["sm"]
            send_small(i, "w_out_odd", 0, sv["gact"], dm, f"d_out_odd{i}")
            dgo = grad_in(dm, weight("w_out_odd", 0, dm), f"dx_out_odd{i}")
            dgate, dra, dix, dxc1, grads["rg_lambda"] = _rg_scan_bwd(
                dgo, sv["proj"], sv["hs"], sv["xc"], sv["ra"], sv["ix"], sm["rg_lambda"], name=f"rg_scan_bwd{i}")
            (dxb, grads["conv_w"], grads["conv_b"], grads["rg_wa"], grads["rg_ba"], grads["rg_wx"],
             grads["rg_bx"]) = _rg_gates_bwd(dra, dix, dxc1, sv["xc"], sv["proj"], sm["conv_w"], sm["rg_wa"],
                                            sm["rg_wx"], name=f"rg_gates_bwd{i}")
            send("small", 0, {k: grads.pop(k) for k in _SMALL})
            dproj = jnp.concatenate([dgate, dxb], axis=1)
            send("w_in_odd", 0, _tn(sv["n1"], dproj, f"d_in_odd{i}"))
            dn1 = grad_in(dproj, weight("w_in_odd", 0, dproj), f"dx_in_odd{i}")

        def pre_norm_bwd(dy, dn, hh, g):
            _, vjp = jax.vjp(_rms, hh, g)
            dx, dg = vjp(dn)
            return dy + dx, dg

        dh, rep_grads["mix_pre_g"][i] = rowcall(
            f"pre_norm_bwd{i}", pre_norm_bwd, [dh1, dn1, sv["h"]], [gain("mix_pre_g", i)], [F32],
            red_rows=(1,), cols=d)

    for name, layer, a, b, mm_name in deferred:
        send(name, layer, _tn(a, b, mm_name, deps=(dh,)))
    for k, rows in rep_grads.items():
        grads[k] = jnp.concatenate(rows, axis=0)
    return loss_cols, dh, grads


_WEIGHTS = ("mix_pre_g", "mix_post_g", "ffn_pre_g", "ffn_post_g", "ple_norm_g", "w_in_even", "w_out_even",
            "hg_lb_logits", "hg_norm_g", "w_in_odd", "conv_w", "conv_b", "rg_wa", "rg_ba", "rg_wx", "rg_bx",
            "rg_lambda", "w_out_odd", "w_gate_up", "w_down", "w_ple_up", "w_ple_gate")
_REPLICATED = ("mix_pre_g", "mix_post_g", "ffn_pre_g", "ffn_post_g", "ple_norm_g", "hg_lb_logits", "hg_norm_g")
_SMALL = ("conv_w", "conv_b", "rg_wa", "rg_ba", "rg_wx", "rg_bx", "rg_lambda")
_BIG = {"w_in_even": True, "w_out_even": False, "w_in_odd": True, "w_out_odd": False,
        "w_gate_up": True, "w_down": False, "w_ple_up": True, "w_ple_gate": False}
_PACK_ROW = SUBLANES * LANES


def _pack(arrays):
    flat = jnp.concatenate([a.reshape(-1) for a in arrays])
    pad = -flat.shape[0] % _PACK_ROW
    return jnp.pad(flat, (0, pad)).reshape(-1, LANES)


def _pack_blocks(arrays):
    flat = jnp.concatenate([a.reshape(N_DEV, -1) for a in arrays], axis=1)
    pad = -flat.shape[1] % _PACK_ROW
    return jnp.pad(flat, ((0, 0), (0, pad))).reshape(N_DEV, -1, LANES)


def _unpack(packed, shapes, lead=()):
    flat = packed.reshape(lead + (-1,))
    out, pos = [], 0
    for shape in shapes:
        n = math.prod(shape)
        out.append(flat[..., pos:pos + n].reshape(lead + tuple(shape)))
        pos += n
    return out


def _to_full_small(name, blocks):
    if name == "conv_w":
        return jnp.transpose(blocks, (1, 0, 2)).reshape(blocks.shape[1], -1)
    if name in ("conv_b", "rg_lambda"):
        return blocks.reshape(1, -1)
    nb = blocks.shape[1]
    if name in ("rg_wa", "rg_wx"):
        return jnp.transpose(blocks, (1, 0, 2, 3)).reshape(nb, RG_BLOCK, RG_BLOCK)
    return jnp.transpose(blocks, (1, 0, 2)).reshape(nb, 1, RG_BLOCK)


def _to_blocks_small(name, full):
    if name == "conv_w":
        return jnp.transpose(full.reshape(full.shape[0], N_DEV, -1), (1, 0, 2))
    if name in ("conv_b", "rg_lambda"):
        return full.reshape(N_DEV, -1)
    nb = full.shape[0]
    if name in ("rg_wa", "rg_wx"):
        return jnp.transpose(full.reshape(nb, N_DEV, RG_BLOCK // N_DEV, RG_BLOCK), (1, 0, 2, 3))
    return jnp.transpose(full.reshape(nb, N_DEV, RG_BLOCK // N_DEV), (1, 0, 2))


def _step(inp):
    w = {k: inp[k] for k in _WEIGHTS}
    x, p, target = inp["x"][0], inp["p"][:, 0], inp["loss_target"][0]
    assert w["hg_lb_logits"].shape[0] == 2 and w["w_in_even"].shape[0] == 1 and w["w_in_odd"].shape[0] == 1

    n_heads = w["w_in_even"].shape[2] * N_DEV // (7 * HEAD_DIM)
    small_shapes = [w[k].shape[1:] for k in _SMALL]

    def lands_in_place(name):
        return _BIG[name] and w[name].shape[2] % LANES == 0

    depth = p.shape[0]
    order = [("w_in_even", 0), ("w_out_even", 0)] if depth else []
    for i in range(depth):
        if i == 1:
            order += [("w_in_odd", 0), ("small", 0), ("w_out_odd", 0)]
        order += [("w_gate_up", i), ("w_down", i), ("w_ple_up", i), ("w_ple_gate", i)]
    heavy = ("w_gate_up", "w_down", "w_in_odd", "w_out_odd")
    gathers = {}
    first_started = 0.0
    for name, l in sorted(order, key=lambda key: key[0] in heavy):
        if name == "small":
            gathers[name, l] = _gather_start(_pack([w[k][0] for k in _SMALL]) + first_started, name="gather_small",
                                             relayed=False)
        else:
            gathers[name, l] = _gather_start((w[name][l] + first_started).astype(BF16), name=f"gather_{name}{l}",
                                             cols=lands_in_place(name))
        if len(gathers) == 1:
            first_started = gathers[name, l]["token"][0, 0]
    ready = {}

    def relay(key, after):
        if "cols" in gathers[key] and key[0] != "small":
            gathers[key] = _gather_relay(gathers[key], after, name=f"gather_{key[0]}{key[1]}")

    def weight(name, layer, after):
        key = ("small", 0) if name in _SMALL else (name, layer)
        if key not in ready:
            relay(key, after)
            for nxt in order[order.index(key) + 1:order.index(key) + 2]:
                relay(nxt, after)
            land, = _spread_wait(gathers[key], after, name=f"gathered_{key[0]}{key[1]}")
            if name in _SMALL:
                ready[key] = {k: _to_full_small(k, b)
                              for k, b in zip(_SMALL, _unpack(land, small_shapes, lead=(N_DEV,)))}
            elif lands_in_place(name):
                ready[key] = land
            elif _BIG[name]:
                ready[key] = jnp.transpose(land, (1, 0, 2)).reshape(land.shape[1], -1)
            else:
                ready[key] = land.reshape(-1, land.shape[2])
        return ready[key][name] if name in _SMALL else ready[key]

    exchanges = []

    def emit(name, layer, g):
        if name == "small":
            handle = _exchange_start([_pack_blocks([_to_blocks_small(k, g[k]) for k in _SMALL])],
                                     name="exchange_small")
            exchanges.append((name, layer, handle))
            return handle["token"]
        _, r, c = w[name].shape
        if lands_in_place(name):
            handle = _exchange_start([g], name=f"exchange_{name}{layer}", cols=True)
        elif _BIG[name]:
            handle = _exchange_start([jnp.transpose(g.reshape(-1, N_DEV, c), (1, 0, 2))],
                                     name=f"exchange_{name}{layer}")
        else:
            handle = _exchange_start([g.reshape(N_DEV, r, c)], name=f"exchange_{name}{layer}")
        exchanges.append((name, layer, handle))
        return handle["token"]

    rep = {k: w[k] for k in _REPLICATED}
    loss_cols, dx, grads = _local_step(x, p, target, rep, weight, emit, n_heads,
                                       [h["token"] for h in gathers.values()])

    loss_part = jnp.sum(loss_cols).reshape(1)
    rep_gather = _gather_start(_pack([grads[k] for k in _REPLICATED] + [loss_part]), name="gather_rep_grads",
                               relayed=False)

    out = {}
    after = dx
    for name, layer, handle in exchanges:
        land, = _spread_wait(handle, after, name=f"exchanged_{name}{layer}")
        if name == "small":
            res = _adamw(land.reshape(N_DEV, -1, LANES), *[_pack([inp[pre + k][0] for k in _SMALL]) for pre in ("", "m_", "v_")],
                         name="adamw_small")
            for k, *vals in zip(_SMALL, *[_unpack(a, small_shapes) for a in res]):
                out[k] = [v[None] for v in vals]
        else:
            n_l, r, c = w[name].shape
            res = out[name] = _adamw(land.reshape(N_DEV, r, c),
                                     *[inp[pre + name].reshape(n_l * r, c) for pre in ("", "m_", "v_")],
                                     name=f"adamw_{name}{layer}", layer=layer, prev=out.get(name))
        after = res[0]
    for name in _BIG:
        out[name] = [a.reshape(w[name].shape) for a in out[name]]

    rep_shapes = [w[k].shape for k in _REPLICATED] + [(1,)]
    rep_parts, = _spread_wait(rep_gather, after, name="gathered_rep_grads")
    res = _adamw(rep_parts, *[_pack([inp[pre + k] for k in _REPLICATED] + [jnp.zeros((1,), F32)])
                              for pre in ("", "m_", "v_")], name="adamw_rep")
    for k, *vals in zip(_REPLICATED + ("loss",), *[_unpack(a, rep_shapes) for a in res]):
        out[k] = vals
    loss = out["loss"][0][0]

    return (loss, dx[None]) + tuple(out[k][j] for j in range(4) for k in _WEIGHTS)


def kernel(x, p, mix_pre_g, mix_post_g, ffn_pre_g, ffn_post_g, ple_norm_g, w_in_even, w_out_even, hg_lb_logits, hg_norm_g, w_in_odd, conv_w, conv_b, rg_wa, rg_ba, rg_wx, rg_bx, rg_lambda, w_out_odd, w_gate_up, w_down, w_ple_up, w_ple_gate, loss_target, m_mix_pre_g, m_mix_post_g, m_ffn_pre_g, m_ffn_post_g, m_ple_norm_g, m_w_in_even, m_w_out_even, m_hg_lb_logits, m_hg_norm_g, m_w_in_odd, m_conv_w, m_conv_b, m_rg_wa, m_rg_ba, m_rg_wx, m_rg_bx, m_rg_lambda, m_w_out_odd, m_w_gate_up, m_w_down, m_w_ple_up, m_w_ple_gate, v_mix_pre_g, v_mix_post_g, v_ffn_pre_g, v_ffn_post_g, v_ple_norm_g, v_w_in_even, v_w_out_even, v_hg_lb_logits, v_hg_norm_g, v_w_in_odd, v_conv_w, v_conv_b, v_rg_wa, v_rg_ba, v_rg_wx, v_rg_bx, v_rg_lambda, v_w_out_odd, v_w_gate_up, v_w_down, v_w_ple_up, v_w_ple_gate):
    return _step(dict(locals()))
```

```python
import functools
import math

import jax
import jax.numpy as jnp
from jax import lax
from jax.experimental import pallas as pl
from jax.experimental.pallas import tpu as pltpu

F32 = jnp.float32
BF16 = jnp.bfloat16

VMEM_LIMIT_BYTES = 56 * 1024 * 1024
ADAMW_BLOCK_BYTES = 40 * 1024 * 1024
LANES = 128
SUBLANES = 8

N_DEV = 8
HEAD_DIM = 128
SB_Q_TILE = 512
SB_K_TILE = 128
HG_CHUNK = 32
HG_HEADS_PER_STEP = 2
RG_BLOCK = 256
CONV_TAPS = 4
RG_C = 8.0
RMS_EPS = 1e-6

ADAM_LR = 0.001
ADAM_B1 = 0.9
ADAM_B2 = 0.999
ADAM_EPS = 1e-08
ADAM_WD = 0.01
ADAM_STEP = 10


def _params(*sem):
    return pltpu.CompilerParams(dimension_semantics=sem, vmem_limit_bytes=VMEM_LIMIT_BYTES)


def _pick(n, cands):
    for c in cands:
        if c <= n and n % c == 0:
            return c
    return n


def _mm(a, b, mode, *, name, out_dtype=F32, tm=512, tn=512, tk=None, deps=()):
    if mode == "nn":
        (m, k), (k2, n) = a.shape, b.shape
    else:
        (m, k), (n, k2) = a.shape, b.shape
    assert k == k2, (a.shape, b.shape, mode)
    tm, tn = min(tm, m), min(tn, n)
    tk = k if tk is None else min(tk, k)
    assert m % tm == 0 and n % tn == 0 and k % tk == 0, (m, n, k, tm, tn, tk)
    nk = k // tk

    a_spec = pl.BlockSpec((tm, tk), lambda i, j, kk: (i, kk))
    if mode == "nn":
        b_spec = pl.BlockSpec((tk, tn), lambda i, j, kk: (kk, j))
        dims = (((1,), (0,)), ((), ()))
    else:
        b_spec = pl.BlockSpec((tn, tk), lambda i, j, kk: (j, kk))
        dims = (((1,), (1,)), ((), ()))

    def body(a_ref, b_ref, *refs):
        o_ref, *acc = refs[len(deps):]
        part = lax.dot_general(a_ref[...].astype(BF16), b_ref[...].astype(BF16), dims, preferred_element_type=F32)
        if nk == 1:
            o_ref[...] = part.astype(out_dtype)
        else:
            acc_ref, = acc
            kk = pl.program_id(2)

            @pl.when(kk == 0)
            def _():
                acc_ref[...] = part

            @pl.when(kk > 0)
            def _():
                acc_ref[...] += part

            @pl.when(kk == nk - 1)
            def _():
                o_ref[...] = acc_ref[...].astype(out_dtype)

    return pl.pallas_call(
        body, name=name,
        grid=(m // tm, n // tn, nk),
        in_specs=[a_spec, b_spec] + [pl.BlockSpec(memory_space=pl.ANY)] * len(deps),
        out_specs=pl.BlockSpec((tm, tn), lambda i, j, kk: (i, j)),
        out_shape=jax.ShapeDtypeStruct((m, n), out_dtype),
        scratch_shapes=[] if nk == 1 else [pltpu.VMEM((tm, tn), F32)],
        compiler_params=_params("parallel", "parallel", "arbitrary"),
    )(a, b, *deps)


def _mm_tn(a, b, *, name, out_dtype, tm, tn, deps=()):
    k, m = a.shape
    n = b.shape[1]

    def body(a_ref, b_ref, *refs):
        o_ref, at_ref = refs[len(deps):]

        @pl.when(pl.program_id(1) == 0)
        def _():
            at_ref[...] = a_ref[...].astype(F32).T.astype(BF16)

        o_ref[...] = jnp.dot(at_ref[...], b_ref[...].astype(BF16), preferred_element_type=F32).astype(out_dtype)

    return pl.pallas_call(
        body, name=name,
        grid=(m // tm, n // tn),
        in_specs=[pl.BlockSpec((k, tm), lambda i, j: (0, i)), pl.BlockSpec((k, tn), lambda i, j: (0, j))]
        + [pl.BlockSpec(memory_space=pl.ANY)] * len(deps),
        out_specs=pl.BlockSpec((tm, tn), lambda i, j: (i, j)),
        out_shape=jax.ShapeDtypeStruct((m, n), out_dtype),
        scratch_shapes=[pltpu.VMEM((tm, k), BF16)],
        compiler_params=_params("parallel", "arbitrary"),
    )(a, b, *deps)


def _rowcall(name, fn, rows, pars, row_outs, red_rows=(), *, cols, ts=256, tc=None, deps=()):
    rows = [r if isinstance(r, tuple) else (r, 0) for r in rows]
    pars = [p if isinstance(p, tuple) else (p, 0) for p in pars]
    s = rows[0][0].shape[0]
    tc = cols if tc is None else tc
    ts = min(ts, s)
    assert s % ts == 0 and cols % tc == 0, (name, s, ts, cols, tc)
    n_in, n_row_out = len(rows) + len(pars), len(row_outs)

    def body(*refs):
        outs = fn(*[r[...] for r in refs[:n_in]])
        outs = outs if isinstance(outs, (tuple, list)) else (outs,)
        o_refs = refs[n_in + len(deps):]
        for o_ref, val in zip(o_refs[:n_row_out], outs[:n_row_out]):
            o_ref[...] = val.astype(o_ref.dtype)
        first = pl.program_id(1) == 0
        for o_ref, val in zip(o_refs[n_row_out:], outs[n_row_out:]):
            @pl.when(first)
            def _(o_ref=o_ref, val=val):
                o_ref[...] = val

            @pl.when(jnp.logical_not(first))
            def _(o_ref=o_ref, val=val):
                o_ref[...] += val

    def row_map(off):
        return lambda j, i: (i, j + off)

    def par_map(off):
        return lambda j, i: (0, j + off)

    return pl.pallas_call(
        body, name=name,
        grid=(cols // tc, s // ts),
        in_specs=[pl.BlockSpec((ts, tc), row_map(off)) for _, off in rows]
        + [pl.BlockSpec((p.shape[0], tc), par_map(off)) for p, off in pars]
        + [pl.BlockSpec(memory_space=pl.ANY)] * len(deps),
        out_specs=[pl.BlockSpec((ts, tc), lambda j, i: (i, j)) for _ in row_outs]
        + [pl.BlockSpec((r, tc), lambda j, i: (0, j)) for r in red_rows],
        out_shape=[jax.ShapeDtypeStruct((s, cols), dt) for dt in row_outs]
        + [jax.ShapeDtypeStruct((r, cols), F32) for r in red_rows],
        compiler_params=_params("parallel", "arbitrary"),
    )(*[r for r, _ in rows], *[p for p, _ in pars], *deps)


def _swiglu_act(g, u):
    return _silu(g) * u


def _gate_up(n, w, *, name):
    s, d = n.shape
    f = w.shape[1] // 2
    tn = _pick(f, (256, 128))
    nj = f // tn

    def body(a_ref, wg_ref, wu_ref, g_ref, u_ref, act_ref):
        a = a_ref[...].astype(BF16)
        g = jnp.dot(a, wg_ref[...].astype(BF16), preferred_element_type=F32)
        u = jnp.dot(a, wu_ref[...].astype(BF16), preferred_element_type=F32)
        g_ref[...] = g.astype(BF16)
        u_ref[...] = u.astype(BF16)
        act_ref[...] = _swiglu_act(g, u).astype(BF16)

    out = pl.BlockSpec((s, tn), lambda j: (0, j))
    return pl.pallas_call(
        body, name=name,
        grid=(nj,),
        in_specs=[pl.BlockSpec((s, d), lambda j: (0, 0)), pl.BlockSpec((d, tn), lambda j: (0, j)),
                  pl.BlockSpec((d, tn), lambda j: (0, nj + j))],
        out_specs=[out] * 3,
        out_shape=[jax.ShapeDtypeStruct((s, f), BF16)] * 3,
        compiler_params=_params("parallel"),
    )(n, w, w)


def _swiglu_bwd(g, u, dact, *, name, ts=128):
    s, f = g.shape
    ts = min(ts, s)

    def body(g_ref, u_ref, dact_ref, o_ref):
        _, vjp = jax.vjp(_swiglu_act, g_ref[...].astype(F32), u_ref[...].astype(F32))
        dg, du = vjp(dact_ref[...].astype(F32))
        o_ref[:, 0:f] = dg.astype(BF16)
        o_ref[:, f:2 * f] = du.astype(BF16)

    narrow = pl.BlockSpec((ts, f), lambda i: (i, 0))
    return pl.pallas_call(
        body, name=name,
        grid=(s // ts,),
        in_specs=[narrow] * 3,
        out_specs=pl.BlockSpec((ts, 2 * f), lambda i: (i, 0)),
        out_shape=jax.ShapeDtypeStruct((s, 2 * f), BF16),
        compiler_params=_params("parallel"),
    )(g, u, dact)


def _rms(x, g):
    return x * lax.rsqrt(jnp.mean(x * x, axis=-1, keepdims=True) + RMS_EPS) * g


def _sigmoid(x):
    return jax.nn.sigmoid(x)


def _silu(x):
    return x * jax.nn.sigmoid(x)


def _gelu(x):
    return 0.5 * x * (1.0 + jnp.tanh(math.sqrt(2.0 / math.pi) * (x + 0.044715 * (x * x * x))))


def _softplus(x):
    return jnp.maximum(x, 0.0) + jnp.log1p(jnp.exp(-jnp.abs(x)))


def _split(x, terms):
    parts = []
    for _ in range(terms - 1):
        parts.append(x.astype(BF16))
        x = x - parts[-1].astype(F32)
    return parts + [x.astype(BF16)]


def _xdot(x, t, terms=3):
    return sum(jnp.dot(p, t, preferred_element_type=F32) for p in _split(x, terms))


def _xdot_l(t, x):
    return sum(jnp.dot(t, p, preferred_element_type=F32) for p in _split(x, 3))


_NT = (((1,), (1,)), ((), ()))
_TN = (((0,), (0,)), ((), ()))


def _dot(a, b, dims=None):
    if dims is None:
        return jnp.dot(a.astype(BF16), b.astype(BF16), preferred_element_type=F32)
    return lax.dot_general(a.astype(BF16), b.astype(BF16), dims, preferred_element_type=F32)


def _iota(shape, axis):
    return lax.broadcasted_iota(jnp.int32, shape, axis)


def _sb_tile(qb, kblk, mask, upper, c_rem):
    z = lax.dot_general(qb, kblk, _NT, preferred_element_type=F32)
    soft = jnp.log1p(jnp.exp(-jnp.abs(z)))
    lbeta = jnp.minimum(z, 0.0) - soft
    l1m = -jnp.maximum(z, 0.0) - soft
    if mask is not None:
        l1m = jnp.where(mask, l1m, 0.0)
    rem = _xdot(l1m, upper, terms=2) + c_rem
    w = jnp.exp(lbeta + rem)
    if mask is not None:
        w = jnp.where(mask, w, 0.0)
    return lbeta, l1m, w


def _sb_tiles(s):
    tq = min(SB_Q_TILE, s)
    return tq, SB_K_TILE, tq // SB_K_TILE


def _sb_key_loops(qi, per_q, step, carry):
    n_full = qi * per_q
    carry = lax.fori_loop(0, per_q, lambda j, c: step(n_full + per_q - 1 - j, True, c), carry)
    return lax.fori_loop(0, n_full, lambda j, c: step(n_full - 1 - j, False, c), carry)


def _sb_fwd(proj, n_heads, *, name):
    s = proj.shape[0]
    t, tk, per_q = _sb_tiles(s)
    scale = HEAD_DIM ** -0.5

    def body(q_ref, k_ref, v_ref, o_ref):
        qi = pl.program_id(1)
        qb = (q_ref[...] * scale).astype(BF16)
        row, col = _iota((t, tk), 0) + qi * t, _iota((t, tk), 1)
        upper = (_iota((tk, tk), 0) > _iota((tk, tk), 1)).astype(BF16)

        def step(kb, masked, carry):
            acc, c_rem = carry
            rows = pl.ds(pl.multiple_of(kb * tk, tk), tk)
            kblk = k_ref[rows, :].astype(BF16)
            vblk = v_ref[rows, :].astype(BF16)
            _, l1m, w = _sb_tile(qb, kblk, (col + kb * tk) < row if masked else None, upper, c_rem)
            acc = acc + jnp.dot(w.astype(BF16), vblk, preferred_element_type=F32)
            return acc, c_rem + jnp.sum(l1m, axis=1, keepdims=True)

        acc, _ = _sb_key_loops(qi, per_q, step, (jnp.zeros((t, HEAD_DIM), F32), jnp.zeros((t, 1), F32)))
        o_ref[...] = acc

    return pl.pallas_call(
        body, name=name,
        grid=(n_heads, s // t),
        in_specs=[pl.BlockSpec((t, HEAD_DIM), lambda h, i: (i, h)),
                  pl.BlockSpec((s, HEAD_DIM), lambda h, i: (0, n_heads + h)),
                  pl.BlockSpec((s, HEAD_DIM), lambda h, i: (0, 2 * n_heads + h))],
        out_specs=pl.BlockSpec((t, HEAD_DIM), lambda h, i: (i, h)),
        out_shape=jax.ShapeDtypeStruct((s, n_heads * HEAD_DIM), F32),
        compiler_params=_params("parallel", "arbitrary"),
    )(proj, proj, proj)


def _sb_bwd(proj, dcat, n_heads, *, name):
    s = proj.shape[0]
    t, tk, per_q = _sb_tiles(s)
    scale = HEAD_DIM ** -0.5

    def body(q_ref, k_ref, v_ref, do_ref, dq_ref, dk_ref, dv_ref, g_s, sig_s):
        qi = pl.program_id(1)

        @pl.when(qi == 0)
        def _():
            dk_ref[...] = jnp.zeros_like(dk_ref)
            dv_ref[...] = jnp.zeros_like(dv_ref)

        qb = (q_ref[...] * scale).astype(BF16)
        dob = do_ref[...].astype(BF16)
        row, col = _iota((t, tk), 0) + qi * t, _iota((t, tk), 1)
        upper = (_iota((tk, tk), 0) > _iota((tk, tk), 1)).astype(BF16)
        lower_incl = (_iota((tk, tk), 0) >= _iota((tk, tk), 1)).astype(BF16)

        def weights(kb, masked, carry):
            c_rem, g_all = carry
            rows = pl.ds(pl.multiple_of(kb * tk, tk), tk)
            kblk = k_ref[rows, :].astype(BF16)
            vblk = v_ref[rows, :].astype(BF16)
            lbeta, l1m, w = _sb_tile(qb, kblk, (col + kb * tk) < row if masked else None, upper, c_rem)
            g = w * lax.dot_general(dob, vblk, _NT, preferred_element_type=F32)
            dv_ref[rows, :] += lax.dot_general(w.astype(BF16), dob, _TN, preferred_element_type=F32)
            g_s[kb] = g
            sig_s[kb] = jnp.exp(lbeta)
            return c_rem + jnp.sum(l1m, axis=1, keepdims=True), g_all + jnp.sum(g, axis=1, keepdims=True)

        zero_col = jnp.zeros((t, 1), F32)
        _, g_all = _sb_key_loops(qi, per_q, weights, (zero_col, zero_col))

        def scores(kb, masked, carry):
            dq, c_g = carry
            rows = pl.ds(pl.multiple_of(kb * tk, tk), tk)
            g, sig = g_s[kb], sig_s[kb]
            g_before = g_all - (_xdot(g, lower_incl) + c_g)
            dz = g * (1.0 - sig) - g_before * sig
            if masked:
                dz = jnp.where((col + kb * tk) < row, dz, 0.0)
            dz = dz.astype(BF16)
            dq = dq + jnp.dot(dz, k_ref[rows, :].astype(BF16), preferred_element_type=F32)
            dk_ref[rows, :] += lax.dot_general(dz, qb, _TN, preferred_element_type=F32)
            return dq, c_g + jnp.sum(g, axis=1, keepdims=True)

        dq, _ = _sb_key_loops(qi, per_q, scores, (jnp.zeros((t, HEAD_DIM), F32), zero_col))
        dq_ref[...] = dq * scale

    width = n_heads * HEAD_DIM
    return pl.pallas_call(
        body, name=name,
        grid=(n_heads, s // t),
        in_specs=[pl.BlockSpec((t, HEAD_DIM), lambda h, i: (i, h)),
                  pl.BlockSpec((s, HEAD_DIM), lambda h, i: (0, n_heads + h)),
                  pl.BlockSpec((s, HEAD_DIM), lambda h, i: (0, 2 * n_heads + h)),
                  pl.BlockSpec((t, HEAD_DIM), lambda h, i: (i, h))],
        out_specs=[pl.BlockSpec((t, HEAD_DIM), lambda h, i: (i, h)),
                   pl.BlockSpec((s, HEAD_DIM), lambda h, i: (0, h)),
                   pl.BlockSpec((s, HEAD_DIM), lambda h, i: (0, h))],
        out_shape=[jax.ShapeDtypeStruct((s, width), F32)] * 3,
        scratch_shapes=[pltpu.VMEM((s // tk, t, tk), F32)] * 2,
        compiler_params=_params("parallel", "arbitrary"),
    )(proj, proj, proj, dcat)


def _hg_pre(hq, hf, logits):
    mx = jnp.max(logits, axis=0, keepdims=True)
    ex = jnp.exp(logits - mx)
    lb = ex[0:1, :] / jnp.sum(ex, axis=0, keepdims=True)
    f = lb + (1.0 - lb) * _sigmoid(hf)
    return _silu(hq), 1.0 - f, jnp.log(f)


def _hg_post(o, norm_g, hgate):
    return _rms(o, norm_g) * _silu(hgate)


def _hg_specs(s, n_heads, first_block):
    def at(group):
        return pl.BlockSpec((s, HEAD_DIM), lambda h: (0, first_block + group * n_heads + h))
    return [at(0), at(1), at(2), at(3)]


def _hg_fwd(proj, logits, norm_g, n_heads, *, name):
    s = proj.shape[0]
    hc = HG_CHUNK
    n_chunks = s // hc
    d = HEAD_DIM

    hp = HG_HEADS_PER_STEP
    assert n_heads % hp == 0
    wide = hp * d

    def body(lg_ref, ng_ref, hq_ref, hf_ref, hi_ref, hgt_ref, out_ref, oraw_ref, st_ref,
             q_s, k_s, lf_s, cum_s, qc_s, oc_s):
        q, k, lf = _hg_pre(hq_ref[...], hf_ref[...], lg_ref[...])
        q_s[...] = q
        k_s[...] = k
        lf_s[...] = lf
        tril = (_iota((hc, hc), 0) >= _iota((hc, hc), 1)).astype(BF16)
        srow = _iota((hc, d), 0)

        def head_chunk(j, ci, rows, st):
            ln = slice(j * d, (j + 1) * d)
            q, k, v = q_s[rows, ln], k_s[rows, ln], hi_ref[rows, ln]
            cum = _xdot_l(tril, lf_s[rows, ln])
            st_ref[j, ci] = st
            o_inter = _dot(q * jnp.exp(cum), st, _NT)
            cum_s[:, ln] = cum
            qc_s[:, ln] = q
            for t in range(hc):
                ng = (t // SUBLANES + 1) * SUBLANES
                e = jnp.where(srow[:ng] <= t, jnp.exp(cum_s[t:t + 1, ln] - cum[:ng]), 0.0)
                sc = jnp.sum(qc_s[t:t + 1, ln] * k[:ng] * e, axis=1, keepdims=True)
                oc_s[t:t + 1, ln] = jnp.sum(sc * v[:ng], axis=0, keepdims=True)
            oraw_ref[rows, ln] = o_inter + oc_s[:, ln]
            last = cum_s[hc - 1:hc, ln]
            return st * jnp.exp(last) + _dot(v, k * jnp.exp(last - cum), _TN)

        def chunk(ci, states):
            rows = pl.ds(pl.multiple_of(ci * hc, hc), hc)
            return tuple(head_chunk(j, ci, rows, st) for j, st in enumerate(states))

        lax.fori_loop(0, n_chunks, chunk, tuple(jnp.zeros((d, d), F32) for _ in range(hp)))
        for j in range(hp):
            ln = slice(j * d, (j + 1) * d)
            out_ref[:, ln] = _hg_post(oraw_ref[:, ln], ng_ref[...], hgt_ref[:, ln]).astype(BF16)

    width = n_heads * d
    first = 3 * n_heads // hp
    groups = [pl.BlockSpec((s, wide), functools.partial(lambda h, g: (0, first + g * (n_heads // hp) + h), g=g))
              for g in range(4)]
    head_block = pl.BlockSpec((s, wide), lambda h: (0, h))
    return pl.pallas_call(
        body, name=name,
        grid=(n_heads // hp,),
        in_specs=[pl.BlockSpec((2, wide), lambda h: (0, h)), pl.BlockSpec((1, d), lambda h: (0, 0))] + groups,
        out_specs=[head_block, head_block, pl.BlockSpec((hp, n_chunks, d, d), lambda h: (h, 0, 0, 0))],
        out_shape=[jax.ShapeDtypeStruct((s, width), BF16), jax.ShapeDtypeStruct((s, width), F32),
                   jax.ShapeDtypeStruct((n_heads, n_chunks, d, d), F32)],
        scratch_shapes=[pltpu.VMEM((s, wide), F32)] * 3 + [pltpu.VMEM((hc, wide), F32)] * 3,
        compiler_params=_params("arbitrary"),
    )(logits, norm_g, proj, proj, proj, proj)


def _hg_bwd(proj, logits, norm_g, oraw, states, dcat, n_heads, *, name):
    s = proj.shape[0]
    hc = HG_CHUNK
    n_chunks = s // hc
    d = HEAD_DIM

    def body(lg_ref, ng_ref, hq_ref, hf_ref, hi_ref, hgt_ref, oraw_ref, st_ref, dout_ref,
             dhq_ref, dhf_ref, dhi_ref, dhgt_ref, dlg_ref, dng_ref,
             q_s, k_s, lf_s, do_s, dq_s, dk_s, dlf_s, cum_s, qc_s, doc_s, dqc_s, dkc_s, dvc_s):
        head = pl.program_id(0)
        (q, k, lf), pre_vjp = jax.vjp(_hg_pre, hq_ref[...], hf_ref[...], lg_ref[...])
        q_s[...] = q
        k_s[...] = k
        lf_s[...] = lf
        _, post_vjp = jax.vjp(_hg_post, oraw_ref[...], ng_ref[...], hgt_ref[...])
        do, dng, dhgt = post_vjp(dout_ref[...])
        do_s[...] = do
        dhgt_ref[...] = dhgt.astype(BF16)

        @pl.when(head == 0)
        def _():
            dng_ref[...] = dng

        @pl.when(head > 0)
        def _():
            dng_ref[...] += dng

        triu = (_iota((hc, hc), 0) <= _iota((hc, hc), 1)).astype(BF16)
        tril = (_iota((hc, hc), 0) >= _iota((hc, hc), 1)).astype(BF16)
        srow = _iota((hc, d), 0)

        def chunk(j, dst):
            ci = n_chunks - 1 - j
            rows = pl.ds(pl.multiple_of(ci * hc, hc), hc)
            q, k, v, do_c = q_s[rows, :], k_s[rows, :], hi_ref[rows, :], do_s[rows, :]
            cum = _xdot_l(tril, lf_s[rows, :])
            st = st_ref[0, ci]
            cum_s[...] = cum
            qc_s[...] = q
            doc_s[...] = do_c
            last = cum_s[hc - 1:hc, :]
            e_cum, e_last = jnp.exp(cum), jnp.exp(last - cum)
            dqc_s[...] = _dot(do_c, st) * e_cum
            dk_state = _dot(v, dst) * e_last
            dkc_s[...] = dk_state
            dvc_s[...] = _dot(k * e_last, dst, _NT)
            d_last = (jnp.sum(dst * st, axis=0, keepdims=True) * jnp.exp(last)
                      + jnp.sum(k * dk_state, axis=0, keepdims=True))
            for t in range(hc):
                ng = (t // SUBLANES + 1) * SUBLANES
                qt, dot_ = qc_s[t:t + 1, :], doc_s[t:t + 1, :]
                e = jnp.where(srow[:ng] <= t, jnp.exp(cum_s[t:t + 1, :] - cum[:ng]), 0.0)
                ke = k[:ng] * e
                d_a = jnp.sum(dot_ * v[:ng], axis=1, keepdims=True)
                dqc_s[t:t + 1, :] += jnp.sum(d_a * ke, axis=0, keepdims=True)
                dkc_s[0:ng, :] += d_a * (qt * e)
                dvc_s[0:ng, :] += jnp.sum(qt * ke, axis=1, keepdims=True) * dot_
            dq, dk = dqc_s[...], dkc_s[...]
            d_b = q * dq - k * dk
            dq_s[rows, :] = dq
            dk_s[rows, :] = dk
            dhi_ref[rows, :] = dvc_s[...].astype(BF16)
            dlf_s[rows, :] = _xdot_l(triu, d_b) + d_last
            return dst * jnp.exp(last) + _dot(do_c, q * e_cum, _TN)

        lax.fori_loop(0, n_chunks, chunk, jnp.zeros((d, d), F32))
        dhq, dhf, dlg = pre_vjp((dq_s[...], dk_s[...], dlf_s[...]))
        dhq_ref[...] = dhq.astype(BF16)
        dhf_ref[...] = dhf.astype(BF16)
        dlg_ref[...] = dlg

    width = n_heads * d
    head_block = pl.BlockSpec((s, d), lambda h: (0, h))
    return pl.pallas_call(
        body, name=name,
        grid=(n_heads,),
        in_specs=[pl.BlockSpec((2, d), lambda h: (0, h)), pl.BlockSpec((1, d), lambda h: (0, 0))]
        + _hg_specs(s, n_heads, 3 * n_heads)
        + [head_block, pl.BlockSpec((1, n_chunks, d, d), lambda h: (h, 0, 0, 0)),
           pl.BlockSpec((s, d), lambda h: (0, n_heads + h))],
        out_specs=[head_block] * 4 + [pl.BlockSpec((2, d), lambda h: (0, h)), pl.BlockSpec((1, d), lambda h: (0, 0))],
        out_shape=[jax.ShapeDtypeStruct((s, width), BF16)] * 4
        + [jax.ShapeDtypeStruct((2, width), F32), jax.ShapeDtypeStruct((1, d), F32)],
        scratch_shapes=[pltpu.VMEM((s, d), F32)] * 7 + [pltpu.VMEM((hc, d), F32)] * 6,
        compiler_params=_params("arbitrary"),
    )(logits, norm_g, proj, proj, proj, proj, oraw, states, dcat)


def _shift_down(x, n, srow):
    if n == 0:
        return x
    return jnp.where(srow >= n, pltpu.roll(x, n, 0), 0.0)


def _shift_up(x, n, srow):
    if n == 0:
        return x
    s = x.shape[0]
    return jnp.where(srow < s - n, pltpu.roll(x, s - n, 0), 0.0)


def _rg_gates_fwd(proj, conv_w, conv_b, wa, ba, wx, bx, *, name):
    s = proj.shape[0]
    nb = wa.shape[0]
    bw = RG_BLOCK

    def body(xb_ref, cw_ref, cb_ref, wa_ref, ba_ref, wx_ref, bx_ref, xc_ref, ra_ref, ix_ref):
        x = xb_ref[...]
        srow = _iota((s, bw), 0)
        cw = cw_ref[...]
        xc = cb_ref[...] + cw[0:1, :] * x
        for tap in range(1, CONV_TAPS):
            xc = xc + cw[tap:tap + 1, :] * _shift_down(x, tap, srow)
        xc_ref[...] = xc
        ra_ref[...] = _dot(xc, wa_ref[0]) + ba_ref[0]
        ix_ref[...] = _dot(xc, wx_ref[0]) + bx_ref[0]

    col = pl.BlockSpec((s, bw), lambda n: (0, n))
    vec = lambda r: pl.BlockSpec((r, bw), lambda n: (0, n))
    mat = pl.BlockSpec((1, bw, bw), lambda n: (n, 0, 0))
    bias = pl.BlockSpec((1, 1, bw), lambda n: (n, 0, 0))
    return pl.pallas_call(
        body, name=name,
        grid=(nb,),
        in_specs=[pl.BlockSpec((s, bw), lambda n: (0, nb + n)), vec(CONV_TAPS), vec(1), mat, bias, mat, bias],
        out_specs=[col] * 3,
        out_shape=[jax.ShapeDtypeStruct((s, nb * bw), F32)] * 3,
        compiler_params=_params("parallel"),
    )(proj, conv_w, conv_b, wa, ba, wx, bx)


def _rg_au(ra, ix, xc, lam, first_row):
    log_a = -RG_C * _sigmoid(ra) * _softplus(-lam)
    th = jnp.tanh(log_a)
    one_minus_a2 = -2.0 * th / (1.0 - th)
    mult = jnp.where(first_row, 1.0, jnp.sqrt(one_minus_a2))
    return jnp.exp(log_a), xc * _sigmoid(ix) * mult


def _rg_out(gate, hs):
    return _gelu(gate) * hs


def _linear_scan(a, b, a_s, b_s, in_s, reverse):
    s, c = a.shape
    within = _iota((s, c), 0) & (SUBLANES - 1)
    shift = 1
    while shift < SUBLANES:
        if reverse:
            take = within < SUBLANES - shift
            a_n, b_n = pltpu.roll(a, s - shift, 0), pltpu.roll(b, s - shift, 0)
        else:
            take = within >= shift
            a_n, b_n = pltpu.roll(a, shift, 0), pltpu.roll(b, shift, 0)
        b = jnp.where(take, a * b_n + b, b)
        a = jnp.where(take, a * a_n, a)
        shift *= 2
    a_s[...] = a
    b_s[...] = b
    n_tiles = s // SUBLANES
    edge = 0 if reverse else SUBLANES - 1

    def tile(i, h):
        rows = pl.ds(pl.multiple_of(((n_tiles - 1 - i) if reverse else i) * SUBLANES, SUBLANES), SUBLANES)
        in_s[rows, :] = jnp.broadcast_to(h, (SUBLANES, c))
        return a_s[rows, :][edge:edge + 1, :] * h + b_s[rows, :][edge:edge + 1, :]

    lax.fori_loop(0, n_tiles, tile, jnp.zeros((1, c), F32))
    return a * in_s[...] + b


def _rg_scan_fwd(proj, xc, ra, ix, lam, *, name):
    s, width = xc.shape
    tc = LANES

    def body(gate_ref, xc_ref, ra_ref, ix_ref, lam_ref, hs_ref, gact_ref, a_s, u_s, in_s):
        first_row = _iota((s, tc), 0) == 0
        a, u = _rg_au(ra_ref[...], ix_ref[...], xc_ref[...], lam_ref[...], first_row)
        hs = _linear_scan(a, u, a_s, u_s, in_s, reverse=False)
        hs_ref[...] = hs
        gact_ref[...] = _rg_out(gate_ref[...], hs).astype(BF16)

    col = pl.BlockSpec((s, tc), lambda n: (0, n))
    return pl.pallas_call(
        body, name=name,
        grid=(width // tc,),
        in_specs=[col, col, col, col, pl.BlockSpec((1, tc), lambda n: (0, n))],
        out_specs=[col, col],
        out_shape=[jax.ShapeDtypeStruct((s, width), F32), jax.ShapeDtypeStruct((s, width), BF16)],
        scratch_shapes=[pltpu.VMEM((s, tc), F32)] * 3,
        compiler_params=_params("parallel"),
    )(proj, xc, ra, ix, lam)


def _rg_scan_bwd(dgo, proj, hs, xc, ra, ix, lam, *, name):
    s, width = xc.shape
    tc = LANES

    def body(dgo_ref, gate_ref, hs_ref, xc_ref, ra_ref, ix_ref, lam_ref,
             dgate_ref, dra_ref, dix_ref, dxc_ref, dlam_ref, a_s, dh_s, g_s):
        srow = _iota((s, tc), 0)
        hs = hs_ref[...]
        _, out_vjp = jax.vjp(_rg_out, gate_ref[...], hs)
        dgate, dh = out_vjp(dgo_ref[...])
        dgate_ref[...] = dgate.astype(BF16)
        au = functools.partial(_rg_au, first_row=srow == 0)
        (a, _), au_vjp = jax.vjp(au, ra_ref[...], ix_ref[...], xc_ref[...], lam_ref[...])
        g = _linear_scan(_shift_up(a, 1, srow), dh, a_s, dh_s, g_s, reverse=True)
        dra, dix, dxc, dlam = au_vjp((g * _shift_down(hs, 1, srow), g))
        dra_ref[...] = dra.astype(BF16)
        dix_ref[...] = dix.astype(BF16)
        dxc_ref[...] = dxc
        dlam_ref[...] = dlam

    col = pl.BlockSpec((s, tc), lambda n: (0, n))
    vec = pl.BlockSpec((1, tc), lambda n: (0, n))
    return pl.pallas_call(
        body, name=name,
        grid=(width // tc,),
        in_specs=[col] * 6 + [vec],
        out_specs=[col] * 4 + [vec],
        out_shape=[jax.ShapeDtypeStruct((s, width), BF16)] * 3
        + [jax.ShapeDtypeStruct((s, width), F32), jax.ShapeDtypeStruct((1, width), F32)],
        scratch_shapes=[pltpu.VMEM((s, tc), F32)] * 3,
        compiler_params=_params("parallel"),
    )(dgo, proj, hs, xc, ra, ix, lam)


def _rg_gates_bwd(dra, dix, dxc1, xc, proj, conv_w, wa, wx, *, name):
    s = proj.shape[0]
    nb = wa.shape[0]
    bw = RG_BLOCK

    def body(dra_ref, dix_ref, dxc_ref, xc_ref, xb_ref, cw_ref, wa_ref, wx_ref,
             dxb_ref, dcw_ref, dcb_ref, dwa_ref, dba_ref, dwx_ref, dbx_ref):
        dra, dix = dra_ref[...], dix_ref[...]
        xc_t = xc_ref[...].T.astype(BF16)
        dwa_ref[0] = jnp.dot(xc_t, dra, preferred_element_type=F32)
        dwx_ref[0] = jnp.dot(xc_t, dix, preferred_element_type=F32)
        dba_ref[0] = jnp.sum(dra.astype(F32), axis=0, keepdims=True)
        dbx_ref[0] = jnp.sum(dix.astype(F32), axis=0, keepdims=True)
        dxc = dxc_ref[...] + _dot(dra, wa_ref[0], _NT) + _dot(dix, wx_ref[0], _NT)
        srow = _iota((s, bw), 0)
        x = xb_ref[...]
        cw = cw_ref[...]
        dx = cw[0:1, :] * dxc
        dcw = [jnp.sum(dxc * x, axis=0, keepdims=True)]
        for tap in range(1, CONV_TAPS):
            dx = dx + cw[tap:tap + 1, :] * _shift_up(dxc, tap, srow)
            dcw.append(jnp.sum(dxc * _shift_down(x, tap, srow), axis=0, keepdims=True))
        dxb_ref[...] = dx.astype(BF16)
        r4 = _iota((CONV_TAPS, bw), 0)
        acc = jnp.zeros((CONV_TAPS, bw), F32)
        for tap in range(CONV_TAPS):
            acc = jnp.where(r4 == tap, dcw[tap], acc)
        dcw_ref[...] = acc
        dcb_ref[...] = jnp.sum(dxc, axis=0, keepdims=True)

    col = pl.BlockSpec((s, bw), lambda n: (0, n))
    vec = lambda r: pl.BlockSpec((r, bw), lambda n: (0, n))
    mat = pl.BlockSpec((1, bw, bw), lambda n: (n, 0, 0))
    bias = pl.BlockSpec((1, 1, bw), lambda n: (n, 0, 0))
    width = nb * bw
    return pl.pallas_call(
        body, name=name,
        grid=(nb,),
        in_specs=[col, col, col, col, pl.BlockSpec((s, bw), lambda n: (0, nb + n)), vec(CONV_TAPS), mat, mat],
        out_specs=[col, vec(CONV_TAPS), vec(1), mat, bias, mat, bias],
        out_shape=[jax.ShapeDtypeStruct((s, width), BF16), jax.ShapeDtypeStruct((CONV_TAPS, width), F32),
                   jax.ShapeDtypeStruct((1, width), F32), jax.ShapeDtypeStruct((nb, bw, bw), F32),
                   jax.ShapeDtypeStruct((nb, 1, bw), F32), jax.ShapeDtypeStruct((nb, bw, bw), F32),
                   jax.ShapeDtypeStruct((nb, 1, bw), F32)],
        compiler_params=_params("parallel"),
    )(dra, dix, dxc1, xc, proj, conv_w, wa, wx)


_HBM = pl.BlockSpec(memory_space=pltpu.HBM)
_FLIPS = ((0, 0, 1), (1, 0, 0), (0, 1, 0), (1, 1, 0))
_ALL_FLIPS = tuple((a, b, c) for a in (0, 1) for b in (0, 1) for c in (0, 1))[1:]


def _flip(pos, f):
    return tuple(1 - p if b else p for p, b in zip(pos, f))


def _dev_index(pos):
    return 4 * pos[0] + 2 * pos[1] + pos[2]


def _block(ref, idx, cols):
    if not cols:
        return ref.at[idx]
    n = ref.shape[-1] // N_DEV
    start = pl.multiple_of(idx * n, LANES)
    return ref.at[(slice(None),) * (len(ref.shape) - 1) + (pl.ds(start, n),)]


_SEM = pl.BlockSpec(memory_space=pltpu.SEMAPHORE)
_ANY = pl.BlockSpec(memory_space=pl.ANY)
_N_PEERS = N_DEV - 1


def _hbm(x):
    return pltpu.with_memory_space_constraint(x, pltpu.HBM)


def _me():
    return lax.axis_index("x"), lax.axis_index("y"), lax.axis_index("c")


def _spread_copies(plan, src_refs, land_refs, send_sems, recv_sems, local_sems):
    local, remote = plan(src_refs, land_refs)
    local = [pltpu.make_async_copy(src, dst, local_sems.at[i]) for i, (src, dst) in enumerate(local)]
    remote = [pltpu.make_async_remote_copy(src_ref=src, dst_ref=dst, send_sem=send_sems.at[k], recv_sem=recv_sems.at[k],
                                           device_id=peer, device_id_type=pl.DeviceIdType.MESH)
              for k, (src, dst, peer) in enumerate(remote)]
    return local, remote


def _spread_start(srcs, lands, plan, n_remote, n_local, *, name, after=()):
    ns, nl = len(srcs), len(lands)
    n_in = ns + nl + len(after)

    def body(*refs):
        src_refs, land_refs = refs[:ns], refs[ns:ns + nl]
        send_sems, recv_sems, local_sems = refs[n_in:n_in + 3]
        local, remote = _spread_copies(plan, src_refs, land_refs, send_sems, recv_sems, local_sems)
        for cp in local + remote:
            cp.start()
        token = refs[-1]
        token[...] = jnp.zeros_like(token)

    lands = [_hbm(lax.empty(*x)) if isinstance(x, tuple) else x for x in lands]
    out = pl.pallas_call(
        body, name=name,
        in_specs=[_HBM] * (ns + nl) + [_ANY] * len(after),
        out_specs=[_SEM] * 3 + [_HBM] * (ns + nl) + [pl.BlockSpec(memory_space=pltpu.VMEM)],
        out_shape=[pltpu.SemaphoreType.DMA((n_remote,)), pltpu.SemaphoreType.DMA((n_remote,)),
                   pltpu.SemaphoreType.DMA((max(n_local, 1),))]
        + [pltpu.HBM(x.shape, x.dtype) for x in list(srcs) + lands]
        + [jax.ShapeDtypeStruct((SUBLANES, LANES), F32)],
        input_output_aliases={i: 3 + i for i in range(ns + nl)},
        compiler_params=pltpu.CompilerParams(has_side_effects=pltpu.SideEffectType.DATAFLOW_SIDE_EFFECTING),
    )(*[_hbm(x) for x in srcs], *lands, *after)
    return dict(sems=list(out[:3]), srcs=list(out[3:3 + ns]), lands=list(out[3 + ns:3 + ns + nl]),
                token=out[-1], plan=plan)


def _spread_wait(handle, after, *, name):
    ns, nl = len(handle["srcs"]), len(handle["lands"])

    def body(*refs):
        src_refs, land_refs = refs[:ns], refs[ns:ns + nl]
        send_sems, recv_sems, local_sems = refs[ns + nl:ns + nl + 3]
        local, remote = _spread_copies(handle["plan"], src_refs, land_refs, send_sems, recv_sems, local_sems)
        for cp in local:
            cp.wait()
        for cp in remote:
            cp.wait_send()
            cp.wait_recv()

    out = pl.pallas_call(
        body, name=name,
        in_specs=[_HBM] * (ns + nl) + [_SEM] * 3 + [_ANY],
        out_specs=[_HBM] * (ns + nl),
        out_shape=[pltpu.HBM(x.shape, x.dtype) for x in handle["srcs"] + handle["lands"]],
        input_output_aliases={i: i for i in range(ns + nl)},
        compiler_params=pltpu.CompilerParams(has_side_effects=pltpu.SideEffectType.DATAFLOW_SIDE_EFFECTING),
    )(*handle["srcs"], *handle["lands"], *handle["sems"], after)
    return list(out[ns:])


def _gather_start(x, *, name, cols=False, relayed=True, after=()):
    shape = x.shape[:-1] + (N_DEV * x.shape[-1],) if cols else (N_DEV,) + x.shape
    flips = _FLIPS if relayed else _ALL_FLIPS

    def plan(src_refs, land_refs):
        me = _me()
        mine = _block(land_refs[0], _dev_index(me), cols)
        return [(src_refs[0], mine)], [(src_refs[0], mine, _flip(me, f)) for f in flips]

    handle = _spread_start([x], [(shape, x.dtype)], plan, len(flips), 1, name=name, after=after)
    handle["cols"] = cols
    return handle


def _gather_relay(handle, after, *, name):
    cols = handle["cols"]
    land, = _spread_wait(handle, after, name=f"{name}_arrived")

    def plan(src_refs, land_refs):
        me = _me()
        blocks = [_block(land_refs[0], _dev_index(_flip(me, f)), cols) for f in _FLIPS[1:]]
        return [], [(blk, blk, _flip(me, _FLIPS[0])) for blk in blocks]

    return _spread_start([], [land], plan, len(_FLIPS) - 1, 0, name=f"{name}_pass")


def _exchange_start(ps, *, name, cols=False, after=()):
    blk = ps[0].shape[:-1] + (ps[0].shape[-1] // N_DEV,) if cols else ps[0].shape[1:]

    def plan(src_refs, land_refs):
        me = _me()
        me_idx = _dev_index(me)
        local = [(_block(src, me_idx, cols), land_refs[0].at[me_idx, a]) for a, src in enumerate(src_refs)]
        remote = [(_block(src, _dev_index(_flip(me, f)), cols), land_refs[0].at[me_idx, a], _flip(me, f))
                  for f in _ALL_FLIPS for a, src in enumerate(src_refs)]
        return local, remote

    return _spread_start(ps, [((N_DEV, len(ps)) + blk, ps[0].dtype)], plan, _N_PEERS * len(ps), len(ps), name=name,
                         after=after)


def _adamw(parts, w, m, v, *, name, layer=0, prev=None):
    n_rows, c = w.shape
    r = parts.shape[1]
    row_bytes = c * (N_DEV * parts.dtype.itemsize + 7 * 4) * 2
    tr = r
    for cand in (512, 256, 128, 64, 32, 16):
        if r % cand == 0 and cand * row_bytes <= ADAMW_BLOCK_BYTES:
            tr = cand
            break
    c1 = 1.0 - ADAM_B1 ** ADAM_STEP
    c2 = 1.0 - ADAM_B2 ** ADAM_STEP

    def body(p_ref, w_ref, m_ref, v_ref, *rest):
        g_ref, d_ref, nm_ref, nv_ref = rest[-4:]
        g = p_ref[0].astype(F32)
        for j in range(1, N_DEV):
            g = g + p_ref[j].astype(F32)
        nm = ADAM_B1 * m_ref[...] + (1.0 - ADAM_B1) * g
        nv = ADAM_B2 * v_ref[...] + (1.0 - ADAM_B2) * (g * g)
        g_ref[...] = g
        nm_ref[...] = nm
        nv_ref[...] = nv
        d_ref[...] = -ADAM_LR * ((nm * (1.0 / c1)) / (jnp.sqrt(nv * (1.0 / c2)) + ADAM_EPS) + ADAM_WD * w_ref[...])

    off = layer * (r // tr)
    blk = pl.BlockSpec((tr, c), lambda i: (i + off, 0))
    prev = list(prev) if prev is not None else []
    return pl.pallas_call(
        body, name=name,
        grid=(r // tr,),
        in_specs=[pl.BlockSpec((N_DEV, tr, c), lambda i: (0, i, 0)), blk, blk, blk] + [_ANY] * len(prev),
        out_specs=[blk] * 4,
        out_shape=[jax.ShapeDtypeStruct((n_rows, c), F32)] * 4,
        input_output_aliases={4 + j: j for j in range(len(prev))},
        compiler_params=_params("parallel"),
    )(parts, w, m, v, *prev)


_TN_CANDS = (512, 256, 128)
_TK_MAX = 5632
_TK_WHOLE_ROWS = 2816


def _contraction_tiles(m, k):
    tk = k
    while tk > _TK_MAX and tk % 2 == 0 and (tk // 2) % LANES == 0:
        tk //= 2
    tm = m if tk <= _TK_WHOLE_ROWS or m % 2 else m // 2
    return tm, tk


def _nn(a, b, name, out_dtype=F32):
    tm, tk = _contraction_tiles(*a.shape)
    return _mm(a, b, "nn", name=name, out_dtype=out_dtype, tm=tm, tn=_pick(b.shape[1], _TN_CANDS), tk=tk)


def _nt(a, b, name, out_dtype=F32, deps=()):
    tm, tk = _contraction_tiles(*a.shape)
    return _mm(a, b, "nt", name=name, out_dtype=out_dtype, tm=tm, tn=_pick(b.shape[0], _TN_CANDS), tk=tk,
               deps=deps)


def _tn(a, b, name, out_dtype=BF16, deps=()):
    assert a.shape[0] == b.shape[0], (a.shape, b.shape)
    return _mm_tn(a, b, name=name, out_dtype=out_dtype, tm=_pick(a.shape[1], _TN_CANDS),
                  tn=_pick(b.shape[1], (1024,) + _TN_CANDS), deps=deps)


def _local_step(x, p, target, rep, weight, emit, n_heads, start_tokens=()):
    s, d = x.shape
    depth = p.shape[0]
    grads = {}
    rep_grads = {k: [None] * depth for k in ("mix_pre_g", "mix_post_g", "ffn_pre_g", "ffn_post_g", "ple_norm_g")}

    pending = list(start_tokens)
    gains = {}

    def gain(name, i):
        if (name, i) not in gains:
            gains[name, i] = rep[name][i:i + 1]
        return gains[name, i]

    def send(name, layer, g):
        token = emit(name, layer, g)
        if token is not None:
            pending.append(token)

    def rowcall(*args, **kwargs):
        deps, pending[:] = tuple(pending), []
        return _rowcall(*args, deps=deps, **kwargs)

    def grad_in(*args, **kwargs):
        deps, pending[:] = tuple(pending), []
        return _nt(*args, deps=deps, **kwargs)

    deferred = []

    def send_small(i, name, layer, a, b, mm_name):
        if i == 0 and depth > 1:
            deferred.append((name, layer, a, b, mm_name))
        else:
            send(name, layer, _tn(a, b, mm_name))

    saved = []
    h = x
    for i in range(depth):
        sv = {"h": h}
        n1, = rowcall(f"pre_norm{i}", lambda hh, g: _rms(hh, g), [h], [gain("mix_pre_g", i)], [BF16], cols=d)
        sv["n1"] = n1
        if i % 2 == 0:
            proj = _nn(n1, weight("w_in_even", 0, n1), f"in_even{i}")
            a_out = _sb_fwd(proj, n_heads, name=f"sb_fwd{i}")
            b_out, oraw, states = _hg_fwd(proj, rep["hg_lb_logits"], rep["hg_norm_g"], n_heads, name=f"hg_fwd{i}")
            cat = jnp.concatenate([a_out.astype(BF16), b_out], axis=1)
            m = _nn(cat, weight("w_out_even", 0, cat), f"out_even{i}")
            sv.update(proj=proj, oraw=oraw, states=states, cat=cat)
        else:
            proj = _nn(n1, weight("w_in_odd", 0, n1), f"in_odd{i}")
            sm = {k: weight(k, 0, proj) for k in _SMALL}
            xc, ra, ix = _rg_gates_fwd(proj, sm["conv_w"], sm["conv_b"], sm["rg_wa"], sm["rg_ba"],
                                       sm["rg_wx"], sm["rg_bx"], name=f"rg_gates_fwd{i}")
            hs, gact = _rg_scan_fwd(proj, xc, ra, ix, sm["rg_lambda"], name=f"rg_scan_fwd{i}")
            m = _nn(gact, weight("w_out_odd", 0, gact), f"out_odd{i}")
            sv.update(proj=proj, xc=xc, ra=ra, ix=ix, hs=hs, gact=gact, sm=sm)

        def post_mix(hh, mm, g_post, g_pre):
            h1 = hh + _rms(mm, g_post)
            return h1, _rms(h1, g_pre)

        h1, n2 = rowcall(f"post_mix{i}", post_mix, [h, m], [gain("mix_post_g", i), gain("ffn_pre_g", i)],
                          [F32, BF16], cols=d)
        gate, up, act = _gate_up(n2, weight("w_gate_up", i, n2), name=f"gate_up{i}")
        f = _nn(act, weight("w_down", i, act), f"down{i}")

        def post_ffn(hh, ff_out, g_post):
            h2 = hh + _rms(ff_out, g_post)
            return h2, h2

        h2, h2b = rowcall(f"post_ffn{i}", post_ffn, [h1, f], [gain("ffn_post_g", i)], [F32, BF16], cols=d)
        e = _nn(p[i], weight("w_ple_up", i, h2b), f"ple_up{i}")
        gl = _nn(h2b, weight("w_ple_gate", i, h2b), f"ple_gate{i}")
        h3, = rowcall(f"ple{i}", lambda hh, a, b, g: hh + _rms(_sigmoid(a) * b, g), [h2, gl, e],
                       [gain("ple_norm_g", i)], [F32], cols=d)
        sv.update(m=m, h1=h1, n2=n2, gate=gate, up=up, act=act, f=f, h2b=h2b, e=e, gl=gl)
        saved.append(sv)
        h = h3

    def loss_fn(y, t):
        err = y - t
        return err * (1.0 / d), jnp.sum(err * err, axis=0, keepdims=True) * (0.5 / d)

    dh, loss_cols = rowcall("loss", loss_fn, [h, target], [], [F32], red_rows=(1,), cols=d)

    for i in reversed(range(depth)):
        sv = saved[i]

        def ple_bwd(dy, a, b, g):
            _, vjp = jax.vjp(lambda a_, b_, g_: _rms(_sigmoid(a_) * b_, g_), a, b, g)
            return vjp(dy)

        dgl, de, rep_grads["ple_norm_g"][i] = rowcall(
            f"ple_bwd{i}", ple_bwd, [dh, sv["gl"], sv["e"]], [gain("ple_norm_g", i)], [BF16, BF16],
            red_rows=(1,), cols=d)
        send_small(i, "w_ple_up", i, p[i], de, f"d_ple_up{i}")
        send_small(i, "w_ple_gate", i, sv["h2b"], dgl, f"d_ple_gate{i}")
        dh2_ple = grad_in(dgl, weight("w_ple_gate", i, dgl), f"dx_ple_gate{i}")

        def post_ffn_bwd(dy, dx, ff_out, g):
            dh2 = dy + dx
            _, vjp = jax.vjp(_rms, ff_out, g)
            df, dg = vjp(dh2)
            return dh2, df, dg

        dh2, df, rep_grads["ffn_post_g"][i] = rowcall(
            f"post_ffn_bwd{i}", post_ffn_bwd, [dh, dh2_ple, sv["f"]], [gain("ffn_post_g", i)], [F32, BF16],
            red_rows=(1,), cols=d)
        send("w_down", i, _tn(sv["act"], df, f"d_down{i}"))
        dact = grad_in(df, weight("w_down", i, df), f"dx_down{i}", out_dtype=BF16)
        dgu = _swiglu_bwd(sv["gate"], sv["up"], dact, name=f"swiglu_bwd{i}")
        send("w_gate_up", i, _tn(sv["n2"], dgu, f"d_gate_up{i}"))
        dn2 = grad_in(dgu, weight("w_gate_up", i, dgu), f"dx_gate_up{i}")

        def post_mix_bwd(dy, dn, h1, mm, g_post, g_pre):
            _, vjp_pre = jax.vjp(_rms, h1, g_pre)
            dh1_n, dg_pre = vjp_pre(dn)
            dh1 = dy + dh1_n
            _, vjp_post = jax.vjp(_rms, mm, g_post)
            dm, dg_post = vjp_post(dh1)
            return dh1, dm, dg_pre, dg_post

        dh1, dm, rep_grads["ffn_pre_g"][i], rep_grads["mix_post_g"][i] = rowcall(
            f"post_mix_bwd{i}", post_mix_bwd, [dh2, dn2, sv["h1"], sv["m"]],
            [gain("mix_post_g", i), gain("ffn_pre_g", i)], [F32, BF16], red_rows=(1, 1), cols=d)

        if i % 2 == 0:
            send_small(i, "w_out_even", 0, sv["cat"], dm, f"d_out_even{i}")
            dcat = grad_in(dm, weight("w_out_even", 0, dm), f"dx_out_even{i}")
            dq, dk, dv = _sb_bwd(sv["proj"], dcat, n_heads, name=f"sb_bwd{i}")
            dhq, dhf, dhi, dhg, grads["hg_lb_logits"], grads["hg_norm_g"] = _hg_bwd(
                sv["proj"], rep["hg_lb_logits"], rep["hg_norm_g"], sv["oraw"], sv["states"], dcat, n_heads,
                name=f"hg_bwd{i}")
            dproj = jnp.concatenate([dq.astype(BF16), dk.astype(BF16), dv.astype(BF16), dhq, dhf, dhi, dhg], axis=1)
            send("w_in_even", 0, _tn(sv["n1"], dproj, f"d_in_even{i}"))
            dn1 = grad_in(dproj, weight("w_in_even", 0, dproj), f"dx_in_even{i}")
        else:
            sm = sv["sm"]
            send_small(i, "w_out_odd", 0, sv["gact"], dm, f"d_out_odd{i}")
            dgo = grad_in(dm, weight("w_out_odd", 0, dm), f"dx_out_odd{i}")
            dgate, dra, dix, dxc1, grads["rg_lambda"] = _rg_scan_bwd(
                dgo, sv["proj"], sv["hs"], sv["xc"], sv["ra"], sv["ix"], sm["rg_lambda"], name=f"rg_scan_bwd{i}")
            (dxb, grads["conv_w"], grads["conv_b"], grads["rg_wa"], grads["rg_ba"], grads["rg_wx"],
             grads["rg_bx"]) = _rg_gates_bwd(dra, dix, dxc1, sv["xc"], sv["proj"], sm["conv_w"], sm["rg_wa"],
                                            sm["rg_wx"], name=f"rg_gates_bwd{i}")
            send("small", 0, {k: grads.pop(k) for k in _SMALL})
            dproj = jnp.concatenate([dgate, dxb], axis=1)
            send("w_in_odd", 0, _tn(sv["n1"], dproj, f"d_in_odd{i}"))
            dn1 = grad_in(dproj, weight("w_in_odd", 0, dproj), f"dx_in_odd{i}")

        def pre_norm_bwd(dy, dn, hh, g):
            _, vjp = jax.vjp(_rms, hh, g)
            dx, dg = vjp(dn)
            return dy + dx, dg

        dh, rep_grads["mix_pre_g"][i] = rowcall(
            f"pre_norm_bwd{i}", pre_norm_bwd, [dh1, dn1, sv["h"]], [gain("mix_pre_g", i)], [F32],
            red_rows=(1,), cols=d)

    for name, layer, a, b, mm_name in deferred:
        send(name, layer, _tn(a, b, mm_name, deps=(dh,)))
    for k, rows in rep_grads.items():
        grads[k] = jnp.concatenate(rows, axis=0)
    return loss_cols, dh, grads


_WEIGHTS = ("mix_pre_g", "mix_post_g", "ffn_pre_g", "ffn_post_g", "ple_norm_g", "w_in_even", "w_out_even",
            "hg_lb_logits", "hg_norm_g", "w_in_odd", "conv_w", "conv_b", "rg_wa", "rg_ba", "rg_wx", "rg_bx",
            "rg_lambda", "w_out_odd", "w_gate_up", "w_down", "w_ple_up", "w_ple_gate")
_REPLICATED = ("mix_pre_g", "mix_post_g", "ffn_pre_g", "ffn_post_g", "ple_norm_g", "hg_lb_logits", "hg_norm_g")
_SMALL = ("conv_w", "conv_b", "rg_wa", "rg_ba", "rg_wx", "rg_bx", "rg_lambda")
_BIG = {"w_in_even": True, "w_out_even": False, "w_in_odd": True, "w_out_odd": False,
        "w_gate_up": True, "w_down": False, "w_ple_up": True, "w_ple_gate": False}
_PACK_ROW = SUBLANES * LANES


def _pack(arrays):
    flat = jnp.concatenate([a.reshape(-1) for a in arrays])
    pad = -flat.shape[0] % _PACK_ROW
    return jnp.pad(flat, (0, pad)).reshape(-1, LANES)


def _pack_blocks(arrays):
    flat = jnp.concatenate([a.reshape(N_DEV, -1) for a in arrays], axis=1)
    pad = -flat.shape[1] % _PACK_ROW
    return jnp.pad(flat, ((0, 0), (0, pad))).reshape(N_DEV, -1, LANES)


def _unpack(packed, shapes, lead=()):
    flat = packed.reshape(lead + (-1,))
    out, pos = [], 0
    for shape in shapes:
        n = math.prod(shape)
        out.append(flat[..., pos:pos + n].reshape(lead + tuple(shape)))
        pos += n
    return out


def _to_full_small(name, blocks):
    if name == "conv_w":
        return jnp.transpose(blocks, (1, 0, 2)).reshape(blocks.shape[1], -1)
    if name in ("conv_b", "rg_lambda"):
        return blocks.reshape(1, -1)
    nb = blocks.shape[1]
    if name in ("rg_wa", "rg_wx"):
        return jnp.transpose(blocks, (1, 0, 2, 3)).reshape(nb, RG_BLOCK, RG_BLOCK)
    return jnp.transpose(blocks, (1, 0, 2)).reshape(nb, 1, RG_BLOCK)


def _to_blocks_small(name, full):
    if name == "conv_w":
        return jnp.transpose(full.reshape(full.shape[0], N_DEV, -1), (1, 0, 2))
    if name in ("conv_b", "rg_lambda"):
        return full.reshape(N_DEV, -1)
    nb = full.shape[0]
    if name in ("rg_wa", "rg_wx"):
        return jnp.transpose(full.reshape(nb, N_DEV, RG_BLOCK // N_DEV, RG_BLOCK), (1, 0, 2, 3))
    return jnp.transpose(full.reshape(nb, N_DEV, RG_BLOCK // N_DEV), (1, 0, 2))


def _step(inp):
    w = {k: inp[k] for k in _WEIGHTS}
    x, p, target = inp["x"][0], inp["p"][:, 0], inp["loss_target"][0]
    assert w["hg_lb_logits"].shape[0] == 2 and w["w_in_even"].shape[0] == 1 and w["w_in_odd"].shape[0] == 1

    n_heads = w["w_in_even"].shape[2] * N_DEV // (7 * HEAD_DIM)
    small_shapes = [w[k].shape[1:] for k in _SMALL]

    def lands_in_place(name):
        return _BIG[name] and w[name].shape[2] % LANES == 0

    depth = p.shape[0]
    order = [("w_in_even", 0), ("w_out_even", 0)] if depth else []
    for i in range(depth):
        if i == 1:
            order += [("w_in_odd", 0), ("small", 0), ("w_out_odd", 0)]
        order += [("w_gate_up", i), ("w_down", i), ("w_ple_up", i), ("w_ple_gate", i)]
    heavy = ("w_gate_up", "w_down", "w_in_odd", "w_out_odd")
    gathers = {}
    first_started = 0.0
    last_token = ()
    for name, l in sorted(order, key=lambda key: key[0] in heavy):
        if name == "small":
            gathers[name, l] = _gather_start(_pack([w[k][0] for k in _SMALL]) + first_started, name="gather_small",
                                             relayed=False, after=last_token)
        else:
            gathers[name, l] = _gather_start((w[name][l] + first_started).astype(BF16), name=f"gather_{name}{l}",
                                             cols=lands_in_place(name), after=last_token)
        last_token = (gathers[name, l]["token"],)
        if len(gathers) == 1:
            first_started = last_token[0][0, 0]
    ready = {}

    def relay(key, after):
        if "cols" in gathers[key] and key[0] != "small":
            gathers[key] = _gather_relay(gathers[key], after, name=f"gather_{key[0]}{key[1]}")

    def weight(name, layer, after):
        key = ("small", 0) if name in _SMALL else (name, layer)
        if key not in ready:
            relay(key, after)
            for nxt in order[order.index(key) + 1:order.index(key) + 2]:
                relay(nxt, after)
            land, = _spread_wait(gathers[key], after, name=f"gathered_{key[0]}{key[1]}")
            if name in _SMALL:
                ready[key] = {k: _to_full_small(k, b)
                              for k, b in zip(_SMALL, _unpack(land, small_shapes, lead=(N_DEV,)))}
            elif lands_in_place(name):
                ready[key] = land
            elif _BIG[name]:
                ready[key] = jnp.transpose(land, (1, 0, 2)).reshape(land.shape[1], -1)
            else:
                ready[key] = land.reshape(-1, land.shape[2])
        return ready[key][name] if name in _SMALL else ready[key]

    exchanges = []

    def emit(name, layer, g):
        after = tuple(h["token"] for _, _, h in exchanges[-1:])
        if name == "small":
            handle = _exchange_start([_pack_blocks([_to_blocks_small(k, g[k]) for k in _SMALL])],
                                     name="exchange_small", after=after)
        elif lands_in_place(name):
            handle = _exchange_start([g], name=f"exchange_{name}{layer}", cols=True, after=after)
        elif _BIG[name]:
            c = w[name].shape[2]
            handle = _exchange_start([jnp.transpose(g.reshape(-1, N_DEV, c), (1, 0, 2))],
                                     name=f"exchange_{name}{layer}", after=after)
        else:
            handle = _exchange_start([g.reshape((N_DEV,) + w[name].shape[1:])], name=f"exchange_{name}{layer}",
                                     after=after)
        exchanges.append((name, layer, handle))
        return handle["token"]

    rep = {k: w[k] for k in _REPLICATED}
    loss_cols, dx, grads = _local_step(x, p, target, rep, weight, emit, n_heads,
                                       [h["token"] for h in gathers.values()])

    loss_part = jnp.sum(loss_cols).reshape(1)
    rep_gather = _gather_start(_pack([grads[k] for k in _REPLICATED] + [loss_part]), name="gather_rep_grads",
                               relayed=False)

    out = {}
    after = dx
    for name, layer, handle in exchanges:
        land, = _spread_wait(handle, after, name=f"exchanged_{name}{layer}")
        if name == "small":
            res = _adamw(land.reshape(N_DEV, -1, LANES), *[_pack([inp[pre + k][0] for k in _SMALL]) for pre in ("", "m_", "v_")],
                         name="adamw_small")
            for k, *vals in zip(_SMALL, *[_unpack(a, small_shapes) for a in res]):
                out[k] = [v[None] for v in vals]
        else:
            n_l, r, c = w[name].shape
            res = out[name] = _adamw(land.reshape(N_DEV, r, c),
                                     *[inp[pre + name].reshape(n_l * r, c) for pre in ("", "m_", "v_")],
                                     name=f"adamw_{name}{layer}", layer=layer, prev=out.get(name))
        after = res[0]
    for name in _BIG:
        out[name] = [a.reshape(w[name].shape) for a in out[name]]

    rep_shapes = [w[k].shape for k in _REPLICATED] + [(1,)]
    rep_parts, = _spread_wait(rep_gather, after, name="gathered_rep_grads")
    res = _adamw(rep_parts, *[_pack([inp[pre + k] for k in _REPLICATED] + [jnp.zeros((1,), F32)])
                              for pre in ("", "m_", "v_")], name="adamw_rep")
    for k, *vals in zip(_REPLICATED + ("loss",), *[_unpack(a, rep_shapes) for a in res]):
        out[k] = vals
    loss = out["loss"][0][0]

    return (loss, dx[None]) + tuple(out[k][j] for j in range(4) for k in _WEIGHTS)


def kernel(x, p, mix_pre_g, mix_post_g, ffn_pre_g, ffn_post_g, ple_norm_g, w_in_even, w_out_even, hg_lb_logits, hg_norm_g, w_in_odd, conv_w, conv_b, rg_wa, rg_ba, rg_wx, rg_bx, rg_lambda, w_out_odd, w_gate_up, w_down, w_ple_up, w_ple_gate, loss_target, m_mix_pre_g, m_mix_post_g, m_ffn_pre_g, m_ffn_post_g, m_ple_norm_g, m_w_in_even, m_w_out_even, m_hg_lb_logits, m_hg_norm_g, m_w_in_odd, m_conv_w, m_conv_b, m_rg_wa, m_rg_ba, m_rg_wx, m_rg_bx, m_rg_lambda, m_w_out_odd, m_w_gate_up, m_w_down, m_w_ple_up, m_w_ple_gate, v_mix_pre_g, v_mix_post_g, v_ffn_pre_g, v_ffn_post_g, v_ple_norm_g, v_w_in_even, v_w_out_even, v_hg_lb_logits, v_hg_norm_g, v_w_in_odd, v_conv_w, v_conv_b, v_rg_wa, v_rg_ba, v_rg_wx, v_rg_bx, v_rg_lambda, v_w_out_odd, v_w_gate_up, v_w_down, v_w_ple_up, v_w_ple_gate):
    return _step(dict(locals()))
```

```python
import functools
import math

import jax
import jax.numpy as jnp
from jax import lax
from jax.experimental import pallas as pl
from jax.experimental.pallas import tpu as pltpu

F32 = jnp.float32
BF16 = jnp.bfloat16

VMEM_LIMIT_BYTES = 56 * 1024 * 1024
ADAMW_BLOCK_BYTES = 40 * 1024 * 1024
LANES = 128
SUBLANES = 8

N_DEV = 8
HEAD_DIM = 128
SB_Q_TILE = 512
SB_K_TILE = 128
HG_CHUNK = 32
HG_HEADS_PER_STEP = 2
RG_BLOCK = 256
CONV_TAPS = 4
RG_C = 8.0
RMS_EPS = 1e-6

ADAM_LR = 0.001
ADAM_B1 = 0.9
ADAM_B2 = 0.999
ADAM_EPS = 1e-08
ADAM_WD = 0.01
ADAM_STEP = 10


def _params(*sem):
    return pltpu.CompilerParams(dimension_semantics=sem, vmem_limit_bytes=VMEM_LIMIT_BYTES)


def _pick(n, cands):
    for c in cands:
        if c <= n and n % c == 0:
            return c
    return n


def _mm(a, b, mode, *, name, out_dtype=F32, tm=512, tn=512, tk=None, deps=()):
    if mode == "nn":
        (m, k), (k2, n) = a.shape, b.shape
    else:
        (m, k), (n, k2) = a.shape, b.shape
    assert k == k2, (a.shape, b.shape, mode)
    tm, tn = min(tm, m), min(tn, n)
    tk = k if tk is None else min(tk, k)
    assert m % tm == 0 and n % tn == 0 and k % tk == 0, (m, n, k, tm, tn, tk)
    nk = k // tk

    a_spec = pl.BlockSpec((tm, tk), lambda i, j, kk: (i, kk))
    if mode == "nn":
        b_spec = pl.BlockSpec((tk, tn), lambda i, j, kk: (kk, j))
        dims = (((1,), (0,)), ((), ()))
    else:
        b_spec = pl.BlockSpec((tn, tk), lambda i, j, kk: (j, kk))
        dims = (((1,), (1,)), ((), ()))

    def body(a_ref, b_ref, *refs):
        o_ref, *acc = refs[len(deps):]
        part = lax.dot_general(a_ref[...].astype(BF16), b_ref[...].astype(BF16), dims, preferred_element_type=F32)
        if nk == 1:
            o_ref[...] = part.astype(out_dtype)
        else:
            acc_ref, = acc
            kk = pl.program_id(2)

            @pl.when(kk == 0)
            def _():
                acc_ref[...] = part

            @pl.when(kk > 0)
            def _():
                acc_ref[...] += part

            @pl.when(kk == nk - 1)
            def _():
                o_ref[...] = acc_ref[...].astype(out_dtype)

    return pl.pallas_call(
        body, name=name,
        grid=(m // tm, n // tn, nk),
        in_specs=[a_spec, b_spec] + [pl.BlockSpec(memory_space=pl.ANY)] * len(deps),
        out_specs=pl.BlockSpec((tm, tn), lambda i, j, kk: (i, j)),
        out_shape=jax.ShapeDtypeStruct((m, n), out_dtype),
        scratch_shapes=[] if nk == 1 else [pltpu.VMEM((tm, tn), F32)],
        compiler_params=_params("parallel", "parallel", "arbitrary"),
    )(a, b, *deps)


def _mm_tn(a, b, *, name, out_dtype, tm, tn, deps=()):
    k, m = a.shape
    n = b.shape[1]

    def body(a_ref, b_ref, *refs):
        o_ref, at_ref = refs[len(deps):]

        @pl.when(pl.program_id(1) == 0)
        def _():
            at_ref[...] = a_ref[...].astype(F32).T.astype(BF16)

        o_ref[...] = jnp.dot(at_ref[...], b_ref[...].astype(BF16), preferred_element_type=F32).astype(out_dtype)

    return pl.pallas_call(
        body, name=name,
        grid=(m // tm, n // tn),
        in_specs=[pl.BlockSpec((k, tm), lambda i, j: (0, i)), pl.BlockSpec((k, tn), lambda i, j: (0, j))]
        + [pl.BlockSpec(memory_space=pl.ANY)] * len(deps),
        out_specs=pl.BlockSpec((tm, tn), lambda i, j: (i, j)),
        out_shape=jax.ShapeDtypeStruct((m, n), out_dtype),
        scratch_shapes=[pltpu.VMEM((tm, k), BF16)],
        compiler_params=_params("parallel", "arbitrary"),
    )(a, b, *deps)


def _rowcall(name, fn, rows, pars, row_outs, red_rows=(), *, cols, ts=256, tc=None, deps=()):
    rows = [r if isinstance(r, tuple) else (r, 0) for r in rows]
    pars = [p if isinstance(p, tuple) else (p, 0) for p in pars]
    s = rows[0][0].shape[0]
    tc = cols if tc is None else tc
    ts = min(ts, s)
    assert s % ts == 0 and cols % tc == 0, (name, s, ts, cols, tc)
    n_in, n_row_out = len(rows) + len(pars), len(row_outs)

    def body(*refs):
        outs = fn(*[r[...] for r in refs[:n_in]])
        outs = outs if isinstance(outs, (tuple, list)) else (outs,)
        o_refs = refs[n_in + len(deps):]
        for o_ref, val in zip(o_refs[:n_row_out], outs[:n_row_out]):
            o_ref[...] = val.astype(o_ref.dtype)
        first = pl.program_id(1) == 0
        for o_ref, val in zip(o_refs[n_row_out:], outs[n_row_out:]):
            @pl.when(first)
            def _(o_ref=o_ref, val=val):
                o_ref[...] = val

            @pl.when(jnp.logical_not(first))
            def _(o_ref=o_ref, val=val):
                o_ref[...] += val

    def row_map(off):
        return lambda j, i: (i, j + off)

    def par_map(off):
        return lambda j, i: (0, j + off)

    return pl.pallas_call(
        body, name=name,
        grid=(cols // tc, s // ts),
        in_specs=[pl.BlockSpec((ts, tc), row_map(off)) for _, off in rows]
        + [pl.BlockSpec((p.shape[0], tc), par_map(off)) for p, off in pars]
        + [pl.BlockSpec(memory_space=pl.ANY)] * len(deps),
        out_specs=[pl.BlockSpec((ts, tc), lambda j, i: (i, j)) for _ in row_outs]
        + [pl.BlockSpec((r, tc), lambda j, i: (0, j)) for r in red_rows],
        out_shape=[jax.ShapeDtypeStruct((s, cols), dt) for dt in row_outs]
        + [jax.ShapeDtypeStruct((r, cols), F32) for r in red_rows],
        compiler_params=_params("parallel", "arbitrary"),
    )(*[r for r, _ in rows], *[p for p, _ in pars], *deps)


def _swiglu_act(g, u):
    return _silu(g) * u


def _gate_up(n, w, *, name):
    s, d = n.shape
    f = w.shape[1] // 2
    tn = _pick(f, (256, 128))
    nj = f // tn

    def body(a_ref, wg_ref, wu_ref, g_ref, u_ref, act_ref):
        a = a_ref[...].astype(BF16)
        g = jnp.dot(a, wg_ref[...].astype(BF16), preferred_element_type=F32)
        u = jnp.dot(a, wu_ref[...].astype(BF16), preferred_element_type=F32)
        g_ref[...] = g.astype(BF16)
        u_ref[...] = u.astype(BF16)
        act_ref[...] = _swiglu_act(g, u).astype(BF16)

    out = pl.BlockSpec((s, tn), lambda j: (0, j))
    return pl.pallas_call(
        body, name=name,
        grid=(nj,),
        in_specs=[pl.BlockSpec((s, d), lambda j: (0, 0)), pl.BlockSpec((d, tn), lambda j: (0, j)),
                  pl.BlockSpec((d, tn), lambda j: (0, nj + j))],
        out_specs=[out] * 3,
        out_shape=[jax.ShapeDtypeStruct((s, f), BF16)] * 3,
        compiler_params=_params("parallel"),
    )(n, w, w)


def _swiglu_bwd(g, u, dact, *, name, ts=128):
    s, f = g.shape
    ts = min(ts, s)

    def body(g_ref, u_ref, dact_ref, o_ref):
        _, vjp = jax.vjp(_swiglu_act, g_ref[...].astype(F32), u_ref[...].astype(F32))
        dg, du = vjp(dact_ref[...].astype(F32))
        o_ref[:, 0:f] = dg.astype(BF16)
        o_ref[:, f:2 * f] = du.astype(BF16)

    narrow = pl.BlockSpec((ts, f), lambda i: (i, 0))
    return pl.pallas_call(
        body, name=name,
        grid=(s // ts,),
        in_specs=[narrow] * 3,
        out_specs=pl.BlockSpec((ts, 2 * f), lambda i: (i, 0)),
        out_shape=jax.ShapeDtypeStruct((s, 2 * f), BF16),
        compiler_params=_params("parallel"),
    )(g, u, dact)


def _rms(x, g):
    return x * lax.rsqrt(jnp.mean(x * x, axis=-1, keepdims=True) + RMS_EPS) * g


def _sigmoid(x):
    return jax.nn.sigmoid(x)


def _silu(x):
    return x * jax.nn.sigmoid(x)


def _gelu(x):
    return 0.5 * x * (1.0 + jnp.tanh(math.sqrt(2.0 / math.pi) * (x + 0.044715 * (x * x * x))))


def _softplus(x):
    return jnp.maximum(x, 0.0) + jnp.log1p(jnp.exp(-jnp.abs(x)))


def _split(x, terms):
    parts = []
    for _ in range(terms - 1):
        parts.append(x.astype(BF16))
        x = x - parts[-1].astype(F32)
    return parts + [x.astype(BF16)]


def _xdot(x, t, terms=3):
    return sum(jnp.dot(p, t, preferred_element_type=F32) for p in _split(x, terms))


def _xdot_l(t, x):
    return sum(jnp.dot(t, p, preferred_element_type=F32) for p in _split(x, 3))


_NT = (((1,), (1,)), ((), ()))
_TN = (((0,), (0,)), ((), ()))


def _dot(a, b, dims=None):
    if dims is None:
        return jnp.dot(a.astype(BF16), b.astype(BF16), preferred_element_type=F32)
    return lax.dot_general(a.astype(BF16), b.astype(BF16), dims, preferred_element_type=F32)


def _iota(shape, axis):
    return lax.broadcasted_iota(jnp.int32, shape, axis)


def _sb_tile(qb, kblk, mask, upper, c_rem):
    z = lax.dot_general(qb, kblk, _NT, preferred_element_type=F32)
    soft = jnp.log1p(jnp.exp(-jnp.abs(z)))
    lbeta = jnp.minimum(z, 0.0) - soft
    l1m = -jnp.maximum(z, 0.0) - soft
    if mask is not None:
        l1m = jnp.where(mask, l1m, 0.0)
    rem = _xdot(l1m, upper, terms=2) + c_rem
    w = jnp.exp(lbeta + rem)
    if mask is not None:
        w = jnp.where(mask, w, 0.0)
    return lbeta, l1m, w


def _sb_tiles(s):
    tq = min(SB_Q_TILE, s)
    return tq, SB_K_TILE, tq // SB_K_TILE


def _sb_key_loops(qi, per_q, step, carry):
    n_full = qi * per_q
    carry = lax.fori_loop(0, per_q, lambda j, c: step(n_full + per_q - 1 - j, True, c), carry)
    return lax.fori_loop(0, n_full, lambda j, c: step(n_full - 1 - j, False, c), carry)


def _sb_fwd(proj, n_heads, *, name):
    s = proj.shape[0]
    t, tk, per_q = _sb_tiles(s)
    scale = HEAD_DIM ** -0.5

    def body(q_ref, k_ref, v_ref, o_ref):
        qi = pl.program_id(1)
        qb = (q_ref[...] * scale).astype(BF16)
        row, col = _iota((t, tk), 0) + qi * t, _iota((t, tk), 1)
        upper = (_iota((tk, tk), 0) > _iota((tk, tk), 1)).astype(BF16)

        def step(kb, masked, carry):
            acc, c_rem = carry
            rows = pl.ds(pl.multiple_of(kb * tk, tk), tk)
            kblk = k_ref[rows, :].astype(BF16)
            vblk = v_ref[rows, :].astype(BF16)
            _, l1m, w = _sb_tile(qb, kblk, (col + kb * tk) < row if masked else None, upper, c_rem)
            acc = acc + jnp.dot(w.astype(BF16), vblk, preferred_element_type=F32)
            return acc, c_rem + jnp.sum(l1m, axis=1, keepdims=True)

        acc, _ = _sb_key_loops(qi, per_q, step, (jnp.zeros((t, HEAD_DIM), F32), jnp.zeros((t, 1), F32)))
        o_ref[...] = acc

    return pl.pallas_call(
        body, name=name,
        grid=(n_heads, s // t),
        in_specs=[pl.BlockSpec((t, HEAD_DIM), lambda h, i: (i, h)),
                  pl.BlockSpec((s, HEAD_DIM), lambda h, i: (0, n_heads + h)),
                  pl.BlockSpec((s, HEAD_DIM), lambda h, i: (0, 2 * n_heads + h))],
        out_specs=pl.BlockSpec((t, HEAD_DIM), lambda h, i: (i, h)),
        out_shape=jax.ShapeDtypeStruct((s, n_heads * HEAD_DIM), F32),
        compiler_params=_params("parallel", "arbitrary"),
    )(proj, proj, proj)


def _sb_bwd(proj, dcat, n_heads, *, name):
    s = proj.shape[0]
    t, tk, per_q = _sb_tiles(s)
    scale = HEAD_DIM ** -0.5

    def body(q_ref, k_ref, v_ref, do_ref, dq_ref, dk_ref, dv_ref, g_s, sig_s):
        qi = pl.program_id(1)

        @pl.when(qi == 0)
        def _():
            dk_ref[...] = jnp.zeros_like(dk_ref)
            dv_ref[...] = jnp.zeros_like(dv_ref)

        qb = (q_ref[...] * scale).astype(BF16)
        dob = do_ref[...].astype(BF16)
        row, col = _iota((t, tk), 0) + qi * t, _iota((t, tk), 1)
        upper = (_iota((tk, tk), 0) > _iota((tk, tk), 1)).astype(BF16)
        lower_incl = (_iota((tk, tk), 0) >= _iota((tk, tk), 1)).astype(BF16)

        def weights(kb, masked, carry):
            c_rem, g_all = carry
            rows = pl.ds(pl.multiple_of(kb * tk, tk), tk)
            kblk = k_ref[rows, :].astype(BF16)
            vblk = v_ref[rows, :].astype(BF16)
            lbeta, l1m, w = _sb_tile(qb, kblk, (col + kb * tk) < row if masked else None, upper, c_rem)
            g = w * lax.dot_general(dob, vblk, _NT, preferred_element_type=F32)
            dv_ref[rows, :] += lax.dot_general(w.astype(BF16), dob, _TN, preferred_element_type=F32)
            g_s[kb] = g
            sig_s[kb] = jnp.exp(lbeta)
            return c_rem + jnp.sum(l1m, axis=1, keepdims=True), g_all + jnp.sum(g, axis=1, keepdims=True)

        zero_col = jnp.zeros((t, 1), F32)
        _, g_all = _sb_key_loops(qi, per_q, weights, (zero_col, zero_col))

        def scores(kb, masked, carry):
            dq, c_g = carry
            rows = pl.ds(pl.multiple_of(kb * tk, tk), tk)
            g, sig = g_s[kb], sig_s[kb]
            g_before = g_all - (_xdot(g, lower_incl) + c_g)
            dz = g * (1.0 - sig) - g_before * sig
            if masked:
                dz = jnp.where((col + kb * tk) < row, dz, 0.0)
            dz = dz.astype(BF16)
            dq = dq + jnp.dot(dz, k_ref[rows, :].astype(BF16), preferred_element_type=F32)
            dk_ref[rows, :] += lax.dot_general(dz, qb, _TN, preferred_element_type=F32)
            return dq, c_g + jnp.sum(g, axis=1, keepdims=True)

        dq, _ = _sb_key_loops(qi, per_q, scores, (jnp.zeros((t, HEAD_DIM), F32), zero_col))
        dq_ref[...] = dq * scale

    width = n_heads * HEAD_DIM
    return pl.pallas_call(
        body, name=name,
        grid=(n_heads, s // t),
        in_specs=[pl.BlockSpec((t, HEAD_DIM), lambda h, i: (i, h)),
                  pl.BlockSpec((s, HEAD_DIM), lambda h, i: (0, n_heads + h)),
                  pl.BlockSpec((s, HEAD_DIM), lambda h, i: (0, 2 * n_heads + h)),
                  pl.BlockSpec((t, HEAD_DIM), lambda h, i: (i, h))],
        out_specs=[pl.BlockSpec((t, HEAD_DIM), lambda h, i: (i, h)),
                   pl.BlockSpec((s, HEAD_DIM), lambda h, i: (0, h)),
                   pl.BlockSpec((s, HEAD_DIM), lambda h, i: (0, h))],
        out_shape=[jax.ShapeDtypeStruct((s, width), F32)] * 3,
        scratch_shapes=[pltpu.VMEM((s // tk, t, tk), F32)] * 2,
        compiler_params=_params("parallel", "arbitrary"),
    )(proj, proj, proj, dcat)


def _hg_pre(hq, hf, logits):
    mx = jnp.max(logits, axis=0, keepdims=True)
    ex = jnp.exp(logits - mx)
    lb = ex[0:1, :] / jnp.sum(ex, axis=0, keepdims=True)
    f = lb + (1.0 - lb) * _sigmoid(hf)
    return _silu(hq), 1.0 - f, jnp.log(f)


def _hg_post(o, norm_g, hgate):
    return _rms(o, norm_g) * _silu(hgate)


def _hg_specs(s, n_heads, first_block):
    def at(group):
        return pl.BlockSpec((s, HEAD_DIM), lambda h: (0, first_block + group * n_heads + h))
    return [at(0), at(1), at(2), at(3)]


def _hg_fwd(proj, logits, norm_g, n_heads, *, name):
    s = proj.shape[0]
    hc = HG_CHUNK
    n_chunks = s // hc
    d = HEAD_DIM

    hp = HG_HEADS_PER_STEP
    assert n_heads % hp == 0
    wide = hp * d

    def body(lg_ref, ng_ref, hq_ref, hf_ref, hi_ref, hgt_ref, out_ref, oraw_ref, st_ref,
             q_s, k_s, lf_s, cum_s, qc_s, oc_s):
        q, k, lf = _hg_pre(hq_ref[...], hf_ref[...], lg_ref[...])
        q_s[...] = q
        k_s[...] = k
        lf_s[...] = lf
        tril = (_iota((hc, hc), 0) >= _iota((hc, hc), 1)).astype(BF16)
        srow = _iota((hc, d), 0)

        def head_chunk(j, ci, rows, st):
            ln = slice(j * d, (j + 1) * d)
            q, k, v = q_s[rows, ln], k_s[rows, ln], hi_ref[rows, ln]
            cum = _xdot_l(tril, lf_s[rows, ln])
            st_ref[j, ci] = st
            o_inter = _dot(q * jnp.exp(cum), st, _NT)
            cum_s[:, ln] = cum
            qc_s[:, ln] = q
            for t in range(hc):
                ng = (t // SUBLANES + 1) * SUBLANES
                e = jnp.where(srow[:ng] <= t, jnp.exp(cum_s[t:t + 1, ln] - cum[:ng]), 0.0)
                sc = jnp.sum(qc_s[t:t + 1, ln] * k[:ng] * e, axis=1, keepdims=True)
                oc_s[t:t + 1, ln] = jnp.sum(sc * v[:ng], axis=0, keepdims=True)
            oraw_ref[rows, ln] = o_inter + oc_s[:, ln]
            last = cum_s[hc - 1:hc, ln]
            return st * jnp.exp(last) + _dot(v, k * jnp.exp(last - cum), _TN)

        def chunk(ci, states):
            rows = pl.ds(pl.multiple_of(ci * hc, hc), hc)
            return tuple(head_chunk(j, ci, rows, st) for j, st in enumerate(states))

        lax.fori_loop(0, n_chunks, chunk, tuple(jnp.zeros((d, d), F32) for _ in range(hp)))
        for j in range(hp):
            ln = slice(j * d, (j + 1) * d)
            out_ref[:, ln] = _hg_post(oraw_ref[:, ln], ng_ref[...], hgt_ref[:, ln]).astype(BF16)

    width = n_heads * d
    first = 3 * n_heads // hp
    groups = [pl.BlockSpec((s, wide), functools.partial(lambda h, g: (0, first + g * (n_heads // hp) + h), g=g))
              for g in range(4)]
    head_block = pl.BlockSpec((s, wide), lambda h: (0, h))
    return pl.pallas_call(
        body, name=name,
        grid=(n_heads // hp,),
        in_specs=[pl.BlockSpec((2, wide), lambda h: (0, h)), pl.BlockSpec((1, d), lambda h: (0, 0))] + groups,
        out_specs=[head_block, head_block, pl.BlockSpec((hp, n_chunks, d, d), lambda h: (h, 0, 0, 0))],
        out_shape=[jax.ShapeDtypeStruct((s, width), BF16), jax.ShapeDtypeStruct((s, width), F32),
                   jax.ShapeDtypeStruct((n_heads, n_chunks, d, d), F32)],
        scratch_shapes=[pltpu.VMEM((s, wide), F32)] * 3 + [pltpu.VMEM((hc, wide), F32)] * 3,
        compiler_params=_params("arbitrary"),
    )(logits, norm_g, proj, proj, proj, proj)


def _hg_bwd(proj, logits, norm_g, oraw, states, dcat, n_heads, *, name):
    s = proj.shape[0]
    hc = HG_CHUNK
    n_chunks = s // hc
    d = HEAD_DIM

    def body(lg_ref, ng_ref, hq_ref, hf_ref, hi_ref, hgt_ref, oraw_ref, st_ref, dout_ref,
             dhq_ref, dhf_ref, dhi_ref, dhgt_ref, dlg_ref, dng_ref,
             q_s, k_s, lf_s, do_s, dq_s, dk_s, dlf_s, cum_s, qc_s, doc_s, dqc_s, dkc_s, dvc_s):
        head = pl.program_id(0)
        (q, k, lf), pre_vjp = jax.vjp(_hg_pre, hq_ref[...], hf_ref[...], lg_ref[...])
        q_s[...] = q
        k_s[...] = k
        lf_s[...] = lf
        _, post_vjp = jax.vjp(_hg_post, oraw_ref[...], ng_ref[...], hgt_ref[...])
        do, dng, dhgt = post_vjp(dout_ref[...])
        do_s[...] = do
        dhgt_ref[...] = dhgt.astype(BF16)

        @pl.when(head == 0)
        def _():
            dng_ref[...] = dng

        @pl.when(head > 0)
        def _():
            dng_ref[...] += dng

        triu = (_iota((hc, hc), 0) <= _iota((hc, hc), 1)).astype(BF16)
        tril = (_iota((hc, hc), 0) >= _iota((hc, hc), 1)).astype(BF16)
        srow = _iota((hc, d), 0)

        def chunk(j, dst):
            ci = n_chunks - 1 - j
            rows = pl.ds(pl.multiple_of(ci * hc, hc), hc)
            q, k, v, do_c = q_s[rows, :], k_s[rows, :], hi_ref[rows, :], do_s[rows, :]
            cum = _xdot_l(tril, lf_s[rows, :])
            st = st_ref[0, ci]
            cum_s[...] = cum
            qc_s[...] = q
            doc_s[...] = do_c
            last = cum_s[hc - 1:hc, :]
            e_cum, e_last = jnp.exp(cum), jnp.exp(last - cum)
            dqc_s[...] = _dot(do_c, st) * e_cum
            dk_state = _dot(v, dst) * e_last
            dkc_s[...] = dk_state
            dvc_s[...] = _dot(k * e_last, dst, _NT)
            d_last = (jnp.sum(dst * st, axis=0, keepdims=True) * jnp.exp(last)
                      + jnp.sum(k * dk_state, axis=0, keepdims=True))
            for t in range(hc):
                ng = (t // SUBLANES + 1) * SUBLANES
                qt, dot_ = qc_s[t:t + 1, :], doc_s[t:t + 1, :]
                e = jnp.where(srow[:ng] <= t, jnp.exp(cum_s[t:t + 1, :] - cum[:ng]), 0.0)
                ke = k[:ng] * e
                d_a = jnp.sum(dot_ * v[:ng], axis=1, keepdims=True)
                dqc_s[t:t + 1, :] += jnp.sum(d_a * ke, axis=0, keepdims=True)
                dkc_s[0:ng, :] += d_a * (qt * e)
                dvc_s[0:ng, :] += jnp.sum(qt * ke, axis=1, keepdims=True) * dot_
            dq, dk = dqc_s[...], dkc_s[...]
            d_b = q * dq - k * dk
            dq_s[rows, :] = dq
            dk_s[rows, :] = dk
            dhi_ref[rows, :] = dvc_s[...].astype(BF16)
            dlf_s[rows, :] = _xdot_l(triu, d_b) + d_last
            return dst * jnp.exp(last) + _dot(do_c, q * e_cum, _TN)

        lax.fori_loop(0, n_chunks, chunk, jnp.zeros((d, d), F32))
        dhq, dhf, dlg = pre_vjp((dq_s[...], dk_s[...], dlf_s[...]))
        dhq_ref[...] = dhq.astype(BF16)
        dhf_ref[...] = dhf.astype(BF16)
        dlg_ref[...] = dlg

    width = n_heads * d
    head_block = pl.BlockSpec((s, d), lambda h: (0, h))
    return pl.pallas_call(
        body, name=name,
        grid=(n_heads,),
        in_specs=[pl.BlockSpec((2, d), lambda h: (0, h)), pl.BlockSpec((1, d), lambda h: (0, 0))]
        + _hg_specs(s, n_heads, 3 * n_heads)
        + [head_block, pl.BlockSpec((1, n_chunks, d, d), lambda h: (h, 0, 0, 0)),
           pl.BlockSpec((s, d), lambda h: (0, n_heads + h))],
        out_specs=[head_block] * 4 + [pl.BlockSpec((2, d), lambda h: (0, h)), pl.BlockSpec((1, d), lambda h: (0, 0))],
        out_shape=[jax.ShapeDtypeStruct((s, width), BF16)] * 4
        + [jax.ShapeDtypeStruct((2, width), F32), jax.ShapeDtypeStruct((1, d), F32)],
        scratch_shapes=[pltpu.VMEM((s, d), F32)] * 7 + [pltpu.VMEM((hc, d), F32)] * 6,
        compiler_params=_params("arbitrary"),
    )(logits, norm_g, proj, proj, proj, proj, oraw, states, dcat)


def _shift_down(x, n, srow):
    if n == 0:
        return x
    return jnp.where(srow >= n, pltpu.roll(x, n, 0), 0.0)


def _shift_up(x, n, srow):
    if n == 0:
        return x
    s = x.shape[0]
    return jnp.where(srow < s - n, pltpu.roll(x, s - n, 0), 0.0)


def _rg_gates_fwd(proj, conv_w, conv_b, wa, ba, wx, bx, *, name):
    s = proj.shape[0]
    nb = wa.shape[0]
    bw = RG_BLOCK

    def body(xb_ref, cw_ref, cb_ref, wa_ref, ba_ref, wx_ref, bx_ref, xc_ref, ra_ref, ix_ref):
        x = xb_ref[...]
        srow = _iota((s, bw), 0)
        cw = cw_ref[...]
        xc = cb_ref[...] + cw[0:1, :] * x
        for tap in range(1, CONV_TAPS):
            xc = xc + cw[tap:tap + 1, :] * _shift_down(x, tap, srow)
        xc_ref[...] = xc
        ra_ref[...] = _dot(xc, wa_ref[0]) + ba_ref[0]
        ix_ref[...] = _dot(xc, wx_ref[0]) + bx_ref[0]

    col = pl.BlockSpec((s, bw), lambda n: (0, n))
    vec = lambda r: pl.BlockSpec((r, bw), lambda n: (0, n))
    mat = pl.BlockSpec((1, bw, bw), lambda n: (n, 0, 0))
    bias = pl.BlockSpec((1, 1, bw), lambda n: (n, 0, 0))
    return pl.pallas_call(
        body, name=name,
        grid=(nb,),
        in_specs=[pl.BlockSpec((s, bw), lambda n: (0, nb + n)), vec(CONV_TAPS), vec(1), mat, bias, mat, bias],
        out_specs=[col] * 3,
        out_shape=[jax.ShapeDtypeStruct((s, nb * bw), F32)] * 3,
        compiler_params=_params("parallel"),
    )(proj, conv_w, conv_b, wa, ba, wx, bx)


def _rg_au(ra, ix, xc, lam, first_row):
    log_a = -RG_C * _sigmoid(ra) * _softplus(-lam)
    th = jnp.tanh(log_a)
    one_minus_a2 = -2.0 * th / (1.0 - th)
    mult = jnp.where(first_row, 1.0, jnp.sqrt(one_minus_a2))
    return jnp.exp(log_a), xc * _sigmoid(ix) * mult


def _rg_out(gate, hs):
    return _gelu(gate) * hs


def _linear_scan(a, b, a_s, b_s, in_s, reverse):
    s, c = a.shape
    within = _iota((s, c), 0) & (SUBLANES - 1)
    shift = 1
    while shift < SUBLANES:
        if reverse:
            take = within < SUBLANES - shift
            a_n, b_n = pltpu.roll(a, s - shift, 0), pltpu.roll(b, s - shift, 0)
        else:
            take = within >= shift
            a_n, b_n = pltpu.roll(a, shift, 0), pltpu.roll(b, shift, 0)
        b = jnp.where(take, a * b_n + b, b)
        a = jnp.where(take, a * a_n, a)
        shift *= 2
    a_s[...] = a
    b_s[...] = b
    n_tiles = s // SUBLANES
    edge = 0 if reverse else SUBLANES - 1

    def tile(i, h):
        rows = pl.ds(pl.multiple_of(((n_tiles - 1 - i) if reverse else i) * SUBLANES, SUBLANES), SUBLANES)
        in_s[rows, :] = jnp.broadcast_to(h, (SUBLANES, c))
        return a_s[rows, :][edge:edge + 1, :] * h + b_s[rows, :][edge:edge + 1, :]

    lax.fori_loop(0, n_tiles, tile, jnp.zeros((1, c), F32))
    return a * in_s[...] + b


def _rg_scan_fwd(proj, xc, ra, ix, lam, *, name):
    s, width = xc.shape
    tc = LANES

    def body(gate_ref, xc_ref, ra_ref, ix_ref, lam_ref, hs_ref, gact_ref, a_s, u_s, in_s):
        first_row = _iota((s, tc), 0) == 0
        a, u = _rg_au(ra_ref[...], ix_ref[...], xc_ref[...], lam_ref[...], first_row)
        hs = _linear_scan(a, u, a_s, u_s, in_s, reverse=False)
        hs_ref[...] = hs
        gact_ref[...] = _rg_out(gate_ref[...], hs).astype(BF16)

    col = pl.BlockSpec((s, tc), lambda n: (0, n))
    return pl.pallas_call(
        body, name=name,
        grid=(width // tc,),
        in_specs=[col, col, col, col, pl.BlockSpec((1, tc), lambda n: (0, n))],
        out_specs=[col, col],
        out_shape=[jax.ShapeDtypeStruct((s, width), F32), jax.ShapeDtypeStruct((s, width), BF16)],
        scratch_shapes=[pltpu.VMEM((s, tc), F32)] * 3,
        compiler_params=_params("parallel"),
    )(proj, xc, ra, ix, lam)


def _rg_scan_bwd(dgo, proj, hs, xc, ra, ix, lam, *, name):
    s, width = xc.shape
    tc = LANES

    def body(dgo_ref, gate_ref, hs_ref, xc_ref, ra_ref, ix_ref, lam_ref,
             dgate_ref, dra_ref, dix_ref, dxc_ref, dlam_ref, a_s, dh_s, g_s):
        srow = _iota((s, tc), 0)
        hs = hs_ref[...]
        _, out_vjp = jax.vjp(_rg_out, gate_ref[...], hs)
        dgate, dh = out_vjp(dgo_ref[...])
        dgate_ref[...] = dgate.astype(BF16)
        au = functools.partial(_rg_au, first_row=srow == 0)
        (a, _), au_vjp = jax.vjp(au, ra_ref[...], ix_ref[...], xc_ref[...], lam_ref[...])
        g = _linear_scan(_shift_up(a, 1, srow), dh, a_s, dh_s, g_s, reverse=True)
        dra, dix, dxc, dlam = au_vjp((g * _shift_down(hs, 1, srow), g))
        dra_ref[...] = dra.astype(BF16)
        dix_ref[...] = dix.astype(BF16)
        dxc_ref[...] = dxc
        dlam_ref[...] = dlam

    col = pl.BlockSpec((s, tc), lambda n: (0, n))
    vec = pl.BlockSpec((1, tc), lambda n: (0, n))
    return pl.pallas_call(
        body, name=name,
        grid=(width // tc,),
        in_specs=[col] * 6 + [vec],
        out_specs=[col] * 4 + [vec],
        out_shape=[jax.ShapeDtypeStruct((s, width), BF16)] * 3
        + [jax.ShapeDtypeStruct((s, width), F32), jax.ShapeDtypeStruct((1, width), F32)],
        scratch_shapes=[pltpu.VMEM((s, tc), F32)] * 3,
        compiler_params=_params("parallel"),
    )(dgo, proj, hs, xc, ra, ix, lam)


def _rg_gates_bwd(dra, dix, dxc1, xc, proj, conv_w, wa, wx, *, name):
    s = proj.shape[0]
    nb = wa.shape[0]
    bw = RG_BLOCK

    def body(dra_ref, dix_ref, dxc_ref, xc_ref, xb_ref, cw_ref, wa_ref, wx_ref,
             dxb_ref, dcw_ref, dcb_ref, dwa_ref, dba_ref, dwx_ref, dbx_ref):
        dra, dix = dra_ref[...], dix_ref[...]
        xc_t = xc_ref[...].T.astype(BF16)
        dwa_ref[0] = jnp.dot(xc_t, dra, preferred_element_type=F32)
        dwx_ref[0] = jnp.dot(xc_t, dix, preferred_element_type=F32)
        dba_ref[0] = jnp.sum(dra.astype(F32), axis=0, keepdims=True)
        dbx_ref[0] = jnp.sum(dix.astype(F32), axis=0, keepdims=True)
        dxc = dxc_ref[...] + _dot(dra, wa_ref[0], _NT) + _dot(dix, wx_ref[0], _NT)
        srow = _iota((s, bw), 0)
        x = xb_ref[...]
        cw = cw_ref[...]
        dx = cw[0:1, :] * dxc
        dcw = [jnp.sum(dxc * x, axis=0, keepdims=True)]
        for tap in range(1, CONV_TAPS):
            dx = dx + cw[tap:tap + 1, :] * _shift_up(dxc, tap, srow)
            dcw.append(jnp.sum(dxc * _shift_down(x, tap, srow), axis=0, keepdims=True))
        dxb_ref[...] = dx.astype(BF16)
        r4 = _iota((CONV_TAPS, bw), 0)
        acc = jnp.zeros((CONV_TAPS, bw), F32)
        for tap in range(CONV_TAPS):
            acc = jnp.where(r4 == tap, dcw[tap], acc)
        dcw_ref[...] = acc
        dcb_ref[...] = jnp.sum(dxc, axis=0, keepdims=True)

    col = pl.BlockSpec((s, bw), lambda n: (0, n))
    vec = lambda r: pl.BlockSpec((r, bw), lambda n: (0, n))
    mat = pl.BlockSpec((1, bw, bw), lambda n: (n, 0, 0))
    bias = pl.BlockSpec((1, 1, bw), lambda n: (n, 0, 0))
    width = nb * bw
    return pl.pallas_call(
        body, name=name,
        grid=(nb,),
        in_specs=[col, col, col, col, pl.BlockSpec((s, bw), lambda n: (0, nb + n)), vec(CONV_TAPS), mat, mat],
        out_specs=[col, vec(CONV_TAPS), vec(1), mat, bias, mat, bias],
        out_shape=[jax.ShapeDtypeStruct((s, width), BF16), jax.ShapeDtypeStruct((CONV_TAPS, width), F32),
                   jax.ShapeDtypeStruct((1, width), F32), jax.ShapeDtypeStruct((nb, bw, bw), F32),
                   jax.ShapeDtypeStruct((nb, 1, bw), F32), jax.ShapeDtypeStruct((nb, bw, bw), F32),
                   jax.ShapeDtypeStruct((nb, 1, bw), F32)],
        compiler_params=_params("parallel"),
    )(dra, dix, dxc1, xc, proj, conv_w, wa, wx)


_HBM = pl.BlockSpec(memory_space=pltpu.HBM)
_FLIPS = ((0, 0, 1), (1, 0, 0), (0, 1, 0), (1, 1, 0))
_ALL_FLIPS = tuple((a, b, c) for a in (0, 1) for b in (0, 1) for c in (0, 1))[1:]


def _flip(pos, f):
    return tuple(1 - p if b else p for p, b in zip(pos, f))


def _dev_index(pos):
    return 4 * pos[0] + 2 * pos[1] + pos[2]


def _block(ref, idx, cols):
    if not cols:
        return ref.at[idx]
    n = ref.shape[-1] // N_DEV
    start = pl.multiple_of(idx * n, LANES)
    return ref.at[(slice(None),) * (len(ref.shape) - 1) + (pl.ds(start, n),)]


_SEM = pl.BlockSpec(memory_space=pltpu.SEMAPHORE)
_ANY = pl.BlockSpec(memory_space=pl.ANY)
_N_PEERS = N_DEV - 1


def _hbm(x):
    return pltpu.with_memory_space_constraint(x, pltpu.HBM)


def _me():
    return lax.axis_index("x"), lax.axis_index("y"), lax.axis_index("c")


def _spread_copies(plan, src_refs, land_refs, send_sems, recv_sems, local_sems):
    local, remote = plan(src_refs, land_refs)
    local = [pltpu.make_async_copy(src, dst, local_sems.at[i]) for i, (src, dst) in enumerate(local)]
    remote = [pltpu.make_async_remote_copy(src_ref=src, dst_ref=dst, send_sem=send_sems.at[k], recv_sem=recv_sems.at[k],
                                           device_id=peer, device_id_type=pl.DeviceIdType.MESH)
              for k, (src, dst, peer) in enumerate(remote)]
    return local, remote


def _spread_start(srcs, lands, plan, n_remote, n_local, *, name, after=()):
    ns, nl = len(srcs), len(lands)
    n_in = ns + nl + len(after)

    def body(*refs):
        src_refs, land_refs = refs[:ns], refs[ns:ns + nl]
        send_sems, recv_sems, local_sems = refs[n_in:n_in + 3]
        local, remote = _spread_copies(plan, src_refs, land_refs, send_sems, recv_sems, local_sems)
        for cp in local + remote:
            cp.start()
        token = refs[-1]
        token[...] = jnp.zeros_like(token)

    lands = [_hbm(lax.empty(*x)) if isinstance(x, tuple) else x for x in lands]
    out = pl.pallas_call(
        body, name=name,
        in_specs=[_HBM] * (ns + nl) + [_ANY] * len(after),
        out_specs=[_SEM] * 3 + [_HBM] * (ns + nl) + [pl.BlockSpec(memory_space=pltpu.VMEM)],
        out_shape=[pltpu.SemaphoreType.DMA((n_remote,)), pltpu.SemaphoreType.DMA((n_remote,)),
                   pltpu.SemaphoreType.DMA((max(n_local, 1),))]
        + [pltpu.HBM(x.shape, x.dtype) for x in list(srcs) + lands]
        + [jax.ShapeDtypeStruct((SUBLANES, LANES), F32)],
        input_output_aliases={i: 3 + i for i in range(ns + nl)},
        compiler_params=pltpu.CompilerParams(has_side_effects=pltpu.SideEffectType.DATAFLOW_SIDE_EFFECTING),
    )(*[_hbm(x) for x in srcs], *lands, *after)
    return dict(sems=list(out[:3]), srcs=list(out[3:3 + ns]), lands=list(out[3 + ns:3 + ns + nl]),
                token=out[-1], plan=plan)


def _spread_wait(handle, after, *, name):
    ns, nl = len(handle["srcs"]), len(handle["lands"])

    def body(*refs):
        src_refs, land_refs = refs[:ns], refs[ns:ns + nl]
        send_sems, recv_sems, local_sems = refs[ns + nl:ns + nl + 3]
        local, remote = _spread_copies(handle["plan"], src_refs, land_refs, send_sems, recv_sems, local_sems)
        for cp in local:
            cp.wait()
        for cp in remote:
            cp.wait_send()
            cp.wait_recv()

    out = pl.pallas_call(
        body, name=name,
        in_specs=[_HBM] * (ns + nl) + [_SEM] * 3 + [_ANY],
        out_specs=[_HBM] * (ns + nl),
        out_shape=[pltpu.HBM(x.shape, x.dtype) for x in handle["srcs"] + handle["lands"]],
        input_output_aliases={i: i for i in range(ns + nl)},
        compiler_params=pltpu.CompilerParams(has_side_effects=pltpu.SideEffectType.DATAFLOW_SIDE_EFFECTING),
    )(*handle["srcs"], *handle["lands"], *handle["sems"], after)
    return list(out[ns:])


def _gather_start(x, *, name, cols=False, relayed=True, after=()):
    shape = x.shape[:-1] + (N_DEV * x.shape[-1],) if cols else (N_DEV,) + x.shape
    flips = _FLIPS if relayed else _ALL_FLIPS

    def plan(src_refs, land_refs):
        me = _me()
        mine = _block(land_refs[0], _dev_index(me), cols)
        return [(src_refs[0], mine)], [(src_refs[0], mine, _flip(me, f)) for f in flips]

    handle = _spread_start([x], [(shape, x.dtype)], plan, len(flips), 1, name=name, after=after)
    handle["cols"] = cols
    return handle


def _gather_relay(handle, after, *, name):
    cols = handle["cols"]
    land, = _spread_wait(handle, after, name=f"{name}_arrived")

    def plan(src_refs, land_refs):
        me = _me()
        blocks = [_block(land_refs[0], _dev_index(_flip(me, f)), cols) for f in _FLIPS[1:]]
        return [], [(blk, blk, _flip(me, _FLIPS[0])) for blk in blocks]

    return _spread_start([], [land], plan, len(_FLIPS) - 1, 0, name=f"{name}_pass")


def _exchange_start(ps, *, name, cols=False, after=()):
    blk = ps[0].shape[:-1] + (ps[0].shape[-1] // N_DEV,) if cols else ps[0].shape[1:]

    def plan(src_refs, land_refs):
        me = _me()
        me_idx = _dev_index(me)
        local = [(_block(src, me_idx, cols), land_refs[0].at[me_idx, a]) for a, src in enumerate(src_refs)]
        remote = [(_block(src, _dev_index(_flip(me, f)), cols), land_refs[0].at[me_idx, a], _flip(me, f))
                  for f in _ALL_FLIPS for a, src in enumerate(src_refs)]
        return local, remote

    return _spread_start(ps, [((N_DEV, len(ps)) + blk, ps[0].dtype)], plan, _N_PEERS * len(ps), len(ps), name=name,
                         after=after)


def _adamw(parts, w, m, v, *, name, layer=0, prev=None):
    n_rows, c = w.shape
    r = parts.shape[1]
    row_bytes = c * (N_DEV * parts.dtype.itemsize + 7 * 4) * 2
    tr = r
    for cand in (512, 256, 128, 64, 32, 16):
        if r % cand == 0 and cand * row_bytes <= ADAMW_BLOCK_BYTES:
            tr = cand
            break
    c1 = 1.0 - ADAM_B1 ** ADAM_STEP
    c2 = 1.0 - ADAM_B2 ** ADAM_STEP

    def body(p_ref, w_ref, m_ref, v_ref, *rest):
        g_ref, d_ref, nm_ref, nv_ref = rest[-4:]
        g = p_ref[0].astype(F32)
        for j in range(1, N_DEV):
            g = g + p_ref[j].astype(F32)
        nm = ADAM_B1 * m_ref[...] + (1.0 - ADAM_B1) * g
        nv = ADAM_B2 * v_ref[...] + (1.0 - ADAM_B2) * (g * g)
        g_ref[...] = g
        nm_ref[...] = nm
        nv_ref[...] = nv
        d_ref[...] = -ADAM_LR * ((nm * (1.0 / c1)) / (jnp.sqrt(nv * (1.0 / c2)) + ADAM_EPS) + ADAM_WD * w_ref[...])

    off = layer * (r // tr)
    blk = pl.BlockSpec((tr, c), lambda i: (i + off, 0))
    prev = list(prev) if prev is not None else []
    return pl.pallas_call(
        body, name=name,
        grid=(r // tr,),
        in_specs=[pl.BlockSpec((N_DEV, tr, c), lambda i: (0, i, 0)), blk, blk, blk] + [_ANY] * len(prev),
        out_specs=[blk] * 4,
        out_shape=[jax.ShapeDtypeStruct((n_rows, c), F32)] * 4,
        input_output_aliases={4 + j: j for j in range(len(prev))},
        compiler_params=_params("parallel"),
    )(parts, w, m, v, *prev)


_TN_CANDS = (512, 256, 128)
_TK_MAX = 5632
_TK_WHOLE_ROWS = 2816


def _contraction_tiles(m, k):
    tk = k
    while tk > _TK_MAX and tk % 2 == 0 and (tk // 2) % LANES == 0:
        tk //= 2
    tm = m if tk <= _TK_WHOLE_ROWS or m % 2 else m // 2
    return tm, tk


def _nn(a, b, name, out_dtype=F32):
    tm, tk = _contraction_tiles(*a.shape)
    return _mm(a, b, "nn", name=name, out_dtype=out_dtype, tm=tm, tn=_pick(b.shape[1], _TN_CANDS), tk=tk)


def _nt(a, b, name, out_dtype=F32, deps=()):
    tm, tk = _contraction_tiles(*a.shape)
    return _mm(a, b, "nt", name=name, out_dtype=out_dtype, tm=tm, tn=_pick(b.shape[0], _TN_CANDS), tk=tk,
               deps=deps)


def _tn(a, b, name, out_dtype=BF16, deps=()):
    assert a.shape[0] == b.shape[0], (a.shape, b.shape)
    return _mm_tn(a, b, name=name, out_dtype=out_dtype, tm=_pick(a.shape[1], _TN_CANDS),
                  tn=_pick(b.shape[1], (1024,) + _TN_CANDS), deps=deps)


def _local_step(x, p, target, rep, weight, emit, n_heads, start_tokens=()):
    s, d = x.shape
    depth = p.shape[0]
    grads = {}
    rep_grads = {k: [None] * depth for k in ("mix_pre_g", "mix_post_g", "ffn_pre_g", "ffn_post_g", "ple_norm_g")}

    pending = list(start_tokens)
    gains = {}

    def gain(name, i):
        if (name, i) not in gains:
            gains[name, i] = rep[name][i:i + 1]
        return gains[name, i]

    def send(name, layer, g):
        token = emit(name, layer, g)
        if token is not None:
            pending.append(token)

    def rowcall(*args, **kwargs):
        deps, pending[:] = tuple(pending), []
        return _rowcall(*args, deps=deps, **kwargs)

    def grad_in(*args, **kwargs):
        deps, pending[:] = tuple(pending), []
        return _nt(*args, deps=deps, **kwargs)

    deferred = []

    def send_small(i, name, layer, a, b, mm_name):
        if i == 0 and depth > 1:
            deferred.append((name, layer, a, b, mm_name))
        else:
            send(name, layer, _tn(a, b, mm_name))

    saved = []
    h = x
    for i in range(depth):
        sv = {"h": h}
        n1, = rowcall(f"pre_norm{i}", lambda hh, g: _rms(hh, g), [h], [gain("mix_pre_g", i)], [BF16], cols=d)
        sv["n1"] = n1
        if i % 2 == 0:
            proj = _nn(n1, weight("w_in_even", 0, n1), f"in_even{i}")
            a_out = _sb_fwd(proj, n_heads, name=f"sb_fwd{i}")
            b_out, oraw, states = _hg_fwd(proj, rep["hg_lb_logits"], rep["hg_norm_g"], n_heads, name=f"hg_fwd{i}")
            cat = jnp.concatenate([a_out.astype(BF16), b_out], axis=1)
            m = _nn(cat, weight("w_out_even", 0, cat), f"out_even{i}")
            sv.update(proj=proj, oraw=oraw, states=states, cat=cat)
        else:
            proj = _nn(n1, weight("w_in_odd", 0, n1), f"in_odd{i}")
            sm = {k: weight(k, 0, proj) for k in _SMALL}
            xc, ra, ix = _rg_gates_fwd(proj, sm["conv_w"], sm["conv_b"], sm["rg_wa"], sm["rg_ba"],
                                       sm["rg_wx"], sm["rg_bx"], name=f"rg_gates_fwd{i}")
            hs, gact = _rg_scan_fwd(proj, xc, ra, ix, sm["rg_lambda"], name=f"rg_scan_fwd{i}")
            m = _nn(gact, weight("w_out_odd", 0, gact), f"out_odd{i}")
            sv.update(proj=proj, xc=xc, ra=ra, ix=ix, hs=hs, gact=gact, sm=sm)

        def post_mix(hh, mm, g_post, g_pre):
            h1 = hh + _rms(mm, g_post)
            return h1, _rms(h1, g_pre)

        h1, n2 = rowcall(f"post_mix{i}", post_mix, [h, m], [gain("mix_post_g", i), gain("ffn_pre_g", i)],
                          [F32, BF16], cols=d)
        gate, up, act = _gate_up(n2, weight("w_gate_up", i, n2), name=f"gate_up{i}")
        f = _nn(act, weight("w_down", i, act), f"down{i}")

        def post_ffn(hh, ff_out, g_post):
            h2 = hh + _rms(ff_out, g_post)
            return h2, h2

        h2, h2b = rowcall(f"post_ffn{i}", post_ffn, [h1, f], [gain("ffn_post_g", i)], [F32, BF16], cols=d)
        e = _nn(p[i], weight("w_ple_up", i, h2b), f"ple_up{i}")
        gl = _nn(h2b, weight("w_ple_gate", i, h2b), f"ple_gate{i}")
        h3, = rowcall(f"ple{i}", lambda hh, a, b, g: hh + _rms(_sigmoid(a) * b, g), [h2, gl, e],
                       [gain("ple_norm_g", i)], [F32], cols=d)
        sv.update(m=m, h1=h1, n2=n2, gate=gate, up=up, act=act, f=f, h2b=h2b, e=e, gl=gl)
        saved.append(sv)
        h = h3

    def loss_fn(y, t):
        err = y - t
        return err * (1.0 / d), jnp.sum(err * err, axis=0, keepdims=True) * (0.5 / d)

    dh, loss_cols = rowcall("loss", loss_fn, [h, target], [], [F32], red_rows=(1,), cols=d)

    for i in reversed(range(depth)):
        sv = saved[i]

        def ple_bwd(dy, a, b, g):
            _, vjp = jax.vjp(lambda a_, b_, g_: _rms(_sigmoid(a_) * b_, g_), a, b, g)
            return vjp(dy)

        dgl, de, rep_grads["ple_norm_g"][i] = rowcall(
            f"ple_bwd{i}", ple_bwd, [dh, sv["gl"], sv["e"]], [gain("ple_norm_g", i)], [BF16, BF16],
            red_rows=(1,), cols=d)
        send_small(i, "w_ple_up", i, p[i], de, f"d_ple_up{i}")
        send_small(i, "w_ple_gate", i, sv["h2b"], dgl, f"d_ple_gate{i}")
        dh2_ple = grad_in(dgl, weight("w_ple_gate", i, dgl), f"dx_ple_gate{i}")

        def post_ffn_bwd(dy, dx, ff_out, g):
            dh2 = dy + dx
            _, vjp = jax.vjp(_rms, ff_out, g)
            df, dg = vjp(dh2)
            return dh2, df, dg

        dh2, df, rep_grads["ffn_post_g"][i] = rowcall(
            f"post_ffn_bwd{i}", post_ffn_bwd, [dh, dh2_ple, sv["f"]], [gain("ffn_post_g", i)], [F32, BF16],
            red_rows=(1,), cols=d)
        send("w_down", i, _tn(sv["act"], df, f"d_down{i}"))
        dact = grad_in(df, weight("w_down", i, df), f"dx_down{i}", out_dtype=BF16)
        dgu = _swiglu_bwd(sv["gate"], sv["up"], dact, name=f"swiglu_bwd{i}")
        send("w_gate_up", i, _tn(sv["n2"], dgu, f"d_gate_up{i}"))
        dn2 = grad_in(dgu, weight("w_gate_up", i, dgu), f"dx_gate_up{i}")

        def post_mix_bwd(dy, dn, h1, mm, g_post, g_pre):
            _, vjp_pre = jax.vjp(_rms, h1, g_pre)
            dh1_n, dg_pre = vjp_pre(dn)
            dh1 = dy + dh1_n
            _, vjp_post = jax.vjp(_rms, mm, g_post)
            dm, dg_post = vjp_post(dh1)
            return dh1, dm, dg_pre, dg_post

        dh1, dm, rep_grads["ffn_pre_g"][i], rep_grads["mix_post_g"][i] = rowcall(
            f"post_mix_bwd{i}", post_mix_bwd, [dh2, dn2, sv["h1"], sv["m"]],
            [gain("mix_post_g", i), gain("ffn_pre_g", i)], [F32, BF16], red_rows=(1, 1), cols=d)

        if i % 2 == 0:
            send_small(i, "w_out_even", 0, sv["cat"], dm, f"d_out_even{i}")
            dcat = grad_in(dm, weight("w_out_even", 0, dm), f"dx_out_even{i}")
            dq, dk, dv = _sb_bwd(sv["proj"], dcat, n_heads, name=f"sb_bwd{i}")
            dhq, dhf, dhi, dhg, grads["hg_lb_logits"], grads["hg_norm_g"] = _hg_bwd(
                sv["proj"], rep["hg_lb_logits"], rep["hg_norm_g"], sv["oraw"], sv["states"], dcat, n_heads,
                name=f"hg_bwd{i}")
            dproj = jnp.concatenate([dq.astype(BF16), dk.astype(BF16), dv.astype(BF16), dhq, dhf, dhi, dhg], axis=1)
            send("w_in_even", 0, _tn(sv["n1"], dproj, f"d_in_even{i}"))
            dn1 = grad_in(dproj, weight("w_in_even", 0, dproj), f"dx_in_even{i}")
        else:
            sm = sv["sm"]
            send_small(i, "w_out_odd", 0, sv["gact"], dm, f"d_out_odd{i}")
            dgo = grad_in(dm, weight("w_out_odd", 0, dm), f"dx_out_odd{i}")
            dgate, dra, dix, dxc1, grads["rg_lambda"] = _rg_scan_bwd(
                dgo, sv["proj"], sv["hs"], sv["xc"], sv["ra"], sv["ix"], sm["rg_lambda"], name=f"rg_scan_bwd{i}")
            (dxb, grads["conv_w"], grads["conv_b"], grads["rg_wa"], grads["rg_ba"], grads["rg_wx"],
             grads["rg_bx"]) = _rg_gates_bwd(dra, dix, dxc1, sv["xc"], sv["proj"], sm["conv_w"], sm["rg_wa"],
                                            sm["rg_wx"], name=f"rg_gates_bwd{i}")
            send("small", 0, {k: grads.pop(k) for k in _SMALL})
            dproj = jnp.concatenate([dgate, dxb], axis=1)
            send("w_in_odd", 0, _tn(sv["n1"], dproj, f"d_in_odd{i}"))
            dn1 = grad_in(dproj, weight("w_in_odd", 0, dproj), f"dx_in_odd{i}")

        def pre_norm_bwd(dy, dn, hh, g):
            _, vjp = jax.vjp(_rms, hh, g)
            dx, dg = vjp(dn)
            return dy + dx, dg

        dh, rep_grads["mix_pre_g"][i] = rowcall(
            f"pre_norm_bwd{i}", pre_norm_bwd, [dh1, dn1, sv["h"]], [gain("mix_pre_g", i)], [F32],
            red_rows=(1,), cols=d)

    for name, layer, a, b, mm_name in deferred:
        send(name, layer, _tn(a, b, mm_name, deps=(dh,)))
    for k, rows in rep_grads.items():
        grads[k] = jnp.concatenate(rows, axis=0)
    return loss_cols, dh, grads


_WEIGHTS = ("mix_pre_g", "mix_post_g", "ffn_pre_g", "ffn_post_g", "ple_norm_g", "w_in_even", "w_out_even",
            "hg_lb_logits", "hg_norm_g", "w_in_odd", "conv_w", "conv_b", "rg_wa", "rg_ba", "rg_wx", "rg_bx",
            "rg_lambda", "w_out_odd", "w_gate_up", "w_down", "w_ple_up", "w_ple_gate")
_REPLICATED = ("mix_pre_g", "mix_post_g", "ffn_pre_g", "ffn_post_g", "ple_norm_g", "hg_lb_logits", "hg_norm_g")
_SMALL = ("conv_w", "conv_b", "rg_wa", "rg_ba", "rg_wx", "rg_bx", "rg_lambda")
_BIG = {"w_in_even": True, "w_out_even": False, "w_in_odd": True, "w_out_odd": False,
        "w_gate_up": True, "w_down": False, "w_ple_up": True, "w_ple_gate": False}
_PACK_ROW = SUBLANES * LANES


def _pack(arrays):
    flat = jnp.concatenate([a.reshape(-1) for a in arrays])
    pad = -flat.shape[0] % _PACK_ROW
    return jnp.pad(flat, (0, pad)).reshape(-1, LANES)


def _pack_blocks(arrays):
    flat = jnp.concatenate([a.reshape(N_DEV, -1) for a in arrays], axis=1)
    pad = -flat.shape[1] % _PACK_ROW
    return jnp.pad(flat, ((0, 0), (0, pad))).reshape(N_DEV, -1, LANES)


def _unpack(packed, shapes, lead=()):
    flat = packed.reshape(lead + (-1,))
    out, pos = [], 0
    for shape in shapes:
        n = math.prod(shape)
        out.append(flat[..., pos:pos + n].reshape(lead + tuple(shape)))
        pos += n
    return out


def _to_full_small(name, blocks):
    if name == "conv_w":
        return jnp.transpose(blocks, (1, 0, 2)).reshape(blocks.shape[1], -1)
    if name in ("conv_b", "rg_lambda"):
        return blocks.reshape(1, -1)
    nb = blocks.shape[1]
    if name in ("rg_wa", "rg_wx"):
        return jnp.transpose(blocks, (1, 0, 2, 3)).reshape(nb, RG_BLOCK, RG_BLOCK)
    return jnp.transpose(blocks, (1, 0, 2)).reshape(nb, 1, RG_BLOCK)


def _to_blocks_small(name, full):
    if name == "conv_w":
        return jnp.transpose(full.reshape(full.shape[0], N_DEV, -1), (1, 0, 2))
    if name in ("conv_b", "rg_lambda"):
        return full.reshape(N_DEV, -1)
    nb = full.shape[0]
    if name in ("rg_wa", "rg_wx"):
        return jnp.transpose(full.reshape(nb, N_DEV, RG_BLOCK // N_DEV, RG_BLOCK), (1, 0, 2, 3))
    return jnp.transpose(full.reshape(nb, N_DEV, RG_BLOCK // N_DEV), (1, 0, 2))


def _step(inp):
    w = {k: inp[k] for k in _WEIGHTS}
    x, p, target = inp["x"][0], inp["p"][:, 0], inp["loss_target"][0]
    assert w["hg_lb_logits"].shape[0] == 2 and w["w_in_even"].shape[0] == 1 and w["w_in_odd"].shape[0] == 1

    n_heads = w["w_in_even"].shape[2] * N_DEV // (7 * HEAD_DIM)
    small_shapes = [w[k].shape[1:] for k in _SMALL]

    def lands_in_place(name):
        return _BIG[name] and w[name].shape[2] % LANES == 0

    depth = p.shape[0]
    order = [("w_in_even", 0), ("w_out_even", 0)] if depth else []
    for i in range(depth):
        if i == 1:
            order += [("w_in_odd", 0), ("small", 0), ("w_out_odd", 0)]
        order += [("w_gate_up", i), ("w_down", i), ("w_ple_up", i), ("w_ple_gate", i)]
    heavy = ("w_gate_up", "w_down", "w_in_odd", "w_out_odd")
    gathers = {}
    first_started = 0.0
    last_token = ()
    for name, l in sorted(order, key=lambda key: key[0] in heavy):
        if name == "small":
            gathers[name, l] = _gather_start(_pack([w[k][0] for k in _SMALL]) + first_started, name="gather_small",
                                             relayed=False, after=last_token)
        else:
            gathers[name, l] = _gather_start((w[name][l] + first_started).astype(BF16), name=f"gather_{name}{l}",
                                             cols=lands_in_place(name), after=last_token)
        last_token = (gathers[name, l]["token"],)
        if len(gathers) == 1:
            first_started = last_token[0][0, 0]
    ready = {}

    def relay(key, after):
        if "cols" in gathers[key] and key[0] != "small":
            gathers[key] = _gather_relay(gathers[key], after, name=f"gather_{key[0]}{key[1]}")

    def weight(name, layer, after):
        key = ("small", 0) if name in _SMALL else (name, layer)
        if key not in ready:
            relay(key, after)
            at = order.index(key)
            for nxt in order[at + 1:at + 2] if at else []:
                relay(nxt, after)
            land, = _spread_wait(gathers[key], after, name=f"gathered_{key[0]}{key[1]}")
            if name in _SMALL:
                ready[key] = {k: _to_full_small(k, b)
                              for k, b in zip(_SMALL, _unpack(land, small_shapes, lead=(N_DEV,)))}
            elif lands_in_place(name):
                ready[key] = land
            elif _BIG[name]:
                ready[key] = jnp.transpose(land, (1, 0, 2)).reshape(land.shape[1], -1)
            else:
                ready[key] = land.reshape(-1, land.shape[2])
        return ready[key][name] if name in _SMALL else ready[key]

    exchanges = []

    def emit(name, layer, g):
        after = tuple(h["token"] for _, _, h in exchanges[-1:])
        if name == "small":
            handle = _exchange_start([_pack_blocks([_to_blocks_small(k, g[k]) for k in _SMALL])],
                                     name="exchange_small", after=after)
        elif lands_in_place(name):
            handle = _exchange_start([g], name=f"exchange_{name}{layer}", cols=True, after=after)
        elif _BIG[name]:
            c = w[name].shape[2]
            handle = _exchange_start([jnp.transpose(g.reshape(-1, N_DEV, c), (1, 0, 2))],
                                     name=f"exchange_{name}{layer}", after=after)
        else:
            handle = _exchange_start([g.reshape((N_DEV,) + w[name].shape[1:])], name=f"exchange_{name}{layer}",
                                     after=after)
        exchanges.append((name, layer, handle))
        return handle["token"]

    rep = {k: w[k] for k in _REPLICATED}
    loss_cols, dx, grads = _local_step(x, p, target, rep, weight, emit, n_heads,
                                       [h["token"] for h in gathers.values()])

    loss_part = jnp.sum(loss_cols).reshape(1)
    rep_gather = _gather_start(_pack([grads[k] for k in _REPLICATED] + [loss_part]), name="gather_rep_grads",
                               relayed=False)

    out = {}
    after = exchanges[-1][2]["token"]
    for name, layer, handle in exchanges:
        land, = _spread_wait(handle, after, name=f"exchanged_{name}{layer}")
        if name == "small":
            res = _adamw(land.reshape(N_DEV, -1, LANES), *[_pack([inp[pre + k][0] for k in _SMALL]) for pre in ("", "m_", "v_")],
                         name="adamw_small")
            for k, *vals in zip(_SMALL, *[_unpack(a, small_shapes) for a in res]):
                out[k] = [v[None] for v in vals]
        else:
            n_l, r, c = w[name].shape
            res = out[name] = _adamw(land.reshape(N_DEV, r, c),
                                     *[inp[pre + name].reshape(n_l * r, c) for pre in ("", "m_", "v_")],
                                     name=f"adamw_{name}{layer}", layer=layer, prev=out.get(name))
        after = res[0]
    for name in _BIG:
        out[name] = [a.reshape(w[name].shape) for a in out[name]]

    rep_shapes = [w[k].shape for k in _REPLICATED] + [(1,)]
    rep_parts, = _spread_wait(rep_gather, after, name="gathered_rep_grads")
    res = _adamw(rep_parts, *[_pack([inp[pre + k] for k in _REPLICATED] + [jnp.zeros((1,), F32)])
                              for pre in ("", "m_", "v_")], name="adamw_rep")
    for k, *vals in zip(_REPLICATED + ("loss",), *[_unpack(a, rep_shapes) for a in res]):
        out[k] = vals
    loss = out["loss"][0][0]

    return (loss, dx[None]) + tuple(out[k][j] for j in range(4) for k in _WEIGHTS)


def kernel(x, p, mix_pre_g, mix_post_g, ffn_pre_g, ffn_post_g, ple_norm_g, w_in_even, w_out_even, hg_lb_logits, hg_norm_g, w_in_odd, conv_w, conv_b, rg_wa, rg_ba, rg_wx, rg_bx, rg_lambda, w_out_odd, w_gate_up, w_down, w_ple_up, w_ple_gate, loss_target, m_mix_pre_g, m_mix_post_g, m_ffn_pre_g, m_ffn_post_g, m_ple_norm_g, m_w_in_even, m_w_out_even, m_hg_lb_logits, m_hg_norm_g, m_w_in_odd, m_conv_w, m_conv_b, m_rg_wa, m_rg_ba, m_rg_wx, m_rg_bx, m_rg_lambda, m_w_out_odd, m_w_gate_up, m_w_down, m_w_ple_up, m_w_ple_gate, v_mix_pre_g, v_mix_post_g, v_ffn_pre_g, v_ffn_post_g, v_ple_norm_g, v_w_in_even, v_w_out_even, v_hg_lb_logits, v_hg_norm_g, v_w_in_odd, v_conv_w, v_conv_b, v_rg_wa, v_rg_ba, v_rg_wx, v_rg_bx, v_rg_lambda, v_w_out_odd, v_w_gate_up, v_w_down, v_w_ple_up, v_w_ple_gate):
    return _step(dict(locals()))
```

```python
import functools
import math

import jax
import jax.numpy as jnp
from jax import lax
from jax.experimental import pallas as pl
from jax.experimental.pallas import tpu as pltpu

F32 = jnp.float32
BF16 = jnp.bfloat16

VMEM_LIMIT_BYTES = 56 * 1024 * 1024
ADAMW_BLOCK_BYTES = 40 * 1024 * 1024
LANES = 128
SUBLANES = 8

N_DEV = 8
HEAD_DIM = 128
SB_Q_TILE = 512
SB_K_TILE = 128
HG_CHUNK = 32
HG_HEADS_PER_STEP = 2
RG_BLOCK = 256
CONV_TAPS = 4
RG_C = 8.0
RMS_EPS = 1e-6

ADAM_LR = 0.001
ADAM_B1 = 0.9
ADAM_B2 = 0.999
ADAM_EPS = 1e-08
ADAM_WD = 0.01
ADAM_STEP = 10


def _params(*sem):
    return pltpu.CompilerParams(dimension_semantics=sem, vmem_limit_bytes=VMEM_LIMIT_BYTES)


def _pick(n, cands):
    for c in cands:
        if c <= n and n % c == 0:
            return c
    return n


def _mm(a, b, mode, *, name, out_dtype=F32, tm=512, tn=512, tk=None, deps=()):
    if mode == "nn":
        (m, k), (k2, n) = a.shape, b.shape
    else:
        (m, k), (n, k2) = a.shape, b.shape
    assert k == k2, (a.shape, b.shape, mode)
    tm, tn = min(tm, m), min(tn, n)
    tk = k if tk is None else min(tk, k)
    assert m % tm == 0 and n % tn == 0 and k % tk == 0, (m, n, k, tm, tn, tk)
    nk = k // tk

    a_spec = pl.BlockSpec((tm, tk), lambda i, j, kk: (i, kk))
    if mode == "nn":
        b_spec = pl.BlockSpec((tk, tn), lambda i, j, kk: (kk, j))
        dims = (((1,), (0,)), ((), ()))
    else:
        b_spec = pl.BlockSpec((tn, tk), lambda i, j, kk: (j, kk))
        dims = (((1,), (1,)), ((), ()))

    def body(a_ref, b_ref, *refs):
        o_ref, *acc = refs[len(deps):]
        part = lax.dot_general(a_ref[...].astype(BF16), b_ref[...].astype(BF16), dims, preferred_element_type=F32)
        if nk == 1:
            o_ref[...] = part.astype(out_dtype)
        else:
            acc_ref, = acc
            kk = pl.program_id(2)

            @pl.when(kk == 0)
            def _():
                acc_ref[...] = part

            @pl.when(kk > 0)
            def _():
                acc_ref[...] += part

            @pl.when(kk == nk - 1)
            def _():
                o_ref[...] = acc_ref[...].astype(out_dtype)

    return pl.pallas_call(
        body, name=name,
        grid=(m // tm, n // tn, nk),
        in_specs=[a_spec, b_spec] + [pl.BlockSpec(memory_space=pl.ANY)] * len(deps),
        out_specs=pl.BlockSpec((tm, tn), lambda i, j, kk: (i, j)),
        out_shape=jax.ShapeDtypeStruct((m, n), out_dtype),
        scratch_shapes=[] if nk == 1 else [pltpu.VMEM((tm, tn), F32)],
        compiler_params=_params("parallel", "parallel", "arbitrary"),
    )(a, b, *deps)


def _mm_tn(a, b, *, name, out_dtype, tm, tn, deps=()):
    k, m = a.shape
    n = b.shape[1]

    def body(a_ref, b_ref, *refs):
        o_ref, at_ref = refs[len(deps):]

        @pl.when(pl.program_id(1) == 0)
        def _():
            at_ref[...] = a_ref[...].astype(F32).T.astype(BF16)

        o_ref[...] = jnp.dot(at_ref[...], b_ref[...].astype(BF16), preferred_element_type=F32).astype(out_dtype)

    return pl.pallas_call(
        body, name=name,
        grid=(m // tm, n // tn),
        in_specs=[pl.BlockSpec((k, tm), lambda i, j: (0, i)), pl.BlockSpec((k, tn), lambda i, j: (0, j))]
        + [pl.BlockSpec(memory_space=pl.ANY)] * len(deps),
        out_specs=pl.BlockSpec((tm, tn), lambda i, j: (i, j)),
        out_shape=jax.ShapeDtypeStruct((m, n), out_dtype),
        scratch_shapes=[pltpu.VMEM((tm, k), BF16)],
        compiler_params=_params("parallel", "arbitrary"),
    )(a, b, *deps)


def _rowcall(name, fn, rows, pars, row_outs, red_rows=(), *, cols, ts=256, tc=None, deps=()):
    rows = [r if isinstance(r, tuple) else (r, 0) for r in rows]
    pars = [p if isinstance(p, tuple) else (p, 0) for p in pars]
    s = rows[0][0].shape[0]
    tc = cols if tc is None else tc
    ts = min(ts, s)
    assert s % ts == 0 and cols % tc == 0, (name, s, ts, cols, tc)
    n_in, n_row_out = len(rows) + len(pars), len(row_outs)

    def body(*refs):
        outs = fn(*[r[...] for r in refs[:n_in]])
        outs = outs if isinstance(outs, (tuple, list)) else (outs,)
        o_refs = refs[n_in + len(deps):]
        for o_ref, val in zip(o_refs[:n_row_out], outs[:n_row_out]):
            o_ref[...] = val.astype(o_ref.dtype)
        first = pl.program_id(1) == 0
        for o_ref, val in zip(o_refs[n_row_out:], outs[n_row_out:]):
            @pl.when(first)
            def _(o_ref=o_ref, val=val):
                o_ref[...] = val

            @pl.when(jnp.logical_not(first))
            def _(o_ref=o_ref, val=val):
                o_ref[...] += val

    def row_map(off):
        return lambda j, i: (i, j + off)

    def par_map(off):
        return lambda j, i: (0, j + off)

    return pl.pallas_call(
        body, name=name,
        grid=(cols // tc, s // ts),
        in_specs=[pl.BlockSpec((ts, tc), row_map(off)) for _, off in rows]
        + [pl.BlockSpec((p.shape[0], tc), par_map(off)) for p, off in pars]
        + [pl.BlockSpec(memory_space=pl.ANY)] * len(deps),
        out_specs=[pl.BlockSpec((ts, tc), lambda j, i: (i, j)) for _ in row_outs]
        + [pl.BlockSpec((r, tc), lambda j, i: (0, j)) for r in red_rows],
        out_shape=[jax.ShapeDtypeStruct((s, cols), dt) for dt in row_outs]
        + [jax.ShapeDtypeStruct((r, cols), F32) for r in red_rows],
        compiler_params=_params("parallel", "arbitrary"),
    )(*[r for r, _ in rows], *[p for p, _ in pars], *deps)


def _swiglu_act(g, u):
    return _silu(g) * u


def _gate_up(n, w, *, name):
    s, d = n.shape
    f = w.shape[1] // 2
    tn = _pick(f, (256, 128))
    nj = f // tn

    def body(a_ref, wg_ref, wu_ref, g_ref, u_ref, act_ref):
        a = a_ref[...].astype(BF16)
        g = jnp.dot(a, wg_ref[...].astype(BF16), preferred_element_type=F32)
        u = jnp.dot(a, wu_ref[...].astype(BF16), preferred_element_type=F32)
        g_ref[...] = g.astype(BF16)
        u_ref[...] = u.astype(BF16)
        act_ref[...] = _swiglu_act(g, u).astype(BF16)

    out = pl.BlockSpec((s, tn), lambda j: (0, j))
    return pl.pallas_call(
        body, name=name,
        grid=(nj,),
        in_specs=[pl.BlockSpec((s, d), lambda j: (0, 0)), pl.BlockSpec((d, tn), lambda j: (0, j)),
                  pl.BlockSpec((d, tn), lambda j: (0, nj + j))],
        out_specs=[out] * 3,
        out_shape=[jax.ShapeDtypeStruct((s, f), BF16)] * 3,
        compiler_params=_params("parallel"),
    )(n, w, w)


def _swiglu_bwd(g, u, dact, *, name, ts=128):
    s, f = g.shape
    ts = min(ts, s)

    def body(g_ref, u_ref, dact_ref, o_ref):
        _, vjp = jax.vjp(_swiglu_act, g_ref[...].astype(F32), u_ref[...].astype(F32))
        dg, du = vjp(dact_ref[...].astype(F32))
        o_ref[:, 0:f] = dg.astype(BF16)
        o_ref[:, f:2 * f] = du.astype(BF16)

    narrow = pl.BlockSpec((ts, f), lambda i: (i, 0))
    return pl.pallas_call(
        body, name=name,
        grid=(s // ts,),
        in_specs=[narrow] * 3,
        out_specs=pl.BlockSpec((ts, 2 * f), lambda i: (i, 0)),
        out_shape=jax.ShapeDtypeStruct((s, 2 * f), BF16),
        compiler_params=_params("parallel"),
    )(g, u, dact)


def _rms(x, g):
    return x * lax.rsqrt(jnp.mean(x * x, axis=-1, keepdims=True) + RMS_EPS) * g


def _sigmoid(x):
    return jax.nn.sigmoid(x)


def _silu(x):
    return x * jax.nn.sigmoid(x)


def _gelu(x):
    return 0.5 * x * (1.0 + jnp.tanh(math.sqrt(2.0 / math.pi) * (x + 0.044715 * (x * x * x))))


def _softplus(x):
    return jnp.maximum(x, 0.0) + jnp.log1p(jnp.exp(-jnp.abs(x)))


def _split(x, terms):
    parts = []
    for _ in range(terms - 1):
        parts.append(x.astype(BF16))
        x = x - parts[-1].astype(F32)
    return parts + [x.astype(BF16)]


def _xdot(x, t, terms=3):
    return sum(jnp.dot(p, t, preferred_element_type=F32) for p in _split(x, terms))


def _xdot_l(t, x):
    return sum(jnp.dot(t, p, preferred_element_type=F32) for p in _split(x, 3))


_NT = (((1,), (1,)), ((), ()))
_TN = (((0,), (0,)), ((), ()))


def _dot(a, b, dims=None):
    if dims is None:
        return jnp.dot(a.astype(BF16), b.astype(BF16), preferred_element_type=F32)
    return lax.dot_general(a.astype(BF16), b.astype(BF16), dims, preferred_element_type=F32)


def _iota(shape, axis):
    return lax.broadcasted_iota(jnp.int32, shape, axis)


def _sb_tile(qb, kblk, mask, upper, c_rem):
    z = lax.dot_general(qb, kblk, _NT, preferred_element_type=F32)
    soft = jnp.log1p(jnp.exp(-jnp.abs(z)))
    lbeta = jnp.minimum(z, 0.0) - soft
    l1m = -jnp.maximum(z, 0.0) - soft
    if mask is not None:
        l1m = jnp.where(mask, l1m, 0.0)
    rem = _xdot(l1m, upper, terms=2) + c_rem
    w = jnp.exp(lbeta + rem)
    if mask is not None:
        w = jnp.where(mask, w, 0.0)
    return lbeta, l1m, w


def _sb_tiles(s):
    tq = min(SB_Q_TILE, s)
    return tq, SB_K_TILE, tq // SB_K_TILE


def _sb_key_loops(qi, per_q, step, carry):
    n_full = qi * per_q
    carry = lax.fori_loop(0, per_q, lambda j, c: step(n_full + per_q - 1 - j, True, c), carry)
    return lax.fori_loop(0, n_full, lambda j, c: step(n_full - 1 - j, False, c), carry)


def _sb_fwd(proj, n_heads, *, name):
    s = proj.shape[0]
    t, tk, per_q = _sb_tiles(s)
    scale = HEAD_DIM ** -0.5

    def body(q_ref, k_ref, v_ref, o_ref):
        qi = pl.program_id(1)
        qb = (q_ref[...] * scale).astype(BF16)
        row, col = _iota((t, tk), 0) + qi * t, _iota((t, tk), 1)
        upper = (_iota((tk, tk), 0) > _iota((tk, tk), 1)).astype(BF16)

        def step(kb, masked, carry):
            acc, c_rem = carry
            rows = pl.ds(pl.multiple_of(kb * tk, tk), tk)
            kblk = k_ref[rows, :].astype(BF16)
            vblk = v_ref[rows, :].astype(BF16)
            _, l1m, w = _sb_tile(qb, kblk, (col + kb * tk) < row if masked else None, upper, c_rem)
            acc = acc + jnp.dot(w.astype(BF16), vblk, preferred_element_type=F32)
            return acc, c_rem + jnp.sum(l1m, axis=1, keepdims=True)

        acc, _ = _sb_key_loops(qi, per_q, step, (jnp.zeros((t, HEAD_DIM), F32), jnp.zeros((t, 1), F32)))
        o_ref[...] = acc

    return pl.pallas_call(
        body, name=name,
        grid=(n_heads, s // t),
        in_specs=[pl.BlockSpec((t, HEAD_DIM), lambda h, i: (i, h)),
                  pl.BlockSpec((s, HEAD_DIM), lambda h, i: (0, n_heads + h)),
                  pl.BlockSpec((s, HEAD_DIM), lambda h, i: (0, 2 * n_heads + h))],
        out_specs=pl.BlockSpec((t, HEAD_DIM), lambda h, i: (i, h)),
        out_shape=jax.ShapeDtypeStruct((s, n_heads * HEAD_DIM), F32),
        compiler_params=_params("parallel", "arbitrary"),
    )(proj, proj, proj)


def _sb_bwd(proj, dcat, n_heads, *, name):
    s = proj.shape[0]
    t, tk, per_q = _sb_tiles(s)
    scale = HEAD_DIM ** -0.5

    def body(q_ref, k_ref, v_ref, do_ref, dq_ref, dk_ref, dv_ref, g_s, sig_s):
        qi = pl.program_id(1)

        @pl.when(qi == 0)
        def _():
            dk_ref[...] = jnp.zeros_like(dk_ref)
            dv_ref[...] = jnp.zeros_like(dv_ref)

        qb = (q_ref[...] * scale).astype(BF16)
        dob = do_ref[...].astype(BF16)
        row, col = _iota((t, tk), 0) + qi * t, _iota((t, tk), 1)
        upper = (_iota((tk, tk), 0) > _iota((tk, tk), 1)).astype(BF16)
        lower_incl = (_iota((tk, tk), 0) >= _iota((tk, tk), 1)).astype(BF16)

        def weights(kb, masked, carry):
            c_rem, g_all = carry
            rows = pl.ds(pl.multiple_of(kb * tk, tk), tk)
            kblk = k_ref[rows, :].astype(BF16)
            vblk = v_ref[rows, :].astype(BF16)
            lbeta, l1m, w = _sb_tile(qb, kblk, (col + kb * tk) < row if masked else None, upper, c_rem)
            g = w * lax.dot_general(dob, vblk, _NT, preferred_element_type=F32)
            dv_ref[rows, :] += lax.dot_general(w.astype(BF16), dob, _TN, preferred_element_type=F32)
            g_s[kb] = g
            sig_s[kb] = jnp.exp(lbeta)
            return c_rem + jnp.sum(l1m, axis=1, keepdims=True), g_all + jnp.sum(g, axis=1, keepdims=True)

        zero_col = jnp.zeros((t, 1), F32)
        _, g_all = _sb_key_loops(qi, per_q, weights, (zero_col, zero_col))

        def scores(kb, masked, carry):
            dq, c_g = carry
            rows = pl.ds(pl.multiple_of(kb * tk, tk), tk)
            g, sig = g_s[kb], sig_s[kb]
            g_before = g_all - (_xdot(g, lower_incl) + c_g)
            dz = g * (1.0 - sig) - g_before * sig
            if masked:
                dz = jnp.where((col + kb * tk) < row, dz, 0.0)
            dz = dz.astype(BF16)
            dq = dq + jnp.dot(dz, k_ref[rows, :].astype(BF16), preferred_element_type=F32)
            dk_ref[rows, :] += lax.dot_general(dz, qb, _TN, preferred_element_type=F32)
            return dq, c_g + jnp.sum(g, axis=1, keepdims=True)

        dq, _ = _sb_key_loops(qi, per_q, scores, (jnp.zeros((t, HEAD_DIM), F32), zero_col))
        dq_ref[...] = dq * scale

    width = n_heads * HEAD_DIM
    return pl.pallas_call(
        body, name=name,
        grid=(n_heads, s // t),
        in_specs=[pl.BlockSpec((t, HEAD_DIM), lambda h, i: (i, h)),
                  pl.BlockSpec((s, HEAD_DIM), lambda h, i: (0, n_heads + h)),
                  pl.BlockSpec((s, HEAD_DIM), lambda h, i: (0, 2 * n_heads + h)),
                  pl.BlockSpec((t, HEAD_DIM), lambda h, i: (i, h))],
        out_specs=[pl.BlockSpec((t, HEAD_DIM), lambda h, i: (i, h)),
                   pl.BlockSpec((s, HEAD_DIM), lambda h, i: (0, h)),
                   pl.BlockSpec((s, HEAD_DIM), lambda h, i: (0, h))],
        out_shape=[jax.ShapeDtypeStruct((s, width), F32)] * 3,
        scratch_shapes=[pltpu.VMEM((s // tk, t, tk), F32)] * 2,
        compiler_params=_params("parallel", "arbitrary"),
    )(proj, proj, proj, dcat)


def _hg_pre(hq, hf, logits):
    mx = jnp.max(logits, axis=0, keepdims=True)
    ex = jnp.exp(logits - mx)
    lb = ex[0:1, :] / jnp.sum(ex, axis=0, keepdims=True)
    f = lb + (1.0 - lb) * _sigmoid(hf)
    return _silu(hq), 1.0 - f, jnp.log(f)


def _hg_post(o, norm_g, hgate):
    return _rms(o, norm_g) * _silu(hgate)


def _hg_specs(s, n_heads, first_block):
    def at(group):
        return pl.BlockSpec((s, HEAD_DIM), lambda h: (0, first_block + group * n_heads + h))
    return [at(0), at(1), at(2), at(3)]


def _hg_fwd(proj, logits, norm_g, n_heads, *, name):
    s = proj.shape[0]
    hc = HG_CHUNK
    n_chunks = s // hc
    d = HEAD_DIM

    hp = HG_HEADS_PER_STEP
    assert n_heads % hp == 0
    wide = hp * d

    def body(lg_ref, ng_ref, hq_ref, hf_ref, hi_ref, hgt_ref, out_ref, oraw_ref, st_ref,
             q_s, k_s, lf_s, cum_s, qc_s, oc_s):
        q, k, lf = _hg_pre(hq_ref[...], hf_ref[...], lg_ref[...])
        q_s[...] = q
        k_s[...] = k
        lf_s[...] = lf
        tril = (_iota((hc, hc), 0) >= _iota((hc, hc), 1)).astype(BF16)
        srow = _iota((hc, d), 0)

        def head_chunk(j, ci, rows, st):
            ln = slice(j * d, (j + 1) * d)
            q, k, v = q_s[rows, ln], k_s[rows, ln], hi_ref[rows, ln]
            cum = _xdot_l(tril, lf_s[rows, ln])
            st_ref[j, ci] = st
            o_inter = _dot(q * jnp.exp(cum), st, _NT)
            cum_s[:, ln] = cum
            qc_s[:, ln] = q
            for t in range(hc):
                ng = (t // SUBLANES + 1) * SUBLANES
                e = jnp.where(srow[:ng] <= t, jnp.exp(cum_s[t:t + 1, ln] - cum[:ng]), 0.0)
                sc = jnp.sum(qc_s[t:t + 1, ln] * k[:ng] * e, axis=1, keepdims=True)
                oc_s[t:t + 1, ln] = jnp.sum(sc * v[:ng], axis=0, keepdims=True)
            oraw_ref[rows, ln] = o_inter + oc_s[:, ln]
            last = cum_s[hc - 1:hc, ln]
            return st * jnp.exp(last) + _dot(v, k * jnp.exp(last - cum), _TN)

        def chunk(ci, states):
            rows = pl.ds(pl.multiple_of(ci * hc, hc), hc)
            return tuple(head_chunk(j, ci, rows, st) for j, st in enumerate(states))

        lax.fori_loop(0, n_chunks, chunk, tuple(jnp.zeros((d, d), F32) for _ in range(hp)))
        for j in range(hp):
            ln = slice(j * d, (j + 1) * d)
            out_ref[:, ln] = _hg_post(oraw_ref[:, ln], ng_ref[...], hgt_ref[:, ln]).astype(BF16)

    width = n_heads * d
    first = 3 * n_heads // hp
    groups = [pl.BlockSpec((s, wide), functools.partial(lambda h, g: (0, first + g * (n_heads // hp) + h), g=g))
              for g in range(4)]
    head_block = pl.BlockSpec((s, wide), lambda h: (0, h))
    return pl.pallas_call(
        body, name=name,
        grid=(n_heads // hp,),
        in_specs=[pl.BlockSpec((2, wide), lambda h: (0, h)), pl.BlockSpec((1, d), lambda h: (0, 0))] + groups,
        out_specs=[head_block, head_block, pl.BlockSpec((hp, n_chunks, d, d), lambda h: (h, 0, 0, 0))],
        out_shape=[jax.ShapeDtypeStruct((s, width), BF16), jax.ShapeDtypeStruct((s, width), F32),
                   jax.ShapeDtypeStruct((n_heads, n_chunks, d, d), F32)],
        scratch_shapes=[pltpu.VMEM((s, wide), F32)] * 3 + [pltpu.VMEM((hc, wide), F32)] * 3,
        compiler_params=_params("arbitrary"),
    )(logits, norm_g, proj, proj, proj, proj)


def _hg_bwd(proj, logits, norm_g, oraw, states, dcat, n_heads, *, name):
    s = proj.shape[0]
    hc = HG_CHUNK
    n_chunks = s // hc
    d = HEAD_DIM

    def body(lg_ref, ng_ref, hq_ref, hf_ref, hi_ref, hgt_ref, oraw_ref, st_ref, dout_ref,
             dhq_ref, dhf_ref, dhi_ref, dhgt_ref, dlg_ref, dng_ref,
             q_s, k_s, lf_s, do_s, dq_s, dk_s, dlf_s, cum_s, qc_s, doc_s, dqc_s, dkc_s, dvc_s):
        head = pl.program_id(0)
        (q, k, lf), pre_vjp = jax.vjp(_hg_pre, hq_ref[...], hf_ref[...], lg_ref[...])
        q_s[...] = q
        k_s[...] = k
        lf_s[...] = lf
        _, post_vjp = jax.vjp(_hg_post, oraw_ref[...], ng_ref[...], hgt_ref[...])
        do, dng, dhgt = post_vjp(dout_ref[...])
        do_s[...] = do
        dhgt_ref[...] = dhgt.astype(BF16)

        @pl.when(head == 0)
        def _():
            dng_ref[...] = dng

        @pl.when(head > 0)
        def _():
            dng_ref[...] += dng

        triu = (_iota((hc, hc), 0) <= _iota((hc, hc), 1)).astype(BF16)
        tril = (_iota((hc, hc), 0) >= _iota((hc, hc), 1)).astype(BF16)
        srow = _iota((hc, d), 0)

        def chunk(j, dst):
            ci = n_chunks - 1 - j
            rows = pl.ds(pl.multiple_of(ci * hc, hc), hc)
            q, k, v, do_c = q_s[rows, :], k_s[rows, :], hi_ref[rows, :], do_s[rows, :]
            cum = _xdot_l(tril, lf_s[rows, :])
            st = st_ref[0, ci]
            cum_s[...] = cum
            qc_s[...] = q
            doc_s[...] = do_c
            last = cum_s[hc - 1:hc, :]
            e_cum, e_last = jnp.exp(cum), jnp.exp(last - cum)
            dqc_s[...] = _dot(do_c, st) * e_cum
            dk_state = _dot(v, dst) * e_last
            dkc_s[...] = dk_state
            dvc_s[...] = _dot(k * e_last, dst, _NT)
            d_last = (jnp.sum(dst * st, axis=0, keepdims=True) * jnp.exp(last)
                      + jnp.sum(k * dk_state, axis=0, keepdims=True))
            for t in range(hc):
                ng = (t // SUBLANES + 1) * SUBLANES
                qt, dot_ = qc_s[t:t + 1, :], doc_s[t:t + 1, :]
                e = jnp.where(srow[:ng] <= t, jnp.exp(cum_s[t:t + 1, :] - cum[:ng]), 0.0)
                ke = k[:ng] * e
                d_a = jnp.sum(dot_ * v[:ng], axis=1, keepdims=True)
                dqc_s[t:t + 1, :] += jnp.sum(d_a * ke, axis=0, keepdims=True)
                dkc_s[0:ng, :] += d_a * (qt * e)
                dvc_s[0:ng, :] += jnp.sum(qt * ke, axis=1, keepdims=True) * dot_
            dq, dk = dqc_s[...], dkc_s[...]
            d_b = q * dq - k * dk
            dq_s[rows, :] = dq
            dk_s[rows, :] = dk
            dhi_ref[rows, :] = dvc_s[...].astype(BF16)
            dlf_s[rows, :] = _xdot_l(triu, d_b) + d_last
            return dst * jnp.exp(last) + _dot(do_c, q * e_cum, _TN)

        lax.fori_loop(0, n_chunks, chunk, jnp.zeros((d, d), F32))
        dhq, dhf, dlg = pre_vjp((dq_s[...], dk_s[...], dlf_s[...]))
        dhq_ref[...] = dhq.astype(BF16)
        dhf_ref[...] = dhf.astype(BF16)
        dlg_ref[...] = dlg

    width = n_heads * d
    head_block = pl.BlockSpec((s, d), lambda h: (0, h))
    return pl.pallas_call(
        body, name=name,
        grid=(n_heads,),
        in_specs=[pl.BlockSpec((2, d), lambda h: (0, h)), pl.BlockSpec((1, d), lambda h: (0, 0))]
        + _hg_specs(s, n_heads, 3 * n_heads)
        + [head_block, pl.BlockSpec((1, n_chunks, d, d), lambda h: (h, 0, 0, 0)),
           pl.BlockSpec((s, d), lambda h: (0, n_heads + h))],
        out_specs=[head_block] * 4 + [pl.BlockSpec((2, d), lambda h: (0, h)), pl.BlockSpec((1, d), lambda h: (0, 0))],
        out_shape=[jax.ShapeDtypeStruct((s, width), BF16)] * 4
        + [jax.ShapeDtypeStruct((2, width), F32), jax.ShapeDtypeStruct((1, d), F32)],
        scratch_shapes=[pltpu.VMEM((s, d), F32)] * 7 + [pltpu.VMEM((hc, d), F32)] * 6,
        compiler_params=_params("arbitrary"),
    )(logits, norm_g, proj, proj, proj, proj, oraw, states, dcat)


def _shift_down(x, n, srow):
    if n == 0:
        return x
    return jnp.where(srow >= n, pltpu.roll(x, n, 0), 0.0)


def _shift_up(x, n, srow):
    if n == 0:
        return x
    s = x.shape[0]
    return jnp.where(srow < s - n, pltpu.roll(x, s - n, 0), 0.0)


def _rg_gates_fwd(proj, conv_w, conv_b, wa, ba, wx, bx, *, name):
    s = proj.shape[0]
    nb = wa.shape[0]
    bw = RG_BLOCK

    def body(xb_ref, cw_ref, cb_ref, wa_ref, ba_ref, wx_ref, bx_ref, xc_ref, ra_ref, ix_ref):
        x = xb_ref[...]
        srow = _iota((s, bw), 0)
        cw = cw_ref[...]
        xc = cb_ref[...] + cw[0:1, :] * x
        for tap in range(1, CONV_TAPS):
            xc = xc + cw[tap:tap + 1, :] * _shift_down(x, tap, srow)
        xc_ref[...] = xc
        ra_ref[...] = _dot(xc, wa_ref[0]) + ba_ref[0]
        ix_ref[...] = _dot(xc, wx_ref[0]) + bx_ref[0]

    col = pl.BlockSpec((s, bw), lambda n: (0, n))
    vec = lambda r: pl.BlockSpec((r, bw), lambda n: (0, n))
    mat = pl.BlockSpec((1, bw, bw), lambda n: (n, 0, 0))
    bias = pl.BlockSpec((1, 1, bw), lambda n: (n, 0, 0))
    return pl.pallas_call(
        body, name=name,
        grid=(nb,),
        in_specs=[pl.BlockSpec((s, bw), lambda n: (0, nb + n)), vec(CONV_TAPS), vec(1), mat, bias, mat, bias],
        out_specs=[col] * 3,
        out_shape=[jax.ShapeDtypeStruct((s, nb * bw), F32)] * 3,
        compiler_params=_params("parallel"),
    )(proj, conv_w, conv_b, wa, ba, wx, bx)


def _rg_au(ra, ix, xc, lam, first_row):
    log_a = -RG_C * _sigmoid(ra) * _softplus(-lam)
    th = jnp.tanh(log_a)
    one_minus_a2 = -2.0 * th / (1.0 - th)
    mult = jnp.where(first_row, 1.0, jnp.sqrt(one_minus_a2))
    return jnp.exp(log_a), xc * _sigmoid(ix) * mult


def _rg_out(gate, hs):
    return _gelu(gate) * hs


def _linear_scan(a, b, a_s, b_s, in_s, reverse):
    s, c = a.shape
    within = _iota((s, c), 0) & (SUBLANES - 1)
    shift = 1
    while shift < SUBLANES:
        if reverse:
            take = within < SUBLANES - shift
            a_n, b_n = pltpu.roll(a, s - shift, 0), pltpu.roll(b, s - shift, 0)
        else:
            take = within >= shift
            a_n, b_n = pltpu.roll(a, shift, 0), pltpu.roll(b, shift, 0)
        b = jnp.where(take, a * b_n + b, b)
        a = jnp.where(take, a * a_n, a)
        shift *= 2
    a_s[...] = a
    b_s[...] = b
    n_tiles = s // SUBLANES
    edge = 0 if reverse else SUBLANES - 1

    def tile(i, h):
        rows = pl.ds(pl.multiple_of(((n_tiles - 1 - i) if reverse else i) * SUBLANES, SUBLANES), SUBLANES)
        in_s[rows, :] = jnp.broadcast_to(h, (SUBLANES, c))
        return a_s[rows, :][edge:edge + 1, :] * h + b_s[rows, :][edge:edge + 1, :]

    lax.fori_loop(0, n_tiles, tile, jnp.zeros((1, c), F32))
    return a * in_s[...] + b


def _rg_scan_fwd(proj, xc, ra, ix, lam, *, name):
    s, width = xc.shape
    tc = LANES

    def body(gate_ref, xc_ref, ra_ref, ix_ref, lam_ref, hs_ref, gact_ref, a_s, u_s, in_s):
        first_row = _iota((s, tc), 0) == 0
        a, u = _rg_au(ra_ref[...], ix_ref[...], xc_ref[...], lam_ref[...], first_row)
        hs = _linear_scan(a, u, a_s, u_s, in_s, reverse=False)
        hs_ref[...] = hs
        gact_ref[...] = _rg_out(gate_ref[...], hs).astype(BF16)

    col = pl.BlockSpec((s, tc), lambda n: (0, n))
    return pl.pallas_call(
        body, name=name,
        grid=(width // tc,),
        in_specs=[col, col, col, col, pl.BlockSpec((1, tc), lambda n: (0, n))],
        out_specs=[col, col],
        out_shape=[jax.ShapeDtypeStruct((s, width), F32), jax.ShapeDtypeStruct((s, width), BF16)],
        scratch_shapes=[pltpu.VMEM((s, tc), F32)] * 3,
        compiler_params=_params("parallel"),
    )(proj, xc, ra, ix, lam)


def _rg_scan_bwd(dgo, proj, hs, xc, ra, ix, lam, *, name):
    s, width = xc.shape
    tc = LANES

    def body(dgo_ref, gate_ref, hs_ref, xc_ref, ra_ref, ix_ref, lam_ref,
             dgate_ref, dra_ref, dix_ref, dxc_ref, dlam_ref, a_s, dh_s, g_s):
        srow = _iota((s, tc), 0)
        hs = hs_ref[...]
        _, out_vjp = jax.vjp(_rg_out, gate_ref[...], hs)
        dgate, dh = out_vjp(dgo_ref[...])
        dgate_ref[...] = dgate.astype(BF16)
        au = functools.partial(_rg_au, first_row=srow == 0)
        (a, _), au_vjp = jax.vjp(au, ra_ref[...], ix_ref[...], xc_ref[...], lam_ref[...])
        g = _linear_scan(_shift_up(a, 1, srow), dh, a_s, dh_s, g_s, reverse=True)
        dra, dix, dxc, dlam = au_vjp((g * _shift_down(hs, 1, srow), g))
        dra_ref[...] = dra.astype(BF16)
        dix_ref[...] = dix.astype(BF16)
        dxc_ref[...] = dxc
        dlam_ref[...] = dlam

    col = pl.BlockSpec((s, tc), lambda n: (0, n))
    vec = pl.BlockSpec((1, tc), lambda n: (0, n))
    return pl.pallas_call(
        body, name=name,
        grid=(width // tc,),
        in_specs=[col] * 6 + [vec],
        out_specs=[col] * 4 + [vec],
        out_shape=[jax.ShapeDtypeStruct((s, width), BF16)] * 3
        + [jax.ShapeDtypeStruct((s, width), F32), jax.ShapeDtypeStruct((1, width), F32)],
        scratch_shapes=[pltpu.VMEM((s, tc), F32)] * 3,
        compiler_params=_params("parallel"),
    )(dgo, proj, hs, xc, ra, ix, lam)


def _rg_gates_bwd(dra, dix, dxc1, xc, proj, conv_w, wa, wx, *, name):
    s = proj.shape[0]
    nb = wa.shape[0]
    bw = RG_BLOCK

    def body(dra_ref, dix_ref, dxc_ref, xc_ref, xb_ref, cw_ref, wa_ref, wx_ref,
             dxb_ref, dcw_ref, dcb_ref, dwa_ref, dba_ref, dwx_ref, dbx_ref):
        dra, dix = dra_ref[...], dix_ref[...]
        xc_t = xc_ref[...].T.astype(BF16)
        dwa_ref[0] = jnp.dot(xc_t, dra, preferred_element_type=F32)
        dwx_ref[0] = jnp.dot(xc_t, dix, preferred_element_type=F32)
        dba_ref[0] = jnp.sum(dra.astype(F32), axis=0, keepdims=True)
        dbx_ref[0] = jnp.sum(dix.astype(F32), axis=0, keepdims=True)
        dxc = dxc_ref[...] + _dot(dra, wa_ref[0], _NT) + _dot(dix, wx_ref[0], _NT)
        srow = _iota((s, bw), 0)
        x = xb_ref[...]
        cw = cw_ref[...]
        dx = cw[0:1, :] * dxc
        dcw = [jnp.sum(dxc * x, axis=0, keepdims=True)]
        for tap in range(1, CONV_TAPS):
            dx = dx + cw[tap:tap + 1, :] * _shift_up(dxc, tap, srow)
            dcw.append(jnp.sum(dxc * _shift_down(x, tap, srow), axis=0, keepdims=True))
        dxb_ref[...] = dx.astype(BF16)
        r4 = _iota((CONV_TAPS, bw), 0)
        acc = jnp.zeros((CONV_TAPS, bw), F32)
        for tap in range(CONV_TAPS):
            acc = jnp.where(r4 == tap, dcw[tap], acc)
        dcw_ref[...] = acc
        dcb_ref[...] = jnp.sum(dxc, axis=0, keepdims=True)

    col = pl.BlockSpec((s, bw), lambda n: (0, n))
    vec = lambda r: pl.BlockSpec((r, bw), lambda n: (0, n))
    mat = pl.BlockSpec((1, bw, bw), lambda n: (n, 0, 0))
    bias = pl.BlockSpec((1, 1, bw), lambda n: (n, 0, 0))
    width = nb * bw
    return pl.pallas_call(
        body, name=name,
        grid=(nb,),
        in_specs=[col, col, col, col, pl.BlockSpec((s, bw), lambda n: (0, nb + n)), vec(CONV_TAPS), mat, mat],
        out_specs=[col, vec(CONV_TAPS), vec(1), mat, bias, mat, bias],
        out_shape=[jax.ShapeDtypeStruct((s, width), BF16), jax.ShapeDtypeStruct((CONV_TAPS, width), F32),
                   jax.ShapeDtypeStruct((1, width), F32), jax.ShapeDtypeStruct((nb, bw, bw), F32),
                   jax.ShapeDtypeStruct((nb, 1, bw), F32), jax.ShapeDtypeStruct((nb, bw, bw), F32),
                   jax.ShapeDtypeStruct((nb, 1, bw), F32)],
        compiler_params=_params("parallel"),
    )(dra, dix, dxc1, xc, proj, conv_w, wa, wx)


_HBM = pl.BlockSpec(memory_space=pltpu.HBM)
_FLIPS = ((0, 0, 1), (1, 0, 0), (0, 1, 0), (1, 1, 0))
_ALL_FLIPS = tuple((a, b, c) for a in (0, 1) for b in (0, 1) for c in (0, 1))[1:]


def _flip(pos, f):
    return tuple(1 - p if b else p for p, b in zip(pos, f))


def _dev_index(pos):
    return 4 * pos[0] + 2 * pos[1] + pos[2]


def _block(ref, idx, cols):
    if not cols:
        return ref.at[idx]
    n = ref.shape[-1] // N_DEV
    start = pl.multiple_of(idx * n, LANES)
    return ref.at[(slice(None),) * (len(ref.shape) - 1) + (pl.ds(start, n),)]


_SEM = pl.BlockSpec(memory_space=pltpu.SEMAPHORE)
_ANY = pl.BlockSpec(memory_space=pl.ANY)
_N_PEERS = N_DEV - 1


def _hbm(x):
    return pltpu.with_memory_space_constraint(x, pltpu.HBM)


def _me():
    return lax.axis_index("x"), lax.axis_index("y"), lax.axis_index("c")


def _spread_copies(plan, src_refs, land_refs, send_sems, recv_sems, local_sems):
    local, remote = plan(src_refs, land_refs)
    local = [pltpu.make_async_copy(src, dst, local_sems.at[i]) for i, (src, dst) in enumerate(local)]
    remote = [pltpu.make_async_remote_copy(src_ref=src, dst_ref=dst, send_sem=send_sems.at[k], recv_sem=recv_sems.at[k],
                                           device_id=peer, device_id_type=pl.DeviceIdType.MESH)
              for k, (src, dst, peer) in enumerate(remote)]
    return local, remote


def _spread_start(srcs, lands, plan, n_remote, n_local, *, name, after=()):
    ns, nl = len(srcs), len(lands)
    n_in = ns + nl + len(after)

    def body(*refs):
        src_refs, land_refs = refs[:ns], refs[ns:ns + nl]
        send_sems, recv_sems, local_sems = refs[n_in:n_in + 3]
        local, remote = _spread_copies(plan, src_refs, land_refs, send_sems, recv_sems, local_sems)
        for cp in local + remote:
            cp.start()
        token = refs[-1]
        token[...] = jnp.zeros_like(token)

    lands = [_hbm(lax.empty(*x)) if isinstance(x, tuple) else x for x in lands]
    out = pl.pallas_call(
        body, name=name,
        in_specs=[_HBM] * (ns + nl) + [_ANY] * len(after),
        out_specs=[_SEM] * 3 + [_HBM] * (ns + nl) + [pl.BlockSpec(memory_space=pltpu.VMEM)],
        out_shape=[pltpu.SemaphoreType.DMA((n_remote,)), pltpu.SemaphoreType.DMA((n_remote,)),
                   pltpu.SemaphoreType.DMA((max(n_local, 1),))]
        + [pltpu.HBM(x.shape, x.dtype) for x in list(srcs) + lands]
        + [jax.ShapeDtypeStruct((SUBLANES, LANES), F32)],
        input_output_aliases={i: 3 + i for i in range(ns + nl)},
        compiler_params=pltpu.CompilerParams(has_side_effects=pltpu.SideEffectType.DATAFLOW_SIDE_EFFECTING),
    )(*[_hbm(x) for x in srcs], *lands, *after)
    return dict(sems=list(out[:3]), srcs=list(out[3:3 + ns]), lands=list(out[3 + ns:3 + ns + nl]),
                token=out[-1], plan=plan)


def _spread_wait(handle, after, *, name):
    ns, nl = len(handle["srcs"]), len(handle["lands"])

    def body(*refs):
        src_refs, land_refs = refs[:ns], refs[ns:ns + nl]
        send_sems, recv_sems, local_sems = refs[ns + nl:ns + nl + 3]
        local, remote = _spread_copies(handle["plan"], src_refs, land_refs, send_sems, recv_sems, local_sems)
        for cp in local:
            cp.wait()
        for cp in remote:
            cp.wait_send()
            cp.wait_recv()

    out = pl.pallas_call(
        body, name=name,
        in_specs=[_HBM] * (ns + nl) + [_SEM] * 3 + [_ANY],
        out_specs=[_HBM] * (ns + nl),
        out_shape=[pltpu.HBM(x.shape, x.dtype) for x in handle["srcs"] + handle["lands"]],
        input_output_aliases={i: i for i in range(ns + nl)},
        compiler_params=pltpu.CompilerParams(has_side_effects=pltpu.SideEffectType.DATAFLOW_SIDE_EFFECTING),
    )(*handle["srcs"], *handle["lands"], *handle["sems"], after)
    return list(out[ns:])


def _gather_start(x, *, name, cols=False, relayed=True, after=()):
    shape = x.shape[:-1] + (N_DEV * x.shape[-1],) if cols else (N_DEV,) + x.shape
    flips = _FLIPS if relayed else _ALL_FLIPS

    def plan(src_refs, land_refs):
        me = _me()
        mine = _block(land_refs[0], _dev_index(me), cols)
        return [(src_refs[0], mine)], [(src_refs[0], mine, _flip(me, f)) for f in flips]

    handle = _spread_start([x], [(shape, x.dtype)], plan, len(flips), 1, name=name, after=after)
    handle["cols"] = cols
    return handle


def _gather_relay(handle, after, *, name):
    cols = handle["cols"]
    land, = _spread_wait(handle, after, name=f"{name}_arrived")

    def plan(src_refs, land_refs):
        me = _me()
        blocks = [_block(land_refs[0], _dev_index(_flip(me, f)), cols) for f in _FLIPS[1:]]
        return [], [(blk, blk, _flip(me, _FLIPS[0])) for blk in blocks]

    return _spread_start([], [land], plan, len(_FLIPS) - 1, 0, name=f"{name}_pass")


def _exchange_start(ps, *, name, cols=False, after=()):
    blk = ps[0].shape[:-1] + (ps[0].shape[-1] // N_DEV,) if cols else ps[0].shape[1:]

    def plan(src_refs, land_refs):
        me = _me()
        me_idx = _dev_index(me)
        local = [(_block(src, me_idx, cols), land_refs[0].at[me_idx, a]) for a, src in enumerate(src_refs)]
        remote = [(_block(src, _dev_index(_flip(me, f)), cols), land_refs[0].at[me_idx, a], _flip(me, f))
                  for f in _ALL_FLIPS for a, src in enumerate(src_refs)]
        return local, remote

    return _spread_start(ps, [((N_DEV, len(ps)) + blk, ps[0].dtype)], plan, _N_PEERS * len(ps), len(ps), name=name,
                         after=after)


def _adamw(parts, w, m, v, *, name, layer=0, prev=None):
    n_rows, c = w.shape
    r = parts.shape[1]
    row_bytes = c * (N_DEV * parts.dtype.itemsize + 7 * 4) * 2
    tr = r
    for cand in (512, 256, 128, 64, 32, 16):
        if r % cand == 0 and cand * row_bytes <= ADAMW_BLOCK_BYTES:
            tr = cand
            break
    c1 = 1.0 - ADAM_B1 ** ADAM_STEP
    c2 = 1.0 - ADAM_B2 ** ADAM_STEP

    def body(p_ref, w_ref, m_ref, v_ref, *rest):
        g_ref, d_ref, nm_ref, nv_ref = rest[-4:]
        g = p_ref[0].astype(F32)
        for j in range(1, N_DEV):
            g = g + p_ref[j].astype(F32)
        nm = ADAM_B1 * m_ref[...] + (1.0 - ADAM_B1) * g
        nv = ADAM_B2 * v_ref[...] + (1.0 - ADAM_B2) * (g * g)
        g_ref[...] = g
        nm_ref[...] = nm
        nv_ref[...] = nv
        d_ref[...] = -ADAM_LR * ((nm * (1.0 / c1)) / (jnp.sqrt(nv * (1.0 / c2)) + ADAM_EPS) + ADAM_WD * w_ref[...])

    off = layer * (r // tr)
    blk = pl.BlockSpec((tr, c), lambda i: (i + off, 0))
    prev = list(prev) if prev is not None else []
    return pl.pallas_call(
        body, name=name,
        grid=(r // tr,),
        in_specs=[pl.BlockSpec((N_DEV, tr, c), lambda i: (0, i, 0)), blk, blk, blk] + [_ANY] * len(prev),
        out_specs=[blk] * 4,
        out_shape=[jax.ShapeDtypeStruct((n_rows, c), F32)] * 4,
        input_output_aliases={4 + j: j for j in range(len(prev))},
        compiler_params=_params("parallel"),
    )(parts, w, m, v, *prev)


_TN_CANDS = (512, 256, 128)
_TK_MAX = 5632
_TK_WHOLE_ROWS = 2816


def _contraction_tiles(m, k):
    tk = k
    while tk > _TK_MAX and tk % 2 == 0 and (tk // 2) % LANES == 0:
        tk //= 2
    tm = m if tk <= _TK_WHOLE_ROWS or m % 2 else m // 2
    return tm, tk


def _nn(a, b, name, out_dtype=F32):
    tm, tk = _contraction_tiles(*a.shape)
    return _mm(a, b, "nn", name=name, out_dtype=out_dtype, tm=tm, tn=_pick(b.shape[1], _TN_CANDS), tk=tk)


def _nt(a, b, name, out_dtype=F32, deps=()):
    tm, tk = _contraction_tiles(*a.shape)
    return _mm(a, b, "nt", name=name, out_dtype=out_dtype, tm=tm, tn=_pick(b.shape[0], _TN_CANDS), tk=tk,
               deps=deps)


def _tn(a, b, name, out_dtype=BF16, deps=()):
    assert a.shape[0] == b.shape[0], (a.shape, b.shape)
    return _mm_tn(a, b, name=name, out_dtype=out_dtype, tm=_pick(a.shape[1], _TN_CANDS),
                  tn=_pick(b.shape[1], (1024,) + _TN_CANDS), deps=deps)


def _local_step(x, p, target, rep, weight, emit, n_heads, start_tokens=()):
    s, d = x.shape
    depth = p.shape[0]
    grads = {}
    rep_grads = {k: [None] * depth for k in ("mix_pre_g", "mix_post_g", "ffn_pre_g", "ffn_post_g", "ple_norm_g")}

    pending = list(start_tokens)
    gains = {}

    def gain(name, i):
        if (name, i) not in gains:
            gains[name, i] = rep[name][i:i + 1]
        return gains[name, i]

    def send(name, layer, g):
        token = emit(name, layer, g)
        if token is not None:
            pending.append(token)

    def rowcall(*args, **kwargs):
        deps, pending[:] = tuple(pending), []
        return _rowcall(*args, deps=deps, **kwargs)

    def grad_in(*args, **kwargs):
        deps, pending[:] = tuple(pending), []
        return _nt(*args, deps=deps, **kwargs)

    deferred = []

    def send_small(i, name, layer, a, b, mm_name):
        if i == 0 and depth > 1:
            deferred.append((name, layer, a, b, mm_name))
        else:
            send(name, layer, _tn(a, b, mm_name))

    saved = []
    h = x
    for i in range(depth):
        sv = {"h": h}
        n1, = rowcall(f"pre_norm{i}", lambda hh, g: _rms(hh, g), [h], [gain("mix_pre_g", i)], [BF16], cols=d)
        sv["n1"] = n1
        if i % 2 == 0:
            proj = _nn(n1, weight("w_in_even", 0, n1), f"in_even{i}")
            a_out = _sb_fwd(proj, n_heads, name=f"sb_fwd{i}")
            b_out, oraw, states = _hg_fwd(proj, rep["hg_lb_logits"], rep["hg_norm_g"], n_heads, name=f"hg_fwd{i}")
            weight("w_gate_up", i, b_out, relay_only=True)
            cat = jnp.concatenate([a_out.astype(BF16), b_out], axis=1)
            m = _nn(cat, weight("w_out_even", 0, cat), f"out_even{i}")
            sv.update(proj=proj, oraw=oraw, states=states, cat=cat)
        else:
            proj = _nn(n1, weight("w_in_odd", 0, n1), f"in_odd{i}")
            sm = {k: weight(k, 0, proj) for k in _SMALL}
            xc, ra, ix = _rg_gates_fwd(proj, sm["conv_w"], sm["conv_b"], sm["rg_wa"], sm["rg_ba"],
                                       sm["rg_wx"], sm["rg_bx"], name=f"rg_gates_fwd{i}")
            hs, gact = _rg_scan_fwd(proj, xc, ra, ix, sm["rg_lambda"], name=f"rg_scan_fwd{i}")
            m = _nn(gact, weight("w_out_odd", 0, gact), f"out_odd{i}")
            sv.update(proj=proj, xc=xc, ra=ra, ix=ix, hs=hs, gact=gact, sm=sm)

        def post_mix(hh, mm, g_post, g_pre):
            h1 = hh + _rms(mm, g_post)
            return h1, _rms(h1, g_pre)

        h1, n2 = rowcall(f"post_mix{i}", post_mix, [h, m], [gain("mix_post_g", i), gain("ffn_pre_g", i)],
                          [F32, BF16], cols=d)
        gate, up, act = _gate_up(n2, weight("w_gate_up", i, n2), name=f"gate_up{i}")
        f = _nn(act, weight("w_down", i, act), f"down{i}")

        def post_ffn(hh, ff_out, g_post):
            h2 = hh + _rms(ff_out, g_post)
            return h2, h2

        h2, h2b = rowcall(f"post_ffn{i}", post_ffn, [h1, f], [gain("ffn_post_g", i)], [F32, BF16], cols=d)
        e = _nn(p[i], weight("w_ple_up", i, h2b), f"ple_up{i}")
        gl = _nn(h2b, weight("w_ple_gate", i, h2b), f"ple_gate{i}")
        h3, = rowcall(f"ple{i}", lambda hh, a, b, g: hh + _rms(_sigmoid(a) * b, g), [h2, gl, e],
                       [gain("ple_norm_g", i)], [F32], cols=d)
        sv.update(m=m, h1=h1, n2=n2, gate=gate, up=up, act=act, f=f, h2b=h2b, e=e, gl=gl)
        saved.append(sv)
        h = h3

    def loss_fn(y, t):
        err = y - t
        return err * (1.0 / d), jnp.sum(err * err, axis=0, keepdims=True) * (0.5 / d)

    dh, loss_cols = rowcall("loss", loss_fn, [h, target], [], [F32], red_rows=(1,), cols=d)

    for i in reversed(range(depth)):
        sv = saved[i]

        def ple_bwd(dy, a, b, g):
            _, vjp = jax.vjp(lambda a_, b_, g_: _rms(_sigmoid(a_) * b_, g_), a, b, g)
            return vjp(dy)

        dgl, de, rep_grads["ple_norm_g"][i] = rowcall(
            f"ple_bwd{i}", ple_bwd, [dh, sv["gl"], sv["e"]], [gain("ple_norm_g", i)], [BF16, BF16],
            red_rows=(1,), cols=d)
        send_small(i, "w_ple_up", i, p[i], de, f"d_ple_up{i}")
        send_small(i, "w_ple_gate", i, sv["h2b"], dgl, f"d_ple_gate{i}")
        dh2_ple = grad_in(dgl, weight("w_ple_gate", i, dgl), f"dx_ple_gate{i}")

        def post_ffn_bwd(dy, dx, ff_out, g):
            dh2 = dy + dx
            _, vjp = jax.vjp(_rms, ff_out, g)
            df, dg = vjp(dh2)
            return dh2, df, dg

        dh2, df, rep_grads["ffn_post_g"][i] = rowcall(
            f"post_ffn_bwd{i}", post_ffn_bwd, [dh, dh2_ple, sv["f"]], [gain("ffn_post_g", i)], [F32, BF16],
            red_rows=(1,), cols=d)
        send("w_down", i, _tn(sv["act"], df, f"d_down{i}"))
        dact = grad_in(df, weight("w_down", i, df), f"dx_down{i}", out_dtype=BF16)
        dgu = _swiglu_bwd(sv["gate"], sv["up"], dact, name=f"swiglu_bwd{i}")
        send("w_gate_up", i, _tn(sv["n2"], dgu, f"d_gate_up{i}"))
        dn2 = grad_in(dgu, weight("w_gate_up", i, dgu), f"dx_gate_up{i}")

        def post_mix_bwd(dy, dn, h1, mm, g_post, g_pre):
            _, vjp_pre = jax.vjp(_rms, h1, g_pre)
            dh1_n, dg_pre = vjp_pre(dn)
            dh1 = dy + dh1_n
            _, vjp_post = jax.vjp(_rms, mm, g_post)
            dm, dg_post = vjp_post(dh1)
            return dh1, dm, dg_pre, dg_post

        dh1, dm, rep_grads["ffn_pre_g"][i], rep_grads["mix_post_g"][i] = rowcall(
            f"post_mix_bwd{i}", post_mix_bwd, [dh2, dn2, sv["h1"], sv["m"]],
            [gain("mix_post_g", i), gain("ffn_pre_g", i)], [F32, BF16], red_rows=(1, 1), cols=d)

        if i % 2 == 0:
            send_small(i, "w_out_even", 0, sv["cat"], dm, f"d_out_even{i}")
            dcat = grad_in(dm, weight("w_out_even", 0, dm), f"dx_out_even{i}")
            dq, dk, dv = _sb_bwd(sv["proj"], dcat, n_heads, name=f"sb_bwd{i}")
            dhq, dhf, dhi, dhg, grads["hg_lb_logits"], grads["hg_norm_g"] = _hg_bwd(
                sv["proj"], rep["hg_lb_logits"], rep["hg_norm_g"], sv["oraw"], sv["states"], dcat, n_heads,
                name=f"hg_bwd{i}")
            dproj = jnp.concatenate([dq.astype(BF16), dk.astype(BF16), dv.astype(BF16), dhq, dhf, dhi, dhg], axis=1)
            send("w_in_even", 0, _tn(sv["n1"], dproj, f"d_in_even{i}"))
            dn1 = grad_in(dproj, weight("w_in_even", 0, dproj), f"dx_in_even{i}")
        else:
            sm = sv["sm"]
            send_small(i, "w_out_odd", 0, sv["gact"], dm, f"d_out_odd{i}")
            dgo = grad_in(dm, weight("w_out_odd", 0, dm), f"dx_out_odd{i}")
            dgate, dra, dix, dxc1, grads["rg_lambda"] = _rg_scan_bwd(
                dgo, sv["proj"], sv["hs"], sv["xc"], sv["ra"], sv["ix"], sm["rg_lambda"], name=f"rg_scan_bwd{i}")
            (dxb, grads["conv_w"], grads["conv_b"], grads["rg_wa"], grads["rg_ba"], grads["rg_wx"],
             grads["rg_bx"]) = _rg_gates_bwd(dra, dix, dxc1, sv["xc"], sv["proj"], sm["conv_w"], sm["rg_wa"],
                                            sm["rg_wx"], name=f"rg_gates_bwd{i}")
            send("small", 0, {k: grads.pop(k) for k in _SMALL})
            dproj = jnp.concatenate([dgate, dxb], axis=1)
            send("w_in_odd", 0, _tn(sv["n1"], dproj, f"d_in_odd{i}"))
            dn1 = grad_in(dproj, weight("w_in_odd", 0, dproj), f"dx_in_odd{i}")

        def pre_norm_bwd(dy, dn, hh, g):
            _, vjp = jax.vjp(_rms, hh, g)
            dx, dg = vjp(dn)
            return dy + dx, dg

        dh, rep_grads["mix_pre_g"][i] = rowcall(
            f"pre_norm_bwd{i}", pre_norm_bwd, [dh1, dn1, sv["h"]], [gain("mix_pre_g", i)], [F32],
            red_rows=(1,), cols=d)

    for name, layer, a, b, mm_name in deferred:
        send(name, layer, _tn(a, b, mm_name, deps=(dh,)))
    for k, rows in rep_grads.items():
        grads[k] = jnp.concatenate(rows, axis=0)
    return loss_cols, dh, grads


_WEIGHTS = ("mix_pre_g", "mix_post_g", "ffn_pre_g", "ffn_post_g", "ple_norm_g", "w_in_even", "w_out_even",
            "hg_lb_logits", "hg_norm_g", "w_in_odd", "conv_w", "conv_b", "rg_wa", "rg_ba", "rg_wx", "rg_bx",
            "rg_lambda", "w_out_odd", "w_gate_up", "w_down", "w_ple_up", "w_ple_gate")
_REPLICATED = ("mix_pre_g", "mix_post_g", "ffn_pre_g", "ffn_post_g", "ple_norm_g", "hg_lb_logits", "hg_norm_g")
_SMALL = ("conv_w", "conv_b", "rg_wa", "rg_ba", "rg_wx", "rg_bx", "rg_lambda")
_BIG = {"w_in_even": True, "w_out_even": False, "w_in_odd": True, "w_out_odd": False,
        "w_gate_up": True, "w_down": False, "w_ple_up": True, "w_ple_gate": False}
_PACK_ROW = SUBLANES * LANES


def _pack(arrays):
    flat = jnp.concatenate([a.reshape(-1) for a in arrays])
    pad = -flat.shape[0] % _PACK_ROW
    return jnp.pad(flat, (0, pad)).reshape(-1, LANES)


def _pack_blocks(arrays):
    flat = jnp.concatenate([a.reshape(N_DEV, -1) for a in arrays], axis=1)
    pad = -flat.shape[1] % _PACK_ROW
    return jnp.pad(flat, ((0, 0), (0, pad))).reshape(N_DEV, -1, LANES)


def _unpack(packed, shapes, lead=()):
    flat = packed.reshape(lead + (-1,))
    out, pos = [], 0
    for shape in shapes:
        n = math.prod(shape)
        out.append(flat[..., pos:pos + n].reshape(lead + tuple(shape)))
        pos += n
    return out


def _to_full_small(name, blocks):
    if name == "conv_w":
        return jnp.transpose(blocks, (1, 0, 2)).reshape(blocks.shape[1], -1)
    if name in ("conv_b", "rg_lambda"):
        return blocks.reshape(1, -1)
    nb = blocks.shape[1]
    if name in ("rg_wa", "rg_wx"):
        return jnp.transpose(blocks, (1, 0, 2, 3)).reshape(nb, RG_BLOCK, RG_BLOCK)
    return jnp.transpose(blocks, (1, 0, 2)).reshape(nb, 1, RG_BLOCK)


def _to_blocks_small(name, full):
    if name == "conv_w":
        return jnp.transpose(full.reshape(full.shape[0], N_DEV, -1), (1, 0, 2))
    if name in ("conv_b", "rg_lambda"):
        return full.reshape(N_DEV, -1)
    nb = full.shape[0]
    if name in ("rg_wa", "rg_wx"):
        return jnp.transpose(full.reshape(nb, N_DEV, RG_BLOCK // N_DEV, RG_BLOCK), (1, 0, 2, 3))
    return jnp.transpose(full.reshape(nb, N_DEV, RG_BLOCK // N_DEV), (1, 0, 2))


def _step(inp):
    w = {k: inp[k] for k in _WEIGHTS}
    x, p, target = inp["x"][0], inp["p"][:, 0], inp["loss_target"][0]
    assert w["hg_lb_logits"].shape[0] == 2 and w["w_in_even"].shape[0] == 1 and w["w_in_odd"].shape[0] == 1

    n_heads = w["w_in_even"].shape[2] * N_DEV // (7 * HEAD_DIM)
    small_shapes = [w[k].shape[1:] for k in _SMALL]

    def lands_in_place(name):
        return _BIG[name] and w[name].shape[2] % LANES == 0

    depth = p.shape[0]
    order = [("w_in_even", 0), ("w_out_even", 0)] if depth else []
    for i in range(depth):
        if i == 1:
            order += [("w_in_odd", 0), ("small", 0), ("w_out_odd", 0)]
        order += [("w_gate_up", i), ("w_down", i), ("w_ple_up", i), ("w_ple_gate", i)]
    heavy = ("w_gate_up", "w_down", "w_in_odd", "w_out_odd")
    gathers = {}
    first_started = 0.0
    last_token = ()
    for name, l in sorted(order, key=lambda key: key[0] in heavy):
        if name == "small":
            gathers[name, l] = _gather_start(_pack([w[k][0] for k in _SMALL]) + first_started, name="gather_small",
                                             relayed=False, after=last_token)
        else:
            gathers[name, l] = _gather_start((w[name][l] + first_started).astype(BF16), name=f"gather_{name}{l}",
                                             cols=lands_in_place(name), after=last_token)
        last_token = (gathers[name, l]["token"],)
        if len(gathers) == 1:
            first_started = last_token[0][0, 0]
    ready = {}

    def relay(key, after):
        if "cols" in gathers[key] and key[0] != "small":
            gathers[key] = _gather_relay(gathers[key], after, name=f"gather_{key[0]}{key[1]}")

    def weight(name, layer, after, relay_only=False):
        key = ("small", 0) if name in _SMALL else (name, layer)
        if relay_only:
            return relay(key, after)
        if key not in ready:
            relay(key, after)
            at = order.index(key)
            for nxt in order[at + 1:at + 2] if at else []:
                relay(nxt, after)
            land, = _spread_wait(gathers[key], after, name=f"gathered_{key[0]}{key[1]}")
            if name in _SMALL:
                ready[key] = {k: _to_full_small(k, b)
                              for k, b in zip(_SMALL, _unpack(land, small_shapes, lead=(N_DEV,)))}
            elif lands_in_place(name):
                ready[key] = land
            elif _BIG[name]:
                ready[key] = jnp.transpose(land, (1, 0, 2)).reshape(land.shape[1], -1)
            else:
                ready[key] = land.reshape(-1, land.shape[2])
        return ready[key][name] if name in _SMALL else ready[key]

    exchanges = []

    def emit(name, layer, g):
        after = tuple(h["token"] for _, _, h in exchanges[-1:])
        if name == "small":
            handle = _exchange_start([_pack_blocks([_to_blocks_small(k, g[k]) for k in _SMALL])],
                                     name="exchange_small", after=after)
        elif lands_in_place(name):
            handle = _exchange_start([g], name=f"exchange_{name}{layer}", cols=True, after=after)
        elif _BIG[name]:
            c = w[name].shape[2]
            handle = _exchange_start([jnp.transpose(g.reshape(-1, N_DEV, c), (1, 0, 2))],
                                     name=f"exchange_{name}{layer}", after=after)
        else:
            handle = _exchange_start([g.reshape((N_DEV,) + w[name].shape[1:])], name=f"exchange_{name}{layer}",
                                     after=after)
        exchanges.append((name, layer, handle))
        return handle["token"]

    rep = {k: w[k] for k in _REPLICATED}
    loss_cols, dx, grads = _local_step(x, p, target, rep, weight, emit, n_heads,
                                       [h["token"] for h in gathers.values()])

    loss_part = jnp.sum(loss_cols).reshape(1)
    rep_gather = _gather_start(_pack([grads[k] for k in _REPLICATED] + [loss_part]), name="gather_rep_grads",
                               relayed=False)

    out = {}
    after = exchanges[-1][2]["token"]
    for name, layer, handle in exchanges:
        land, = _spread_wait(handle, after, name=f"exchanged_{name}{layer}")
        if name == "small":
            res = _adamw(land.reshape(N_DEV, -1, LANES), *[_pack([inp[pre + k][0] for k in _SMALL]) for pre in ("", "m_", "v_")],
                         name="adamw_small")
            for k, *vals in zip(_SMALL, *[_unpack(a, small_shapes) for a in res]):
                out[k] = [v[None] for v in vals]
        else:
            n_l, r, c = w[name].shape
            res = out[name] = _adamw(land.reshape(N_DEV, r, c),
                                     *[inp[pre + name].reshape(n_l * r, c) for pre in ("", "m_", "v_")],
                                     name=f"adamw_{name}{layer}", layer=layer, prev=out.get(name))
        after = res[0]
    for name in _BIG:
        out[name] = [a.reshape(w[name].shape) for a in out[name]]

    rep_shapes = [w[k].shape for k in _REPLICATED] + [(1,)]
    rep_parts, = _spread_wait(rep_gather, after, name="gathered_rep_grads")
    res = _adamw(rep_parts, *[_pack([inp[pre + k] for k in _REPLICATED] + [jnp.zeros((1,), F32)])
                              for pre in ("", "m_", "v_")], name="adamw_rep")
    for k, *vals in zip(_REPLICATED + ("loss",), *[_unpack(a, rep_shapes) for a in res]):
        out[k] = vals
    loss = out["loss"][0][0]

    return (loss, dx[None]) + tuple(out[k][j] for j in range(4) for k in _WEIGHTS)


def kernel(x, p, mix_pre_g, mix_post_g, ffn_pre_g, ffn_post_g, ple_norm_g, w_in_even, w_out_even, hg_lb_logits, hg_norm_g, w_in_odd, conv_w, conv_b, rg_wa, rg_ba, rg_wx, rg_bx, rg_lambda, w_out_odd, w_gate_up, w_down, w_ple_up, w_ple_gate, loss_target, m_mix_pre_g, m_mix_post_g, m_ffn_pre_g, m_ffn_post_g, m_ple_norm_g, m_w_in_even, m_w_out_even, m_hg_lb_logits, m_hg_norm_g, m_w_in_odd, m_conv_w, m_conv_b, m_rg_wa, m_rg_ba, m_rg_wx, m_rg_bx, m_rg_lambda, m_w_out_odd, m_w_gate_up, m_w_down, m_w_ple_up, m_w_ple_gate, v_mix_pre_g, v_mix_post_g, v_ffn_pre_g, v_ffn_post_g, v_ple_norm_g, v_w_in_even, v_w_out_even, v_hg_lb_logits, v_hg_norm_g, v_w_in_odd, v_conv_w, v_conv_b, v_rg_wa, v_rg_ba, v_rg_wx, v_rg_bx, v_rg_lambda, v_w_out_odd, v_w_gate_up, v_w_down, v_w_ple_up, v_w_ple_gate):
    return _step(dict(locals()))
```

```python
import functools
import math

import jax
import jax.numpy as jnp
from jax import lax
from jax.experimental import pallas as pl
from jax.experimental.pallas import tpu as pltpu

F32 = jnp.float32
BF16 = jnp.bfloat16

VMEM_LIMIT_BYTES = 56 * 1024 * 1024
ADAMW_BLOCK_BYTES = 40 * 1024 * 1024
LANES = 128
SUBLANES = 8

N_DEV = 8
HEAD_DIM = 128
SB_Q_TILE = 512
SB_K_TILE = 256
HG_CHUNK = 32
HG_HEADS_PER_STEP = 2
RG_BLOCK = 256
CONV_TAPS = 4
RG_C = 8.0
RMS_EPS = 1e-6

ADAM_LR = 0.001
ADAM_B1 = 0.9
ADAM_B2 = 0.999
ADAM_EPS = 1e-08
ADAM_WD = 0.01
ADAM_STEP = 10


def _params(*sem):
    return pltpu.CompilerParams(dimension_semantics=sem, vmem_limit_bytes=VMEM_LIMIT_BYTES)


def _pick(n, cands):
    for c in cands:
        if c <= n and n % c == 0:
            return c
    return n


def _mm(a, b, mode, *, name, out_dtype=F32, tm=512, tn=512, tk=None, deps=()):
    if mode == "nn":
        (m, k), (k2, n) = a.shape, b.shape
    else:
        (m, k), (n, k2) = a.shape, b.shape
    assert k == k2, (a.shape, b.shape, mode)
    tm, tn = min(tm, m), min(tn, n)
    tk = k if tk is None else min(tk, k)
    assert m % tm == 0 and n % tn == 0 and k % tk == 0, (m, n, k, tm, tn, tk)
    nk = k // tk

    a_spec = pl.BlockSpec((tm, tk), lambda i, j, kk: (i, kk))
    if mode == "nn":
        b_spec = pl.BlockSpec((tk, tn), lambda i, j, kk: (kk, j))
        dims = (((1,), (0,)), ((), ()))
    else:
        b_spec = pl.BlockSpec((tn, tk), lambda i, j, kk: (j, kk))
        dims = (((1,), (1,)), ((), ()))

    def body(a_ref, b_ref, *refs):
        o_ref, *acc = refs[len(deps):]
        part = lax.dot_general(a_ref[...].astype(BF16), b_ref[...].astype(BF16), dims, preferred_element_type=F32)
        if nk == 1:
            o_ref[...] = part.astype(out_dtype)
        else:
            acc_ref, = acc
            kk = pl.program_id(2)

            @pl.when(kk == 0)
            def _():
                acc_ref[...] = part

            @pl.when(kk > 0)
            def _():
                acc_ref[...] += part

            @pl.when(kk == nk - 1)
            def _():
                o_ref[...] = acc_ref[...].astype(out_dtype)

    return pl.pallas_call(
        body, name=name,
        grid=(m // tm, n // tn, nk),
        in_specs=[a_spec, b_spec] + [pl.BlockSpec(memory_space=pl.ANY)] * len(deps),
        out_specs=pl.BlockSpec((tm, tn), lambda i, j, kk: (i, j)),
        out_shape=jax.ShapeDtypeStruct((m, n), out_dtype),
        scratch_shapes=[] if nk == 1 else [pltpu.VMEM((tm, tn), F32)],
        compiler_params=_params("parallel", "parallel", "arbitrary"),
    )(a, b, *deps)


def _mm_tn(a, b, *, name, out_dtype, tm, tn, deps=()):
    k, m = a.shape
    n = b.shape[1]

    def body(a_ref, b_ref, *refs):
        o_ref, at_ref = refs[len(deps):]

        @pl.when(pl.program_id(1) == 0)
        def _():
            at_ref[...] = a_ref[...].astype(F32).T.astype(BF16)

        o_ref[...] = jnp.dot(at_ref[...], b_ref[...].astype(BF16), preferred_element_type=F32).astype(out_dtype)

    return pl.pallas_call(
        body, name=name,
        grid=(m // tm, n // tn),
        in_specs=[pl.BlockSpec((k, tm), lambda i, j: (0, i)), pl.BlockSpec((k, tn), lambda i, j: (0, j))]
        + [pl.BlockSpec(memory_space=pl.ANY)] * len(deps),
        out_specs=pl.BlockSpec((tm, tn), lambda i, j: (i, j)),
        out_shape=jax.ShapeDtypeStruct((m, n), out_dtype),
        scratch_shapes=[pltpu.VMEM((tm, k), BF16)],
        compiler_params=_params("parallel", "arbitrary"),
    )(a, b, *deps)


def _rowcall(name, fn, rows, pars, row_outs, red_rows=(), *, cols, ts=256, tc=None, deps=()):
    rows = [r if isinstance(r, tuple) else (r, 0) for r in rows]
    pars = [p if isinstance(p, tuple) else (p, 0) for p in pars]
    s = rows[0][0].shape[0]
    tc = cols if tc is None else tc
    ts = min(ts, s)
    assert s % ts == 0 and cols % tc == 0, (name, s, ts, cols, tc)
    n_in, n_row_out = len(rows) + len(pars), len(row_outs)

    def body(*refs):
        outs = fn(*[r[...] for r in refs[:n_in]])
        outs = outs if isinstance(outs, (tuple, list)) else (outs,)
        o_refs = refs[n_in + len(deps):]
        for o_ref, val in zip(o_refs[:n_row_out], outs[:n_row_out]):
            o_ref[...] = val.astype(o_ref.dtype)
        first = pl.program_id(1) == 0
        for o_ref, val in zip(o_refs[n_row_out:], outs[n_row_out:]):
            @pl.when(first)
            def _(o_ref=o_ref, val=val):
                o_ref[...] = val

            @pl.when(jnp.logical_not(first))
            def _(o_ref=o_ref, val=val):
                o_ref[...] += val

    def row_map(off):
        return lambda j, i: (i, j + off)

    def par_map(off):
        return lambda j, i: (0, j + off)

    return pl.pallas_call(
        body, name=name,
        grid=(cols // tc, s // ts),
        in_specs=[pl.BlockSpec((ts, tc), row_map(off)) for _, off in rows]
        + [pl.BlockSpec((p.shape[0], tc), par_map(off)) for p, off in pars]
        + [pl.BlockSpec(memory_space=pl.ANY)] * len(deps),
        out_specs=[pl.BlockSpec((ts, tc), lambda j, i: (i, j)) for _ in row_outs]
        + [pl.BlockSpec((r, tc), lambda j, i: (0, j)) for r in red_rows],
        out_shape=[jax.ShapeDtypeStruct((s, cols), dt) for dt in row_outs]
        + [jax.ShapeDtypeStruct((r, cols), F32) for r in red_rows],
        compiler_params=_params("parallel", "arbitrary"),
    )(*[r for r, _ in rows], *[p for p, _ in pars], *deps)


def _swiglu_act(g, u):
    return _silu(g) * u


def _gate_up(n, w, *, name):
    s, d = n.shape
    f = w.shape[1] // 2
    tn = _pick(f, (256, 128))
    nj = f // tn

    def body(a_ref, wg_ref, wu_ref, g_ref, u_ref, act_ref):
        a = a_ref[...].astype(BF16)
        g = jnp.dot(a, wg_ref[...].astype(BF16), preferred_element_type=F32)
        u = jnp.dot(a, wu_ref[...].astype(BF16), preferred_element_type=F32)
        g_ref[...] = g.astype(BF16)
        u_ref[...] = u.astype(BF16)
        act_ref[...] = _swiglu_act(g, u).astype(BF16)

    out = pl.BlockSpec((s, tn), lambda j: (0, j))
    return pl.pallas_call(
        body, name=name,
        grid=(nj,),
        in_specs=[pl.BlockSpec((s, d), lambda j: (0, 0)), pl.BlockSpec((d, tn), lambda j: (0, j)),
                  pl.BlockSpec((d, tn), lambda j: (0, nj + j))],
        out_specs=[out] * 3,
        out_shape=[jax.ShapeDtypeStruct((s, f), BF16)] * 3,
        compiler_params=_params("parallel"),
    )(n, w, w)


def _swiglu_bwd(g, u, dact, *, name, ts=128):
    s, f = g.shape
    ts = min(ts, s)

    def body(g_ref, u_ref, dact_ref, o_ref):
        _, vjp = jax.vjp(_swiglu_act, g_ref[...].astype(F32), u_ref[...].astype(F32))
        dg, du = vjp(dact_ref[...].astype(F32))
        o_ref[:, 0:f] = dg.astype(BF16)
        o_ref[:, f:2 * f] = du.astype(BF16)

    narrow = pl.BlockSpec((ts, f), lambda i: (i, 0))
    return pl.pallas_call(
        body, name=name,
        grid=(s // ts,),
        in_specs=[narrow] * 3,
        out_specs=pl.BlockSpec((ts, 2 * f), lambda i: (i, 0)),
        out_shape=jax.ShapeDtypeStruct((s, 2 * f), BF16),
        compiler_params=_params("parallel"),
    )(g, u, dact)


def _rms(x, g):
    return x * lax.rsqrt(jnp.mean(x * x, axis=-1, keepdims=True) + RMS_EPS) * g


def _sigmoid(x):
    return jax.nn.sigmoid(x)


def _silu(x):
    return x * jax.nn.sigmoid(x)


def _gelu(x):
    return 0.5 * x * (1.0 + jnp.tanh(math.sqrt(2.0 / math.pi) * (x + 0.044715 * (x * x * x))))


def _softplus(x):
    return jnp.maximum(x, 0.0) + jnp.log1p(jnp.exp(-jnp.abs(x)))


def _split(x, terms):
    parts = []
    for _ in range(terms - 1):
        parts.append(x.astype(BF16))
        x = x - parts[-1].astype(F32)
    return parts + [x.astype(BF16)]


def _xdot(x, t, terms=3):
    return sum(jnp.dot(p, t, preferred_element_type=F32) for p in _split(x, terms))


def _xdot_l(t, x):
    return sum(jnp.dot(t, p, preferred_element_type=F32) for p in _split(x, 3))


_NT = (((1,), (1,)), ((), ()))
_TN = (((0,), (0,)), ((), ()))


def _dot(a, b, dims=None):
    if dims is None:
        return jnp.dot(a.astype(BF16), b.astype(BF16), preferred_element_type=F32)
    return lax.dot_general(a.astype(BF16), b.astype(BF16), dims, preferred_element_type=F32)


def _iota(shape, axis):
    return lax.broadcasted_iota(jnp.int32, shape, axis)


def _sb_tile(qb, kblk, mask, upper, c_rem):
    z = lax.dot_general(qb, kblk, _NT, preferred_element_type=F32)
    soft = jnp.log1p(jnp.exp(-jnp.abs(z)))
    lbeta = jnp.minimum(z, 0.0) - soft
    l1m = -jnp.maximum(z, 0.0) - soft
    if mask is not None:
        l1m = jnp.where(mask, l1m, 0.0)
    rem = _xdot(l1m, upper, terms=2) + c_rem
    w = jnp.exp(lbeta + rem)
    if mask is not None:
        w = jnp.where(mask, w, 0.0)
    return lbeta, l1m, w


def _sb_tiles(s):
    tq = min(SB_Q_TILE, s)
    return tq, SB_K_TILE, tq // SB_K_TILE


def _sb_key_loops(qi, per_q, step, carry):
    n_full = qi * per_q
    carry = lax.fori_loop(0, per_q, lambda j, c: step(n_full + per_q - 1 - j, True, c), carry)
    return lax.fori_loop(0, n_full, lambda j, c: step(n_full - 1 - j, False, c), carry)


def _sb_fwd(proj, n_heads, *, name):
    s = proj.shape[0]
    t, tk, per_q = _sb_tiles(s)
    scale = HEAD_DIM ** -0.5

    def body(q_ref, k_ref, v_ref, o_ref):
        qi = pl.program_id(1)
        qb = (q_ref[...] * scale).astype(BF16)
        row, col = _iota((t, tk), 0) + qi * t, _iota((t, tk), 1)
        upper = (_iota((tk, tk), 0) > _iota((tk, tk), 1)).astype(BF16)

        def step(kb, masked, carry):
            acc, c_rem = carry
            rows = pl.ds(pl.multiple_of(kb * tk, tk), tk)
            kblk = k_ref[rows, :].astype(BF16)
            vblk = v_ref[rows, :].astype(BF16)
            _, l1m, w = _sb_tile(qb, kblk, (col + kb * tk) < row if masked else None, upper, c_rem)
            acc = acc + jnp.dot(w.astype(BF16), vblk, preferred_element_type=F32)
            return acc, c_rem + jnp.sum(l1m, axis=1, keepdims=True)

        acc, _ = _sb_key_loops(qi, per_q, step, (jnp.zeros((t, HEAD_DIM), F32), jnp.zeros((t, 1), F32)))
        o_ref[...] = acc

    return pl.pallas_call(
        body, name=name,
        grid=(n_heads, s // t),
        in_specs=[pl.BlockSpec((t, HEAD_DIM), lambda h, i: (i, h)),
                  pl.BlockSpec((s, HEAD_DIM), lambda h, i: (0, n_heads + h)),
                  pl.BlockSpec((s, HEAD_DIM), lambda h, i: (0, 2 * n_heads + h))],
        out_specs=pl.BlockSpec((t, HEAD_DIM), lambda h, i: (i, h)),
        out_shape=jax.ShapeDtypeStruct((s, n_heads * HEAD_DIM), F32),
        compiler_params=_params("parallel", "arbitrary"),
    )(proj, proj, proj)


def _sb_bwd(proj, dcat, n_heads, *, name):
    s = proj.shape[0]
    t, tk, per_q = _sb_tiles(s)
    scale = HEAD_DIM ** -0.5

    def body(q_ref, k_ref, v_ref, do_ref, dq_ref, dk_ref, dv_ref, g_s, sig_s):
        qi = pl.program_id(1)

        @pl.when(qi == 0)
        def _():
            dk_ref[...] = jnp.zeros_like(dk_ref)
            dv_ref[...] = jnp.zeros_like(dv_ref)

        qb = (q_ref[...] * scale).astype(BF16)
        dob = do_ref[...].astype(BF16)
        row, col = _iota((t, tk), 0) + qi * t, _iota((t, tk), 1)
        upper = (_iota((tk, tk), 0) > _iota((tk, tk), 1)).astype(BF16)
        lower_incl = (_iota((tk, tk), 0) >= _iota((tk, tk), 1)).astype(BF16)

        def weights(kb, masked, carry):
            c_rem, g_all = carry
            rows = pl.ds(pl.multiple_of(kb * tk, tk), tk)
            kblk = k_ref[rows, :].astype(BF16)
            vblk = v_ref[rows, :].astype(BF16)
            lbeta, l1m, w = _sb_tile(qb, kblk, (col + kb * tk) < row if masked else None, upper, c_rem)
            g = w * lax.dot_general(dob, vblk, _NT, preferred_element_type=F32)
            dv_ref[rows, :] += lax.dot_general(w.astype(BF16), dob, _TN, preferred_element_type=F32)
            g_s[kb] = g
            sig_s[kb] = jnp.exp(lbeta)
            return c_rem + jnp.sum(l1m, axis=1, keepdims=True), g_all + jnp.sum(g, axis=1, keepdims=True)

        zero_col = jnp.zeros((t, 1), F32)
        _, g_all = _sb_key_loops(qi, per_q, weights, (zero_col, zero_col))

        def scores(kb, masked, carry):
            dq, c_g = carry
            rows = pl.ds(pl.multiple_of(kb * tk, tk), tk)
            g, sig = g_s[kb], sig_s[kb]
            g_before = g_all - (_xdot(g, lower_incl) + c_g)
            dz = g * (1.0 - sig) - g_before * sig
            if masked:
                dz = jnp.where((col + kb * tk) < row, dz, 0.0)
            dz = dz.astype(BF16)
            dq = dq + jnp.dot(dz, k_ref[rows, :].astype(BF16), preferred_element_type=F32)
            dk_ref[rows, :] += lax.dot_general(dz, qb, _TN, preferred_element_type=F32)
            return dq, c_g + jnp.sum(g, axis=1, keepdims=True)

        dq, _ = _sb_key_loops(qi, per_q, scores, (jnp.zeros((t, HEAD_DIM), F32), zero_col))
        dq_ref[...] = dq * scale

    width = n_heads * HEAD_DIM
    return pl.pallas_call(
        body, name=name,
        grid=(n_heads, s // t),
        in_specs=[pl.BlockSpec((t, HEAD_DIM), lambda h, i: (i, h)),
                  pl.BlockSpec((s, HEAD_DIM), lambda h, i: (0, n_heads + h)),
                  pl.BlockSpec((s, HEAD_DIM), lambda h, i: (0, 2 * n_heads + h)),
                  pl.BlockSpec((t, HEAD_DIM), lambda h, i: (i, h))],
        out_specs=[pl.BlockSpec((t, HEAD_DIM), lambda h, i: (i, h)),
                   pl.BlockSpec((s, HEAD_DIM), lambda h, i: (0, h)),
                   pl.BlockSpec((s, HEAD_DIM), lambda h, i: (0, h))],
        out_shape=[jax.ShapeDtypeStruct((s, width), F32)] * 3,
        scratch_shapes=[pltpu.VMEM((s // tk, t, tk), F32)] * 2,
        compiler_params=_params("parallel", "arbitrary"),
    )(proj, proj, proj, dcat)


def _hg_pre(hq, hf, logits):
    mx = jnp.max(logits, axis=0, keepdims=True)
    ex = jnp.exp(logits - mx)
    lb = ex[0:1, :] / jnp.sum(ex, axis=0, keepdims=True)
    f = lb + (1.0 - lb) * _sigmoid(hf)
    return _silu(hq), 1.0 - f, jnp.log(f)


def _hg_post(o, norm_g, hgate):
    return _rms(o, norm_g) * _silu(hgate)


def _hg_specs(s, n_heads, first_block):
    def at(group):
        return pl.BlockSpec((s, HEAD_DIM), lambda h: (0, first_block + group * n_heads + h))
    return [at(0), at(1), at(2), at(3)]


def _hg_fwd(proj, logits, norm_g, n_heads, *, name):
    s = proj.shape[0]
    hc = HG_CHUNK
    n_chunks = s // hc
    d = HEAD_DIM

    hp = HG_HEADS_PER_STEP
    assert n_heads % hp == 0
    wide = hp * d

    def body(lg_ref, ng_ref, hq_ref, hf_ref, hi_ref, hgt_ref, out_ref, oraw_ref, st_ref,
             q_s, k_s, lf_s, cum_s, qc_s, oc_s):
        q, k, lf = _hg_pre(hq_ref[...], hf_ref[...], lg_ref[...])
        q_s[...] = q
        k_s[...] = k
        lf_s[...] = lf
        tril = (_iota((hc, hc), 0) >= _iota((hc, hc), 1)).astype(BF16)
        srow = _iota((hc, d), 0)

        def head_chunk(j, ci, rows, st):
            ln = slice(j * d, (j + 1) * d)
            q, k, v = q_s[rows, ln], k_s[rows, ln], hi_ref[rows, ln]
            cum = _xdot_l(tril, lf_s[rows, ln])
            st_ref[j, ci] = st
            o_inter = _dot(q * jnp.exp(cum), st, _NT)
            cum_s[:, ln] = cum
            qc_s[:, ln] = q
            for t in range(hc):
                ng = (t // SUBLANES + 1) * SUBLANES
                e = jnp.where(srow[:ng] <= t, jnp.exp(cum_s[t:t + 1, ln] - cum[:ng]), 0.0)
                sc = jnp.sum(qc_s[t:t + 1, ln] * k[:ng] * e, axis=1, keepdims=True)
                oc_s[t:t + 1, ln] = jnp.sum(sc * v[:ng], axis=0, keepdims=True)
            oraw_ref[rows, ln] = o_inter + oc_s[:, ln]
            last = cum_s[hc - 1:hc, ln]
            return st * jnp.exp(last) + _dot(v, k * jnp.exp(last - cum), _TN)

        def chunk(ci, states):
            rows = pl.ds(pl.multiple_of(ci * hc, hc), hc)
            return tuple(head_chunk(j, ci, rows, st) for j, st in enumerate(states))

        lax.fori_loop(0, n_chunks, chunk, tuple(jnp.zeros((d, d), F32) for _ in range(hp)))
        for j in range(hp):
            ln = slice(j * d, (j + 1) * d)
            out_ref[:, ln] = _hg_post(oraw_ref[:, ln], ng_ref[...], hgt_ref[:, ln]).astype(BF16)

    width = n_heads * d
    first = 3 * n_heads // hp
    groups = [pl.BlockSpec((s, wide), functools.partial(lambda h, g: (0, first + g * (n_heads // hp) + h), g=g))
              for g in range(4)]
    head_block = pl.BlockSpec((s, wide), lambda h: (0, h))
    return pl.pallas_call(
        body, name=name,
        grid=(n_heads // hp,),
        in_specs=[pl.BlockSpec((2, wide), lambda h: (0, h)), pl.BlockSpec((1, d), lambda h: (0, 0))] + groups,
        out_specs=[head_block, head_block, pl.BlockSpec((hp, n_chunks, d, d), lambda h: (h, 0, 0, 0))],
        out_shape=[jax.ShapeDtypeStruct((s, width), BF16), jax.ShapeDtypeStruct((s, width), F32),
                   jax.ShapeDtypeStruct((n_heads, n_chunks, d, d), F32)],
        scratch_shapes=[pltpu.VMEM((s, wide), F32)] * 3 + [pltpu.VMEM((hc, wide), F32)] * 3,
        compiler_params=_params("arbitrary"),
    )(logits, norm_g, proj, proj, proj, proj)


def _hg_bwd(proj, logits, norm_g, oraw, states, dcat, n_heads, *, name):
    s = proj.shape[0]
    hc = HG_CHUNK
    n_chunks = s // hc
    d = HEAD_DIM

    def body(lg_ref, ng_ref, hq_ref, hf_ref, hi_ref, hgt_ref, oraw_ref, st_ref, dout_ref,
             dhq_ref, dhf_ref, dhi_ref, dhgt_ref, dlg_ref, dng_ref,
             q_s, k_s, lf_s, do_s, dq_s, dk_s, dlf_s, cum_s, qc_s, doc_s, dqc_s, dkc_s, dvc_s):
        head = pl.program_id(0)
        (q, k, lf), pre_vjp = jax.vjp(_hg_pre, hq_ref[...], hf_ref[...], lg_ref[...])
        q_s[...] = q
        k_s[...] = k
        lf_s[...] = lf
        _, post_vjp = jax.vjp(_hg_post, oraw_ref[...], ng_ref[...], hgt_ref[...])
        do, dng, dhgt = post_vjp(dout_ref[...])
        do_s[...] = do
        dhgt_ref[...] = dhgt.astype(BF16)

        @pl.when(head == 0)
        def _():
            dng_ref[...] = dng

        @pl.when(head > 0)
        def _():
            dng_ref[...] += dng

        triu = (_iota((hc, hc), 0) <= _iota((hc, hc), 1)).astype(BF16)
        tril = (_iota((hc, hc), 0) >= _iota((hc, hc), 1)).astype(BF16)
        srow = _iota((hc, d), 0)

        def chunk(j, dst):
            ci = n_chunks - 1 - j
            rows = pl.ds(pl.multiple_of(ci * hc, hc), hc)
            q, k, v, do_c = q_s[rows, :], k_s[rows, :], hi_ref[rows, :], do_s[rows, :]
            cum = _xdot_l(tril, lf_s[rows, :])
            st = st_ref[0, ci]
            cum_s[...] = cum
            qc_s[...] = q
            doc_s[...] = do_c
            last = cum_s[hc - 1:hc, :]
            e_cum, e_last = jnp.exp(cum), jnp.exp(last - cum)
            dqc_s[...] = _dot(do_c, st) * e_cum
            dk_state = _dot(v, dst) * e_last
            dkc_s[...] = dk_state
            dvc_s[...] = _dot(k * e_last, dst, _NT)
            d_last = (jnp.sum(dst * st, axis=0, keepdims=True) * jnp.exp(last)
                      + jnp.sum(k * dk_state, axis=0, keepdims=True))
            for t in range(hc):
                ng = (t // SUBLANES + 1) * SUBLANES
                qt, dot_ = qc_s[t:t + 1, :], doc_s[t:t + 1, :]
                e = jnp.where(srow[:ng] <= t, jnp.exp(cum_s[t:t + 1, :] - cum[:ng]), 0.0)
                ke = k[:ng] * e
                d_a = jnp.sum(dot_ * v[:ng], axis=1, keepdims=True)
                dqc_s[t:t + 1, :] += jnp.sum(d_a * ke, axis=0, keepdims=True)
                dkc_s[0:ng, :] += d_a * (qt * e)
                dvc_s[0:ng, :] += jnp.sum(qt * ke, axis=1, keepdims=True) * dot_
            dq, dk = dqc_s[...], dkc_s[...]
            d_b = q * dq - k * dk
            dq_s[rows, :] = dq
            dk_s[rows, :] = dk
            dhi_ref[rows, :] = dvc_s[...].astype(BF16)
            dlf_s[rows, :] = _xdot_l(triu, d_b) + d_last
            return dst * jnp.exp(last) + _dot(do_c, q * e_cum, _TN)

        lax.fori_loop(0, n_chunks, chunk, jnp.zeros((d, d), F32))
        dhq, dhf, dlg = pre_vjp((dq_s[...], dk_s[...], dlf_s[...]))
        dhq_ref[...] = dhq.astype(BF16)
        dhf_ref[...] = dhf.astype(BF16)
        dlg_ref[...] = dlg

    width = n_heads * d
    head_block = pl.BlockSpec((s, d), lambda h: (0, h))
    return pl.pallas_call(
        body, name=name,
        grid=(n_heads,),
        in_specs=[pl.BlockSpec((2, d), lambda h: (0, h)), pl.BlockSpec((1, d), lambda h: (0, 0))]
        + _hg_specs(s, n_heads, 3 * n_heads)
        + [head_block, pl.BlockSpec((1, n_chunks, d, d), lambda h: (h, 0, 0, 0)),
           pl.BlockSpec((s, d), lambda h: (0, n_heads + h))],
        out_specs=[head_block] * 4 + [pl.BlockSpec((2, d), lambda h: (0, h)), pl.BlockSpec((1, d), lambda h: (0, 0))],
        out_shape=[jax.ShapeDtypeStruct((s, width), BF16)] * 4
        + [jax.ShapeDtypeStruct((2, width), F32), jax.ShapeDtypeStruct((1, d), F32)],
        scratch_shapes=[pltpu.VMEM((s, d), F32)] * 7 + [pltpu.VMEM((hc, d), F32)] * 6,
        compiler_params=_params("arbitrary"),
    )(logits, norm_g, proj, proj, proj, proj, oraw, states, dcat)


def _shift_down(x, n, srow):
    if n == 0:
        return x
    return jnp.where(srow >= n, pltpu.roll(x, n, 0), 0.0)


def _shift_up(x, n, srow):
    if n == 0:
        return x
    s = x.shape[0]
    return jnp.where(srow < s - n, pltpu.roll(x, s - n, 0), 0.0)


def _rg_gates_fwd(proj, conv_w, conv_b, wa, ba, wx, bx, *, name):
    s = proj.shape[0]
    nb = wa.shape[0]
    bw = RG_BLOCK

    def body(xb_ref, cw_ref, cb_ref, wa_ref, ba_ref, wx_ref, bx_ref, xc_ref, ra_ref, ix_ref):
        x = xb_ref[...]
        srow = _iota((s, bw), 0)
        cw = cw_ref[...]
        xc = cb_ref[...] + cw[0:1, :] * x
        for tap in range(1, CONV_TAPS):
            xc = xc + cw[tap:tap + 1, :] * _shift_down(x, tap, srow)
        xc_ref[...] = xc
        ra_ref[...] = _dot(xc, wa_ref[0]) + ba_ref[0]
        ix_ref[...] = _dot(xc, wx_ref[0]) + bx_ref[0]

    col = pl.BlockSpec((s, bw), lambda n: (0, n))
    vec = lambda r: pl.BlockSpec((r, bw), lambda n: (0, n))
    mat = pl.BlockSpec((1, bw, bw), lambda n: (n, 0, 0))
    bias = pl.BlockSpec((1, 1, bw), lambda n: (n, 0, 0))
    return pl.pallas_call(
        body, name=name,
        grid=(nb,),
        in_specs=[pl.BlockSpec((s, bw), lambda n: (0, nb + n)), vec(CONV_TAPS), vec(1), mat, bias, mat, bias],
        out_specs=[col] * 3,
        out_shape=[jax.ShapeDtypeStruct((s, nb * bw), F32)] * 3,
        compiler_params=_params("parallel"),
    )(proj, conv_w, conv_b, wa, ba, wx, bx)


def _rg_au(ra, ix, xc, lam, first_row):
    log_a = -RG_C * _sigmoid(ra) * _softplus(-lam)
    th = jnp.tanh(log_a)
    one_minus_a2 = -2.0 * th / (1.0 - th)
    mult = jnp.where(first_row, 1.0, jnp.sqrt(one_minus_a2))
    return jnp.exp(log_a), xc * _sigmoid(ix) * mult


def _rg_out(gate, hs):
    return _gelu(gate) * hs


def _linear_scan(a, b, a_s, b_s, in_s, reverse):
    s, c = a.shape
    within = _iota((s, c), 0) & (SUBLANES - 1)
    shift = 1
    while shift < SUBLANES:
        if reverse:
            take = within < SUBLANES - shift
            a_n, b_n = pltpu.roll(a, s - shift, 0), pltpu.roll(b, s - shift, 0)
        else:
            take = within >= shift
            a_n, b_n = pltpu.roll(a, shift, 0), pltpu.roll(b, shift, 0)
        b = jnp.where(take, a * b_n + b, b)
        a = jnp.where(take, a * a_n, a)
        shift *= 2
    a_s[...] = a
    b_s[...] = b
    n_tiles = s // SUBLANES
    edge = 0 if reverse else SUBLANES - 1

    def tile(i, h):
        rows = pl.ds(pl.multiple_of(((n_tiles - 1 - i) if reverse else i) * SUBLANES, SUBLANES), SUBLANES)
        in_s[rows, :] = jnp.broadcast_to(h, (SUBLANES, c))
        return a_s[rows, :][edge:edge + 1, :] * h + b_s[rows, :][edge:edge + 1, :]

    lax.fori_loop(0, n_tiles, tile, jnp.zeros((1, c), F32))
    return a * in_s[...] + b


def _rg_scan_fwd(proj, xc, ra, ix, lam, *, name):
    s, width = xc.shape
    tc = LANES

    def body(gate_ref, xc_ref, ra_ref, ix_ref, lam_ref, hs_ref, gact_ref, a_s, u_s, in_s):
        first_row = _iota((s, tc), 0) == 0
        a, u = _rg_au(ra_ref[...], ix_ref[...], xc_ref[...], lam_ref[...], first_row)
        hs = _linear_scan(a, u, a_s, u_s, in_s, reverse=False)
        hs_ref[...] = hs
        gact_ref[...] = _rg_out(gate_ref[...], hs).astype(BF16)

    col = pl.BlockSpec((s, tc), lambda n: (0, n))
    return pl.pallas_call(
        body, name=name,
        grid=(width // tc,),
        in_specs=[col, col, col, col, pl.BlockSpec((1, tc), lambda n: (0, n))],
        out_specs=[col, col],
        out_shape=[jax.ShapeDtypeStruct((s, width), F32), jax.ShapeDtypeStruct((s, width), BF16)],
        scratch_shapes=[pltpu.VMEM((s, tc), F32)] * 3,
        compiler_params=_params("parallel"),
    )(proj, xc, ra, ix, lam)


def _rg_scan_bwd(dgo, proj, hs, xc, ra, ix, lam, *, name):
    s, width = xc.shape
    tc = LANES

    def body(dgo_ref, gate_ref, hs_ref, xc_ref, ra_ref, ix_ref, lam_ref,
             dgate_ref, dra_ref, dix_ref, dxc_ref, dlam_ref, a_s, dh_s, g_s):
        srow = _iota((s, tc), 0)
        hs = hs_ref[...]
        _, out_vjp = jax.vjp(_rg_out, gate_ref[...], hs)
        dgate, dh = out_vjp(dgo_ref[...])
        dgate_ref[...] = dgate.astype(BF16)
        au = functools.partial(_rg_au, first_row=srow == 0)
        (a, _), au_vjp = jax.vjp(au, ra_ref[...], ix_ref[...], xc_ref[...], lam_ref[...])
        g = _linear_scan(_shift_up(a, 1, srow), dh, a_s, dh_s, g_s, reverse=True)
        dra, dix, dxc, dlam = au_vjp((g * _shift_down(hs, 1, srow), g))
        dra_ref[...] = dra.astype(BF16)
        dix_ref[...] = dix.astype(BF16)
        dxc_ref[...] = dxc
        dlam_ref[...] = dlam

    col = pl.BlockSpec((s, tc), lambda n: (0, n))
    vec = pl.BlockSpec((1, tc), lambda n: (0, n))
    return pl.pallas_call(
        body, name=name,
        grid=(width // tc,),
        in_specs=[col] * 6 + [vec],
        out_specs=[col] * 4 + [vec],
        out_shape=[jax.ShapeDtypeStruct((s, width), BF16)] * 3
        + [jax.ShapeDtypeStruct((s, width), F32), jax.ShapeDtypeStruct((1, width), F32)],
        scratch_shapes=[pltpu.VMEM((s, tc), F32)] * 3,
        compiler_params=_params("parallel"),
    )(dgo, proj, hs, xc, ra, ix, lam)


def _rg_gates_bwd(dra, dix, dxc1, xc, proj, conv_w, wa, wx, *, name):
    s = proj.shape[0]
    nb = wa.shape[0]
    bw = RG_BLOCK

    def body(dra_ref, dix_ref, dxc_ref, xc_ref, xb_ref, cw_ref, wa_ref, wx_ref,
             dxb_ref, dcw_ref, dcb_ref, dwa_ref, dba_ref, dwx_ref, dbx_ref):
        dra, dix = dra_ref[...], dix_ref[...]
        xc_t = xc_ref[...].T.astype(BF16)
        dwa_ref[0] = jnp.dot(xc_t, dra, preferred_element_type=F32)
        dwx_ref[0] = jnp.dot(xc_t, dix, preferred_element_type=F32)
        dba_ref[0] = jnp.sum(dra.astype(F32), axis=0, keepdims=True)
        dbx_ref[0] = jnp.sum(dix.astype(F32), axis=0, keepdims=True)
        dxc = dxc_ref[...] + _dot(dra, wa_ref[0], _NT) + _dot(dix, wx_ref[0], _NT)
        srow = _iota((s, bw), 0)
        x = xb_ref[...]
        cw = cw_ref[...]
        dx = cw[0:1, :] * dxc
        dcw = [jnp.sum(dxc * x, axis=0, keepdims=True)]
        for tap in range(1, CONV_TAPS):
            dx = dx + cw[tap:tap + 1, :] * _shift_up(dxc, tap, srow)
            dcw.append(jnp.sum(dxc * _shift_down(x, tap, srow), axis=0, keepdims=True))
        dxb_ref[...] = dx.astype(BF16)
        r4 = _iota((CONV_TAPS, bw), 0)
        acc = jnp.zeros((CONV_TAPS, bw), F32)
        for tap in range(CONV_TAPS):
            acc = jnp.where(r4 == tap, dcw[tap], acc)
        dcw_ref[...] = acc
        dcb_ref[...] = jnp.sum(dxc, axis=0, keepdims=True)

    col = pl.BlockSpec((s, bw), lambda n: (0, n))
    vec = lambda r: pl.BlockSpec((r, bw), lambda n: (0, n))
    mat = pl.BlockSpec((1, bw, bw), lambda n: (n, 0, 0))
    bias = pl.BlockSpec((1, 1, bw), lambda n: (n, 0, 0))
    width = nb * bw
    return pl.pallas_call(
        body, name=name,
        grid=(nb,),
        in_specs=[col, col, col, col, pl.BlockSpec((s, bw), lambda n: (0, nb + n)), vec(CONV_TAPS), mat, mat],
        out_specs=[col, vec(CONV_TAPS), vec(1), mat, bias, mat, bias],
        out_shape=[jax.ShapeDtypeStruct((s, width), BF16), jax.ShapeDtypeStruct((CONV_TAPS, width), F32),
                   jax.ShapeDtypeStruct((1, width), F32), jax.ShapeDtypeStruct((nb, bw, bw), F32),
                   jax.ShapeDtypeStruct((nb, 1, bw), F32), jax.ShapeDtypeStruct((nb, bw, bw), F32),
                   jax.ShapeDtypeStruct((nb, 1, bw), F32)],
        compiler_params=_params("parallel"),
    )(dra, dix, dxc1, xc, proj, conv_w, wa, wx)


_HBM = pl.BlockSpec(memory_space=pltpu.HBM)
_FLIPS = ((0, 0, 1), (1, 0, 0), (0, 1, 0), (1, 1, 0))
_ALL_FLIPS = tuple((a, b, c) for a in (0, 1) for b in (0, 1) for c in (0, 1))[1:]


def _flip(pos, f):
    return tuple(1 - p if b else p for p, b in zip(pos, f))


def _dev_index(pos):
    return 4 * pos[0] + 2 * pos[1] + pos[2]


def _block(ref, idx, cols):
    if not cols:
        return ref.at[idx]
    n = ref.shape[-1] // N_DEV
    start = pl.multiple_of(idx * n, LANES)
    return ref.at[(slice(None),) * (len(ref.shape) - 1) + (pl.ds(start, n),)]


_SEM = pl.BlockSpec(memory_space=pltpu.SEMAPHORE)
_ANY = pl.BlockSpec(memory_space=pl.ANY)
_N_PEERS = N_DEV - 1


def _hbm(x):
    return pltpu.with_memory_space_constraint(x, pltpu.HBM)


def _me():
    return lax.axis_index("x"), lax.axis_index("y"), lax.axis_index("c")


def _spread_copies(plan, src_refs, land_refs, send_sems, recv_sems, local_sems):
    local, remote = plan(src_refs, land_refs)
    local = [pltpu.make_async_copy(src, dst, local_sems.at[i]) for i, (src, dst) in enumerate(local)]
    remote = [pltpu.make_async_remote_copy(src_ref=src, dst_ref=dst, send_sem=send_sems.at[k], recv_sem=recv_sems.at[k],
                                           device_id=peer, device_id_type=pl.DeviceIdType.MESH)
              for k, (src, dst, peer) in enumerate(remote)]
    return local, remote


def _spread_start(srcs, lands, plan, n_remote, n_local, *, name, after=()):
    ns, nl = len(srcs), len(lands)
    n_in = ns + nl + len(after)

    def body(*refs):
        src_refs, land_refs = refs[:ns], refs[ns:ns + nl]
        send_sems, recv_sems, local_sems = refs[n_in:n_in + 3]
        local, remote = _spread_copies(plan, src_refs, land_refs, send_sems, recv_sems, local_sems)
        for cp in local + remote:
            cp.start()
        token = refs[-1]
        token[...] = jnp.zeros_like(token)

    lands = [_hbm(lax.empty(*x)) if isinstance(x, tuple) else x for x in lands]
    out = pl.pallas_call(
        body, name=name,
        in_specs=[_HBM] * (ns + nl) + [_ANY] * len(after),
        out_specs=[_SEM] * 3 + [_HBM] * (ns + nl) + [pl.BlockSpec(memory_space=pltpu.VMEM)],
        out_shape=[pltpu.SemaphoreType.DMA((n_remote,)), pltpu.SemaphoreType.DMA((n_remote,)),
                   pltpu.SemaphoreType.DMA((max(n_local, 1),))]
        + [pltpu.HBM(x.shape, x.dtype) for x in list(srcs) + lands]
        + [jax.ShapeDtypeStruct((SUBLANES, LANES), F32)],
        input_output_aliases={i: 3 + i for i in range(ns + nl)},
        compiler_params=pltpu.CompilerParams(has_side_effects=pltpu.SideEffectType.DATAFLOW_SIDE_EFFECTING),
    )(*[_hbm(x) for x in srcs], *lands, *after)
    return dict(sems=list(out[:3]), srcs=list(out[3:3 + ns]), lands=list(out[3 + ns:3 + ns + nl]),
                token=out[-1], plan=plan)


def _spread_wait(handle, after, *, name):
    ns, nl = len(handle["srcs"]), len(handle["lands"])

    def body(*refs):
        src_refs, land_refs = refs[:ns], refs[ns:ns + nl]
        send_sems, recv_sems, local_sems = refs[ns + nl:ns + nl + 3]
        local, remote = _spread_copies(handle["plan"], src_refs, land_refs, send_sems, recv_sems, local_sems)
        for cp in local:
            cp.wait()
        for cp in remote:
            cp.wait_send()
            cp.wait_recv()

    out = pl.pallas_call(
        body, name=name,
        in_specs=[_HBM] * (ns + nl) + [_SEM] * 3 + [_ANY],
        out_specs=[_HBM] * (ns + nl),
        out_shape=[pltpu.HBM(x.shape, x.dtype) for x in handle["srcs"] + handle["lands"]],
        input_output_aliases={i: i for i in range(ns + nl)},
        compiler_params=pltpu.CompilerParams(has_side_effects=pltpu.SideEffectType.DATAFLOW_SIDE_EFFECTING),
    )(*handle["srcs"], *handle["lands"], *handle["sems"], after)
    return list(out[ns:])


def _gather_start(x, *, name, cols=False, relayed=True, after=()):
    shape = x.shape[:-1] + (N_DEV * x.shape[-1],) if cols else (N_DEV,) + x.shape
    flips = _FLIPS if relayed else _ALL_FLIPS

    def plan(src_refs, land_refs):
        me = _me()
        mine = _block(land_refs[0], _dev_index(me), cols)
        return [(src_refs[0], mine)], [(src_refs[0], mine, _flip(me, f)) for f in flips]

    handle = _spread_start([x], [(shape, x.dtype)], plan, len(flips), 1, name=name, after=after)
    handle["cols"] = cols
    return handle


def _gather_relay(handle, after, *, name):
    cols = handle["cols"]
    land, = _spread_wait(handle, after, name=f"{name}_arrived")

    def plan(src_refs, land_refs):
        me = _me()
        blocks = [_block(land_refs[0], _dev_index(_flip(me, f)), cols) for f in _FLIPS[1:]]
        return [], [(blk, blk, _flip(me, _FLIPS[0])) for blk in blocks]

    return _spread_start([], [land], plan, len(_FLIPS) - 1, 0, name=f"{name}_pass")


def _exchange_start(ps, *, name, cols=False, after=()):
    blk = ps[0].shape[:-1] + (ps[0].shape[-1] // N_DEV,) if cols else ps[0].shape[1:]

    def plan(src_refs, land_refs):
        me = _me()
        me_idx = _dev_index(me)
        local = [(_block(src, me_idx, cols), land_refs[0].at[me_idx, a]) for a, src in enumerate(src_refs)]
        remote = [(_block(src, _dev_index(_flip(me, f)), cols), land_refs[0].at[me_idx, a], _flip(me, f))
                  for f in _ALL_FLIPS for a, src in enumerate(src_refs)]
        return local, remote

    return _spread_start(ps, [((N_DEV, len(ps)) + blk, ps[0].dtype)], plan, _N_PEERS * len(ps), len(ps), name=name,
                         after=after)


def _adamw(parts, w, m, v, *, name, layer=0, prev=None):
    n_rows, c = w.shape
    r = parts.shape[1]
    row_bytes = c * (N_DEV * parts.dtype.itemsize + 7 * 4) * 2
    tr = r
    for cand in (512, 256, 128, 64, 32, 16):
        if r % cand == 0 and cand * row_bytes <= ADAMW_BLOCK_BYTES:
            tr = cand
            break
    c1 = 1.0 - ADAM_B1 ** ADAM_STEP
    c2 = 1.0 - ADAM_B2 ** ADAM_STEP

    def body(p_ref, w_ref, m_ref, v_ref, *rest):
        g_ref, d_ref, nm_ref, nv_ref = rest[-4:]
        g = p_ref[0].astype(F32)
        for j in range(1, N_DEV):
            g = g + p_ref[j].astype(F32)
        nm = ADAM_B1 * m_ref[...] + (1.0 - ADAM_B1) * g
        nv = ADAM_B2 * v_ref[...] + (1.0 - ADAM_B2) * (g * g)
        g_ref[...] = g
        nm_ref[...] = nm
        nv_ref[...] = nv
        d_ref[...] = -ADAM_LR * ((nm * (1.0 / c1)) / (jnp.sqrt(nv * (1.0 / c2)) + ADAM_EPS) + ADAM_WD * w_ref[...])

    off = layer * (r // tr)
    blk = pl.BlockSpec((tr, c), lambda i: (i + off, 0))
    prev = list(prev) if prev is not None else []
    return pl.pallas_call(
        body, name=name,
        grid=(r // tr,),
        in_specs=[pl.BlockSpec((N_DEV, tr, c), lambda i: (0, i, 0)), blk, blk, blk] + [_ANY] * len(prev),
        out_specs=[blk] * 4,
        out_shape=[jax.ShapeDtypeStruct((n_rows, c), F32)] * 4,
        input_output_aliases={4 + j: j for j in range(len(prev))},
        compiler_params=_params("parallel"),
    )(parts, w, m, v, *prev)


_TN_CANDS = (512, 256, 128)
_TK_MAX = 5632
_TK_WHOLE_ROWS = 2816


def _contraction_tiles(m, k):
    tk = k
    while tk > _TK_MAX and tk % 2 == 0 and (tk // 2) % LANES == 0:
        tk //= 2
    tm = m if tk <= _TK_WHOLE_ROWS or m % 2 else m // 2
    return tm, tk


def _nn(a, b, name, out_dtype=F32):
    tm, tk = _contraction_tiles(*a.shape)
    return _mm(a, b, "nn", name=name, out_dtype=out_dtype, tm=tm, tn=_pick(b.shape[1], _TN_CANDS), tk=tk)


def _nt(a, b, name, out_dtype=F32, deps=()):
    tm, tk = _contraction_tiles(*a.shape)
    return _mm(a, b, "nt", name=name, out_dtype=out_dtype, tm=tm, tn=_pick(b.shape[0], _TN_CANDS), tk=tk,
               deps=deps)


def _tn(a, b, name, out_dtype=BF16, deps=()):
    assert a.shape[0] == b.shape[0], (a.shape, b.shape)
    return _mm_tn(a, b, name=name, out_dtype=out_dtype, tm=_pick(a.shape[1], _TN_CANDS),
                  tn=_pick(b.shape[1], (1024,) + _TN_CANDS), deps=deps)


def _local_step(x, p, target, rep, weight, emit, n_heads, start_tokens=()):
    s, d = x.shape
    depth = p.shape[0]
    grads = {}
    rep_grads = {k: [None] * depth for k in ("mix_pre_g", "mix_post_g", "ffn_pre_g", "ffn_post_g", "ple_norm_g")}

    pending = list(start_tokens)
    gains = {}

    def gain(name, i):
        if (name, i) not in gains:
            gains[name, i] = rep[name][i:i + 1]
        return gains[name, i]

    def send(name, layer, g):
        token = emit(name, layer, g)
        if token is not None:
            pending.append(token)

    def rowcall(*args, **kwargs):
        deps, pending[:] = tuple(pending), []
        return _rowcall(*args, deps=deps, **kwargs)

    def grad_in(*args, **kwargs):
        deps, pending[:] = tuple(pending), []
        return _nt(*args, deps=deps, **kwargs)

    deferred = []

    def send_small(i, name, layer, a, b, mm_name):
        if i == 0 and depth > 1:
            deferred.append((name, layer, a, b, mm_name))
        else:
            send(name, layer, _tn(a, b, mm_name))

    saved = []
    h = x
    for i in range(depth):
        sv = {"h": h}
        n1, = rowcall(f"pre_norm{i}", lambda hh, g: _rms(hh, g), [h], [gain("mix_pre_g", i)], [BF16], cols=d)
        sv["n1"] = n1
        if i % 2 == 0:
            proj = _nn(n1, weight("w_in_even", 0, n1), f"in_even{i}")
            a_out = _sb_fwd(proj, n_heads, name=f"sb_fwd{i}")
            b_out, oraw, states = _hg_fwd(proj, rep["hg_lb_logits"], rep["hg_norm_g"], n_heads, name=f"hg_fwd{i}")
            cat = jnp.concatenate([a_out.astype(BF16), b_out], axis=1)
            m = _nn(cat, weight("w_out_even", 0, cat), f"out_even{i}")
            sv.update(proj=proj, oraw=oraw, states=states, cat=cat)
        else:
            proj = _nn(n1, weight("w_in_odd", 0, n1), f"in_odd{i}")
            sm = {k: weight(k, 0, proj) for k in _SMALL}
            xc, ra, ix = _rg_gates_fwd(proj, sm["conv_w"], sm["conv_b"], sm["rg_wa"], sm["rg_ba"],
                                       sm["rg_wx"], sm["rg_bx"], name=f"rg_gates_fwd{i}")
            hs, gact = _rg_scan_fwd(proj, xc, ra, ix, sm["rg_lambda"], name=f"rg_scan_fwd{i}")
            m = _nn(gact, weight("w_out_odd", 0, gact), f"out_odd{i}")
            sv.update(proj=proj, xc=xc, ra=ra, ix=ix, hs=hs, gact=gact, sm=sm)

        def post_mix(hh, mm, g_post, g_pre):
            h1 = hh + _rms(mm, g_post)
            return h1, _rms(h1, g_pre)

        h1, n2 = rowcall(f"post_mix{i}", post_mix, [h, m], [gain("mix_post_g", i), gain("ffn_pre_g", i)],
                          [F32, BF16], cols=d)
        gate, up, act = _gate_up(n2, weight("w_gate_up", i, n2), name=f"gate_up{i}")
        f = _nn(act, weight("w_down", i, act), f"down{i}")

        def post_ffn(hh, ff_out, g_post):
            h2 = hh + _rms(ff_out, g_post)
            return h2, h2

        h2, h2b = rowcall(f"post_ffn{i}", post_ffn, [h1, f], [gain("ffn_post_g", i)], [F32, BF16], cols=d)
        e = _nn(p[i], weight("w_ple_up", i, h2b), f"ple_up{i}")
        gl = _nn(h2b, weight("w_ple_gate", i, h2b), f"ple_gate{i}")
        h3, = rowcall(f"ple{i}", lambda hh, a, b, g: hh + _rms(_sigmoid(a) * b, g), [h2, gl, e],
                       [gain("ple_norm_g", i)], [F32], cols=d)
        sv.update(m=m, h1=h1, n2=n2, gate=gate, up=up, act=act, f=f, h2b=h2b, e=e, gl=gl)
        saved.append(sv)
        h = h3

    def loss_fn(y, t):
        err = y - t
        return err * (1.0 / d), jnp.sum(err * err, axis=0, keepdims=True) * (0.5 / d)

    dh, loss_cols = rowcall("loss", loss_fn, [h, target], [], [F32], red_rows=(1,), cols=d)

    for i in reversed(range(depth)):
        sv = saved[i]

        def ple_bwd(dy, a, b, g):
            _, vjp = jax.vjp(lambda a_, b_, g_: _rms(_sigmoid(a_) * b_, g_), a, b, g)
            return vjp(dy)

        dgl, de, rep_grads["ple_norm_g"][i] = rowcall(
            f"ple_bwd{i}", ple_bwd, [dh, sv["gl"], sv["e"]], [gain("ple_norm_g", i)], [BF16, BF16],
            red_rows=(1,), cols=d)
        send_small(i, "w_ple_up", i, p[i], de, f"d_ple_up{i}")
        send_small(i, "w_ple_gate", i, sv["h2b"], dgl, f"d_ple_gate{i}")
        dh2_ple = grad_in(dgl, weight("w_ple_gate", i, dgl), f"dx_ple_gate{i}")

        def post_ffn_bwd(dy, dx, ff_out, g):
            dh2 = dy + dx
            _, vjp = jax.vjp(_rms, ff_out, g)
            df, dg = vjp(dh2)
            return dh2, df, dg

        dh2, df, rep_grads["ffn_post_g"][i] = rowcall(
            f"post_ffn_bwd{i}", post_ffn_bwd, [dh, dh2_ple, sv["f"]], [gain("ffn_post_g", i)], [F32, BF16],
            red_rows=(1,), cols=d)
        send("w_down", i, _tn(sv["act"], df, f"d_down{i}"))
        dact = grad_in(df, weight("w_down", i, df), f"dx_down{i}", out_dtype=BF16)
        dgu = _swiglu_bwd(sv["gate"], sv["up"], dact, name=f"swiglu_bwd{i}")
        send("w_gate_up", i, _tn(sv["n2"], dgu, f"d_gate_up{i}"))
        dn2 = grad_in(dgu, weight("w_gate_up", i, dgu), f"dx_gate_up{i}")

        def post_mix_bwd(dy, dn, h1, mm, g_post, g_pre):
            _, vjp_pre = jax.vjp(_rms, h1, g_pre)
            dh1_n, dg_pre = vjp_pre(dn)
            dh1 = dy + dh1_n
            _, vjp_post = jax.vjp(_rms, mm, g_post)
            dm, dg_post = vjp_post(dh1)
            return dh1, dm, dg_pre, dg_post

        dh1, dm, rep_grads["ffn_pre_g"][i], rep_grads["mix_post_g"][i] = rowcall(
            f"post_mix_bwd{i}", post_mix_bwd, [dh2, dn2, sv["h1"], sv["m"]],
            [gain("mix_post_g", i), gain("ffn_pre_g", i)], [F32, BF16], red_rows=(1, 1), cols=d)

        if i % 2 == 0:
            send_small(i, "w_out_even", 0, sv["cat"], dm, f"d_out_even{i}")
            dcat = grad_in(dm, weight("w_out_even", 0, dm), f"dx_out_even{i}")
            dq, dk, dv = _sb_bwd(sv["proj"], dcat, n_heads, name=f"sb_bwd{i}")
            dhq, dhf, dhi, dhg, grads["hg_lb_logits"], grads["hg_norm_g"] = _hg_bwd(
                sv["proj"], rep["hg_lb_logits"], rep["hg_norm_g"], sv["oraw"], sv["states"], dcat, n_heads,
                name=f"hg_bwd{i}")
            dproj = jnp.concatenate([dq.astype(BF16), dk.astype(BF16), dv.astype(BF16), dhq, dhf, dhi, dhg], axis=1)
            send("w_in_even", 0, _tn(sv["n1"], dproj, f"d_in_even{i}"))
            dn1 = grad_in(dproj, weight("w_in_even", 0, dproj), f"dx_in_even{i}")
        else:
            sm = sv["sm"]
            send_small(i, "w_out_odd", 0, sv["gact"], dm, f"d_out_odd{i}")
            dgo = grad_in(dm, weight("w_out_odd", 0, dm), f"dx_out_odd{i}")
            dgate, dra, dix, dxc1, grads["rg_lambda"] = _rg_scan_bwd(
                dgo, sv["proj"], sv["hs"], sv["xc"], sv["ra"], sv["ix"], sm["rg_lambda"], name=f"rg_scan_bwd{i}")
            (dxb, grads["conv_w"], grads["conv_b"], grads["rg_wa"], grads["rg_ba"], grads["rg_wx"],
             grads["rg_bx"]) = _rg_gates_bwd(dra, dix, dxc1, sv["xc"], sv["proj"], sm["conv_w"], sm["rg_wa"],
                                            sm["rg_wx"], name=f"rg_gates_bwd{i}")
            send("small", 0, {k: grads.pop(k) for k in _SMALL})
            dproj = jnp.concatenate([dgate, dxb], axis=1)
            send("w_in_odd", 0, _tn(sv["n1"], dproj, f"d_in_odd{i}"))
            dn1 = grad_in(dproj, weight("w_in_odd", 0, dproj), f"dx_in_odd{i}")

        def pre_norm_bwd(dy, dn, hh, g):
            _, vjp = jax.vjp(_rms, hh, g)
            dx, dg = vjp(dn)
            return dy + dx, dg

        dh, rep_grads["mix_pre_g"][i] = rowcall(
            f"pre_norm_bwd{i}", pre_norm_bwd, [dh1, dn1, sv["h"]], [gain("mix_pre_g", i)], [F32],
            red_rows=(1,), cols=d)

    for name, layer, a, b, mm_name in deferred:
        send(name, layer, _tn(a, b, mm_name, deps=(dh,)))
    for k, rows in rep_grads.items():
        grads[k] = jnp.concatenate(rows, axis=0)
    return loss_cols, dh, grads


_WEIGHTS = ("mix_pre_g", "mix_post_g", "ffn_pre_g", "ffn_post_g", "ple_norm_g", "w_in_even", "w_out_even",
            "hg_lb_logits", "hg_norm_g", "w_in_odd", "conv_w", "conv_b", "rg_wa", "rg_ba", "rg_wx", "rg_bx",
            "rg_lambda", "w_out_odd", "w_gate_up", "w_down", "w_ple_up", "w_ple_gate")
_REPLICATED = ("mix_pre_g", "mix_post_g", "ffn_pre_g", "ffn_post_g", "ple_norm_g", "hg_lb_logits", "hg_norm_g")
_SMALL = ("conv_w", "conv_b", "rg_wa", "rg_ba", "rg_wx", "rg_bx", "rg_lambda")
_BIG = {"w_in_even": True, "w_out_even": False, "w_in_odd": True, "w_out_odd": False,
        "w_gate_up": True, "w_down": False, "w_ple_up": True, "w_ple_gate": False}
_PACK_ROW = SUBLANES * LANES


def _pack(arrays):
    flat = jnp.concatenate([a.reshape(-1) for a in arrays])
    pad = -flat.shape[0] % _PACK_ROW
    return jnp.pad(flat, (0, pad)).reshape(-1, LANES)


def _pack_blocks(arrays):
    flat = jnp.concatenate([a.reshape(N_DEV, -1) for a in arrays], axis=1)
    pad = -flat.shape[1] % _PACK_ROW
    return jnp.pad(flat, ((0, 0), (0, pad))).reshape(N_DEV, -1, LANES)


def _unpack(packed, shapes, lead=()):
    flat = packed.reshape(lead + (-1,))
    out, pos = [], 0
    for shape in shapes:
        n = math.prod(shape)
        out.append(flat[..., pos:pos + n].reshape(lead + tuple(shape)))
        pos += n
    return out


def _to_full_small(name, blocks):
    if name == "conv_w":
        return jnp.transpose(blocks, (1, 0, 2)).reshape(blocks.shape[1], -1)
    if name in ("conv_b", "rg_lambda"):
        return blocks.reshape(1, -1)
    nb = blocks.shape[1]
    if name in ("rg_wa", "rg_wx"):
        return jnp.transpose(blocks, (1, 0, 2, 3)).reshape(nb, RG_BLOCK, RG_BLOCK)
    return jnp.transpose(blocks, (1, 0, 2)).reshape(nb, 1, RG_BLOCK)


def _to_blocks_small(name, full):
    if name == "conv_w":
        return jnp.transpose(full.reshape(full.shape[0], N_DEV, -1), (1, 0, 2))
    if name in ("conv_b", "rg_lambda"):
        return full.reshape(N_DEV, -1)
    nb = full.shape[0]
    if name in ("rg_wa", "rg_wx"):
        return jnp.transpose(full.reshape(nb, N_DEV, RG_BLOCK // N_DEV, RG_BLOCK), (1, 0, 2, 3))
    return jnp.transpose(full.reshape(nb, N_DEV, RG_BLOCK // N_DEV), (1, 0, 2))


def _step(inp):
    w = {k: inp[k] for k in _WEIGHTS}
    x, p, target = inp["x"][0], inp["p"][:, 0], inp["loss_target"][0]
    assert w["hg_lb_logits"].shape[0] == 2 and w["w_in_even"].shape[0] == 1 and w["w_in_odd"].shape[0] == 1

    n_heads = w["w_in_even"].shape[2] * N_DEV // (7 * HEAD_DIM)
    small_shapes = [w[k].shape[1:] for k in _SMALL]

    def lands_in_place(name):
        return _BIG[name] and w[name].shape[2] % LANES == 0

    depth = p.shape[0]
    order = [("w_in_even", 0), ("w_out_even", 0)] if depth else []
    for i in range(depth):
        if i == 1:
            order += [("w_in_odd", 0), ("small", 0), ("w_out_odd", 0)]
        order += [("w_gate_up", i), ("w_down", i), ("w_ple_up", i), ("w_ple_gate", i)]
    heavy = ("w_gate_up", "w_down", "w_in_odd", "w_out_odd")
    gathers = {}
    first_started = 0.0
    last_token = ()
    for name, l in sorted(order, key=lambda key: key[0] in heavy):
        if name == "small":
            gathers[name, l] = _gather_start(_pack([w[k][0] for k in _SMALL]) + first_started, name="gather_small",
                                             relayed=False, after=last_token)
        else:
            gathers[name, l] = _gather_start((w[name][l] + first_started).astype(BF16), name=f"gather_{name}{l}",
                                             cols=lands_in_place(name), after=last_token)
        last_token = (gathers[name, l]["token"],)
        if len(gathers) == 1:
            first_started = last_token[0][0, 0]
    ready = {}

    def relay(key, after):
        if "cols" in gathers[key] and key[0] != "small":
            gathers[key] = _gather_relay(gathers[key], after, name=f"gather_{key[0]}{key[1]}")

    def weight(name, layer, after):
        key = ("small", 0) if name in _SMALL else (name, layer)
        if key not in ready:
            relay(key, after)
            at = order.index(key)
            for nxt in order[at + 1:at + 2] if at else []:
                relay(nxt, after)
            land, = _spread_wait(gathers[key], after, name=f"gathered_{key[0]}{key[1]}")
            if name in _SMALL:
                ready[key] = {k: _to_full_small(k, b)
                              for k, b in zip(_SMALL, _unpack(land, small_shapes, lead=(N_DEV,)))}
            elif lands_in_place(name):
                ready[key] = land
            elif _BIG[name]:
                ready[key] = jnp.transpose(land, (1, 0, 2)).reshape(land.shape[1], -1)
            else:
                ready[key] = land.reshape(-1, land.shape[2])
        return ready[key][name] if name in _SMALL else ready[key]

    exchanges = []

    def emit(name, layer, g):
        after = tuple(h["token"] for _, _, h in exchanges[-1:])
        if name == "small":
            handle = _exchange_start([_pack_blocks([_to_blocks_small(k, g[k]) for k in _SMALL])],
                                     name="exchange_small", after=after)
        elif lands_in_place(name):
            handle = _exchange_start([g], name=f"exchange_{name}{layer}", cols=True, after=after)
        elif _BIG[name]:
            c = w[name].shape[2]
            handle = _exchange_start([jnp.transpose(g.reshape(-1, N_DEV, c), (1, 0, 2))],
                                     name=f"exchange_{name}{layer}", after=after)
        else:
            handle = _exchange_start([g.reshape((N_DEV,) + w[name].shape[1:])], name=f"exchange_{name}{layer}",
                                     after=after)
        exchanges.append((name, layer, handle))
        return handle["token"]

    rep = {k: w[k] for k in _REPLICATED}
    loss_cols, dx, grads = _local_step(x, p, target, rep, weight, emit, n_heads,
                                       [h["token"] for h in gathers.values()])

    loss_part = jnp.sum(loss_cols).reshape(1)
    rep_gather = _gather_start(_pack([grads[k] for k in _REPLICATED] + [loss_part]), name="gather_rep_grads",
                               relayed=False)

    out = {}
    after = exchanges[-1][2]["token"]
    for name, layer, handle in exchanges:
        land, = _spread_wait(handle, after, name=f"exchanged_{name}{layer}")
        if name == "small":
            res = _adamw(land.reshape(N_DEV, -1, LANES), *[_pack([inp[pre + k][0] for k in _SMALL]) for pre in ("", "m_", "v_")],
                         name="adamw_small")
            for k, *vals in zip(_SMALL, *[_unpack(a, small_shapes) for a in res]):
                out[k] = [v[None] for v in vals]
        else:
            n_l, r, c = w[name].shape
            res = out[name] = _adamw(land.reshape(N_DEV, r, c),
                                     *[inp[pre + name].reshape(n_l * r, c) for pre in ("", "m_", "v_")],
                                     name=f"adamw_{name}{layer}", layer=layer, prev=out.get(name))
        after = res[0]
    for name in _BIG:
        out[name] = [a.reshape(w[name].shape) for a in out[name]]

    rep_shapes = [w[k].shape for k in _REPLICATED] + [(1,)]
    rep_parts, = _spread_wait(rep_gather, after, name="gathered_rep_grads")
    res = _adamw(rep_parts, *[_pack([inp[pre + k] for k in _REPLICATED] + [jnp.zeros((1,), F32)])
                              for pre in ("", "m_", "v_")], name="adamw_rep")
    for k, *vals in zip(_REPLICATED + ("loss",), *[_unpack(a, rep_shapes) for a in res]):
        out[k] = vals
    loss = out["loss"][0][0]

    return (loss, dx[None]) + tuple(out[k][j] for j in range(4) for k in _WEIGHTS)


def kernel(x, p, mix_pre_g, mix_post_g, ffn_pre_g, ffn_post_g, ple_norm_g, w_in_even, w_out_even, hg_lb_logits, hg_norm_g, w_in_odd, conv_w, conv_b, rg_wa, rg_ba, rg_wx, rg_bx, rg_lambda, w_out_odd, w_gate_up, w_down, w_ple_up, w_ple_gate, loss_target, m_mix_pre_g, m_mix_post_g, m_ffn_pre_g, m_ffn_post_g, m_ple_norm_g, m_w_in_even, m_w_out_even, m_hg_lb_logits, m_hg_norm_g, m_w_in_odd, m_conv_w, m_conv_b, m_rg_wa, m_rg_ba, m_rg_wx, m_rg_bx, m_rg_lambda, m_w_out_odd, m_w_gate_up, m_w_down, m_w_ple_up, m_w_ple_gate, v_mix_pre_g, v_mix_post_g, v_ffn_pre_g, v_ffn_post_g, v_ple_norm_g, v_w_in_even, v_w_out_even, v_hg_lb_logits, v_hg_norm_g, v_w_in_odd, v_conv_w, v_conv_b, v_rg_wa, v_rg_ba, v_rg_wx, v_rg_bx, v_rg_lambda, v_w_out_odd, v_w_gate_up, v_w_down, v_w_ple_up, v_w_ple_gate):
    return _step(dict(locals()))
```

```python
import functools
import math

import jax
import jax.numpy as jnp
from jax import lax
from jax.experimental import pallas as pl
from jax.experimental.pallas import tpu as pltpu

F32 = jnp.float32
BF16 = jnp.bfloat16

VMEM_LIMIT_BYTES = 56 * 1024 * 1024
ADAMW_BLOCK_BYTES = 40 * 1024 * 1024
LANES = 128
SUBLANES = 8

N_DEV = 8
HEAD_DIM = 128
SB_Q_TILE = 512
SB_K_TILE = 512
HG_CHUNK = 32
HG_HEADS_PER_STEP = 2
RG_BLOCK = 256
CONV_TAPS = 4
RG_C = 8.0
RMS_EPS = 1e-6

ADAM_LR = 0.001
ADAM_B1 = 0.9
ADAM_B2 = 0.999
ADAM_EPS = 1e-08
ADAM_WD = 0.01
ADAM_STEP = 10


def _params(*sem):
    return pltpu.CompilerParams(dimension_semantics=sem, vmem_limit_bytes=VMEM_LIMIT_BYTES)


def _pick(n, cands):
    for c in cands:
        if c <= n and n % c == 0:
            return c
    return n


def _mm(a, b, mode, *, name, out_dtype=F32, tm=512, tn=512, tk=None, deps=()):
    if mode == "nn":
        (m, k), (k2, n) = a.shape, b.shape
    else:
        (m, k), (n, k2) = a.shape, b.shape
    assert k == k2, (a.shape, b.shape, mode)
    tm, tn = min(tm, m), min(tn, n)
    tk = k if tk is None else min(tk, k)
    assert m % tm == 0 and n % tn == 0 and k % tk == 0, (m, n, k, tm, tn, tk)
    nk = k // tk

    a_spec = pl.BlockSpec((tm, tk), lambda i, j, kk: (i, kk))
    if mode == "nn":
        b_spec = pl.BlockSpec((tk, tn), lambda i, j, kk: (kk, j))
        dims = (((1,), (0,)), ((), ()))
    else:
        b_spec = pl.BlockSpec((tn, tk), lambda i, j, kk: (j, kk))
        dims = (((1,), (1,)), ((), ()))

    def body(a_ref, b_ref, *refs):
        o_ref, *acc = refs[len(deps):]
        part = lax.dot_general(a_ref[...].astype(BF16), b_ref[...].astype(BF16), dims, preferred_element_type=F32)
        if nk == 1:
            o_ref[...] = part.astype(out_dtype)
        else:
            acc_ref, = acc
            kk = pl.program_id(2)

            @pl.when(kk == 0)
            def _():
                acc_ref[...] = part

            @pl.when(kk > 0)
            def _():
                acc_ref[...] += part

            @pl.when(kk == nk - 1)
            def _():
                o_ref[...] = acc_ref[...].astype(out_dtype)

    return pl.pallas_call(
        body, name=name,
        grid=(m // tm, n // tn, nk),
        in_specs=[a_spec, b_spec] + [pl.BlockSpec(memory_space=pl.ANY)] * len(deps),
        out_specs=pl.BlockSpec((tm, tn), lambda i, j, kk: (i, j)),
        out_shape=jax.ShapeDtypeStruct((m, n), out_dtype),
        scratch_shapes=[] if nk == 1 else [pltpu.VMEM((tm, tn), F32)],
        compiler_params=_params("parallel", "parallel", "arbitrary"),
    )(a, b, *deps)


def _mm_tn(a, b, *, name, out_dtype, tm, tn, deps=()):
    k, m = a.shape
    n = b.shape[1]

    def body(a_ref, b_ref, *refs):
        o_ref, at_ref = refs[len(deps):]

        @pl.when(pl.program_id(1) == 0)
        def _():
            at_ref[...] = a_ref[...].astype(F32).T.astype(BF16)

        o_ref[...] = jnp.dot(at_ref[...], b_ref[...].astype(BF16), preferred_element_type=F32).astype(out_dtype)

    return pl.pallas_call(
        body, name=name,
        grid=(m // tm, n // tn),
        in_specs=[pl.BlockSpec((k, tm), lambda i, j: (0, i)), pl.BlockSpec((k, tn), lambda i, j: (0, j))]
        + [pl.BlockSpec(memory_space=pl.ANY)] * len(deps),
        out_specs=pl.BlockSpec((tm, tn), lambda i, j: (i, j)),
        out_shape=jax.ShapeDtypeStruct((m, n), out_dtype),
        scratch_shapes=[pltpu.VMEM((tm, k), BF16)],
        compiler_params=_params("parallel", "arbitrary"),
    )(a, b, *deps)


def _rowcall(name, fn, rows, pars, row_outs, red_rows=(), *, cols, ts=256, tc=None, deps=()):
    rows = [r if isinstance(r, tuple) else (r, 0) for r in rows]
    pars = [p if isinstance(p, tuple) else (p, 0) for p in pars]
    s = rows[0][0].shape[0]
    tc = cols if tc is None else tc
    ts = min(ts, s)
    assert s % ts == 0 and cols % tc == 0, (name, s, ts, cols, tc)
    n_in, n_row_out = len(rows) + len(pars), len(row_outs)

    def body(*refs):
        outs = fn(*[r[...] for r in refs[:n_in]])
        outs = outs if isinstance(outs, (tuple, list)) else (outs,)
        o_refs = refs[n_in + len(deps):]
        for o_ref, val in zip(o_refs[:n_row_out], outs[:n_row_out]):
            o_ref[...] = val.astype(o_ref.dtype)
        first = pl.program_id(1) == 0
        for o_ref, val in zip(o_refs[n_row_out:], outs[n_row_out:]):
            @pl.when(first)
            def _(o_ref=o_ref, val=val):
                o_ref[...] = val

            @pl.when(jnp.logical_not(first))
            def _(o_ref=o_ref, val=val):
                o_ref[...] += val

    def row_map(off):
        return lambda j, i: (i, j + off)

    def par_map(off):
        return lambda j, i: (0, j + off)

    return pl.pallas_call(
        body, name=name,
        grid=(cols // tc, s // ts),
        in_specs=[pl.BlockSpec((ts, tc), row_map(off)) for _, off in rows]
        + [pl.BlockSpec((p.shape[0], tc), par_map(off)) for p, off in pars]
        + [pl.BlockSpec(memory_space=pl.ANY)] * len(deps),
        out_specs=[pl.BlockSpec((ts, tc), lambda j, i: (i, j)) for _ in row_outs]
        + [pl.BlockSpec((r, tc), lambda j, i: (0, j)) for r in red_rows],
        out_shape=[jax.ShapeDtypeStruct((s, cols), dt) for dt in row_outs]
        + [jax.ShapeDtypeStruct((r, cols), F32) for r in red_rows],
        compiler_params=_params("parallel", "arbitrary"),
    )(*[r for r, _ in rows], *[p for p, _ in pars], *deps)


def _swiglu_act(g, u):
    return _silu(g) * u


def _gate_up(n, w, *, name):
    s, d = n.shape
    f = w.shape[1] // 2
    tn = _pick(f, (256, 128))
    nj = f // tn

    def body(a_ref, wg_ref, wu_ref, g_ref, u_ref, act_ref):
        a = a_ref[...].astype(BF16)
        g = jnp.dot(a, wg_ref[...].astype(BF16), preferred_element_type=F32)
        u = jnp.dot(a, wu_ref[...].astype(BF16), preferred_element_type=F32)
        g_ref[...] = g.astype(BF16)
        u_ref[...] = u.astype(BF16)
        act_ref[...] = _swiglu_act(g, u).astype(BF16)

    out = pl.BlockSpec((s, tn), lambda j: (0, j))
    return pl.pallas_call(
        body, name=name,
        grid=(nj,),
        in_specs=[pl.BlockSpec((s, d), lambda j: (0, 0)), pl.BlockSpec((d, tn), lambda j: (0, j)),
                  pl.BlockSpec((d, tn), lambda j: (0, nj + j))],
        out_specs=[out] * 3,
        out_shape=[jax.ShapeDtypeStruct((s, f), BF16)] * 3,
        compiler_params=_params("parallel"),
    )(n, w, w)


def _swiglu_bwd(g, u, dact, *, name, ts=128):
    s, f = g.shape
    ts = min(ts, s)

    def body(g_ref, u_ref, dact_ref, o_ref):
        _, vjp = jax.vjp(_swiglu_act, g_ref[...].astype(F32), u_ref[...].astype(F32))
        dg, du = vjp(dact_ref[...].astype(F32))
        o_ref[:, 0:f] = dg.astype(BF16)
        o_ref[:, f:2 * f] = du.astype(BF16)

    narrow = pl.BlockSpec((ts, f), lambda i: (i, 0))
    return pl.pallas_call(
        body, name=name,
        grid=(s // ts,),
        in_specs=[narrow] * 3,
        out_specs=pl.BlockSpec((ts, 2 * f), lambda i: (i, 0)),
        out_shape=jax.ShapeDtypeStruct((s, 2 * f), BF16),
        compiler_params=_params("parallel"),
    )(g, u, dact)


def _rms(x, g):
    return x * lax.rsqrt(jnp.mean(x * x, axis=-1, keepdims=True) + RMS_EPS) * g


def _sigmoid(x):
    return jax.nn.sigmoid(x)


def _silu(x):
    return x * jax.nn.sigmoid(x)


def _gelu(x):
    return 0.5 * x * (1.0 + jnp.tanh(math.sqrt(2.0 / math.pi) * (x + 0.044715 * (x * x * x))))


def _softplus(x):
    return jnp.maximum(x, 0.0) + jnp.log1p(jnp.exp(-jnp.abs(x)))


def _split(x, terms):
    parts = []
    for _ in range(terms - 1):
        parts.append(x.astype(BF16))
        x = x - parts[-1].astype(F32)
    return parts + [x.astype(BF16)]


def _xdot(x, t, terms=3):
    return sum(jnp.dot(p, t, preferred_element_type=F32) for p in _split(x, terms))


def _xdot_l(t, x):
    return sum(jnp.dot(t, p, preferred_element_type=F32) for p in _split(x, 3))


_NT = (((1,), (1,)), ((), ()))
_TN = (((0,), (0,)), ((), ()))


def _dot(a, b, dims=None):
    if dims is None:
        return jnp.dot(a.astype(BF16), b.astype(BF16), preferred_element_type=F32)
    return lax.dot_general(a.astype(BF16), b.astype(BF16), dims, preferred_element_type=F32)


def _iota(shape, axis):
    return lax.broadcasted_iota(jnp.int32, shape, axis)


def _sb_tile(qb, kblk, mask, upper, c_rem):
    z = lax.dot_general(qb, kblk, _NT, preferred_element_type=F32)
    soft = jnp.log1p(jnp.exp(-jnp.abs(z)))
    lbeta = jnp.minimum(z, 0.0) - soft
    l1m = -jnp.maximum(z, 0.0) - soft
    if mask is not None:
        l1m = jnp.where(mask, l1m, 0.0)
    rem = _xdot(l1m, upper, terms=2) + c_rem
    w = jnp.exp(lbeta + rem)
    if mask is not None:
        w = jnp.where(mask, w, 0.0)
    return lbeta, l1m, w


def _sb_tiles(s):
    tq = min(SB_Q_TILE, s)
    return tq, SB_K_TILE, tq // SB_K_TILE


def _sb_key_loops(qi, per_q, step, carry):
    n_full = qi * per_q
    carry = lax.fori_loop(0, per_q, lambda j, c: step(n_full + per_q - 1 - j, True, c), carry)
    return lax.fori_loop(0, n_full, lambda j, c: step(n_full - 1 - j, False, c), carry)


def _sb_fwd(proj, n_heads, *, name):
    s = proj.shape[0]
    t, tk, per_q = _sb_tiles(s)
    scale = HEAD_DIM ** -0.5

    def body(q_ref, k_ref, v_ref, o_ref):
        qi = pl.program_id(1)
        qb = (q_ref[...] * scale).astype(BF16)
        row, col = _iota((t, tk), 0) + qi * t, _iota((t, tk), 1)
        upper = (_iota((tk, tk), 0) > _iota((tk, tk), 1)).astype(BF16)

        def step(kb, masked, carry):
            acc, c_rem = carry
            rows = pl.ds(pl.multiple_of(kb * tk, tk), tk)
            kblk = k_ref[rows, :].astype(BF16)
            vblk = v_ref[rows, :].astype(BF16)
            _, l1m, w = _sb_tile(qb, kblk, (col + kb * tk) < row if masked else None, upper, c_rem)
            acc = acc + jnp.dot(w.astype(BF16), vblk, preferred_element_type=F32)
            return acc, c_rem + jnp.sum(l1m, axis=1, keepdims=True)

        acc, _ = _sb_key_loops(qi, per_q, step, (jnp.zeros((t, HEAD_DIM), F32), jnp.zeros((t, 1), F32)))
        o_ref[...] = acc

    return pl.pallas_call(
        body, name=name,
        grid=(n_heads, s // t),
        in_specs=[pl.BlockSpec((t, HEAD_DIM), lambda h, i: (i, h)),
                  pl.BlockSpec((s, HEAD_DIM), lambda h, i: (0, n_heads + h)),
                  pl.BlockSpec((s, HEAD_DIM), lambda h, i: (0, 2 * n_heads + h))],
        out_specs=pl.BlockSpec((t, HEAD_DIM), lambda h, i: (i, h)),
        out_shape=jax.ShapeDtypeStruct((s, n_heads * HEAD_DIM), F32),
        compiler_params=_params("parallel", "arbitrary"),
    )(proj, proj, proj)


def _sb_bwd(proj, dcat, n_heads, *, name):
    s = proj.shape[0]
    t, tk, per_q = _sb_tiles(s)
    scale = HEAD_DIM ** -0.5

    def body(q_ref, k_ref, v_ref, do_ref, dq_ref, dk_ref, dv_ref, g_s, sig_s):
        qi = pl.program_id(1)

        @pl.when(qi == 0)
        def _():
            dk_ref[...] = jnp.zeros_like(dk_ref)
            dv_ref[...] = jnp.zeros_like(dv_ref)

        qb = (q_ref[...] * scale).astype(BF16)
        dob = do_ref[...].astype(BF16)
        row, col = _iota((t, tk), 0) + qi * t, _iota((t, tk), 1)
        upper = (_iota((tk, tk), 0) > _iota((tk, tk), 1)).astype(BF16)
        lower_incl = (_iota((tk, tk), 0) >= _iota((tk, tk), 1)).astype(BF16)

        def weights(kb, masked, carry):
            c_rem, g_all = carry
            rows = pl.ds(pl.multiple_of(kb * tk, tk), tk)
            kblk = k_ref[rows, :].astype(BF16)
            vblk = v_ref[rows, :].astype(BF16)
            lbeta, l1m, w = _sb_tile(qb, kblk, (col + kb * tk) < row if masked else None, upper, c_rem)
            g = w * lax.dot_general(dob, vblk, _NT, preferred_element_type=F32)
            dv_ref[rows, :] += lax.dot_general(w.astype(BF16), dob, _TN, preferred_element_type=F32)
            g_s[kb] = g
            sig_s[kb] = jnp.exp(lbeta)
            return c_rem + jnp.sum(l1m, axis=1, keepdims=True), g_all + jnp.sum(g, axis=1, keepdims=True)

        zero_col = jnp.zeros((t, 1), F32)
        _, g_all = _sb_key_loops(qi, per_q, weights, (zero_col, zero_col))

        def scores(kb, masked, carry):
            dq, c_g = carry
            rows = pl.ds(pl.multiple_of(kb * tk, tk), tk)
            g, sig = g_s[kb], sig_s[kb]
            g_before = g_all - (_xdot(g, lower_incl) + c_g)
            dz = g * (1.0 - sig) - g_before * sig
            if masked:
                dz = jnp.where((col + kb * tk) < row, dz, 0.0)
            dz = dz.astype(BF16)
            dq = dq + jnp.dot(dz, k_ref[rows, :].astype(BF16), preferred_element_type=F32)
            dk_ref[rows, :] += lax.dot_general(dz, qb, _TN, preferred_element_type=F32)
            return dq, c_g + jnp.sum(g, axis=1, keepdims=True)

        dq, _ = _sb_key_loops(qi, per_q, scores, (jnp.zeros((t, HEAD_DIM), F32), zero_col))
        dq_ref[...] = dq * scale

    width = n_heads * HEAD_DIM
    return pl.pallas_call(
        body, name=name,
        grid=(n_heads, s // t),
        in_specs=[pl.BlockSpec((t, HEAD_DIM), lambda h, i: (i, h)),
                  pl.BlockSpec((s, HEAD_DIM), lambda h, i: (0, n_heads + h)),
                  pl.BlockSpec((s, HEAD_DIM), lambda h, i: (0, 2 * n_heads + h)),
                  pl.BlockSpec((t, HEAD_DIM), lambda h, i: (i, h))],
        out_specs=[pl.BlockSpec((t, HEAD_DIM), lambda h, i: (i, h)),
                   pl.BlockSpec((s, HEAD_DIM), lambda h, i: (0, h)),
                   pl.BlockSpec((s, HEAD_DIM), lambda h, i: (0, h))],
        out_shape=[jax.ShapeDtypeStruct((s, width), F32)] * 3,
        scratch_shapes=[pltpu.VMEM((s // tk, t, tk), F32)] * 2,
        compiler_params=_params("parallel", "arbitrary"),
    )(proj, proj, proj, dcat)


def _hg_pre(hq, hf, logits):
    mx = jnp.max(logits, axis=0, keepdims=True)
    ex = jnp.exp(logits - mx)
    lb = ex[0:1, :] / jnp.sum(ex, axis=0, keepdims=True)
    f = lb + (1.0 - lb) * _sigmoid(hf)
    return _silu(hq), 1.0 - f, jnp.log(f)


def _hg_post(o, norm_g, hgate):
    return _rms(o, norm_g) * _silu(hgate)


def _hg_specs(s, n_heads, first_block):
    def at(group):
        return pl.BlockSpec((s, HEAD_DIM), lambda h: (0, first_block + group * n_heads + h))
    return [at(0), at(1), at(2), at(3)]


def _hg_fwd(proj, logits, norm_g, n_heads, *, name):
    s = proj.shape[0]
    hc = HG_CHUNK
    n_chunks = s // hc
    d = HEAD_DIM

    hp = HG_HEADS_PER_STEP
    assert n_heads % hp == 0
    wide = hp * d

    def body(lg_ref, ng_ref, hq_ref, hf_ref, hi_ref, hgt_ref, out_ref, oraw_ref, st_ref,
             q_s, k_s, lf_s, cum_s, qc_s, oc_s):
        q, k, lf = _hg_pre(hq_ref[...], hf_ref[...], lg_ref[...])
        q_s[...] = q
        k_s[...] = k
        lf_s[...] = lf
        tril = (_iota((hc, hc), 0) >= _iota((hc, hc), 1)).astype(BF16)
        srow = _iota((hc, d), 0)

        def head_chunk(j, ci, rows, st):
            ln = slice(j * d, (j + 1) * d)
            q, k, v = q_s[rows, ln], k_s[rows, ln], hi_ref[rows, ln]
            cum = _xdot_l(tril, lf_s[rows, ln])
            st_ref[j, ci] = st
            o_inter = _dot(q * jnp.exp(cum), st, _NT)
            cum_s[:, ln] = cum
            qc_s[:, ln] = q
            for t in range(hc):
                ng = (t // SUBLANES + 1) * SUBLANES
                e = jnp.where(srow[:ng] <= t, jnp.exp(cum_s[t:t + 1, ln] - cum[:ng]), 0.0)
                sc = jnp.sum(qc_s[t:t + 1, ln] * k[:ng] * e, axis=1, keepdims=True)
                oc_s[t:t + 1, ln] = jnp.sum(sc * v[:ng], axis=0, keepdims=True)
            oraw_ref[rows, ln] = o_inter + oc_s[:, ln]
            last = cum_s[hc - 1:hc, ln]
            return st * jnp.exp(last) + _dot(v, k * jnp.exp(last - cum), _TN)

        def chunk(ci, states):
            rows = pl.ds(pl.multiple_of(ci * hc, hc), hc)
            return tuple(head_chunk(j, ci, rows, st) for j, st in enumerate(states))

        lax.fori_loop(0, n_chunks, chunk, tuple(jnp.zeros((d, d), F32) for _ in range(hp)))
        for j in range(hp):
            ln = slice(j * d, (j + 1) * d)
            out_ref[:, ln] = _hg_post(oraw_ref[:, ln], ng_ref[...], hgt_ref[:, ln]).astype(BF16)

    width = n_heads * d
    first = 3 * n_heads // hp
    groups = [pl.BlockSpec((s, wide), functools.partial(lambda h, g: (0, first + g * (n_heads // hp) + h), g=g))
              for g in range(4)]
    head_block = pl.BlockSpec((s, wide), lambda h: (0, h))
    return pl.pallas_call(
        body, name=name,
        grid=(n_heads // hp,),
        in_specs=[pl.BlockSpec((2, wide), lambda h: (0, h)), pl.BlockSpec((1, d), lambda h: (0, 0))] + groups,
        out_specs=[head_block, head_block, pl.BlockSpec((hp, n_chunks, d, d), lambda h: (h, 0, 0, 0))],
        out_shape=[jax.ShapeDtypeStruct((s, width), BF16), jax.ShapeDtypeStruct((s, width), F32),
                   jax.ShapeDtypeStruct((n_heads, n_chunks, d, d), F32)],
        scratch_shapes=[pltpu.VMEM((s, wide), F32)] * 3 + [pltpu.VMEM((hc, wide), F32)] * 3,
        compiler_params=_params("arbitrary"),
    )(logits, norm_g, proj, proj, proj, proj)


def _hg_bwd(proj, logits, norm_g, oraw, states, dcat, n_heads, *, name):
    s = proj.shape[0]
    hc = HG_CHUNK
    n_chunks = s // hc
    d = HEAD_DIM

    def body(lg_ref, ng_ref, hq_ref, hf_ref, hi_ref, hgt_ref, oraw_ref, st_ref, dout_ref,
             dhq_ref, dhf_ref, dhi_ref, dhgt_ref, dlg_ref, dng_ref,
             q_s, k_s, lf_s, do_s, dq_s, dk_s, dlf_s, cum_s, qc_s, doc_s, dqc_s, dkc_s, dvc_s):
        head = pl.program_id(0)
        (q, k, lf), pre_vjp = jax.vjp(_hg_pre, hq_ref[...], hf_ref[...], lg_ref[...])
        q_s[...] = q
        k_s[...] = k
        lf_s[...] = lf
        _, post_vjp = jax.vjp(_hg_post, oraw_ref[...], ng_ref[...], hgt_ref[...])
        do, dng, dhgt = post_vjp(dout_ref[...])
        do_s[...] = do
        dhgt_ref[...] = dhgt.astype(BF16)

        @pl.when(head == 0)
        def _():
            dng_ref[...] = dng

        @pl.when(head > 0)
        def _():
            dng_ref[...] += dng

        triu = (_iota((hc, hc), 0) <= _iota((hc, hc), 1)).astype(BF16)
        tril = (_iota((hc, hc), 0) >= _iota((hc, hc), 1)).astype(BF16)
        srow = _iota((hc, d), 0)

        def chunk(j, dst):
            ci = n_chunks - 1 - j
            rows = pl.ds(pl.multiple_of(ci * hc, hc), hc)
            q, k, v, do_c = q_s[rows, :], k_s[rows, :], hi_ref[rows, :], do_s[rows, :]
            cum = _xdot_l(tril, lf_s[rows, :])
            st = st_ref[0, ci]
            cum_s[...] = cum
            qc_s[...] = q
            doc_s[...] = do_c
            last = cum_s[hc - 1:hc, :]
            e_cum, e_last = jnp.exp(cum), jnp.exp(last - cum)
            dqc_s[...] = _dot(do_c, st) * e_cum
            dk_state = _dot(v, dst) * e_last
            dkc_s[...] = dk_state
            dvc_s[...] = _dot(k * e_last, dst, _NT)
            d_last = (jnp.sum(dst * st, axis=0, keepdims=True) * jnp.exp(last)
                      + jnp.sum(k * dk_state, axis=0, keepdims=True))
            for t in range(hc):
                ng = (t // SUBLANES + 1) * SUBLANES
                qt, dot_ = qc_s[t:t + 1, :], doc_s[t:t + 1, :]
                e = jnp.where(srow[:ng] <= t, jnp.exp(cum_s[t:t + 1, :] - cum[:ng]), 0.0)
                ke = k[:ng] * e
                d_a = jnp.sum(dot_ * v[:ng], axis=1, keepdims=True)
                dqc_s[t:t + 1, :] += jnp.sum(d_a * ke, axis=0, keepdims=True)
                dkc_s[0:ng, :] += d_a * (qt * e)
                dvc_s[0:ng, :] += jnp.sum(qt * ke, axis=1, keepdims=True) * dot_
            dq, dk = dqc_s[...], dkc_s[...]
            d_b = q * dq - k * dk
            dq_s[rows, :] = dq
            dk_s[rows, :] = dk
            dhi_ref[rows, :] = dvc_s[...].astype(BF16)
            dlf_s[rows, :] = _xdot_l(triu, d_b) + d_last
            return dst * jnp.exp(last) + _dot(do_c, q * e_cum, _TN)

        lax.fori_loop(0, n_chunks, chunk, jnp.zeros((d, d), F32))
        dhq, dhf, dlg = pre_vjp((dq_s[...], dk_s[...], dlf_s[...]))
        dhq_ref[...] = dhq.astype(BF16)
        dhf_ref[...] = dhf.astype(BF16)
        dlg_ref[...] = dlg

    width = n_heads * d
    head_block = pl.BlockSpec((s, d), lambda h: (0, h))
    return pl.pallas_call(
        body, name=name,
        grid=(n_heads,),
        in_specs=[pl.BlockSpec((2, d), lambda h: (0, h)), pl.BlockSpec((1, d), lambda h: (0, 0))]
        + _hg_specs(s, n_heads, 3 * n_heads)
        + [head_block, pl.BlockSpec((1, n_chunks, d, d), lambda h: (h, 0, 0, 0)),
           pl.BlockSpec((s, d), lambda h: (0, n_heads + h))],
        out_specs=[head_block] * 4 + [pl.BlockSpec((2, d), lambda h: (0, h)), pl.BlockSpec((1, d), lambda h: (0, 0))],
        out_shape=[jax.ShapeDtypeStruct((s, width), BF16)] * 4
        + [jax.ShapeDtypeStruct((2, width), F32), jax.ShapeDtypeStruct((1, d), F32)],
        scratch_shapes=[pltpu.VMEM((s, d), F32)] * 7 + [pltpu.VMEM((hc, d), F32)] * 6,
        compiler_params=_params("arbitrary"),
    )(logits, norm_g, proj, proj, proj, proj, oraw, states, dcat)


def _shift_down(x, n, srow):
    if n == 0:
        return x
    return jnp.where(srow >= n, pltpu.roll(x, n, 0), 0.0)


def _shift_up(x, n, srow):
    if n == 0:
        return x
    s = x.shape[0]
    return jnp.where(srow < s - n, pltpu.roll(x, s - n, 0), 0.0)


def _rg_gates_fwd(proj, conv_w, conv_b, wa, ba, wx, bx, *, name):
    s = proj.shape[0]
    nb = wa.shape[0]
    bw = RG_BLOCK

    def body(xb_ref, cw_ref, cb_ref, wa_ref, ba_ref, wx_ref, bx_ref, xc_ref, ra_ref, ix_ref):
        x = xb_ref[...]
        srow = _iota((s, bw), 0)
        cw = cw_ref[...]
        xc = cb_ref[...] + cw[0:1, :] * x
        for tap in range(1, CONV_TAPS):
            xc = xc + cw[tap:tap + 1, :] * _shift_down(x, tap, srow)
        xc_ref[...] = xc
        ra_ref[...] = _dot(xc, wa_ref[0]) + ba_ref[0]
        ix_ref[...] = _dot(xc, wx_ref[0]) + bx_ref[0]

    col = pl.BlockSpec((s, bw), lambda n: (0, n))
    vec = lambda r: pl.BlockSpec((r, bw), lambda n: (0, n))
    mat = pl.BlockSpec((1, bw, bw), lambda n: (n, 0, 0))
    bias = pl.BlockSpec((1, 1, bw), lambda n: (n, 0, 0))
    return pl.pallas_call(
        body, name=name,
        grid=(nb,),
        in_specs=[pl.BlockSpec((s, bw), lambda n: (0, nb + n)), vec(CONV_TAPS), vec(1), mat, bias, mat, bias],
        out_specs=[col] * 3,
        out_shape=[jax.ShapeDtypeStruct((s, nb * bw), F32)] * 3,
        compiler_params=_params("parallel"),
    )(proj, conv_w, conv_b, wa, ba, wx, bx)


def _rg_au(ra, ix, xc, lam, first_row):
    log_a = -RG_C * _sigmoid(ra) * _softplus(-lam)
    th = jnp.tanh(log_a)
    one_minus_a2 = -2.0 * th / (1.0 - th)
    mult = jnp.where(first_row, 1.0, jnp.sqrt(one_minus_a2))
    return jnp.exp(log_a), xc * _sigmoid(ix) * mult


def _rg_out(gate, hs):
    return _gelu(gate) * hs


def _linear_scan(a, b, a_s, b_s, in_s, reverse):
    s, c = a.shape
    within = _iota((s, c), 0) & (SUBLANES - 1)
    shift = 1
    while shift < SUBLANES:
        if reverse:
            take = within < SUBLANES - shift
            a_n, b_n = pltpu.roll(a, s - shift, 0), pltpu.roll(b, s - shift, 0)
        else:
            take = within >= shift
            a_n, b_n = pltpu.roll(a, shift, 0), pltpu.roll(b, shift, 0)
        b = jnp.where(take, a * b_n + b, b)
        a = jnp.where(take, a * a_n, a)
        shift *= 2
    a_s[...] = a
    b_s[...] = b
    n_tiles = s // SUBLANES
    edge = 0 if reverse else SUBLANES - 1

    def tile(i, h):
        rows = pl.ds(pl.multiple_of(((n_tiles - 1 - i) if reverse else i) * SUBLANES, SUBLANES), SUBLANES)
        in_s[rows, :] = jnp.broadcast_to(h, (SUBLANES, c))
        return a_s[rows, :][edge:edge + 1, :] * h + b_s[rows, :][edge:edge + 1, :]

    lax.fori_loop(0, n_tiles, tile, jnp.zeros((1, c), F32))
    return a * in_s[...] + b


def _rg_scan_fwd(proj, xc, ra, ix, lam, *, name):
    s, width = xc.shape
    tc = LANES

    def body(gate_ref, xc_ref, ra_ref, ix_ref, lam_ref, hs_ref, gact_ref, a_s, u_s, in_s):
        first_row = _iota((s, tc), 0) == 0
        a, u = _rg_au(ra_ref[...], ix_ref[...], xc_ref[...], lam_ref[...], first_row)
        hs = _linear_scan(a, u, a_s, u_s, in_s, reverse=False)
        hs_ref[...] = hs
        gact_ref[...] = _rg_out(gate_ref[...], hs).astype(BF16)

    col = pl.BlockSpec((s, tc), lambda n: (0, n))
    return pl.pallas_call(
        body, name=name,
        grid=(width // tc,),
        in_specs=[col, col, col, col, pl.BlockSpec((1, tc), lambda n: (0, n))],
        out_specs=[col, col],
        out_shape=[jax.ShapeDtypeStruct((s, width), F32), jax.ShapeDtypeStruct((s, width), BF16)],
        scratch_shapes=[pltpu.VMEM((s, tc), F32)] * 3,
        compiler_params=_params("parallel"),
    )(proj, xc, ra, ix, lam)


def _rg_scan_bwd(dgo, proj, hs, xc, ra, ix, lam, *, name):
    s, width = xc.shape
    tc = LANES

    def body(dgo_ref, gate_ref, hs_ref, xc_ref, ra_ref, ix_ref, lam_ref,
             dgate_ref, dra_ref, dix_ref, dxc_ref, dlam_ref, a_s, dh_s, g_s):
        srow = _iota((s, tc), 0)
        hs = hs_ref[...]
        _, out_vjp = jax.vjp(_rg_out, gate_ref[...], hs)
        dgate, dh = out_vjp(dgo_ref[...])
        dgate_ref[...] = dgate.astype(BF16)
        au = functools.partial(_rg_au, first_row=srow == 0)
        (a, _), au_vjp = jax.vjp(au, ra_ref[...], ix_ref[...], xc_ref[...], lam_ref[...])
        g = _linear_scan(_shift_up(a, 1, srow), dh, a_s, dh_s, g_s, reverse=True)
        dra, dix, dxc, dlam = au_vjp((g * _shift_down(hs, 1, srow), g))
        dra_ref[...] = dra.astype(BF16)
        dix_ref[...] = dix.astype(BF16)
        dxc_ref[...] = dxc
        dlam_ref[...] = dlam

    col = pl.BlockSpec((s, tc), lambda n: (0, n))
    vec = pl.BlockSpec((1, tc), lambda n: (0, n))
    return pl.pallas_call(
        body, name=name,
        grid=(width // tc,),
        in_specs=[col] * 6 + [vec],
        out_specs=[col] * 4 + [vec],
        out_shape=[jax.ShapeDtypeStruct((s, width), BF16)] * 3
        + [jax.ShapeDtypeStruct((s, width), F32), jax.ShapeDtypeStruct((1, width), F32)],
        scratch_shapes=[pltpu.VMEM((s, tc), F32)] * 3,
        compiler_params=_params("parallel"),
    )(dgo, proj, hs, xc, ra, ix, lam)


def _rg_gates_bwd(dra, dix, dxc1, xc, proj, conv_w, wa, wx, *, name):
    s = proj.shape[0]
    nb = wa.shape[0]
    bw = RG_BLOCK

    def body(dra_ref, dix_ref, dxc_ref, xc_ref, xb_ref, cw_ref, wa_ref, wx_ref,
             dxb_ref, dcw_ref, dcb_ref, dwa_ref, dba_ref, dwx_ref, dbx_ref):
        dra, dix = dra_ref[...], dix_ref[...]
        xc_t = xc_ref[...].T.astype(BF16)
        dwa_ref[0] = jnp.dot(xc_t, dra, preferred_element_type=F32)
        dwx_ref[0] = jnp.dot(xc_t, dix, preferred_element_type=F32)
        dba_ref[0] = jnp.sum(dra.astype(F32), axis=0, keepdims=True)
        dbx_ref[0] = jnp.sum(dix.astype(F32), axis=0, keepdims=True)
        dxc = dxc_ref[...] + _dot(dra, wa_ref[0], _NT) + _dot(dix, wx_ref[0], _NT)
        srow = _iota((s, bw), 0)
        x = xb_ref[...]
        cw = cw_ref[...]
        dx = cw[0:1, :] * dxc
        dcw = [jnp.sum(dxc * x, axis=0, keepdims=True)]
        for tap in range(1, CONV_TAPS):
            dx = dx + cw[tap:tap + 1, :] * _shift_up(dxc, tap, srow)
            dcw.append(jnp.sum(dxc * _shift_down(x, tap, srow), axis=0, keepdims=True))
        dxb_ref[...] = dx.astype(BF16)
        r4 = _iota((CONV_TAPS, bw), 0)
        acc = jnp.zeros((CONV_TAPS, bw), F32)
        for tap in range(CONV_TAPS):
            acc = jnp.where(r4 == tap, dcw[tap], acc)
        dcw_ref[...] = acc
        dcb_ref[...] = jnp.sum(dxc, axis=0, keepdims=True)

    col = pl.BlockSpec((s, bw), lambda n: (0, n))
    vec = lambda r: pl.BlockSpec((r, bw), lambda n: (0, n))
    mat = pl.BlockSpec((1, bw, bw), lambda n: (n, 0, 0))
    bias = pl.BlockSpec((1, 1, bw), lambda n: (n, 0, 0))
    width = nb * bw
    return pl.pallas_call(
        body, name=name,
        grid=(nb,),
        in_specs=[col, col, col, col, pl.BlockSpec((s, bw), lambda n: (0, nb + n)), vec(CONV_TAPS), mat, mat],
        out_specs=[col, vec(CONV_TAPS), vec(1), mat, bias, mat, bias],
        out_shape=[jax.ShapeDtypeStruct((s, width), BF16), jax.ShapeDtypeStruct((CONV_TAPS, width), F32),
                   jax.ShapeDtypeStruct((1, width), F32), jax.ShapeDtypeStruct((nb, bw, bw), F32),
                   jax.ShapeDtypeStruct((nb, 1, bw), F32), jax.ShapeDtypeStruct((nb, bw, bw), F32),
                   jax.ShapeDtypeStruct((nb, 1, bw), F32)],
        compiler_params=_params("parallel"),
    )(dra, dix, dxc1, xc, proj, conv_w, wa, wx)


_HBM = pl.BlockSpec(memory_space=pltpu.HBM)
_FLIPS = ((0, 0, 1), (1, 0, 0), (0, 1, 0), (1, 1, 0))
_ALL_FLIPS = tuple((a, b, c) for a in (0, 1) for b in (0, 1) for c in (0, 1))[1:]


def _flip(pos, f):
    return tuple(1 - p if b else p for p, b in zip(pos, f))


def _dev_index(pos):
    return 4 * pos[0] + 2 * pos[1] + pos[2]


def _block(ref, idx, cols):
    if not cols:
        return ref.at[idx]
    n = ref.shape[-1] // N_DEV
    start = pl.multiple_of(idx * n, LANES)
    return ref.at[(slice(None),) * (len(ref.shape) - 1) + (pl.ds(start, n),)]


_SEM = pl.BlockSpec(memory_space=pltpu.SEMAPHORE)
_ANY = pl.BlockSpec(memory_space=pl.ANY)
_N_PEERS = N_DEV - 1


def _hbm(x):
    return pltpu.with_memory_space_constraint(x, pltpu.HBM)


def _me():
    return lax.axis_index("x"), lax.axis_index("y"), lax.axis_index("c")


def _spread_copies(plan, src_refs, land_refs, send_sems, recv_sems, local_sems):
    local, remote = plan(src_refs, land_refs)
    local = [pltpu.make_async_copy(src, dst, local_sems.at[i]) for i, (src, dst) in enumerate(local)]
    remote = [pltpu.make_async_remote_copy(src_ref=src, dst_ref=dst, send_sem=send_sems.at[k], recv_sem=recv_sems.at[k],
                                           device_id=peer, device_id_type=pl.DeviceIdType.MESH)
              for k, (src, dst, peer) in enumerate(remote)]
    return local, remote


def _spread_start(srcs, lands, plan, n_remote, n_local, *, name, after=()):
    ns, nl = len(srcs), len(lands)
    n_in = ns + nl + len(after)

    def body(*refs):
        src_refs, land_refs = refs[:ns], refs[ns:ns + nl]
        send_sems, recv_sems, local_sems = refs[n_in:n_in + 3]
        local, remote = _spread_copies(plan, src_refs, land_refs, send_sems, recv_sems, local_sems)
        for cp in local + remote:
            cp.start()
        token = refs[-1]
        token[...] = jnp.zeros_like(token)

    lands = [_hbm(lax.empty(*x)) if isinstance(x, tuple) else x for x in lands]
    out = pl.pallas_call(
        body, name=name,
        in_specs=[_HBM] * (ns + nl) + [_ANY] * len(after),
        out_specs=[_SEM] * 3 + [_HBM] * (ns + nl) + [pl.BlockSpec(memory_space=pltpu.VMEM)],
        out_shape=[pltpu.SemaphoreType.DMA((n_remote,)), pltpu.SemaphoreType.DMA((n_remote,)),
                   pltpu.SemaphoreType.DMA((max(n_local, 1),))]
        + [pltpu.HBM(x.shape, x.dtype) for x in list(srcs) + lands]
        + [jax.ShapeDtypeStruct((SUBLANES, LANES), F32)],
        input_output_aliases={i: 3 + i for i in range(ns + nl)},
        compiler_params=pltpu.CompilerParams(has_side_effects=pltpu.SideEffectType.DATAFLOW_SIDE_EFFECTING),
    )(*[_hbm(x) for x in srcs], *lands, *after)
    return dict(sems=list(out[:3]), srcs=list(out[3:3 + ns]), lands=list(out[3 + ns:3 + ns + nl]),
                token=out[-1], plan=plan)


def _spread_wait(handle, after, *, name):
    ns, nl = len(handle["srcs"]), len(handle["lands"])

    def body(*refs):
        src_refs, land_refs = refs[:ns], refs[ns:ns + nl]
        send_sems, recv_sems, local_sems = refs[ns + nl:ns + nl + 3]
        local, remote = _spread_copies(handle["plan"], src_refs, land_refs, send_sems, recv_sems, local_sems)
        for cp in local:
            cp.wait()
        for cp in remote:
            cp.wait_send()
            cp.wait_recv()

    out = pl.pallas_call(
        body, name=name,
        in_specs=[_HBM] * (ns + nl) + [_SEM] * 3 + [_ANY],
        out_specs=[_HBM] * (ns + nl),
        out_shape=[pltpu.HBM(x.shape, x.dtype) for x in handle["srcs"] + handle["lands"]],
        input_output_aliases={i: i for i in range(ns + nl)},
        compiler_params=pltpu.CompilerParams(has_side_effects=pltpu.SideEffectType.DATAFLOW_SIDE_EFFECTING),
    )(*handle["srcs"], *handle["lands"], *handle["sems"], after)
    return list(out[ns:])


def _gather_start(x, *, name, cols=False, relayed=True, after=()):
    shape = x.shape[:-1] + (N_DEV * x.shape[-1],) if cols else (N_DEV,) + x.shape
    flips = _FLIPS if relayed else _ALL_FLIPS

    def plan(src_refs, land_refs):
        me = _me()
        mine = _block(land_refs[0], _dev_index(me), cols)
        return [(src_refs[0], mine)], [(src_refs[0], mine, _flip(me, f)) for f in flips]

    handle = _spread_start([x], [(shape, x.dtype)], plan, len(flips), 1, name=name, after=after)
    handle["cols"] = cols
    return handle


def _gather_relay(handle, after, *, name):
    cols = handle["cols"]
    land, = _spread_wait(handle, after, name=f"{name}_arrived")

    def plan(src_refs, land_refs):
        me = _me()
        blocks = [_block(land_refs[0], _dev_index(_flip(me, f)), cols) for f in _FLIPS[1:]]
        return [], [(blk, blk, _flip(me, _FLIPS[0])) for blk in blocks]

    return _spread_start([], [land], plan, len(_FLIPS) - 1, 0, name=f"{name}_pass")


def _exchange_start(ps, *, name, cols=False, after=()):
    blk = ps[0].shape[:-1] + (ps[0].shape[-1] // N_DEV,) if cols else ps[0].shape[1:]

    def plan(src_refs, land_refs):
        me = _me()
        me_idx = _dev_index(me)
        local = [(_block(src, me_idx, cols), land_refs[0].at[me_idx, a]) for a, src in enumerate(src_refs)]
        remote = [(_block(src, _dev_index(_flip(me, f)), cols), land_refs[0].at[me_idx, a], _flip(me, f))
                  for f in _ALL_FLIPS for a, src in enumerate(src_refs)]
        return local, remote

    return _spread_start(ps, [((N_DEV, len(ps)) + blk, ps[0].dtype)], plan, _N_PEERS * len(ps), len(ps), name=name,
                         after=after)


def _adamw(parts, w, m, v, *, name, layer=0, prev=None):
    n_rows, c = w.shape
    r = parts.shape[1]
    row_bytes = c * (N_DEV * parts.dtype.itemsize + 7 * 4) * 2
    tr = r
    for cand in (512, 256, 128, 64, 32, 16):
        if r % cand == 0 and cand * row_bytes <= ADAMW_BLOCK_BYTES:
            tr = cand
            break
    c1 = 1.0 - ADAM_B1 ** ADAM_STEP
    c2 = 1.0 - ADAM_B2 ** ADAM_STEP

    def body(p_ref, w_ref, m_ref, v_ref, *rest):
        g_ref, d_ref, nm_ref, nv_ref = rest[-4:]
        g = p_ref[0].astype(F32)
        for j in range(1, N_DEV):
            g = g + p_ref[j].astype(F32)
        nm = ADAM_B1 * m_ref[...] + (1.0 - ADAM_B1) * g
        nv = ADAM_B2 * v_ref[...] + (1.0 - ADAM_B2) * (g * g)
        g_ref[...] = g
        nm_ref[...] = nm
        nv_ref[...] = nv
        d_ref[...] = -ADAM_LR * ((nm * (1.0 / c1)) / (jnp.sqrt(nv * (1.0 / c2)) + ADAM_EPS) + ADAM_WD * w_ref[...])

    off = layer * (r // tr)
    blk = pl.BlockSpec((tr, c), lambda i: (i + off, 0))
    prev = list(prev) if prev is not None else []
    return pl.pallas_call(
        body, name=name,
        grid=(r // tr,),
        in_specs=[pl.BlockSpec((N_DEV, tr, c), lambda i: (0, i, 0)), blk, blk, blk] + [_ANY] * len(prev),
        out_specs=[blk] * 4,
        out_shape=[jax.ShapeDtypeStruct((n_rows, c), F32)] * 4,
        input_output_aliases={4 + j: j for j in range(len(prev))},
        compiler_params=_params("parallel"),
    )(parts, w, m, v, *prev)


_TN_CANDS = (512, 256, 128)
_TK_MAX = 5632
_TK_WHOLE_ROWS = 2816


def _contraction_tiles(m, k):
    tk = k
    while tk > _TK_MAX and tk % 2 == 0 and (tk // 2) % LANES == 0:
        tk //= 2
    tm = m if tk <= _TK_WHOLE_ROWS or m % 2 else m // 2
    return tm, tk


def _nn(a, b, name, out_dtype=F32):
    tm, tk = _contraction_tiles(*a.shape)
    return _mm(a, b, "nn", name=name, out_dtype=out_dtype, tm=tm, tn=_pick(b.shape[1], _TN_CANDS), tk=tk)


def _nt(a, b, name, out_dtype=F32, deps=()):
    tm, tk = _contraction_tiles(*a.shape)
    return _mm(a, b, "nt", name=name, out_dtype=out_dtype, tm=tm, tn=_pick(b.shape[0], _TN_CANDS), tk=tk,
               deps=deps)


def _tn(a, b, name, out_dtype=BF16, deps=()):
    assert a.shape[0] == b.shape[0], (a.shape, b.shape)
    return _mm_tn(a, b, name=name, out_dtype=out_dtype, tm=_pick(a.shape[1], _TN_CANDS),
                  tn=_pick(b.shape[1], (1024,) + _TN_CANDS), deps=deps)


def _local_step(x, p, target, rep, weight, emit, n_heads, start_tokens=()):
    s, d = x.shape
    depth = p.shape[0]
    grads = {}
    rep_grads = {k: [None] * depth for k in ("mix_pre_g", "mix_post_g", "ffn_pre_g", "ffn_post_g", "ple_norm_g")}

    pending = list(start_tokens)
    gains = {}

    def gain(name, i):
        if (name, i) not in gains:
            gains[name, i] = rep[name][i:i + 1]
        return gains[name, i]

    def send(name, layer, g):
        token = emit(name, layer, g)
        if token is not None:
            pending.append(token)

    def rowcall(*args, **kwargs):
        deps, pending[:] = tuple(pending), []
        return _rowcall(*args, deps=deps, **kwargs)

    def grad_in(*args, **kwargs):
        deps, pending[:] = tuple(pending), []
        return _nt(*args, deps=deps, **kwargs)

    deferred = []

    def send_small(i, name, layer, a, b, mm_name):
        if i == 0 and depth > 1:
            deferred.append((name, layer, a, b, mm_name))
        else:
            send(name, layer, _tn(a, b, mm_name))

    saved = []
    h = x
    for i in range(depth):
        sv = {"h": h}
        n1, = rowcall(f"pre_norm{i}", lambda hh, g: _rms(hh, g), [h], [gain("mix_pre_g", i)], [BF16], cols=d)
        sv["n1"] = n1
        if i % 2 == 0:
            proj = _nn(n1, weight("w_in_even", 0, n1), f"in_even{i}")
            a_out = _sb_fwd(proj, n_heads, name=f"sb_fwd{i}")
            b_out, oraw, states = _hg_fwd(proj, rep["hg_lb_logits"], rep["hg_norm_g"], n_heads, name=f"hg_fwd{i}")
            cat = jnp.concatenate([a_out.astype(BF16), b_out], axis=1)
            m = _nn(cat, weight("w_out_even", 0, cat), f"out_even{i}")
            sv.update(proj=proj, oraw=oraw, states=states, cat=cat)
        else:
            proj = _nn(n1, weight("w_in_odd", 0, n1), f"in_odd{i}")
            sm = {k: weight(k, 0, proj) for k in _SMALL}
            xc, ra, ix = _rg_gates_fwd(proj, sm["conv_w"], sm["conv_b"], sm["rg_wa"], sm["rg_ba"],
                                       sm["rg_wx"], sm["rg_bx"], name=f"rg_gates_fwd{i}")
            hs, gact = _rg_scan_fwd(proj, xc, ra, ix, sm["rg_lambda"], name=f"rg_scan_fwd{i}")
            m = _nn(gact, weight("w_out_odd", 0, gact), f"out_odd{i}")
            sv.update(proj=proj, xc=xc, ra=ra, ix=ix, hs=hs, gact=gact, sm=sm)

        def post_mix(hh, mm, g_post, g_pre):
            h1 = hh + _rms(mm, g_post)
            return h1, _rms(h1, g_pre)

        h1, n2 = rowcall(f"post_mix{i}", post_mix, [h, m], [gain("mix_post_g", i), gain("ffn_pre_g", i)],
                          [F32, BF16], cols=d)
        gate, up, act = _gate_up(n2, weight("w_gate_up", i, n2), name=f"gate_up{i}")
        f = _nn(act, weight("w_down", i, act), f"down{i}")

        def post_ffn(hh, ff_out, g_post):
            h2 = hh + _rms(ff_out, g_post)
            return h2, h2

        h2, h2b = rowcall(f"post_ffn{i}", post_ffn, [h1, f], [gain("ffn_post_g", i)], [F32, BF16], cols=d)
        e = _nn(p[i], weight("w_ple_up", i, h2b), f"ple_up{i}")
        gl = _nn(h2b, weight("w_ple_gate", i, h2b), f"ple_gate{i}")
        h3, = rowcall(f"ple{i}", lambda hh, a, b, g: hh + _rms(_sigmoid(a) * b, g), [h2, gl, e],
                       [gain("ple_norm_g", i)], [F32], cols=d)
        sv.update(m=m, h1=h1, n2=n2, gate=gate, up=up, act=act, f=f, h2b=h2b, e=e, gl=gl)
        saved.append(sv)
        h = h3

    def loss_fn(y, t):
        err = y - t
        return err * (1.0 / d), jnp.sum(err * err, axis=0, keepdims=True) * (0.5 / d)

    dh, loss_cols = rowcall("loss", loss_fn, [h, target], [], [F32], red_rows=(1,), cols=d)

    for i in reversed(range(depth)):
        sv = saved[i]

        def ple_bwd(dy, a, b, g):
            _, vjp = jax.vjp(lambda a_, b_, g_: _rms(_sigmoid(a_) * b_, g_), a, b, g)
            return vjp(dy)

        dgl, de, rep_grads["ple_norm_g"][i] = rowcall(
            f"ple_bwd{i}", ple_bwd, [dh, sv["gl"], sv["e"]], [gain("ple_norm_g", i)], [BF16, BF16],
            red_rows=(1,), cols=d)
        send_small(i, "w_ple_up", i, p[i], de, f"d_ple_up{i}")
        send_small(i, "w_ple_gate", i, sv["h2b"], dgl, f"d_ple_gate{i}")
        dh2_ple = grad_in(dgl, weight("w_ple_gate", i, dgl), f"dx_ple_gate{i}")

        def post_ffn_bwd(dy, dx, ff_out, g):
            dh2 = dy + dx
            _, vjp = jax.vjp(_rms, ff_out, g)
            df, dg = vjp(dh2)
            return dh2, df, dg

        dh2, df, rep_grads["ffn_post_g"][i] = rowcall(
            f"post_ffn_bwd{i}", post_ffn_bwd, [dh, dh2_ple, sv["f"]], [gain("ffn_post_g", i)], [F32, BF16],
            red_rows=(1,), cols=d)
        send("w_down", i, _tn(sv["act"], df, f"d_down{i}"))
        dact = grad_in(df, weight("w_down", i, df), f"dx_down{i}", out_dtype=BF16)
        dgu = _swiglu_bwd(sv["gate"], sv["up"], dact, name=f"swiglu_bwd{i}")
        send("w_gate_up", i, _tn(sv["n2"], dgu, f"d_gate_up{i}"))
        dn2 = grad_in(dgu, weight("w_gate_up", i, dgu), f"dx_gate_up{i}")

        def post_mix_bwd(dy, dn, h1, mm, g_post, g_pre):
            _, vjp_pre = jax.vjp(_rms, h1, g_pre)
            dh1_n, dg_pre = vjp_pre(dn)
            dh1 = dy + dh1_n
            _, vjp_post = jax.vjp(_rms, mm, g_post)
            dm, dg_post = vjp_post(dh1)
            return dh1, dm, dg_pre, dg_post

        dh1, dm, rep_grads["ffn_pre_g"][i], rep_grads["mix_post_g"][i] = rowcall(
            f"post_mix_bwd{i}", post_mix_bwd, [dh2, dn2, sv["h1"], sv["m"]],
            [gain("mix_post_g", i), gain("ffn_pre_g", i)], [F32, BF16], red_rows=(1, 1), cols=d)

        if i % 2 == 0:
            send_small(i, "w_out_even", 0, sv["cat"], dm, f"d_out_even{i}")
            dcat = grad_in(dm, weight("w_out_even", 0, dm), f"dx_out_even{i}")
            dq, dk, dv = _sb_bwd(sv["proj"], dcat, n_heads, name=f"sb_bwd{i}")
            dhq, dhf, dhi, dhg, grads["hg_lb_logits"], grads["hg_norm_g"] = _hg_bwd(
                sv["proj"], rep["hg_lb_logits"], rep["hg_norm_g"], sv["oraw"], sv["states"], dcat, n_heads,
                name=f"hg_bwd{i}")
            dproj = jnp.concatenate([dq.astype(BF16), dk.astype(BF16), dv.astype(BF16), dhq, dhf, dhi, dhg], axis=1)
            send("w_in_even", 0, _tn(sv["n1"], dproj, f"d_in_even{i}"))
            dn1 = grad_in(dproj, weight("w_in_even", 0, dproj), f"dx_in_even{i}")
        else:
            sm = sv["sm"]
            send_small(i, "w_out_odd", 0, sv["gact"], dm, f"d_out_odd{i}")
            dgo = grad_in(dm, weight("w_out_odd", 0, dm), f"dx_out_odd{i}")
            dgate, dra, dix, dxc1, grads["rg_lambda"] = _rg_scan_bwd(
                dgo, sv["proj"], sv["hs"], sv["xc"], sv["ra"], sv["ix"], sm["rg_lambda"], name=f"rg_scan_bwd{i}")
            (dxb, grads["conv_w"], grads["conv_b"], grads["rg_wa"], grads["rg_ba"], grads["rg_wx"],
             grads["rg_bx"]) = _rg_gates_bwd(dra, dix, dxc1, sv["xc"], sv["proj"], sm["conv_w"], sm["rg_wa"],
                                            sm["rg_wx"], name=f"rg_gates_bwd{i}")
            send("small", 0, {k: grads.pop(k) for k in _SMALL})
            dproj = jnp.concatenate([dgate, dxb], axis=1)
            send("w_in_odd", 0, _tn(sv["n1"], dproj, f"d_in_odd{i}"))
            dn1 = grad_in(dproj, weight("w_in_odd", 0, dproj), f"dx_in_odd{i}")

        def pre_norm_bwd(dy, dn, hh, g):
            _, vjp = jax.vjp(_rms, hh, g)
            dx, dg = vjp(dn)
            return dy + dx, dg

        dh, rep_grads["mix_pre_g"][i] = rowcall(
            f"pre_norm_bwd{i}", pre_norm_bwd, [dh1, dn1, sv["h"]], [gain("mix_pre_g", i)], [F32],
            red_rows=(1,), cols=d)

    for name, layer, a, b, mm_name in deferred:
        send(name, layer, _tn(a, b, mm_name, deps=(dh,)))
    for k, rows in rep_grads.items():
        grads[k] = jnp.concatenate(rows, axis=0)
    return loss_cols, dh, grads


_WEIGHTS = ("mix_pre_g", "mix_post_g", "ffn_pre_g", "ffn_post_g", "ple_norm_g", "w_in_even", "w_out_even",
            "hg_lb_logits", "hg_norm_g", "w_in_odd", "conv_w", "conv_b", "rg_wa", "rg_ba", "rg_wx", "rg_bx",
            "rg_lambda", "w_out_odd", "w_gate_up", "w_down", "w_ple_up", "w_ple_gate")
_REPLICATED = ("mix_pre_g", "mix_post_g", "ffn_pre_g", "ffn_post_g", "ple_norm_g", "hg_lb_logits", "hg_norm_g")
_SMALL = ("conv_w", "conv_b", "rg_wa", "rg_ba", "rg_wx", "rg_bx", "rg_lambda")
_BIG = {"w_in_even": True, "w_out_even": False, "w_in_odd": True, "w_out_odd": False,
        "w_gate_up": True, "w_down": False, "w_ple_up": True, "w_ple_gate": False}
_PACK_ROW = SUBLANES * LANES


def _pack(arrays):
    flat = jnp.concatenate([a.reshape(-1) for a in arrays])
    pad = -flat.shape[0] % _PACK_ROW
    return jnp.pad(flat, (0, pad)).reshape(-1, LANES)


def _pack_blocks(arrays):
    flat = jnp.concatenate([a.reshape(N_DEV, -1) for a in arrays], axis=1)
    pad = -flat.shape[1] % _PACK_ROW
    return jnp.pad(flat, ((0, 0), (0, pad))).reshape(N_DEV, -1, LANES)


def _unpack(packed, shapes, lead=()):
    flat = packed.reshape(lead + (-1,))
    out, pos = [], 0
    for shape in shapes:
        n = math.prod(shape)
        out.append(flat[..., pos:pos + n].reshape(lead + tuple(shape)))
        pos += n
    return out


def _to_full_small(name, blocks):
    if name == "conv_w":
        return jnp.transpose(blocks, (1, 0, 2)).reshape(blocks.shape[1], -1)
    if name in ("conv_b", "rg_lambda"):
        return blocks.reshape(1, -1)
    nb = blocks.shape[1]
    if name in ("rg_wa", "rg_wx"):
        return jnp.transpose(blocks, (1, 0, 2, 3)).reshape(nb, RG_BLOCK, RG_BLOCK)
    return jnp.transpose(blocks, (1, 0, 2)).reshape(nb, 1, RG_BLOCK)


def _to_blocks_small(name, full):
    if name == "conv_w":
        return jnp.transpose(full.reshape(full.shape[0], N_DEV, -1), (1, 0, 2))
    if name in ("conv_b", "rg_lambda"):
        return full.reshape(N_DEV, -1)
    nb = full.shape[0]
    if name in ("rg_wa", "rg_wx"):
        return jnp.transpose(full.reshape(nb, N_DEV, RG_BLOCK // N_DEV, RG_BLOCK), (1, 0, 2, 3))
    return jnp.transpose(full.reshape(nb, N_DEV, RG_BLOCK // N_DEV), (1, 0, 2))


def _step(inp):
    w = {k: inp[k] for k in _WEIGHTS}
    x, p, target = inp["x"][0], inp["p"][:, 0], inp["loss_target"][0]
    assert w["hg_lb_logits"].shape[0] == 2 and w["w_in_even"].shape[0] == 1 and w["w_in_odd"].shape[0] == 1

    n_heads = w["w_in_even"].shape[2] * N_DEV // (7 * HEAD_DIM)
    small_shapes = [w[k].shape[1:] for k in _SMALL]

    def lands_in_place(name):
        return _BIG[name] and w[name].shape[2] % LANES == 0

    depth = p.shape[0]
    order = [("w_in_even", 0), ("w_out_even", 0)] if depth else []
    for i in range(depth):
        if i == 1:
            order += [("w_in_odd", 0), ("small", 0), ("w_out_odd", 0)]
        order += [("w_gate_up", i), ("w_down", i), ("w_ple_up", i), ("w_ple_gate", i)]
    heavy = ("w_gate_up", "w_down", "w_in_odd", "w_out_odd")
    gathers = {}
    first_started = 0.0
    last_token = ()
    for name, l in sorted(order, key=lambda key: key[0] in heavy):
        if name == "small":
            gathers[name, l] = _gather_start(_pack([w[k][0] for k in _SMALL]) + first_started, name="gather_small",
                                             relayed=False, after=last_token)
        else:
            gathers[name, l] = _gather_start((w[name][l] + first_started).astype(BF16), name=f"gather_{name}{l}",
                                             cols=lands_in_place(name), after=last_token)
        last_token = (gathers[name, l]["token"],)
        if len(gathers) == 1:
            first_started = last_token[0][0, 0]
    ready = {}

    def relay(key, after):
        if "cols" in gathers[key] and key[0] != "small":
            gathers[key] = _gather_relay(gathers[key], after, name=f"gather_{key[0]}{key[1]}")

    def weight(name, layer, after):
        key = ("small", 0) if name in _SMALL else (name, layer)
        if key not in ready:
            relay(key, after)
            at = order.index(key)
            for nxt in order[at + 1:at + 2] if at else []:
                relay(nxt, after)
            land, = _spread_wait(gathers[key], after, name=f"gathered_{key[0]}{key[1]}")
            if name in _SMALL:
                ready[key] = {k: _to_full_small(k, b)
                              for k, b in zip(_SMALL, _unpack(land, small_shapes, lead=(N_DEV,)))}
            elif lands_in_place(name):
                ready[key] = land
            elif _BIG[name]:
                ready[key] = jnp.transpose(land, (1, 0, 2)).reshape(land.shape[1], -1)
            else:
                ready[key] = land.reshape(-1, land.shape[2])
        return ready[key][name] if name in _SMALL else ready[key]

    exchanges = []

    def emit(name, layer, g):
        after = tuple(h["token"] for _, _, h in exchanges[-1:])
        if name == "small":
            handle = _exchange_start([_pack_blocks([_to_blocks_small(k, g[k]) for k in _SMALL])],
                                     name="exchange_small", after=after)
        elif lands_in_place(name):
            handle = _exchange_start([g], name=f"exchange_{name}{layer}", cols=True, after=after)
        elif _BIG[name]:
            c = w[name].shape[2]
            handle = _exchange_start([jnp.transpose(g.reshape(-1, N_DEV, c), (1, 0, 2))],
                                     name=f"exchange_{name}{layer}", after=after)
        else:
            handle = _exchange_start([g.reshape((N_DEV,) + w[name].shape[1:])], name=f"exchange_{name}{layer}",
                                     after=after)
        exchanges.append((name, layer, handle))
        return handle["token"]

    rep = {k: w[k] for k in _REPLICATED}
    loss_cols, dx, grads = _local_step(x, p, target, rep, weight, emit, n_heads,
                                       [h["token"] for h in gathers.values()])

    loss_part = jnp.sum(loss_cols).reshape(1)
    rep_gather = _gather_start(_pack([grads[k] for k in _REPLICATED] + [loss_part]), name="gather_rep_grads",
                               relayed=False)

    out = {}
    after = exchanges[-1][2]["token"]
    for name, layer, handle in exchanges:
        land, = _spread_wait(handle, after, name=f"exchanged_{name}{layer}")
        if name == "small":
            res = _adamw(land.reshape(N_DEV, -1, LANES), *[_pack([inp[pre + k][0] for k in _SMALL]) for pre in ("", "m_", "v_")],
                         name="adamw_small")
            for k, *vals in zip(_SMALL, *[_unpack(a, small_shapes) for a in res]):
                out[k] = [v[None] for v in vals]
        else:
            n_l, r, c = w[name].shape
            res = out[name] = _adamw(land.reshape(N_DEV, r, c),
                                     *[inp[pre + name].reshape(n_l * r, c) for pre in ("", "m_", "v_")],
                                     name=f"adamw_{name}{layer}", layer=layer, prev=out.get(name))
        after = res[0]
    for name in _BIG:
        out[name] = [a.reshape(w[name].shape) for a in out[name]]

    rep_shapes = [w[k].shape for k in _REPLICATED] + [(1,)]
    rep_parts, = _spread_wait(rep_gather, after, name="gathered_rep_grads")
    res = _adamw(rep_parts, *[_pack([inp[pre + k] for k in _REPLICATED] + [jnp.zeros((1,), F32)])
                              for pre in ("", "m_", "v_")], name="adamw_rep")
    for k, *vals in zip(_REPLICATED + ("loss",), *[_unpack(a, rep_shapes) for a in res]):
        out[k] = vals
    loss = out["loss"][0][0]

    return (loss, dx[None]) + tuple(out[k][j] for j in range(4) for k in _WEIGHTS)


def kernel(x, p, mix_pre_g, mix_post_g, ffn_pre_g, ffn_post_g, ple_norm_g, w_in_even, w_out_even, hg_lb_logits, hg_norm_g, w_in_odd, conv_w, conv_b, rg_wa, rg_ba, rg_wx, rg_bx, rg_lambda, w_out_odd, w_gate_up, w_down, w_ple_up, w_ple_gate, loss_target, m_mix_pre_g, m_mix_post_g, m_ffn_pre_g, m_ffn_post_g, m_ple_norm_g, m_w_in_even, m_w_out_even, m_hg_lb_logits, m_hg_norm_g, m_w_in_odd, m_conv_w, m_conv_b, m_rg_wa, m_rg_ba, m_rg_wx, m_rg_bx, m_rg_lambda, m_w_out_odd, m_w_gate_up, m_w_down, m_w_ple_up, m_w_ple_gate, v_mix_pre_g, v_mix_post_g, v_ffn_pre_g, v_ffn_post_g, v_ple_norm_g, v_w_in_even, v_w_out_even, v_hg_lb_logits, v_hg_norm_g, v_w_in_odd, v_conv_w, v_conv_b, v_rg_wa, v_rg_ba, v_rg_wx, v_rg_bx, v_rg_lambda, v_w_out_odd, v_w_gate_up, v_w_down, v_w_ple_up, v_w_ple_gate):
    return _step(dict(locals()))
```
